```python
import jax, jax.numpy as jnp
from jax import lax
import numpy as np

D_MODEL = 1024
BATCH = 8
SEQ = 2048
DEPTH = 2
DEC_BATCH = 128
DEC_SEQ = 1
PAST_LEN = 16384
PAGE_SIZE = 128

N_BRANCH = 4
N_HEADS = 4
HEAD_DIM = 64
W_MIX = N_HEADS * HEAD_DIM
RWKV_LORA_W = 64
RWKV_LORA_A = 64
CONV_W = 4
LRU_C = 8.0
RET_CHUNK = 128
GDN_CHUNK = 64
ROPE_BASE = 10000.0
EPS = 1e-6
RWKV_GN_EPS = 64e-5

A_SHIFT_W = 3 * W_MIX + RWKV_LORA_W + RWKV_LORA_A
A_W = A_SHIFT_W + W_MIX
B_W = 4 * W_MIX
C_W = 2 * W_MIX
D_QKV_W = 3 * W_MIX
D_W = D_QKV_W + 2 * N_HEADS + W_MIX
G_W = N_BRANCH * D_MODEL
OFF_A = 0
OFF_B = OFF_A + A_W
OFF_C = OFF_B + B_W
OFF_D = OFF_C + C_W
OFF_G = OFF_D + D_W
IN_W = OFF_G + G_W
N_STATES = 7

kernel_name = "hybrid_rwkv7_retnet_rglru_gdn_step"


def _f32(t):
    return t.astype(jnp.float32)


def rms_norm(x, gain=None, eps=EPS):
    xf = _f32(x)
    y = xf * lax.rsqrt(jnp.mean(xf * xf, axis=-1, keepdims=True) + eps)
    return y if gain is None else y * gain


def l2_normalize(x, eps=EPS):
    return x * lax.rsqrt(jnp.sum(x * x, axis=-1, keepdims=True) + eps)


def heads(t):
    return t.reshape(t.shape[:-1] + (N_HEADS, HEAD_DIM))


def chunk_len(T, C):
    return C if T % C == 0 else T


def to_chunks(t, C):
    B, T, H = t.shape[:3]
    t = t.reshape((B, T // C, C, H) + t.shape[3:])
    return jnp.moveaxis(t, (1, 3), (0, 2))


def from_chunks(t):
    t = jnp.moveaxis(t, (0, 2), (1, 3))
    B, n, C, H, D = t.shape
    return t.reshape(B, n * C, H, D)


def causal_conv(u, buf, w):
    T = u.shape[1]
    full = jnp.concatenate([_f32(buf), u], axis=1)
    out = sum(full[:, j:j + T] * w[j] for j in range(CONV_W))
    return out, full[:, T:]


def rotary(x, pos):
    half = HEAD_DIM // 2
    inv = ROPE_BASE ** (-jnp.arange(half, dtype=jnp.float32) / half)
    ang = pos.astype(jnp.float32)[:, None] * inv[None, :]
    cos = jnp.cos(ang)[None, :, None, :]
    sin = jnp.sin(ang)[None, :, None, :]
    x1, x2 = x[..., :half], x[..., half:]
    return jnp.concatenate([x1 * cos - x2 * sin, x1 * sin + x2 * cos], axis=-1)


def rwkv7_scan(r, w, k, v, a, b, S0):
    def step(S, inp):
        r_t, w_t, k_t, v_t, a_t, b_t = inp
        sa = jnp.einsum('bhvk,bhk->bhv', S, a_t)
        S = S * w_t[:, :, None, :] + sa[..., :, None] * b_t[..., None, :] + v_t[..., :, None] * k_t[..., None, :]
        return S, jnp.einsum('bhvk,bhk->bhv', S, r_t)
    xs = tuple(jnp.swapaxes(t, 0, 1) for t in (r, w, k, v, a, b))
    S, o = lax.scan(step, S0, xs)
    return jnp.swapaxes(o, 0, 1), S


def retention_chunked(q, k, v, log_gamma, S0, C):
    qc, kc, vc = (to_chunks(t, C) for t in (q, k, v))
    idx = jnp.arange(C, dtype=jnp.float32)
    rel = idx[:, None] - idx[None, :]
    causal = rel >= 0
    decay = jnp.where(causal, jnp.exp(log_gamma[:, None, None] * jnp.where(causal, rel, 0.0)), 0.0)
    q_dec = jnp.exp(log_gamma[:, None] * (idx + 1.0))[..., None]
    k_dec = jnp.exp(log_gamma[:, None] * (C - 1.0 - idx))[..., None]
    g_c = jnp.exp(log_gamma * C)[:, None, None]

    def step(S, inp):
        q_i, k_i, v_i = inp
        s_in = jnp.einsum('bhid,bhjd->bhij', q_i, k_i) * decay
        o = jnp.einsum('bhij,bhjv->bhiv', s_in, v_i) + jnp.einsum('bhik,bhkv->bhiv', q_i, S) * q_dec
        S = S * g_c + jnp.einsum('bhjk,bhjv->bhkv', k_i * k_dec, v_i)
        return S, o
    S, o = lax.scan(step, S0, (qc, kc, vc))
    return from_chunks(o), S


def gated_delta_chunked(q, k, v, g, beta, S0, C):
    qc, kc, vc = (to_chunks(t, C) for t in (q, k, v))
    gc = jnp.cumsum(to_chunks(g, C), axis=-1)
    bc = to_chunks(beta, C)
    idx = jnp.arange(C)
    lower = idx[:, None] >= idx[None, :]
    strict = idx[:, None] > idx[None, :]
    diff = gc[..., :, None] - gc[..., None, :]
    decay = jnp.where(lower, jnp.exp(jnp.where(lower, diff, 0.0)), 0.0)
    kb = kc * bc[..., None]
    a_mat = jnp.where(strict, jnp.einsum('nbhid,nbhjd->nbhij', kb, kc) * decay, 0.0)
    m = a_mat + jnp.eye(C, dtype=a_mat.dtype)
    u = lax.linalg.triangular_solve(m, vc * bc[..., None], left_side=True, lower=True)
    w = lax.linalg.triangular_solve(m, kb * jnp.exp(gc)[..., None], left_side=True, lower=True)
    qk = jnp.einsum('nbhid,nbhjd->nbhij', qc, kc) * decay
    g_last = gc[..., -1]
    k_out = kc * jnp.exp(g_last[..., None] - gc)[..., None]
    q_in = qc * jnp.exp(gc)[..., None]

    def step(S, inp):
        u_i, w_i, qk_i, q_i, k_i, gl_i = inp
        v_new = u_i - jnp.einsum('bhik,bhkv->bhiv', w_i, S)
        o = jnp.einsum('bhik,bhkv->bhiv', q_i, S) + jnp.einsum('bhij,bhjv->bhiv', qk_i, v_new)
        S = S * jnp.exp(gl_i)[..., None, None] + jnp.einsum('bhik,bhiv->bhkv', k_i, v_new)
        return S, o
    S, o = lax.scan(step, S0, (u, w, qk, q_in, k_out, g_last))
    return from_chunks(o), S


def rwkv7_branch(p, wkv0, shift0, mu, w0, w2, a0, a2, k_k, k_a, r_k, ln_g, ln_b):
    pa, z = p[..., :A_SHIFT_W], p[..., A_SHIFT_W:]
    prev = jnp.concatenate([_f32(shift0)[:, None], pa[:, :-1]], axis=1)
    pm = pa + (prev - pa) * mu
    r, k, v = (pm[..., i * W_MIX:(i + 1) * W_MIX] for i in range(3))
    wd = pm[..., 3 * W_MIX:3 * W_MIX + RWKV_LORA_W]
    ad = pm[..., 3 * W_MIX + RWKV_LORA_W:]
    w_log = -jax.nn.softplus(-(w0 + jnp.tanh(wd) @ w2)) - 0.5
    decay = jnp.exp(-jnp.exp(w_log))
    a = jax.nn.sigmoid(a0 + ad @ a2)
    kk = l2_normalize(heads(k * k_k))
    k = k * (1.0 + (a - 1.0) * k_a)
    r_h, k_h, v_h, a_h = heads(r), heads(k), heads(v), heads(a)
    o, wkv1 = rwkv7_scan(r_h, heads(decay), k_h, v_h, -kk, kk * a_h, _f32(wkv0))
    mean = jnp.mean(o, axis=-1, keepdims=True)
    var = jnp.mean(jnp.square(o - mean), axis=-1, keepdims=True)
    o = ((o - mean) * lax.rsqrt(var + RWKV_GN_EPS)).reshape(z.shape) * ln_g + ln_b
    bonus = (jnp.sum(r_h * k_h * r_k, axis=-1, keepdims=True) * v_h).reshape(z.shape)
    out = (o + bonus) * jax.nn.silu(z)
    return out, wkv1.astype(wkv0.dtype), pa[:, -1].astype(shift0.dtype)


def retention_branch(p, S0, pos):
    T = p.shape[1]
    q, k, v, z = (p[..., i * W_MIX:(i + 1) * W_MIX] for i in range(4))
    q = rotary(heads(q), pos)
    k = rotary(heads(k), pos) * HEAD_DIM ** -0.5
    log_gamma = jnp.log(1.0 - 2.0 ** (-5.0 - jnp.arange(N_HEADS, dtype=jnp.float32)))
    o, S1 = retention_chunked(q, k, heads(v), log_gamma, _f32(S0), chunk_len(T, RET_CHUNK))
    out = rms_norm(o).reshape(z.shape) * jax.nn.silu(z)
    return out, S1.astype(S0.dtype)


def rglru_branch(p, h0, conv0, conv_w, conv_b, gate_w, gate_b, lam):
    xr, z = p[..., :W_MIX], p[..., W_MIX:]
    xc, conv1 = causal_conv(xr, conv0, conv_w)
    xc = xc + conv_b
    gates = jnp.einsum('btnd,gnde->gbtne', heads(xc), gate_w)
    gates = gates.reshape((2,) + xc.shape) + gate_b[:, None, None, :]
    r_gate, i_gate = jax.nn.sigmoid(gates[0]), jax.nn.sigmoid(gates[1])
    log_a = -LRU_C * r_gate * jax.nn.softplus(-lam)
    a = jnp.exp(log_a)
    b = jnp.sqrt(-jnp.expm1(2.0 * log_a)) * (i_gate * xc)
    b = b.at[:, 0].add(a[:, 0] * _f32(h0))

    def combine(e1, e2):
        return (e1[0] * e2[0], e2[0] * e1[1] + e2[1])
    _, h = lax.associative_scan(combine, (a, b), axis=1)
    out = h * jax.nn.silu(z)
    return out, h[:, -1].astype(h0.dtype), conv1.astype(conv0.dtype)


def gdn_branch(p, S0, conv0, conv_w, A_log, dt_bias, norm_g):
    T = p.shape[1]
    qkv_raw = p[..., :D_QKV_W]
    b_raw = p[..., D_QKV_W:D_QKV_W + N_HEADS]
    a_raw = p[..., D_QKV_W + N_HEADS:D_QKV_W + 2 * N_HEADS]
    z = p[..., D_QKV_W + 2 * N_HEADS:]
    qkv, conv1 = causal_conv(qkv_raw, conv0, conv_w)
    qkv = jax.nn.silu(qkv)
    q = l2_normalize(heads(qkv[..., :W_MIX])) * HEAD_DIM ** -0.5
    k = l2_normalize(heads(qkv[..., W_MIX:2 * W_MIX]))
    v = heads(qkv[..., 2 * W_MIX:])
    beta = jax.nn.sigmoid(b_raw)
    g = -jnp.exp(A_log) * jax.nn.softplus(a_raw + dt_bias)
    o, S1 = gated_delta_chunked(q, k, v, g, beta, _f32(S0), chunk_len(T, GDN_CHUNK))
    out = rms_norm(o, norm_g).reshape(z.shape) * jax.nn.silu(z)
    return out, S1.astype(S0.dtype), conv1.astype(conv0.dtype)


def mixer_layer(x, c, pos, st, lp):
    wkv0, shift0, ret0, lru_h0, lru_conv0, gdn0, gdn_conv0 = st
    B, T, _ = x.shape
    mod = jax.nn.silu(_f32(c)) @ lp['ada_w'] + lp['ada_b']
    shift, scale, gate = jnp.split(mod, 3, axis=-1)
    h = rms_norm(x, lp['norm_g']) * (1.0 + scale[:, None]) + shift[:, None]
    proj = jnp.einsum('btd,de->bte', h, lp['w_in'])
    oA, wkv1, shift1 = rwkv7_branch(proj[..., OFF_A:OFF_B], wkv0, shift0, lp['rwkv_mu'], lp['rwkv_w0'],
                                    lp['rwkv_w2'], lp['rwkv_a0'], lp['rwkv_a2'], lp['rwkv_k_k'],
                                    lp['rwkv_k_a'], lp['rwkv_r_k'], lp['rwkv_ln_g'], lp['rwkv_ln_b'])
    oB, ret1 = retention_branch(proj[..., OFF_B:OFF_C], ret0, pos)
    oC, lru_h1, lru_conv1 = rglru_branch(proj[..., OFF_C:OFF_D], lru_h0, lru_conv0, lp['lru_conv_w'],
                                         lp['lru_conv_b'], lp['lru_gate_w'], lp['lru_gate_b'], lp['lru_lambda'])
    oD, gdn1, gdn_conv1 = gdn_branch(proj[..., OFF_D:OFF_G], gdn0, gdn_conv0, lp['gdn_conv_w'],
                                     lp['gdn_A_log'], lp['gdn_dt_bias'], lp['gdn_norm_g'])
    branches = jnp.stack([oA, oB, oC, oD], axis=2)
    up = jnp.einsum('btnw,nwd->btnd', branches, lp['w_up'])
    gl = proj[..., OFF_G:].reshape(B, T, N_BRANCH, D_MODEL)
    merged = jnp.sum(jax.nn.sigmoid(gl) * up, axis=2)
    out = merged @ lp['w_out']
    x_new = (_f32(x) + gate[:, None] * out).astype(x.dtype)
    return x_new, (wkv1, shift1, ret1, lru_h1, lru_conv1, gdn1, gdn_conv1)


def run_trunk(x, c, pos, layer_states, layer_params, final_g):
    new = []
    for l in range(DEPTH):
        x, st = mixer_layer(x, c, pos, layer_states[l], layer_params[l])
        new.append(st)
    stacked = [jnp.stack([s[i] for s in new], axis=0) for i in range(N_STATES)]
    y = rms_norm(x, _f32(final_g)).astype(x.dtype)
    return y, stacked


def setup_inputs(seed: int = 0) -> dict:
    key = jax.random.key(seed)
    ks = iter(jax.random.split(key, 48))
    f32 = jnp.float32

    def nrm(shape, s):
        return s * jax.random.normal(next(ks), shape, f32)

    def unif(shape, lo, hi):
        return jax.random.uniform(next(ks), shape, f32, lo, hi)

    L, D, W = DEPTH, D_MODEL, W_MIX
    s_lru = unif((L, W), 0.9, 0.999) ** (1.0 / LRU_C)
    dt = jnp.exp(unif((L, N_HEADS), float(np.log(1e-3)), float(np.log(1e-1))))
    return {
        'x_prompt': nrm((BATCH, SEQ, D), 1.0),
        'x_sample': nrm((DEC_BATCH, DEC_SEQ, D), 1.0),
        'c_prompt': nrm((BATCH, D), 1.0),
        'c_sample': nrm((DEC_BATCH, D), 1.0),
        'state_rwkv_wkv': nrm((L, DEC_BATCH, N_HEADS, HEAD_DIM, HEAD_DIM), 0.1),
        'state_rwkv_shift': nrm((L, DEC_BATCH, A_SHIFT_W), 1.0),
        'state_ret': nrm((L, DEC_BATCH, N_HEADS, HEAD_DIM, HEAD_DIM), 0.5),
        'state_lru_h': nrm((L, DEC_BATCH, W), 0.5),
        'state_lru_conv': nrm((L, DEC_BATCH, CONV_W - 1, W), 1.0),
        'state_gdn': nrm((L, DEC_BATCH, N_HEADS, HEAD_DIM, HEAD_DIM), 0.1),
        'state_gdn_conv': nrm((L, DEC_BATCH, CONV_W - 1, D_QKV_W), 1.0),
        'ada_w': nrm((L, D, 3 * D), 0.5 * D ** -0.5),
        'ada_b': nrm((L, 3 * D), 0.02),
        'norm_g': 1.0 + nrm((L, D), 0.02),
        'w_in': nrm((L, D, IN_W), D ** -0.5),
        'rwkv_mu': unif((L, A_SHIFT_W), 0.0, 1.0),
        'rwkv_w0': unif((L, W), -6.0, 0.0),
        'rwkv_w2': nrm((L, RWKV_LORA_W, W), 0.5 * RWKV_LORA_W ** -0.5),
        'rwkv_a0': nrm((L, W), 0.1),
        'rwkv_a2': nrm((L, RWKV_LORA_A, W), 0.5 * RWKV_LORA_A ** -0.5),
        'rwkv_k_k': 0.85 + nrm((L, W), 0.02),
        'rwkv_k_a': 1.0 + nrm((L, W), 0.02),
        'rwkv_r_k': nrm((L, N_HEADS, HEAD_DIM), 0.1),
        'rwkv_ln_g': 1.0 + nrm((L, W), 0.02),
        'rwkv_ln_b': nrm((L, W), 0.02),
        'lru_conv_w': nrm((L, CONV_W, W), 0.5),
        'lru_conv_b': nrm((L, W), 0.02),
        'lru_gate_w': nrm((L, 2, N_HEADS, HEAD_DIM, HEAD_DIM), HEAD_DIM ** -0.5),
        'lru_gate_b': nrm((L, 2, W), 0.02),
        'lru_lambda': jnp.log(s_lru) - jnp.log1p(-s_lru),
        'gdn_conv_w': nrm((L, CONV_W, D_QKV_W), 0.5),
        'gdn_A_log': jnp.log(unif((L, N_HEADS), 1.0, 16.0)),
        'gdn_dt_bias': dt + jnp.log(-jnp.expm1(-dt)),
        'gdn_norm_g': 1.0 + nrm((L, HEAD_DIM), 0.02),
        'w_up': nrm((L, N_BRANCH, W, D), W ** -0.5),
        'w_out': nrm((L, D, D), D ** -0.5),
        'final_g': 1.0 + nrm((D,), 0.02),
    }


def reference(x_prompt, x_sample, c_prompt, c_sample, state_rwkv_wkv, state_rwkv_shift, state_ret,
              state_lru_h, state_lru_conv, state_gdn, state_gdn_conv, ada_w, ada_b, norm_g, w_in,
              rwkv_mu, rwkv_w0, rwkv_w2, rwkv_a0, rwkv_a2, rwkv_k_k, rwkv_k_a, rwkv_r_k, rwkv_ln_g,
              rwkv_ln_b, lru_conv_w, lru_conv_b, lru_gate_w, lru_gate_b, lru_lambda, gdn_conv_w,
              gdn_A_log, gdn_dt_bias, gdn_norm_g, w_up, w_out, final_g):
    layer_params = [dict(
        ada_w=_f32(ada_w[l]), ada_b=_f32(ada_b[l]), norm_g=_f32(norm_g[l]), w_in=_f32(w_in[l]),
        rwkv_mu=_f32(rwkv_mu[l]), rwkv_w0=_f32(rwkv_w0[l]), rwkv_w2=_f32(rwkv_w2[l]),
        rwkv_a0=_f32(rwkv_a0[l]), rwkv_a2=_f32(rwkv_a2[l]), rwkv_k_k=_f32(rwkv_k_k[l]),
        rwkv_k_a=_f32(rwkv_k_a[l]), rwkv_r_k=_f32(rwkv_r_k[l]), rwkv_ln_g=_f32(rwkv_ln_g[l]),
        rwkv_ln_b=_f32(rwkv_ln_b[l]), lru_conv_w=_f32(lru_conv_w[l]), lru_conv_b=_f32(lru_conv_b[l]),
        lru_gate_w=_f32(lru_gate_w[l]), lru_gate_b=_f32(lru_gate_b[l]), lru_lambda=_f32(lru_lambda[l]),
        gdn_conv_w=_f32(gdn_conv_w[l]), gdn_A_log=_f32(gdn_A_log[l]), gdn_dt_bias=_f32(gdn_dt_bias[l]),
        gdn_norm_g=_f32(gdn_norm_g[l]), w_up=_f32(w_up[l]), w_out=_f32(w_out[l]))
        for l in range(DEPTH)]

    dt_p = x_prompt.dtype
    zero_states = [(
        jnp.zeros((BATCH, N_HEADS, HEAD_DIM, HEAD_DIM), dt_p),
        jnp.zeros((BATCH, A_SHIFT_W), dt_p),
        jnp.zeros((BATCH, N_HEADS, HEAD_DIM, HEAD_DIM), dt_p),
        jnp.zeros((BATCH, W_MIX), dt_p),
        jnp.zeros((BATCH, CONV_W - 1, W_MIX), dt_p),
        jnp.zeros((BATCH, N_HEADS, HEAD_DIM, HEAD_DIM), dt_p),
        jnp.zeros((BATCH, CONV_W - 1, D_QKV_W), dt_p)) for _ in range(DEPTH)]
    pos_p = jnp.arange(SEQ, dtype=jnp.int32)
    y_prompt, new_p = run_trunk(x_prompt, c_prompt, pos_p, zero_states, layer_params, final_g)
    p_wkv, p_shift, p_ret, p_lru_h, p_lru_conv, p_gdn, p_gdn_conv = new_p

    carried = [(state_rwkv_wkv[l], state_rwkv_shift[l], state_ret[l], state_lru_h[l], state_lru_conv[l],
                state_gdn[l], state_gdn_conv[l]) for l in range(DEPTH)]
    pos_s = PAST_LEN + jnp.arange(DEC_SEQ, dtype=jnp.int32)
    y_sample, new_s = run_trunk(x_sample, c_sample, pos_s, carried, layer_params, final_g)
    s_wkv, s_shift, s_ret, s_lru_h, s_lru_conv, s_gdn, s_gdn_conv = new_s

    return (y_prompt, y_sample, p_wkv, p_shift, p_ret, p_lru_h, p_lru_conv, p_gdn, p_gdn_conv,
            s_wkv, s_shift, s_ret, s_lru_h, s_lru_conv, s_gdn, s_gdn_conv)
```

```python
import functools
import math

import jax
import jax.numpy as jnp
from jax import lax
from jax.experimental import pallas as pl
from jax.experimental.pallas import tpu as pltpu

F32 = jnp.float32
BF16 = jnp.bfloat16
HI = lax.Precision.HIGHEST

N_HEADS = 4
HEAD_DIM = 64
W_MIX = N_HEADS * HEAD_DIM
LORA = 64
CONV_W = 4
N_BRANCH = 4
LRU_C = 8.0
ROPE_BASE = 10000.0
EPS = 1e-6
RWKV_GN_EPS = 64e-5
PAST_LEN = 16384
A_SHIFT_W = 3 * W_MIX + 2 * LORA
A_W = A_SHIFT_W + W_MIX
B_W = 4 * W_MIX
C_W = 2 * W_MIX
D_QKV_W = 3 * W_MIX
D_W = D_QKV_W + 2 * N_HEADS + W_MIX
D_PACK_W = D_QKV_W + 3 * W_MIX

SUBLANES = 8
LANES = 128
VMEM_LIMIT = 56 * 1024 * 1024

CHUNK = 64
RET_CHUNK = 128
INV_BLOCK = 16
ROW_TILE = 256
N_VEC = 17


def _mm(a, b, prec=HI):
    return lax.dot_general(a, b, (((1,), (0,)), ((), ())), precision=prec, preferred_element_type=F32)


def _mm_nt(a, b, prec=HI):
    return lax.dot_general(a, b, (((1,), (1,)), ((), ())), precision=prec, preferred_element_type=F32)


def _mm_tn(a, b, prec=HI):
    return lax.dot_general(a, b, (((0,), (0,)), ((), ())), precision=prec, preferred_element_type=F32)


def _mm_bf16(a, b):
    return lax.dot_general(a.astype(BF16), b.astype(BF16), (((1,), (0,)), ((), ())), preferred_element_type=F32)


def _iota(shape, dim):
    return lax.broadcasted_iota(jnp.int32, shape, dim)


def _silu(x):
    return x * jax.nn.sigmoid(x)


def _softplus(x):
    return jnp.maximum(x, 0.0) + jnp.log1p(jnp.exp(-jnp.abs(x)))


def _head_ones():
    return (_iota((W_MIX, W_MIX), 0) // HEAD_DIM == _iota((W_MIX, W_MIX), 1) // HEAD_DIM).astype(F32)


def _head_sum(x, ones):
    return _mm(x, ones)


def _rms(x):
    return x * lax.rsqrt(jnp.mean(x * x, axis=-1, keepdims=True) + EPS)


def _inv_unit_lower(a):
    n = a.shape[0]
    ri, ci = _iota((n, n), 0), _iota((n, n), 1)
    eye = (ri == ci).astype(F32)
    diag_blk = (ri // INV_BLOCK) == (ci // INV_BLOCK)
    d = jnp.where(diag_blk, a, 0.0)
    nb = a - d
    td = eye - d
    p = d
    for _ in range(int(math.log2(INV_BLOCK)) - 1):
        p = _mm(p, p)
        td = _mm(td, eye + p)
    x = _mm(td, nb)
    t = eye - x
    p = x
    for _ in range(int(math.log2(n // INV_BLOCK)) - 1):
        p = _mm(p, p)
        t = _mm(t, eye + p)
    return _mm(t, td)


def _ada_kernel(c_ref, w_ref, b_ref, o_ref):
    o_ref[...] = _mm(_silu(c_ref[...]), w_ref[...]) + b_ref[...]


def _ada_call(c_all, ada_w, ada_b):
    n_layers, d, d3 = ada_w.shape
    rows = c_all.shape[0]
    return pl.pallas_call(
        _ada_kernel,
        grid=(n_layers, d3 // d),
        in_specs=[
            pl.BlockSpec((rows, d), lambda l, j: (0, 0)),
            pl.BlockSpec((None, d, d), lambda l, j: (l, 0, j)),
            pl.BlockSpec((None, 1, d), lambda l, j: (l, 0, j)),
        ],
        out_specs=pl.BlockSpec((None, rows, d), lambda l, j: (l, 0, j)),
        out_shape=jax.ShapeDtypeStruct((n_layers, rows, d3), F32),
        compiler_params=pltpu.CompilerParams(dimension_semantics=("arbitrary", "arbitrary"),
                                             vmem_limit_bytes=VMEM_LIMIT),
        name="ada_mod",
    )(c_all, ada_w, ada_b.reshape(n_layers, 1, d3))


def _modulated_norm(x, g, scale, shift):
    return _rms(x) * g * (1.0 + scale) + shift


def _inproj_kernel(x_ref, sc_ref, sh_ref, g_ref, w_ref, oa_ref, ob_ref, oc_ref, od_ref):
    h = _modulated_norm(x_ref[...], g_ref[...], sc_ref[...], sh_ref[...]).astype(BF16)
    lo = 0
    for o_ref in (oa_ref, ob_ref, oc_ref, od_ref):
        wd = o_ref.shape[-1]
        o_ref[...] = jnp.dot(h, w_ref[:, lo:lo + wd], preferred_element_type=F32)
        lo += wd


def _mod_specs(mods, tm, seq_len):
    d = mods[0].shape[-1]
    if seq_len == 1:
        return [m for m in mods], [pl.BlockSpec((tm, d), lambda i: (i, 0)) for _ in mods]
    per_seq = seq_len // tm
    return ([m.reshape(m.shape[0], 1, d) for m in mods],
            [pl.BlockSpec((None, 1, d), lambda i: (i // per_seq, 0, 0)) for _ in mods])


def _inproj_call(x2, scale, shift, g, w_pack, seq_len):
    m, d = x2.shape
    tm = min(ROW_TILE, m) if seq_len > 1 else m
    tm = min(tm, seq_len) if seq_len > 1 else tm
    widths = (A_W, B_W, C_W, D_PACK_W)
    mods, mod_specs = _mod_specs((scale, shift), tm, seq_len)
    return pl.pallas_call(
        _inproj_kernel,
        grid=(m // tm,),
        in_specs=[pl.BlockSpec((tm, d), lambda i: (i, 0))] + mod_specs + [
            pl.BlockSpec((1, d), lambda i: (0, 0)),
            pl.BlockSpec(w_pack.shape, lambda i: (0, 0)),
        ],
        out_specs=[pl.BlockSpec((tm, wd), lambda i: (i, 0)) for wd in widths],
        out_shape=[jax.ShapeDtypeStruct((m, wd), F32) for wd in widths],
        compiler_params=pltpu.CompilerParams(dimension_semantics=("parallel",), vmem_limit_bytes=VMEM_LIMIT),
        name="in_proj",
    )(x2, *mods, g.reshape(1, d), w_pack)


def _outproj_kernel(x_ref, sc_ref, sh_ref, gt_ref, g_ref, ba_ref, bb_ref, bc_ref, bd_ref,
                    wg_ref, wup_ref, wout_ref, fg_ref, o_ref, *, final):
    x = x_ref[...]
    d = x.shape[-1]
    h = _modulated_norm(x, g_ref[...], sc_ref[...], sh_ref[...]).astype(BF16)
    merged = jnp.zeros(x.shape, F32)
    for n, br_ref in enumerate((ba_ref, bb_ref, bc_ref, bd_ref)):
        gl = jnp.dot(h, wg_ref[:, n * d:(n + 1) * d], preferred_element_type=F32)
        up = jnp.dot(br_ref[...].astype(BF16), wup_ref[n], preferred_element_type=F32)
        merged = merged + jax.nn.sigmoid(gl) * up
    out = jnp.dot(merged.astype(BF16), wout_ref[...], preferred_element_type=F32)
    xn = x + gt_ref[...] * out
    if final:
        xn = _rms(xn) * fg_ref[...]
    o_ref[...] = xn


def _outproj_call(x2, scale, shift, gate, g, branches, wg, wup, wout, final_g, seq_len, final):
    m, d = x2.shape
    tm = min(ROW_TILE, m, seq_len) if seq_len > 1 else m
    mods, mod_specs = _mod_specs((scale, shift, gate), tm, seq_len)
    full = lambda a: pl.BlockSpec(a.shape, lambda i: (0,) * a.ndim)
    return pl.pallas_call(
        functools.partial(_outproj_kernel, final=final),
        grid=(m // tm,),
        in_specs=[pl.BlockSpec((tm, d), lambda i: (i, 0))] + mod_specs + [pl.BlockSpec((1, d), lambda i: (0, 0))]
        + [pl.BlockSpec((tm, W_MIX), lambda i: (i, 0)) for _ in branches]
        + [full(wg), full(wup), full(wout), pl.BlockSpec((1, d), lambda i: (0, 0))],
        out_specs=pl.BlockSpec((tm, d), lambda i: (i, 0)),
        out_shape=jax.ShapeDtypeStruct((m, d), F32),
        compiler_params=pltpu.CompilerParams(dimension_semantics=("parallel",), vmem_limit_bytes=VMEM_LIMIT),
        name="out_proj",
    )(x2, *mods, g.reshape(1, d), *branches, wg, wup, wout, final_g.reshape(1, d))


def _rwkv_token_math(pm, w0, w2, a0, a2, k_k, k_a, ones):
    r = pm[:, 0:W_MIX]
    k = pm[:, W_MIX:2 * W_MIX]
    v = pm[:, 2 * W_MIX:3 * W_MIX]
    wd = pm[:, 3 * W_MIX:3 * W_MIX + LORA]
    ad = pm[:, 3 * W_MIX + LORA:]
    w_log = -_softplus(-(w0 + _mm(jnp.tanh(wd), w2))) - 0.5
    log_decay = -jnp.exp(w_log)
    a = jax.nn.sigmoid(a0 + _mm(ad, a2))
    kx = k * k_k
    kk = kx * lax.rsqrt(_head_sum(kx * kx, ones) + EPS)
    k = k * (1.0 + (a - 1.0) * k_a)
    return r, k, v, log_decay, -kk, kk * a


def _rwkv_finish(o, r, k, v, z, r_k, ln_g, ln_b, ones):
    mean = _head_sum(o, ones) * (1.0 / HEAD_DIM)
    dlt = o - mean
    var = _head_sum(dlt * dlt, ones) * (1.0 / HEAD_DIM)
    on = dlt * lax.rsqrt(var + RWKV_GN_EPS) * ln_g + ln_b
    bonus = _head_sum(r * k * r_k, ones) * v
    return (on + bonus) * _silu(z)


def _swap_halves(x):
    half = HEAD_DIM // 2
    n = x.shape[-1]
    first = (_iota(x.shape, 1) & half) == 0
    return jnp.where(first, pltpu.roll(x, n - half, axis=1), pltpu.roll(x, half, axis=1))


def _rotary(x, cos, sin):
    return x * cos + _swap_halves(x) * sin


def _lru_token_math(xc, gate_w, gate_b, lam):
    gates = _mm(xc, gate_w) + gate_b
    r_gate = jax.nn.sigmoid(gates[:, :W_MIX])
    i_gate = jax.nn.sigmoid(gates[:, W_MIX:])
    log_a = -LRU_C * r_gate * _softplus(-lam)
    a = jnp.exp(log_a)
    b = jnp.sqrt(1.0 - jnp.exp(2.0 * log_a)) * (i_gate * xc)
    return a, b


def _gdn_token_math(qkv, b_raw, a_raw, a_log, dt_bias, ones):
    qkv = _silu(qkv)
    q = qkv[:, 0:W_MIX]
    k = qkv[:, W_MIX:2 * W_MIX]
    v = qkv[:, 2 * W_MIX:]
    q = q * lax.rsqrt(_head_sum(q * q, ones) + EPS) * (HEAD_DIM ** -0.5)
    k = k * lax.rsqrt(_head_sum(k * k, ones) + EPS)
    beta = jax.nn.sigmoid(b_raw)
    g = -jnp.exp(a_log) * _softplus(a_raw + dt_bias)
    return q, k, v, beta, g


def _head_rms_finish(o, z, ones, gain=None):
    y = o * lax.rsqrt(_head_sum(o * o, ones) * (1.0 / HEAD_DIM) + EPS)
    if gain is not None:
        y = y * gain
    return y * _silu(z)


def _conv_tile(u, ext_ref, w_ref, first):
    n = u.shape[0]

    @pl.when(first)
    def _():
        ext_ref[0:SUBLANES, :] = jnp.zeros((SUBLANES, u.shape[1]), F32)

    ext_ref[SUBLANES:SUBLANES + n, :] = u
    out = None
    for j in range(CONV_W):
        back = CONV_W - 1 - j
        term = ext_ref[SUBLANES - back:SUBLANES - back + n, :] * w_ref[j:j + 1, :]
        out = term if out is None else out + term
    ext_ref[0:SUBLANES, :] = u[n - SUBLANES:n, :]
    return out


def _rwkv_kernel(p_ref, mu_ref, w0_ref, w2_ref, a0_ref, a2_ref, kk_ref, ka_ref, rk_ref, lng_ref, lnb_ref,
                 o_ref, s_out_ref, shift_out_ref, s_scr, prev_scr):
    j = pl.program_id(1)
    last = pl.num_programs(1) - 1
    ct = p_ref.shape[0]

    @pl.when(j == 0)
    def _():
        s_scr[...] = jnp.zeros(s_scr.shape, F32)
        prev_scr[...] = jnp.zeros(prev_scr.shape, F32)

    p = p_ref[...]
    pa = p[:, :A_SHIFT_W]
    z = p[:, A_SHIFT_W:]
    rows = _iota((ct, 1), 0)
    prev = jnp.where(rows == 0, prev_scr[...], pltpu.roll(pa, 1, axis=0))
    prev_scr[...] = pa[ct - 1:ct, :]
    pm = pa + (prev - pa) * mu_ref[...]
    ones = _head_ones()
    r, k, v, ld, av, bv = _rwkv_token_math(pm, w0_ref[...], w2_ref[...], a0_ref[...], a2_ref[...],
                                           kk_ref[...], ka_ref[...], ones)

    c = min(CHUNK, ct)
    ri, ci = _iota((c, c), 0), _iota((c, c), 1)
    strict = ri > ci
    incl = ri >= ci
    eye = (ri == ci).astype(F32)
    lt = incl.astype(F32)
    states = [s_scr[h] for h in range(N_HEADS)]
    o_rows = []
    for c0 in range(0, ct, c):
        sl = slice(c0, c0 + c)
        ldc = ld[sl]
        cum = _mm(lt, ldc)
        e_pos = jnp.exp(cum)
        e_neg = jnp.exp(-cum)
        at = av[sl] * jnp.exp(cum - ldc)
        bt = bv[sl] * e_neg
        kt = k[sl] * e_neg
        rt = r[sl] * e_pos
        vc = v[sl]
        g_end = e_pos[c - 1:c, :]
        o_heads = []
        for h in range(N_HEADS):
            hs = slice(h * HEAD_DIM, (h + 1) * HEAD_DIM)
            ah, bh, kh, rh, vh, gh = at[:, hs], bt[:, hs], kt[:, hs], rt[:, hs], vc[:, hs], g_end[:, hs]
            m_ab = jnp.where(strict, _mm_nt(ah, bh), 0.0)
            m_ak = jnp.where(strict, _mm_nt(ah, kh), 0.0)
            t_inv = _inv_unit_lower(-m_ab)
            a_hat = _mm(t_inv, ah)
            u1 = _mm(t_inv, _mm(m_ak, vh))
            m_rb = jnp.where(incl, _mm_nt(rh, bh), 0.0)
            m_rk = jnp.where(incl, _mm_nt(rh, kh), 0.0)
            r_hat = rh + _mm(m_rb, a_hat)
            o1 = _mm(m_rb, u1) + _mm(m_rk, vh)
            g_mat = (eye + _mm_tn(a_hat, bh)) * gh
            h_mat = (_mm_tn(u1, bh) + _mm_tn(vh, kh)) * gh
            s = states[h]
            o_heads.append(_mm_nt(r_hat, s) + o1)
            states[h] = _mm(s, g_mat) + h_mat
        o_rows.append(jnp.concatenate(o_heads, axis=1))
    o = o_rows[0] if len(o_rows) == 1 else jnp.concatenate(o_rows, axis=0)
    for h in range(N_HEADS):
        s_scr[h] = states[h]
    o_ref[...] = _rwkv_finish(o, r, k, v, z, rk_ref[...], lng_ref[...], lnb_ref[...], ones)

    @pl.when(j == last)
    def _():
        for h in range(N_HEADS):
            s_out_ref[h] = states[h]
        shift_out_ref[...] = pa[ct - 1:ct, :]


def _row(a):
    return a.reshape(1, -1)


def _rwkv_call(p_a, n_seq, seq_len, lp):
    ct = min(ROW_TILE, seq_len)
    p3 = p_a.reshape(n_seq, seq_len, A_W)
    params = [_row(lp["rwkv_mu"]), _row(lp["rwkv_w0"]), lp["rwkv_w2"], _row(lp["rwkv_a0"]), lp["rwkv_a2"],
              _row(lp["rwkv_k_k"]), _row(lp["rwkv_k_a"]), _row(lp["rwkv_r_k"]), _row(lp["rwkv_ln_g"]),
              _row(lp["rwkv_ln_b"])]
    o, s1, shift1 = pl.pallas_call(
        _rwkv_kernel,
        grid=(n_seq, seq_len // ct),
        in_specs=[pl.BlockSpec((None, ct, A_W), lambda b, j: (b, j, 0))]
        + [pl.BlockSpec(a.shape, lambda b, j: (0, 0)) for a in params],
        out_specs=[
            pl.BlockSpec((None, ct, W_MIX), lambda b, j: (b, j, 0)),
            pl.BlockSpec((None, N_HEADS, HEAD_DIM, HEAD_DIM), lambda b, j: (b, 0, 0, 0)),
            pl.BlockSpec((None, 1, A_SHIFT_W), lambda b, j: (b, 0, 0)),
        ],
        out_shape=[
            jax.ShapeDtypeStruct((n_seq, seq_len, W_MIX), F32),
            jax.ShapeDtypeStruct((n_seq, N_HEADS, HEAD_DIM, HEAD_DIM), F32),
            jax.ShapeDtypeStruct((n_seq, 1, A_SHIFT_W), F32),
        ],
        scratch_shapes=[pltpu.VMEM((N_HEADS, HEAD_DIM, HEAD_DIM), F32), pltpu.VMEM((1, A_SHIFT_W), F32)],
        compiler_params=pltpu.CompilerParams(dimension_semantics=("parallel", "arbitrary"),
                                             vmem_limit_bytes=VMEM_LIMIT),
        name="rwkv7_prompt",
    )(p3, *params)
    return o.reshape(n_seq * seq_len, W_MIX), s1, shift1.reshape(n_seq, A_SHIFT_W)


def _ret_kernel(p_ref, cos_ref, sin_ref, o_ref, s_out_ref, s_scr):
    j = pl.program_id(1)
    last = pl.num_programs(1) - 1
    ct = p_ref.shape[0]

    @pl.when(j == 0)
    def _():
        s_scr[...] = jnp.zeros(s_scr.shape, F32)

    p = p_ref[...]
    cos, sin = cos_ref[...], sin_ref[...]
    q = _rotary(p[:, 0:W_MIX], cos, sin)
    k = _rotary(p[:, W_MIX:2 * W_MIX], cos, sin) * (HEAD_DIM ** -0.5)
    v = p[:, 2 * W_MIX:3 * W_MIX]
    z = p[:, 3 * W_MIX:]
    c = min(RET_CHUNK, ct)
    ri, ci = _iota((c, c), 0), _iota((c, c), 1)
    causal = ri >= ci
    rel = jnp.where(causal, ri - ci, 0).astype(F32)
    idx = _iota((c, 1), 0).astype(F32)
    states = [s_scr[h] for h in range(N_HEADS)]
    o_heads_all = [[] for _ in range(N_HEADS)]
    for h in range(N_HEADS):
        lg = math.log(1.0 - 2.0 ** (-5.0 - h))
        decay = jnp.where(causal, jnp.exp(lg * rel), 0.0)
        q_dec = jnp.exp(lg * (idx + 1.0))
        k_dec = jnp.exp(lg * (c - 1.0 - idx))
        g_c = math.exp(lg * c)
        hs = slice(h * HEAD_DIM, (h + 1) * HEAD_DIM)
        s = states[h]
        for c0 in range(0, ct, c):
            sl = slice(c0, c0 + c)
            qh, kh, vh = q[sl, hs], k[sl, hs], v[sl, hs]
            s_in = _mm_nt(qh, kh) * decay
            o_heads_all[h].append(_mm(s_in, vh) + _mm(qh, s) * q_dec)
            s = s * g_c + _mm_tn(kh * k_dec, vh)
        states[h] = s
    cols = [oh[0] if len(oh) == 1 else jnp.concatenate(oh, axis=0) for oh in o_heads_all]
    o = jnp.concatenate(cols, axis=1)
    for h in range(N_HEADS):
        s_scr[h] = states[h]
    o_ref[...] = _head_rms_finish(o, z, _head_ones())

    @pl.when(j == last)
    def _():
        for h in range(N_HEADS):
            s_out_ref[h] = states[h]


def _rope_tables(pos):
    half = HEAD_DIM // 2
    inv = ROPE_BASE ** (-jnp.arange(half, dtype=F32) / half)
    ang = pos.astype(F32)[:, None] * inv[None, :]
    cos, sin = jnp.cos(ang), jnp.sin(ang)
    cos_t = jnp.tile(jnp.concatenate([cos, cos], axis=-1), (1, N_HEADS))
    sin_t = jnp.tile(jnp.concatenate([-sin, sin], axis=-1), (1, N_HEADS))
    return cos_t, sin_t


def _ret_call(p_b, n_seq, seq_len, cos_t, sin_t):
    ct = min(ROW_TILE, seq_len)
    p3 = p_b.reshape(n_seq, seq_len, B_W)
    o, s1 = pl.pallas_call(
        _ret_kernel,
        grid=(n_seq, seq_len // ct),
        in_specs=[
            pl.BlockSpec((None, ct, B_W), lambda b, j: (b, j, 0)),
            pl.BlockSpec((ct, W_MIX), lambda b, j: (j, 0)),
            pl.BlockSpec((ct, W_MIX), lambda b, j: (j, 0)),
        ],
        out_specs=[
            pl.BlockSpec((None, ct, W_MIX), lambda b, j: (b, j, 0)),
            pl.BlockSpec((None, N_HEADS, HEAD_DIM, HEAD_DIM), lambda b, j: (b, 0, 0, 0)),
        ],
        out_shape=[
            jax.ShapeDtypeStruct((n_seq, seq_len, W_MIX), F32),
            jax.ShapeDtypeStruct((n_seq, N_HEADS, HEAD_DIM, HEAD_DIM), F32),
        ],
        scratch_shapes=[pltpu.VMEM((N_HEADS, HEAD_DIM, HEAD_DIM), F32)],
        compiler_params=pltpu.CompilerParams(dimension_semantics=("parallel", "arbitrary"),
                                             vmem_limit_bytes=VMEM_LIMIT),
        name="retention_prompt",
    )(p3, cos_t, sin_t)
    return o.reshape(n_seq * seq_len, W_MIX), s1


def _lru_kernel(p_ref, cw_ref, cb_ref, gw_ref, gb_ref, sp_ref, o_ref, h_out_ref, conv_out_ref, ext_scr, h_scr):
    j = pl.program_id(1)
    last = pl.num_programs(1) - 1
    ct = p_ref.shape[0]

    @pl.when(j == 0)
    def _():
        h_scr[...] = jnp.zeros(h_scr.shape, F32)

    p = p_ref[...]
    xr = p[:, :W_MIX]
    z = p[:, W_MIX:]
    xc = _conv_tile(xr, ext_scr, cw_ref, j == 0) + cb_ref[...]
    a, b = _lru_token_math(xc, gw_ref[...], gb_ref[...], sp_ref[...])
    rows = _iota((ct, 1), 0)
    dist = 1
    while dist < ct:
        keep = rows >= dist
        a_prev = jnp.where(keep, pltpu.roll(a, dist, axis=0), 1.0)
        b_prev = jnp.where(keep, pltpu.roll(b, dist, axis=0), 0.0)
        b = a * b_prev + b
        a = a * a_prev
        dist *= 2
    hcur = a * h_scr[...] + b
    h_scr[...] = hcur[ct - 1:ct, :]
    o_ref[...] = hcur * _silu(z)

    @pl.when(j == last)
    def _():
        h_out_ref[...] = hcur[ct - 1:ct, :]
        conv_out_ref[...] = xr[ct - SUBLANES:ct, :]


def _block_diag_gates(gate_w):
    out = jnp.zeros((W_MIX, 2 * W_MIX), F32)
    for g in range(2):
        for n in range(N_HEADS):
            out = out.at[n * HEAD_DIM:(n + 1) * HEAD_DIM,
                         g * W_MIX + n * HEAD_DIM:g * W_MIX + (n + 1) * HEAD_DIM].set(gate_w[g, n])
    return out


def _lru_params(lp):
    return [lp["lru_conv_w"], _row(lp["lru_conv_b"]), _block_diag_gates(lp["lru_gate_w"]),
            _row(lp["lru_gate_b"]), _row(lp["lru_lambda"])]


def _lru_call(p_c, n_seq, seq_len, lp):
    ct = min(ROW_TILE, seq_len)
    p3 = p_c.reshape(n_seq, seq_len, C_W)
    params = _lru_params(lp)
    o, h1, conv_tail = pl.pallas_call(
        _lru_kernel,
        grid=(n_seq, seq_len // ct),
        in_specs=[pl.BlockSpec((None, ct, C_W), lambda b, j: (b, j, 0))]
        + [pl.BlockSpec(a.shape, lambda b, j: (0, 0)) for a in params],
        out_specs=[
            pl.BlockSpec((None, ct, W_MIX), lambda b, j: (b, j, 0)),
            pl.BlockSpec((None, 1, W_MIX), lambda b, j: (b, 0, 0)),
            pl.BlockSpec((None, SUBLANES, W_MIX), lambda b, j: (b, 0, 0)),
        ],
        out_shape=[
            jax.ShapeDtypeStruct((n_seq, seq_len, W_MIX), F32),
            jax.ShapeDtypeStruct((n_seq, 1, W_MIX), F32),
            jax.ShapeDtypeStruct((n_seq, SUBLANES, W_MIX), F32),
        ],
        scratch_shapes=[pltpu.VMEM((ct + SUBLANES, W_MIX), F32), pltpu.VMEM((1, W_MIX), F32)],
        compiler_params=pltpu.CompilerParams(dimension_semantics=("parallel", "arbitrary"),
                                             vmem_limit_bytes=VMEM_LIMIT),
        name="rglru_prompt",
    )(p3, *params)
    return (o.reshape(n_seq * seq_len, W_MIX), h1.reshape(n_seq, W_MIX),
            conv_tail[:, SUBLANES - (CONV_W - 1):, :])


def _gdn_kernel(p_ref, cw_ref, nal_ref, dtb_ref, ng_ref, o_ref, s_out_ref, conv_out_ref, ext_scr, s_scr):
    j = pl.program_id(1)
    last = pl.num_programs(1) - 1
    ct = p_ref.shape[0]

    @pl.when(j == 0)
    def _():
        s_scr[...] = jnp.zeros(s_scr.shape, F32)

    p = p_ref[...]
    raw = p[:, :D_QKV_W]
    z = p[:, D_QKV_W:D_QKV_W + W_MIX]
    b_raw = p[:, D_QKV_W + W_MIX:D_QKV_W + 2 * W_MIX]
    a_raw = p[:, D_QKV_W + 2 * W_MIX:]
    ones = _head_ones()
    qkv = _conv_tile(raw, ext_scr, cw_ref, j == 0)
    q, k, v, beta, g = _gdn_token_math(qkv, b_raw, a_raw, nal_ref[...], dtb_ref[...], ones)

    c = min(CHUNK, ct)
    ri, ci = _iota((c, c), 0), _iota((c, c), 1)
    lower = ri >= ci
    strict = ri > ci
    eye = (ri == ci).astype(F32)
    lt = lower.astype(F32)
    eye_h = (_iota((HEAD_DIM, HEAD_DIM), 0) == _iota((HEAD_DIM, HEAD_DIM), 1)).astype(F32)
    states = [s_scr[h] for h in range(N_HEADS)]
    o_rows = []
    for c0 in range(0, ct, c):
        sl = slice(c0, c0 + c)
        gc = _mm(lt, g[sl])
        kb = k[sl] * beta[sl]
        vb = v[sl] * beta[sl]
        e_gc = jnp.exp(gc)
        g_last = gc[c - 1:c, :]
        k_out = k[sl] * jnp.exp(g_last - gc)
        q_in = q[sl] * e_gc
        kbe = kb * e_gc
        e_last = jnp.exp(g_last)
        o_heads = []
        for h in range(N_HEADS):
            hs = slice(h * HEAD_DIM, (h + 1) * HEAD_DIM)
            gi = gc[:, hs]
            gj = _mm_nt(jnp.full((c, HEAD_DIM), 1.0 / HEAD_DIM, F32), gi)
            diff = gi[:, :c] - gj
            decay = jnp.where(lower, jnp.exp(jnp.where(lower, diff, 0.0)), 0.0)
            kh, qh = k[sl, hs], q[sl, hs]
            a_mat = jnp.where(strict, _mm_nt(kb[:, hs], kh) * decay, 0.0)
            t_inv = _inv_unit_lower(a_mat)
            u = _mm(t_inv, vb[:, hs])
            w = _mm(t_inv, kbe[:, hs])
            qk = _mm_nt(qh, kh) * decay
            g_mat = eye_h * e_last[:, hs] - _mm_tn(k_out[:, hs], w)
            h_mat = _mm_tn(k_out[:, hs], u)
            q_hat = q_in[:, hs] - _mm(qk, w)
            o1 = _mm(qk, u)
            s = states[h]
            o_heads.append(_mm(q_hat, s) + o1)
            states[h] = _mm(g_mat, s) + h_mat
        o_rows.append(jnp.concatenate(o_heads, axis=1))
    o = o_rows[0] if len(o_rows) == 1 else jnp.concatenate(o_rows, axis=0)
    for h in range(N_HEADS):
        s_scr[h] = states[h]
    o_ref[...] = _head_rms_finish(o, z, ones, ng_ref[...])

    @pl.when(j == last)
    def _():
        for h in range(N_HEADS):
            s_out_ref[h] = states[h]
        conv_out_ref[...] = raw[ct - SUBLANES:ct, :]


def _gdn_params(lp):
    return [lp["gdn_conv_w"], _row(jnp.repeat(lp["gdn_A_log"], HEAD_DIM)),
            _row(jnp.repeat(lp["gdn_dt_bias"], HEAD_DIM)), _row(jnp.tile(lp["gdn_norm_g"], N_HEADS))]


def _gdn_call(p_d, n_seq, seq_len, lp):
    ct = min(ROW_TILE, seq_len)
    p3 = p_d.reshape(n_seq, seq_len, D_PACK_W)
    params = _gdn_params(lp)
    o, s1, conv_tail = pl.pallas_call(
        _gdn_kernel,
        grid=(n_seq, seq_len // ct),
        in_specs=[pl.BlockSpec((None, ct, D_PACK_W), lambda b, j: (b, j, 0))]
        + [pl.BlockSpec(a.shape, lambda b, j: (0, 0)) for a in params],
        out_specs=[
            pl.BlockSpec((None, ct, W_MIX), lambda b, j: (b, j, 0)),
            pl.BlockSpec((None, N_HEADS, HEAD_DIM, HEAD_DIM), lambda b, j: (b, 0, 0, 0)),
            pl.BlockSpec((None, SUBLANES, D_QKV_W), lambda b, j: (b, 0, 0)),
        ],
        out_shape=[
            jax.ShapeDtypeStruct((n_seq, seq_len, W_MIX), F32),
            jax.ShapeDtypeStruct((n_seq, N_HEADS, HEAD_DIM, HEAD_DIM), F32),
            jax.ShapeDtypeStruct((n_seq, SUBLANES, D_QKV_W), F32),
        ],
        scratch_shapes=[pltpu.VMEM((ct + SUBLANES, D_QKV_W), F32),
                        pltpu.VMEM((N_HEADS, HEAD_DIM, HEAD_DIM), F32)],
        compiler_params=pltpu.CompilerParams(dimension_semantics=("parallel", "arbitrary"),
                                             vmem_limit_bytes=VMEM_LIMIT),
        name="gdn_prompt",
    )(p3, *params)
    return o.reshape(n_seq * seq_len, W_MIX), s1, conv_tail[:, SUBLANES - (CONV_W - 1):, :]


def _decode_pre_kernel(pa_ref, pb_ref, pc_ref, pd_ref, shift_ref, h0_ref, lconv_ref, gconv_ref, cos_ref, sin_ref,
                       mu_ref, w0_ref, w2_ref, a0_ref, a2_ref, kk_ref, ka_ref,
                       lcw_ref, lcb_ref, lgw_ref, lgb_ref, lsp_ref, gcw_ref, nal_ref, dtb_ref,
                       vec_ref, oc_ref, h1_ref, lconv1_ref, gconv1_ref):
    ones = _head_ones()
    pa_full = pa_ref[...]
    pa = pa_full[:, :A_SHIFT_W]
    pm = pa + (shift_ref[...] - pa) * mu_ref[...]
    r, k, v, ld, av, bv = _rwkv_token_math(pm, w0_ref[...], w2_ref[...], a0_ref[...], a2_ref[...],
                                           kk_ref[...], ka_ref[...], ones)
    vecs = [r, jnp.exp(ld), k, v, av, bv, pa_full[:, A_SHIFT_W:]]
    pb = pb_ref[...]
    cos, sin = cos_ref[...], sin_ref[...]
    vecs += [_rotary(pb[:, 0:W_MIX], cos, sin), _rotary(pb[:, W_MIX:2 * W_MIX], cos, sin) * (HEAD_DIM ** -0.5),
             pb[:, 2 * W_MIX:3 * W_MIX], pb[:, 3 * W_MIX:]]
    pc = pc_ref[...]
    xr = pc[:, :W_MIX]
    lconv = lconv_ref[...]
    taps = [lconv[:, i * W_MIX:(i + 1) * W_MIX] for i in range(CONV_W - 1)] + [xr]
    xc = taps[0] * lcw_ref[0:1, :]
    for i in range(1, CONV_W):
        xc = xc + taps[i] * lcw_ref[i:i + 1, :]
    xc = xc + lcb_ref[...]
    a, b = _lru_token_math(xc, lgw_ref[...], lgb_ref[...], lsp_ref[...])
    hcur = a * h0_ref[...] + b
    oc_ref[...] = hcur * _silu(pc[:, W_MIX:])
    h1_ref[...] = hcur
    lconv1_ref[...] = jnp.concatenate(taps[1:], axis=1)
    pd = pd_ref[...]
    raw = pd[:, :D_QKV_W]
    gconv = gconv_ref[...]
    gtaps = [gconv[:, i * D_QKV_W:(i + 1) * D_QKV_W] for i in range(CONV_W - 1)] + [raw]
    qkv = gtaps[0] * gcw_ref[0:1, :]
    for i in range(1, CONV_W):
        qkv = qkv + gtaps[i] * gcw_ref[i:i + 1, :]
    q, kg, vg, beta, g = _gdn_token_math(qkv, pd[:, D_QKV_W + W_MIX:D_QKV_W + 2 * W_MIX],
                                         pd[:, D_QKV_W + 2 * W_MIX:], nal_ref[...], dtb_ref[...], ones)
    gconv1_ref[...] = jnp.concatenate(gtaps[1:], axis=1)
    vecs += [q, kg, vg, beta, g, pd[:, D_QKV_W:D_QKV_W + W_MIX]]
    assert len(vecs) == N_VEC
    for i, vec in enumerate(vecs):
        vec_ref[i] = vec


def _split3(x):
    hi = x.astype(BF16)
    r1 = x - hi.astype(F32)
    mid = r1.astype(BF16)
    lo = (r1 - mid.astype(F32)).astype(BF16)
    return hi, mid, lo


def _rep(x, e_rep):
    out = None
    for piece in _split3(x):
        t = jnp.dot(piece, e_rep, preferred_element_type=F32)
        out = t if out is None else out + t
    return out


def _tile(x):
    return jnp.tile(x, (1, HEAD_DIM))


def _red_minor(y, e_rep):
    out = None
    for piece in _split3(y):
        t = lax.dot_general(piece, e_rep, (((1,), (1,)), ((), ())), preferred_element_type=F32)
        out = t if out is None else out + t
    return out


def _red_major(y):
    acc = y[:, 0:LANES]
    for m in range(1, y.shape[1] // LANES):
        acc = acc + y[:, m * LANES:(m + 1) * LANES]
    return acc[:, :HEAD_DIM] + acc[:, HEAD_DIM:]


def _decode_state_kernel(vec_ref, wkv_ref, ret_ref, gdn_ref, rk_ref, lng_ref, lnb_ref, gam_ref, ng_ref, e_ref,
                         wkv1_ref, ret1_ref, gdn1_ref, o_ref):
    e_rep = e_ref[...]
    r, w, k, v, av, bv, z_a = (vec_ref[i] for i in range(7))
    s = wkv_ref[...]
    sa = _red_minor(s * _tile(av), e_rep)
    s = s * _tile(w) + _rep(sa, e_rep) * _tile(bv) + _rep(v, e_rep) * _tile(k)
    wkv1_ref[...] = s
    o = _red_minor(s * _tile(r), e_rep)
    mean = jnp.mean(o, axis=-1, keepdims=True)
    dlt = o - mean
    var = jnp.mean(dlt * dlt, axis=-1, keepdims=True)
    on = dlt * lax.rsqrt(var + RWKV_GN_EPS) * lng_ref[...] + lnb_ref[...]
    bonus = jnp.sum(r * k * rk_ref[...], axis=-1, keepdims=True) * v
    o_ref[0] = (on + bonus) * _silu(z_a)
    q, k, v, z_b = (vec_ref[i] for i in range(7, 11))
    s = ret_ref[...] * gam_ref[:, 0:1] + _rep(k, e_rep) * _tile(v)
    ret1_ref[...] = s
    o = _red_major(_rep(q, e_rep) * s)
    o_ref[1] = _rms(o) * _silu(z_b)
    q, k, v, beta, g, z_d = (vec_ref[i] for i in range(11, 17))
    beta1, eg = beta[:, 0:1], jnp.exp(g[:, 0:1])
    s = gdn_ref[...]
    k_rep = _rep(k, e_rep)
    v_new = v * beta1 - _red_major(k_rep * s) * (beta1 * eg)
    o = _red_major(_rep(q, e_rep) * s) * eg + jnp.sum(q * k, axis=-1, keepdims=True) * v_new
    gdn1_ref[...] = s * eg + k_rep * _tile(v_new)
    o_ref[2] = _rms(o) * ng_ref[...] * _silu(z_d)


def _decode_layer(p_a, p_b, p_c, p_d, states, cos_t, sin_t, lp):
    wkv0, shift0, ret0, lru_h0, lru_conv0, gdn0, gdn_conv0 = states
    n = p_a.shape[0]
    flat = HEAD_DIM * HEAD_DIM
    rwkv_params = [_row(lp["rwkv_mu"]), _row(lp["rwkv_w0"]), lp["rwkv_w2"], _row(lp["rwkv_a0"]), lp["rwkv_a2"],
                   _row(lp["rwkv_k_k"]), _row(lp["rwkv_k_a"])]
    gdn_params = _gdn_params(lp)
    ins = [p_a, p_b, p_c, p_d, shift0, lru_h0, lru_conv0.reshape(n, -1), gdn_conv0.reshape(n, -1), cos_t, sin_t,
           *rwkv_params, *_lru_params(lp), *gdn_params[:3]]
    full = lambda a: pl.BlockSpec(a.shape, lambda i: (0,) * a.ndim)
    out_shapes = [
        jax.ShapeDtypeStruct((N_VEC, n, W_MIX), F32),
        jax.ShapeDtypeStruct((n, W_MIX), F32),
        jax.ShapeDtypeStruct((n, W_MIX), F32),
        jax.ShapeDtypeStruct((n, (CONV_W - 1) * W_MIX), F32),
        jax.ShapeDtypeStruct((n, (CONV_W - 1) * D_QKV_W), F32),
    ]
    vecs, o_c, lru_h1, lru_conv1, gdn_conv1 = pl.pallas_call(
        _decode_pre_kernel,
        grid=(1,),
        in_specs=[full(a) for a in ins],
        out_specs=[pl.BlockSpec(s.shape, lambda i, nd=len(s.shape): (0,) * nd) for s in out_shapes],
        out_shape=out_shapes,
        compiler_params=pltpu.CompilerParams(dimension_semantics=("arbitrary",), vmem_limit_bytes=VMEM_LIMIT),
        name="decode_tokens",
    )(*ins)

    vec_h = vecs.reshape(N_VEC, n, N_HEADS, HEAD_DIM).transpose(2, 0, 1, 3)
    head_rows = lambda a: a.reshape(N_HEADS, 1, HEAD_DIM)
    gam = jnp.broadcast_to((1.0 - 2.0 ** (-5.0 - jnp.arange(N_HEADS, dtype=F32)))[:, None, None],
                           (N_HEADS, 1, HEAD_DIM))
    e_rep = (jnp.arange(HEAD_DIM)[:, None] == (jnp.arange(flat)[None, :] // HEAD_DIM)).astype(BF16)
    per_head = [head_rows(lp["rwkv_r_k"]), head_rows(lp["rwkv_ln_g"]), head_rows(lp["rwkv_ln_b"]), gam]
    state_spec = pl.BlockSpec((n, flat), lambda h: (0, h))
    wkv1, ret1, gdn1, o_h = pl.pallas_call(
        _decode_state_kernel,
        grid=(N_HEADS,),
        in_specs=[pl.BlockSpec((None, N_VEC, n, HEAD_DIM), lambda h: (h, 0, 0, 0)), state_spec, state_spec, state_spec]
        + [pl.BlockSpec((None, 1, HEAD_DIM), lambda h: (h, 0, 0)) for _ in per_head]
        + [pl.BlockSpec((1, HEAD_DIM), lambda h: (0, 0)), pl.BlockSpec((HEAD_DIM, flat), lambda h: (0, 0))],
        out_specs=[state_spec, state_spec, state_spec,
                   pl.BlockSpec((None, 3, n, HEAD_DIM), lambda h: (h, 0, 0, 0))],
        out_shape=[jax.ShapeDtypeStruct((n, N_HEADS * flat), F32)] * 3
        + [jax.ShapeDtypeStruct((N_HEADS, 3, n, HEAD_DIM), F32)],
        compiler_params=pltpu.CompilerParams(dimension_semantics=("parallel",), vmem_limit_bytes=VMEM_LIMIT),
        name="decode_states",
    )(vec_h, wkv0.reshape(n, -1), ret0.reshape(n, -1), gdn0.reshape(n, -1), *per_head,
      _row(lp["gdn_norm_g"]), e_rep)
    o3 = o_h.transpose(1, 2, 0, 3).reshape(3, n, W_MIX)
    shape4 = (n, N_HEADS, HEAD_DIM, HEAD_DIM)
    new_states = (wkv1.reshape(shape4), p_a[:, :A_SHIFT_W], ret1.reshape(shape4), lru_h1,
                  lru_conv1.reshape(n, CONV_W - 1, W_MIX), gdn1.reshape(shape4),
                  gdn_conv1.reshape(n, CONV_W - 1, D_QKV_W))
    return (o3[0], o3[1], o_c, o3[2]), new_states


def _pack_in_weights(w_in):
    off_d = A_W + B_W + C_W
    off_ba = off_d + D_QKV_W
    off_z = off_ba + 2 * N_HEADS
    off_g = off_d + D_W
    w_pack = jnp.concatenate([
        w_in[:, :off_ba],
        w_in[:, off_z:off_g],
        jnp.repeat(w_in[:, off_ba:off_ba + N_HEADS], HEAD_DIM, axis=1),
        jnp.repeat(w_in[:, off_ba + N_HEADS:off_z], HEAD_DIM, axis=1),
    ], axis=1).astype(BF16)
    return w_pack, w_in[:, off_g:].astype(BF16)


def _prompt_layer(p_a, p_b, p_c, p_d, n_seq, seq_len, cos_t, sin_t, lp):
    o_a, wkv1, shift1 = _rwkv_call(p_a, n_seq, seq_len, lp)
    o_b, ret1 = _ret_call(p_b, n_seq, seq_len, cos_t, sin_t)
    o_c, lru_h1, lru_conv1 = _lru_call(p_c, n_seq, seq_len, lp)
    o_d, gdn1, gdn_conv1 = _gdn_call(p_d, n_seq, seq_len, lp)
    return (o_a, o_b, o_c, o_d), (wkv1, shift1, ret1, lru_h1, lru_conv1, gdn1, gdn_conv1)


def _run_group(x, mods, pos, carried, layers, final_g):
    n_seq, seq_len, d = x.shape
    x2 = x.reshape(n_seq * seq_len, d)
    cos_t, sin_t = _rope_tables(pos)
    new = []
    n_layers = len(layers)
    for l, lp in enumerate(layers):
        shift, scale, gate = (mods[l][:, i * d:(i + 1) * d] for i in range(3))
        p_a, p_b, p_c, p_d = _inproj_call(x2, scale, shift, lp["norm_g"], lp["w_pack"], seq_len)
        if carried is None:
            branches, st = _prompt_layer(p_a, p_b, p_c, p_d, n_seq, seq_len, cos_t, sin_t, lp)
        else:
            branches, st = _decode_layer(p_a, p_b, p_c, p_d, carried[l], cos_t, sin_t, lp)
        new.append(st)
        x2 = _outproj_call(x2, scale, shift, gate, lp["norm_g"], branches, lp["w_gate"], lp["w_up_bf16"],
                           lp["w_out_bf16"], final_g, seq_len, final=(l == n_layers - 1))
    stacked = tuple(jnp.stack([s[i] for s in new], axis=0) for i in range(7))
    return x2.reshape(n_seq, seq_len, d), stacked


def kernel(x_prompt, x_sample, c_prompt, c_sample, state_rwkv_wkv, state_rwkv_shift, state_ret, state_lru_h, state_lru_conv, state_gdn, state_gdn_conv, ada_w, ada_b, norm_g, w_in, rwkv_mu, rwkv_w0, rwkv_w2, rwkv_a0, rwkv_a2, rwkv_k_k, rwkv_k_a, rwkv_r_k, rwkv_ln_g, rwkv_ln_b, lru_conv_w, lru_conv_b, lru_gate_w, lru_gate_b, lru_lambda, gdn_conv_w, gdn_A_log, gdn_dt_bias, gdn_norm_g, w_up, w_out, final_g):
    n_layers = ada_w.shape[0]
    n_prompt, seq_len, _ = x_prompt.shape
    n_sample, dec_len, _ = x_sample.shape
    assert dec_len == 1, "the decode path handles one token per sequence"
    layers = []
    for l in range(n_layers):
        w_pack, w_gate = _pack_in_weights(w_in[l])
        layers.append(dict(
            norm_g=norm_g[l], w_pack=w_pack, w_gate=w_gate, w_up_bf16=w_up[l].astype(BF16),
            w_out_bf16=w_out[l].astype(BF16),
            rwkv_mu=rwkv_mu[l], rwkv_w0=rwkv_w0[l], rwkv_w2=rwkv_w2[l], rwkv_a0=rwkv_a0[l], rwkv_a2=rwkv_a2[l],
            rwkv_k_k=rwkv_k_k[l], rwkv_k_a=rwkv_k_a[l], rwkv_r_k=rwkv_r_k[l], rwkv_ln_g=rwkv_ln_g[l],
            rwkv_ln_b=rwkv_ln_b[l], lru_conv_w=lru_conv_w[l], lru_conv_b=lru_conv_b[l],
            lru_gate_w=lru_gate_w[l], lru_gate_b=lru_gate_b[l], lru_lambda=lru_lambda[l],
            gdn_conv_w=gdn_conv_w[l], gdn_A_log=gdn_A_log[l], gdn_dt_bias=gdn_dt_bias[l],
            gdn_norm_g=gdn_norm_g[l]))
    mods = _ada_call(jnp.concatenate([c_prompt, c_sample], axis=0), ada_w, ada_b)
    mods_p = [mods[l, :n_prompt] for l in range(n_layers)]
    mods_s = [mods[l, n_prompt:] for l in range(n_layers)]

    y_prompt, new_p = _run_group(x_prompt, mods_p, jnp.arange(seq_len, dtype=jnp.int32), None, layers, final_g)
    carried = [(state_rwkv_wkv[l], state_rwkv_shift[l], state_ret[l], state_lru_h[l], state_lru_conv[l],
                state_gdn[l], state_gdn_conv[l]) for l in range(n_layers)]
    pos_s = PAST_LEN + jnp.arange(dec_len, dtype=jnp.int32)
    y_sample, new_s = _run_group(x_sample, mods_s, pos_s, carried, layers, final_g)
    return (y_prompt, y_sample) + new_p + new_s
```

```python
import functools
import math

import jax
import jax.numpy as jnp
from jax import lax
from jax.experimental import pallas as pl
from jax.experimental.pallas import tpu as pltpu

F32 = jnp.float32
BF16 = jnp.bfloat16
HI = lax.Precision.HIGHEST

N_HEADS = 4
HEAD_DIM = 64
W_MIX = N_HEADS * HEAD_DIM
LORA = 64
CONV_W = 4
N_BRANCH = 4
LRU_C = 8.0
ROPE_BASE = 10000.0
EPS = 1e-6
RWKV_GN_EPS = 64e-5
PAST_LEN = 16384
A_SHIFT_W = 3 * W_MIX + 2 * LORA
A_W = A_SHIFT_W + W_MIX
B_W = 4 * W_MIX
C_W = 2 * W_MIX
D_QKV_W = 3 * W_MIX
D_W = D_QKV_W + 2 * N_HEADS + W_MIX
D_PACK_W = D_QKV_W + 3 * W_MIX

SUBLANES = 8
LANES = 128
VMEM_LIMIT = 56 * 1024 * 1024

CHUNK = 64
RET_CHUNK = 128
INV_BLOCK = 16
ROW_TILE = 256
N_VEC = 17


def _mm(a, b, prec=HI):
    return lax.dot_general(a, b, (((1,), (0,)), ((), ())), precision=prec, preferred_element_type=F32)


def _mm_nt(a, b, prec=HI):
    return lax.dot_general(a, b, (((1,), (1,)), ((), ())), precision=prec, preferred_element_type=F32)


def _mm_tn(a, b, prec=HI):
    return lax.dot_general(a, b, (((0,), (0,)), ((), ())), precision=prec, preferred_element_type=F32)


_NN = (((1,), (0,)), ((), ()))
_NT = (((1,), (1,)), ((), ()))
_TN = (((0,), (0,)), ((), ()))

P_INV = 3
P_STATE = 3
P_MISC = 1


class _Split:
    def __init__(self, x, passes):
        self.hi = x.astype(BF16)
        self.lo = (x - self.hi.astype(F32)).astype(BF16) if passes > 1 else None


def _dotp(a, b, dims=_NN, passes=1):
    a = a if isinstance(a, _Split) else _Split(a, passes)
    b = b if isinstance(b, _Split) else _Split(b, passes)
    d = lambda x, y: lax.dot_general(x, y, dims, preferred_element_type=F32)
    out = d(a.hi, b.hi)
    if passes > 1:
        out = out + (d(a.hi, b.lo) + d(a.lo, b.hi))
    return out


def _iota(shape, dim):
    return lax.broadcasted_iota(jnp.int32, shape, dim)


def _silu(x):
    return x * jax.nn.sigmoid(x)


def _softplus(x):
    return jnp.maximum(x, 0.0) + jnp.log1p(jnp.exp(-jnp.abs(x)))


def _head_ones():
    return (_iota((W_MIX, W_MIX), 0) // HEAD_DIM == _iota((W_MIX, W_MIX), 1) // HEAD_DIM).astype(F32)


def _head_sum(x, ones):
    return _mm(x, ones)


def _rms(x):
    return x * lax.rsqrt(jnp.mean(x * x, axis=-1, keepdims=True) + EPS)


def _inv_unit_lower(a):
    return _inv_unit_lower_many([a])[0]


def _inv_unit_lower_many(mats):
    n = mats[0].shape[0]
    ri, ci = _iota((n, n), 0), _iota((n, n), 1)
    eye = (ri == ci).astype(F32)
    diag_blk = (ri // INV_BLOCK) == (ci // INV_BLOCK)
    mm = lambda x, y: _dotp(x, y, _NN, P_INV)
    sp = lambda x: _Split(x, P_INV)
    d = [jnp.where(diag_blk, a, 0.0) for a in mats]
    nb = [a - di for a, di in zip(mats, d)]
    td = [eye - di for di in d]
    p = d
    for _ in range(int(math.log2(INV_BLOCK)) - 1):
        ps = [sp(pi) for pi in p]
        p = [mm(pi, pi) for pi in ps]
        td = [mm(ti, eye + pi) for ti, pi in zip(td, p)]
    tds = [sp(ti) for ti in td]
    x = [mm(ti, ni) for ti, ni in zip(tds, nb)]
    t = [eye - xi for xi in x]
    p = x
    for _ in range(int(math.log2(n // INV_BLOCK)) - 1):
        ps = [sp(pi) for pi in p]
        p = [mm(pi, pi) for pi in ps]
        t = [mm(ti, eye + pi) for ti, pi in zip(t, p)]
    return [mm(ti, tdi) for ti, tdi in zip(t, tds)]


def _ada_kernel(c_ref, w_ref, b_ref, o_ref):
    o_ref[...] = _mm(_silu(c_ref[...]), w_ref[...]) + b_ref[...]


def _ada_call(c_all, ada_w, ada_b):
    n_layers, d, d3 = ada_w.shape
    rows = c_all.shape[0]
    return pl.pallas_call(
        _ada_kernel,
        grid=(n_layers, d3 // d),
        in_specs=[
            pl.BlockSpec((rows, d), lambda l, j: (0, 0)),
            pl.BlockSpec((None, d, d), lambda l, j: (l, 0, j)),
            pl.BlockSpec((None, 1, d), lambda l, j: (l, 0, j)),
        ],
        out_specs=pl.BlockSpec((None, rows, d), lambda l, j: (l, 0, j)),
        out_shape=jax.ShapeDtypeStruct((n_layers, rows, d3), F32),
        compiler_params=pltpu.CompilerParams(dimension_semantics=("arbitrary", "arbitrary"),
                                             vmem_limit_bytes=VMEM_LIMIT),
        name="ada_mod",
    )(c_all, ada_w, ada_b.reshape(n_layers, 1, d3))


def _modulated_norm(x, g, scale, shift):
    return _rms(x) * g * (1.0 + scale) + shift


def _inproj_kernel(x_ref, sc_ref, sh_ref, g_ref, w_ref, oa_ref, ob_ref, oc_ref, od_ref):
    h = _modulated_norm(x_ref[...], g_ref[...], sc_ref[...], sh_ref[...]).astype(BF16)
    lo = 0
    for o_ref in (oa_ref, ob_ref, oc_ref, od_ref):
        wd = o_ref.shape[-1]
        o_ref[...] = jnp.dot(h, w_ref[:, lo:lo + wd], preferred_element_type=F32)
        lo += wd


def _mod_specs(mods, tm, seq_len):
    d = mods[0].shape[-1]
    if seq_len == 1:
        return [m for m in mods], [pl.BlockSpec((tm, d), lambda i: (i, 0)) for _ in mods]
    per_seq = seq_len // tm
    return ([m.reshape(m.shape[0], 1, d) for m in mods],
            [pl.BlockSpec((None, 1, d), lambda i: (i // per_seq, 0, 0)) for _ in mods])


def _inproj_call(x2, scale, shift, g, w_pack, seq_len):
    m, d = x2.shape
    tm = min(ROW_TILE, m) if seq_len > 1 else m
    tm = min(tm, seq_len) if seq_len > 1 else tm
    widths = (A_W, B_W, C_W, D_PACK_W)
    mods, mod_specs = _mod_specs((scale, shift), tm, seq_len)
    return pl.pallas_call(
        _inproj_kernel,
        grid=(m // tm,),
        in_specs=[pl.BlockSpec((tm, d), lambda i: (i, 0))] + mod_specs + [
            pl.BlockSpec((1, d), lambda i: (0, 0)),
            pl.BlockSpec(w_pack.shape, lambda i: (0, 0)),
        ],
        out_specs=[pl.BlockSpec((tm, wd), lambda i: (i, 0)) for wd in widths],
        out_shape=[jax.ShapeDtypeStruct((m, wd), F32) for wd in widths],
        compiler_params=pltpu.CompilerParams(dimension_semantics=("parallel",), vmem_limit_bytes=VMEM_LIMIT),
        name="in_proj",
    )(x2, *mods, g.reshape(1, d), w_pack)


def _outproj_kernel(x_ref, sc_ref, sh_ref, gt_ref, g_ref, ba_ref, bb_ref, bc_ref, bd_ref,
                    wg_ref, wup_ref, wout_ref, fg_ref, o_ref, *, final):
    x = x_ref[...]
    d = x.shape[-1]
    h = _modulated_norm(x, g_ref[...], sc_ref[...], sh_ref[...]).astype(BF16)
    merged = jnp.zeros(x.shape, F32)
    for n, br_ref in enumerate((ba_ref, bb_ref, bc_ref, bd_ref)):
        gl = jnp.dot(h, wg_ref[:, n * d:(n + 1) * d], preferred_element_type=F32)
        up = jnp.dot(br_ref[...].astype(BF16), wup_ref[n], preferred_element_type=F32)
        merged = merged + jax.nn.sigmoid(gl) * up
    out = jnp.dot(merged.astype(BF16), wout_ref[...], preferred_element_type=F32)
    xn = x + gt_ref[...] * out
    if final:
        xn = _rms(xn) * fg_ref[...]
    o_ref[...] = xn


def _outproj_call(x2, scale, shift, gate, g, branches, wg, wup, wout, final_g, seq_len, final):
    m, d = x2.shape
    tm = min(ROW_TILE, m, seq_len) if seq_len > 1 else m
    mods, mod_specs = _mod_specs((scale, shift, gate), tm, seq_len)
    full = lambda a: pl.BlockSpec(a.shape, lambda i: (0,) * a.ndim)
    return pl.pallas_call(
        functools.partial(_outproj_kernel, final=final),
        grid=(m // tm,),
        in_specs=[pl.BlockSpec((tm, d), lambda i: (i, 0))] + mod_specs + [pl.BlockSpec((1, d), lambda i: (0, 0))]
        + [pl.BlockSpec((tm, W_MIX), lambda i: (i, 0)) for _ in branches]
        + [full(wg), full(wup), full(wout), pl.BlockSpec((1, d), lambda i: (0, 0))],
        out_specs=pl.BlockSpec((tm, d), lambda i: (i, 0)),
        out_shape=jax.ShapeDtypeStruct((m, d), F32),
        compiler_params=pltpu.CompilerParams(dimension_semantics=("parallel",), vmem_limit_bytes=VMEM_LIMIT),
        name="out_proj",
    )(x2, *mods, g.reshape(1, d), *branches, wg, wup, wout, final_g.reshape(1, d))


def _rwkv_token_math(pm, w0, w2, a0, a2, k_k, k_a, ones):
    r = pm[:, 0:W_MIX]
    k = pm[:, W_MIX:2 * W_MIX]
    v = pm[:, 2 * W_MIX:3 * W_MIX]
    wd = pm[:, 3 * W_MIX:3 * W_MIX + LORA]
    ad = pm[:, 3 * W_MIX + LORA:]
    w_log = -_softplus(-(w0 + _mm(jnp.tanh(wd), w2))) - 0.5
    log_decay = -jnp.exp(w_log)
    a = jax.nn.sigmoid(a0 + _mm(ad, a2))
    kx = k * k_k
    kk = kx * lax.rsqrt(_head_sum(kx * kx, ones) + EPS)
    k = k * (1.0 + (a - 1.0) * k_a)
    return r, k, v, log_decay, -kk, kk * a


def _rwkv_finish(o, r, k, v, z, r_k, ln_g, ln_b, ones):
    mean = _head_sum(o, ones) * (1.0 / HEAD_DIM)
    dlt = o - mean
    var = _head_sum(dlt * dlt, ones) * (1.0 / HEAD_DIM)
    on = dlt * lax.rsqrt(var + RWKV_GN_EPS) * ln_g + ln_b
    bonus = _head_sum(r * k * r_k, ones) * v
    return (on + bonus) * _silu(z)


def _swap_halves(x):
    half = HEAD_DIM // 2
    n = x.shape[-1]
    first = (_iota(x.shape, 1) & half) == 0
    return jnp.where(first, pltpu.roll(x, n - half, axis=1), pltpu.roll(x, half, axis=1))


def _rotary(x, cos, sin):
    return x * cos + _swap_halves(x) * sin


def _lru_token_math(xc, gate_w, gate_b, lam):
    gates = _mm(xc, gate_w) + gate_b
    r_gate = jax.nn.sigmoid(gates[:, :W_MIX])
    i_gate = jax.nn.sigmoid(gates[:, W_MIX:])
    log_a = -LRU_C * r_gate * _softplus(-lam)
    a = jnp.exp(log_a)
    b = jnp.sqrt(1.0 - jnp.exp(2.0 * log_a)) * (i_gate * xc)
    return a, b


def _gdn_token_math(qkv, b_raw, a_raw, a_log, dt_bias, ones):
    qkv = _silu(qkv)
    q = qkv[:, 0:W_MIX]
    k = qkv[:, W_MIX:2 * W_MIX]
    v = qkv[:, 2 * W_MIX:]
    q = q * lax.rsqrt(_head_sum(q * q, ones) + EPS) * (HEAD_DIM ** -0.5)
    k = k * lax.rsqrt(_head_sum(k * k, ones) + EPS)
    beta = jax.nn.sigmoid(b_raw)
    g = -jnp.exp(a_log) * _softplus(a_raw + dt_bias)
    return q, k, v, beta, g


def _head_rms_finish(o, z, ones, gain=None):
    y = o * lax.rsqrt(_head_sum(o * o, ones) * (1.0 / HEAD_DIM) + EPS)
    if gain is not None:
        y = y * gain
    return y * _silu(z)


def _conv_tile(u, ext_ref, w_ref, first):
    n = u.shape[0]

    @pl.when(first)
    def _():
        ext_ref[0:SUBLANES, :] = jnp.zeros((SUBLANES, u.shape[1]), F32)

    ext_ref[SUBLANES:SUBLANES + n, :] = u
    out = None
    for j in range(CONV_W):
        back = CONV_W - 1 - j
        term = ext_ref[SUBLANES - back:SUBLANES - back + n, :] * w_ref[j:j + 1, :]
        out = term if out is None else out + term
    ext_ref[0:SUBLANES, :] = u[n - SUBLANES:n, :]
    return out


def _rwkv_kernel(p_ref, mu_ref, w0_ref, w2_ref, a0_ref, a2_ref, kk_ref, ka_ref, rk_ref, lng_ref, lnb_ref,
                 o_ref, s_out_ref, shift_out_ref, s_scr, prev_scr):
    j = pl.program_id(1)
    last = pl.num_programs(1) - 1
    ct = p_ref.shape[0]

    @pl.when(j == 0)
    def _():
        s_scr[...] = jnp.zeros(s_scr.shape, F32)
        prev_scr[...] = jnp.zeros(prev_scr.shape, F32)

    p = p_ref[...]
    pa = p[:, :A_SHIFT_W]
    z = p[:, A_SHIFT_W:]
    rows = _iota((ct, 1), 0)
    prev = jnp.where(rows == 0, prev_scr[...], pltpu.roll(pa, 1, axis=0))
    prev_scr[...] = pa[ct - 1:ct, :]
    pm = pa + (prev - pa) * mu_ref[...]
    ones = _head_ones()
    r, k, v, ld, av, bv = _rwkv_token_math(pm, w0_ref[...], w2_ref[...], a0_ref[...], a2_ref[...],
                                           kk_ref[...], ka_ref[...], ones)

    c = min(CHUNK, ct)
    ri, ci = _iota((c, c), 0), _iota((c, c), 1)
    strict = ri > ci
    incl = ri >= ci
    eye = (ri == ci).astype(F32)
    lt = incl.astype(F32)
    units = []
    for c0 in range(0, ct, c):
        sl = slice(c0, c0 + c)
        ldc = ld[sl]
        cum = _mm(lt, ldc)
        e_neg = jnp.exp(-cum)
        e_out = jnp.exp(cum[c - 1:c, :] - cum)
        at = av[sl] * jnp.exp(cum - ldc)
        rt = r[sl] * jnp.exp(cum)
        bt, kt = bv[sl] * e_neg, k[sl] * e_neg
        bo, ko = bv[sl] * e_out, k[sl] * e_out
        g_end = jnp.exp(cum[c - 1:c, :])
        vc = v[sl]
        for h in range(N_HEADS):
            hs = slice(h * HEAD_DIM, (h + 1) * HEAD_DIM)
            units.append(dict(a=at[:, hs], r=rt[:, hs], b=bt[:, hs], k=kt[:, hs], bo=bo[:, hs], ko=ko[:, hs],
                              v=vc[:, hs], g=g_end[:, hs]))
    misc = lambda x, y, dims=_NN: _dotp(x, y, dims, P_MISC)
    inv = lambda x, y, dims=_NN: _dotp(x, y, dims, P_INV)
    for u in units:
        lhs = jnp.concatenate([u["a"], u["r"]], axis=0)
        u["mb"] = misc(lhs, u["b"], _NT)
        u["mk"] = misc(lhs, u["k"], _NT)
    for u in units:
        u["m_ab"] = jnp.where(strict, u["mb"][:c], 0.0)
        u["m_ak"] = jnp.where(strict, u["mk"][:c], 0.0)
        u["m_rb"] = jnp.where(incl, u["mb"][c:], 0.0)
        u["m_rk"] = jnp.where(incl, u["mk"][c:], 0.0)
    for u, t_inv in zip(units, _inv_unit_lower_many([-u["m_ab"] for u in units])):
        u["t_inv"] = _Split(t_inv, P_INV)
    for u in units:
        u["makv"] = misc(u["m_ak"], u["v"])
    for u in units:
        u["a_hat"] = inv(u["t_inv"], u["a"])
        u["u1"] = inv(u["t_inv"], u["makv"])
    for u in units:
        u["r_hat"] = u["r"] + misc(u["m_rb"], u["a_hat"])
        u["o1"] = misc(u["m_rb"], u["u1"]) + misc(u["m_rk"], u["v"])
        u["g_t"] = eye * u["g"] + misc(u["bo"], u["a_hat"], _TN)
        u["h_t"] = misc(u["bo"], u["u1"], _TN) + misc(u["ko"], u["v"], _TN)
    states = [s_scr[h] for h in range(N_HEADS)]
    o_rows = []
    for ci_ in range(ct // c):
        o_heads = []
        for h in range(N_HEADS):
            u = units[ci_ * N_HEADS + h]
            zz = _dotp(jnp.concatenate([u["r_hat"], u["g_t"]], axis=0), states[h], _NN, P_STATE)
            o_heads.append(zz[:c] + u["o1"])
            states[h] = zz[c:] + u["h_t"]
        o_rows.append(jnp.concatenate(o_heads, axis=1))
    o = o_rows[0] if len(o_rows) == 1 else jnp.concatenate(o_rows, axis=0)
    for h in range(N_HEADS):
        s_scr[h] = states[h]
    o_ref[...] = _rwkv_finish(o, r, k, v, z, rk_ref[...], lng_ref[...], lnb_ref[...], ones)

    @pl.when(j == last)
    def _():
        eye_h = (_iota((HEAD_DIM, HEAD_DIM), 0) == _iota((HEAD_DIM, HEAD_DIM), 1)).astype(F32)
        for h in range(N_HEADS):
            s_out_ref[h] = _mm_nt(eye_h, states[h])
        shift_out_ref[...] = pa[ct - 1:ct, :]


def _row(a):
    return a.reshape(1, -1)


def _rwkv_call(p_a, n_seq, seq_len, lp):
    ct = min(ROW_TILE, seq_len)
    p3 = p_a.reshape(n_seq, seq_len, A_W)
    params = [_row(lp["rwkv_mu"]), _row(lp["rwkv_w0"]), lp["rwkv_w2"], _row(lp["rwkv_a0"]), lp["rwkv_a2"],
              _row(lp["rwkv_k_k"]), _row(lp["rwkv_k_a"]), _row(lp["rwkv_r_k"]), _row(lp["rwkv_ln_g"]),
              _row(lp["rwkv_ln_b"])]
    o, s1, shift1 = pl.pallas_call(
        _rwkv_kernel,
        grid=(n_seq, seq_len // ct),
        in_specs=[pl.BlockSpec((None, ct, A_W), lambda b, j: (b, j, 0))]
        + [pl.BlockSpec(a.shape, lambda b, j: (0, 0)) for a in params],
        out_specs=[
            pl.BlockSpec((None, ct, W_MIX), lambda b, j: (b, j, 0)),
            pl.BlockSpec((None, N_HEADS, HEAD_DIM, HEAD_DIM), lambda b, j: (b, 0, 0, 0)),
            pl.BlockSpec((None, 1, A_SHIFT_W), lambda b, j: (b, 0, 0)),
        ],
        out_shape=[
            jax.ShapeDtypeStruct((n_seq, seq_len, W_MIX), F32),
            jax.ShapeDtypeStruct((n_seq, N_HEADS, HEAD_DIM, HEAD_DIM), F32),
            jax.ShapeDtypeStruct((n_seq, 1, A_SHIFT_W), F32),
        ],
        scratch_shapes=[pltpu.VMEM((N_HEADS, HEAD_DIM, HEAD_DIM), F32), pltpu.VMEM((1, A_SHIFT_W), F32)],
        compiler_params=pltpu.CompilerParams(dimension_semantics=("parallel", "arbitrary"),
                                             vmem_limit_bytes=VMEM_LIMIT),
        name="rwkv7_prompt",
    )(p3, *params)
    return o.reshape(n_seq * seq_len, W_MIX), s1, shift1.reshape(n_seq, A_SHIFT_W)


def _ret_kernel(p_ref, cos_ref, sin_ref, o_ref, s_out_ref, s_scr):
    j = pl.program_id(1)
    last = pl.num_programs(1) - 1
    ct = p_ref.shape[0]

    @pl.when(j == 0)
    def _():
        s_scr[...] = jnp.zeros(s_scr.shape, F32)

    p = p_ref[...]
    cos, sin = cos_ref[...], sin_ref[...]
    q = _rotary(p[:, 0:W_MIX], cos, sin)
    k = _rotary(p[:, W_MIX:2 * W_MIX], cos, sin) * (HEAD_DIM ** -0.5)
    v = p[:, 2 * W_MIX:3 * W_MIX]
    z = p[:, 3 * W_MIX:]
    c = min(RET_CHUNK, ct)
    ri, ci = _iota((c, c), 0), _iota((c, c), 1)
    causal = ri >= ci
    rel = jnp.where(causal, ri - ci, 0).astype(F32)
    idx = _iota((c, 1), 0).astype(F32)
    states = [s_scr[h] for h in range(N_HEADS)]
    o_heads_all = [[] for _ in range(N_HEADS)]
    for h in range(N_HEADS):
        lg = math.log(1.0 - 2.0 ** (-5.0 - h))
        decay = jnp.where(causal, jnp.exp(lg * rel), 0.0)
        q_dec = jnp.exp(lg * (idx + 1.0))
        k_dec = jnp.exp(lg * (c - 1.0 - idx))
        g_c = math.exp(lg * c)
        hs = slice(h * HEAD_DIM, (h + 1) * HEAD_DIM)
        s = states[h]
        for c0 in range(0, ct, c):
            sl = slice(c0, c0 + c)
            qh, kh, vh = q[sl, hs], k[sl, hs], v[sl, hs]
            s_in = _mm_nt(qh, kh) * decay
            o_heads_all[h].append(_mm(s_in, vh) + _mm(qh, s) * q_dec)
            s = s * g_c + _mm_tn(kh * k_dec, vh)
        states[h] = s
    cols = [oh[0] if len(oh) == 1 else jnp.concatenate(oh, axis=0) for oh in o_heads_all]
    o = jnp.concatenate(cols, axis=1)
    for h in range(N_HEADS):
        s_scr[h] = states[h]
    o_ref[...] = _head_rms_finish(o, z, _head_ones())

    @pl.when(j == last)
    def _():
        for h in range(N_HEADS):
            s_out_ref[h] = states[h]


def _rope_tables(pos):
    half = HEAD_DIM // 2
    inv = ROPE_BASE ** (-jnp.arange(half, dtype=F32) / half)
    ang = pos.astype(F32)[:, None] * inv[None, :]
    cos, sin = jnp.cos(ang), jnp.sin(ang)
    cos_t = jnp.tile(jnp.concatenate([cos, cos], axis=-1), (1, N_HEADS))
    sin_t = jnp.tile(jnp.concatenate([-sin, sin], axis=-1), (1, N_HEADS))
    return cos_t, sin_t


def _ret_call(p_b, n_seq, seq_len, cos_t, sin_t):
    ct = min(ROW_TILE, seq_len)
    p3 = p_b.reshape(n_seq, seq_len, B_W)
    o, s1 = pl.pallas_call(
        _ret_kernel,
        grid=(n_seq, seq_len // ct),
        in_specs=[
            pl.BlockSpec((None, ct, B_W), lambda b, j: (b, j, 0)),
            pl.BlockSpec((ct, W_MIX), lambda b, j: (j, 0)),
            pl.BlockSpec((ct, W_MIX), lambda b, j: (j, 0)),
        ],
        out_specs=[
            pl.BlockSpec((None, ct, W_MIX), lambda b, j: (b, j, 0)),
            pl.BlockSpec((None, N_HEADS, HEAD_DIM, HEAD_DIM), lambda b, j: (b, 0, 0, 0)),
        ],
        out_shape=[
            jax.ShapeDtypeStruct((n_seq, seq_len, W_MIX), F32),
            jax.ShapeDtypeStruct((n_seq, N_HEADS, HEAD_DIM, HEAD_DIM), F32),
        ],
        scratch_shapes=[pltpu.VMEM((N_HEADS, HEAD_DIM, HEAD_DIM), F32)],
        compiler_params=pltpu.CompilerParams(dimension_semantics=("parallel", "arbitrary"),
                                             vmem_limit_bytes=VMEM_LIMIT),
        name="retention_prompt",
    )(p3, cos_t, sin_t)
    return o.reshape(n_seq * seq_len, W_MIX), s1


def _lru_kernel(p_ref, cw_ref, cb_ref, gw_ref, gb_ref, sp_ref, o_ref, h_out_ref, conv_out_ref, ext_scr, h_scr):
    j = pl.program_id(1)
    last = pl.num_programs(1) - 1
    ct = p_ref.shape[0]

    @pl.when(j == 0)
    def _():
        h_scr[...] = jnp.zeros(h_scr.shape, F32)

    p = p_ref[...]
    xr = p[:, :W_MIX]
    z = p[:, W_MIX:]
    xc = _conv_tile(xr, ext_scr, cw_ref, j == 0) + cb_ref[...]
    a, b = _lru_token_math(xc, gw_ref[...], gb_ref[...], sp_ref[...])
    rows = _iota((ct, 1), 0)
    dist = 1
    while dist < ct:
        keep = rows >= dist
        a_prev = jnp.where(keep, pltpu.roll(a, dist, axis=0), 1.0)
        b_prev = jnp.where(keep, pltpu.roll(b, dist, axis=0), 0.0)
        b = a * b_prev + b
        a = a * a_prev
        dist *= 2
    hcur = a * h_scr[...] + b
    h_scr[...] = hcur[ct - 1:ct, :]
    o_ref[...] = hcur * _silu(z)

    @pl.when(j == last)
    def _():
        h_out_ref[...] = hcur[ct - 1:ct, :]
        conv_out_ref[...] = xr[ct - SUBLANES:ct, :]


def _block_diag_gates(gate_w):
    out = jnp.zeros((W_MIX, 2 * W_MIX), F32)
    for g in range(2):
        for n in range(N_HEADS):
            out = out.at[n * HEAD_DIM:(n + 1) * HEAD_DIM,
                         g * W_MIX + n * HEAD_DIM:g * W_MIX + (n + 1) * HEAD_DIM].set(gate_w[g, n])
    return out


def _lru_params(lp):
    return [lp["lru_conv_w"], _row(lp["lru_conv_b"]), _block_diag_gates(lp["lru_gate_w"]),
            _row(lp["lru_gate_b"]), _row(lp["lru_lambda"])]


def _lru_call(p_c, n_seq, seq_len, lp):
    ct = min(ROW_TILE, seq_len)
    p3 = p_c.reshape(n_seq, seq_len, C_W)
    params = _lru_params(lp)
    o, h1, conv_tail = pl.pallas_call(
        _lru_kernel,
        grid=(n_seq, seq_len // ct),
        in_specs=[pl.BlockSpec((None, ct, C_W), lambda b, j: (b, j, 0))]
        + [pl.BlockSpec(a.shape, lambda b, j: (0, 0)) for a in params],
        out_specs=[
            pl.BlockSpec((None, ct, W_MIX), lambda b, j: (b, j, 0)),
            pl.BlockSpec((None, 1, W_MIX), lambda b, j: (b, 0, 0)),
            pl.BlockSpec((None, SUBLANES, W_MIX), lambda b, j: (b, 0, 0)),
        ],
        out_shape=[
            jax.ShapeDtypeStruct((n_seq, seq_len, W_MIX), F32),
            jax.ShapeDtypeStruct((n_seq, 1, W_MIX), F32),
            jax.ShapeDtypeStruct((n_seq, SUBLANES, W_MIX), F32),
        ],
        scratch_shapes=[pltpu.VMEM((ct + SUBLANES, W_MIX), F32), pltpu.VMEM((1, W_MIX), F32)],
        compiler_params=pltpu.CompilerParams(dimension_semantics=("parallel", "arbitrary"),
                                             vmem_limit_bytes=VMEM_LIMIT),
        name="rglru_prompt",
    )(p3, *params)
    return (o.reshape(n_seq * seq_len, W_MIX), h1.reshape(n_seq, W_MIX),
            conv_tail[:, SUBLANES - (CONV_W - 1):, :])


def _gdn_kernel(p_ref, cw_ref, nal_ref, dtb_ref, ng_ref, o_ref, s_out_ref, conv_out_ref, ext_scr, s_scr):
    j = pl.program_id(1)
    last = pl.num_programs(1) - 1
    ct = p_ref.shape[0]

    @pl.when(j == 0)
    def _():
        s_scr[...] = jnp.zeros(s_scr.shape, F32)

    p = p_ref[...]
    raw = p[:, :D_QKV_W]
    z = p[:, D_QKV_W:D_QKV_W + W_MIX]
    b_raw = p[:, D_QKV_W + W_MIX:D_QKV_W + 2 * W_MIX]
    a_raw = p[:, D_QKV_W + 2 * W_MIX:]
    ones = _head_ones()
    qkv = _conv_tile(raw, ext_scr, cw_ref, j == 0)
    q, k, v, beta, g = _gdn_token_math(qkv, b_raw, a_raw, nal_ref[...], dtb_ref[...], ones)

    c = min(CHUNK, ct)
    ri, ci = _iota((c, c), 0), _iota((c, c), 1)
    lower = ri >= ci
    strict = ri > ci
    eye = (ri == ci).astype(F32)
    lt = lower.astype(F32)
    eye_h = (_iota((HEAD_DIM, HEAD_DIM), 0) == _iota((HEAD_DIM, HEAD_DIM), 1)).astype(F32)
    units = []
    for c0 in range(0, ct, c):
        sl = slice(c0, c0 + c)
        gc = _mm(lt, g[sl])
        gj_all = _mm_nt(ones * (1.0 / HEAD_DIM), gc)
        kb = k[sl] * beta[sl]
        vb = v[sl] * beta[sl]
        e_gc = jnp.exp(gc)
        g_last = gc[c - 1:c, :]
        k_out = k[sl] * jnp.exp(g_last - gc)
        q_in = q[sl] * e_gc
        kbe = kb * e_gc
        e_last = jnp.exp(g_last)
        for h in range(N_HEADS):
            hs = slice(h * HEAD_DIM, (h + 1) * HEAD_DIM)
            diff = gc[:, hs][:, :c] - gj_all[h * HEAD_DIM:h * HEAD_DIM + c, :]
            decay = jnp.where(lower, jnp.exp(jnp.where(lower, diff, 0.0)), 0.0)
            units.append(dict(decay=decay, kb=kb[:, hs], q=q[sl, hs], k=k[sl, hs], vb=vb[:, hs], kbe=kbe[:, hs],
                              k_out=k_out[:, hs], q_in=q_in[:, hs], e_last=e_last[:, hs]))
    misc = lambda x, y, dims=_NN: _dotp(x, y, dims, P_MISC)
    inv = lambda x, y, dims=_NN: _dotp(x, y, dims, P_INV)
    for u in units:
        kq = misc(jnp.concatenate([u["kb"], u["q"]], axis=0), u["k"], _NT)
        u["a_mat"] = jnp.where(strict, kq[:c] * u["decay"], 0.0)
        u["qk"] = kq[c:] * u["decay"]
    for u, t_inv in zip(units, _inv_unit_lower_many([u["a_mat"] for u in units])):
        u["t_inv"] = _Split(t_inv, P_INV)
    for u in units:
        u["u"] = inv(u["t_inv"], u["vb"])
        u["w"] = inv(u["t_inv"], u["kbe"])
    for u in units:
        u["g_mat"] = eye_h * u["e_last"] - misc(u["k_out"], u["w"], _TN)
        u["h_mat"] = misc(u["k_out"], u["u"], _TN)
        u["q_hat"] = u["q_in"] - misc(u["qk"], u["w"])
        u["o1"] = misc(u["qk"], u["u"])
    states = [s_scr[h] for h in range(N_HEADS)]
    o_rows = []
    for ci_ in range(ct // c):
        o_heads = []
        for h in range(N_HEADS):
            u = units[ci_ * N_HEADS + h]
            zz = _dotp(jnp.concatenate([u["q_hat"], u["g_mat"]], axis=0), states[h], _NN, P_STATE)
            o_heads.append(zz[:c] + u["o1"])
            states[h] = zz[c:] + u["h_mat"]
        o_rows.append(jnp.concatenate(o_heads, axis=1))
    o = o_rows[0] if len(o_rows) == 1 else jnp.concatenate(o_rows, axis=0)
    for h in range(N_HEADS):
        s_scr[h] = states[h]
    o_ref[...] = _head_rms_finish(o, z, ones, ng_ref[...])

    @pl.when(j == last)
    def _():
        for h in range(N_HEADS):
            s_out_ref[h] = states[h]
        conv_out_ref[...] = raw[ct - SUBLANES:ct, :]


def _gdn_params(lp):
    return [lp["gdn_conv_w"], _row(jnp.repeat(lp["gdn_A_log"], HEAD_DIM)),
            _row(jnp.repeat(lp["gdn_dt_bias"], HEAD_DIM)), _row(jnp.tile(lp["gdn_norm_g"], N_HEADS))]


def _gdn_call(p_d, n_seq, seq_len, lp):
    ct = min(ROW_TILE, seq_len)
    p3 = p_d.reshape(n_seq, seq_len, D_PACK_W)
    params = _gdn_params(lp)
    o, s1, conv_tail = pl.pallas_call(
        _gdn_kernel,
        grid=(n_seq, seq_len // ct),
        in_specs=[pl.BlockSpec((None, ct, D_PACK_W), lambda b, j: (b, j, 0))]
        + [pl.BlockSpec(a.shape, lambda b, j: (0, 0)) for a in params],
        out_specs=[
            pl.BlockSpec((None, ct, W_MIX), lambda b, j: (b, j, 0)),
            pl.BlockSpec((None, N_HEADS, HEAD_DIM, HEAD_DIM), lambda b, j: (b, 0, 0, 0)),
            pl.BlockSpec((None, SUBLANES, D_QKV_W), lambda b, j: (b, 0, 0)),
        ],
        out_shape=[
            jax.ShapeDtypeStruct((n_seq, seq_len, W_MIX), F32),
            jax.ShapeDtypeStruct((n_seq, N_HEADS, HEAD_DIM, HEAD_DIM), F32),
            jax.ShapeDtypeStruct((n_seq, SUBLANES, D_QKV_W), F32),
        ],
        scratch_shapes=[pltpu.VMEM((ct + SUBLANES, D_QKV_W), F32),
                        pltpu.VMEM((N_HEADS, HEAD_DIM, HEAD_DIM), F32)],
        compiler_params=pltpu.CompilerParams(dimension_semantics=("parallel", "arbitrary"),
                                             vmem_limit_bytes=VMEM_LIMIT),
        name="gdn_prompt",
    )(p3, *params)
    return o.reshape(n_seq * seq_len, W_MIX), s1, conv_tail[:, SUBLANES - (CONV_W - 1):, :]


def _decode_pre_kernel(pa_ref, pb_ref, pc_ref, pd_ref, shift_ref, h0_ref, lconv_ref, gconv_ref, cos_ref, sin_ref,
                       mu_ref, w0_ref, w2_ref, a0_ref, a2_ref, kk_ref, ka_ref,
                       lcw_ref, lcb_ref, lgw_ref, lgb_ref, lsp_ref, gcw_ref, nal_ref, dtb_ref,
                       vec_ref, oc_ref, h1_ref, lconv1_ref, gconv1_ref):
    ones = _head_ones()
    pa_full = pa_ref[...]
    pa = pa_full[:, :A_SHIFT_W]
    pm = pa + (shift_ref[...] - pa) * mu_ref[...]
    r, k, v, ld, av, bv = _rwkv_token_math(pm, w0_ref[...], w2_ref[...], a0_ref[...], a2_ref[...],
                                           kk_ref[...], ka_ref[...], ones)
    vecs = [r, jnp.exp(ld), k, v, av, bv, pa_full[:, A_SHIFT_W:]]
    pb = pb_ref[...]
    cos, sin = cos_ref[...], sin_ref[...]
    vecs += [_rotary(pb[:, 0:W_MIX], cos, sin), _rotary(pb[:, W_MIX:2 * W_MIX], cos, sin) * (HEAD_DIM ** -0.5),
             pb[:, 2 * W_MIX:3 * W_MIX], pb[:, 3 * W_MIX:]]
    pc = pc_ref[...]
    xr = pc[:, :W_MIX]
    lconv = lconv_ref[...]
    taps = [lconv[:, i * W_MIX:(i + 1) * W_MIX] for i in range(CONV_W - 1)] + [xr]
    xc = taps[0] * lcw_ref[0:1, :]
    for i in range(1, CONV_W):
        xc = xc + taps[i] * lcw_ref[i:i + 1, :]
    xc = xc + lcb_ref[...]
    a, b = _lru_token_math(xc, lgw_ref[...], lgb_ref[...], lsp_ref[...])
    hcur = a * h0_ref[...] + b
    oc_ref[...] = hcur * _silu(pc[:, W_MIX:])
    h1_ref[...] = hcur
    lconv1_ref[...] = jnp.concatenate(taps[1:], axis=1)
    pd = pd_ref[...]
    raw = pd[:, :D_QKV_W]
    gconv = gconv_ref[...]
    gtaps = [gconv[:, i * D_QKV_W:(i + 1) * D_QKV_W] for i in range(CONV_W - 1)] + [raw]
    qkv = gtaps[0] * gcw_ref[0:1, :]
    for i in range(1, CONV_W):
        qkv = qkv + gtaps[i] * gcw_ref[i:i + 1, :]
    q, kg, vg, beta, g = _gdn_token_math(qkv, pd[:, D_QKV_W + W_MIX:D_QKV_W + 2 * W_MIX],
                                         pd[:, D_QKV_W + 2 * W_MIX:], nal_ref[...], dtb_ref[...], ones)
    gconv1_ref[...] = jnp.concatenate(gtaps[1:], axis=1)
    vecs += [q, kg, vg, beta, g, pd[:, D_QKV_W:D_QKV_W + W_MIX]]
    assert len(vecs) == N_VEC
    for i, vec in enumerate(vecs):
        vec_ref[i] = vec


def _split3(x):
    hi = x.astype(BF16)
    r1 = x - hi.astype(F32)
    mid = r1.astype(BF16)
    lo = (r1 - mid.astype(F32)).astype(BF16)
    return hi, mid, lo


def _rep(x, e_rep):
    out = None
    for piece in _split3(x):
        t = jnp.dot(piece, e_rep, preferred_element_type=F32)
        out = t if out is None else out + t
    return out


def _tile(x):
    return jnp.tile(x, (1, HEAD_DIM))


def _red_minor(y, e_rep):
    out = None
    for piece in _split3(y):
        t = lax.dot_general(piece, e_rep, (((1,), (1,)), ((), ())), preferred_element_type=F32)
        out = t if out is None else out + t
    return out


def _red_major(y):
    acc = y[:, 0:LANES]
    for m in range(1, y.shape[1] // LANES):
        acc = acc + y[:, m * LANES:(m + 1) * LANES]
    return acc[:, :HEAD_DIM] + acc[:, HEAD_DIM:]


def _decode_state_kernel(vec_ref, wkv_ref, ret_ref, gdn_ref, rk_ref, lng_ref, lnb_ref, gam_ref, ng_ref, e_ref,
                         wkv1_ref, ret1_ref, gdn1_ref, o_ref):
    e_rep = e_ref[...]
    r, w, k, v, av, bv, z_a = (vec_ref[i] for i in range(7))
    s = wkv_ref[...]
    sa = _red_minor(s * _tile(av), e_rep)
    s = s * _tile(w) + _rep(sa, e_rep) * _tile(bv) + _rep(v, e_rep) * _tile(k)
    wkv1_ref[...] = s
    o = _red_minor(s * _tile(r), e_rep)
    mean = jnp.mean(o, axis=-1, keepdims=True)
    dlt = o - mean
    var = jnp.mean(dlt * dlt, axis=-1, keepdims=True)
    on = dlt * lax.rsqrt(var + RWKV_GN_EPS) * lng_ref[...] + lnb_ref[...]
    bonus = jnp.sum(r * k * rk_ref[...], axis=-1, keepdims=True) * v
    o_ref[0] = (on + bonus) * _silu(z_a)
    q, k, v, z_b = (vec_ref[i] for i in range(7, 11))
    s = ret_ref[...] * gam_ref[:, 0:1] + _rep(k, e_rep) * _tile(v)
    ret1_ref[...] = s
    o = _red_major(_rep(q, e_rep) * s)
    o_ref[1] = _rms(o) * _silu(z_b)
    q, k, v, beta, g, z_d = (vec_ref[i] for i in range(11, 17))
    beta1, eg = beta[:, 0:1], jnp.exp(g[:, 0:1])
    s = gdn_ref[...]
    k_rep = _rep(k, e_rep)
    v_new = v * beta1 - _red_major(k_rep * s) * (beta1 * eg)
    o = _red_major(_rep(q, e_rep) * s) * eg + jnp.sum(q * k, axis=-1, keepdims=True) * v_new
    gdn1_ref[...] = s * eg + k_rep * _tile(v_new)
    o_ref[2] = _rms(o) * ng_ref[...] * _silu(z_d)


def _decode_layer(p_a, p_b, p_c, p_d, states, cos_t, sin_t, lp):
    wkv0, shift0, ret0, lru_h0, lru_conv0, gdn0, gdn_conv0 = states
    n = p_a.shape[0]
    flat = HEAD_DIM * HEAD_DIM
    rwkv_params = [_row(lp["rwkv_mu"]), _row(lp["rwkv_w0"]), lp["rwkv_w2"], _row(lp["rwkv_a0"]), lp["rwkv_a2"],
                   _row(lp["rwkv_k_k"]), _row(lp["rwkv_k_a"])]
    gdn_params = _gdn_params(lp)
    ins = [p_a, p_b, p_c, p_d, shift0, lru_h0, lru_conv0.reshape(n, -1), gdn_conv0.reshape(n, -1), cos_t, sin_t,
           *rwkv_params, *_lru_params(lp), *gdn_params[:3]]
    full = lambda a: pl.BlockSpec(a.shape, lambda i: (0,) * a.ndim)
    out_shapes = [
        jax.ShapeDtypeStruct((N_VEC, n, W_MIX), F32),
        jax.ShapeDtypeStruct((n, W_MIX), F32),
        jax.ShapeDtypeStruct((n, W_MIX), F32),
        jax.ShapeDtypeStruct((n, (CONV_W - 1) * W_MIX), F32),
        jax.ShapeDtypeStruct((n, (CONV_W - 1) * D_QKV_W), F32),
    ]
    vecs, o_c, lru_h1, lru_conv1, gdn_conv1 = pl.pallas_call(
        _decode_pre_kernel,
        grid=(1,),
        in_specs=[full(a) for a in ins],
        out_specs=[pl.BlockSpec(s.shape, lambda i, nd=len(s.shape): (0,) * nd) for s in out_shapes],
        out_shape=out_shapes,
        compiler_params=pltpu.CompilerParams(dimension_semantics=("arbitrary",), vmem_limit_bytes=VMEM_LIMIT),
        name="decode_tokens",
    )(*ins)

    vec_h = vecs.reshape(N_VEC, n, N_HEADS, HEAD_DIM).transpose(2, 0, 1, 3)
    head_rows = lambda a: a.reshape(N_HEADS, 1, HEAD_DIM)
    gam = jnp.broadcast_to((1.0 - 2.0 ** (-5.0 - jnp.arange(N_HEADS, dtype=F32)))[:, None, None],
                           (N_HEADS, 1, HEAD_DIM))
    e_rep = (jnp.arange(HEAD_DIM)[:, None] == (jnp.arange(flat)[None, :] // HEAD_DIM)).astype(BF16)
    per_head = [head_rows(lp["rwkv_r_k"]), head_rows(lp["rwkv_ln_g"]), head_rows(lp["rwkv_ln_b"]), gam]
    state_spec = pl.BlockSpec((n, flat), lambda h: (0, h))
    wkv1, ret1, gdn1, o_h = pl.pallas_call(
        _decode_state_kernel,
        grid=(N_HEADS,),
        in_specs=[pl.BlockSpec((None, N_VEC, n, HEAD_DIM), lambda h: (h, 0, 0, 0)), state_spec, state_spec, state_spec]
        + [pl.BlockSpec((None, 1, HEAD_DIM), lambda h: (h, 0, 0)) for _ in per_head]
        + [pl.BlockSpec((1, HEAD_DIM), lambda h: (0, 0)), pl.BlockSpec((HEAD_DIM, flat), lambda h: (0, 0))],
        out_specs=[state_spec, state_spec, state_spec,
                   pl.BlockSpec((None, 3, n, HEAD_DIM), lambda h: (h, 0, 0, 0))],
        out_shape=[jax.ShapeDtypeStruct((n, N_HEADS * flat), F32)] * 3
        + [jax.ShapeDtypeStruct((N_HEADS, 3, n, HEAD_DIM), F32)],
        compiler_params=pltpu.CompilerParams(dimension_semantics=("parallel",), vmem_limit_bytes=VMEM_LIMIT),
        name="decode_states",
    )(vec_h, wkv0.reshape(n, -1), ret0.reshape(n, -1), gdn0.reshape(n, -1), *per_head,
      _row(lp["gdn_norm_g"]), e_rep)
    o3 = o_h.transpose(1, 2, 0, 3).reshape(3, n, W_MIX)
    shape4 = (n, N_HEADS, HEAD_DIM, HEAD_DIM)
    new_states = (wkv1.reshape(shape4), p_a[:, :A_SHIFT_W], ret1.reshape(shape4), lru_h1,
                  lru_conv1.reshape(n, CONV_W - 1, W_MIX), gdn1.reshape(shape4),
                  gdn_conv1.reshape(n, CONV_W - 1, D_QKV_W))
    return (o3[0], o3[1], o_c, o3[2]), new_states


def _pack_in_weights(w_in):
    off_d = A_W + B_W + C_W
    off_ba = off_d + D_QKV_W
    off_z = off_ba + 2 * N_HEADS
    off_g = off_d + D_W
    w_pack = jnp.concatenate([
        w_in[:, :off_ba],
        w_in[:, off_z:off_g],
        jnp.repeat(w_in[:, off_ba:off_ba + N_HEADS], HEAD_DIM, axis=1),
        jnp.repeat(w_in[:, off_ba + N_HEADS:off_z], HEAD_DIM, axis=1),
    ], axis=1).astype(BF16)
    return w_pack, w_in[:, off_g:].astype(BF16)


def _prompt_layer(p_a, p_b, p_c, p_d, n_seq, seq_len, cos_t, sin_t, lp):
    o_a, wkv1, shift1 = _rwkv_call(p_a, n_seq, seq_len, lp)
    o_b, ret1 = _ret_call(p_b, n_seq, seq_len, cos_t, sin_t)
    o_c, lru_h1, lru_conv1 = _lru_call(p_c, n_seq, seq_len, lp)
    o_d, gdn1, gdn_conv1 = _gdn_call(p_d, n_seq, seq_len, lp)
    return (o_a, o_b, o_c, o_d), (wkv1, shift1, ret1, lru_h1, lru_conv1, gdn1, gdn_conv1)


def _run_group(x, mods, pos, carried, layers, final_g):
    n_seq, seq_len, d = x.shape
    x2 = x.reshape(n_seq * seq_len, d)
    cos_t, sin_t = _rope_tables(pos)
    new = []
    n_layers = len(layers)
    for l, lp in enumerate(layers):
        shift, scale, gate = (mods[l][:, i * d:(i + 1) * d] for i in range(3))
        p_a, p_b, p_c, p_d = _inproj_call(x2, scale, shift, lp["norm_g"], lp["w_pack"], seq_len)
        if carried is None:
            branches, st = _prompt_layer(p_a, p_b, p_c, p_d, n_seq, seq_len, cos_t, sin_t, lp)
        else:
            branches, st = _decode_layer(p_a, p_b, p_c, p_d, carried[l], cos_t, sin_t, lp)
        new.append(st)
        x2 = _outproj_call(x2, scale, shift, gate, lp["norm_g"], branches, lp["w_gate"], lp["w_up_bf16"],
                           lp["w_out_bf16"], final_g, seq_len, final=(l == n_layers - 1))
    stacked = tuple(jnp.stack([s[i] for s in new], axis=0) for i in range(7))
    return x2.reshape(n_seq, seq_len, d), stacked


def kernel(x_prompt, x_sample, c_prompt, c_sample, state_rwkv_wkv, state_rwkv_shift, state_ret, state_lru_h, state_lru_conv, state_gdn, state_gdn_conv, ada_w, ada_b, norm_g, w_in, rwkv_mu, rwkv_w0, rwkv_w2, rwkv_a0, rwkv_a2, rwkv_k_k, rwkv_k_a, rwkv_r_k, rwkv_ln_g, rwkv_ln_b, lru_conv_w, lru_conv_b, lru_gate_w, lru_gate_b, lru_lambda, gdn_conv_w, gdn_A_log, gdn_dt_bias, gdn_norm_g, w_up, w_out, final_g):
    n_layers = ada_w.shape[0]
    n_prompt, seq_len, _ = x_prompt.shape
    n_sample, dec_len, _ = x_sample.shape
    assert dec_len == 1, "the decode path handles one token per sequence"
    layers = []
    for l in range(n_layers):
        w_pack, w_gate = _pack_in_weights(w_in[l])
        layers.append(dict(
            norm_g=norm_g[l], w_pack=w_pack, w_gate=w_gate, w_up_bf16=w_up[l].astype(BF16),
            w_out_bf16=w_out[l].astype(BF16),
            rwkv_mu=rwkv_mu[l], rwkv_w0=rwkv_w0[l], rwkv_w2=rwkv_w2[l], rwkv_a0=rwkv_a0[l], rwkv_a2=rwkv_a2[l],
            rwkv_k_k=rwkv_k_k[l], rwkv_k_a=rwkv_k_a[l], rwkv_r_k=rwkv_r_k[l], rwkv_ln_g=rwkv_ln_g[l],
            rwkv_ln_b=rwkv_ln_b[l], lru_conv_w=lru_conv_w[l], lru_conv_b=lru_conv_b[l],
            lru_gate_w=lru_gate_w[l], lru_gate_b=lru_gate_b[l], lru_lambda=lru_lambda[l],
            gdn_conv_w=gdn_conv_w[l], gdn_A_log=gdn_A_log[l], gdn_dt_bias=gdn_dt_bias[l],
            gdn_norm_g=gdn_norm_g[l]))
    mods = _ada_call(jnp.concatenate([c_prompt, c_sample], axis=0), ada_w, ada_b)
    mods_p = [mods[l, :n_prompt] for l in range(n_layers)]
    mods_s = [mods[l, n_prompt:] for l in range(n_layers)]

    y_prompt, new_p = _run_group(x_prompt, mods_p, jnp.arange(seq_len, dtype=jnp.int32), None, layers, final_g)
    carried = [(state_rwkv_wkv[l], state_rwkv_shift[l], state_ret[l], state_lru_h[l], state_lru_conv[l],
                state_gdn[l], state_gdn_conv[l]) for l in range(n_layers)]
    pos_s = PAST_LEN + jnp.arange(dec_len, dtype=jnp.int32)
    y_sample, new_s = _run_group(x_sample, mods_s, pos_s, carried, layers, final_g)
    return (y_prompt, y_sample) + new_p + new_s
```

```python
import functools
import math

import jax
import jax.numpy as jnp
from jax import lax
from jax.experimental import pallas as pl
from jax.experimental.pallas import tpu as pltpu

F32 = jnp.float32
BF16 = jnp.bfloat16
HI = lax.Precision.HIGHEST

N_HEADS = 4
HEAD_DIM = 64
W_MIX = N_HEADS * HEAD_DIM
LORA = 64
CONV_W = 4
N_BRANCH = 4
LRU_C = 8.0
ROPE_BASE = 10000.0
EPS = 1e-6
RWKV_GN_EPS = 64e-5
PAST_LEN = 16384
A_SHIFT_W = 3 * W_MIX + 2 * LORA
A_W = A_SHIFT_W + W_MIX
B_W = 4 * W_MIX
C_W = 2 * W_MIX
D_QKV_W = 3 * W_MIX
D_W = D_QKV_W + 2 * N_HEADS + W_MIX
D_PACK_W = D_QKV_W + 3 * W_MIX

SUBLANES = 8
LANES = 128
VMEM_LIMIT = 56 * 1024 * 1024

CHUNK = 64
RET_CHUNK = 128
INV_BLOCK = 16
ROW_TILE = 256
N_VEC = 17


def _mm(a, b, prec=HI):
    return lax.dot_general(a, b, (((1,), (0,)), ((), ())), precision=prec, preferred_element_type=F32)


def _mm_nt(a, b, prec=HI):
    return lax.dot_general(a, b, (((1,), (1,)), ((), ())), precision=prec, preferred_element_type=F32)


def _mm_tn(a, b, prec=HI):
    return lax.dot_general(a, b, (((0,), (0,)), ((), ())), precision=prec, preferred_element_type=F32)


_NN = (((1,), (0,)), ((), ()))
_NT = (((1,), (1,)), ((), ()))
_TN = (((0,), (0,)), ((), ()))

P_INV = 1
P_STATE = 1
P_MISC = 1


class _Split:
    def __init__(self, x, passes):
        self.hi = x.astype(BF16)
        self.lo = (x - self.hi.astype(F32)).astype(BF16) if passes > 1 else None


def _dotp(a, b, dims=_NN, passes=1):
    a = a if isinstance(a, _Split) else _Split(a, passes)
    b = b if isinstance(b, _Split) else _Split(b, passes)
    d = lambda x, y: lax.dot_general(x, y, dims, preferred_element_type=F32)
    out = d(a.hi, b.hi)
    if passes > 1:
        out = out + (d(a.hi, b.lo) + d(a.lo, b.hi))
    return out


def _iota(shape, dim):
    return lax.broadcasted_iota(jnp.int32, shape, dim)


def _silu(x):
    return x * jax.nn.sigmoid(x)


def _softplus(x):
    return jnp.maximum(x, 0.0) + jnp.log1p(jnp.exp(-jnp.abs(x)))


def _pieces(x, n):
    out = []
    for i in range(n):
        p = x.astype(BF16)
        out.append(p)
        if i + 1 < n:
            x = x - p.astype(F32)
    return out


def _dot_const(x, const, dims=_NN, n=2, const_left=False):
    out = None
    for p in _pieces(x, n):
        t = lax.dot_general(*((const, p) if const_left else (p, const)), dims, preferred_element_type=F32)
        out = t if out is None else out + t
    return out


def _head_ones():
    return (_iota((W_MIX, W_MIX), 0) // HEAD_DIM == _iota((W_MIX, W_MIX), 1) // HEAD_DIM).astype(BF16)


def _head_sum(x, ones):
    return _dot_const(x, ones)


def _rms(x):
    return x * lax.rsqrt(jnp.mean(x * x, axis=-1, keepdims=True) + EPS)


def _inv_unit_lower(a):
    return _inv_unit_lower_many([a])[0]


def _inv_unit_lower_many(mats):
    n = mats[0].shape[0]
    ri, ci = _iota((n, n), 0), _iota((n, n), 1)
    eye = (ri == ci).astype(F32)
    diag_blk = (ri // INV_BLOCK) == (ci // INV_BLOCK)
    mm = lambda x, y: _dotp(x, y, _NN, P_INV)
    sp = lambda x: _Split(x, P_INV)
    d = [jnp.where(diag_blk, a, 0.0) for a in mats]
    nb = [a - di for a, di in zip(mats, d)]
    td = [eye - di for di in d]
    p = d
    for _ in range(int(math.log2(INV_BLOCK)) - 1):
        ps = [sp(pi) for pi in p]
        p = [mm(pi, pi) for pi in ps]
        td = [mm(ti, eye + pi) for ti, pi in zip(td, p)]
    tds = [sp(ti) for ti in td]
    x = [mm(ti, ni) for ti, ni in zip(tds, nb)]
    t = [eye - xi for xi in x]
    p = x
    for _ in range(int(math.log2(n // INV_BLOCK)) - 1):
        ps = [sp(pi) for pi in p]
        p = [mm(pi, pi) for pi in ps]
        t = [mm(ti, eye + pi) for ti, pi in zip(t, p)]
    return [mm(ti, tdi) for ti, tdi in zip(t, tds)]


def _ada_kernel(c_ref, w_ref, b_ref, o_ref):
    o_ref[...] = _mm(_silu(c_ref[...]), w_ref[...]) + b_ref[...]


def _ada_call(c_all, ada_w, ada_b):
    n_layers, d, d3 = ada_w.shape
    rows = c_all.shape[0]
    return pl.pallas_call(
        _ada_kernel,
        grid=(n_layers, d3 // d),
        in_specs=[
            pl.BlockSpec((rows, d), lambda l, j: (0, 0)),
            pl.BlockSpec((None, d, d), lambda l, j: (l, 0, j)),
            pl.BlockSpec((None, 1, d), lambda l, j: (l, 0, j)),
        ],
        out_specs=pl.BlockSpec((None, rows, d), lambda l, j: (l, 0, j)),
        out_shape=jax.ShapeDtypeStruct((n_layers, rows, d3), F32),
        compiler_params=pltpu.CompilerParams(dimension_semantics=("arbitrary", "arbitrary"),
                                             vmem_limit_bytes=VMEM_LIMIT),
        name="ada_mod",
    )(c_all, ada_w, ada_b.reshape(n_layers, 1, d3))


def _modulated_norm(x, g, scale, shift):
    return _rms(x) * g * (1.0 + scale) + shift


def _inproj_kernel(x_ref, sc_ref, sh_ref, g_ref, w_ref, oa_ref, ob_ref, oc_ref, od_ref):
    h = _modulated_norm(x_ref[...], g_ref[...], sc_ref[...], sh_ref[...]).astype(BF16)
    lo = 0
    for o_ref in (oa_ref, ob_ref, oc_ref, od_ref):
        wd = o_ref.shape[-1]
        o_ref[...] = jnp.dot(h, w_ref[:, lo:lo + wd], preferred_element_type=F32)
        lo += wd


def _mod_specs(mods, tm, seq_len):
    d = mods[0].shape[-1]
    if seq_len == 1:
        return [m for m in mods], [pl.BlockSpec((tm, d), lambda i: (i, 0)) for _ in mods]
    per_seq = seq_len // tm
    return ([m.reshape(m.shape[0], 1, d) for m in mods],
            [pl.BlockSpec((None, 1, d), lambda i: (i // per_seq, 0, 0)) for _ in mods])


def _inproj_call(x2, scale, shift, g, w_pack, seq_len):
    m, d = x2.shape
    tm = min(ROW_TILE, m) if seq_len > 1 else m
    tm = min(tm, seq_len) if seq_len > 1 else tm
    widths = (A_W, B_W, C_W, D_PACK_W)
    mods, mod_specs = _mod_specs((scale, shift), tm, seq_len)
    return pl.pallas_call(
        _inproj_kernel,
        grid=(m // tm,),
        in_specs=[pl.BlockSpec((tm, d), lambda i: (i, 0))] + mod_specs + [
            pl.BlockSpec((1, d), lambda i: (0, 0)),
            pl.BlockSpec(w_pack.shape, lambda i: (0, 0)),
        ],
        out_specs=[pl.BlockSpec((tm, wd), lambda i: (i, 0)) for wd in widths],
        out_shape=[jax.ShapeDtypeStruct((m, wd), F32) for wd in widths],
        compiler_params=pltpu.CompilerParams(dimension_semantics=("parallel",), vmem_limit_bytes=VMEM_LIMIT),
        name="in_proj",
    )(x2, *mods, g.reshape(1, d), w_pack)


def _outproj_kernel(x_ref, sc_ref, sh_ref, gt_ref, g_ref, ba_ref, bb_ref, bc_ref, bd_ref,
                    wg_ref, wup_ref, wout_ref, fg_ref, o_ref, *, final):
    x = x_ref[...]
    d = x.shape[-1]
    h = _modulated_norm(x, g_ref[...], sc_ref[...], sh_ref[...]).astype(BF16)
    merged = jnp.zeros(x.shape, F32)
    for n, br_ref in enumerate((ba_ref, bb_ref, bc_ref, bd_ref)):
        gl = jnp.dot(h, wg_ref[:, n * d:(n + 1) * d], preferred_element_type=F32)
        up = jnp.dot(br_ref[...].astype(BF16), wup_ref[n], preferred_element_type=F32)
        merged = merged + jax.nn.sigmoid(gl) * up
    out = jnp.dot(merged.astype(BF16), wout_ref[...], preferred_element_type=F32)
    xn = x + gt_ref[...] * out
    if final:
        xn = _rms(xn) * fg_ref[...]
    o_ref[...] = xn


def _outproj_call(x2, scale, shift, gate, g, branches, wg, wup, wout, final_g, seq_len, final):
    m, d = x2.shape
    tm = min(ROW_TILE, m, seq_len) if seq_len > 1 else m
    mods, mod_specs = _mod_specs((scale, shift, gate), tm, seq_len)
    full = lambda a: pl.BlockSpec(a.shape, lambda i: (0,) * a.ndim)
    return pl.pallas_call(
        functools.partial(_outproj_kernel, final=final),
        grid=(m // tm,),
        in_specs=[pl.BlockSpec((tm, d), lambda i: (i, 0))] + mod_specs + [pl.BlockSpec((1, d), lambda i: (0, 0))]
        + [pl.BlockSpec((tm, W_MIX), lambda i: (i, 0)) for _ in branches]
        + [full(wg), full(wup), full(wout), pl.BlockSpec((1, d), lambda i: (0, 0))],
        out_specs=pl.BlockSpec((tm, d), lambda i: (i, 0)),
        out_shape=jax.ShapeDtypeStruct((m, d), F32),
        compiler_params=pltpu.CompilerParams(dimension_semantics=("parallel",), vmem_limit_bytes=VMEM_LIMIT),
        name="out_proj",
    )(x2, *mods, g.reshape(1, d), *branches, wg, wup, wout, final_g.reshape(1, d))


def _rwkv_token_math(pm, w0, w2, a0, a2, k_k, k_a, ones):
    r = pm[:, 0:W_MIX]
    k = pm[:, W_MIX:2 * W_MIX]
    v = pm[:, 2 * W_MIX:3 * W_MIX]
    wd = pm[:, 3 * W_MIX:3 * W_MIX + LORA]
    ad = pm[:, 3 * W_MIX + LORA:]
    w_log = -_softplus(-(w0 + _dotp(jnp.tanh(wd), w2, _NN, P_MISC))) - 0.5
    log_decay = -jnp.exp(w_log)
    a = jax.nn.sigmoid(a0 + _dotp(ad, a2, _NN, P_MISC))
    kx = k * k_k
    kk = kx * lax.rsqrt(_head_sum(kx * kx, ones) + EPS)
    k = k * (1.0 + (a - 1.0) * k_a)
    return r, k, v, log_decay, -kk, kk * a


def _rwkv_finish(o, r, k, v, z, r_k, ln_g, ln_b, ones):
    mean = _head_sum(o, ones) * (1.0 / HEAD_DIM)
    dlt = o - mean
    var = _head_sum(dlt * dlt, ones) * (1.0 / HEAD_DIM)
    on = dlt * lax.rsqrt(var + RWKV_GN_EPS) * ln_g + ln_b
    bonus = _head_sum(r * k * r_k, ones) * v
    return (on + bonus) * _silu(z)


def _swap_halves(x):
    half = HEAD_DIM // 2
    n = x.shape[-1]
    first = (_iota(x.shape, 1) & half) == 0
    return jnp.where(first, pltpu.roll(x, n - half, axis=1), pltpu.roll(x, half, axis=1))


def _rotary(x, cos, sin):
    return x * cos + _swap_halves(x) * sin


def _lru_token_math(xc, gate_w, gate_b, lam):
    gates = _dotp(xc, gate_w, _NN, P_MISC) + gate_b
    r_gate = jax.nn.sigmoid(gates[:, :W_MIX])
    i_gate = jax.nn.sigmoid(gates[:, W_MIX:])
    log_a = -LRU_C * r_gate * _softplus(-lam)
    a = jnp.exp(log_a)
    b = jnp.sqrt(1.0 - jnp.exp(2.0 * log_a)) * (i_gate * xc)
    return a, b


def _gdn_token_math(qkv, b_raw, a_raw, a_log, dt_bias, ones):
    qkv = _silu(qkv)
    q = qkv[:, 0:W_MIX]
    k = qkv[:, W_MIX:2 * W_MIX]
    v = qkv[:, 2 * W_MIX:]
    q = q * lax.rsqrt(_head_sum(q * q, ones) + EPS) * (HEAD_DIM ** -0.5)
    k = k * lax.rsqrt(_head_sum(k * k, ones) + EPS)
    beta = jax.nn.sigmoid(b_raw)
    g = -jnp.exp(a_log) * _softplus(a_raw + dt_bias)
    return q, k, v, beta, g


def _head_rms_finish(o, z, ones, gain=None):
    y = o * lax.rsqrt(_head_sum(o * o, ones) * (1.0 / HEAD_DIM) + EPS)
    if gain is not None:
        y = y * gain
    return y * _silu(z)


def _conv_tile(u, ext_ref, w_ref, first):
    n = u.shape[0]

    @pl.when(first)
    def _():
        ext_ref[0:SUBLANES, :] = jnp.zeros((SUBLANES, u.shape[1]), F32)

    ext_ref[SUBLANES:SUBLANES + n, :] = u
    out = None
    for j in range(CONV_W):
        back = CONV_W - 1 - j
        term = ext_ref[SUBLANES - back:SUBLANES - back + n, :] * w_ref[j:j + 1, :]
        out = term if out is None else out + term
    ext_ref[0:SUBLANES, :] = u[n - SUBLANES:n, :]
    return out


def _rwkv_kernel(p_ref, mu_ref, w0_ref, w2_ref, a0_ref, a2_ref, kk_ref, ka_ref, rk_ref, lng_ref, lnb_ref,
                 o_ref, s_out_ref, shift_out_ref, s_scr, prev_scr):
    j = pl.program_id(1)
    last = pl.num_programs(1) - 1
    ct = p_ref.shape[0]

    @pl.when(j == 0)
    def _():
        s_scr[...] = jnp.zeros(s_scr.shape, F32)
        prev_scr[...] = jnp.zeros(prev_scr.shape, F32)

    p = p_ref[...]
    pa = p[:, :A_SHIFT_W]
    z = p[:, A_SHIFT_W:]
    rows = _iota((ct, 1), 0)
    prev = jnp.where(rows == 0, prev_scr[...], pltpu.roll(pa, 1, axis=0))
    prev_scr[...] = pa[ct - 1:ct, :]
    pm = pa + (prev - pa) * mu_ref[...]
    ones = _head_ones()
    r, k, v, ld, av, bv = _rwkv_token_math(pm, w0_ref[...], w2_ref[...], a0_ref[...], a2_ref[...],
                                           kk_ref[...], ka_ref[...], ones)

    c = min(CHUNK, ct)
    ri, ci = _iota((c, c), 0), _iota((c, c), 1)
    strict = ri > ci
    incl = ri >= ci
    eye = (ri == ci).astype(F32)
    lt = incl.astype(BF16)
    units = []
    for c0 in range(0, ct, c):
        sl = slice(c0, c0 + c)
        ldc = ld[sl]
        cum = _dot_const(ldc, lt, _NN, 3, const_left=True)
        e_neg = jnp.exp(-cum)
        e_out = jnp.exp(cum[c - 1:c, :] - cum)
        at = av[sl] * jnp.exp(cum - ldc)
        rt = r[sl] * jnp.exp(cum)
        bt, kt = bv[sl] * e_neg, k[sl] * e_neg
        bo, ko = bv[sl] * e_out, k[sl] * e_out
        g_end = jnp.exp(cum[c - 1:c, :])
        vc = v[sl]
        for h in range(N_HEADS):
            hs = slice(h * HEAD_DIM, (h + 1) * HEAD_DIM)
            units.append(dict(a=at[:, hs], r=rt[:, hs], b=bt[:, hs], k=kt[:, hs], bo=bo[:, hs], ko=ko[:, hs],
                              v=vc[:, hs], g=g_end[:, hs]))
    misc = lambda x, y, dims=_NN: _dotp(x, y, dims, P_MISC)
    inv = lambda x, y, dims=_NN: _dotp(x, y, dims, P_INV)
    for u in units:
        lhs = jnp.concatenate([u["a"], u["r"]], axis=0)
        u["mb"] = misc(lhs, u["b"], _NT)
        u["mk"] = misc(lhs, u["k"], _NT)
    for u in units:
        u["m_ab"] = jnp.where(strict, u["mb"][:c], 0.0)
        u["m_ak"] = jnp.where(strict, u["mk"][:c], 0.0)
        u["m_rb"] = jnp.where(incl, u["mb"][c:], 0.0)
        u["m_rk"] = jnp.where(incl, u["mk"][c:], 0.0)
    for u, t_inv in zip(units, _inv_unit_lower_many([-u["m_ab"] for u in units])):
        u["t_inv"] = _Split(t_inv, P_INV)
    for u in units:
        u["makv"] = misc(u["m_ak"], u["v"])
    for u in units:
        u["a_hat"] = inv(u["t_inv"], u["a"])
        u["u1"] = inv(u["t_inv"], u["makv"])
    for u in units:
        u["r_hat"] = u["r"] + misc(u["m_rb"], u["a_hat"])
        u["o1"] = misc(u["m_rb"], u["u1"]) + misc(u["m_rk"], u["v"])
        u["g_t"] = eye * u["g"] + misc(u["bo"], u["a_hat"], _TN)
        u["h_t"] = misc(u["bo"], u["u1"], _TN) + misc(u["ko"], u["v"], _TN)
    states = [s_scr[h] for h in range(N_HEADS)]
    o_rows = []
    for ci_ in range(ct // c):
        o_heads = []
        for h in range(N_HEADS):
            u = units[ci_ * N_HEADS + h]
            zz = _dotp(jnp.concatenate([u["r_hat"], u["g_t"]], axis=0), states[h], _NN, P_STATE)
            o_heads.append(zz[:c] + u["o1"])
            states[h] = zz[c:] + u["h_t"]
        o_rows.append(jnp.concatenate(o_heads, axis=1))
    o = o_rows[0] if len(o_rows) == 1 else jnp.concatenate(o_rows, axis=0)
    for h in range(N_HEADS):
        s_scr[h] = states[h]
    o_ref[...] = _rwkv_finish(o, r, k, v, z, rk_ref[...], lng_ref[...], lnb_ref[...], ones)

    @pl.when(j == last)
    def _():
        eye_h = (_iota((HEAD_DIM, HEAD_DIM), 0) == _iota((HEAD_DIM, HEAD_DIM), 1)).astype(F32)
        for h in range(N_HEADS):
            s_out_ref[h] = _mm_nt(eye_h, states[h])
        shift_out_ref[...] = pa[ct - 1:ct, :]


def _row(a):
    return a.reshape(1, -1)


def _rwkv_call(p_a, n_seq, seq_len, lp):
    ct = min(ROW_TILE, seq_len)
    p3 = p_a.reshape(n_seq, seq_len, A_W)
    params = [_row(lp["rwkv_mu"]), _row(lp["rwkv_w0"]), lp["rwkv_w2"], _row(lp["rwkv_a0"]), lp["rwkv_a2"],
              _row(lp["rwkv_k_k"]), _row(lp["rwkv_k_a"]), _row(lp["rwkv_r_k"]), _row(lp["rwkv_ln_g"]),
              _row(lp["rwkv_ln_b"])]
    o, s1, shift1 = pl.pallas_call(
        _rwkv_kernel,
        grid=(n_seq, seq_len // ct),
        in_specs=[pl.BlockSpec((None, ct, A_W), lambda b, j: (b, j, 0))]
        + [pl.BlockSpec(a.shape, lambda b, j: (0, 0)) for a in params],
        out_specs=[
            pl.BlockSpec((None, ct, W_MIX), lambda b, j: (b, j, 0)),
            pl.BlockSpec((None, N_HEADS, HEAD_DIM, HEAD_DIM), lambda b, j: (b, 0, 0, 0)),
            pl.BlockSpec((None, 1, A_SHIFT_W), lambda b, j: (b, 0, 0)),
        ],
        out_shape=[
            jax.ShapeDtypeStruct((n_seq, seq_len, W_MIX), F32),
            jax.ShapeDtypeStruct((n_seq, N_HEADS, HEAD_DIM, HEAD_DIM), F32),
            jax.ShapeDtypeStruct((n_seq, 1, A_SHIFT_W), F32),
        ],
        scratch_shapes=[pltpu.VMEM((N_HEADS, HEAD_DIM, HEAD_DIM), F32), pltpu.VMEM((1, A_SHIFT_W), F32)],
        compiler_params=pltpu.CompilerParams(dimension_semantics=("parallel", "arbitrary"),
                                             vmem_limit_bytes=VMEM_LIMIT),
        name="rwkv7_prompt",
    )(p3, *params)
    return o.reshape(n_seq * seq_len, W_MIX), s1, shift1.reshape(n_seq, A_SHIFT_W)


def _ret_kernel(p_ref, cos_ref, sin_ref, o_ref, s_out_ref, s_scr):
    j = pl.program_id(1)
    last = pl.num_programs(1) - 1
    ct = p_ref.shape[0]

    @pl.when(j == 0)
    def _():
        s_scr[...] = jnp.zeros(s_scr.shape, F32)

    p = p_ref[...]
    cos, sin = cos_ref[...], sin_ref[...]
    q = _rotary(p[:, 0:W_MIX], cos, sin)
    k = _rotary(p[:, W_MIX:2 * W_MIX], cos, sin) * (HEAD_DIM ** -0.5)
    v = p[:, 2 * W_MIX:3 * W_MIX]
    z = p[:, 3 * W_MIX:]
    c = min(RET_CHUNK, ct)
    ri, ci = _iota((c, c), 0), _iota((c, c), 1)
    causal = ri >= ci
    rel = jnp.where(causal, ri - ci, 0).astype(F32)
    idx = _iota((c, 1), 0).astype(F32)
    states = [s_scr[h] for h in range(N_HEADS)]
    o_heads_all = [[] for _ in range(N_HEADS)]
    for h in range(N_HEADS):
        lg = math.log(1.0 - 2.0 ** (-5.0 - h))
        decay = jnp.where(causal, jnp.exp(lg * rel), 0.0)
        q_dec = jnp.exp(lg * (idx + 1.0))
        k_dec = jnp.exp(lg * (c - 1.0 - idx))
        g_c = math.exp(lg * c)
        hs = slice(h * HEAD_DIM, (h + 1) * HEAD_DIM)
        s = states[h]
        for c0 in range(0, ct, c):
            sl = slice(c0, c0 + c)
            qh, kh, vh = q[sl, hs], k[sl, hs], v[sl, hs]
            s_in = _dotp(qh, kh, _NT, P_MISC) * decay
            o_heads_all[h].append(_dotp(s_in, vh, _NN, P_MISC) + _dotp(qh, s, _NN, P_MISC) * q_dec)
            s = s * g_c + _dotp(kh * k_dec, vh, _TN, P_MISC)
        states[h] = s
    cols = [oh[0] if len(oh) == 1 else jnp.concatenate(oh, axis=0) for oh in o_heads_all]
    o = jnp.concatenate(cols, axis=1)
    for h in range(N_HEADS):
        s_scr[h] = states[h]
    o_ref[...] = _head_rms_finish(o, z, _head_ones())

    @pl.when(j == last)
    def _():
        for h in range(N_HEADS):
            s_out_ref[h] = states[h]


def _rope_tables(pos):
    half = HEAD_DIM // 2
    inv = ROPE_BASE ** (-jnp.arange(half, dtype=F32) / half)
    ang = pos.astype(F32)[:, None] * inv[None, :]
    cos, sin = jnp.cos(ang), jnp.sin(ang)
    cos_t = jnp.tile(jnp.concatenate([cos, cos], axis=-1), (1, N_HEADS))
    sin_t = jnp.tile(jnp.concatenate([-sin, sin], axis=-1), (1, N_HEADS))
    return cos_t, sin_t


def _ret_call(p_b, n_seq, seq_len, cos_t, sin_t):
    ct = min(ROW_TILE, seq_len)
    p3 = p_b.reshape(n_seq, seq_len, B_W)
    o, s1 = pl.pallas_call(
        _ret_kernel,
        grid=(n_seq, seq_len // ct),
        in_specs=[
            pl.BlockSpec((None, ct, B_W), lambda b, j: (b, j, 0)),
            pl.BlockSpec((ct, W_MIX), lambda b, j: (j, 0)),
            pl.BlockSpec((ct, W_MIX), lambda b, j: (j, 0)),
        ],
        out_specs=[
            pl.BlockSpec((None, ct, W_MIX), lambda b, j: (b, j, 0)),
            pl.BlockSpec((None, N_HEADS, HEAD_DIM, HEAD_DIM), lambda b, j: (b, 0, 0, 0)),
        ],
        out_shape=[
            jax.ShapeDtypeStruct((n_seq, seq_len, W_MIX), F32),
            jax.ShapeDtypeStruct((n_seq, N_HEADS, HEAD_DIM, HEAD_DIM), F32),
        ],
        scratch_shapes=[pltpu.VMEM((N_HEADS, HEAD_DIM, HEAD_DIM), F32)],
        compiler_params=pltpu.CompilerParams(dimension_semantics=("parallel", "arbitrary"),
                                             vmem_limit_bytes=VMEM_LIMIT),
        name="retention_prompt",
    )(p3, cos_t, sin_t)
    return o.reshape(n_seq * seq_len, W_MIX), s1


def _lru_kernel(p_ref, cw_ref, cb_ref, gw_ref, gb_ref, sp_ref, o_ref, h_out_ref, conv_out_ref, ext_scr, h_scr):
    j = pl.program_id(1)
    last = pl.num_programs(1) - 1
    ct = p_ref.shape[0]

    @pl.when(j == 0)
    def _():
        h_scr[...] = jnp.zeros(h_scr.shape, F32)

    p = p_ref[...]
    xr = p[:, :W_MIX]
    z = p[:, W_MIX:]
    xc = _conv_tile(xr, ext_scr, cw_ref, j == 0) + cb_ref[...]
    a, b = _lru_token_math(xc, gw_ref[...], gb_ref[...], sp_ref[...])
    rows = _iota((ct, 1), 0)
    dist = 1
    while dist < ct:
        keep = rows >= dist
        a_prev = jnp.where(keep, pltpu.roll(a, dist, axis=0), 1.0)
        b_prev = jnp.where(keep, pltpu.roll(b, dist, axis=0), 0.0)
        b = a * b_prev + b
        a = a * a_prev
        dist *= 2
    hcur = a * h_scr[...] + b
    h_scr[...] = hcur[ct - 1:ct, :]
    o_ref[...] = hcur * _silu(z)

    @pl.when(j == last)
    def _():
        h_out_ref[...] = hcur[ct - 1:ct, :]
        conv_out_ref[...] = xr[ct - SUBLANES:ct, :]


def _block_diag_gates(gate_w):
    out = jnp.zeros((W_MIX, 2 * W_MIX), F32)
    for g in range(2):
        for n in range(N_HEADS):
            out = out.at[n * HEAD_DIM:(n + 1) * HEAD_DIM,
                         g * W_MIX + n * HEAD_DIM:g * W_MIX + (n + 1) * HEAD_DIM].set(gate_w[g, n])
    return out


def _lru_params(lp):
    return [lp["lru_conv_w"], _row(lp["lru_conv_b"]), _block_diag_gates(lp["lru_gate_w"]),
            _row(lp["lru_gate_b"]), _row(lp["lru_lambda"])]


def _lru_call(p_c, n_seq, seq_len, lp):
    ct = min(ROW_TILE, seq_len)
    p3 = p_c.reshape(n_seq, seq_len, C_W)
    params = _lru_params(lp)
    o, h1, conv_tail = pl.pallas_call(
        _lru_kernel,
        grid=(n_seq, seq_len // ct),
        in_specs=[pl.BlockSpec((None, ct, C_W), lambda b, j: (b, j, 0))]
        + [pl.BlockSpec(a.shape, lambda b, j: (0, 0)) for a in params],
        out_specs=[
            pl.BlockSpec((None, ct, W_MIX), lambda b, j: (b, j, 0)),
            pl.BlockSpec((None, 1, W_MIX), lambda b, j: (b, 0, 0)),
            pl.BlockSpec((None, SUBLANES, W_MIX), lambda b, j: (b, 0, 0)),
        ],
        out_shape=[
            jax.ShapeDtypeStruct((n_seq, seq_len, W_MIX), F32),
            jax.ShapeDtypeStruct((n_seq, 1, W_MIX), F32),
            jax.ShapeDtypeStruct((n_seq, SUBLANES, W_MIX), F32),
        ],
        scratch_shapes=[pltpu.VMEM((ct + SUBLANES, W_MIX), F32), pltpu.VMEM((1, W_MIX), F32)],
        compiler_params=pltpu.CompilerParams(dimension_semantics=("parallel", "arbitrary"),
                                             vmem_limit_bytes=VMEM_LIMIT),
        name="rglru_prompt",
    )(p3, *params)
    return (o.reshape(n_seq * seq_len, W_MIX), h1.reshape(n_seq, W_MIX),
            conv_tail[:, SUBLANES - (CONV_W - 1):, :])


def _gdn_kernel(p_ref, cw_ref, nal_ref, dtb_ref, ng_ref, o_ref, s_out_ref, conv_out_ref, ext_scr, s_scr):
    j = pl.program_id(1)
    last = pl.num_programs(1) - 1
    ct = p_ref.shape[0]

    @pl.when(j == 0)
    def _():
        s_scr[...] = jnp.zeros(s_scr.shape, F32)

    p = p_ref[...]
    raw = p[:, :D_QKV_W]
    z = p[:, D_QKV_W:D_QKV_W + W_MIX]
    b_raw = p[:, D_QKV_W + W_MIX:D_QKV_W + 2 * W_MIX]
    a_raw = p[:, D_QKV_W + 2 * W_MIX:]
    ones = _head_ones()
    qkv = _conv_tile(raw, ext_scr, cw_ref, j == 0)
    q, k, v, beta, g = _gdn_token_math(qkv, b_raw, a_raw, nal_ref[...], dtb_ref[...], ones)

    c = min(CHUNK, ct)
    ri, ci = _iota((c, c), 0), _iota((c, c), 1)
    lower = ri >= ci
    strict = ri > ci
    eye = (ri == ci).astype(F32)
    lt = lower.astype(BF16)
    head_mean = (ones.astype(F32) * (1.0 / HEAD_DIM)).astype(BF16)
    eye_h = (_iota((HEAD_DIM, HEAD_DIM), 0) == _iota((HEAD_DIM, HEAD_DIM), 1)).astype(F32)
    units = []
    for c0 in range(0, ct, c):
        sl = slice(c0, c0 + c)
        gc = _dot_const(g[sl], lt, _NN, 3, const_left=True)
        gj_all = _dot_const(gc, head_mean, _NT, 3, const_left=True)
        kb = k[sl] * beta[sl]
        vb = v[sl] * beta[sl]
        e_gc = jnp.exp(gc)
        g_last = gc[c - 1:c, :]
        k_out = k[sl] * jnp.exp(g_last - gc)
        q_in = q[sl] * e_gc
        kbe = kb * e_gc
        e_last = jnp.exp(g_last)
        for h in range(N_HEADS):
            hs = slice(h * HEAD_DIM, (h + 1) * HEAD_DIM)
            diff = gc[:, hs][:, :c] - gj_all[h * HEAD_DIM:h * HEAD_DIM + c, :]
            decay = jnp.where(lower, jnp.exp(jnp.where(lower, diff, 0.0)), 0.0)
            units.append(dict(decay=decay, kb=kb[:, hs], q=q[sl, hs], k=k[sl, hs], vb=vb[:, hs], kbe=kbe[:, hs],
                              k_out=k_out[:, hs], q_in=q_in[:, hs], e_last=e_last[:, hs]))
    misc = lambda x, y, dims=_NN: _dotp(x, y, dims, P_MISC)
    inv = lambda x, y, dims=_NN: _dotp(x, y, dims, P_INV)
    for u in units:
        kq = misc(jnp.concatenate([u["kb"], u["q"]], axis=0), u["k"], _NT)
        u["a_mat"] = jnp.where(strict, kq[:c] * u["decay"], 0.0)
        u["qk"] = kq[c:] * u["decay"]
    for u, t_inv in zip(units, _inv_unit_lower_many([u["a_mat"] for u in units])):
        u["t_inv"] = _Split(t_inv, P_INV)
    for u in units:
        u["u"] = inv(u["t_inv"], u["vb"])
        u["w"] = inv(u["t_inv"], u["kbe"])
    for u in units:
        u["g_mat"] = eye_h * u["e_last"] - misc(u["k_out"], u["w"], _TN)
        u["h_mat"] = misc(u["k_out"], u["u"], _TN)
        u["q_hat"] = u["q_in"] - misc(u["qk"], u["w"])
        u["o1"] = misc(u["qk"], u["u"])
    states = [s_scr[h] for h in range(N_HEADS)]
    o_rows = []
    for ci_ in range(ct // c):
        o_heads = []
        for h in range(N_HEADS):
            u = units[ci_ * N_HEADS + h]
            zz = _dotp(jnp.concatenate([u["q_hat"], u["g_mat"]], axis=0), states[h], _NN, P_STATE)
            o_heads.append(zz[:c] + u["o1"])
            states[h] = zz[c:] + u["h_mat"]
        o_rows.append(jnp.concatenate(o_heads, axis=1))
    o = o_rows[0] if len(o_rows) == 1 else jnp.concatenate(o_rows, axis=0)
    for h in range(N_HEADS):
        s_scr[h] = states[h]
    o_ref[...] = _head_rms_finish(o, z, ones, ng_ref[...])

    @pl.when(j == last)
    def _():
        for h in range(N_HEADS):
            s_out_ref[h] = states[h]
        conv_out_ref[...] = raw[ct - SUBLANES:ct, :]


def _gdn_params(lp):
    return [lp["gdn_conv_w"], _row(jnp.repeat(lp["gdn_A_log"], HEAD_DIM)),
            _row(jnp.repeat(lp["gdn_dt_bias"], HEAD_DIM)), _row(jnp.tile(lp["gdn_norm_g"], N_HEADS))]


def _gdn_call(p_d, n_seq, seq_len, lp):
    ct = min(ROW_TILE, seq_len)
    p3 = p_d.reshape(n_seq, seq_len, D_PACK_W)
    params = _gdn_params(lp)
    o, s1, conv_tail = pl.pallas_call(
        _gdn_kernel,
        grid=(n_seq, seq_len // ct),
        in_specs=[pl.BlockSpec((None, ct, D_PACK_W), lambda b, j: (b, j, 0))]
        + [pl.BlockSpec(a.shape, lambda b, j: (0, 0)) for a in params],
        out_specs=[
            pl.BlockSpec((None, ct, W_MIX), lambda b, j: (b, j, 0)),
            pl.BlockSpec((None, N_HEADS, HEAD_DIM, HEAD_DIM), lambda b, j: (b, 0, 0, 0)),
            pl.BlockSpec((None, SUBLANES, D_QKV_W), lambda b, j: (b, 0, 0)),
        ],
        out_shape=[
            jax.ShapeDtypeStruct((n_seq, seq_len, W_MIX), F32),
            jax.ShapeDtypeStruct((n_seq, N_HEADS, HEAD_DIM, HEAD_DIM), F32),
            jax.ShapeDtypeStruct((n_seq, SUBLANES, D_QKV_W), F32),
        ],
        scratch_shapes=[pltpu.VMEM((ct + SUBLANES, D_QKV_W), F32),
                        pltpu.VMEM((N_HEADS, HEAD_DIM, HEAD_DIM), F32)],
        compiler_params=pltpu.CompilerParams(dimension_semantics=("parallel", "arbitrary"),
                                             vmem_limit_bytes=VMEM_LIMIT),
        name="gdn_prompt",
    )(p3, *params)
    return o.reshape(n_seq * seq_len, W_MIX), s1, conv_tail[:, SUBLANES - (CONV_W - 1):, :]


def _decode_pre_kernel(pa_ref, pb_ref, pc_ref, pd_ref, shift_ref, h0_ref, lconv_ref, gconv_ref, cos_ref, sin_ref,
                       mu_ref, w0_ref, w2_ref, a0_ref, a2_ref, kk_ref, ka_ref,
                       lcw_ref, lcb_ref, lgw_ref, lgb_ref, lsp_ref, gcw_ref, nal_ref, dtb_ref,
                       vec_ref, oc_ref, h1_ref, lconv1_ref, gconv1_ref):
    ones = _head_ones()
    pa_full = pa_ref[...]
    pa = pa_full[:, :A_SHIFT_W]
    pm = pa + (shift_ref[...] - pa) * mu_ref[...]
    r, k, v, ld, av, bv = _rwkv_token_math(pm, w0_ref[...], w2_ref[...], a0_ref[...], a2_ref[...],
                                           kk_ref[...], ka_ref[...], ones)
    vecs = [r, jnp.exp(ld), k, v, av, bv, pa_full[:, A_SHIFT_W:]]
    pb = pb_ref[...]
    cos, sin = cos_ref[...], sin_ref[...]
    vecs += [_rotary(pb[:, 0:W_MIX], cos, sin), _rotary(pb[:, W_MIX:2 * W_MIX], cos, sin) * (HEAD_DIM ** -0.5),
             pb[:, 2 * W_MIX:3 * W_MIX], pb[:, 3 * W_MIX:]]
    pc = pc_ref[...]
    xr = pc[:, :W_MIX]
    lconv = lconv_ref[...]
    taps = [lconv[:, i * W_MIX:(i + 1) * W_MIX] for i in range(CONV_W - 1)] + [xr]
    xc = taps[0] * lcw_ref[0:1, :]
    for i in range(1, CONV_W):
        xc = xc + taps[i] * lcw_ref[i:i + 1, :]
    xc = xc + lcb_ref[...]
    a, b = _lru_token_math(xc, lgw_ref[...], lgb_ref[...], lsp_ref[...])
    hcur = a * h0_ref[...] + b
    oc_ref[...] = hcur * _silu(pc[:, W_MIX:])
    h1_ref[...] = hcur
    lconv1_ref[...] = jnp.concatenate(taps[1:], axis=1)
    pd = pd_ref[...]
    raw = pd[:, :D_QKV_W]
    gconv = gconv_ref[...]
    gtaps = [gconv[:, i * D_QKV_W:(i + 1) * D_QKV_W] for i in range(CONV_W - 1)] + [raw]
    qkv = gtaps[0] * gcw_ref[0:1, :]
    for i in range(1, CONV_W):
        qkv = qkv + gtaps[i] * gcw_ref[i:i + 1, :]
    q, kg, vg, beta, g = _gdn_token_math(qkv, pd[:, D_QKV_W + W_MIX:D_QKV_W + 2 * W_MIX],
                                         pd[:, D_QKV_W + 2 * W_MIX:], nal_ref[...], dtb_ref[...], ones)
    gconv1_ref[...] = jnp.concatenate(gtaps[1:], axis=1)
    vecs += [q, kg, vg, beta, g, pd[:, D_QKV_W:D_QKV_W + W_MIX]]
    assert len(vecs) == N_VEC
    for i, vec in enumerate(vecs):
        vec_ref[i] = vec


def _split3(x):
    hi = x.astype(BF16)
    r1 = x - hi.astype(F32)
    mid = r1.astype(BF16)
    lo = (r1 - mid.astype(F32)).astype(BF16)
    return hi, mid, lo


def _rep(x, e_rep):
    out = None
    for piece in _split3(x):
        t = jnp.dot(piece, e_rep, preferred_element_type=F32)
        out = t if out is None else out + t
    return out


def _tile(x):
    return jnp.tile(x, (1, HEAD_DIM))


def _red_minor(y, e_rep):
    out = None
    for piece in _split3(y):
        t = lax.dot_general(piece, e_rep, (((1,), (1,)), ((), ())), preferred_element_type=F32)
        out = t if out is None else out + t
    return out


def _red_major(y):
    acc = y[:, 0:LANES]
    for m in range(1, y.shape[1] // LANES):
        acc = acc + y[:, m * LANES:(m + 1) * LANES]
    return acc[:, :HEAD_DIM] + acc[:, HEAD_DIM:]


def _decode_state_kernel(vec_ref, wkv_ref, ret_ref, gdn_ref, rk_ref, lng_ref, lnb_ref, gam_ref, ng_ref, e_ref,
                         wkv1_ref, ret1_ref, gdn1_ref, o_ref):
    e_rep = e_ref[...]
    r, w, k, v, av, bv, z_a = (vec_ref[i] for i in range(7))
    s = wkv_ref[...]
    sa = _red_minor(s * _tile(av), e_rep)
    s = s * _tile(w) + _rep(sa, e_rep) * _tile(bv) + _rep(v, e_rep) * _tile(k)
    wkv1_ref[...] = s
    o = _red_minor(s * _tile(r), e_rep)
    mean = jnp.mean(o, axis=-1, keepdims=True)
    dlt = o - mean
    var = jnp.mean(dlt * dlt, axis=-1, keepdims=True)
    on = dlt * lax.rsqrt(var + RWKV_GN_EPS) * lng_ref[...] + lnb_ref[...]
    bonus = jnp.sum(r * k * rk_ref[...], axis=-1, keepdims=True) * v
    o_ref[0] = (on + bonus) * _silu(z_a)
    q, k, v, z_b = (vec_ref[i] for i in range(7, 11))
    s = ret_ref[...] * gam_ref[:, 0:1] + _rep(k, e_rep) * _tile(v)
    ret1_ref[...] = s
    o = _red_major(_rep(q, e_rep) * s)
    o_ref[1] = _rms(o) * _silu(z_b)
    q, k, v, beta, g, z_d = (vec_ref[i] for i in range(11, 17))
    beta1, eg = beta[:, 0:1], jnp.exp(g[:, 0:1])
    s = gdn_ref[...]
    k_rep = _rep(k, e_rep)
    v_new = v * beta1 - _red_major(k_rep * s) * (beta1 * eg)
    o = _red_major(_rep(q, e_rep) * s) * eg + jnp.sum(q * k, axis=-1, keepdims=True) * v_new
    gdn1_ref[...] = s * eg + k_rep * _tile(v_new)
    o_ref[2] = _rms(o) * ng_ref[...] * _silu(z_d)


def _decode_layer(p_a, p_b, p_c, p_d, states, cos_t, sin_t, lp):
    wkv0, shift0, ret0, lru_h0, lru_conv0, gdn0, gdn_conv0 = states
    n = p_a.shape[0]
    flat = HEAD_DIM * HEAD_DIM
    rwkv_params = [_row(lp["rwkv_mu"]), _row(lp["rwkv_w0"]), lp["rwkv_w2"], _row(lp["rwkv_a0"]), lp["rwkv_a2"],
                   _row(lp["rwkv_k_k"]), _row(lp["rwkv_k_a"])]
    gdn_params = _gdn_params(lp)
    ins = [p_a, p_b, p_c, p_d, shift0, lru_h0, lru_conv0.reshape(n, -1), gdn_conv0.reshape(n, -1), cos_t, sin_t,
           *rwkv_params, *_lru_params(lp), *gdn_params[:3]]
    full = lambda a: pl.BlockSpec(a.shape, lambda i: (0,) * a.ndim)
    out_shapes = [
        jax.ShapeDtypeStruct((N_VEC, n, W_MIX), F32),
        jax.ShapeDtypeStruct((n, W_MIX), F32),
        jax.ShapeDtypeStruct((n, W_MIX), F32),
        jax.ShapeDtypeStruct((n, (CONV_W - 1) * W_MIX), F32),
        jax.ShapeDtypeStruct((n, (CONV_W - 1) * D_QKV_W), F32),
    ]
    vecs, o_c, lru_h1, lru_conv1, gdn_conv1 = pl.pallas_call(
        _decode_pre_kernel,
        grid=(1,),
        in_specs=[full(a) for a in ins],
        out_specs=[pl.BlockSpec(s.shape, lambda i, nd=len(s.shape): (0,) * nd) for s in out_shapes],
        out_shape=out_shapes,
        compiler_params=pltpu.CompilerParams(dimension_semantics=("arbitrary",), vmem_limit_bytes=VMEM_LIMIT),
        name="decode_tokens",
    )(*ins)

    vec_h = vecs.reshape(N_VEC, n, N_HEADS, HEAD_DIM).transpose(2, 0, 1, 3)
    head_rows = lambda a: a.reshape(N_HEADS, 1, HEAD_DIM)
    gam = jnp.broadcast_to((1.0 - 2.0 ** (-5.0 - jnp.arange(N_HEADS, dtype=F32)))[:, None, None],
                           (N_HEADS, 1, HEAD_DIM))
    e_rep = (jnp.arange(HEAD_DIM)[:, None] == (jnp.arange(flat)[None, :] // HEAD_DIM)).astype(BF16)
    per_head = [head_rows(lp["rwkv_r_k"]), head_rows(lp["rwkv_ln_g"]), head_rows(lp["rwkv_ln_b"]), gam]
    state_spec = pl.BlockSpec((n, flat), lambda h: (0, h))
    wkv1, ret1, gdn1, o_h = pl.pallas_call(
        _decode_state_kernel,
        grid=(N_HEADS,),
        in_specs=[pl.BlockSpec((None, N_VEC, n, HEAD_DIM), lambda h: (h, 0, 0, 0)), state_spec, state_spec, state_spec]
        + [pl.BlockSpec((None, 1, HEAD_DIM), lambda h: (h, 0, 0)) for _ in per_head]
        + [pl.BlockSpec((1, HEAD_DIM), lambda h: (0, 0)), pl.BlockSpec((HEAD_DIM, flat), lambda h: (0, 0))],
        out_specs=[state_spec, state_spec, state_spec,
                   pl.BlockSpec((None, 3, n, HEAD_DIM), lambda h: (h, 0, 0, 0))],
        out_shape=[jax.ShapeDtypeStruct((n, N_HEADS * flat), F32)] * 3
        + [jax.ShapeDtypeStruct((N_HEADS, 3, n, HEAD_DIM), F32)],
        compiler_params=pltpu.CompilerParams(dimension_semantics=("parallel",), vmem_limit_bytes=VMEM_LIMIT),
        name="decode_states",
    )(vec_h, wkv0.reshape(n, -1), ret0.reshape(n, -1), gdn0.reshape(n, -1), *per_head,
      _row(lp["gdn_norm_g"]), e_rep)
    o3 = o_h.transpose(1, 2, 0, 3).reshape(3, n, W_MIX)
    shape4 = (n, N_HEADS, HEAD_DIM, HEAD_DIM)
    new_states = (wkv1.reshape(shape4), p_a[:, :A_SHIFT_W], ret1.reshape(shape4), lru_h1,
                  lru_conv1.reshape(n, CONV_W - 1, W_MIX), gdn1.reshape(shape4),
                  gdn_conv1.reshape(n, CONV_W - 1, D_QKV_W))
    return (o3[0], o3[1], o_c, o3[2]), new_states


def _pack_in_weights(w_in):
    off_d = A_W + B_W + C_W
    off_ba = off_d + D_QKV_W
    off_z = off_ba + 2 * N_HEADS
    off_g = off_d + D_W
    w_pack = jnp.concatenate([
        w_in[:, :off_ba],
        w_in[:, off_z:off_g],
        jnp.repeat(w_in[:, off_ba:off_ba + N_HEADS], HEAD_DIM, axis=1),
        jnp.repeat(w_in[:, off_ba + N_HEADS:off_z], HEAD_DIM, axis=1),
    ], axis=1).astype(BF16)
    return w_pack, w_in[:, off_g:].astype(BF16)


def _prompt_layer(p_a, p_b, p_c, p_d, n_seq, seq_len, cos_t, sin_t, lp):
    o_a, wkv1, shift1 = _rwkv_call(p_a, n_seq, seq_len, lp)
    o_b, ret1 = _ret_call(p_b, n_seq, seq_len, cos_t, sin_t)
    o_c, lru_h1, lru_conv1 = _lru_call(p_c, n_seq, seq_len, lp)
    o_d, gdn1, gdn_conv1 = _gdn_call(p_d, n_seq, seq_len, lp)
    return (o_a, o_b, o_c, o_d), (wkv1, shift1, ret1, lru_h1, lru_conv1, gdn1, gdn_conv1)


def _run_group(x, mods, pos, carried, layers, final_g):
    n_seq, seq_len, d = x.shape
    x2 = x.reshape(n_seq * seq_len, d)
    cos_t, sin_t = _rope_tables(pos)
    new = []
    n_layers = len(layers)
    for l, lp in enumerate(layers):
        shift, scale, gate = (mods[l][:, i * d:(i + 1) * d] for i in range(3))
        p_a, p_b, p_c, p_d = _inproj_call(x2, scale, shift, lp["norm_g"], lp["w_pack"], seq_len)
        if carried is None:
            branches, st = _prompt_layer(p_a, p_b, p_c, p_d, n_seq, seq_len, cos_t, sin_t, lp)
        else:
            branches, st = _decode_layer(p_a, p_b, p_c, p_d, carried[l], cos_t, sin_t, lp)
        new.append(st)
        x2 = _outproj_call(x2, scale, shift, gate, lp["norm_g"], branches, lp["w_gate"], lp["w_up_bf16"],
                           lp["w_out_bf16"], final_g, seq_len, final=(l == n_layers - 1))
    stacked = tuple(jnp.stack([s[i] for s in new], axis=0) for i in range(7))
    return x2.reshape(n_seq, seq_len, d), stacked


def kernel(x_prompt, x_sample, c_prompt, c_sample, state_rwkv_wkv, state_rwkv_shift, state_ret, state_lru_h, state_lru_conv, state_gdn, state_gdn_conv, ada_w, ada_b, norm_g, w_in, rwkv_mu, rwkv_w0, rwkv_w2, rwkv_a0, rwkv_a2, rwkv_k_k, rwkv_k_a, rwkv_r_k, rwkv_ln_g, rwkv_ln_b, lru_conv_w, lru_conv_b, lru_gate_w, lru_gate_b, lru_lambda, gdn_conv_w, gdn_A_log, gdn_dt_bias, gdn_norm_g, w_up, w_out, final_g):
    n_layers = ada_w.shape[0]
    n_prompt, seq_len, _ = x_prompt.shape
    n_sample, dec_len, _ = x_sample.shape
    assert dec_len == 1, "the decode path handles one token per sequence"
    layers = []
    for l in range(n_layers):
        w_pack, w_gate = _pack_in_weights(w_in[l])
        layers.append(dict(
            norm_g=norm_g[l], w_pack=w_pack, w_gate=w_gate, w_up_bf16=w_up[l].astype(BF16),
            w_out_bf16=w_out[l].astype(BF16),
            rwkv_mu=rwkv_mu[l], rwkv_w0=rwkv_w0[l], rwkv_w2=rwkv_w2[l], rwkv_a0=rwkv_a0[l], rwkv_a2=rwkv_a2[l],
            rwkv_k_k=rwkv_k_k[l], rwkv_k_a=rwkv_k_a[l], rwkv_r_k=rwkv_r_k[l], rwkv_ln_g=rwkv_ln_g[l],
            rwkv_ln_b=rwkv_ln_b[l], lru_conv_w=lru_conv_w[l], lru_conv_b=lru_conv_b[l],
            lru_gate_w=lru_gate_w[l], lru_gate_b=lru_gate_b[l], lru_lambda=lru_lambda[l],
            gdn_conv_w=gdn_conv_w[l], gdn_A_log=gdn_A_log[l], gdn_dt_bias=gdn_dt_bias[l],
            gdn_norm_g=gdn_norm_g[l]))
    mods = _ada_call(jnp.concatenate([c_prompt, c_sample], axis=0), ada_w, ada_b)
    mods_p = [mods[l, :n_prompt] for l in range(n_layers)]
    mods_s = [mods[l, n_prompt:] for l in range(n_layers)]

    y_prompt, new_p = _run_group(x_prompt, mods_p, jnp.arange(seq_len, dtype=jnp.int32), None, layers, final_g)
    carried = [(state_rwkv_wkv[l], state_rwkv_shift[l], state_ret[l], state_lru_h[l], state_lru_conv[l],
                state_gdn[l], state_gdn_conv[l]) for l in range(n_layers)]
    pos_s = PAST_LEN + jnp.arange(dec_len, dtype=jnp.int32)
    y_sample, new_s = _run_group(x_sample, mods_s, pos_s, carried, layers, final_g)
    return (y_prompt, y_sample) + new_p + new_s
```

```python
import functools
import math

import jax
import jax.numpy as jnp
from jax import lax
from jax.experimental import pallas as pl
from jax.experimental.pallas import tpu as pltpu

F32 = jnp.float32
BF16 = jnp.bfloat16
HI = lax.Precision.HIGHEST

N_HEADS = 4
HEAD_DIM = 64
W_MIX = N_HEADS * HEAD_DIM
LORA = 64
CONV_W = 4
N_BRANCH = 4
LRU_C = 8.0
ROPE_BASE = 10000.0
EPS = 1e-6
RWKV_GN_EPS = 64e-5
PAST_LEN = 16384
A_SHIFT_W = 3 * W_MIX + 2 * LORA
A_W = A_SHIFT_W + W_MIX
B_W = 4 * W_MIX
C_W = 2 * W_MIX
D_QKV_W = 3 * W_MIX
D_W = D_QKV_W + 2 * N_HEADS + W_MIX
D_PACK_W = D_QKV_W + 3 * W_MIX

SUBLANES = 8
LANES = 128
VMEM_LIMIT = 56 * 1024 * 1024

CHUNK = 64
RET_CHUNK = 128
INV_BLOCK = 16
ROW_TILE = 256
PROJ_TILE = 512
N_VEC = 17


def _mm(a, b, prec=HI):
    return lax.dot_general(a, b, (((1,), (0,)), ((), ())), precision=prec, preferred_element_type=F32)


def _mm_nt(a, b, prec=HI):
    return lax.dot_general(a, b, (((1,), (1,)), ((), ())), precision=prec, preferred_element_type=F32)


def _mm_tn(a, b, prec=HI):
    return lax.dot_general(a, b, (((0,), (0,)), ((), ())), precision=prec, preferred_element_type=F32)


_NN = (((1,), (0,)), ((), ()))
_NT = (((1,), (1,)), ((), ()))
_TN = (((0,), (0,)), ((), ()))

P_INV = 1
P_STATE = 1
P_MISC = 1


class _Split:
    def __init__(self, x, passes):
        self.hi = x.astype(BF16)
        self.lo = (x - self.hi.astype(F32)).astype(BF16) if passes > 1 else None


def _dotp(a, b, dims=_NN, passes=1):
    a = a if isinstance(a, _Split) else _Split(a, passes)
    b = b if isinstance(b, _Split) else _Split(b, passes)
    d = lambda x, y: lax.dot_general(x, y, dims, preferred_element_type=F32)
    out = d(a.hi, b.hi)
    if passes > 1:
        out = out + (d(a.hi, b.lo) + d(a.lo, b.hi))
    return out


def _iota(shape, dim):
    return lax.broadcasted_iota(jnp.int32, shape, dim)


def _silu(x):
    return x * jax.nn.sigmoid(x)


def _softplus(x):
    return jnp.maximum(x, 0.0) + jnp.log1p(jnp.exp(-jnp.abs(x)))


def _pieces(x, n):
    out = []
    for i in range(n):
        p = x.astype(BF16)
        out.append(p)
        if i + 1 < n:
            x = x - p.astype(F32)
    return out


def _dot_const(x, const, dims=_NN, n=2, const_left=False):
    out = None
    for p in _pieces(x, n):
        t = lax.dot_general(*((const, p) if const_left else (p, const)), dims, preferred_element_type=F32)
        out = t if out is None else out + t
    return out


def _head_ones():
    return (_iota((W_MIX, W_MIX), 0) // HEAD_DIM == _iota((W_MIX, W_MIX), 1) // HEAD_DIM).astype(BF16)


def _head_sum(x, ones):
    return _dot_const(x, ones)


def _rms(x):
    return x * lax.rsqrt(jnp.mean(x * x, axis=-1, keepdims=True) + EPS)


def _inv_unit_lower(a):
    return _inv_unit_lower_many([a])[0]


def _inv_unit_lower_many(mats):
    n = mats[0].shape[0]
    ri, ci = _iota((n, n), 0), _iota((n, n), 1)
    eye = (ri == ci).astype(F32)
    diag_blk = (ri // INV_BLOCK) == (ci // INV_BLOCK)
    mm = lambda x, y: _dotp(x, y, _NN, P_INV)
    sp = lambda x: _Split(x, P_INV)
    d = [jnp.where(diag_blk, a, 0.0) for a in mats]
    nb = [a - di for a, di in zip(mats, d)]
    td = [eye - di for di in d]
    p = d
    for _ in range(int(math.log2(INV_BLOCK)) - 1):
        ps = [sp(pi) for pi in p]
        p = [mm(pi, pi) for pi in ps]
        td = [mm(ti, eye + pi) for ti, pi in zip(td, p)]
    tds = [sp(ti) for ti in td]
    x = [mm(ti, ni) for ti, ni in zip(tds, nb)]
    t = [eye - xi for xi in x]
    p = x
    for _ in range(int(math.log2(n // INV_BLOCK)) - 1):
        ps = [sp(pi) for pi in p]
        p = [mm(pi, pi) for pi in ps]
        t = [mm(ti, eye + pi) for ti, pi in zip(t, p)]
    return [mm(ti, tdi) for ti, tdi in zip(t, tds)]


def _ada_kernel(c_ref, w_ref, b_ref, o_ref):
    o_ref[...] = _mm(_silu(c_ref[...]), w_ref[...]) + b_ref[...]


def _ada_call(c_all, ada_w, ada_b):
    n_layers, d, d3 = ada_w.shape
    rows = c_all.shape[0]
    return pl.pallas_call(
        _ada_kernel,
        grid=(n_layers, d3 // d),
        in_specs=[
            pl.BlockSpec((rows, d), lambda l, j: (0, 0)),
            pl.BlockSpec((None, d, d), lambda l, j: (l, 0, j)),
            pl.BlockSpec((None, 1, d), lambda l, j: (l, 0, j)),
        ],
        out_specs=pl.BlockSpec((None, rows, d), lambda l, j: (l, 0, j)),
        out_shape=jax.ShapeDtypeStruct((n_layers, rows, d3), F32),
        compiler_params=pltpu.CompilerParams(dimension_semantics=("arbitrary", "arbitrary"),
                                             vmem_limit_bytes=VMEM_LIMIT),
        name="ada_mod",
    )(c_all, ada_w, ada_b.reshape(n_layers, 1, d3))


def _modulated_norm(x, g, scale, shift):
    return _rms(x) * g * (1.0 + scale) + shift


def _inproj_kernel(x_ref, sc_ref, sh_ref, g_ref, w_ref, oa_ref, ob_ref, oc_ref, od_ref):
    h = _modulated_norm(x_ref[...], g_ref[...], sc_ref[...], sh_ref[...]).astype(BF16)
    lo = 0
    for o_ref in (oa_ref, ob_ref, oc_ref, od_ref):
        wd = o_ref.shape[-1]
        o_ref[...] = jnp.dot(h, w_ref[:, lo:lo + wd], preferred_element_type=F32)
        lo += wd


def _mod_specs(mods, tm, seq_len):
    d = mods[0].shape[-1]
    if seq_len == 1:
        return [m for m in mods], [pl.BlockSpec((tm, d), lambda i: (i, 0)) for _ in mods]
    per_seq = seq_len // tm
    return ([m.reshape(m.shape[0], 1, d) for m in mods],
            [pl.BlockSpec((None, 1, d), lambda i: (i // per_seq, 0, 0)) for _ in mods])


def _inproj_call(x2, scale, shift, g, w_pack, seq_len):
    m, d = x2.shape
    tm = min(PROJ_TILE, m, seq_len) if seq_len > 1 else m
    widths = (A_W, B_W, C_W, D_PACK_W)
    mods, mod_specs = _mod_specs((scale, shift), tm, seq_len)
    return pl.pallas_call(
        _inproj_kernel,
        grid=(m // tm,),
        in_specs=[pl.BlockSpec((tm, d), lambda i: (i, 0))] + mod_specs + [
            pl.BlockSpec((1, d), lambda i: (0, 0)),
            pl.BlockSpec(w_pack.shape, lambda i: (0, 0)),
        ],
        out_specs=[pl.BlockSpec((tm, wd), lambda i: (i, 0)) for wd in widths],
        out_shape=[jax.ShapeDtypeStruct((m, wd), F32) for wd in widths],
        compiler_params=pltpu.CompilerParams(dimension_semantics=("parallel",), vmem_limit_bytes=VMEM_LIMIT),
        name="in_proj",
    )(x2, *mods, g.reshape(1, d), w_pack)


def _outproj_kernel(x_ref, sc_ref, sh_ref, gt_ref, g_ref, ba_ref, bb_ref, bc_ref, bd_ref,
                    wg_ref, wup_ref, wout_ref, fg_ref, o_ref, *, final):
    x = x_ref[...]
    d = x.shape[-1]
    h = _modulated_norm(x, g_ref[...], sc_ref[...], sh_ref[...]).astype(BF16)
    merged = jnp.zeros(x.shape, F32)
    for n, br_ref in enumerate((ba_ref, bb_ref, bc_ref, bd_ref)):
        gl = jnp.dot(h, wg_ref[:, n * d:(n + 1) * d], preferred_element_type=F32)
        up = jnp.dot(br_ref[...].astype(BF16), wup_ref[n], preferred_element_type=F32)
        merged = merged + jax.nn.sigmoid(gl) * up
    out = jnp.dot(merged.astype(BF16), wout_ref[...], preferred_element_type=F32)
    xn = x + gt_ref[...] * out
    if final:
        xn = _rms(xn) * fg_ref[...]
    o_ref[...] = xn


def _outproj_call(x2, scale, shift, gate, g, branches, wg, wup, wout, final_g, seq_len, final):
    m, d = x2.shape
    tm = min(PROJ_TILE, m, seq_len) if seq_len > 1 else m
    mods, mod_specs = _mod_specs((scale, shift, gate), tm, seq_len)
    full = lambda a: pl.BlockSpec(a.shape, lambda i: (0,) * a.ndim)
    return pl.pallas_call(
        functools.partial(_outproj_kernel, final=final),
        grid=(m // tm,),
        in_specs=[pl.BlockSpec((tm, d), lambda i: (i, 0))] + mod_specs + [pl.BlockSpec((1, d), lambda i: (0, 0))]
        + [pl.BlockSpec((tm, W_MIX), lambda i: (i, 0)) for _ in branches]
        + [full(wg), full(wup), full(wout), pl.BlockSpec((1, d), lambda i: (0, 0))],
        out_specs=pl.BlockSpec((tm, d), lambda i: (i, 0)),
        out_shape=jax.ShapeDtypeStruct((m, d), F32),
        compiler_params=pltpu.CompilerParams(dimension_semantics=("parallel",), vmem_limit_bytes=VMEM_LIMIT),
        name="out_proj",
    )(x2, *mods, g.reshape(1, d), *branches, wg, wup, wout, final_g.reshape(1, d))


def _rwkv_token_math(pm, w0, w2, a0, a2, k_k, k_a, ones):
    r = pm[:, 0:W_MIX]
    k = pm[:, W_MIX:2 * W_MIX]
    v = pm[:, 2 * W_MIX:3 * W_MIX]
    wd = pm[:, 3 * W_MIX:3 * W_MIX + LORA]
    ad = pm[:, 3 * W_MIX + LORA:]
    w_log = -_softplus(-(w0 + _dotp(jnp.tanh(wd), w2, _NN, P_MISC))) - 0.5
    log_decay = -jnp.exp(w_log)
    a = jax.nn.sigmoid(a0 + _dotp(ad, a2, _NN, P_MISC))
    kx = k * k_k
    kk = kx * lax.rsqrt(_head_sum(kx * kx, ones) + EPS)
    k = k * (1.0 + (a - 1.0) * k_a)
    return r, k, v, log_decay, -kk, kk * a


def _rwkv_finish(o, r, k, v, z, r_k, ln_g, ln_b, ones):
    mean = _head_sum(o, ones) * (1.0 / HEAD_DIM)
    dlt = o - mean
    var = _head_sum(dlt * dlt, ones) * (1.0 / HEAD_DIM)
    on = dlt * lax.rsqrt(var + RWKV_GN_EPS) * ln_g + ln_b
    bonus = _head_sum(r * k * r_k, ones) * v
    return (on + bonus) * _silu(z)


def _swap_halves(x):
    half = HEAD_DIM // 2
    n = x.shape[-1]
    first = (_iota(x.shape, 1) & half) == 0
    return jnp.where(first, pltpu.roll(x, n - half, axis=1), pltpu.roll(x, half, axis=1))


def _rotary(x, cos, sin):
    return x * cos + _swap_halves(x) * sin


def _lru_token_math(xc, gate_w, gate_b, lam):
    gates = _dotp(xc, gate_w, _NN, P_MISC) + gate_b
    r_gate = jax.nn.sigmoid(gates[:, :W_MIX])
    i_gate = jax.nn.sigmoid(gates[:, W_MIX:])
    log_a = -LRU_C * r_gate * _softplus(-lam)
    a = jnp.exp(log_a)
    b = jnp.sqrt(1.0 - jnp.exp(2.0 * log_a)) * (i_gate * xc)
    return a, b


def _gdn_token_math(qkv, b_raw, a_raw, a_log, dt_bias, ones):
    qkv = _silu(qkv)
    q = qkv[:, 0:W_MIX]
    k = qkv[:, W_MIX:2 * W_MIX]
    v = qkv[:, 2 * W_MIX:]
    q = q * lax.rsqrt(_head_sum(q * q, ones) + EPS) * (HEAD_DIM ** -0.5)
    k = k * lax.rsqrt(_head_sum(k * k, ones) + EPS)
    beta = jax.nn.sigmoid(b_raw)
    g = -jnp.exp(a_log) * _softplus(a_raw + dt_bias)
    return q, k, v, beta, g


def _head_rms_finish(o, z, ones, gain=None):
    y = o * lax.rsqrt(_head_sum(o * o, ones) * (1.0 / HEAD_DIM) + EPS)
    if gain is not None:
        y = y * gain
    return y * _silu(z)


def _conv_tile(u, ext_ref, w_ref, first):
    n = u.shape[0]

    @pl.when(first)
    def _():
        ext_ref[0:SUBLANES, :] = jnp.zeros((SUBLANES, u.shape[1]), F32)

    ext_ref[SUBLANES:SUBLANES + n, :] = u
    out = None
    for j in range(CONV_W):
        back = CONV_W - 1 - j
        term = ext_ref[SUBLANES - back:SUBLANES - back + n, :] * w_ref[j:j + 1, :]
        out = term if out is None else out + term
    ext_ref[0:SUBLANES, :] = u[n - SUBLANES:n, :]
    return out


def _rwkv_kernel(p_ref, mu_ref, w0_ref, w2_ref, a0_ref, a2_ref, kk_ref, ka_ref, rk_ref, lng_ref, lnb_ref,
                 o_ref, s_out_ref, shift_out_ref, s_scr, prev_scr):
    j = pl.program_id(1)
    last = pl.num_programs(1) - 1
    ct = p_ref.shape[0]

    @pl.when(j == 0)
    def _():
        s_scr[...] = jnp.zeros(s_scr.shape, F32)
        prev_scr[...] = jnp.zeros(prev_scr.shape, F32)

    p = p_ref[...]
    pa = p[:, :A_SHIFT_W]
    z = p[:, A_SHIFT_W:]
    rows = _iota((ct, 1), 0)
    prev = jnp.where(rows == 0, prev_scr[...], pltpu.roll(pa, 1, axis=0))
    prev_scr[...] = pa[ct - 1:ct, :]
    pm = pa + (prev - pa) * mu_ref[...]
    ones = _head_ones()
    r, k, v, ld, av, bv = _rwkv_token_math(pm, w0_ref[...], w2_ref[...], a0_ref[...], a2_ref[...],
                                           kk_ref[...], ka_ref[...], ones)

    c = min(CHUNK, ct)
    ri, ci = _iota((c, c), 0), _iota((c, c), 1)
    strict = ri > ci
    incl = ri >= ci
    eye = (ri == ci).astype(F32)
    lt = incl.astype(BF16)
    units = []
    for c0 in range(0, ct, c):
        sl = slice(c0, c0 + c)
        ldc = ld[sl]
        cum = _dot_const(ldc, lt, _NN, 3, const_left=True)
        e_neg = jnp.exp(-cum)
        e_out = jnp.exp(cum[c - 1:c, :] - cum)
        at = av[sl] * jnp.exp(cum - ldc)
        rt = r[sl] * jnp.exp(cum)
        bt, kt = bv[sl] * e_neg, k[sl] * e_neg
        bo, ko = bv[sl] * e_out, k[sl] * e_out
        g_end = jnp.exp(cum[c - 1:c, :])
        vc = v[sl]
        for h in range(N_HEADS):
            hs = slice(h * HEAD_DIM, (h + 1) * HEAD_DIM)
            units.append(dict(a=at[:, hs], r=rt[:, hs], b=bt[:, hs], k=kt[:, hs], bo=bo[:, hs], ko=ko[:, hs],
                              v=vc[:, hs], g=g_end[:, hs]))
    misc = lambda x, y, dims=_NN: _dotp(x, y, dims, P_MISC)
    inv = lambda x, y, dims=_NN: _dotp(x, y, dims, P_INV)
    for u in units:
        lhs = jnp.concatenate([u["a"], u["r"]], axis=0)
        u["mb"] = misc(lhs, u["b"], _NT)
        u["mk"] = misc(lhs, u["k"], _NT)
    for u in units:
        u["m_ab"] = jnp.where(strict, u["mb"][:c], 0.0)
        u["m_ak"] = jnp.where(strict, u["mk"][:c], 0.0)
        u["m_rb"] = jnp.where(incl, u["mb"][c:], 0.0)
        u["m_rk"] = jnp.where(incl, u["mk"][c:], 0.0)
    for u, t_inv in zip(units, _inv_unit_lower_many([-u["m_ab"] for u in units])):
        u["t_inv"] = _Split(t_inv, P_INV)
    for u in units:
        u["makv"] = misc(u["m_ak"], u["v"])
    for u in units:
        u["a_hat"] = inv(u["t_inv"], u["a"])
        u["u1"] = inv(u["t_inv"], u["makv"])
    for u in units:
        u["r_hat"] = u["r"] + misc(u["m_rb"], u["a_hat"])
        u["o1"] = misc(u["m_rb"], u["u1"]) + misc(u["m_rk"], u["v"])
        u["g_t"] = eye * u["g"] + misc(u["bo"], u["a_hat"], _TN)
        u["h_t"] = misc(u["bo"], u["u1"], _TN) + misc(u["ko"], u["v"], _TN)
    states = [s_scr[h] for h in range(N_HEADS)]
    o_rows = []
    for ci_ in range(ct // c):
        o_heads = []
        for h in range(N_HEADS):
            u = units[ci_ * N_HEADS + h]
            zz = _dotp(jnp.concatenate([u["r_hat"], u["g_t"]], axis=0), states[h], _NN, P_STATE)
            o_heads.append(zz[:c] + u["o1"])
            states[h] = zz[c:] + u["h_t"]
        o_rows.append(jnp.concatenate(o_heads, axis=1))
    o = o_rows[0] if len(o_rows) == 1 else jnp.concatenate(o_rows, axis=0)
    for h in range(N_HEADS):
        s_scr[h] = states[h]
    o_ref[...] = _rwkv_finish(o, r, k, v, z, rk_ref[...], lng_ref[...], lnb_ref[...], ones)

    @pl.when(j == last)
    def _():
        eye_h = (_iota((HEAD_DIM, HEAD_DIM), 0) == _iota((HEAD_DIM, HEAD_DIM), 1)).astype(F32)
        for h in range(N_HEADS):
            s_out_ref[h] = _mm_nt(eye_h, states[h])
        shift_out_ref[...] = pa[ct - 1:ct, :]


def _row(a):
    return a.reshape(1, -1)


def _rwkv_call(p_a, n_seq, seq_len, lp):
    ct = min(ROW_TILE, seq_len)
    p3 = p_a.reshape(n_seq, seq_len, A_W)
    params = [_row(lp["rwkv_mu"]), _row(lp["rwkv_w0"]), lp["rwkv_w2"], _row(lp["rwkv_a0"]), lp["rwkv_a2"],
              _row(lp["rwkv_k_k"]), _row(lp["rwkv_k_a"]), _row(lp["rwkv_r_k"]), _row(lp["rwkv_ln_g"]),
              _row(lp["rwkv_ln_b"])]
    o, s1, shift1 = pl.pallas_call(
        _rwkv_kernel,
        grid=(n_seq, seq_len // ct),
        in_specs=[pl.BlockSpec((None, ct, A_W), lambda b, j: (b, j, 0))]
        + [pl.BlockSpec(a.shape, lambda b, j: (0, 0)) for a in params],
        out_specs=[
            pl.BlockSpec((None, ct, W_MIX), lambda b, j: (b, j, 0)),
            pl.BlockSpec((None, N_HEADS, HEAD_DIM, HEAD_DIM), lambda b, j: (b, 0, 0, 0)),
            pl.BlockSpec((None, 1, A_SHIFT_W), lambda b, j: (b, 0, 0)),
        ],
        out_shape=[
            jax.ShapeDtypeStruct((n_seq, seq_len, W_MIX), F32),
            jax.ShapeDtypeStruct((n_seq, N_HEADS, HEAD_DIM, HEAD_DIM), F32),
            jax.ShapeDtypeStruct((n_seq, 1, A_SHIFT_W), F32),
        ],
        scratch_shapes=[pltpu.VMEM((N_HEADS, HEAD_DIM, HEAD_DIM), F32), pltpu.VMEM((1, A_SHIFT_W), F32)],
        compiler_params=pltpu.CompilerParams(dimension_semantics=("parallel", "arbitrary"),
                                             vmem_limit_bytes=VMEM_LIMIT),
        name="rwkv7_prompt",
    )(p3, *params)
    return o.reshape(n_seq * seq_len, W_MIX), s1, shift1.reshape(n_seq, A_SHIFT_W)


def _ret_kernel(p_ref, cos_ref, sin_ref, o_ref, s_out_ref, s_scr):
    j = pl.program_id(1)
    last = pl.num_programs(1) - 1
    ct = p_ref.shape[0]

    @pl.when(j == 0)
    def _():
        s_scr[...] = jnp.zeros(s_scr.shape, F32)

    p = p_ref[...]
    cos, sin = cos_ref[...], sin_ref[...]
    q = _rotary(p[:, 0:W_MIX], cos, sin)
    k = _rotary(p[:, W_MIX:2 * W_MIX], cos, sin) * (HEAD_DIM ** -0.5)
    v = p[:, 2 * W_MIX:3 * W_MIX]
    z = p[:, 3 * W_MIX:]
    c = min(RET_CHUNK, ct)
    ri, ci = _iota((c, c), 0), _iota((c, c), 1)
    causal = ri >= ci
    rel = jnp.where(causal, ri - ci, 0).astype(F32)
    idx = _iota((c, 1), 0).astype(F32)
    states = [s_scr[h] for h in range(N_HEADS)]
    o_heads_all = [[] for _ in range(N_HEADS)]
    for h in range(N_HEADS):
        lg = math.log(1.0 - 2.0 ** (-5.0 - h))
        decay = jnp.where(causal, jnp.exp(lg * rel), 0.0)
        q_dec = jnp.exp(lg * (idx + 1.0))
        k_dec = jnp.exp(lg * (c - 1.0 - idx))
        g_c = math.exp(lg * c)
        hs = slice(h * HEAD_DIM, (h + 1) * HEAD_DIM)
        s = states[h]
        for c0 in range(0, ct, c):
            sl = slice(c0, c0 + c)
            qh, kh, vh = q[sl, hs], k[sl, hs], v[sl, hs]
            s_in = _dotp(qh, kh, _NT, P_MISC) * decay
            o_heads_all[h].append(_dotp(s_in, vh, _NN, P_MISC) + _dotp(qh, s, _NN, P_MISC) * q_dec)
            s = s * g_c + _dotp(kh * k_dec, vh, _TN, P_MISC)
        states[h] = s
    cols = [oh[0] if len(oh) == 1 else jnp.concatenate(oh, axis=0) for oh in o_heads_all]
    o = jnp.concatenate(cols, axis=1)
    for h in range(N_HEADS):
        s_scr[h] = states[h]
    o_ref[...] = _head_rms_finish(o, z, _head_ones())

    @pl.when(j == last)
    def _():
        for h in range(N_HEADS):
            s_out_ref[h] = states[h]


def _rope_tables(pos):
    half = HEAD_DIM // 2
    inv = ROPE_BASE ** (-jnp.arange(half, dtype=F32) / half)
    ang = pos.astype(F32)[:, None] * inv[None, :]
    cos, sin = jnp.cos(ang), jnp.sin(ang)
    cos_t = jnp.tile(jnp.concatenate([cos, cos], axis=-1), (1, N_HEADS))
    sin_t = jnp.tile(jnp.concatenate([-sin, sin], axis=-1), (1, N_HEADS))
    return cos_t, sin_t


def _ret_call(p_b, n_seq, seq_len, cos_t, sin_t):
    ct = min(ROW_TILE, seq_len)
    p3 = p_b.reshape(n_seq, seq_len, B_W)
    o, s1 = pl.pallas_call(
        _ret_kernel,
        grid=(n_seq, seq_len // ct),
        in_specs=[
            pl.BlockSpec((None, ct, B_W), lambda b, j: (b, j, 0)),
            pl.BlockSpec((ct, W_MIX), lambda b, j: (j, 0)),
            pl.BlockSpec((ct, W_MIX), lambda b, j: (j, 0)),
        ],
        out_specs=[
            pl.BlockSpec((None, ct, W_MIX), lambda b, j: (b, j, 0)),
            pl.BlockSpec((None, N_HEADS, HEAD_DIM, HEAD_DIM), lambda b, j: (b, 0, 0, 0)),
        ],
        out_shape=[
            jax.ShapeDtypeStruct((n_seq, seq_len, W_MIX), F32),
            jax.ShapeDtypeStruct((n_seq, N_HEADS, HEAD_DIM, HEAD_DIM), F32),
        ],
        scratch_shapes=[pltpu.VMEM((N_HEADS, HEAD_DIM, HEAD_DIM), F32)],
        compiler_params=pltpu.CompilerParams(dimension_semantics=("parallel", "arbitrary"),
                                             vmem_limit_bytes=VMEM_LIMIT),
        name="retention_prompt",
    )(p3, cos_t, sin_t)
    return o.reshape(n_seq * seq_len, W_MIX), s1


def _lru_kernel(p_ref, cw_ref, cb_ref, gw_ref, gb_ref, sp_ref, o_ref, h_out_ref, conv_out_ref, ext_scr, h_scr):
    j = pl.program_id(1)
    last = pl.num_programs(1) - 1
    ct = p_ref.shape[0]

    @pl.when(j == 0)
    def _():
        h_scr[...] = jnp.zeros(h_scr.shape, F32)

    p = p_ref[...]
    xr = p[:, :W_MIX]
    z = p[:, W_MIX:]
    xc = _conv_tile(xr, ext_scr, cw_ref, j == 0) + cb_ref[...]
    a, b = _lru_token_math(xc, gw_ref[...], gb_ref[...], sp_ref[...])
    rows = _iota((ct, 1), 0)
    dist = 1
    while dist < ct:
        keep = rows >= dist
        a_prev = jnp.where(keep, pltpu.roll(a, dist, axis=0), 1.0)
        b_prev = jnp.where(keep, pltpu.roll(b, dist, axis=0), 0.0)
        b = a * b_prev + b
        a = a * a_prev
        dist *= 2
    hcur = a * h_scr[...] + b
    h_scr[...] = hcur[ct - 1:ct, :]
    o_ref[...] = hcur * _silu(z)

    @pl.when(j == last)
    def _():
        h_out_ref[...] = hcur[ct - 1:ct, :]
        conv_out_ref[...] = xr[ct - SUBLANES:ct, :]


def _block_diag_gates(gate_w):
    out = jnp.zeros((W_MIX, 2 * W_MIX), F32)
    for g in range(2):
        for n in range(N_HEADS):
            out = out.at[n * HEAD_DIM:(n + 1) * HEAD_DIM,
                         g * W_MIX + n * HEAD_DIM:g * W_MIX + (n + 1) * HEAD_DIM].set(gate_w[g, n])
    return out


def _lru_params(lp):
    return [lp["lru_conv_w"], _row(lp["lru_conv_b"]), _block_diag_gates(lp["lru_gate_w"]),
            _row(lp["lru_gate_b"]), _row(lp["lru_lambda"])]


def _lru_call(p_c, n_seq, seq_len, lp):
    ct = min(ROW_TILE, seq_len)
    p3 = p_c.reshape(n_seq, seq_len, C_W)
    params = _lru_params(lp)
    o, h1, conv_tail = pl.pallas_call(
        _lru_kernel,
        grid=(n_seq, seq_len // ct),
        in_specs=[pl.BlockSpec((None, ct, C_W), lambda b, j: (b, j, 0))]
        + [pl.BlockSpec(a.shape, lambda b, j: (0, 0)) for a in params],
        out_specs=[
            pl.BlockSpec((None, ct, W_MIX), lambda b, j: (b, j, 0)),
            pl.BlockSpec((None, 1, W_MIX), lambda b, j: (b, 0, 0)),
            pl.BlockSpec((None, SUBLANES, W_MIX), lambda b, j: (b, 0, 0)),
        ],
        out_shape=[
            jax.ShapeDtypeStruct((n_seq, seq_len, W_MIX), F32),
            jax.ShapeDtypeStruct((n_seq, 1, W_MIX), F32),
            jax.ShapeDtypeStruct((n_seq, SUBLANES, W_MIX), F32),
        ],
        scratch_shapes=[pltpu.VMEM((ct + SUBLANES, W_MIX), F32), pltpu.VMEM((1, W_MIX), F32)],
        compiler_params=pltpu.CompilerParams(dimension_semantics=("parallel", "arbitrary"),
                                             vmem_limit_bytes=VMEM_LIMIT),
        name="rglru_prompt",
    )(p3, *params)
    return (o.reshape(n_seq * seq_len, W_MIX), h1.reshape(n_seq, W_MIX),
            conv_tail[:, SUBLANES - (CONV_W - 1):, :])


def _gdn_kernel(p_ref, cw_ref, nal_ref, dtb_ref, ng_ref, o_ref, s_out_ref, conv_out_ref, ext_scr, s_scr):
    j = pl.program_id(1)
    last = pl.num_programs(1) - 1
    ct = p_ref.shape[0]

    @pl.when(j == 0)
    def _():
        s_scr[...] = jnp.zeros(s_scr.shape, F32)

    p = p_ref[...]
    raw = p[:, :D_QKV_W]
    z = p[:, D_QKV_W:D_QKV_W + W_MIX]
    b_raw = p[:, D_QKV_W + W_MIX:D_QKV_W + 2 * W_MIX]
    a_raw = p[:, D_QKV_W + 2 * W_MIX:]
    ones = _head_ones()
    qkv = _conv_tile(raw, ext_scr, cw_ref, j == 0)
    q, k, v, beta, g = _gdn_token_math(qkv, b_raw, a_raw, nal_ref[...], dtb_ref[...], ones)

    c = min(CHUNK, ct)
    ri, ci = _iota((c, c), 0), _iota((c, c), 1)
    lower = ri >= ci
    strict = ri > ci
    eye = (ri == ci).astype(F32)
    lt = lower.astype(BF16)
    head_mean = (ones.astype(F32) * (1.0 / HEAD_DIM)).astype(BF16)
    eye_h = (_iota((HEAD_DIM, HEAD_DIM), 0) == _iota((HEAD_DIM, HEAD_DIM), 1)).astype(F32)
    units = []
    for c0 in range(0, ct, c):
        sl = slice(c0, c0 + c)
        gc = _dot_const(g[sl], lt, _NN, 3, const_left=True)
        gj_all = _dot_const(gc, head_mean, _NT, 3, const_left=True)
        kb = k[sl] * beta[sl]
        vb = v[sl] * beta[sl]
        e_gc = jnp.exp(gc)
        g_last = gc[c - 1:c, :]
        k_out = k[sl] * jnp.exp(g_last - gc)
        q_in = q[sl] * e_gc
        kbe = kb * e_gc
        e_last = jnp.exp(g_last)
        for h in range(N_HEADS):
            hs = slice(h * HEAD_DIM, (h + 1) * HEAD_DIM)
            diff = gc[:, hs][:, :c] - gj_all[h * HEAD_DIM:h * HEAD_DIM + c, :]
            decay = jnp.where(lower, jnp.exp(jnp.where(lower, diff, 0.0)), 0.0)
            units.append(dict(decay=decay, kb=kb[:, hs], q=q[sl, hs], k=k[sl, hs], vb=vb[:, hs], kbe=kbe[:, hs],
                              k_out=k_out[:, hs], q_in=q_in[:, hs], e_last=e_last[:, hs]))
    misc = lambda x, y, dims=_NN: _dotp(x, y, dims, P_MISC)
    inv = lambda x, y, dims=_NN: _dotp(x, y, dims, P_INV)
    for u in units:
        kq = misc(jnp.concatenate([u["kb"], u["q"]], axis=0), u["k"], _NT)
        u["a_mat"] = jnp.where(strict, kq[:c] * u["decay"], 0.0)
        u["qk"] = kq[c:] * u["decay"]
    for u, t_inv in zip(units, _inv_unit_lower_many([u["a_mat"] for u in units])):
        u["t_inv"] = _Split(t_inv, P_INV)
    for u in units:
        u["u"] = inv(u["t_inv"], u["vb"])
        u["w"] = inv(u["t_inv"], u["kbe"])
    for u in units:
        u["g_mat"] = eye_h * u["e_last"] - misc(u["k_out"], u["w"], _TN)
        u["h_mat"] = misc(u["k_out"], u["u"], _TN)
        u["q_hat"] = u["q_in"] - misc(u["qk"], u["w"])
        u["o1"] = misc(u["qk"], u["u"])
    states = [s_scr[h] for h in range(N_HEADS)]
    o_rows = []
    for ci_ in range(ct // c):
        o_heads = []
        for h in range(N_HEADS):
            u = units[ci_ * N_HEADS + h]
            zz = _dotp(jnp.concatenate([u["q_hat"], u["g_mat"]], axis=0), states[h], _NN, P_STATE)
            o_heads.append(zz[:c] + u["o1"])
            states[h] = zz[c:] + u["h_mat"]
        o_rows.append(jnp.concatenate(o_heads, axis=1))
    o = o_rows[0] if len(o_rows) == 1 else jnp.concatenate(o_rows, axis=0)
    for h in range(N_HEADS):
        s_scr[h] = states[h]
    o_ref[...] = _head_rms_finish(o, z, ones, ng_ref[...])

    @pl.when(j == last)
    def _():
        for h in range(N_HEADS):
            s_out_ref[h] = states[h]
        conv_out_ref[...] = raw[ct - SUBLANES:ct, :]


def _gdn_params(lp):
    return [lp["gdn_conv_w"], _row(jnp.repeat(lp["gdn_A_log"], HEAD_DIM)),
            _row(jnp.repeat(lp["gdn_dt_bias"], HEAD_DIM)), _row(jnp.tile(lp["gdn_norm_g"], N_HEADS))]


def _gdn_call(p_d, n_seq, seq_len, lp):
    ct = min(ROW_TILE, seq_len)
    p3 = p_d.reshape(n_seq, seq_len, D_PACK_W)
    params = _gdn_params(lp)
    o, s1, conv_tail = pl.pallas_call(
        _gdn_kernel,
        grid=(n_seq, seq_len // ct),
        in_specs=[pl.BlockSpec((None, ct, D_PACK_W), lambda b, j: (b, j, 0))]
        + [pl.BlockSpec(a.shape, lambda b, j: (0, 0)) for a in params],
        out_specs=[
            pl.BlockSpec((None, ct, W_MIX), lambda b, j: (b, j, 0)),
            pl.BlockSpec((None, N_HEADS, HEAD_DIM, HEAD_DIM), lambda b, j: (b, 0, 0, 0)),
            pl.BlockSpec((None, SUBLANES, D_QKV_W), lambda b, j: (b, 0, 0)),
        ],
        out_shape=[
            jax.ShapeDtypeStruct((n_seq, seq_len, W_MIX), F32),
            jax.ShapeDtypeStruct((n_seq, N_HEADS, HEAD_DIM, HEAD_DIM), F32),
            jax.ShapeDtypeStruct((n_seq, SUBLANES, D_QKV_W), F32),
        ],
        scratch_shapes=[pltpu.VMEM((ct + SUBLANES, D_QKV_W), F32),
                        pltpu.VMEM((N_HEADS, HEAD_DIM, HEAD_DIM), F32)],
        compiler_params=pltpu.CompilerParams(dimension_semantics=("parallel", "arbitrary"),
                                             vmem_limit_bytes=VMEM_LIMIT),
        name="gdn_prompt",
    )(p3, *params)
    return o.reshape(n_seq * seq_len, W_MIX), s1, conv_tail[:, SUBLANES - (CONV_W - 1):, :]


def _decode_pre_kernel(pa_ref, pb_ref, pc_ref, pd_ref, shift_ref, h0_ref, lconv_ref, gconv_ref, cos_ref, sin_ref,
                       mu_ref, w0_ref, w2_ref, a0_ref, a2_ref, kk_ref, ka_ref,
                       lcw_ref, lcb_ref, lgw_ref, lgb_ref, lsp_ref, gcw_ref, nal_ref, dtb_ref,
                       vec_ref, oc_ref, h1_ref, lconv1_ref, gconv1_ref):
    ones = _head_ones()
    pa_full = pa_ref[...]
    pa = pa_full[:, :A_SHIFT_W]
    pm = pa + (shift_ref[...] - pa) * mu_ref[...]
    r, k, v, ld, av, bv = _rwkv_token_math(pm, w0_ref[...], w2_ref[...], a0_ref[...], a2_ref[...],
                                           kk_ref[...], ka_ref[...], ones)
    vecs = [r, jnp.exp(ld), k, v, av, bv, pa_full[:, A_SHIFT_W:]]
    pb = pb_ref[...]
    cos, sin = cos_ref[...], sin_ref[...]
    vecs += [_rotary(pb[:, 0:W_MIX], cos, sin), _rotary(pb[:, W_MIX:2 * W_MIX], cos, sin) * (HEAD_DIM ** -0.5),
             pb[:, 2 * W_MIX:3 * W_MIX], pb[:, 3 * W_MIX:]]
    pc = pc_ref[...]
    xr = pc[:, :W_MIX]
    lconv = lconv_ref[...]
    taps = [lconv[:, i * W_MIX:(i + 1) * W_MIX] for i in range(CONV_W - 1)] + [xr]
    xc = taps[0] * lcw_ref[0:1, :]
    for i in range(1, CONV_W):
        xc = xc + taps[i] * lcw_ref[i:i + 1, :]
    xc = xc + lcb_ref[...]
    a, b = _lru_token_math(xc, lgw_ref[...], lgb_ref[...], lsp_ref[...])
    hcur = a * h0_ref[...] + b
    oc_ref[...] = hcur * _silu(pc[:, W_MIX:])
    h1_ref[...] = hcur
    lconv1_ref[...] = jnp.concatenate(taps[1:], axis=1)
    pd = pd_ref[...]
    raw = pd[:, :D_QKV_W]
    gconv = gconv_ref[...]
    gtaps = [gconv[:, i * D_QKV_W:(i + 1) * D_QKV_W] for i in range(CONV_W - 1)] + [raw]
    qkv = gtaps[0] * gcw_ref[0:1, :]
    for i in range(1, CONV_W):
        qkv = qkv + gtaps[i] * gcw_ref[i:i + 1, :]
    q, kg, vg, beta, g = _gdn_token_math(qkv, pd[:, D_QKV_W + W_MIX:D_QKV_W + 2 * W_MIX],
                                         pd[:, D_QKV_W + 2 * W_MIX:], nal_ref[...], dtb_ref[...], ones)
    gconv1_ref[...] = jnp.concatenate(gtaps[1:], axis=1)
    vecs += [q, kg, vg, beta, g, pd[:, D_QKV_W:D_QKV_W + W_MIX]]
    assert len(vecs) == N_VEC
    for i, vec in enumerate(vecs):
        vec_ref[i] = vec


def _split3(x):
    hi = x.astype(BF16)
    r1 = x - hi.astype(F32)
    mid = r1.astype(BF16)
    lo = (r1 - mid.astype(F32)).astype(BF16)
    return hi, mid, lo


def _rep(x, e_rep):
    out = None
    for piece in _split3(x):
        t = jnp.dot(piece, e_rep, preferred_element_type=F32)
        out = t if out is None else out + t
    return out


def _tile(x):
    return jnp.tile(x, (1, HEAD_DIM))


def _red_minor(y, e_rep):
    out = None
    for piece in _split3(y):
        t = lax.dot_general(piece, e_rep, (((1,), (1,)), ((), ())), preferred_element_type=F32)
        out = t if out is None else out + t
    return out


def _red_major(y):
    acc = y[:, 0:LANES]
    for m in range(1, y.shape[1] // LANES):
        acc = acc + y[:, m * LANES:(m + 1) * LANES]
    return acc[:, :HEAD_DIM] + acc[:, HEAD_DIM:]


def _decode_state_kernel(vec_ref, wkv_ref, ret_ref, gdn_ref, rk_ref, lng_ref, lnb_ref, gam_ref, ng_ref, e_ref,
                         wkv1_ref, ret1_ref, gdn1_ref, o_ref):
    e_rep = e_ref[...]
    r, w, k, v, av, bv, z_a = (vec_ref[i] for i in range(7))
    s = wkv_ref[...]
    sa = _red_minor(s * _tile(av), e_rep)
    s = s * _tile(w) + _rep(sa, e_rep) * _tile(bv) + _rep(v, e_rep) * _tile(k)
    wkv1_ref[...] = s
    o = _red_minor(s * _tile(r), e_rep)
    mean = jnp.mean(o, axis=-1, keepdims=True)
    dlt = o - mean
    var = jnp.mean(dlt * dlt, axis=-1, keepdims=True)
    on = dlt * lax.rsqrt(var + RWKV_GN_EPS) * lng_ref[...] + lnb_ref[...]
    bonus = jnp.sum(r * k * rk_ref[...], axis=-1, keepdims=True) * v
    o_ref[0] = (on + bonus) * _silu(z_a)
    q, k, v, z_b = (vec_ref[i] for i in range(7, 11))
    s = ret_ref[...] * gam_ref[:, 0:1] + _rep(k, e_rep) * _tile(v)
    ret1_ref[...] = s
    o = _red_major(_rep(q, e_rep) * s)
    o_ref[1] = _rms(o) * _silu(z_b)
    q, k, v, beta, g, z_d = (vec_ref[i] for i in range(11, 17))
    beta1, eg = beta[:, 0:1], jnp.exp(g[:, 0:1])
    s = gdn_ref[...]
    k_rep = _rep(k, e_rep)
    v_new = v * beta1 - _red_major(k_rep * s) * (beta1 * eg)
    o = _red_major(_rep(q, e_rep) * s) * eg + jnp.sum(q * k, axis=-1, keepdims=True) * v_new
    gdn1_ref[...] = s * eg + k_rep * _tile(v_new)
    o_ref[2] = _rms(o) * ng_ref[...] * _silu(z_d)


def _decode_layer(p_a, p_b, p_c, p_d, states, cos_t, sin_t, lp):
    wkv0, shift0, ret0, lru_h0, lru_conv0, gdn0, gdn_conv0 = states
    n = p_a.shape[0]
    flat = HEAD_DIM * HEAD_DIM
    rwkv_params = [_row(lp["rwkv_mu"]), _row(lp["rwkv_w0"]), lp["rwkv_w2"], _row(lp["rwkv_a0"]), lp["rwkv_a2"],
                   _row(lp["rwkv_k_k"]), _row(lp["rwkv_k_a"])]
    gdn_params = _gdn_params(lp)
    ins = [p_a, p_b, p_c, p_d, shift0, lru_h0, lru_conv0.reshape(n, -1), gdn_conv0.reshape(n, -1), cos_t, sin_t,
           *rwkv_params, *_lru_params(lp), *gdn_params[:3]]
    full = lambda a: pl.BlockSpec(a.shape, lambda i: (0,) * a.ndim)
    out_shapes = [
        jax.ShapeDtypeStruct((N_VEC, n, W_MIX), F32),
        jax.ShapeDtypeStruct((n, W_MIX), F32),
        jax.ShapeDtypeStruct((n, W_MIX), F32),
        jax.ShapeDtypeStruct((n, (CONV_W - 1) * W_MIX), F32),
        jax.ShapeDtypeStruct((n, (CONV_W - 1) * D_QKV_W), F32),
    ]
    vecs, o_c, lru_h1, lru_conv1, gdn_conv1 = pl.pallas_call(
        _decode_pre_kernel,
        grid=(1,),
        in_specs=[full(a) for a in ins],
        out_specs=[pl.BlockSpec(s.shape, lambda i, nd=len(s.shape): (0,) * nd) for s in out_shapes],
        out_shape=out_shapes,
        compiler_params=pltpu.CompilerParams(dimension_semantics=("arbitrary",), vmem_limit_bytes=VMEM_LIMIT),
        name="decode_tokens",
    )(*ins)

    vec_h = vecs.reshape(N_VEC, n, N_HEADS, HEAD_DIM).transpose(2, 0, 1, 3)
    head_rows = lambda a: a.reshape(N_HEADS, 1, HEAD_DIM)
    gam = jnp.broadcast_to((1.0 - 2.0 ** (-5.0 - jnp.arange(N_HEADS, dtype=F32)))[:, None, None],
                           (N_HEADS, 1, HEAD_DIM))
    e_rep = (jnp.arange(HEAD_DIM)[:, None] == (jnp.arange(flat)[None, :] // HEAD_DIM)).astype(BF16)
    per_head = [head_rows(lp["rwkv_r_k"]), head_rows(lp["rwkv_ln_g"]), head_rows(lp["rwkv_ln_b"]), gam]
    state_spec = pl.BlockSpec((n, flat), lambda h: (0, h))
    wkv1, ret1, gdn1, o_h = pl.pallas_call(
        _decode_state_kernel,
        grid=(N_HEADS,),
        in_specs=[pl.BlockSpec((None, N_VEC, n, HEAD_DIM), lambda h: (h, 0, 0, 0)), state_spec, state_spec, state_spec]
        + [pl.BlockSpec((None, 1, HEAD_DIM), lambda h: (h, 0, 0)) for _ in per_head]
        + [pl.BlockSpec((1, HEAD_DIM), lambda h: (0, 0)), pl.BlockSpec((HEAD_DIM, flat), lambda h: (0, 0))],
        out_specs=[state_spec, state_spec, state_spec,
                   pl.BlockSpec((None, 3, n, HEAD_DIM), lambda h: (h, 0, 0, 0))],
        out_shape=[jax.ShapeDtypeStruct((n, N_HEADS * flat), F32)] * 3
        + [jax.ShapeDtypeStruct((N_HEADS, 3, n, HEAD_DIM), F32)],
        compiler_params=pltpu.CompilerParams(dimension_semantics=("parallel",), vmem_limit_bytes=VMEM_LIMIT),
        name="decode_states",
    )(vec_h, wkv0.reshape(n, -1), ret0.reshape(n, -1), gdn0.reshape(n, -1), *per_head,
      _row(lp["gdn_norm_g"]), e_rep)
    o3 = o_h.transpose(1, 2, 0, 3).reshape(3, n, W_MIX)
    shape4 = (n, N_HEADS, HEAD_DIM, HEAD_DIM)
    new_states = (wkv1.reshape(shape4), p_a[:, :A_SHIFT_W], ret1.reshape(shape4), lru_h1,
                  lru_conv1.reshape(n, CONV_W - 1, W_MIX), gdn1.reshape(shape4),
                  gdn_conv1.reshape(n, CONV_W - 1, D_QKV_W))
    return (o3[0], o3[1], o_c, o3[2]), new_states


def _pack_in_weights(w_in):
    off_d = A_W + B_W + C_W
    off_ba = off_d + D_QKV_W
    off_z = off_ba + 2 * N_HEADS
    off_g = off_d + D_W
    w_pack = jnp.concatenate([
        w_in[:, :off_ba],
        w_in[:, off_z:off_g],
        jnp.repeat(w_in[:, off_ba:off_ba + N_HEADS], HEAD_DIM, axis=1),
        jnp.repeat(w_in[:, off_ba + N_HEADS:off_z], HEAD_DIM, axis=1),
    ], axis=1)
    return w_pack, w_in[:, off_g:]


def _prompt_layer(p_a, p_b, p_c, p_d, n_seq, seq_len, cos_t, sin_t, lp):
    o_a, wkv1, shift1 = _rwkv_call(p_a, n_seq, seq_len, lp)
    o_b, ret1 = _ret_call(p_b, n_seq, seq_len, cos_t, sin_t)
    o_c, lru_h1, lru_conv1 = _lru_call(p_c, n_seq, seq_len, lp)
    o_d, gdn1, gdn_conv1 = _gdn_call(p_d, n_seq, seq_len, lp)
    return (o_a, o_b, o_c, o_d), (wkv1, shift1, ret1, lru_h1, lru_conv1, gdn1, gdn_conv1)


def _run_group(x, mods, pos, carried, layers, final_g):
    n_seq, seq_len, d = x.shape
    x2 = x.reshape(n_seq * seq_len, d)
    cos_t, sin_t = _rope_tables(pos)
    new = []
    n_layers = len(layers)
    for l, lp in enumerate(layers):
        shift, scale, gate = (mods[l][:, i * d:(i + 1) * d] for i in range(3))
        p_a, p_b, p_c, p_d = _inproj_call(x2, scale, shift, lp["norm_g"], lp["w_pack"], seq_len)
        if carried is None:
            branches, st = _prompt_layer(p_a, p_b, p_c, p_d, n_seq, seq_len, cos_t, sin_t, lp)
        else:
            branches, st = _decode_layer(p_a, p_b, p_c, p_d, carried[l], cos_t, sin_t, lp)
        new.append(st)
        x2 = _outproj_call(x2, scale, shift, gate, lp["norm_g"], branches, lp["w_gate"], lp["w_up_bf16"],
                           lp["w_out_bf16"], final_g, seq_len, final=(l == n_layers - 1))
    stacked = tuple(jnp.stack([s[i] for s in new], axis=0) for i in range(7))
    return x2.reshape(n_seq, seq_len, d), stacked


def kernel(x_prompt, x_sample, c_prompt, c_sample, state_rwkv_wkv, state_rwkv_shift, state_ret, state_lru_h, state_lru_conv, state_gdn, state_gdn_conv, ada_w, ada_b, norm_g, w_in, rwkv_mu, rwkv_w0, rwkv_w2, rwkv_a0, rwkv_a2, rwkv_k_k, rwkv_k_a, rwkv_r_k, rwkv_ln_g, rwkv_ln_b, lru_conv_w, lru_conv_b, lru_gate_w, lru_gate_b, lru_lambda, gdn_conv_w, gdn_A_log, gdn_dt_bias, gdn_norm_g, w_up, w_out, final_g):
    n_layers = ada_w.shape[0]
    n_prompt, seq_len, _ = x_prompt.shape
    n_sample, dec_len, _ = x_sample.shape
    assert dec_len == 1, "the decode path handles one token per sequence"
    layers = []
    w_in_bf16 = w_in.astype(BF16)
    for l in range(n_layers):
        w_pack, w_gate = _pack_in_weights(w_in_bf16[l])
        layers.append(dict(
            norm_g=norm_g[l], w_pack=w_pack, w_gate=w_gate, w_up_bf16=w_up[l].astype(BF16),
            w_out_bf16=w_out[l].astype(BF16),
            rwkv_mu=rwkv_mu[l], rwkv_w0=rwkv_w0[l], rwkv_w2=rwkv_w2[l], rwkv_a0=rwkv_a0[l], rwkv_a2=rwkv_a2[l],
            rwkv_k_k=rwkv_k_k[l], rwkv_k_a=rwkv_k_a[l], rwkv_r_k=rwkv_r_k[l], rwkv_ln_g=rwkv_ln_g[l],
            rwkv_ln_b=rwkv_ln_b[l], lru_conv_w=lru_conv_w[l], lru_conv_b=lru_conv_b[l],
            lru_gate_w=lru_gate_w[l], lru_gate_b=lru_gate_b[l], lru_lambda=lru_lambda[l],
            gdn_conv_w=gdn_conv_w[l], gdn_A_log=gdn_A_log[l], gdn_dt_bias=gdn_dt_bias[l],
            gdn_norm_g=gdn_norm_g[l]))
    mods = _ada_call(jnp.concatenate([c_prompt, c_sample], axis=0), ada_w, ada_b)
    mods_p = [mods[l, :n_prompt] for l in range(n_layers)]
    mods_s = [mods[l, n_prompt:] for l in range(n_layers)]

    y_prompt, new_p = _run_group(x_prompt, mods_p, jnp.arange(seq_len, dtype=jnp.int32), None, layers, final_g)
    carried = [(state_rwkv_wkv[l], state_rwkv_shift[l], state_ret[l], state_lru_h[l], state_lru_conv[l],
                state_gdn[l], state_gdn_conv[l]) for l in range(n_layers)]
    pos_s = PAST_LEN + jnp.arange(dec_len, dtype=jnp.int32)
    y_sample, new_s = _run_group(x_sample, mods_s, pos_s, carried, layers, final_g)
    return (y_prompt, y_sample) + new_p + new_s
```

```python
import functools
import math

import jax
import jax.numpy as jnp
from jax import lax
from jax.experimental import pallas as pl
from jax.experimental.pallas import tpu as pltpu

F32 = jnp.float32
BF16 = jnp.bfloat16
HI = lax.Precision.HIGHEST

N_HEADS = 4
HEAD_DIM = 64
W_MIX = N_HEADS * HEAD_DIM
LORA = 64
CONV_W = 4
N_BRANCH = 4
LRU_C = 8.0
ROPE_BASE = 10000.0
EPS = 1e-6
RWKV_GN_EPS = 64e-5
PAST_LEN = 16384
A_SHIFT_W = 3 * W_MIX + 2 * LORA
A_W = A_SHIFT_W + W_MIX
B_W = 4 * W_MIX
C_W = 2 * W_MIX
D_QKV_W = 3 * W_MIX
D_W = D_QKV_W + 2 * N_HEADS + W_MIX
D_PACK_W = D_QKV_W + 3 * W_MIX

SUBLANES = 8
LANES = 128
VMEM_LIMIT = 56 * 1024 * 1024

CHUNK = 64
RET_CHUNK = 128
INV_BLOCK = 16
ROW_TILE = 256
PROJ_TILE = 512
N_VEC_T = 14
N_VEC_PLAIN = 6


def _mm(a, b, prec=HI):
    return lax.dot_general(a, b, (((1,), (0,)), ((), ())), precision=prec, preferred_element_type=F32)


def _mm_nt(a, b, prec=HI):
    return lax.dot_general(a, b, (((1,), (1,)), ((), ())), precision=prec, preferred_element_type=F32)


def _mm_tn(a, b, prec=HI):
    return lax.dot_general(a, b, (((0,), (0,)), ((), ())), precision=prec, preferred_element_type=F32)


_NN = (((1,), (0,)), ((), ()))
_NT = (((1,), (1,)), ((), ()))
_TN = (((0,), (0,)), ((), ()))

P_INV = 1
P_STATE = 1
P_MISC = 1


class _Split:
    def __init__(self, x, passes):
        self.hi = x.astype(BF16)
        self.lo = (x - self.hi.astype(F32)).astype(BF16) if passes > 1 else None


def _dotp(a, b, dims=_NN, passes=1):
    a = a if isinstance(a, _Split) else _Split(a, passes)
    b = b if isinstance(b, _Split) else _Split(b, passes)
    d = lambda x, y: lax.dot_general(x, y, dims, preferred_element_type=F32)
    out = d(a.hi, b.hi)
    if passes > 1:
        out = out + (d(a.hi, b.lo) + d(a.lo, b.hi))
    return out


def _iota(shape, dim):
    return lax.broadcasted_iota(jnp.int32, shape, dim)


def _silu(x):
    return x * jax.nn.sigmoid(x)


def _softplus(x):
    return jnp.maximum(x, 0.0) + jnp.log1p(jnp.exp(-jnp.abs(x)))


def _pieces(x, n):
    out = []
    for i in range(n):
        p = x.astype(BF16)
        out.append(p)
        if i + 1 < n:
            x = x - p.astype(F32)
    return out


def _dot_const(x, const, dims=_NN, n=2, const_left=False):
    out = None
    for p in _pieces(x, n):
        t = lax.dot_general(*((const, p) if const_left else (p, const)), dims, preferred_element_type=F32)
        out = t if out is None else out + t
    return out


def _head_ones():
    return (_iota((W_MIX, W_MIX), 0) // HEAD_DIM == _iota((W_MIX, W_MIX), 1) // HEAD_DIM).astype(BF16)


def _head_sum(x, ones):
    return _dot_const(x, ones)


def _rms(x):
    return x * lax.rsqrt(jnp.mean(x * x, axis=-1, keepdims=True) + EPS)


def _inv_unit_lower(a):
    return _inv_unit_lower_many([a])[0]


def _inv_unit_lower_many(mats):
    n = mats[0].shape[0]
    ri, ci = _iota((n, n), 0), _iota((n, n), 1)
    eye = (ri == ci).astype(F32)
    diag_blk = (ri // INV_BLOCK) == (ci // INV_BLOCK)
    mm = lambda x, y: _dotp(x, y, _NN, P_INV)
    sp = lambda x: _Split(x, P_INV)
    d = [jnp.where(diag_blk, a, 0.0) for a in mats]
    nb = [a - di for a, di in zip(mats, d)]
    td = [eye - di for di in d]
    p = d
    for _ in range(int(math.log2(INV_BLOCK)) - 1):
        ps = [sp(pi) for pi in p]
        p = [mm(pi, pi) for pi in ps]
        td = [mm(ti, eye + pi) for ti, pi in zip(td, p)]
    tds = [sp(ti) for ti in td]
    x = [mm(ti, ni) for ti, ni in zip(tds, nb)]
    t = [eye - xi for xi in x]
    p = x
    for _ in range(int(math.log2(n // INV_BLOCK)) - 1):
        ps = [sp(pi) for pi in p]
        p = [mm(pi, pi) for pi in ps]
        t = [mm(ti, eye + pi) for ti, pi in zip(t, p)]
    return [mm(ti, tdi) for ti, tdi in zip(t, tds)]


def _ada_kernel(c_ref, w_ref, b_ref, o_ref):
    o_ref[...] = _mm(_silu(c_ref[...]), w_ref[...]) + b_ref[...]


def _ada_call(c_all, ada_w, ada_b):
    n_layers, d, d3 = ada_w.shape
    rows = c_all.shape[0]
    return pl.pallas_call(
        _ada_kernel,
        grid=(n_layers, d3 // d),
        in_specs=[
            pl.BlockSpec((rows, d), lambda l, j: (0, 0)),
            pl.BlockSpec((None, d, d), lambda l, j: (l, 0, j)),
            pl.BlockSpec((None, 1, d), lambda l, j: (l, 0, j)),
        ],
        out_specs=pl.BlockSpec((None, rows, d), lambda l, j: (l, 0, j)),
        out_shape=jax.ShapeDtypeStruct((n_layers, rows, d3), F32),
        compiler_params=pltpu.CompilerParams(dimension_semantics=("arbitrary", "arbitrary"),
                                             vmem_limit_bytes=VMEM_LIMIT),
        name="ada_mod",
    )(c_all, ada_w, ada_b.reshape(n_layers, 1, d3))


def _modulated_norm(x, g, scale, shift):
    return _rms(x) * g * (1.0 + scale) + shift


def _inproj_kernel(x_ref, sc_ref, sh_ref, g_ref, w_ref, oa_ref, ob_ref, oc_ref, od_ref):
    h = _modulated_norm(x_ref[...], g_ref[...], sc_ref[...], sh_ref[...]).astype(BF16)
    lo = 0
    for o_ref in (oa_ref, ob_ref, oc_ref, od_ref):
        wd = o_ref.shape[-1]
        o_ref[...] = jnp.dot(h, w_ref[:, lo:lo + wd], preferred_element_type=F32)
        lo += wd


def _mod_specs(mods, tm, seq_len):
    d = mods[0].shape[-1]
    if seq_len == 1:
        return [m for m in mods], [pl.BlockSpec((tm, d), lambda i: (i, 0)) for _ in mods]
    per_seq = seq_len // tm
    return ([m.reshape(m.shape[0], 1, d) for m in mods],
            [pl.BlockSpec((None, 1, d), lambda i: (i // per_seq, 0, 0)) for _ in mods])


def _inproj_call(x2, scale, shift, g, w_pack, seq_len):
    m, d = x2.shape
    tm = min(PROJ_TILE, m, seq_len) if seq_len > 1 else m
    widths = (A_W, B_W, C_W, D_PACK_W)
    mods, mod_specs = _mod_specs((scale, shift), tm, seq_len)
    return pl.pallas_call(
        _inproj_kernel,
        grid=(m // tm,),
        in_specs=[pl.BlockSpec((tm, d), lambda i: (i, 0))] + mod_specs + [
            pl.BlockSpec((1, d), lambda i: (0, 0)),
            pl.BlockSpec(w_pack.shape, lambda i: (0, 0)),
        ],
        out_specs=[pl.BlockSpec((tm, wd), lambda i: (i, 0)) for wd in widths],
        out_shape=[jax.ShapeDtypeStruct((m, wd), F32) for wd in widths],
        compiler_params=pltpu.CompilerParams(dimension_semantics=("parallel",), vmem_limit_bytes=VMEM_LIMIT),
        name="in_proj",
    )(x2, *mods, g.reshape(1, d), w_pack)


def _outproj_kernel(x_ref, sc_ref, sh_ref, gt_ref, g_ref, ba_ref, bb_ref, bc_ref, bd_ref,
                    wg_ref, wup_ref, wout_ref, fg_ref, o_ref, *, final):
    x = x_ref[...]
    d = x.shape[-1]
    h = _modulated_norm(x, g_ref[...], sc_ref[...], sh_ref[...]).astype(BF16)
    merged = jnp.zeros(x.shape, F32)
    for n, br_ref in enumerate((ba_ref, bb_ref, bc_ref, bd_ref)):
        gl = jnp.dot(h, wg_ref[:, n * d:(n + 1) * d], preferred_element_type=F32)
        up = jnp.dot(br_ref[...].astype(BF16), wup_ref[n], preferred_element_type=F32)
        merged = merged + jax.nn.sigmoid(gl) * up
    out = jnp.dot(merged.astype(BF16), wout_ref[...], preferred_element_type=F32)
    xn = x + gt_ref[...] * out
    if final:
        xn = _rms(xn) * fg_ref[...]
    o_ref[...] = xn


def _outproj_call(x2, scale, shift, gate, g, branches, wg, wup, wout, final_g, seq_len, final):
    m, d = x2.shape
    tm = min(PROJ_TILE, m, seq_len) if seq_len > 1 else m
    mods, mod_specs = _mod_specs((scale, shift, gate), tm, seq_len)
    full = lambda a: pl.BlockSpec(a.shape, lambda i: (0,) * a.ndim)
    return pl.pallas_call(
        functools.partial(_outproj_kernel, final=final),
        grid=(m // tm,),
        in_specs=[pl.BlockSpec((tm, d), lambda i: (i, 0))] + mod_specs + [pl.BlockSpec((1, d), lambda i: (0, 0))]
        + [pl.BlockSpec((tm, W_MIX), lambda i: (i, 0)) for _ in branches]
        + [full(wg), full(wup), full(wout), pl.BlockSpec((1, d), lambda i: (0, 0))],
        out_specs=pl.BlockSpec((tm, d), lambda i: (i, 0)),
        out_shape=jax.ShapeDtypeStruct((m, d), F32),
        compiler_params=pltpu.CompilerParams(dimension_semantics=("parallel",), vmem_limit_bytes=VMEM_LIMIT),
        name="out_proj",
    )(x2, *mods, g.reshape(1, d), *branches, wg, wup, wout, final_g.reshape(1, d))


def _rwkv_token_math(pm, w0, w2, a0, a2, k_k, k_a, ones):
    r = pm[:, 0:W_MIX]
    k = pm[:, W_MIX:2 * W_MIX]
    v = pm[:, 2 * W_MIX:3 * W_MIX]
    wd = pm[:, 3 * W_MIX:3 * W_MIX + LORA]
    ad = pm[:, 3 * W_MIX + LORA:]
    w_log = -_softplus(-(w0 + _dotp(jnp.tanh(wd), w2, _NN, P_MISC))) - 0.5
    log_decay = -jnp.exp(w_log)
    a = jax.nn.sigmoid(a0 + _dotp(ad, a2, _NN, P_MISC))
    kx = k * k_k
    kk = kx * lax.rsqrt(_head_sum(kx * kx, ones) + EPS)
    k = k * (1.0 + (a - 1.0) * k_a)
    return r, k, v, log_decay, -kk, kk * a


def _rwkv_finish(o, r, k, v, z, r_k, ln_g, ln_b, ones):
    mean = _head_sum(o, ones) * (1.0 / HEAD_DIM)
    dlt = o - mean
    var = _head_sum(dlt * dlt, ones) * (1.0 / HEAD_DIM)
    on = dlt * lax.rsqrt(var + RWKV_GN_EPS) * ln_g + ln_b
    bonus = _head_sum(r * k * r_k, ones) * v
    return (on + bonus) * _silu(z)


def _swap_halves(x):
    half = HEAD_DIM // 2
    n = x.shape[-1]
    first = (_iota(x.shape, 1) & half) == 0
    return jnp.where(first, pltpu.roll(x, n - half, axis=1), pltpu.roll(x, half, axis=1))


def _rotary(x, cos, sin):
    return x * cos + _swap_halves(x) * sin


def _lru_token_math(xc, gate_w, gate_b, lam):
    gates = _dotp(xc, gate_w, _NN, P_MISC) + gate_b
    r_gate = jax.nn.sigmoid(gates[:, :W_MIX])
    i_gate = jax.nn.sigmoid(gates[:, W_MIX:])
    log_a = -LRU_C * r_gate * _softplus(-lam)
    a = jnp.exp(log_a)
    b = jnp.sqrt(1.0 - jnp.exp(2.0 * log_a)) * (i_gate * xc)
    return a, b


def _gdn_token_math(qkv, b_raw, a_raw, a_log, dt_bias, ones):
    qkv = _silu(qkv)
    q = qkv[:, 0:W_MIX]
    k = qkv[:, W_MIX:2 * W_MIX]
    v = qkv[:, 2 * W_MIX:]
    q = q * lax.rsqrt(_head_sum(q * q, ones) + EPS) * (HEAD_DIM ** -0.5)
    k = k * lax.rsqrt(_head_sum(k * k, ones) + EPS)
    beta = jax.nn.sigmoid(b_raw)
    g = -jnp.exp(a_log) * _softplus(a_raw + dt_bias)
    return q, k, v, beta, g


def _head_rms_finish(o, z, ones, gain=None):
    y = o * lax.rsqrt(_head_sum(o * o, ones) * (1.0 / HEAD_DIM) + EPS)
    if gain is not None:
        y = y * gain
    return y * _silu(z)


def _conv_tile(u, ext_ref, w_ref, first):
    n = u.shape[0]

    @pl.when(first)
    def _():
        ext_ref[0:SUBLANES, :] = jnp.zeros((SUBLANES, u.shape[1]), F32)

    ext_ref[SUBLANES:SUBLANES + n, :] = u
    out = None
    for j in range(CONV_W):
        back = CONV_W - 1 - j
        term = ext_ref[SUBLANES - back:SUBLANES - back + n, :] * w_ref[j:j + 1, :]
        out = term if out is None else out + term
    ext_ref[0:SUBLANES, :] = u[n - SUBLANES:n, :]
    return out


def _rwkv_kernel(p_ref, mu_ref, w0_ref, w2_ref, a0_ref, a2_ref, kk_ref, ka_ref, rk_ref, lng_ref, lnb_ref,
                 o_ref, s_out_ref, shift_out_ref, s_scr, prev_scr):
    j = pl.program_id(1)
    last = pl.num_programs(1) - 1
    ct = p_ref.shape[0]

    @pl.when(j == 0)
    def _():
        s_scr[...] = jnp.zeros(s_scr.shape, F32)
        prev_scr[...] = jnp.zeros(prev_scr.shape, F32)

    p = p_ref[...]
    pa = p[:, :A_SHIFT_W]
    z = p[:, A_SHIFT_W:]
    rows = _iota((ct, 1), 0)
    prev = jnp.where(rows == 0, prev_scr[...], pltpu.roll(pa, 1, axis=0))
    prev_scr[...] = pa[ct - 1:ct, :]
    pm = pa + (prev - pa) * mu_ref[...]
    ones = _head_ones()
    r, k, v, ld, av, bv = _rwkv_token_math(pm, w0_ref[...], w2_ref[...], a0_ref[...], a2_ref[...],
                                           kk_ref[...], ka_ref[...], ones)

    c = min(CHUNK, ct)
    ri, ci = _iota((c, c), 0), _iota((c, c), 1)
    strict = ri > ci
    incl = ri >= ci
    eye = (ri == ci).astype(F32)
    lt = incl.astype(BF16)
    units = []
    for c0 in range(0, ct, c):
        sl = slice(c0, c0 + c)
        ldc = ld[sl]
        cum = _dot_const(ldc, lt, _NN, 3, const_left=True)
        e_neg = jnp.exp(-cum)
        e_out = jnp.exp(cum[c - 1:c, :] - cum)
        at = av[sl] * jnp.exp(cum - ldc)
        rt = r[sl] * jnp.exp(cum)
        bt, kt = bv[sl] * e_neg, k[sl] * e_neg
        bo, ko = bv[sl] * e_out, k[sl] * e_out
        g_end = jnp.exp(cum[c - 1:c, :])
        vc = v[sl]
        for h in range(N_HEADS):
            hs = slice(h * HEAD_DIM, (h + 1) * HEAD_DIM)
            units.append(dict(a=at[:, hs], r=rt[:, hs], b=bt[:, hs], k=kt[:, hs], bo=bo[:, hs], ko=ko[:, hs],
                              v=vc[:, hs], g=g_end[:, hs]))
    misc = lambda x, y, dims=_NN: _dotp(x, y, dims, P_MISC)
    inv = lambda x, y, dims=_NN: _dotp(x, y, dims, P_INV)
    for u in units:
        lhs = jnp.concatenate([u["a"], u["r"]], axis=0)
        u["mb"] = misc(lhs, u["b"], _NT)
        u["mk"] = misc(lhs, u["k"], _NT)
    for u in units:
        u["m_ab"] = jnp.where(strict, u["mb"][:c], 0.0)
        u["m_ak"] = jnp.where(strict, u["mk"][:c], 0.0)
        u["m_rb"] = jnp.where(incl, u["mb"][c:], 0.0)
        u["m_rk"] = jnp.where(incl, u["mk"][c:], 0.0)
    for u, t_inv in zip(units, _inv_unit_lower_many([-u["m_ab"] for u in units])):
        u["t_inv"] = _Split(t_inv, P_INV)
    for u in units:
        u["makv"] = misc(u["m_ak"], u["v"])
    for u in units:
        u["a_hat"] = inv(u["t_inv"], u["a"])
        u["u1"] = inv(u["t_inv"], u["makv"])
    for u in units:
        u["r_hat"] = u["r"] + misc(u["m_rb"], u["a_hat"])
        u["o1"] = misc(u["m_rb"], u["u1"]) + misc(u["m_rk"], u["v"])
        u["g_t"] = eye * u["g"] + misc(u["bo"], u["a_hat"], _TN)
        u["h_t"] = misc(u["bo"], u["u1"], _TN) + misc(u["ko"], u["v"], _TN)
    states = [s_scr[h] for h in range(N_HEADS)]
    o_rows = []
    for ci_ in range(ct // c):
        o_heads = []
        for h in range(N_HEADS):
            u = units[ci_ * N_HEADS + h]
            zz = _dotp(jnp.concatenate([u["r_hat"], u["g_t"]], axis=0), states[h], _NN, P_STATE)
            o_heads.append(zz[:c] + u["o1"])
            states[h] = zz[c:] + u["h_t"]
        o_rows.append(jnp.concatenate(o_heads, axis=1))
    o = o_rows[0] if len(o_rows) == 1 else jnp.concatenate(o_rows, axis=0)
    for h in range(N_HEADS):
        s_scr[h] = states[h]
    o_ref[...] = _rwkv_finish(o, r, k, v, z, rk_ref[...], lng_ref[...], lnb_ref[...], ones)

    @pl.when(j == last)
    def _():
        eye_h = (_iota((HEAD_DIM, HEAD_DIM), 0) == _iota((HEAD_DIM, HEAD_DIM), 1)).astype(F32)
        for h in range(N_HEADS):
            s_out_ref[h] = _mm_nt(eye_h, states[h])
        shift_out_ref[...] = pa[ct - 1:ct, :]


def _row(a):
    return a.reshape(1, -1)


def _rwkv_call(p_a, n_seq, seq_len, lp):
    ct = min(ROW_TILE, seq_len)
    p3 = p_a.reshape(n_seq, seq_len, A_W)
    params = [_row(lp["rwkv_mu"]), _row(lp["rwkv_w0"]), lp["rwkv_w2"], _row(lp["rwkv_a0"]), lp["rwkv_a2"],
              _row(lp["rwkv_k_k"]), _row(lp["rwkv_k_a"]), _row(lp["rwkv_r_k"]), _row(lp["rwkv_ln_g"]),
              _row(lp["rwkv_ln_b"])]
    o, s1, shift1 = pl.pallas_call(
        _rwkv_kernel,
        grid=(n_seq, seq_len // ct),
        in_specs=[pl.BlockSpec((None, ct, A_W), lambda b, j: (b, j, 0))]
        + [pl.BlockSpec(a.shape, lambda b, j: (0, 0)) for a in params],
        out_specs=[
            pl.BlockSpec((None, ct, W_MIX), lambda b, j: (b, j, 0)),
            pl.BlockSpec((None, N_HEADS, HEAD_DIM, HEAD_DIM), lambda b, j: (b, 0, 0, 0)),
            pl.BlockSpec((None, 1, A_SHIFT_W), lambda b, j: (b, 0, 0)),
        ],
        out_shape=[
            jax.ShapeDtypeStruct((n_seq, seq_len, W_MIX), F32),
            jax.ShapeDtypeStruct((n_seq, N_HEADS, HEAD_DIM, HEAD_DIM), F32),
            jax.ShapeDtypeStruct((n_seq, 1, A_SHIFT_W), F32),
        ],
        scratch_shapes=[pltpu.VMEM((N_HEADS, HEAD_DIM, HEAD_DIM), F32), pltpu.VMEM((1, A_SHIFT_W), F32)],
        compiler_params=pltpu.CompilerParams(dimension_semantics=("parallel", "arbitrary"),
                                             vmem_limit_bytes=VMEM_LIMIT),
        name="rwkv7_prompt",
    )(p3, *params)
    return o.reshape(n_seq * seq_len, W_MIX), s1, shift1.reshape(n_seq, A_SHIFT_W)


def _ret_kernel(p_ref, cos_ref, sin_ref, o_ref, s_out_ref, s_scr):
    j = pl.program_id(1)
    last = pl.num_programs(1) - 1
    ct = p_ref.shape[0]

    @pl.when(j == 0)
    def _():
        s_scr[...] = jnp.zeros(s_scr.shape, F32)

    p = p_ref[...]
    cos, sin = cos_ref[...], sin_ref[...]
    q = _rotary(p[:, 0:W_MIX], cos, sin)
    k = _rotary(p[:, W_MIX:2 * W_MIX], cos, sin) * (HEAD_DIM ** -0.5)
    v = p[:, 2 * W_MIX:3 * W_MIX]
    z = p[:, 3 * W_MIX:]
    c = min(RET_CHUNK, ct)
    ri, ci = _iota((c, c), 0), _iota((c, c), 1)
    causal = ri >= ci
    rel = jnp.where(causal, ri - ci, 0).astype(F32)
    idx = _iota((c, 1), 0).astype(F32)
    states = [s_scr[h] for h in range(N_HEADS)]
    o_heads_all = [[] for _ in range(N_HEADS)]
    for h in range(N_HEADS):
        lg = math.log(1.0 - 2.0 ** (-5.0 - h))
        decay = jnp.where(causal, jnp.exp(lg * rel), 0.0)
        q_dec = jnp.exp(lg * (idx + 1.0))
        k_dec = jnp.exp(lg * (c - 1.0 - idx))
        g_c = math.exp(lg * c)
        hs = slice(h * HEAD_DIM, (h + 1) * HEAD_DIM)
        s = states[h]
        for c0 in range(0, ct, c):
            sl = slice(c0, c0 + c)
            qh, kh, vh = q[sl, hs], k[sl, hs], v[sl, hs]
            s_in = _dotp(qh, kh, _NT, P_MISC) * decay
            o_heads_all[h].append(_dotp(s_in, vh, _NN, P_MISC) + _dotp(qh, s, _NN, P_MISC) * q_dec)
            s = s * g_c + _dotp(kh * k_dec, vh, _TN, P_MISC)
        states[h] = s
    cols = [oh[0] if len(oh) == 1 else jnp.concatenate(oh, axis=0) for oh in o_heads_all]
    o = jnp.concatenate(cols, axis=1)
    for h in range(N_HEADS):
        s_scr[h] = states[h]
    o_ref[...] = _head_rms_finish(o, z, _head_ones())

    @pl.when(j == last)
    def _():
        for h in range(N_HEADS):
            s_out_ref[h] = states[h]


def _rope_tables(pos):
    half = HEAD_DIM // 2
    inv = ROPE_BASE ** (-jnp.arange(half, dtype=F32) / half)
    ang = pos.astype(F32)[:, None] * inv[None, :]
    cos, sin = jnp.cos(ang), jnp.sin(ang)
    cos_t = jnp.tile(jnp.concatenate([cos, cos], axis=-1), (1, N_HEADS))
    sin_t = jnp.tile(jnp.concatenate([-sin, sin], axis=-1), (1, N_HEADS))
    return cos_t, sin_t


def _ret_call(p_b, n_seq, seq_len, cos_t, sin_t):
    ct = min(ROW_TILE, seq_len)
    p3 = p_b.reshape(n_seq, seq_len, B_W)
    o, s1 = pl.pallas_call(
        _ret_kernel,
        grid=(n_seq, seq_len // ct),
        in_specs=[
            pl.BlockSpec((None, ct, B_W), lambda b, j: (b, j, 0)),
            pl.BlockSpec((ct, W_MIX), lambda b, j: (j, 0)),
            pl.BlockSpec((ct, W_MIX), lambda b, j: (j, 0)),
        ],
        out_specs=[
            pl.BlockSpec((None, ct, W_MIX), lambda b, j: (b, j, 0)),
            pl.BlockSpec((None, N_HEADS, HEAD_DIM, HEAD_DIM), lambda b, j: (b, 0, 0, 0)),
        ],
        out_shape=[
            jax.ShapeDtypeStruct((n_seq, seq_len, W_MIX), F32),
            jax.ShapeDtypeStruct((n_seq, N_HEADS, HEAD_DIM, HEAD_DIM), F32),
        ],
        scratch_shapes=[pltpu.VMEM((N_HEADS, HEAD_DIM, HEAD_DIM), F32)],
        compiler_params=pltpu.CompilerParams(dimension_semantics=("parallel", "arbitrary"),
                                             vmem_limit_bytes=VMEM_LIMIT),
        name="retention_prompt",
    )(p3, cos_t, sin_t)
    return o.reshape(n_seq * seq_len, W_MIX), s1


def _lru_kernel(p_ref, cw_ref, cb_ref, gw_ref, gb_ref, sp_ref, o_ref, h_out_ref, conv_out_ref, ext_scr, h_scr):
    j = pl.program_id(1)
    last = pl.num_programs(1) - 1
    ct = p_ref.shape[0]

    @pl.when(j == 0)
    def _():
        h_scr[...] = jnp.zeros(h_scr.shape, F32)

    p = p_ref[...]
    xr = p[:, :W_MIX]
    z = p[:, W_MIX:]
    xc = _conv_tile(xr, ext_scr, cw_ref, j == 0) + cb_ref[...]
    a, b = _lru_token_math(xc, gw_ref[...], gb_ref[...], sp_ref[...])
    rows = _iota((ct, 1), 0)
    dist = 1
    while dist < ct:
        keep = rows >= dist
        a_prev = jnp.where(keep, pltpu.roll(a, dist, axis=0), 1.0)
        b_prev = jnp.where(keep, pltpu.roll(b, dist, axis=0), 0.0)
        b = a * b_prev + b
        a = a * a_prev
        dist *= 2
    hcur = a * h_scr[...] + b
    h_scr[...] = hcur[ct - 1:ct, :]
    o_ref[...] = hcur * _silu(z)

    @pl.when(j == last)
    def _():
        h_out_ref[...] = hcur[ct - 1:ct, :]
        conv_out_ref[...] = xr[ct - SUBLANES:ct, :]


def _block_diag_gates(gate_w):
    out = jnp.zeros((W_MIX, 2 * W_MIX), F32)
    for g in range(2):
        for n in range(N_HEADS):
            out = out.at[n * HEAD_DIM:(n + 1) * HEAD_DIM,
                         g * W_MIX + n * HEAD_DIM:g * W_MIX + (n + 1) * HEAD_DIM].set(gate_w[g, n])
    return out


def _lru_params(lp):
    return [lp["lru_conv_w"], _row(lp["lru_conv_b"]), _block_diag_gates(lp["lru_gate_w"]),
            _row(lp["lru_gate_b"]), _row(lp["lru_lambda"])]


def _lru_call(p_c, n_seq, seq_len, lp):
    ct = min(ROW_TILE, seq_len)
    p3 = p_c.reshape(n_seq, seq_len, C_W)
    params = _lru_params(lp)
    o, h1, conv_tail = pl.pallas_call(
        _lru_kernel,
        grid=(n_seq, seq_len // ct),
        in_specs=[pl.BlockSpec((None, ct, C_W), lambda b, j: (b, j, 0))]
        + [pl.BlockSpec(a.shape, lambda b, j: (0, 0)) for a in params],
        out_specs=[
            pl.BlockSpec((None, ct, W_MIX), lambda b, j: (b, j, 0)),
            pl.BlockSpec((None, 1, W_MIX), lambda b, j: (b, 0, 0)),
            pl.BlockSpec((None, SUBLANES, W_MIX), lambda b, j: (b, 0, 0)),
        ],
        out_shape=[
            jax.ShapeDtypeStruct((n_seq, seq_len, W_MIX), F32),
            jax.ShapeDtypeStruct((n_seq, 1, W_MIX), F32),
            jax.ShapeDtypeStruct((n_seq, SUBLANES, W_MIX), F32),
        ],
        scratch_shapes=[pltpu.VMEM((ct + SUBLANES, W_MIX), F32), pltpu.VMEM((1, W_MIX), F32)],
        compiler_params=pltpu.CompilerParams(dimension_semantics=("parallel", "arbitrary"),
                                             vmem_limit_bytes=VMEM_LIMIT),
        name="rglru_prompt",
    )(p3, *params)
    return (o.reshape(n_seq * seq_len, W_MIX), h1.reshape(n_seq, W_MIX),
            conv_tail[:, SUBLANES - (CONV_W - 1):, :])


def _gdn_kernel(p_ref, cw_ref, nal_ref, dtb_ref, ng_ref, o_ref, s_out_ref, conv_out_ref, ext_scr, s_scr):
    j = pl.program_id(1)
    last = pl.num_programs(1) - 1
    ct = p_ref.shape[0]

    @pl.when(j == 0)
    def _():
        s_scr[...] = jnp.zeros(s_scr.shape, F32)

    p = p_ref[...]
    raw = p[:, :D_QKV_W]
    z = p[:, D_QKV_W:D_QKV_W + W_MIX]
    b_raw = p[:, D_QKV_W + W_MIX:D_QKV_W + 2 * W_MIX]
    a_raw = p[:, D_QKV_W + 2 * W_MIX:]
    ones = _head_ones()
    qkv = _conv_tile(raw, ext_scr, cw_ref, j == 0)
    q, k, v, beta, g = _gdn_token_math(qkv, b_raw, a_raw, nal_ref[...], dtb_ref[...], ones)

    c = min(CHUNK, ct)
    ri, ci = _iota((c, c), 0), _iota((c, c), 1)
    lower = ri >= ci
    strict = ri > ci
    eye = (ri == ci).astype(F32)
    lt = lower.astype(BF16)
    head_mean = (ones.astype(F32) * (1.0 / HEAD_DIM)).astype(BF16)
    eye_h = (_iota((HEAD_DIM, HEAD_DIM), 0) == _iota((HEAD_DIM, HEAD_DIM), 1)).astype(F32)
    units = []
    for c0 in range(0, ct, c):
        sl = slice(c0, c0 + c)
        gc = _dot_const(g[sl], lt, _NN, 3, const_left=True)
        gj_all = _dot_const(gc, head_mean, _NT, 3, const_left=True)
        kb = k[sl] * beta[sl]
        vb = v[sl] * beta[sl]
        e_gc = jnp.exp(gc)
        g_last = gc[c - 1:c, :]
        k_out = k[sl] * jnp.exp(g_last - gc)
        q_in = q[sl] * e_gc
        kbe = kb * e_gc
        e_last = jnp.exp(g_last)
        for h in range(N_HEADS):
            hs = slice(h * HEAD_DIM, (h + 1) * HEAD_DIM)
            diff = gc[:, hs][:, :c] - gj_all[h * HEAD_DIM:h * HEAD_DIM + c, :]
            decay = jnp.where(lower, jnp.exp(jnp.where(lower, diff, 0.0)), 0.0)
            units.append(dict(decay=decay, kb=kb[:, hs], q=q[sl, hs], k=k[sl, hs], vb=vb[:, hs], kbe=kbe[:, hs],
                              k_out=k_out[:, hs], q_in=q_in[:, hs], e_last=e_last[:, hs]))
    misc = lambda x, y, dims=_NN: _dotp(x, y, dims, P_MISC)
    inv = lambda x, y, dims=_NN: _dotp(x, y, dims, P_INV)
    for u in units:
        kq = misc(jnp.concatenate([u["kb"], u["q"]], axis=0), u["k"], _NT)
        u["a_mat"] = jnp.where(strict, kq[:c] * u["decay"], 0.0)
        u["qk"] = kq[c:] * u["decay"]
    for u, t_inv in zip(units, _inv_unit_lower_many([u["a_mat"] for u in units])):
        u["t_inv"] = _Split(t_inv, P_INV)
    for u in units:
        u["u"] = inv(u["t_inv"], u["vb"])
        u["w"] = inv(u["t_inv"], u["kbe"])
    for u in units:
        u["g_mat"] = eye_h * u["e_last"] - misc(u["k_out"], u["w"], _TN)
        u["h_mat"] = misc(u["k_out"], u["u"], _TN)
        u["q_hat"] = u["q_in"] - misc(u["qk"], u["w"])
        u["o1"] = misc(u["qk"], u["u"])
    states = [s_scr[h] for h in range(N_HEADS)]
    o_rows = []
    for ci_ in range(ct // c):
        o_heads = []
        for h in range(N_HEADS):
            u = units[ci_ * N_HEADS + h]
            zz = _dotp(jnp.concatenate([u["q_hat"], u["g_mat"]], axis=0), states[h], _NN, P_STATE)
            o_heads.append(zz[:c] + u["o1"])
            states[h] = zz[c:] + u["h_mat"]
        o_rows.append(jnp.concatenate(o_heads, axis=1))
    o = o_rows[0] if len(o_rows) == 1 else jnp.concatenate(o_rows, axis=0)
    for h in range(N_HEADS):
        s_scr[h] = states[h]
    o_ref[...] = _head_rms_finish(o, z, ones, ng_ref[...])

    @pl.when(j == last)
    def _():
        for h in range(N_HEADS):
            s_out_ref[h] = states[h]
        conv_out_ref[...] = raw[ct - SUBLANES:ct, :]


def _gdn_params(lp):
    return [lp["gdn_conv_w"], _row(jnp.repeat(lp["gdn_A_log"], HEAD_DIM)),
            _row(jnp.repeat(lp["gdn_dt_bias"], HEAD_DIM)), _row(jnp.tile(lp["gdn_norm_g"], N_HEADS))]


def _gdn_call(p_d, n_seq, seq_len, lp):
    ct = min(ROW_TILE, seq_len)
    p3 = p_d.reshape(n_seq, seq_len, D_PACK_W)
    params = _gdn_params(lp)
    o, s1, conv_tail = pl.pallas_call(
        _gdn_kernel,
        grid=(n_seq, seq_len // ct),
        in_specs=[pl.BlockSpec((None, ct, D_PACK_W), lambda b, j: (b, j, 0))]
        + [pl.BlockSpec(a.shape, lambda b, j: (0, 0)) for a in params],
        out_specs=[
            pl.BlockSpec((None, ct, W_MIX), lambda b, j: (b, j, 0)),
            pl.BlockSpec((None, N_HEADS, HEAD_DIM, HEAD_DIM), lambda b, j: (b, 0, 0, 0)),
            pl.BlockSpec((None, SUBLANES, D_QKV_W), lambda b, j: (b, 0, 0)),
        ],
        out_shape=[
            jax.ShapeDtypeStruct((n_seq, seq_len, W_MIX), F32),
            jax.ShapeDtypeStruct((n_seq, N_HEADS, HEAD_DIM, HEAD_DIM), F32),
            jax.ShapeDtypeStruct((n_seq, SUBLANES, D_QKV_W), F32),
        ],
        scratch_shapes=[pltpu.VMEM((ct + SUBLANES, D_QKV_W), F32),
                        pltpu.VMEM((N_HEADS, HEAD_DIM, HEAD_DIM), F32)],
        compiler_params=pltpu.CompilerParams(dimension_semantics=("parallel", "arbitrary"),
                                             vmem_limit_bytes=VMEM_LIMIT),
        name="gdn_prompt",
    )(p3, *params)
    return o.reshape(n_seq * seq_len, W_MIX), s1, conv_tail[:, SUBLANES - (CONV_W - 1):, :]


def _decode_pre_kernel(pa_ref, pb_ref, pc_ref, pd_ref, shift_ref, h0_ref, lconv_ref, gconv_ref, cos_ref, sin_ref,
                       mu_ref, w0_ref, w2_ref, a0_ref, a2_ref, kk_ref, ka_ref,
                       lcw_ref, lcb_ref, lgw_ref, lgb_ref, lsp_ref, gcw_ref, nal_ref, dtb_ref,
                       vt_ref, vn_ref, oc_ref, h1_ref, lconv1_ref, gconv1_ref):
    ones = _head_ones()
    pa_full = pa_ref[...]
    pa = pa_full[:, :A_SHIFT_W]
    pm = pa + (shift_ref[...] - pa) * mu_ref[...]
    r, k, v, ld, av, bv = _rwkv_token_math(pm, w0_ref[...], w2_ref[...], a0_ref[...], a2_ref[...],
                                           kk_ref[...], ka_ref[...], ones)
    vecs = [r, jnp.exp(ld), k, v, av, bv]
    plain = [r, k, v, pa_full[:, A_SHIFT_W:]]
    pb = pb_ref[...]
    cos, sin = cos_ref[...], sin_ref[...]
    vecs += [_rotary(pb[:, 0:W_MIX], cos, sin), _rotary(pb[:, W_MIX:2 * W_MIX], cos, sin) * (HEAD_DIM ** -0.5),
             pb[:, 2 * W_MIX:3 * W_MIX]]
    plain.append(pb[:, 3 * W_MIX:])
    pc = pc_ref[...]
    xr = pc[:, :W_MIX]
    taps = [lconv_ref[i] for i in range(CONV_W - 1)] + [xr]
    xc = taps[0] * lcw_ref[0:1, :]
    for i in range(1, CONV_W):
        xc = xc + taps[i] * lcw_ref[i:i + 1, :]
    xc = xc + lcb_ref[...]
    a, b = _lru_token_math(xc, lgw_ref[...], lgb_ref[...], lsp_ref[...])
    hcur = a * h0_ref[...] + b
    oc_ref[...] = hcur * _silu(pc[:, W_MIX:])
    h1_ref[...] = hcur
    for i in range(CONV_W - 1):
        lconv1_ref[i] = taps[i + 1]
    pd = pd_ref[...]
    raw = pd[:, :D_QKV_W]
    gtaps = [gconv_ref[i] for i in range(CONV_W - 1)] + [raw]
    qkv = gtaps[0] * gcw_ref[0:1, :]
    for i in range(1, CONV_W):
        qkv = qkv + gtaps[i] * gcw_ref[i:i + 1, :]
    q, kg, vg, beta, g = _gdn_token_math(qkv, pd[:, D_QKV_W + W_MIX:D_QKV_W + 2 * W_MIX],
                                         pd[:, D_QKV_W + 2 * W_MIX:], nal_ref[...], dtb_ref[...], ones)
    for i in range(CONV_W - 1):
        gconv1_ref[i] = gtaps[i + 1]
    vecs += [q, kg, vg, beta, g]
    plain.append(pd[:, D_QKV_W:D_QKV_W + W_MIX])
    assert len(vecs) == N_VEC_T and len(plain) == N_VEC_PLAIN
    for i, vec in enumerate(vecs):
        vt_ref[i] = vec.T
    for i, vec in enumerate(plain):
        vn_ref[i] = vec


def _decode_state_kernel(vt_ref, wkv_ref, ret_ref, gdn_ref, gam_ref, wkv1_ref, ret1_ref, gdn1_ref, o_ref):
    v_r, v_w, v_k, v_v, v_a, v_b, r_q, r_k, r_v, g_q, g_k, g_v, g_beta, g_g = range(N_VEC_T)
    hd = HEAD_DIM
    n = vt_ref.shape[-1]
    row = lambda idx, i: vt_ref[idx, pl.ds(i, 1), :]
    rows_of = lambda i: pl.ds(pl.multiple_of(i * hd, hd), hd)
    gamma = gam_ref[...]
    beta = vt_ref[g_beta, 0:1, :]
    eg = jnp.exp(vt_ref[g_g, 0:1, :])

    def first_pass(i, carry):
        acc_ret, acc_w, acc_q = carry
        rows = rows_of(i)
        s = wkv_ref[rows, :]
        sa = jnp.sum(s * vt_ref[v_a], axis=0, keepdims=True)
        s = s * vt_ref[v_w] + sa * vt_ref[v_b] + row(v_v, i) * vt_ref[v_k]
        wkv1_ref[rows, :] = s
        o_ref[0, pl.ds(i, 1), :] = jnp.sum(s * vt_ref[v_r], axis=0, keepdims=True)
        s = ret_ref[rows, :] * gamma + row(r_k, i) * vt_ref[r_v]
        ret1_ref[rows, :] = s
        acc_ret = acc_ret + row(r_q, i) * s
        s = gdn_ref[rows, :]
        return acc_ret, acc_w + row(g_k, i) * s, acc_q + row(g_q, i) * s

    zeros = jnp.zeros((hd, n), F32)
    acc_ret, acc_w, acc_q = lax.fori_loop(0, hd, first_pass, (zeros, zeros, zeros))
    o_ref[1] = acc_ret
    v_new = vt_ref[g_v] * beta - acc_w * (beta * eg)
    qk = jnp.sum(vt_ref[g_q] * vt_ref[g_k], axis=0, keepdims=True)
    o_ref[2] = acc_q * eg + qk * v_new

    def second_pass(i, carry):
        rows = rows_of(i)
        gdn1_ref[rows, :] = gdn_ref[rows, :] * eg + row(g_k, i) * v_new
        return carry

    lax.fori_loop(0, hd, second_pass, 0)


def _decode_finish_kernel(ot_ref, vn_ref, rk_ref, lng_ref, lnb_ref, ng_ref, oa_ref, ob_ref, od_ref):
    ones = _head_ones()
    r, k, v, z_a, z_b, z_d = (vn_ref[i] for i in range(N_VEC_PLAIN))
    oa_ref[...] = _rwkv_finish(ot_ref[0].T, r, k, v, z_a, rk_ref[...], lng_ref[...], lnb_ref[...], ones)
    ob_ref[...] = _head_rms_finish(ot_ref[1].T, z_b, ones)
    od_ref[...] = _head_rms_finish(ot_ref[2].T, z_d, ones, ng_ref[...])


def _batch_minor(state):
    n_layers, n = state.shape[:2]
    return jnp.transpose(state, (0, 2, 3, 4, 1)).reshape(n_layers, -1, n)


def _batch_major(flat_state):
    n = flat_state.shape[-1]
    return jnp.transpose(flat_state.reshape(N_HEADS, HEAD_DIM, HEAD_DIM, n), (3, 0, 1, 2))


def _decode_layer(l, p_a, p_b, p_c, p_d, carried, cos_t, sin_t, lp):
    n = p_a.shape[0]
    assert n % LANES == 0, "the decode state kernel keeps the batch on lanes"
    flat = HEAD_DIM * HEAD_DIM
    taps = CONV_W - 1
    rwkv_params = [_row(lp["rwkv_mu"]), _row(lp["rwkv_w0"]), lp["rwkv_w2"], _row(lp["rwkv_a0"]), lp["rwkv_a2"],
                   _row(lp["rwkv_k_k"]), _row(lp["rwkv_k_a"])]
    gdn_params = _gdn_params(lp)
    full = lambda a: pl.BlockSpec(a.shape, lambda i: (0,) * a.ndim)
    layer_blk = lambda a: pl.BlockSpec((None,) + a.shape[1:], lambda i, nd=a.ndim: (l,) + (0,) * (nd - 1))
    projs = [p_a, p_b, p_c, p_d]
    layered = [carried["shift"], carried["lru_h"], carried["lru_conv"], carried["gdn_conv"]]
    consts = [cos_t, sin_t, *rwkv_params, *_lru_params(lp), *gdn_params[:3]]
    out_shapes = [
        jax.ShapeDtypeStruct((N_VEC_T, W_MIX, n), F32),
        jax.ShapeDtypeStruct((N_VEC_PLAIN, n, W_MIX), F32),
        jax.ShapeDtypeStruct((n, W_MIX), F32),
        jax.ShapeDtypeStruct((n, W_MIX), F32),
        jax.ShapeDtypeStruct((taps, n, W_MIX), F32),
        jax.ShapeDtypeStruct((taps, n, D_QKV_W), F32),
    ]
    vec_t, vec_n, o_c, lru_h1, lru_conv1, gdn_conv1 = pl.pallas_call(
        _decode_pre_kernel,
        grid=(1,),
        in_specs=[full(a) for a in projs] + [layer_blk(a) for a in layered] + [full(a) for a in consts],
        out_specs=[pl.BlockSpec(s.shape, lambda i, nd=len(s.shape): (0,) * nd) for s in out_shapes],
        out_shape=out_shapes,
        compiler_params=pltpu.CompilerParams(dimension_semantics=("arbitrary",), vmem_limit_bytes=VMEM_LIMIT),
        name="decode_tokens",
    )(*projs, *layered, *consts)

    gam = jnp.broadcast_to((1.0 - 2.0 ** (-5.0 - jnp.arange(N_HEADS, dtype=F32)))[:, None, None], (N_HEADS, 1, n))
    state_in = pl.BlockSpec((None, flat, n), lambda h: (l, h, 0))
    state_out = pl.BlockSpec((flat, n), lambda h: (h, 0))
    wkv1, ret1, gdn1, o_t = pl.pallas_call(
        _decode_state_kernel,
        grid=(N_HEADS,),
        in_specs=[pl.BlockSpec((N_VEC_T, HEAD_DIM, n), lambda h: (0, h, 0)), state_in, state_in, state_in,
                  pl.BlockSpec((None, 1, n), lambda h: (h, 0, 0))],
        out_specs=[state_out, state_out, state_out, pl.BlockSpec((3, HEAD_DIM, n), lambda h: (0, h, 0))],
        out_shape=[jax.ShapeDtypeStruct((N_HEADS * flat, n), F32)] * 3 + [jax.ShapeDtypeStruct((3, W_MIX, n), F32)],
        compiler_params=pltpu.CompilerParams(dimension_semantics=("parallel",), vmem_limit_bytes=VMEM_LIMIT),
        name="decode_states",
    )(vec_t, carried["wkv"], carried["ret"], carried["gdn"], gam)

    finish_ins = [o_t, vec_n, _row(lp["rwkv_r_k"]), _row(lp["rwkv_ln_g"]), _row(lp["rwkv_ln_b"]), gdn_params[3]]
    o_a, o_b, o_d = pl.pallas_call(
        _decode_finish_kernel,
        grid=(1,),
        in_specs=[full(a) for a in finish_ins],
        out_specs=[pl.BlockSpec((n, W_MIX), lambda i: (0, 0))] * 3,
        out_shape=[jax.ShapeDtypeStruct((n, W_MIX), F32)] * 3,
        compiler_params=pltpu.CompilerParams(dimension_semantics=("arbitrary",), vmem_limit_bytes=VMEM_LIMIT),
        name="decode_finish",
    )(*finish_ins)
    new_states = (_batch_major(wkv1), p_a[:, :A_SHIFT_W], _batch_major(ret1), lru_h1,
                  jnp.transpose(lru_conv1, (1, 0, 2)), _batch_major(gdn1), jnp.transpose(gdn_conv1, (1, 0, 2)))
    return (o_a, o_b, o_c, o_d), new_states


def _pack_in_weights(w_in):
    off_d = A_W + B_W + C_W
    off_ba = off_d + D_QKV_W
    off_z = off_ba + 2 * N_HEADS
    off_g = off_d + D_W
    w_pack = jnp.concatenate([
        w_in[:, :off_ba],
        w_in[:, off_z:off_g],
        jnp.repeat(w_in[:, off_ba:off_ba + N_HEADS], HEAD_DIM, axis=1),
        jnp.repeat(w_in[:, off_ba + N_HEADS:off_z], HEAD_DIM, axis=1),
    ], axis=1)
    return w_pack, w_in[:, off_g:]


def _prompt_layer(p_a, p_b, p_c, p_d, n_seq, seq_len, cos_t, sin_t, lp):
    o_a, wkv1, shift1 = _rwkv_call(p_a, n_seq, seq_len, lp)
    o_b, ret1 = _ret_call(p_b, n_seq, seq_len, cos_t, sin_t)
    o_c, lru_h1, lru_conv1 = _lru_call(p_c, n_seq, seq_len, lp)
    o_d, gdn1, gdn_conv1 = _gdn_call(p_d, n_seq, seq_len, lp)
    return (o_a, o_b, o_c, o_d), (wkv1, shift1, ret1, lru_h1, lru_conv1, gdn1, gdn_conv1)


def _run_group(x, mods, pos, carried, layers, final_g):
    n_seq, seq_len, d = x.shape
    x2 = x.reshape(n_seq * seq_len, d)
    cos_t, sin_t = _rope_tables(pos)
    new = []
    n_layers = len(layers)
    for l, lp in enumerate(layers):
        shift, scale, gate = (mods[l][:, i * d:(i + 1) * d] for i in range(3))
        p_a, p_b, p_c, p_d = _inproj_call(x2, scale, shift, lp["norm_g"], lp["w_pack"], seq_len)
        if carried is None:
            branches, st = _prompt_layer(p_a, p_b, p_c, p_d, n_seq, seq_len, cos_t, sin_t, lp)
        else:
            branches, st = _decode_layer(l, p_a, p_b, p_c, p_d, carried, cos_t, sin_t, lp)
        new.append(st)
        x2 = _outproj_call(x2, scale, shift, gate, lp["norm_g"], branches, lp["w_gate"], lp["w_up_bf16"],
                           lp["w_out_bf16"], final_g, seq_len, final=(l == n_layers - 1))
    stacked = tuple(jnp.stack([s[i] for s in new], axis=0) for i in range(7))
    return x2.reshape(n_seq, seq_len, d), stacked


def kernel(x_prompt, x_sample, c_prompt, c_sample, state_rwkv_wkv, state_rwkv_shift, state_ret, state_lru_h, state_lru_conv, state_gdn, state_gdn_conv, ada_w, ada_b, norm_g, w_in, rwkv_mu, rwkv_w0, rwkv_w2, rwkv_a0, rwkv_a2, rwkv_k_k, rwkv_k_a, rwkv_r_k, rwkv_ln_g, rwkv_ln_b, lru_conv_w, lru_conv_b, lru_gate_w, lru_gate_b, lru_lambda, gdn_conv_w, gdn_A_log, gdn_dt_bias, gdn_norm_g, w_up, w_out, final_g):
    n_layers = ada_w.shape[0]
    n_prompt, seq_len, _ = x_prompt.shape
    n_sample, dec_len, _ = x_sample.shape
    assert dec_len == 1, "the decode path handles one token per sequence"
    layers = []
    w_in_bf16 = w_in.astype(BF16)
    for l in range(n_layers):
        w_pack, w_gate = _pack_in_weights(w_in_bf16[l])
        layers.append(dict(
            norm_g=norm_g[l], w_pack=w_pack, w_gate=w_gate, w_up_bf16=w_up[l].astype(BF16),
            w_out_bf16=w_out[l].astype(BF16),
            rwkv_mu=rwkv_mu[l], rwkv_w0=rwkv_w0[l], rwkv_w2=rwkv_w2[l], rwkv_a0=rwkv_a0[l], rwkv_a2=rwkv_a2[l],
            rwkv_k_k=rwkv_k_k[l], rwkv_k_a=rwkv_k_a[l], rwkv_r_k=rwkv_r_k[l], rwkv_ln_g=rwkv_ln_g[l],
            rwkv_ln_b=rwkv_ln_b[l], lru_conv_w=lru_conv_w[l], lru_conv_b=lru_conv_b[l],
            lru_gate_w=lru_gate_w[l], lru_gate_b=lru_gate_b[l], lru_lambda=lru_lambda[l],
            gdn_conv_w=gdn_conv_w[l], gdn_A_log=gdn_A_log[l], gdn_dt_bias=gdn_dt_bias[l],
            gdn_norm_g=gdn_norm_g[l]))
    mods = _ada_call(jnp.concatenate([c_prompt, c_sample], axis=0), ada_w, ada_b)
    mods_p = [mods[l, :n_prompt] for l in range(n_layers)]
    mods_s = [mods[l, n_prompt:] for l in range(n_layers)]

    y_prompt, new_p = _run_group(x_prompt, mods_p, jnp.arange(seq_len, dtype=jnp.int32), None, layers, final_g)
    carried = dict(wkv=_batch_minor(state_rwkv_wkv), ret=_batch_minor(state_ret), gdn=_batch_minor(state_gdn),
                   shift=state_rwkv_shift, lru_h=state_lru_h,
                   lru_conv=jnp.transpose(state_lru_conv, (0, 2, 1, 3)),
                   gdn_conv=jnp.transpose(state_gdn_conv, (0, 2, 1, 3)))
    pos_s = PAST_LEN + jnp.arange(dec_len, dtype=jnp.int32)
    y_sample, new_s = _run_group(x_sample, mods_s, pos_s, carried, layers, final_g)
    return (y_prompt, y_sample) + new_p + new_s
```

```python
import functools
import math

import jax
import jax.numpy as jnp
from jax import lax
from jax.experimental import pallas as pl
from jax.experimental.pallas import tpu as pltpu

F32 = jnp.float32
BF16 = jnp.bfloat16
HI = lax.Precision.HIGHEST

N_HEADS = 4
HEAD_DIM = 64
W_MIX = N_HEADS * HEAD_DIM
LORA = 64
CONV_W = 4
N_BRANCH = 4
LRU_C = 8.0
ROPE_BASE = 10000.0
EPS = 1e-6
RWKV_GN_EPS = 64e-5
PAST_LEN = 16384
A_SHIFT_W = 3 * W_MIX + 2 * LORA
A_W = A_SHIFT_W + W_MIX
B_W = 4 * W_MIX
C_W = 2 * W_MIX
D_QKV_W = 3 * W_MIX
D_W = D_QKV_W + 2 * N_HEADS + W_MIX
D_PACK_W = D_QKV_W + 3 * W_MIX

SUBLANES = 8
LANES = 128
VMEM_LIMIT = 56 * 1024 * 1024

CHUNK = 64
RET_CHUNK = 128
INV_BLOCK = 16
ROW_TILE = 512
PROJ_TILE = 512
N_VEC_T = 14
N_VEC_PLAIN = 6


def _mm(a, b, prec=HI):
    return lax.dot_general(a, b, (((1,), (0,)), ((), ())), precision=prec, preferred_element_type=F32)


def _mm_nt(a, b, prec=HI):
    return lax.dot_general(a, b, (((1,), (1,)), ((), ())), precision=prec, preferred_element_type=F32)


def _mm_tn(a, b, prec=HI):
    return lax.dot_general(a, b, (((0,), (0,)), ((), ())), precision=prec, preferred_element_type=F32)


_NN = (((1,), (0,)), ((), ()))
_NT = (((1,), (1,)), ((), ()))
_TN = (((0,), (0,)), ((), ()))

P_INV = 1
P_STATE = 1
P_MISC = 1


class _Split:
    def __init__(self, x, passes):
        self.hi = x.astype(BF16)
        self.lo = (x - self.hi.astype(F32)).astype(BF16) if passes > 1 else None


def _dotp(a, b, dims=_NN, passes=1):
    a = a if isinstance(a, _Split) else _Split(a, passes)
    b = b if isinstance(b, _Split) else _Split(b, passes)
    d = lambda x, y: lax.dot_general(x, y, dims, preferred_element_type=F32)
    out = d(a.hi, b.hi)
    if passes > 1:
        out = out + (d(a.hi, b.lo) + d(a.lo, b.hi))
    return out


def _iota(shape, dim):
    return lax.broadcasted_iota(jnp.int32, shape, dim)


def _silu(x):
    return x * jax.nn.sigmoid(x)


def _softplus(x):
    return jnp.maximum(x, 0.0) + jnp.log1p(jnp.exp(-jnp.abs(x)))


def _pieces(x, n):
    out = []
    for i in range(n):
        p = x.astype(BF16)
        out.append(p)
        if i + 1 < n:
            x = x - p.astype(F32)
    return out


def _dot_const(x, const, dims=_NN, n=2, const_left=False):
    out = None
    for p in _pieces(x, n):
        t = lax.dot_general(*((const, p) if const_left else (p, const)), dims, preferred_element_type=F32)
        out = t if out is None else out + t
    return out


def _head_ones():
    return (_iota((W_MIX, W_MIX), 0) // HEAD_DIM == _iota((W_MIX, W_MIX), 1) // HEAD_DIM).astype(BF16)


def _head_sum(x, ones):
    return _dot_const(x, ones)


def _rms(x):
    return x * lax.rsqrt(jnp.mean(x * x, axis=-1, keepdims=True) + EPS)


def _inv_unit_lower(a):
    return _inv_unit_lower_many([a])[0]


def _inv_unit_lower_many(mats):
    n = mats[0].shape[0]
    ri, ci = _iota((n, n), 0), _iota((n, n), 1)
    eye = (ri == ci).astype(F32)
    diag_blk = (ri // INV_BLOCK) == (ci // INV_BLOCK)
    mm = lambda x, y: _dotp(x, y, _NN, P_INV)
    sp = lambda x: _Split(x, P_INV)
    d = [jnp.where(diag_blk, a, 0.0) for a in mats]
    nb = [a - di for a, di in zip(mats, d)]
    td = [eye - di for di in d]
    p = d
    for _ in range(int(math.log2(INV_BLOCK)) - 1):
        ps = [sp(pi) for pi in p]
        p = [mm(pi, pi) for pi in ps]
        td = [mm(ti, eye + pi) for ti, pi in zip(td, p)]
    tds = [sp(ti) for ti in td]
    x = [mm(ti, ni) for ti, ni in zip(tds, nb)]
    t = [eye - xi for xi in x]
    p = x
    for _ in range(int(math.log2(n // INV_BLOCK)) - 1):
        ps = [sp(pi) for pi in p]
        p = [mm(pi, pi) for pi in ps]
        t = [mm(ti, eye + pi) for ti, pi in zip(t, p)]
    return [mm(ti, tdi) for ti, tdi in zip(t, tds)]


def _ada_kernel(c_ref, w_ref, b_ref, o_ref):
    o_ref[...] = _mm(_silu(c_ref[...]), w_ref[...]) + b_ref[...]


def _ada_call(c_all, ada_w, ada_b):
    n_layers, d, d3 = ada_w.shape
    rows = c_all.shape[0]
    return pl.pallas_call(
        _ada_kernel,
        grid=(n_layers, d3 // d),
        in_specs=[
            pl.BlockSpec((rows, d), lambda l, j: (0, 0)),
            pl.BlockSpec((None, d, d), lambda l, j: (l, 0, j)),
            pl.BlockSpec((None, 1, d), lambda l, j: (l, 0, j)),
        ],
        out_specs=pl.BlockSpec((None, rows, d), lambda l, j: (l, 0, j)),
        out_shape=jax.ShapeDtypeStruct((n_layers, rows, d3), F32),
        compiler_params=pltpu.CompilerParams(dimension_semantics=("arbitrary", "arbitrary"),
                                             vmem_limit_bytes=VMEM_LIMIT),
        name="ada_mod",
    )(c_all, ada_w, ada_b.reshape(n_layers, 1, d3))


def _modulated_norm(x, g, scale, shift):
    return _rms(x) * g * (1.0 + scale) + shift


def _inproj_kernel(x_ref, sc_ref, sh_ref, g_ref, w_ref, oa_ref, ob_ref, oc_ref, od_ref):
    h = _modulated_norm(x_ref[...], g_ref[...], sc_ref[...], sh_ref[...]).astype(BF16)
    lo = 0
    for o_ref in (oa_ref, ob_ref, oc_ref, od_ref):
        wd = o_ref.shape[-1]
        o_ref[...] = jnp.dot(h, w_ref[:, lo:lo + wd], preferred_element_type=F32)
        lo += wd


def _mod_specs(mods, tm, seq_len):
    d = mods[0].shape[-1]
    if seq_len == 1:
        return [m for m in mods], [pl.BlockSpec((tm, d), lambda i: (i, 0)) for _ in mods]
    per_seq = seq_len // tm
    return ([m.reshape(m.shape[0], 1, d) for m in mods],
            [pl.BlockSpec((None, 1, d), lambda i: (i // per_seq, 0, 0)) for _ in mods])


def _inproj_call(x2, scale, shift, g, w_pack, seq_len):
    m, d = x2.shape
    tm = min(PROJ_TILE, m, seq_len) if seq_len > 1 else m
    widths = (A_W, B_W, C_W, D_PACK_W)
    mods, mod_specs = _mod_specs((scale, shift), tm, seq_len)
    return pl.pallas_call(
        _inproj_kernel,
        grid=(m // tm,),
        in_specs=[pl.BlockSpec((tm, d), lambda i: (i, 0))] + mod_specs + [
            pl.BlockSpec((1, d), lambda i: (0, 0)),
            pl.BlockSpec(w_pack.shape, lambda i: (0, 0)),
        ],
        out_specs=[pl.BlockSpec((tm, wd), lambda i: (i, 0)) for wd in widths],
        out_shape=[jax.ShapeDtypeStruct((m, wd), F32) for wd in widths],
        compiler_params=pltpu.CompilerParams(dimension_semantics=("parallel",), vmem_limit_bytes=VMEM_LIMIT),
        name="in_proj",
    )(x2, *mods, g.reshape(1, d), w_pack)


def _outproj_kernel(x_ref, sc_ref, sh_ref, gt_ref, g_ref, ba_ref, bb_ref, bc_ref, bd_ref,
                    wg_ref, wup_ref, wout_ref, fg_ref, o_ref, *, final):
    x = x_ref[...]
    d = x.shape[-1]
    h = _modulated_norm(x, g_ref[...], sc_ref[...], sh_ref[...]).astype(BF16)
    merged = jnp.zeros(x.shape, F32)
    for n, br_ref in enumerate((ba_ref, bb_ref, bc_ref, bd_ref)):
        gl = jnp.dot(h, wg_ref[:, n * d:(n + 1) * d], preferred_element_type=F32)
        up = jnp.dot(br_ref[...].astype(BF16), wup_ref[n], preferred_element_type=F32)
        merged = merged + jax.nn.sigmoid(gl) * up
    out = jnp.dot(merged.astype(BF16), wout_ref[...], preferred_element_type=F32)
    xn = x + gt_ref[...] * out
    if final:
        xn = _rms(xn) * fg_ref[...]
    o_ref[...] = xn


def _outproj_call(x2, scale, shift, gate, g, branches, wg, wup, wout, final_g, seq_len, final):
    m, d = x2.shape
    tm = min(PROJ_TILE, m, seq_len) if seq_len > 1 else m
    mods, mod_specs = _mod_specs((scale, shift, gate), tm, seq_len)
    full = lambda a: pl.BlockSpec(a.shape, lambda i: (0,) * a.ndim)
    return pl.pallas_call(
        functools.partial(_outproj_kernel, final=final),
        grid=(m // tm,),
        in_specs=[pl.BlockSpec((tm, d), lambda i: (i, 0))] + mod_specs + [pl.BlockSpec((1, d), lambda i: (0, 0))]
        + [pl.BlockSpec((tm, W_MIX), lambda i: (i, 0)) for _ in branches]
        + [full(wg), full(wup), full(wout), pl.BlockSpec((1, d), lambda i: (0, 0))],
        out_specs=pl.BlockSpec((tm, d), lambda i: (i, 0)),
        out_shape=jax.ShapeDtypeStruct((m, d), F32),
        compiler_params=pltpu.CompilerParams(dimension_semantics=("parallel",), vmem_limit_bytes=VMEM_LIMIT),
        name="out_proj",
    )(x2, *mods, g.reshape(1, d), *branches, wg, wup, wout, final_g.reshape(1, d))


def _rwkv_token_math(pm, w0, w2, a0, a2, k_k, k_a, ones):
    r = pm[:, 0:W_MIX]
    k = pm[:, W_MIX:2 * W_MIX]
    v = pm[:, 2 * W_MIX:3 * W_MIX]
    wd = pm[:, 3 * W_MIX:3 * W_MIX + LORA]
    ad = pm[:, 3 * W_MIX + LORA:]
    w_log = -_softplus(-(w0 + _dotp(jnp.tanh(wd), w2, _NN, P_MISC))) - 0.5
    log_decay = -jnp.exp(w_log)
    a = jax.nn.sigmoid(a0 + _dotp(ad, a2, _NN, P_MISC))
    kx = k * k_k
    kk = kx * lax.rsqrt(_head_sum(kx * kx, ones) + EPS)
    k = k * (1.0 + (a - 1.0) * k_a)
    return r, k, v, log_decay, -kk, kk * a


def _rwkv_finish(o, r, k, v, z, r_k, ln_g, ln_b, ones):
    mean = _head_sum(o, ones) * (1.0 / HEAD_DIM)
    dlt = o - mean
    var = _head_sum(dlt * dlt, ones) * (1.0 / HEAD_DIM)
    on = dlt * lax.rsqrt(var + RWKV_GN_EPS) * ln_g + ln_b
    bonus = _head_sum(r * k * r_k, ones) * v
    return (on + bonus) * _silu(z)


def _swap_halves(x):
    half = HEAD_DIM // 2
    n = x.shape[-1]
    first = (_iota(x.shape, 1) & half) == 0
    return jnp.where(first, pltpu.roll(x, n - half, axis=1), pltpu.roll(x, half, axis=1))


def _rotary(x, cos, sin):
    return x * cos + _swap_halves(x) * sin


def _lru_token_math(xc, gate_w, gate_b, lam):
    gates = _dotp(xc, gate_w, _NN, P_MISC) + gate_b
    r_gate = jax.nn.sigmoid(gates[:, :W_MIX])
    i_gate = jax.nn.sigmoid(gates[:, W_MIX:])
    log_a = -LRU_C * r_gate * _softplus(-lam)
    a = jnp.exp(log_a)
    b = jnp.sqrt(1.0 - jnp.exp(2.0 * log_a)) * (i_gate * xc)
    return a, b


def _gdn_token_math(qkv, b_raw, a_raw, a_log, dt_bias, ones):
    qkv = _silu(qkv)
    q = qkv[:, 0:W_MIX]
    k = qkv[:, W_MIX:2 * W_MIX]
    v = qkv[:, 2 * W_MIX:]
    q = q * lax.rsqrt(_head_sum(q * q, ones) + EPS) * (HEAD_DIM ** -0.5)
    k = k * lax.rsqrt(_head_sum(k * k, ones) + EPS)
    beta = jax.nn.sigmoid(b_raw)
    g = -jnp.exp(a_log) * _softplus(a_raw + dt_bias)
    return q, k, v, beta, g


def _head_rms_finish(o, z, ones, gain=None):
    y = o * lax.rsqrt(_head_sum(o * o, ones) * (1.0 / HEAD_DIM) + EPS)
    if gain is not None:
        y = y * gain
    return y * _silu(z)


def _conv_tile(u, ext_ref, w_ref, first):
    n = u.shape[0]

    @pl.when(first)
    def _():
        ext_ref[0:SUBLANES, :] = jnp.zeros((SUBLANES, u.shape[1]), F32)

    ext_ref[SUBLANES:SUBLANES + n, :] = u
    out = None
    for j in range(CONV_W):
        back = CONV_W - 1 - j
        term = ext_ref[SUBLANES - back:SUBLANES - back + n, :] * w_ref[j:j + 1, :]
        out = term if out is None else out + term
    ext_ref[0:SUBLANES, :] = u[n - SUBLANES:n, :]
    return out


def _rwkv_kernel(p_ref, mu_ref, w0_ref, w2_ref, a0_ref, a2_ref, kk_ref, ka_ref, rk_ref, lng_ref, lnb_ref,
                 o_ref, s_out_ref, shift_out_ref, s_scr, prev_scr):
    j = pl.program_id(1)
    last = pl.num_programs(1) - 1
    ct = p_ref.shape[0]

    @pl.when(j == 0)
    def _():
        s_scr[...] = jnp.zeros(s_scr.shape, F32)
        prev_scr[...] = jnp.zeros(prev_scr.shape, F32)

    p = p_ref[...]
    pa = p[:, :A_SHIFT_W]
    z = p[:, A_SHIFT_W:]
    rows = _iota((ct, 1), 0)
    prev = jnp.where(rows == 0, prev_scr[...], pltpu.roll(pa, 1, axis=0))
    prev_scr[...] = pa[ct - 1:ct, :]
    pm = pa + (prev - pa) * mu_ref[...]
    ones = _head_ones()
    r, k, v, ld, av, bv = _rwkv_token_math(pm, w0_ref[...], w2_ref[...], a0_ref[...], a2_ref[...],
                                           kk_ref[...], ka_ref[...], ones)

    c = min(CHUNK, ct)
    ri, ci = _iota((c, c), 0), _iota((c, c), 1)
    strict = ri > ci
    incl = ri >= ci
    eye = (ri == ci).astype(F32)
    lt = incl.astype(BF16)
    units = []
    for c0 in range(0, ct, c):
        sl = slice(c0, c0 + c)
        ldc = ld[sl]
        cum = _dot_const(ldc, lt, _NN, 3, const_left=True)
        e_neg = jnp.exp(-cum)
        e_out = jnp.exp(cum[c - 1:c, :] - cum)
        at = av[sl] * jnp.exp(cum - ldc)
        rt = r[sl] * jnp.exp(cum)
        bt, kt = bv[sl] * e_neg, k[sl] * e_neg
        bo, ko = bv[sl] * e_out, k[sl] * e_out
        g_end = jnp.exp(cum[c - 1:c, :])
        vc = v[sl]
        for h in range(N_HEADS):
            hs = slice(h * HEAD_DIM, (h + 1) * HEAD_DIM)
            units.append(dict(a=at[:, hs], r=rt[:, hs], b=bt[:, hs], k=kt[:, hs], bo=bo[:, hs], ko=ko[:, hs],
                              v=vc[:, hs], g=g_end[:, hs]))
    misc = lambda x, y, dims=_NN: _dotp(x, y, dims, P_MISC)
    inv = lambda x, y, dims=_NN: _dotp(x, y, dims, P_INV)
    for u in units:
        lhs = jnp.concatenate([u["a"], u["r"]], axis=0)
        u["mb"] = misc(lhs, u["b"], _NT)
        u["mk"] = misc(lhs, u["k"], _NT)
    for u in units:
        u["m_ab"] = jnp.where(strict, u["mb"][:c], 0.0)
        u["m_ak"] = jnp.where(strict, u["mk"][:c], 0.0)
        u["m_rb"] = jnp.where(incl, u["mb"][c:], 0.0)
        u["m_rk"] = jnp.where(incl, u["mk"][c:], 0.0)
    for u, t_inv in zip(units, _inv_unit_lower_many([-u["m_ab"] for u in units])):
        u["t_inv"] = _Split(t_inv, P_INV)
    for u in units:
        u["makv"] = misc(u["m_ak"], u["v"])
    for u in units:
        u["a_hat"] = inv(u["t_inv"], u["a"])
        u["u1"] = inv(u["t_inv"], u["makv"])
    for u in units:
        u["r_hat"] = u["r"] + misc(u["m_rb"], u["a_hat"])
        u["o1"] = misc(u["m_rb"], u["u1"]) + misc(u["m_rk"], u["v"])
        u["g_t"] = eye * u["g"] + misc(u["bo"], u["a_hat"], _TN)
        u["h_t"] = misc(u["bo"], u["u1"], _TN) + misc(u["ko"], u["v"], _TN)
    states = [s_scr[h] for h in range(N_HEADS)]
    o_rows = []
    for ci_ in range(ct // c):
        o_heads = []
        for h in range(N_HEADS):
            u = units[ci_ * N_HEADS + h]
            zz = _dotp(jnp.concatenate([u["r_hat"], u["g_t"]], axis=0), states[h], _NN, P_STATE)
            o_heads.append(zz[:c] + u["o1"])
            states[h] = zz[c:] + u["h_t"]
        o_rows.append(jnp.concatenate(o_heads, axis=1))
    o = o_rows[0] if len(o_rows) == 1 else jnp.concatenate(o_rows, axis=0)
    for h in range(N_HEADS):
        s_scr[h] = states[h]
    o_ref[...] = _rwkv_finish(o, r, k, v, z, rk_ref[...], lng_ref[...], lnb_ref[...], ones)

    @pl.when(j == last)
    def _():
        eye_h = (_iota((HEAD_DIM, HEAD_DIM), 0) == _iota((HEAD_DIM, HEAD_DIM), 1)).astype(F32)
        for h in range(N_HEADS):
            s_out_ref[h] = _mm_nt(eye_h, states[h])
        shift_out_ref[...] = pa[ct - 1:ct, :]


def _row(a):
    return a.reshape(1, -1)


def _rwkv_call(p_a, n_seq, seq_len, lp):
    ct = min(ROW_TILE, seq_len)
    p3 = p_a.reshape(n_seq, seq_len, A_W)
    params = [_row(lp["rwkv_mu"]), _row(lp["rwkv_w0"]), lp["rwkv_w2"], _row(lp["rwkv_a0"]), lp["rwkv_a2"],
              _row(lp["rwkv_k_k"]), _row(lp["rwkv_k_a"]), _row(lp["rwkv_r_k"]), _row(lp["rwkv_ln_g"]),
              _row(lp["rwkv_ln_b"])]
    o, s1, shift1 = pl.pallas_call(
        _rwkv_kernel,
        grid=(n_seq, seq_len // ct),
        in_specs=[pl.BlockSpec((None, ct, A_W), lambda b, j: (b, j, 0))]
        + [pl.BlockSpec(a.shape, lambda b, j: (0, 0)) for a in params],
        out_specs=[
            pl.BlockSpec((None, ct, W_MIX), lambda b, j: (b, j, 0)),
            pl.BlockSpec((None, N_HEADS, HEAD_DIM, HEAD_DIM), lambda b, j: (b, 0, 0, 0)),
            pl.BlockSpec((None, 1, A_SHIFT_W), lambda b, j: (b, 0, 0)),
        ],
        out_shape=[
            jax.ShapeDtypeStruct((n_seq, seq_len, W_MIX), F32),
            jax.ShapeDtypeStruct((n_seq, N_HEADS, HEAD_DIM, HEAD_DIM), F32),
            jax.ShapeDtypeStruct((n_seq, 1, A_SHIFT_W), F32),
        ],
        scratch_shapes=[pltpu.VMEM((N_HEADS, HEAD_DIM, HEAD_DIM), F32), pltpu.VMEM((1, A_SHIFT_W), F32)],
        compiler_params=pltpu.CompilerParams(dimension_semantics=("parallel", "arbitrary"),
                                             vmem_limit_bytes=VMEM_LIMIT),
        name="rwkv7_prompt",
    )(p3, *params)
    return o.reshape(n_seq * seq_len, W_MIX), s1, shift1.reshape(n_seq, A_SHIFT_W)


def _ret_kernel(p_ref, cos_ref, sin_ref, o_ref, s_out_ref, s_scr):
    j = pl.program_id(1)
    last = pl.num_programs(1) - 1
    ct = p_ref.shape[0]

    @pl.when(j == 0)
    def _():
        s_scr[...] = jnp.zeros(s_scr.shape, F32)

    p = p_ref[...]
    cos, sin = cos_ref[...], sin_ref[...]
    q = _rotary(p[:, 0:W_MIX], cos, sin)
    k = _rotary(p[:, W_MIX:2 * W_MIX], cos, sin) * (HEAD_DIM ** -0.5)
    v = p[:, 2 * W_MIX:3 * W_MIX]
    z = p[:, 3 * W_MIX:]
    c = min(RET_CHUNK, ct)
    ri, ci = _iota((c, c), 0), _iota((c, c), 1)
    causal = ri >= ci
    rel = jnp.where(causal, ri - ci, 0).astype(F32)
    idx = _iota((c, 1), 0).astype(F32)
    states = [s_scr[h] for h in range(N_HEADS)]
    o_heads_all = [[] for _ in range(N_HEADS)]
    for h in range(N_HEADS):
        lg = math.log(1.0 - 2.0 ** (-5.0 - h))
        decay = jnp.where(causal, jnp.exp(lg * rel), 0.0)
        q_dec = jnp.exp(lg * (idx + 1.0))
        k_dec = jnp.exp(lg * (c - 1.0 - idx))
        g_c = math.exp(lg * c)
        hs = slice(h * HEAD_DIM, (h + 1) * HEAD_DIM)
        s = states[h]
        for c0 in range(0, ct, c):
            sl = slice(c0, c0 + c)
            qh, kh, vh = q[sl, hs], k[sl, hs], v[sl, hs]
            s_in = _dotp(qh, kh, _NT, P_MISC) * decay
            o_heads_all[h].append(_dotp(s_in, vh, _NN, P_MISC) + _dotp(qh, s, _NN, P_MISC) * q_dec)
            s = s * g_c + _dotp(kh * k_dec, vh, _TN, P_MISC)
        states[h] = s
    cols = [oh[0] if len(oh) == 1 else jnp.concatenate(oh, axis=0) for oh in o_heads_all]
    o = jnp.concatenate(cols, axis=1)
    for h in range(N_HEADS):
        s_scr[h] = states[h]
    o_ref[...] = _head_rms_finish(o, z, _head_ones())

    @pl.when(j == last)
    def _():
        for h in range(N_HEADS):
            s_out_ref[h] = states[h]


def _rope_tables(pos):
    half = HEAD_DIM // 2
    inv = ROPE_BASE ** (-jnp.arange(half, dtype=F32) / half)
    ang = pos.astype(F32)[:, None] * inv[None, :]
    cos, sin = jnp.cos(ang), jnp.sin(ang)
    cos_t = jnp.tile(jnp.concatenate([cos, cos], axis=-1), (1, N_HEADS))
    sin_t = jnp.tile(jnp.concatenate([-sin, sin], axis=-1), (1, N_HEADS))
    return cos_t, sin_t


def _ret_call(p_b, n_seq, seq_len, cos_t, sin_t):
    ct = min(ROW_TILE, seq_len)
    p3 = p_b.reshape(n_seq, seq_len, B_W)
    o, s1 = pl.pallas_call(
        _ret_kernel,
        grid=(n_seq, seq_len // ct),
        in_specs=[
            pl.BlockSpec((None, ct, B_W), lambda b, j: (b, j, 0)),
            pl.BlockSpec((ct, W_MIX), lambda b, j: (j, 0)),
            pl.BlockSpec((ct, W_MIX), lambda b, j: (j, 0)),
        ],
        out_specs=[
            pl.BlockSpec((None, ct, W_MIX), lambda b, j: (b, j, 0)),
            pl.BlockSpec((None, N_HEADS, HEAD_DIM, HEAD_DIM), lambda b, j: (b, 0, 0, 0)),
        ],
        out_shape=[
            jax.ShapeDtypeStruct((n_seq, seq_len, W_MIX), F32),
            jax.ShapeDtypeStruct((n_seq, N_HEADS, HEAD_DIM, HEAD_DIM), F32),
        ],
        scratch_shapes=[pltpu.VMEM((N_HEADS, HEAD_DIM, HEAD_DIM), F32)],
        compiler_params=pltpu.CompilerParams(dimension_semantics=("parallel", "arbitrary"),
                                             vmem_limit_bytes=VMEM_LIMIT),
        name="retention_prompt",
    )(p3, cos_t, sin_t)
    return o.reshape(n_seq * seq_len, W_MIX), s1


def _lru_kernel(p_ref, cw_ref, cb_ref, gw_ref, gb_ref, sp_ref, o_ref, h_out_ref, conv_out_ref, ext_scr, h_scr):
    j = pl.program_id(1)
    last = pl.num_programs(1) - 1
    ct = p_ref.shape[0]

    @pl.when(j == 0)
    def _():
        h_scr[...] = jnp.zeros(h_scr.shape, F32)

    p = p_ref[...]
    xr = p[:, :W_MIX]
    z = p[:, W_MIX:]
    xc = _conv_tile(xr, ext_scr, cw_ref, j == 0) + cb_ref[...]
    a, b = _lru_token_math(xc, gw_ref[...], gb_ref[...], sp_ref[...])
    rows = _iota((ct, 1), 0)
    dist = 1
    while dist < ct:
        keep = rows >= dist
        a_prev = jnp.where(keep, pltpu.roll(a, dist, axis=0), 1.0)
        b_prev = jnp.where(keep, pltpu.roll(b, dist, axis=0), 0.0)
        b = a * b_prev + b
        a = a * a_prev
        dist *= 2
    hcur = a * h_scr[...] + b
    h_scr[...] = hcur[ct - 1:ct, :]
    o_ref[...] = hcur * _silu(z)

    @pl.when(j == last)
    def _():
        h_out_ref[...] = hcur[ct - 1:ct, :]
        conv_out_ref[...] = xr[ct - SUBLANES:ct, :]


def _block_diag_gates(gate_w):
    out = jnp.zeros((W_MIX, 2 * W_MIX), F32)
    for g in range(2):
        for n in range(N_HEADS):
            out = out.at[n * HEAD_DIM:(n + 1) * HEAD_DIM,
                         g * W_MIX + n * HEAD_DIM:g * W_MIX + (n + 1) * HEAD_DIM].set(gate_w[g, n])
    return out


def _lru_params(lp):
    return [lp["lru_conv_w"], _row(lp["lru_conv_b"]), _block_diag_gates(lp["lru_gate_w"]),
            _row(lp["lru_gate_b"]), _row(lp["lru_lambda"])]


def _lru_call(p_c, n_seq, seq_len, lp):
    ct = min(ROW_TILE, seq_len)
    p3 = p_c.reshape(n_seq, seq_len, C_W)
    params = _lru_params(lp)
    o, h1, conv_tail = pl.pallas_call(
        _lru_kernel,
        grid=(n_seq, seq_len // ct),
        in_specs=[pl.BlockSpec((None, ct, C_W), lambda b, j: (b, j, 0))]
        + [pl.BlockSpec(a.shape, lambda b, j: (0, 0)) for a in params],
        out_specs=[
            pl.BlockSpec((None, ct, W_MIX), lambda b, j: (b, j, 0)),
            pl.BlockSpec((None, 1, W_MIX), lambda b, j: (b, 0, 0)),
            pl.BlockSpec((None, SUBLANES, W_MIX), lambda b, j: (b, 0, 0)),
        ],
        out_shape=[
            jax.ShapeDtypeStruct((n_seq, seq_len, W_MIX), F32),
            jax.ShapeDtypeStruct((n_seq, 1, W_MIX), F32),
            jax.ShapeDtypeStruct((n_seq, SUBLANES, W_MIX), F32),
        ],
        scratch_shapes=[pltpu.VMEM((ct + SUBLANES, W_MIX), F32), pltpu.VMEM((1, W_MIX), F32)],
        compiler_params=pltpu.CompilerParams(dimension_semantics=("parallel", "arbitrary"),
                                             vmem_limit_bytes=VMEM_LIMIT),
        name="rglru_prompt",
    )(p3, *params)
    return (o.reshape(n_seq * seq_len, W_MIX), h1.reshape(n_seq, W_MIX),
            conv_tail[:, SUBLANES - (CONV_W - 1):, :])


def _gdn_kernel(p_ref, cw_ref, nal_ref, dtb_ref, ng_ref, o_ref, s_out_ref, conv_out_ref, ext_scr, s_scr):
    j = pl.program_id(1)
    last = pl.num_programs(1) - 1
    ct = p_ref.shape[0]

    @pl.when(j == 0)
    def _():
        s_scr[...] = jnp.zeros(s_scr.shape, F32)

    p = p_ref[...]
    raw = p[:, :D_QKV_W]
    z = p[:, D_QKV_W:D_QKV_W + W_MIX]
    b_raw = p[:, D_QKV_W + W_MIX:D_QKV_W + 2 * W_MIX]
    a_raw = p[:, D_QKV_W + 2 * W_MIX:]
    ones = _head_ones()
    qkv = _conv_tile(raw, ext_scr, cw_ref, j == 0)
    q, k, v, beta, g = _gdn_token_math(qkv, b_raw, a_raw, nal_ref[...], dtb_ref[...], ones)

    c = min(CHUNK, ct)
    ri, ci = _iota((c, c), 0), _iota((c, c), 1)
    lower = ri >= ci
    strict = ri > ci
    eye = (ri == ci).astype(F32)
    lt = lower.astype(BF16)
    head_mean = (ones.astype(F32) * (1.0 / HEAD_DIM)).astype(BF16)
    eye_h = (_iota((HEAD_DIM, HEAD_DIM), 0) == _iota((HEAD_DIM, HEAD_DIM), 1)).astype(F32)
    units = []
    for c0 in range(0, ct, c):
        sl = slice(c0, c0 + c)
        gc = _dot_const(g[sl], lt, _NN, 3, const_left=True)
        gj_all = _dot_const(gc, head_mean, _NT, 3, const_left=True)
        kb = k[sl] * beta[sl]
        vb = v[sl] * beta[sl]
        e_gc = jnp.exp(gc)
        g_last = gc[c - 1:c, :]
        k_out = k[sl] * jnp.exp(g_last - gc)
        q_in = q[sl] * e_gc
        kbe = kb * e_gc
        e_last = jnp.exp(g_last)
        for h in range(N_HEADS):
            hs = slice(h * HEAD_DIM, (h + 1) * HEAD_DIM)
            diff = gc[:, hs][:, :c] - gj_all[h * HEAD_DIM:h * HEAD_DIM + c, :]
            decay = jnp.where(lower, jnp.exp(jnp.where(lower, diff, 0.0)), 0.0)
            units.append(dict(decay=decay, kb=kb[:, hs], q=q[sl, hs], k=k[sl, hs], vb=vb[:, hs], kbe=kbe[:, hs],
                              k_out=k_out[:, hs], q_in=q_in[:, hs], e_last=e_last[:, hs]))
    misc = lambda x, y, dims=_NN: _dotp(x, y, dims, P_MISC)
    inv = lambda x, y, dims=_NN: _dotp(x, y, dims, P_INV)
    for u in units:
        kq = misc(jnp.concatenate([u["kb"], u["q"]], axis=0), u["k"], _NT)
        u["a_mat"] = jnp.where(strict, kq[:c] * u["decay"], 0.0)
        u["qk"] = kq[c:] * u["decay"]
    for u, t_inv in zip(units, _inv_unit_lower_many([u["a_mat"] for u in units])):
        u["t_inv"] = _Split(t_inv, P_INV)
    for u in units:
        u["u"] = inv(u["t_inv"], u["vb"])
        u["w"] = inv(u["t_inv"], u["kbe"])
    for u in units:
        u["g_mat"] = eye_h * u["e_last"] - misc(u["k_out"], u["w"], _TN)
        u["h_mat"] = misc(u["k_out"], u["u"], _TN)
        u["q_hat"] = u["q_in"] - misc(u["qk"], u["w"])
        u["o1"] = misc(u["qk"], u["u"])
    states = [s_scr[h] for h in range(N_HEADS)]
    o_rows = []
    for ci_ in range(ct // c):
        o_heads = []
        for h in range(N_HEADS):
            u = units[ci_ * N_HEADS + h]
            zz = _dotp(jnp.concatenate([u["q_hat"], u["g_mat"]], axis=0), states[h], _NN, P_STATE)
            o_heads.append(zz[:c] + u["o1"])
            states[h] = zz[c:] + u["h_mat"]
        o_rows.append(jnp.concatenate(o_heads, axis=1))
    o = o_rows[0] if len(o_rows) == 1 else jnp.concatenate(o_rows, axis=0)
    for h in range(N_HEADS):
        s_scr[h] = states[h]
    o_ref[...] = _head_rms_finish(o, z, ones, ng_ref[...])

    @pl.when(j == last)
    def _():
        for h in range(N_HEADS):
            s_out_ref[h] = states[h]
        conv_out_ref[...] = raw[ct - SUBLANES:ct, :]


def _gdn_params(lp):
    return [lp["gdn_conv_w"], _row(jnp.repeat(lp["gdn_A_log"], HEAD_DIM)),
            _row(jnp.repeat(lp["gdn_dt_bias"], HEAD_DIM)), _row(jnp.tile(lp["gdn_norm_g"], N_HEADS))]


def _gdn_call(p_d, n_seq, seq_len, lp):
    ct = min(ROW_TILE, seq_len)
    p3 = p_d.reshape(n_seq, seq_len, D_PACK_W)
    params = _gdn_params(lp)
    o, s1, conv_tail = pl.pallas_call(
        _gdn_kernel,
        grid=(n_seq, seq_len // ct),
        in_specs=[pl.BlockSpec((None, ct, D_PACK_W), lambda b, j: (b, j, 0))]
        + [pl.BlockSpec(a.shape, lambda b, j: (0, 0)) for a in params],
        out_specs=[
            pl.BlockSpec((None, ct, W_MIX), lambda b, j: (b, j, 0)),
            pl.BlockSpec((None, N_HEADS, HEAD_DIM, HEAD_DIM), lambda b, j: (b, 0, 0, 0)),
            pl.BlockSpec((None, SUBLANES, D_QKV_W), lambda b, j: (b, 0, 0)),
        ],
        out_shape=[
            jax.ShapeDtypeStruct((n_seq, seq_len, W_MIX), F32),
            jax.ShapeDtypeStruct((n_seq, N_HEADS, HEAD_DIM, HEAD_DIM), F32),
            jax.ShapeDtypeStruct((n_seq, SUBLANES, D_QKV_W), F32),
        ],
        scratch_shapes=[pltpu.VMEM((ct + SUBLANES, D_QKV_W), F32),
                        pltpu.VMEM((N_HEADS, HEAD_DIM, HEAD_DIM), F32)],
        compiler_params=pltpu.CompilerParams(dimension_semantics=("parallel", "arbitrary"),
                                             vmem_limit_bytes=VMEM_LIMIT),
        name="gdn_prompt",
    )(p3, *params)
    return o.reshape(n_seq * seq_len, W_MIX), s1, conv_tail[:, SUBLANES - (CONV_W - 1):, :]


def _decode_pre_kernel(pa_ref, pb_ref, pc_ref, pd_ref, shift_ref, h0_ref, lconv_ref, gconv_ref, cos_ref, sin_ref,
                       mu_ref, w0_ref, w2_ref, a0_ref, a2_ref, kk_ref, ka_ref,
                       lcw_ref, lcb_ref, lgw_ref, lgb_ref, lsp_ref, gcw_ref, nal_ref, dtb_ref,
                       vt_ref, vn_ref, oc_ref, h1_ref, lconv1_ref, gconv1_ref):
    ones = _head_ones()
    pa_full = pa_ref[...]
    pa = pa_full[:, :A_SHIFT_W]
    pm = pa + (shift_ref[...] - pa) * mu_ref[...]
    r, k, v, ld, av, bv = _rwkv_token_math(pm, w0_ref[...], w2_ref[...], a0_ref[...], a2_ref[...],
                                           kk_ref[...], ka_ref[...], ones)
    vecs = [r, jnp.exp(ld), k, v, av, bv]
    plain = [r, k, v, pa_full[:, A_SHIFT_W:]]
    pb = pb_ref[...]
    cos, sin = cos_ref[...], sin_ref[...]
    vecs += [_rotary(pb[:, 0:W_MIX], cos, sin), _rotary(pb[:, W_MIX:2 * W_MIX], cos, sin) * (HEAD_DIM ** -0.5),
             pb[:, 2 * W_MIX:3 * W_MIX]]
    plain.append(pb[:, 3 * W_MIX:])
    pc = pc_ref[...]
    xr = pc[:, :W_MIX]
    taps = [lconv_ref[i] for i in range(CONV_W - 1)] + [xr]
    xc = taps[0] * lcw_ref[0:1, :]
    for i in range(1, CONV_W):
        xc = xc + taps[i] * lcw_ref[i:i + 1, :]
    xc = xc + lcb_ref[...]
    a, b = _lru_token_math(xc, lgw_ref[...], lgb_ref[...], lsp_ref[...])
    hcur = a * h0_ref[...] + b
    oc_ref[...] = hcur * _silu(pc[:, W_MIX:])
    h1_ref[...] = hcur
    for i in range(CONV_W - 1):
        lconv1_ref[i] = taps[i + 1]
    pd = pd_ref[...]
    raw = pd[:, :D_QKV_W]
    gtaps = [gconv_ref[i] for i in range(CONV_W - 1)] + [raw]
    qkv = gtaps[0] * gcw_ref[0:1, :]
    for i in range(1, CONV_W):
        qkv = qkv + gtaps[i] * gcw_ref[i:i + 1, :]
    q, kg, vg, beta, g = _gdn_token_math(qkv, pd[:, D_QKV_W + W_MIX:D_QKV_W + 2 * W_MIX],
                                         pd[:, D_QKV_W + 2 * W_MIX:], nal_ref[...], dtb_ref[...], ones)
    for i in range(CONV_W - 1):
        gconv1_ref[i] = gtaps[i + 1]
    vecs += [q, kg, vg, beta, g]
    plain.append(pd[:, D_QKV_W:D_QKV_W + W_MIX])
    assert len(vecs) == N_VEC_T and len(plain) == N_VEC_PLAIN
    for i, vec in enumerate(vecs):
        vt_ref[i] = vec.T
    for i, vec in enumerate(plain):
        vn_ref[i] = vec


def _decode_state_kernel(vt_ref, wkv_ref, ret_ref, gdn_ref, gam_ref, wkv1_ref, ret1_ref, gdn1_ref, o_ref):
    v_r, v_w, v_k, v_v, v_a, v_b, r_q, r_k, r_v, g_q, g_k, g_v, g_beta, g_g = range(N_VEC_T)
    hd = HEAD_DIM
    n = vt_ref.shape[-1]
    row = lambda idx, i: vt_ref[idx, pl.ds(i, 1), :]
    rows_of = lambda i: pl.ds(pl.multiple_of(i * hd, hd), hd)
    gamma = gam_ref[...]
    beta = vt_ref[g_beta, 0:1, :]
    eg = jnp.exp(vt_ref[g_g, 0:1, :])

    def first_pass(i, carry):
        acc_ret, acc_w, acc_q = carry
        rows = rows_of(i)
        s = wkv_ref[rows, :]
        sa = jnp.sum(s * vt_ref[v_a], axis=0, keepdims=True)
        s = s * vt_ref[v_w] + sa * vt_ref[v_b] + row(v_v, i) * vt_ref[v_k]
        wkv1_ref[rows, :] = s
        o_ref[0, pl.ds(i, 1), :] = jnp.sum(s * vt_ref[v_r], axis=0, keepdims=True)
        s = ret_ref[rows, :] * gamma + row(r_k, i) * vt_ref[r_v]
        ret1_ref[rows, :] = s
        acc_ret = acc_ret + row(r_q, i) * s
        s = gdn_ref[rows, :]
        return acc_ret, acc_w + row(g_k, i) * s, acc_q + row(g_q, i) * s

    zeros = jnp.zeros((hd, n), F32)
    acc_ret, acc_w, acc_q = lax.fori_loop(0, hd, first_pass, (zeros, zeros, zeros))
    o_ref[1] = acc_ret
    v_new = vt_ref[g_v] * beta - acc_w * (beta * eg)
    qk = jnp.sum(vt_ref[g_q] * vt_ref[g_k], axis=0, keepdims=True)
    o_ref[2] = acc_q * eg + qk * v_new

    def second_pass(i, carry):
        rows = rows_of(i)
        gdn1_ref[rows, :] = gdn_ref[rows, :] * eg + row(g_k, i) * v_new
        return carry

    lax.fori_loop(0, hd, second_pass, 0)


def _decode_finish_kernel(ot_ref, vn_ref, rk_ref, lng_ref, lnb_ref, ng_ref, oa_ref, ob_ref, od_ref):
    ones = _head_ones()
    r, k, v, z_a, z_b, z_d = (vn_ref[i] for i in range(N_VEC_PLAIN))
    oa_ref[...] = _rwkv_finish(ot_ref[0].T, r, k, v, z_a, rk_ref[...], lng_ref[...], lnb_ref[...], ones)
    ob_ref[...] = _head_rms_finish(ot_ref[1].T, z_b, ones)
    od_ref[...] = _head_rms_finish(ot_ref[2].T, z_d, ones, ng_ref[...])


def _batch_minor(state):
    n_layers, n = state.shape[:2]
    return jnp.transpose(state, (0, 2, 3, 4, 1)).reshape(n_layers, -1, n)


def _batch_major(flat_state):
    n = flat_state.shape[-1]
    return jnp.transpose(flat_state.reshape(N_HEADS, HEAD_DIM, HEAD_DIM, n), (3, 0, 1, 2))


def _decode_layer(l, p_a, p_b, p_c, p_d, carried, cos_t, sin_t, lp):
    n = p_a.shape[0]
    assert n % LANES == 0, "the decode state kernel keeps the batch on lanes"
    flat = HEAD_DIM * HEAD_DIM
    taps = CONV_W - 1
    rwkv_params = [_row(lp["rwkv_mu"]), _row(lp["rwkv_w0"]), lp["rwkv_w2"], _row(lp["rwkv_a0"]), lp["rwkv_a2"],
                   _row(lp["rwkv_k_k"]), _row(lp["rwkv_k_a"])]
    gdn_params = _gdn_params(lp)
    full = lambda a: pl.BlockSpec(a.shape, lambda i: (0,) * a.ndim)
    layer_blk = lambda a: pl.BlockSpec((None,) + a.shape[1:], lambda i, nd=a.ndim: (l,) + (0,) * (nd - 1))
    projs = [p_a, p_b, p_c, p_d]
    layered = [carried["shift"], carried["lru_h"], carried["lru_conv"], carried["gdn_conv"]]
    consts = [cos_t, sin_t, *rwkv_params, *_lru_params(lp), *gdn_params[:3]]
    out_shapes = [
        jax.ShapeDtypeStruct((N_VEC_T, W_MIX, n), F32),
        jax.ShapeDtypeStruct((N_VEC_PLAIN, n, W_MIX), F32),
        jax.ShapeDtypeStruct((n, W_MIX), F32),
        jax.ShapeDtypeStruct((n, W_MIX), F32),
        jax.ShapeDtypeStruct((taps, n, W_MIX), F32),
        jax.ShapeDtypeStruct((taps, n, D_QKV_W), F32),
    ]
    vec_t, vec_n, o_c, lru_h1, lru_conv1, gdn_conv1 = pl.pallas_call(
        _decode_pre_kernel,
        grid=(1,),
        in_specs=[full(a) for a in projs] + [layer_blk(a) for a in layered] + [full(a) for a in consts],
        out_specs=[pl.BlockSpec(s.shape, lambda i, nd=len(s.shape): (0,) * nd) for s in out_shapes],
        out_shape=out_shapes,
        compiler_params=pltpu.CompilerParams(dimension_semantics=("arbitrary",), vmem_limit_bytes=VMEM_LIMIT),
        name="decode_tokens",
    )(*projs, *layered, *consts)

    gam = jnp.broadcast_to((1.0 - 2.0 ** (-5.0 - jnp.arange(N_HEADS, dtype=F32)))[:, None, None], (N_HEADS, 1, n))
    state_in = pl.BlockSpec((None, flat, n), lambda h: (l, h, 0))
    state_out = pl.BlockSpec((flat, n), lambda h: (h, 0))
    wkv1, ret1, gdn1, o_t = pl.pallas_call(
        _decode_state_kernel,
        grid=(N_HEADS,),
        in_specs=[pl.BlockSpec((N_VEC_T, HEAD_DIM, n), lambda h: (0, h, 0)), state_in, state_in, state_in,
                  pl.BlockSpec((None, 1, n), lambda h: (h, 0, 0))],
        out_specs=[state_out, state_out, state_out, pl.BlockSpec((3, HEAD_DIM, n), lambda h: (0, h, 0))],
        out_shape=[jax.ShapeDtypeStruct((N_HEADS * flat, n), F32)] * 3 + [jax.ShapeDtypeStruct((3, W_MIX, n), F32)],
        compiler_params=pltpu.CompilerParams(dimension_semantics=("parallel",), vmem_limit_bytes=VMEM_LIMIT),
        name="decode_states",
    )(vec_t, carried["wkv"], carried["ret"], carried["gdn"], gam)

    finish_ins = [o_t, vec_n, _row(lp["rwkv_r_k"]), _row(lp["rwkv_ln_g"]), _row(lp["rwkv_ln_b"]), gdn_params[3]]
    o_a, o_b, o_d = pl.pallas_call(
        _decode_finish_kernel,
        grid=(1,),
        in_specs=[full(a) for a in finish_ins],
        out_specs=[pl.BlockSpec((n, W_MIX), lambda i: (0, 0))] * 3,
        out_shape=[jax.ShapeDtypeStruct((n, W_MIX), F32)] * 3,
        compiler_params=pltpu.CompilerParams(dimension_semantics=("arbitrary",), vmem_limit_bytes=VMEM_LIMIT),
        name="decode_finish",
    )(*finish_ins)
    new_states = (_batch_major(wkv1), p_a[:, :A_SHIFT_W], _batch_major(ret1), lru_h1,
                  jnp.transpose(lru_conv1, (1, 0, 2)), _batch_major(gdn1), jnp.transpose(gdn_conv1, (1, 0, 2)))
    return (o_a, o_b, o_c, o_d), new_states


def _pack_in_weights(w_in):
    off_d = A_W + B_W + C_W
    off_ba = off_d + D_QKV_W
    off_z = off_ba + 2 * N_HEADS
    off_g = off_d + D_W
    w_pack = jnp.concatenate([
        w_in[:, :off_ba],
        w_in[:, off_z:off_g],
        jnp.repeat(w_in[:, off_ba:off_ba + N_HEADS], HEAD_DIM, axis=1),
        jnp.repeat(w_in[:, off_ba + N_HEADS:off_z], HEAD_DIM, axis=1),
    ], axis=1)
    return w_pack, w_in[:, off_g:]


def _prompt_layer(p_a, p_b, p_c, p_d, n_seq, seq_len, cos_t, sin_t, lp):
    o_a, wkv1, shift1 = _rwkv_call(p_a, n_seq, seq_len, lp)
    o_b, ret1 = _ret_call(p_b, n_seq, seq_len, cos_t, sin_t)
    o_c, lru_h1, lru_conv1 = _lru_call(p_c, n_seq, seq_len, lp)
    o_d, gdn1, gdn_conv1 = _gdn_call(p_d, n_seq, seq_len, lp)
    return (o_a, o_b, o_c, o_d), (wkv1, shift1, ret1, lru_h1, lru_conv1, gdn1, gdn_conv1)


def _run_group(x, mods, pos, carried, layers, final_g):
    n_seq, seq_len, d = x.shape
    x2 = x.reshape(n_seq * seq_len, d)
    cos_t, sin_t = _rope_tables(pos)
    new = []
    n_layers = len(layers)
    for l, lp in enumerate(layers):
        shift, scale, gate = (mods[l][:, i * d:(i + 1) * d] for i in range(3))
        p_a, p_b, p_c, p_d = _inproj_call(x2, scale, shift, lp["norm_g"], lp["w_pack"], seq_len)
        if carried is None:
            branches, st = _prompt_layer(p_a, p_b, p_c, p_d, n_seq, seq_len, cos_t, sin_t, lp)
        else:
            branches, st = _decode_layer(l, p_a, p_b, p_c, p_d, carried, cos_t, sin_t, lp)
        new.append(st)
        x2 = _outproj_call(x2, scale, shift, gate, lp["norm_g"], branches, lp["w_gate"], lp["w_up_bf16"],
                           lp["w_out_bf16"], final_g, seq_len, final=(l == n_layers - 1))
    stacked = tuple(jnp.stack([s[i] for s in new], axis=0) for i in range(7))
    return x2.reshape(n_seq, seq_len, d), stacked


def kernel(x_prompt, x_sample, c_prompt, c_sample, state_rwkv_wkv, state_rwkv_shift, state_ret, state_lru_h, state_lru_conv, state_gdn, state_gdn_conv, ada_w, ada_b, norm_g, w_in, rwkv_mu, rwkv_w0, rwkv_w2, rwkv_a0, rwkv_a2, rwkv_k_k, rwkv_k_a, rwkv_r_k, rwkv_ln_g, rwkv_ln_b, lru_conv_w, lru_conv_b, lru_gate_w, lru_gate_b, lru_lambda, gdn_conv_w, gdn_A_log, gdn_dt_bias, gdn_norm_g, w_up, w_out, final_g):
    n_layers = ada_w.shape[0]
    n_prompt, seq_len, _ = x_prompt.shape
    n_sample, dec_len, _ = x_sample.shape
    assert dec_len == 1, "the decode path handles one token per sequence"
    layers = []
    w_in_bf16 = w_in.astype(BF16)
    for l in range(n_layers):
        w_pack, w_gate = _pack_in_weights(w_in_bf16[l])
        layers.append(dict(
            norm_g=norm_g[l], w_pack=w_pack, w_gate=w_gate, w_up_bf16=w_up[l].astype(BF16),
            w_out_bf16=w_out[l].astype(BF16),
            rwkv_mu=rwkv_mu[l], rwkv_w0=rwkv_w0[l], rwkv_w2=rwkv_w2[l], rwkv_a0=rwkv_a0[l], rwkv_a2=rwkv_a2[l],
            rwkv_k_k=rwkv_k_k[l], rwkv_k_a=rwkv_k_a[l], rwkv_r_k=rwkv_r_k[l], rwkv_ln_g=rwkv_ln_g[l],
            rwkv_ln_b=rwkv_ln_b[l], lru_conv_w=lru_conv_w[l], lru_conv_b=lru_conv_b[l],
            lru_gate_w=lru_gate_w[l], lru_gate_b=lru_gate_b[l], lru_lambda=lru_lambda[l],
            gdn_conv_w=gdn_conv_w[l], gdn_A_log=gdn_A_log[l], gdn_dt_bias=gdn_dt_bias[l],
            gdn_norm_g=gdn_norm_g[l]))
    mods = _ada_call(jnp.concatenate([c_prompt, c_sample], axis=0), ada_w, ada_b)
    mods_p = [mods[l, :n_prompt] for l in range(n_layers)]
    mods_s = [mods[l, n_prompt:] for l in range(n_layers)]

    y_prompt, new_p = _run_group(x_prompt, mods_p, jnp.arange(seq_len, dtype=jnp.int32), None, layers, final_g)
    carried = dict(wkv=_batch_minor(state_rwkv_wkv), ret=_batch_minor(state_ret), gdn=_batch_minor(state_gdn),
                   shift=state_rwkv_shift, lru_h=state_lru_h,
                   lru_conv=jnp.transpose(state_lru_conv, (0, 2, 1, 3)),
                   gdn_conv=jnp.transpose(state_gdn_conv, (0, 2, 1, 3)))
    pos_s = PAST_LEN + jnp.arange(dec_len, dtype=jnp.int32)
    y_sample, new_s = _run_group(x_sample, mods_s, pos_s, carried, layers, final_g)
    return (y_prompt, y_sample) + new_p + new_s
```

```python
import functools
import math

import jax
import jax.numpy as jnp
from jax import lax
from jax.experimental import pallas as pl
from jax.experimental.pallas import tpu as pltpu

F32 = jnp.float32
BF16 = jnp.bfloat16
HI = lax.Precision.HIGHEST

N_HEADS = 4
HEAD_DIM = 64
W_MIX = N_HEADS * HEAD_DIM
LORA = 64
CONV_W = 4
N_BRANCH = 4
LRU_C = 8.0
ROPE_BASE = 10000.0
EPS = 1e-6
RWKV_GN_EPS = 64e-5
PAST_LEN = 16384
A_SHIFT_W = 3 * W_MIX + 2 * LORA
A_W = A_SHIFT_W + W_MIX
B_W = 4 * W_MIX
C_W = 2 * W_MIX
D_QKV_W = 3 * W_MIX
D_W = D_QKV_W + 2 * N_HEADS + W_MIX
D_PACK_W = D_QKV_W + 3 * W_MIX

SUBLANES = 8
LANES = 128
VMEM_LIMIT = 56 * 1024 * 1024

CHUNK = 64
RET_CHUNK = 128
INV_BLOCK = 16
WAVE = 8
ROW_TILE = 512
PROJ_TILE = 512
N_VEC_T = 14
N_VEC_PLAIN = 6


def _mm(a, b, prec=HI):
    return lax.dot_general(a, b, (((1,), (0,)), ((), ())), precision=prec, preferred_element_type=F32)


def _mm_nt(a, b, prec=HI):
    return lax.dot_general(a, b, (((1,), (1,)), ((), ())), precision=prec, preferred_element_type=F32)


def _mm_tn(a, b, prec=HI):
    return lax.dot_general(a, b, (((0,), (0,)), ((), ())), precision=prec, preferred_element_type=F32)


_NN = (((1,), (0,)), ((), ()))
_NT = (((1,), (1,)), ((), ()))
_TN = (((0,), (0,)), ((), ()))

P_INV = 1
P_STATE = 1
P_MISC = 1
HEAD_SUM_PIECES = 1
CUMSUM_PIECES = 2


class _Split:
    def __init__(self, x, passes):
        self.hi = x.astype(BF16)
        self.lo = (x - self.hi.astype(F32)).astype(BF16) if passes > 1 else None


def _dotp(a, b, dims=_NN, passes=1):
    a = a if isinstance(a, _Split) else _Split(a, passes)
    b = b if isinstance(b, _Split) else _Split(b, passes)
    d = lambda x, y: lax.dot_general(x, y, dims, preferred_element_type=F32)
    out = d(a.hi, b.hi)
    if passes > 1:
        out = out + (d(a.hi, b.lo) + d(a.lo, b.hi))
    return out


def _iota(shape, dim):
    return lax.broadcasted_iota(jnp.int32, shape, dim)


def _silu(x):
    return x * jax.nn.sigmoid(x)


def _softplus(x):
    return jnp.maximum(x, 0.0) + jnp.log1p(jnp.exp(-jnp.abs(x)))


def _pieces(x, n):
    out = []
    for i in range(n):
        p = x.astype(BF16)
        out.append(p)
        if i + 1 < n:
            x = x - p.astype(F32)
    return out


def _dot_const(x, const, dims=_NN, n=2, const_left=False):
    out = None
    for p in _pieces(x, n):
        t = lax.dot_general(*((const, p) if const_left else (p, const)), dims, preferred_element_type=F32)
        out = t if out is None else out + t
    return out


def _head_ones():
    return (_iota((W_MIX, W_MIX), 0) // HEAD_DIM == _iota((W_MIX, W_MIX), 1) // HEAD_DIM).astype(BF16)


def _head_sum(x, ones, signed=False):
    return _dot_const(x, ones, n=HEAD_SUM_PIECES + (1 if signed else 0))


def _rms(x):
    return x * lax.rsqrt(jnp.mean(x * x, axis=-1, keepdims=True) + EPS)


def _inv_unit_lower(a):
    return _inv_unit_lower_many([a])[0]


def _inv_unit_lower_many(mats):
    n = mats[0].shape[0]
    ri, ci = _iota((n, n), 0), _iota((n, n), 1)
    eye = (ri == ci).astype(F32)
    diag_blk = (ri // INV_BLOCK) == (ci // INV_BLOCK)
    mm = lambda x, y: _dotp(x, y, _NN, P_INV)
    sp = lambda x: _Split(x, P_INV)
    d = [jnp.where(diag_blk, a, 0.0) for a in mats]
    nb = [a - di for a, di in zip(mats, d)]
    td = [eye - di for di in d]
    p = d
    for _ in range(int(math.log2(INV_BLOCK)) - 1):
        ps = [sp(pi) for pi in p]
        p = [mm(pi, pi) for pi in ps]
        td = [mm(ti, eye + pi) for ti, pi in zip(td, p)]
    tds = [sp(ti) for ti in td]
    x = [mm(ti, ni) for ti, ni in zip(tds, nb)]
    t = [eye - xi for xi in x]
    p = x
    for _ in range(int(math.log2(n // INV_BLOCK)) - 1):
        ps = [sp(pi) for pi in p]
        p = [mm(pi, pi) for pi in ps]
        t = [mm(ti, eye + pi) for ti, pi in zip(t, p)]
    return [mm(ti, tdi) for ti, tdi in zip(t, tds)]


class _HeadAlgebra:
    def __init__(self, c):
        assert c == HEAD_DIM, "side-by-side head products need CHUNK == HEAD_DIM"
        w = W_MIX
        row, lane = _iota((c, w), 0), _iota((c, w), 1)
        col = lane % HEAD_DIM
        self.lane_head = [lane // HEAD_DIM == h for h in range(N_HEADS)]
        self.eye = (row == col).astype(F32)
        self.strict = row > col
        self.incl = row >= col
        self.inv_blk = (row // INV_BLOCK) == (col // INV_BLOCK)
        r2, c2 = _iota((w, w), 0), _iota((w, w), 1)
        self.eye_full = r2 == c2
        self.same_head = (r2 // HEAD_DIM) == (c2 // HEAD_DIM)

    def bd(self, y):
        yb = y.astype(BF16)
        zero = jnp.zeros_like(yb)
        return jnp.concatenate([jnp.where(m, yb, zero) for m in self.lane_head], axis=0)

    def nn(self, x, bd_y, out=F32):
        return lax.dot_general(x.astype(BF16), bd_y, _NN, preferred_element_type=F32).astype(out)

    def nt(self, x, bd_y):
        return lax.dot_general(x.astype(BF16), bd_y, _NT, preferred_element_type=F32)

    def tn_bd(self, x, y):
        full = lax.dot_general(x.astype(BF16), y.astype(BF16), _TN, preferred_element_type=F32)
        return jnp.where(self.same_head, full, 0.0)

    def diag_bd(self, row_vec):
        return jnp.where(self.eye_full, row_vec, 0.0)

    def plus_eye(self, bd_p):
        return jnp.where(self.eye_full, jnp.ones_like(bd_p), bd_p)

    def inv_unit_lower_many(self, mats):
        mats = [a.astype(BF16) for a in mats]
        zero = jnp.zeros_like(mats[0])
        eye = self.eye.astype(BF16)
        d = [jnp.where(self.inv_blk, a, zero) for a in mats]
        nb = [jnp.where(self.inv_blk, zero, a) for a in mats]
        td = [eye - di for di in d]
        p = d
        bdp = [self.bd(pi) for pi in p]
        for _ in range(int(math.log2(INV_BLOCK)) - 1):
            p = [self.nn(pi, bi, BF16) for pi, bi in zip(p, bdp)]
            bdp = [self.bd(pi) for pi in p]
            td = [self.nn(ti, self.plus_eye(bi), BF16) for ti, bi in zip(td, bdp)]
        bd_td = [self.bd(ti) for ti in td]
        x = [self.nn(ti, self.bd(ni), BF16) for ti, ni in zip(td, nb)]
        t = [eye - xi for xi in x]
        p = x
        bdp = [self.bd(pi) for pi in p]
        for _ in range(int(math.log2(HEAD_DIM // INV_BLOCK)) - 1):
            p = [self.nn(pi, bi, BF16) for pi, bi in zip(p, bdp)]
            bdp = [self.bd(pi) for pi in p]
            t = [self.nn(ti, self.plus_eye(bi), BF16) for ti, bi in zip(t, bdp)]
        return [self.nn(ti, bi, BF16) for ti, bi in zip(t, bd_td)]


def _ada_kernel(c_ref, w_ref, b_ref, o_ref):
    o_ref[...] = _mm(_silu(c_ref[...]), w_ref[...]) + b_ref[...]


def _ada_call(c_all, ada_w, ada_b):
    n_layers, d, d3 = ada_w.shape
    rows = c_all.shape[0]
    return pl.pallas_call(
        _ada_kernel,
        grid=(n_layers, d3 // d),
        in_specs=[
            pl.BlockSpec((rows, d), lambda l, j: (0, 0)),
            pl.BlockSpec((None, d, d), lambda l, j: (l, 0, j)),
            pl.BlockSpec((None, 1, d), lambda l, j: (l, 0, j)),
        ],
        out_specs=pl.BlockSpec((None, rows, d), lambda l, j: (l, 0, j)),
        out_shape=jax.ShapeDtypeStruct((n_layers, rows, d3), F32),
        compiler_params=pltpu.CompilerParams(dimension_semantics=("arbitrary", "arbitrary"),
                                             vmem_limit_bytes=VMEM_LIMIT),
        name="ada_mod",
    )(c_all, ada_w, ada_b.reshape(n_layers, 1, d3))


def _modulated_norm(x, g, scale, shift):
    return _rms(x) * g * (1.0 + scale) + shift


def _inproj_kernel(x_ref, sc_ref, sh_ref, g_ref, w_ref, oa_ref, ob_ref, oc_ref, od_ref):
    h = _modulated_norm(x_ref[...], g_ref[...], sc_ref[...], sh_ref[...]).astype(BF16)
    lo = 0
    for o_ref in (oa_ref, ob_ref, oc_ref, od_ref):
        wd = o_ref.shape[-1]
        o_ref[...] = jnp.dot(h, w_ref[:, lo:lo + wd], preferred_element_type=F32)
        lo += wd


def _mod_specs(mods, tm, seq_len):
    d = mods[0].shape[-1]
    if seq_len == 1:
        return [m for m in mods], [pl.BlockSpec((tm, d), lambda i: (i, 0)) for _ in mods]
    per_seq = seq_len // tm
    return ([m.reshape(m.shape[0], 1, d) for m in mods],
            [pl.BlockSpec((None, 1, d), lambda i: (i // per_seq, 0, 0)) for _ in mods])


def _inproj_call(x2, scale, shift, g, w_pack, seq_len):
    m, d = x2.shape
    tm = min(PROJ_TILE, m, seq_len) if seq_len > 1 else m
    widths = (A_W, B_W, C_W, D_PACK_W)
    mods, mod_specs = _mod_specs((scale, shift), tm, seq_len)
    return pl.pallas_call(
        _inproj_kernel,
        grid=(m // tm,),
        in_specs=[pl.BlockSpec((tm, d), lambda i: (i, 0))] + mod_specs + [
            pl.BlockSpec((1, d), lambda i: (0, 0)),
            pl.BlockSpec(w_pack.shape, lambda i: (0, 0)),
        ],
        out_specs=[pl.BlockSpec((tm, wd), lambda i: (i, 0)) for wd in widths],
        out_shape=[jax.ShapeDtypeStruct((m, wd), F32) for wd in widths],
        compiler_params=pltpu.CompilerParams(dimension_semantics=("parallel",), vmem_limit_bytes=VMEM_LIMIT),
        name="in_proj",
    )(x2, *mods, g.reshape(1, d), w_pack)


def _outproj_kernel(x_ref, sc_ref, sh_ref, gt_ref, g_ref, ba_ref, bb_ref, bc_ref, bd_ref,
                    wg_ref, wup_ref, wout_ref, fg_ref, o_ref, *, final):
    x = x_ref[...]
    d = x.shape[-1]
    h = _modulated_norm(x, g_ref[...], sc_ref[...], sh_ref[...]).astype(BF16)
    merged = jnp.zeros(x.shape, F32)
    for n, br_ref in enumerate((ba_ref, bb_ref, bc_ref, bd_ref)):
        gl = jnp.dot(h, wg_ref[:, n * d:(n + 1) * d], preferred_element_type=F32)
        up = jnp.dot(br_ref[...].astype(BF16), wup_ref[n], preferred_element_type=F32)
        merged = merged + jax.nn.sigmoid(gl) * up
    out = jnp.dot(merged.astype(BF16), wout_ref[...], preferred_element_type=F32)
    xn = x + gt_ref[...] * out
    if final:
        xn = _rms(xn) * fg_ref[...]
    o_ref[...] = xn


def _outproj_call(x2, scale, shift, gate, g, branches, wg, wup, wout, final_g, seq_len, final):
    m, d = x2.shape
    tm = min(PROJ_TILE, m, seq_len) if seq_len > 1 else m
    mods, mod_specs = _mod_specs((scale, shift, gate), tm, seq_len)
    full = lambda a: pl.BlockSpec(a.shape, lambda i: (0,) * a.ndim)
    return pl.pallas_call(
        functools.partial(_outproj_kernel, final=final),
        grid=(m // tm,),
        in_specs=[pl.BlockSpec((tm, d), lambda i: (i, 0))] + mod_specs + [pl.BlockSpec((1, d), lambda i: (0, 0))]
        + [pl.BlockSpec((tm, W_MIX), lambda i: (i, 0)) for _ in branches]
        + [full(wg), full(wup), full(wout), pl.BlockSpec((1, d), lambda i: (0, 0))],
        out_specs=pl.BlockSpec((tm, d), lambda i: (i, 0)),
        out_shape=jax.ShapeDtypeStruct((m, d), F32),
        compiler_params=pltpu.CompilerParams(dimension_semantics=("parallel",), vmem_limit_bytes=VMEM_LIMIT),
        name="out_proj",
    )(x2, *mods, g.reshape(1, d), *branches, wg, wup, wout, final_g.reshape(1, d))


def _rwkv_token_math(pm, w0, w2, a0, a2, k_k, k_a, ones):
    r = pm[:, 0:W_MIX]
    k = pm[:, W_MIX:2 * W_MIX]
    v = pm[:, 2 * W_MIX:3 * W_MIX]
    wd = pm[:, 3 * W_MIX:3 * W_MIX + LORA]
    ad = pm[:, 3 * W_MIX + LORA:]
    w_log = -_softplus(-(w0 + _dotp(jnp.tanh(wd), w2, _NN, P_MISC))) - 0.5
    log_decay = -jnp.exp(w_log)
    a = jax.nn.sigmoid(a0 + _dotp(ad, a2, _NN, P_MISC))
    kx = k * k_k
    kk = kx * lax.rsqrt(_head_sum(kx * kx, ones) + EPS)
    k = k * (1.0 + (a - 1.0) * k_a)
    return r, k, v, log_decay, -kk, kk * a


def _rwkv_finish(o, r, k, v, z, r_k, ln_g, ln_b, ones):
    mean = _head_sum(o, ones, signed=True) * (1.0 / HEAD_DIM)
    dlt = o - mean
    var = _head_sum(dlt * dlt, ones) * (1.0 / HEAD_DIM)
    on = dlt * lax.rsqrt(var + RWKV_GN_EPS) * ln_g + ln_b
    bonus = _head_sum(r * k * r_k, ones, signed=True) * v
    return (on + bonus) * _silu(z)


def _swap_halves(x):
    half = HEAD_DIM // 2
    n = x.shape[-1]
    first = (_iota(x.shape, 1) & half) == 0
    return jnp.where(first, pltpu.roll(x, n - half, axis=1), pltpu.roll(x, half, axis=1))


def _rotary(x, cos, sin):
    return x * cos + _swap_halves(x) * sin


def _lru_token_math(xc, gate_w, gate_b, lam):
    gates = _dotp(xc, gate_w, _NN, P_MISC) + gate_b
    r_gate = jax.nn.sigmoid(gates[:, :W_MIX])
    i_gate = jax.nn.sigmoid(gates[:, W_MIX:])
    log_a = -LRU_C * r_gate * _softplus(-lam)
    a = jnp.exp(log_a)
    b = jnp.sqrt(1.0 - jnp.exp(2.0 * log_a)) * (i_gate * xc)
    return a, b


def _gdn_token_math(qkv, b_raw, a_raw, a_log, dt_bias, ones):
    qkv = _silu(qkv)
    q = qkv[:, 0:W_MIX]
    k = qkv[:, W_MIX:2 * W_MIX]
    v = qkv[:, 2 * W_MIX:]
    q = q * lax.rsqrt(_head_sum(q * q, ones) + EPS) * (HEAD_DIM ** -0.5)
    k = k * lax.rsqrt(_head_sum(k * k, ones) + EPS)
    beta = jax.nn.sigmoid(b_raw)
    g = -jnp.exp(a_log) * _softplus(a_raw + dt_bias)
    return q, k, v, beta, g


def _head_rms_finish(o, z, ones, gain=None):
    y = o * lax.rsqrt(_head_sum(o * o, ones) * (1.0 / HEAD_DIM) + EPS)
    if gain is not None:
        y = y * gain
    return y * _silu(z)


def _conv_tile(u, ext_ref, w_ref, first):
    n = u.shape[0]

    @pl.when(first)
    def _():
        ext_ref[0:SUBLANES, :] = jnp.zeros((SUBLANES, u.shape[1]), F32)

    ext_ref[SUBLANES:SUBLANES + n, :] = u
    out = None
    for j in range(CONV_W):
        back = CONV_W - 1 - j
        term = ext_ref[SUBLANES - back:SUBLANES - back + n, :] * w_ref[j:j + 1, :]
        out = term if out is None else out + term
    ext_ref[0:SUBLANES, :] = u[n - SUBLANES:n, :]
    return out


def _rwkv_kernel(p_ref, mu_ref, w0_ref, w2_ref, a0_ref, a2_ref, kk_ref, ka_ref, rk_ref, lng_ref, lnb_ref,
                 o_ref, s_out_ref, shift_out_ref, s_scr, prev_scr):
    j = pl.program_id(1)
    last = pl.num_programs(1) - 1
    ct = p_ref.shape[0]

    @pl.when(j == 0)
    def _():
        s_scr[...] = jnp.zeros(s_scr.shape, F32)
        prev_scr[...] = jnp.zeros(prev_scr.shape, F32)

    p = p_ref[...]
    pa = p[:, :A_SHIFT_W]
    z = p[:, A_SHIFT_W:]
    rows = _iota((ct, 1), 0)
    prev = jnp.where(rows == 0, prev_scr[...], pltpu.roll(pa, 1, axis=0))
    prev_scr[...] = pa[ct - 1:ct, :]
    pm = pa + (prev - pa) * mu_ref[...]
    ones = _head_ones()
    r, k, v, ld, av, bv = _rwkv_token_math(pm, w0_ref[...], w2_ref[...], a0_ref[...], a2_ref[...],
                                           kk_ref[...], ka_ref[...], ones)

    c = min(CHUNK, ct)
    ha = _HeadAlgebra(c)
    lt = (_iota((c, c), 0) >= _iota((c, c), 1)).astype(BF16)
    units = []
    for c0 in range(0, ct, c):
        sl = slice(c0, c0 + c)
        ldc = ld[sl]
        cum = _dot_const(ldc, lt, _NN, CUMSUM_PIECES, const_left=True)
        e_neg = jnp.exp(-cum)
        e_out = jnp.exp(cum[c - 1:c, :] - cum)
        units.append(dict(a=av[sl] * jnp.exp(cum - ldc), r=r[sl] * jnp.exp(cum), b=bv[sl] * e_neg, k=k[sl] * e_neg,
                          bo=bv[sl] * e_out, ko=k[sl] * e_out, v=v[sl], g=jnp.exp(cum[c - 1:c, :])))
    all_units = units
    state = s_scr[...]
    o_rows = []
    for w0 in range(0, len(all_units), WAVE):
        units = all_units[w0:w0 + WAVE]
        for u in units:
            lhs = jnp.concatenate([u["a"], u["r"]], axis=0)
            u["mb"] = ha.nt(lhs, ha.bd(u["b"]))
            u["mk"] = ha.nt(lhs, ha.bd(u["k"]))
            u["bd_v"] = ha.bd(u["v"])
        for u in units:
            u["m_ab"] = jnp.where(ha.strict, u["mb"][:c], 0.0)
            u["m_ak"] = jnp.where(ha.strict, u["mk"][:c], 0.0)
            u["m_rb"] = jnp.where(ha.incl, u["mb"][c:], 0.0)
            u["m_rk"] = jnp.where(ha.incl, u["mk"][c:], 0.0)
        for u, t_inv in zip(units, ha.inv_unit_lower_many([-u["m_ab"] for u in units])):
            u["t_inv"] = t_inv
        for u in units:
            u["makv"] = ha.nn(u["m_ak"], u["bd_v"], BF16)
        for u in units:
            u["a_hat"] = ha.nn(u["t_inv"], ha.bd(u["a"]), BF16)
            u["u1"] = ha.nn(u["t_inv"], ha.bd(u["makv"]), BF16)
        for u in units:
            u["r_hat"] = u["r"] + ha.nn(u["m_rb"], ha.bd(u["a_hat"]))
            u["o1"] = ha.nn(u["m_rb"], ha.bd(u["u1"])) + ha.nn(u["m_rk"], u["bd_v"])
            u["g_t"] = ha.diag_bd(u["g"]) + ha.tn_bd(u["bo"], u["a_hat"])
            u["h_t"] = ha.tn_bd(jnp.concatenate([u["bo"], u["ko"]], axis=0),
                                jnp.concatenate([u["u1"], u["v"].astype(BF16)], axis=0))
            zz = lax.dot_general(jnp.concatenate([u["r_hat"], u["g_t"]], axis=0).astype(BF16), state.astype(BF16),
                                 _NN, preferred_element_type=F32)
            o_rows.append(zz[:c] + u["o1"])
            state = zz[c:] + u["h_t"]
    o = o_rows[0] if len(o_rows) == 1 else jnp.concatenate(o_rows, axis=0)
    s_scr[...] = state
    o_ref[...] = _rwkv_finish(o, r, k, v, z, rk_ref[...], lng_ref[...], lnb_ref[...], ones)

    @pl.when(j == last)
    def _():
        eye_h = (_iota((HEAD_DIM, HEAD_DIM), 0) == _iota((HEAD_DIM, HEAD_DIM), 1)).astype(F32)
        for h in range(N_HEADS):
            hs = slice(h * HEAD_DIM, (h + 1) * HEAD_DIM)
            s_out_ref[h] = _mm_nt(eye_h, state[hs, hs])
        shift_out_ref[...] = pa[ct - 1:ct, :]


def _row(a):
    return a.reshape(1, -1)


def _rwkv_call(p_a, n_seq, seq_len, lp):
    ct = min(ROW_TILE, seq_len)
    p3 = p_a.reshape(n_seq, seq_len, A_W)
    params = [_row(lp["rwkv_mu"]), _row(lp["rwkv_w0"]), lp["rwkv_w2"], _row(lp["rwkv_a0"]), lp["rwkv_a2"],
              _row(lp["rwkv_k_k"]), _row(lp["rwkv_k_a"]), _row(lp["rwkv_r_k"]), _row(lp["rwkv_ln_g"]),
              _row(lp["rwkv_ln_b"])]
    o, s1, shift1 = pl.pallas_call(
        _rwkv_kernel,
        grid=(n_seq, seq_len // ct),
        in_specs=[pl.BlockSpec((None, ct, A_W), lambda b, j: (b, j, 0))]
        + [pl.BlockSpec(a.shape, lambda b, j: (0, 0)) for a in params],
        out_specs=[
            pl.BlockSpec((None, ct, W_MIX), lambda b, j: (b, j, 0)),
            pl.BlockSpec((None, N_HEADS, HEAD_DIM, HEAD_DIM), lambda b, j: (b, 0, 0, 0)),
            pl.BlockSpec((None, 1, A_SHIFT_W), lambda b, j: (b, 0, 0)),
        ],
        out_shape=[
            jax.ShapeDtypeStruct((n_seq, seq_len, W_MIX), F32),
            jax.ShapeDtypeStruct((n_seq, N_HEADS, HEAD_DIM, HEAD_DIM), F32),
            jax.ShapeDtypeStruct((n_seq, 1, A_SHIFT_W), F32),
        ],
        scratch_shapes=[pltpu.VMEM((W_MIX, W_MIX), F32), pltpu.VMEM((1, A_SHIFT_W), F32)],
        compiler_params=pltpu.CompilerParams(dimension_semantics=("parallel", "arbitrary"),
                                             vmem_limit_bytes=VMEM_LIMIT),
        name="rwkv7_prompt",
    )(p3, *params)
    return o.reshape(n_seq * seq_len, W_MIX), s1, shift1.reshape(n_seq, A_SHIFT_W)


def _ret_kernel(p_ref, cos_ref, sin_ref, o_ref, s_out_ref, s_scr):
    j = pl.program_id(1)
    last = pl.num_programs(1) - 1
    ct = p_ref.shape[0]

    @pl.when(j == 0)
    def _():
        s_scr[...] = jnp.zeros(s_scr.shape, F32)

    p = p_ref[...]
    cos, sin = cos_ref[...], sin_ref[...]
    q = _rotary(p[:, 0:W_MIX], cos, sin)
    k = _rotary(p[:, W_MIX:2 * W_MIX], cos, sin) * (HEAD_DIM ** -0.5)
    v = p[:, 2 * W_MIX:3 * W_MIX]
    z = p[:, 3 * W_MIX:]
    c = min(RET_CHUNK, ct)
    ri, ci = _iota((c, c), 0), _iota((c, c), 1)
    causal = ri >= ci
    rel = jnp.where(causal, ri - ci, 0).astype(F32)
    idx = _iota((c, 1), 0).astype(F32)
    states = [s_scr[h] for h in range(N_HEADS)]
    o_heads_all = [[] for _ in range(N_HEADS)]
    for h in range(N_HEADS):
        lg = math.log(1.0 - 2.0 ** (-5.0 - h))
        decay = jnp.where(causal, jnp.exp(lg * rel), 0.0)
        q_dec = jnp.exp(lg * (idx + 1.0))
        k_dec = jnp.exp(lg * (c - 1.0 - idx))
        g_c = math.exp(lg * c)
        hs = slice(h * HEAD_DIM, (h + 1) * HEAD_DIM)
        s = states[h]
        for c0 in range(0, ct, c):
            sl = slice(c0, c0 + c)
            qh, kh, vh = q[sl, hs], k[sl, hs], v[sl, hs]
            s_in = _dotp(qh, kh, _NT, P_MISC) * decay
            o_heads_all[h].append(_dotp(s_in, vh, _NN, P_MISC) + _dotp(qh, s, _NN, P_MISC) * q_dec)
            s = s * g_c + _dotp(kh * k_dec, vh, _TN, P_MISC)
        states[h] = s
    cols = [oh[0] if len(oh) == 1 else jnp.concatenate(oh, axis=0) for oh in o_heads_all]
    o = jnp.concatenate(cols, axis=1)
    for h in range(N_HEADS):
        s_scr[h] = states[h]
    o_ref[...] = _head_rms_finish(o, z, _head_ones())

    @pl.when(j == last)
    def _():
        for h in range(N_HEADS):
            s_out_ref[h] = states[h]


def _rope_tables(pos):
    half = HEAD_DIM // 2
    inv = ROPE_BASE ** (-jnp.arange(half, dtype=F32) / half)
    ang = pos.astype(F32)[:, None] * inv[None, :]
    cos, sin = jnp.cos(ang), jnp.sin(ang)
    cos_t = jnp.tile(jnp.concatenate([cos, cos], axis=-1), (1, N_HEADS))
    sin_t = jnp.tile(jnp.concatenate([-sin, sin], axis=-1), (1, N_HEADS))
    return cos_t, sin_t


def _ret_call(p_b, n_seq, seq_len, cos_t, sin_t):
    ct = min(ROW_TILE, seq_len)
    p3 = p_b.reshape(n_seq, seq_len, B_W)
    o, s1 = pl.pallas_call(
        _ret_kernel,
        grid=(n_seq, seq_len // ct),
        in_specs=[
            pl.BlockSpec((None, ct, B_W), lambda b, j: (b, j, 0)),
            pl.BlockSpec((ct, W_MIX), lambda b, j: (j, 0)),
            pl.BlockSpec((ct, W_MIX), lambda b, j: (j, 0)),
        ],
        out_specs=[
            pl.BlockSpec((None, ct, W_MIX), lambda b, j: (b, j, 0)),
            pl.BlockSpec((None, N_HEADS, HEAD_DIM, HEAD_DIM), lambda b, j: (b, 0, 0, 0)),
        ],
        out_shape=[
            jax.ShapeDtypeStruct((n_seq, seq_len, W_MIX), F32),
            jax.ShapeDtypeStruct((n_seq, N_HEADS, HEAD_DIM, HEAD_DIM), F32),
        ],
        scratch_shapes=[pltpu.VMEM((N_HEADS, HEAD_DIM, HEAD_DIM), F32)],
        compiler_params=pltpu.CompilerParams(dimension_semantics=("parallel", "arbitrary"),
                                             vmem_limit_bytes=VMEM_LIMIT),
        name="retention_prompt",
    )(p3, cos_t, sin_t)
    return o.reshape(n_seq * seq_len, W_MIX), s1


def _lru_kernel(p_ref, cw_ref, cb_ref, gw_ref, gb_ref, sp_ref, o_ref, h_out_ref, conv_out_ref, ext_scr, h_scr):
    j = pl.program_id(1)
    last = pl.num_programs(1) - 1
    ct = p_ref.shape[0]

    @pl.when(j == 0)
    def _():
        h_scr[...] = jnp.zeros(h_scr.shape, F32)

    p = p_ref[...]
    xr = p[:, :W_MIX]
    z = p[:, W_MIX:]
    xc = _conv_tile(xr, ext_scr, cw_ref, j == 0) + cb_ref[...]
    a, b = _lru_token_math(xc, gw_ref[...], gb_ref[...], sp_ref[...])
    rows = _iota((ct, 1), 0)
    dist = 1
    while dist < ct:
        keep = rows >= dist
        a_prev = jnp.where(keep, pltpu.roll(a, dist, axis=0), 1.0)
        b_prev = jnp.where(keep, pltpu.roll(b, dist, axis=0), 0.0)
        b = a * b_prev + b
        a = a * a_prev
        dist *= 2
    hcur = a * h_scr[...] + b
    h_scr[...] = hcur[ct - 1:ct, :]
    o_ref[...] = hcur * _silu(z)

    @pl.when(j == last)
    def _():
        h_out_ref[...] = hcur[ct - 1:ct, :]
        conv_out_ref[...] = xr[ct - SUBLANES:ct, :]


def _block_diag_gates(gate_w):
    out = jnp.zeros((W_MIX, 2 * W_MIX), F32)
    for g in range(2):
        for n in range(N_HEADS):
            out = out.at[n * HEAD_DIM:(n + 1) * HEAD_DIM,
                         g * W_MIX + n * HEAD_DIM:g * W_MIX + (n + 1) * HEAD_DIM].set(gate_w[g, n])
    return out


def _lru_params(lp):
    return [lp["lru_conv_w"], _row(lp["lru_conv_b"]), _block_diag_gates(lp["lru_gate_w"]),
            _row(lp["lru_gate_b"]), _row(lp["lru_lambda"])]


def _lru_call(p_c, n_seq, seq_len, lp):
    ct = min(ROW_TILE, seq_len)
    p3 = p_c.reshape(n_seq, seq_len, C_W)
    params = _lru_params(lp)
    o, h1, conv_tail = pl.pallas_call(
        _lru_kernel,
        grid=(n_seq, seq_len // ct),
        in_specs=[pl.BlockSpec((None, ct, C_W), lambda b, j: (b, j, 0))]
        + [pl.BlockSpec(a.shape, lambda b, j: (0, 0)) for a in params],
        out_specs=[
            pl.BlockSpec((None, ct, W_MIX), lambda b, j: (b, j, 0)),
            pl.BlockSpec((None, 1, W_MIX), lambda b, j: (b, 0, 0)),
            pl.BlockSpec((None, SUBLANES, W_MIX), lambda b, j: (b, 0, 0)),
        ],
        out_shape=[
            jax.ShapeDtypeStruct((n_seq, seq_len, W_MIX), F32),
            jax.ShapeDtypeStruct((n_seq, 1, W_MIX), F32),
            jax.ShapeDtypeStruct((n_seq, SUBLANES, W_MIX), F32),
        ],
        scratch_shapes=[pltpu.VMEM((ct + SUBLANES, W_MIX), F32), pltpu.VMEM((1, W_MIX), F32)],
        compiler_params=pltpu.CompilerParams(dimension_semantics=("parallel", "arbitrary"),
                                             vmem_limit_bytes=VMEM_LIMIT),
        name="rglru_prompt",
    )(p3, *params)
    return (o.reshape(n_seq * seq_len, W_MIX), h1.reshape(n_seq, W_MIX),
            conv_tail[:, SUBLANES - (CONV_W - 1):, :])


def _gdn_kernel(p_ref, cw_ref, nal_ref, dtb_ref, ng_ref, o_ref, s_out_ref, conv_out_ref, ext_scr, s_scr):
    j = pl.program_id(1)
    last = pl.num_programs(1) - 1
    ct = p_ref.shape[0]

    @pl.when(j == 0)
    def _():
        s_scr[...] = jnp.zeros(s_scr.shape, F32)

    p = p_ref[...]
    raw = p[:, :D_QKV_W]
    z = p[:, D_QKV_W:D_QKV_W + W_MIX]
    b_raw = p[:, D_QKV_W + W_MIX:D_QKV_W + 2 * W_MIX]
    a_raw = p[:, D_QKV_W + 2 * W_MIX:]
    ones = _head_ones()
    qkv = _conv_tile(raw, ext_scr, cw_ref, j == 0)
    q, k, v, beta, g = _gdn_token_math(qkv, b_raw, a_raw, nal_ref[...], dtb_ref[...], ones)

    c = min(CHUNK, ct)
    ha = _HeadAlgebra(c)
    lt = (_iota((c, c), 0) >= _iota((c, c), 1)).astype(BF16)
    units = []
    for c0 in range(0, ct, c):
        sl = slice(c0, c0 + c)
        gc = _dot_const(g[sl], lt, _NN, CUMSUM_PIECES, const_left=True)
        gc_cols = jnp.sum(gc * ha.eye, axis=0, keepdims=True)
        diff = gc - gc_cols
        decay = jnp.where(ha.incl, jnp.exp(jnp.where(ha.incl, diff, 0.0)), 0.0)
        kb = k[sl] * beta[sl]
        e_gc = jnp.exp(gc)
        g_last = gc[c - 1:c, :]
        units.append(dict(decay=decay, kb=kb, q=q[sl], k=k[sl], vb=v[sl] * beta[sl], kbe=kb * e_gc,
                          k_out=k[sl] * jnp.exp(g_last - gc), q_in=q[sl] * e_gc, e_last=jnp.exp(g_last)))
    all_units = units
    state = s_scr[...]
    o_rows = []
    for w0 in range(0, len(all_units), WAVE):
        units = all_units[w0:w0 + WAVE]
        for u in units:
            kq = ha.nt(jnp.concatenate([u["kb"], u["q"]], axis=0), ha.bd(u["k"]))
            u["a_mat"] = jnp.where(ha.strict, kq[:c] * u["decay"], 0.0)
            u["qk"] = kq[c:] * u["decay"]
        for u, t_inv in zip(units, ha.inv_unit_lower_many([u["a_mat"] for u in units])):
            u["t_inv"] = t_inv
        for u in units:
            u["u"] = ha.nn(u["t_inv"], ha.bd(u["vb"]), BF16)
            u["w"] = ha.nn(u["t_inv"], ha.bd(u["kbe"]), BF16)
        for u in units:
            u["g_mat"] = ha.diag_bd(u["e_last"]) - ha.tn_bd(u["k_out"], u["w"])
            u["h_mat"] = ha.tn_bd(u["k_out"], u["u"])
            u["q_hat"] = u["q_in"] - ha.nn(u["qk"], ha.bd(u["w"]))
            u["o1"] = ha.nn(u["qk"], ha.bd(u["u"]))
            zz = lax.dot_general(jnp.concatenate([u["q_hat"], u["g_mat"]], axis=0).astype(BF16),
                                 state.astype(BF16), _NN, preferred_element_type=F32)
            o_rows.append(zz[:c] + u["o1"])
            state = zz[c:] + u["h_mat"]
    o = o_rows[0] if len(o_rows) == 1 else jnp.concatenate(o_rows, axis=0)
    s_scr[...] = state
    o_ref[...] = _head_rms_finish(o, z, ones, ng_ref[...])

    @pl.when(j == last)
    def _():
        for h in range(N_HEADS):
            hs = slice(h * HEAD_DIM, (h + 1) * HEAD_DIM)
            s_out_ref[h] = state[hs, hs]
        conv_out_ref[...] = raw[ct - SUBLANES:ct, :]


def _gdn_params(lp):
    return [lp["gdn_conv_w"], _row(jnp.repeat(lp["gdn_A_log"], HEAD_DIM)),
            _row(jnp.repeat(lp["gdn_dt_bias"], HEAD_DIM)), _row(jnp.tile(lp["gdn_norm_g"], N_HEADS))]


def _gdn_call(p_d, n_seq, seq_len, lp):
    ct = min(ROW_TILE, seq_len)
    p3 = p_d.reshape(n_seq, seq_len, D_PACK_W)
    params = _gdn_params(lp)
    o, s1, conv_tail = pl.pallas_call(
        _gdn_kernel,
        grid=(n_seq, seq_len // ct),
        in_specs=[pl.BlockSpec((None, ct, D_PACK_W), lambda b, j: (b, j, 0))]
        + [pl.BlockSpec(a.shape, lambda b, j: (0, 0)) for a in params],
        out_specs=[
            pl.BlockSpec((None, ct, W_MIX), lambda b, j: (b, j, 0)),
            pl.BlockSpec((None, N_HEADS, HEAD_DIM, HEAD_DIM), lambda b, j: (b, 0, 0, 0)),
            pl.BlockSpec((None, SUBLANES, D_QKV_W), lambda b, j: (b, 0, 0)),
        ],
        out_shape=[
            jax.ShapeDtypeStruct((n_seq, seq_len, W_MIX), F32),
            jax.ShapeDtypeStruct((n_seq, N_HEADS, HEAD_DIM, HEAD_DIM), F32),
            jax.ShapeDtypeStruct((n_seq, SUBLANES, D_QKV_W), F32),
        ],
        scratch_shapes=[pltpu.VMEM((ct + SUBLANES, D_QKV_W), F32), pltpu.VMEM((W_MIX, W_MIX), F32)],
        compiler_params=pltpu.CompilerParams(dimension_semantics=("parallel", "arbitrary"),
                                             vmem_limit_bytes=VMEM_LIMIT),
        name="gdn_prompt",
    )(p3, *params)
    return o.reshape(n_seq * seq_len, W_MIX), s1, conv_tail[:, SUBLANES - (CONV_W - 1):, :]


def _decode_pre_kernel(pa_ref, pb_ref, pc_ref, pd_ref, shift_ref, h0_ref, lconv_ref, gconv_ref, cos_ref, sin_ref,
                       mu_ref, w0_ref, w2_ref, a0_ref, a2_ref, kk_ref, ka_ref,
                       lcw_ref, lcb_ref, lgw_ref, lgb_ref, lsp_ref, gcw_ref, nal_ref, dtb_ref,
                       vt_ref, vn_ref, oc_ref, h1_ref, lconv1_ref, gconv1_ref):
    ones = _head_ones()
    pa_full = pa_ref[...]
    pa = pa_full[:, :A_SHIFT_W]
    pm = pa + (shift_ref[...] - pa) * mu_ref[...]
    r, k, v, ld, av, bv = _rwkv_token_math(pm, w0_ref[...], w2_ref[...], a0_ref[...], a2_ref[...],
                                           kk_ref[...], ka_ref[...], ones)
    vecs = [r, jnp.exp(ld), k, v, av, bv]
    plain = [r, k, v, pa_full[:, A_SHIFT_W:]]
    pb = pb_ref[...]
    cos, sin = cos_ref[...], sin_ref[...]
    vecs += [_rotary(pb[:, 0:W_MIX], cos, sin), _rotary(pb[:, W_MIX:2 * W_MIX], cos, sin) * (HEAD_DIM ** -0.5),
             pb[:, 2 * W_MIX:3 * W_MIX]]
    plain.append(pb[:, 3 * W_MIX:])
    pc = pc_ref[...]
    xr = pc[:, :W_MIX]
    taps = [lconv_ref[i] for i in range(CONV_W - 1)] + [xr]
    xc = taps[0] * lcw_ref[0:1, :]
    for i in range(1, CONV_W):
        xc = xc + taps[i] * lcw_ref[i:i + 1, :]
    xc = xc + lcb_ref[...]
    a, b = _lru_token_math(xc, lgw_ref[...], lgb_ref[...], lsp_ref[...])
    hcur = a * h0_ref[...] + b
    oc_ref[...] = hcur * _silu(pc[:, W_MIX:])
    h1_ref[...] = hcur
    for i in range(CONV_W - 1):
        lconv1_ref[i] = taps[i + 1]
    pd = pd_ref[...]
    raw = pd[:, :D_QKV_W]
    gtaps = [gconv_ref[i] for i in range(CONV_W - 1)] + [raw]
    qkv = gtaps[0] * gcw_ref[0:1, :]
    for i in range(1, CONV_W):
        qkv = qkv + gtaps[i] * gcw_ref[i:i + 1, :]
    q, kg, vg, beta, g = _gdn_token_math(qkv, pd[:, D_QKV_W + W_MIX:D_QKV_W + 2 * W_MIX],
                                         pd[:, D_QKV_W + 2 * W_MIX:], nal_ref[...], dtb_ref[...], ones)
    for i in range(CONV_W - 1):
        gconv1_ref[i] = gtaps[i + 1]
    vecs += [q, kg, vg, beta, g]
    plain.append(pd[:, D_QKV_W:D_QKV_W + W_MIX])
    assert len(vecs) == N_VEC_T and len(plain) == N_VEC_PLAIN
    for i, vec in enumerate(vecs):
        vt_ref[i] = vec.T
    for i, vec in enumerate(plain):
        vn_ref[i] = vec


def _decode_state_kernel(vt_ref, wkv_ref, ret_ref, gdn_ref, gam_ref, wkv1_ref, ret1_ref, gdn1_ref, o_ref):
    v_r, v_w, v_k, v_v, v_a, v_b, r_q, r_k, r_v, g_q, g_k, g_v, g_beta, g_g = range(N_VEC_T)
    hd = HEAD_DIM
    n = vt_ref.shape[-1]
    row = lambda idx, i: vt_ref[idx, pl.ds(i, 1), :]
    rows_of = lambda i: pl.ds(pl.multiple_of(i * hd, hd), hd)
    gamma = gam_ref[...]
    beta = vt_ref[g_beta, 0:1, :]
    eg = jnp.exp(vt_ref[g_g, 0:1, :])

    def first_pass(i, carry):
        acc_ret, acc_w, acc_q = carry
        rows = rows_of(i)
        s = wkv_ref[rows, :]
        sa = jnp.sum(s * vt_ref[v_a], axis=0, keepdims=True)
        s = s * vt_ref[v_w] + sa * vt_ref[v_b] + row(v_v, i) * vt_ref[v_k]
        wkv1_ref[rows, :] = s
        o_ref[0, pl.ds(i, 1), :] = jnp.sum(s * vt_ref[v_r], axis=0, keepdims=True)
        s = ret_ref[rows, :] * gamma + row(r_k, i) * vt_ref[r_v]
        ret1_ref[rows, :] = s
        acc_ret = acc_ret + row(r_q, i) * s
        s = gdn_ref[rows, :]
        return acc_ret, acc_w + row(g_k, i) * s, acc_q + row(g_q, i) * s

    zeros = jnp.zeros((hd, n), F32)
    acc_ret, acc_w, acc_q = lax.fori_loop(0, hd, first_pass, (zeros, zeros, zeros))
    o_ref[1] = acc_ret
    v_new = vt_ref[g_v] * beta - acc_w * (beta * eg)
    qk = jnp.sum(vt_ref[g_q] * vt_ref[g_k], axis=0, keepdims=True)
    o_ref[2] = acc_q * eg + qk * v_new

    def second_pass(i, carry):
        rows = rows_of(i)
        gdn1_ref[rows, :] = gdn_ref[rows, :] * eg + row(g_k, i) * v_new
        return carry

    lax.fori_loop(0, hd, second_pass, 0)


def _decode_finish_kernel(ot_ref, vn_ref, rk_ref, lng_ref, lnb_ref, ng_ref, oa_ref, ob_ref, od_ref):
    ones = _head_ones()
    r, k, v, z_a, z_b, z_d = (vn_ref[i] for i in range(N_VEC_PLAIN))
    oa_ref[...] = _rwkv_finish(ot_ref[0].T, r, k, v, z_a, rk_ref[...], lng_ref[...], lnb_ref[...], ones)
    ob_ref[...] = _head_rms_finish(ot_ref[1].T, z_b, ones)
    od_ref[...] = _head_rms_finish(ot_ref[2].T, z_d, ones, ng_ref[...])


def _batch_minor(state):
    n_layers, n = state.shape[:2]
    return jnp.transpose(state, (0, 2, 3, 4, 1)).reshape(n_layers, -1, n)


def _batch_major(flat_state):
    n = flat_state.shape[-1]
    return jnp.transpose(flat_state.reshape(N_HEADS, HEAD_DIM, HEAD_DIM, n), (3, 0, 1, 2))


def _decode_layer(l, p_a, p_b, p_c, p_d, carried, cos_t, sin_t, lp):
    n = p_a.shape[0]
    assert n % LANES == 0, "the decode state kernel keeps the batch on lanes"
    flat = HEAD_DIM * HEAD_DIM
    taps = CONV_W - 1
    rwkv_params = [_row(lp["rwkv_mu"]), _row(lp["rwkv_w0"]), lp["rwkv_w2"], _row(lp["rwkv_a0"]), lp["rwkv_a2"],
                   _row(lp["rwkv_k_k"]), _row(lp["rwkv_k_a"])]
    gdn_params = _gdn_params(lp)
    full = lambda a: pl.BlockSpec(a.shape, lambda i: (0,) * a.ndim)
    layer_blk = lambda a: pl.BlockSpec((None,) + a.shape[1:], lambda i, nd=a.ndim: (l,) + (0,) * (nd - 1))
    projs = [p_a, p_b, p_c, p_d]
    layered = [carried["shift"], carried["lru_h"], carried["lru_conv"], carried["gdn_conv"]]
    consts = [cos_t, sin_t, *rwkv_params, *_lru_params(lp), *gdn_params[:3]]
    out_shapes = [
        jax.ShapeDtypeStruct((N_VEC_T, W_MIX, n), F32),
        jax.ShapeDtypeStruct((N_VEC_PLAIN, n, W_MIX), F32),
        jax.ShapeDtypeStruct((n, W_MIX), F32),
        jax.ShapeDtypeStruct((n, W_MIX), F32),
        jax.ShapeDtypeStruct((taps, n, W_MIX), F32),
        jax.ShapeDtypeStruct((taps, n, D_QKV_W), F32),
    ]
    vec_t, vec_n, o_c, lru_h1, lru_conv1, gdn_conv1 = pl.pallas_call(
        _decode_pre_kernel,
        grid=(1,),
        in_specs=[full(a) for a in projs] + [layer_blk(a) for a in layered] + [full(a) for a in consts],
        out_specs=[pl.BlockSpec(s.shape, lambda i, nd=len(s.shape): (0,) * nd) for s in out_shapes],
        out_shape=out_shapes,
        compiler_params=pltpu.CompilerParams(dimension_semantics=("arbitrary",), vmem_limit_bytes=VMEM_LIMIT),
        name="decode_tokens",
    )(*projs, *layered, *consts)

    gam = jnp.broadcast_to((1.0 - 2.0 ** (-5.0 - jnp.arange(N_HEADS, dtype=F32)))[:, None, None], (N_HEADS, 1, n))
    state_in = pl.BlockSpec((None, flat, n), lambda h: (l, h, 0))
    state_out = pl.BlockSpec((flat, n), lambda h: (h, 0))
    wkv1, ret1, gdn1, o_t = pl.pallas_call(
        _decode_state_kernel,
        grid=(N_HEADS,),
        in_specs=[pl.BlockSpec((N_VEC_T, HEAD_DIM, n), lambda h: (0, h, 0)), state_in, state_in, state_in,
                  pl.BlockSpec((None, 1, n), lambda h: (h, 0, 0))],
        out_specs=[state_out, state_out, state_out, pl.BlockSpec((3, HEAD_DIM, n), lambda h: (0, h, 0))],
        out_shape=[jax.ShapeDtypeStruct((N_HEADS * flat, n), F32)] * 3 + [jax.ShapeDtypeStruct((3, W_MIX, n), F32)],
        compiler_params=pltpu.CompilerParams(dimension_semantics=("parallel",), vmem_limit_bytes=VMEM_LIMIT),
        name="decode_states",
    )(vec_t, carried["wkv"], carried["ret"], carried["gdn"], gam)

    finish_ins = [o_t, vec_n, _row(lp["rwkv_r_k"]), _row(lp["rwkv_ln_g"]), _row(lp["rwkv_ln_b"]), gdn_params[3]]
    o_a, o_b, o_d = pl.pallas_call(
        _decode_finish_kernel,
        grid=(1,),
        in_specs=[full(a) for a in finish_ins],
        out_specs=[pl.BlockSpec((n, W_MIX), lambda i: (0, 0))] * 3,
        out_shape=[jax.ShapeDtypeStruct((n, W_MIX), F32)] * 3,
        compiler_params=pltpu.CompilerParams(dimension_semantics=("arbitrary",), vmem_limit_bytes=VMEM_LIMIT),
        name="decode_finish",
    )(*finish_ins)
    new_states = (_batch_major(wkv1), p_a[:, :A_SHIFT_W], _batch_major(ret1), lru_h1,
                  jnp.transpose(lru_conv1, (1, 0, 2)), _batch_major(gdn1), jnp.transpose(gdn_conv1, (1, 0, 2)))
    return (o_a, o_b, o_c, o_d), new_states


def _pack_in_weights(w_in):
    off_d = A_W + B_W + C_W
    off_ba = off_d + D_QKV_W
    off_z = off_ba + 2 * N_HEADS
    off_g = off_d + D_W
    w_pack = jnp.concatenate([
        w_in[:, :off_ba],
        w_in[:, off_z:off_g],
        jnp.repeat(w_in[:, off_ba:off_ba + N_HEADS], HEAD_DIM, axis=1),
        jnp.repeat(w_in[:, off_ba + N_HEADS:off_z], HEAD_DIM, axis=1),
    ], axis=1)
    return w_pack, w_in[:, off_g:]


def _prompt_layer(p_a, p_b, p_c, p_d, n_seq, seq_len, cos_t, sin_t, lp):
    o_a, wkv1, shift1 = _rwkv_call(p_a, n_seq, seq_len, lp)
    o_b, ret1 = _ret_call(p_b, n_seq, seq_len, cos_t, sin_t)
    o_c, lru_h1, lru_conv1 = _lru_call(p_c, n_seq, seq_len, lp)
    o_d, gdn1, gdn_conv1 = _gdn_call(p_d, n_seq, seq_len, lp)
    return (o_a, o_b, o_c, o_d), (wkv1, shift1, ret1, lru_h1, lru_conv1, gdn1, gdn_conv1)


def _run_group(x, mods, pos, carried, layers, final_g):
    n_seq, seq_len, d = x.shape
    x2 = x.reshape(n_seq * seq_len, d)
    cos_t, sin_t = _rope_tables(pos)
    new = []
    n_layers = len(layers)
    for l, lp in enumerate(layers):
        shift, scale, gate = (mods[l][:, i * d:(i + 1) * d] for i in range(3))
        p_a, p_b, p_c, p_d = _inproj_call(x2, scale, shift, lp["norm_g"], lp["w_pack"], seq_len)
        if carried is None:
            branches, st = _prompt_layer(p_a, p_b, p_c, p_d, n_seq, seq_len, cos_t, sin_t, lp)
        else:
            branches, st = _decode_layer(l, p_a, p_b, p_c, p_d, carried, cos_t, sin_t, lp)
        new.append(st)
        x2 = _outproj_call(x2, scale, shift, gate, lp["norm_g"], branches, lp["w_gate"], lp["w_up_bf16"],
                           lp["w_out_bf16"], final_g, seq_len, final=(l == n_layers - 1))
    stacked = tuple(jnp.stack([s[i] for s in new], axis=0) for i in range(7))
    return x2.reshape(n_seq, seq_len, d), stacked


def kernel(x_prompt, x_sample, c_prompt, c_sample, state_rwkv_wkv, state_rwkv_shift, state_ret, state_lru_h, state_lru_conv, state_gdn, state_gdn_conv, ada_w, ada_b, norm_g, w_in, rwkv_mu, rwkv_w0, rwkv_w2, rwkv_a0, rwkv_a2, rwkv_k_k, rwkv_k_a, rwkv_r_k, rwkv_ln_g, rwkv_ln_b, lru_conv_w, lru_conv_b, lru_gate_w, lru_gate_b, lru_lambda, gdn_conv_w, gdn_A_log, gdn_dt_bias, gdn_norm_g, w_up, w_out, final_g):
    n_layers = ada_w.shape[0]
    n_prompt, seq_len, _ = x_prompt.shape
    n_sample, dec_len, _ = x_sample.shape
    assert dec_len == 1, "the decode path handles one token per sequence"
    layers = []
    w_in_bf16 = w_in.astype(BF16)
    for l in range(n_layers):
        w_pack, w_gate = _pack_in_weights(w_in_bf16[l])
        layers.append(dict(
            norm_g=norm_g[l], w_pack=w_pack, w_gate=w_gate, w_up_bf16=w_up[l].astype(BF16),
            w_out_bf16=w_out[l].astype(BF16),
            rwkv_mu=rwkv_mu[l], rwkv_w0=rwkv_w0[l], rwkv_w2=rwkv_w2[l], rwkv_a0=rwkv_a0[l], rwkv_a2=rwkv_a2[l],
            rwkv_k_k=rwkv_k_k[l], rwkv_k_a=rwkv_k_a[l], rwkv_r_k=rwkv_r_k[l], rwkv_ln_g=rwkv_ln_g[l],
            rwkv_ln_b=rwkv_ln_b[l], lru_conv_w=lru_conv_w[l], lru_conv_b=lru_conv_b[l],
            lru_gate_w=lru_gate_w[l], lru_gate_b=lru_gate_b[l], lru_lambda=lru_lambda[l],
            gdn_conv_w=gdn_conv_w[l], gdn_A_log=gdn_A_log[l], gdn_dt_bias=gdn_dt_bias[l],
            gdn_norm_g=gdn_norm_g[l]))
    mods = _ada_call(jnp.concatenate([c_prompt, c_sample], axis=0), ada_w, ada_b)
    mods_p = [mods[l, :n_prompt] for l in range(n_layers)]
    mods_s = [mods[l, n_prompt:] for l in range(n_layers)]

    y_prompt, new_p = _run_group(x_prompt, mods_p, jnp.arange(seq_len, dtype=jnp.int32), None, layers, final_g)
    carried = dict(wkv=_batch_minor(state_rwkv_wkv), ret=_batch_minor(state_ret), gdn=_batch_minor(state_gdn),
                   shift=state_rwkv_shift, lru_h=state_lru_h,
                   lru_conv=jnp.transpose(state_lru_conv, (0, 2, 1, 3)),
                   gdn_conv=jnp.transpose(state_gdn_conv, (0, 2, 1, 3)))
    pos_s = PAST_LEN + jnp.arange(dec_len, dtype=jnp.int32)
    y_sample, new_s = _run_group(x_sample, mods_s, pos_s, carried, layers, final_g)
    return (y_prompt, y_sample) + new_p + new_s
```

```python
import functools
import math

import jax
import jax.numpy as jnp
from jax import lax
from jax.experimental import pallas as pl
from jax.experimental.pallas import tpu as pltpu

F32 = jnp.float32
BF16 = jnp.bfloat16
HI = lax.Precision.HIGHEST

N_HEADS = 4
HEAD_DIM = 64
W_MIX = N_HEADS * HEAD_DIM
LORA = 64
CONV_W = 4
N_BRANCH = 4
LRU_C = 8.0
ROPE_BASE = 10000.0
EPS = 1e-6
RWKV_GN_EPS = 64e-5
PAST_LEN = 16384
A_SHIFT_W = 3 * W_MIX + 2 * LORA
A_W = A_SHIFT_W + W_MIX
B_W = 4 * W_MIX
C_W = 2 * W_MIX
D_QKV_W = 3 * W_MIX
D_W = D_QKV_W + 2 * N_HEADS + W_MIX
D_PACK_W = D_QKV_W + 3 * W_MIX

SUBLANES = 8
LANES = 128
VMEM_LIMIT = 56 * 1024 * 1024

CHUNK = 64
RET_CHUNK = 128
INV_BLOCK = 16
WAVE = 8
ROW_TILE = 512
PROJ_TILE = 512
OUT_TILE = 1024
N_VEC_T = 14
N_VEC_PLAIN = 6


def _mm(a, b, prec=HI):
    return lax.dot_general(a, b, (((1,), (0,)), ((), ())), precision=prec, preferred_element_type=F32)


def _mm_nt(a, b, prec=HI):
    return lax.dot_general(a, b, (((1,), (1,)), ((), ())), precision=prec, preferred_element_type=F32)


def _mm_tn(a, b, prec=HI):
    return lax.dot_general(a, b, (((0,), (0,)), ((), ())), precision=prec, preferred_element_type=F32)


_NN = (((1,), (0,)), ((), ()))
_NT = (((1,), (1,)), ((), ()))
_TN = (((0,), (0,)), ((), ()))

P_INV = 1
P_STATE = 1
P_MISC = 1
HEAD_SUM_PIECES = 1
CUMSUM_PIECES = 2


class _Split:
    def __init__(self, x, passes):
        self.hi = x.astype(BF16)
        self.lo = (x - self.hi.astype(F32)).astype(BF16) if passes > 1 else None


def _dotp(a, b, dims=_NN, passes=1):
    a = a if isinstance(a, _Split) else _Split(a, passes)
    b = b if isinstance(b, _Split) else _Split(b, passes)
    d = lambda x, y: lax.dot_general(x, y, dims, preferred_element_type=F32)
    out = d(a.hi, b.hi)
    if passes > 1:
        out = out + (d(a.hi, b.lo) + d(a.lo, b.hi))
    return out


def _iota(shape, dim):
    return lax.broadcasted_iota(jnp.int32, shape, dim)


def _silu(x):
    return x * jax.nn.sigmoid(x)


def _softplus(x):
    return jnp.maximum(x, 0.0) + jnp.log1p(jnp.exp(-jnp.abs(x)))


def _pieces(x, n):
    out = []
    for i in range(n):
        p = x.astype(BF16)
        out.append(p)
        if i + 1 < n:
            x = x - p.astype(F32)
    return out


def _dot_const(x, const, dims=_NN, n=2, const_left=False):
    out = None
    for p in _pieces(x, n):
        t = lax.dot_general(*((const, p) if const_left else (p, const)), dims, preferred_element_type=F32)
        out = t if out is None else out + t
    return out


def _head_ones():
    return (_iota((W_MIX, W_MIX), 0) // HEAD_DIM == _iota((W_MIX, W_MIX), 1) // HEAD_DIM).astype(BF16)


def _head_sum(x, ones, signed=False):
    return _dot_const(x, ones, n=HEAD_SUM_PIECES + (1 if signed else 0))


def _rms(x):
    return x * lax.rsqrt(jnp.mean(x * x, axis=-1, keepdims=True) + EPS)


def _inv_unit_lower(a):
    return _inv_unit_lower_many([a])[0]


def _inv_unit_lower_many(mats):
    n = mats[0].shape[0]
    ri, ci = _iota((n, n), 0), _iota((n, n), 1)
    eye = (ri == ci).astype(F32)
    diag_blk = (ri // INV_BLOCK) == (ci // INV_BLOCK)
    mm = lambda x, y: _dotp(x, y, _NN, P_INV)
    sp = lambda x: _Split(x, P_INV)
    d = [jnp.where(diag_blk, a, 0.0) for a in mats]
    nb = [a - di for a, di in zip(mats, d)]
    td = [eye - di for di in d]
    p = d
    for _ in range(int(math.log2(INV_BLOCK)) - 1):
        ps = [sp(pi) for pi in p]
        p = [mm(pi, pi) for pi in ps]
        td = [mm(ti, eye + pi) for ti, pi in zip(td, p)]
    tds = [sp(ti) for ti in td]
    x = [mm(ti, ni) for ti, ni in zip(tds, nb)]
    t = [eye - xi for xi in x]
    p = x
    for _ in range(int(math.log2(n // INV_BLOCK)) - 1):
        ps = [sp(pi) for pi in p]
        p = [mm(pi, pi) for pi in ps]
        t = [mm(ti, eye + pi) for ti, pi in zip(t, p)]
    return [mm(ti, tdi) for ti, tdi in zip(t, tds)]


class _HeadAlgebra:
    def __init__(self, c):
        assert c == HEAD_DIM, "side-by-side head products need CHUNK == HEAD_DIM"
        w = W_MIX
        row, lane = _iota((c, w), 0), _iota((c, w), 1)
        col = lane % HEAD_DIM
        tile_lane = _iota((c, LANES), 1)
        self.tile_head = [tile_lane // HEAD_DIM == h for h in range(LANES // HEAD_DIM)]
        self.eye = (row == col).astype(F32)
        self.strict = row > col
        self.incl = row >= col
        self.inv_blk = (row // INV_BLOCK) == (col // INV_BLOCK)
        r2, c2 = _iota((w, w), 0), _iota((w, w), 1)
        self.eye_full = r2 == c2
        self.same_head = (r2 // HEAD_DIM) == (c2 // HEAD_DIM)

    def bd(self, y):
        yb = y.astype(BF16)
        zero = jnp.zeros((yb.shape[0], LANES), BF16)
        blocks = []
        for t in range(W_MIX // LANES):
            tile = yb[:, t * LANES:(t + 1) * LANES]
            for m in self.tile_head:
                kept = jnp.where(m, tile, zero)
                blocks.append(jnp.concatenate([kept if s == t else zero for s in range(W_MIX // LANES)], axis=1))
        return jnp.concatenate(blocks, axis=0)

    def nn(self, x, bd_y, out=F32):
        return lax.dot_general(x.astype(BF16), bd_y, _NN, preferred_element_type=F32).astype(out)

    def nt(self, x, bd_y):
        return lax.dot_general(x.astype(BF16), bd_y, _NT, preferred_element_type=F32)

    def tn_bd(self, x, y):
        full = lax.dot_general(x.astype(BF16), y.astype(BF16), _TN, preferred_element_type=F32)
        return jnp.where(self.same_head, full, 0.0)

    def diag_bd(self, row_vec):
        return jnp.where(self.eye_full, row_vec, 0.0)

    def plus_eye(self, bd_p):
        return jnp.where(self.eye_full, jnp.ones_like(bd_p), bd_p)

    def inv_unit_lower_many(self, mats):
        mats = [a.astype(BF16) for a in mats]
        zero = jnp.zeros_like(mats[0])
        eye = self.eye.astype(BF16)
        d = [jnp.where(self.inv_blk, a, zero) for a in mats]
        nb = [jnp.where(self.inv_blk, zero, a) for a in mats]
        td = [eye - di for di in d]
        p = d
        bdp = [self.bd(pi) for pi in p]
        for _ in range(int(math.log2(INV_BLOCK)) - 1):
            p = [self.nn(pi, bi, BF16) for pi, bi in zip(p, bdp)]
            bdp = [self.bd(pi) for pi in p]
            td = [self.nn(ti, self.plus_eye(bi), BF16) for ti, bi in zip(td, bdp)]
        bd_td = [self.bd(ti) for ti in td]
        x = [self.nn(ti, self.bd(ni), BF16) for ti, ni in zip(td, nb)]
        t = [eye - xi for xi in x]
        p = x
        bdp = [self.bd(pi) for pi in p]
        for _ in range(int(math.log2(HEAD_DIM // INV_BLOCK)) - 1):
            p = [self.nn(pi, bi, BF16) for pi, bi in zip(p, bdp)]
            bdp = [self.bd(pi) for pi in p]
            t = [self.nn(ti, self.plus_eye(bi), BF16) for ti, bi in zip(t, bdp)]
        return [self.nn(ti, bi, BF16) for ti, bi in zip(t, bd_td)]


def _ada_kernel(c_ref, w_ref, b_ref, o_ref):
    o_ref[...] = _dotp(_silu(c_ref[...]), w_ref[...], _NN, 3) + b_ref[...]


def _ada_call(c_all, ada_w, ada_b):
    n_layers, d, d3 = ada_w.shape
    rows = c_all.shape[0]
    return pl.pallas_call(
        _ada_kernel,
        grid=(n_layers, d3 // d),
        in_specs=[
            pl.BlockSpec((rows, d), lambda l, j: (0, 0)),
            pl.BlockSpec((None, d, d), lambda l, j: (l, 0, j)),
            pl.BlockSpec((None, 1, d), lambda l, j: (l, 0, j)),
        ],
        out_specs=pl.BlockSpec((None, rows, d), lambda l, j: (l, 0, j)),
        out_shape=jax.ShapeDtypeStruct((n_layers, rows, d3), F32),
        compiler_params=pltpu.CompilerParams(dimension_semantics=("arbitrary", "arbitrary"),
                                             vmem_limit_bytes=VMEM_LIMIT),
        name="ada_mod",
    )(c_all, ada_w, ada_b.reshape(n_layers, 1, d3))


def _modulated_norm(x, g, scale, shift):
    return _rms(x) * g * (1.0 + scale) + shift


def _inproj_kernel(x_ref, sc_ref, sh_ref, g_ref, w_ref, oa_ref, ob_ref, oc_ref, od_ref):
    h = _modulated_norm(x_ref[...], g_ref[...], sc_ref[...], sh_ref[...]).astype(BF16)
    lo = 0
    for o_ref in (oa_ref, ob_ref, oc_ref, od_ref):
        wd = o_ref.shape[-1]
        o_ref[...] = jnp.dot(h, w_ref[:, lo:lo + wd], preferred_element_type=F32)
        lo += wd


def _mod_specs(mods, tm, seq_len):
    d = mods[0].shape[-1]
    if seq_len == 1:
        return [m for m in mods], [pl.BlockSpec((tm, d), lambda i: (i, 0)) for _ in mods]
    per_seq = seq_len // tm
    return ([m.reshape(m.shape[0], 1, d) for m in mods],
            [pl.BlockSpec((None, 1, d), lambda i: (i // per_seq, 0, 0)) for _ in mods])


def _inproj_call(x2, scale, shift, g, w_pack, seq_len):
    m, d = x2.shape
    tm = min(PROJ_TILE, m, seq_len) if seq_len > 1 else m
    widths = (A_W, B_W, C_W, D_PACK_W)
    mods, mod_specs = _mod_specs((scale, shift), tm, seq_len)
    return pl.pallas_call(
        _inproj_kernel,
        grid=(m // tm,),
        in_specs=[pl.BlockSpec((tm, d), lambda i: (i, 0))] + mod_specs + [
            pl.BlockSpec((1, d), lambda i: (0, 0)),
            pl.BlockSpec(w_pack.shape, lambda i: (0, 0)),
        ],
        out_specs=[pl.BlockSpec((tm, wd), lambda i: (i, 0)) for wd in widths],
        out_shape=[jax.ShapeDtypeStruct((m, wd), F32) for wd in widths],
        compiler_params=pltpu.CompilerParams(dimension_semantics=("parallel",), vmem_limit_bytes=VMEM_LIMIT),
        name="in_proj",
    )(x2, *mods, g.reshape(1, d), w_pack)


def _outproj_kernel(x_ref, sc_ref, sh_ref, gt_ref, g_ref, ba_ref, bb_ref, bc_ref, bd_ref,
                    wg_ref, wup_ref, wout_ref, fg_ref, o_ref, *, final):
    x = x_ref[...]
    d = x.shape[-1]
    h = _modulated_norm(x, g_ref[...], sc_ref[...], sh_ref[...]).astype(BF16)
    merged = jnp.zeros(x.shape, F32)
    for n, br_ref in enumerate((ba_ref, bb_ref, bc_ref, bd_ref)):
        gl = jnp.dot(h, wg_ref[:, n * d:(n + 1) * d], preferred_element_type=F32)
        up = jnp.dot(br_ref[...].astype(BF16), wup_ref[n], preferred_element_type=F32)
        merged = merged + jax.nn.sigmoid(gl) * up
    out = jnp.dot(merged.astype(BF16), wout_ref[...], preferred_element_type=F32)
    xn = x + gt_ref[...] * out
    if final:
        xn = _rms(xn) * fg_ref[...]
    o_ref[...] = xn


def _outproj_call(x2, scale, shift, gate, g, branches, wg, wup, wout, final_g, seq_len, final):
    m, d = x2.shape
    tm = min(OUT_TILE, m, seq_len) if seq_len > 1 else m
    mods, mod_specs = _mod_specs((scale, shift, gate), tm, seq_len)
    full = lambda a: pl.BlockSpec(a.shape, lambda i: (0,) * a.ndim)
    return pl.pallas_call(
        functools.partial(_outproj_kernel, final=final),
        grid=(m // tm,),
        in_specs=[pl.BlockSpec((tm, d), lambda i: (i, 0))] + mod_specs + [pl.BlockSpec((1, d), lambda i: (0, 0))]
        + [pl.BlockSpec((tm, W_MIX), lambda i: (i, 0)) for _ in branches]
        + [full(wg), full(wup), full(wout), pl.BlockSpec((1, d), lambda i: (0, 0))],
        out_specs=pl.BlockSpec((tm, d), lambda i: (i, 0)),
        out_shape=jax.ShapeDtypeStruct((m, d), F32),
        compiler_params=pltpu.CompilerParams(dimension_semantics=("parallel",), vmem_limit_bytes=VMEM_LIMIT),
        name="out_proj",
    )(x2, *mods, g.reshape(1, d), *branches, wg, wup, wout, final_g.reshape(1, d))


def _rwkv_token_math(pm, w0, w2, a0, a2, k_k, k_a, ones):
    r = pm[:, 0:W_MIX]
    k = pm[:, W_MIX:2 * W_MIX]
    v = pm[:, 2 * W_MIX:3 * W_MIX]
    wd = pm[:, 3 * W_MIX:3 * W_MIX + LORA]
    ad = pm[:, 3 * W_MIX + LORA:]
    w_log = -_softplus(-(w0 + _dotp(jnp.tanh(wd), w2, _NN, P_MISC))) - 0.5
    log_decay = -jnp.exp(w_log)
    a = jax.nn.sigmoid(a0 + _dotp(ad, a2, _NN, P_MISC))
    kx = k * k_k
    kk = kx * lax.rsqrt(_head_sum(kx * kx, ones) + EPS)
    k = k * (1.0 + (a - 1.0) * k_a)
    return r, k, v, log_decay, -kk, kk * a


def _rwkv_finish(o, r, k, v, z, r_k, ln_g, ln_b, ones):
    mean = _head_sum(o, ones, signed=True) * (1.0 / HEAD_DIM)
    dlt = o - mean
    var = _head_sum(dlt * dlt, ones) * (1.0 / HEAD_DIM)
    on = dlt * lax.rsqrt(var + RWKV_GN_EPS) * ln_g + ln_b
    bonus = _head_sum(r * k * r_k, ones, signed=True) * v
    return (on + bonus) * _silu(z)


def _swap_halves(x):
    half = HEAD_DIM // 2
    n = x.shape[-1]
    first = (_iota(x.shape, 1) & half) == 0
    return jnp.where(first, pltpu.roll(x, n - half, axis=1), pltpu.roll(x, half, axis=1))


def _rotary(x, cos, sin):
    return x * cos + _swap_halves(x) * sin


def _lru_token_math(xc, gate_w, gate_b, lam):
    gates = _dotp(xc, gate_w, _NN, P_MISC) + gate_b
    r_gate = jax.nn.sigmoid(gates[:, :W_MIX])
    i_gate = jax.nn.sigmoid(gates[:, W_MIX:])
    log_a = -LRU_C * r_gate * _softplus(-lam)
    a = jnp.exp(log_a)
    b = jnp.sqrt(1.0 - jnp.exp(2.0 * log_a)) * (i_gate * xc)
    return a, b


def _gdn_token_math(qkv, b_raw, a_raw, a_log, dt_bias, ones):
    qkv = _silu(qkv)
    q = qkv[:, 0:W_MIX]
    k = qkv[:, W_MIX:2 * W_MIX]
    v = qkv[:, 2 * W_MIX:]
    q = q * lax.rsqrt(_head_sum(q * q, ones) + EPS) * (HEAD_DIM ** -0.5)
    k = k * lax.rsqrt(_head_sum(k * k, ones) + EPS)
    beta = jax.nn.sigmoid(b_raw)
    g = -jnp.exp(a_log) * _softplus(a_raw + dt_bias)
    return q, k, v, beta, g


def _head_rms_finish(o, z, ones, gain=None):
    y = o * lax.rsqrt(_head_sum(o * o, ones) * (1.0 / HEAD_DIM) + EPS)
    if gain is not None:
        y = y * gain
    return y * _silu(z)


def _conv_tile(u, ext_ref, w_ref, first):
    n = u.shape[0]

    @pl.when(first)
    def _():
        ext_ref[0:SUBLANES, :] = jnp.zeros((SUBLANES, u.shape[1]), F32)

    ext_ref[SUBLANES:SUBLANES + n, :] = u
    out = None
    for j in range(CONV_W):
        back = CONV_W - 1 - j
        term = ext_ref[SUBLANES - back:SUBLANES - back + n, :] * w_ref[j:j + 1, :]
        out = term if out is None else out + term
    ext_ref[0:SUBLANES, :] = u[n - SUBLANES:n, :]
    return out


def _rwkv_kernel(p_ref, mu_ref, w0_ref, w2_ref, a0_ref, a2_ref, kk_ref, ka_ref, rk_ref, lng_ref, lnb_ref,
                 o_ref, s_out_ref, shift_out_ref, s_scr, prev_scr):
    j = pl.program_id(1)
    last = pl.num_programs(1) - 1
    ct = p_ref.shape[0]

    @pl.when(j == 0)
    def _():
        s_scr[...] = jnp.zeros(s_scr.shape, F32)
        prev_scr[...] = jnp.zeros(prev_scr.shape, F32)

    p = p_ref[...]
    pa = p[:, :A_SHIFT_W]
    z = p[:, A_SHIFT_W:]
    rows = _iota((ct, 1), 0)
    prev = jnp.where(rows == 0, prev_scr[...], pltpu.roll(pa, 1, axis=0))
    prev_scr[...] = pa[ct - 1:ct, :]
    pm = pa + (prev - pa) * mu_ref[...]
    ones = _head_ones()
    r, k, v, ld, av, bv = _rwkv_token_math(pm, w0_ref[...], w2_ref[...], a0_ref[...], a2_ref[...],
                                           kk_ref[...], ka_ref[...], ones)

    c = min(CHUNK, ct)
    ha = _HeadAlgebra(c)
    lt = (_iota((c, c), 0) >= _iota((c, c), 1)).astype(BF16)
    units = []
    for c0 in range(0, ct, c):
        sl = slice(c0, c0 + c)
        ldc = ld[sl]
        cum = _dot_const(ldc, lt, _NN, CUMSUM_PIECES, const_left=True)
        e_neg = jnp.exp(-cum)
        e_out = jnp.exp(cum[c - 1:c, :] - cum)
        units.append(dict(a=av[sl] * jnp.exp(cum - ldc), r=r[sl] * jnp.exp(cum), b=bv[sl] * e_neg, k=k[sl] * e_neg,
                          bo=bv[sl] * e_out, ko=k[sl] * e_out, v=v[sl], g=jnp.exp(cum[c - 1:c, :])))
    all_units = units
    state = s_scr[...]
    o_rows = []
    for w0 in range(0, len(all_units), WAVE):
        units = all_units[w0:w0 + WAVE]
        for u in units:
            lhs = jnp.concatenate([u["a"], u["r"]], axis=0)
            u["mb"] = ha.nt(lhs, ha.bd(u["b"]))
            u["mk"] = ha.nt(lhs, ha.bd(u["k"]))
            u["bd_v"] = ha.bd(u["v"])
        for u in units:
            u["m_ab"] = jnp.where(ha.strict, u["mb"][:c], 0.0)
            u["m_ak"] = jnp.where(ha.strict, u["mk"][:c], 0.0)
            u["m_rb"] = jnp.where(ha.incl, u["mb"][c:], 0.0)
            u["m_rk"] = jnp.where(ha.incl, u["mk"][c:], 0.0)
        for u, t_inv in zip(units, ha.inv_unit_lower_many([-u["m_ab"] for u in units])):
            u["t_inv"] = t_inv
        for u in units:
            u["makv"] = ha.nn(u["m_ak"], u["bd_v"], BF16)
        for u in units:
            u["a_hat"] = ha.nn(u["t_inv"], ha.bd(u["a"]), BF16)
            u["u1"] = ha.nn(u["t_inv"], ha.bd(u["makv"]), BF16)
        for u in units:
            u["r_hat"] = u["r"] + ha.nn(u["m_rb"], ha.bd(u["a_hat"]))
            u["o1"] = ha.nn(u["m_rb"], ha.bd(u["u1"])) + ha.nn(u["m_rk"], u["bd_v"])
            u["g_t"] = ha.diag_bd(u["g"]) + ha.tn_bd(u["bo"], u["a_hat"])
            u["h_t"] = ha.tn_bd(jnp.concatenate([u["bo"], u["ko"]], axis=0),
                                jnp.concatenate([u["u1"], u["v"].astype(BF16)], axis=0))
            zz = lax.dot_general(jnp.concatenate([u["r_hat"], u["g_t"]], axis=0).astype(BF16), state.astype(BF16),
                                 _NN, preferred_element_type=F32)
            o_rows.append(zz[:c] + u["o1"])
            state = zz[c:] + u["h_t"]
    o = o_rows[0] if len(o_rows) == 1 else jnp.concatenate(o_rows, axis=0)
    s_scr[...] = state
    o_ref[...] = _rwkv_finish(o, r, k, v, z, rk_ref[...], lng_ref[...], lnb_ref[...], ones)

    @pl.when(j == last)
    def _():
        eye_h = (_iota((HEAD_DIM, HEAD_DIM), 0) == _iota((HEAD_DIM, HEAD_DIM), 1)).astype(F32)
        for h in range(N_HEADS):
            hs = slice(h * HEAD_DIM, (h + 1) * HEAD_DIM)
            s_out_ref[h] = _mm_nt(eye_h, state[hs, hs])
        shift_out_ref[...] = pa[ct - 1:ct, :]


def _row(a):
    return a.reshape(1, -1)


def _rwkv_call(p_a, n_seq, seq_len, lp):
    ct = min(ROW_TILE, seq_len)
    p3 = p_a.reshape(n_seq, seq_len, A_W)
    params = [_row(lp["rwkv_mu"]), _row(lp["rwkv_w0"]), lp["rwkv_w2"], _row(lp["rwkv_a0"]), lp["rwkv_a2"],
              _row(lp["rwkv_k_k"]), _row(lp["rwkv_k_a"]), _row(lp["rwkv_r_k"]), _row(lp["rwkv_ln_g"]),
              _row(lp["rwkv_ln_b"])]
    o, s1, shift1 = pl.pallas_call(
        _rwkv_kernel,
        grid=(n_seq, seq_len // ct),
        in_specs=[pl.BlockSpec((None, ct, A_W), lambda b, j: (b, j, 0))]
        + [pl.BlockSpec(a.shape, lambda b, j: (0, 0)) for a in params],
        out_specs=[
            pl.BlockSpec((None, ct, W_MIX), lambda b, j: (b, j, 0)),
            pl.BlockSpec((None, N_HEADS, HEAD_DIM, HEAD_DIM), lambda b, j: (b, 0, 0, 0)),
            pl.BlockSpec((None, 1, A_SHIFT_W), lambda b, j: (b, 0, 0)),
        ],
        out_shape=[
            jax.ShapeDtypeStruct((n_seq, seq_len, W_MIX), F32),
            jax.ShapeDtypeStruct((n_seq, N_HEADS, HEAD_DIM, HEAD_DIM), F32),
            jax.ShapeDtypeStruct((n_seq, 1, A_SHIFT_W), F32),
        ],
        scratch_shapes=[pltpu.VMEM((W_MIX, W_MIX), F32), pltpu.VMEM((1, A_SHIFT_W), F32)],
        compiler_params=pltpu.CompilerParams(dimension_semantics=("parallel", "arbitrary"),
                                             vmem_limit_bytes=VMEM_LIMIT),
        name="rwkv7_prompt",
    )(p3, *params)
    return o.reshape(n_seq * seq_len, W_MIX), s1, shift1.reshape(n_seq, A_SHIFT_W)


def _ret_kernel(p_ref, cos_ref, sin_ref, o_ref, s_out_ref, s_scr):
    j = pl.program_id(1)
    last = pl.num_programs(1) - 1
    ct = p_ref.shape[0]

    @pl.when(j == 0)
    def _():
        s_scr[...] = jnp.zeros(s_scr.shape, F32)

    p = p_ref[...]
    cos, sin = cos_ref[...], sin_ref[...]
    q = _rotary(p[:, 0:W_MIX], cos, sin)
    k = _rotary(p[:, W_MIX:2 * W_MIX], cos, sin) * (HEAD_DIM ** -0.5)
    v = p[:, 2 * W_MIX:3 * W_MIX]
    z = p[:, 3 * W_MIX:]
    c = min(RET_CHUNK, ct)
    ri, ci = _iota((c, c), 0), _iota((c, c), 1)
    causal = ri >= ci
    rel = jnp.where(causal, ri - ci, 0).astype(F32)
    idx = _iota((c, 1), 0).astype(F32)
    states = [s_scr[h] for h in range(N_HEADS)]
    o_heads_all = [[] for _ in range(N_HEADS)]
    for h in range(N_HEADS):
        lg = math.log(1.0 - 2.0 ** (-5.0 - h))
        decay = jnp.where(causal, jnp.exp(lg * rel), 0.0)
        q_dec = jnp.exp(lg * (idx + 1.0))
        k_dec = jnp.exp(lg * (c - 1.0 - idx))
        g_c = math.exp(lg * c)
        hs = slice(h * HEAD_DIM, (h + 1) * HEAD_DIM)
        s = states[h]
        for c0 in range(0, ct, c):
            sl = slice(c0, c0 + c)
            qh, kh, vh = q[sl, hs], k[sl, hs], v[sl, hs]
            s_in = _dotp(qh, kh, _NT, P_MISC) * decay
            o_heads_all[h].append(_dotp(s_in, vh, _NN, P_MISC) + _dotp(qh, s, _NN, P_MISC) * q_dec)
            s = s * g_c + _dotp(kh * k_dec, vh, _TN, P_MISC)
        states[h] = s
    cols = [oh[0] if len(oh) == 1 else jnp.concatenate(oh, axis=0) for oh in o_heads_all]
    o = jnp.concatenate(cols, axis=1)
    for h in range(N_HEADS):
        s_scr[h] = states[h]
    o_ref[...] = _head_rms_finish(o, z, _head_ones())

    @pl.when(j == last)
    def _():
        for h in range(N_HEADS):
            s_out_ref[h] = states[h]


def _rope_tables(pos):
    half = HEAD_DIM // 2
    inv = ROPE_BASE ** (-jnp.arange(half, dtype=F32) / half)
    ang = pos.astype(F32)[:, None] * inv[None, :]
    cos, sin = jnp.cos(ang), jnp.sin(ang)
    cos_t = jnp.tile(jnp.concatenate([cos, cos], axis=-1), (1, N_HEADS))
    sin_t = jnp.tile(jnp.concatenate([-sin, sin], axis=-1), (1, N_HEADS))
    return cos_t, sin_t


def _ret_call(p_b, n_seq, seq_len, cos_t, sin_t):
    ct = min(ROW_TILE, seq_len)
    p3 = p_b.reshape(n_seq, seq_len, B_W)
    o, s1 = pl.pallas_call(
        _ret_kernel,
        grid=(n_seq, seq_len // ct),
        in_specs=[
            pl.BlockSpec((None, ct, B_W), lambda b, j: (b, j, 0)),
            pl.BlockSpec((ct, W_MIX), lambda b, j: (j, 0)),
            pl.BlockSpec((ct, W_MIX), lambda b, j: (j, 0)),
        ],
        out_specs=[
            pl.BlockSpec((None, ct, W_MIX), lambda b, j: (b, j, 0)),
            pl.BlockSpec((None, N_HEADS, HEAD_DIM, HEAD_DIM), lambda b, j: (b, 0, 0, 0)),
        ],
        out_shape=[
            jax.ShapeDtypeStruct((n_seq, seq_len, W_MIX), F32),
            jax.ShapeDtypeStruct((n_seq, N_HEADS, HEAD_DIM, HEAD_DIM), F32),
        ],
        scratch_shapes=[pltpu.VMEM((N_HEADS, HEAD_DIM, HEAD_DIM), F32)],
        compiler_params=pltpu.CompilerParams(dimension_semantics=("parallel", "arbitrary"),
                                             vmem_limit_bytes=VMEM_LIMIT),
        name="retention_prompt",
    )(p3, cos_t, sin_t)
    return o.reshape(n_seq * seq_len, W_MIX), s1


def _lru_kernel(p_ref, cw_ref, cb_ref, gw_ref, gb_ref, sp_ref, o_ref, h_out_ref, conv_out_ref, ext_scr, h_scr):
    j = pl.program_id(1)
    last = pl.num_programs(1) - 1
    ct = p_ref.shape[0]

    @pl.when(j == 0)
    def _():
        h_scr[...] = jnp.zeros(h_scr.shape, F32)

    p = p_ref[...]
    xr = p[:, :W_MIX]
    z = p[:, W_MIX:]
    xc = _conv_tile(xr, ext_scr, cw_ref, j == 0) + cb_ref[...]
    a, b = _lru_token_math(xc, gw_ref[...], gb_ref[...], sp_ref[...])
    rows = _iota((ct, 1), 0)
    dist = 1
    while dist < ct:
        keep = rows >= dist
        a_prev = jnp.where(keep, pltpu.roll(a, dist, axis=0), 1.0)
        b_prev = jnp.where(keep, pltpu.roll(b, dist, axis=0), 0.0)
        b = a * b_prev + b
        a = a * a_prev
        dist *= 2
    hcur = a * h_scr[...] + b
    h_scr[...] = hcur[ct - 1:ct, :]
    o_ref[...] = hcur * _silu(z)

    @pl.when(j == last)
    def _():
        h_out_ref[...] = hcur[ct - 1:ct, :]
        conv_out_ref[...] = xr[ct - SUBLANES:ct, :]


def _block_diag_gates(gate_w):
    out = jnp.zeros((W_MIX, 2 * W_MIX), F32)
    for g in range(2):
        for n in range(N_HEADS):
            out = out.at[n * HEAD_DIM:(n + 1) * HEAD_DIM,
                         g * W_MIX + n * HEAD_DIM:g * W_MIX + (n + 1) * HEAD_DIM].set(gate_w[g, n])
    return out


def _lru_params(lp):
    return [lp["lru_conv_w"], _row(lp["lru_conv_b"]), _block_diag_gates(lp["lru_gate_w"]),
            _row(lp["lru_gate_b"]), _row(lp["lru_lambda"])]


def _lru_call(p_c, n_seq, seq_len, lp):
    ct = min(ROW_TILE, seq_len)
    p3 = p_c.reshape(n_seq, seq_len, C_W)
    params = _lru_params(lp)
    o, h1, conv_tail = pl.pallas_call(
        _lru_kernel,
        grid=(n_seq, seq_len // ct),
        in_specs=[pl.BlockSpec((None, ct, C_W), lambda b, j: (b, j, 0))]
        + [pl.BlockSpec(a.shape, lambda b, j: (0, 0)) for a in params],
        out_specs=[
            pl.BlockSpec((None, ct, W_MIX), lambda b, j: (b, j, 0)),
            pl.BlockSpec((None, 1, W_MIX), lambda b, j: (b, 0, 0)),
            pl.BlockSpec((None, SUBLANES, W_MIX), lambda b, j: (b, 0, 0)),
        ],
        out_shape=[
            jax.ShapeDtypeStruct((n_seq, seq_len, W_MIX), F32),
            jax.ShapeDtypeStruct((n_seq, 1, W_MIX), F32),
            jax.ShapeDtypeStruct((n_seq, SUBLANES, W_MIX), F32),
        ],
        scratch_shapes=[pltpu.VMEM((ct + SUBLANES, W_MIX), F32), pltpu.VMEM((1, W_MIX), F32)],
        compiler_params=pltpu.CompilerParams(dimension_semantics=("parallel", "arbitrary"),
                                             vmem_limit_bytes=VMEM_LIMIT),
        name="rglru_prompt",
    )(p3, *params)
    return (o.reshape(n_seq * seq_len, W_MIX), h1.reshape(n_seq, W_MIX),
            conv_tail[:, SUBLANES - (CONV_W - 1):, :])


def _gdn_kernel(p_ref, cw_ref, nal_ref, dtb_ref, ng_ref, o_ref, s_out_ref, conv_out_ref, ext_scr, s_scr):
    j = pl.program_id(1)
    last = pl.num_programs(1) - 1
    ct = p_ref.shape[0]

    @pl.when(j == 0)
    def _():
        s_scr[...] = jnp.zeros(s_scr.shape, F32)

    p = p_ref[...]
    raw = p[:, :D_QKV_W]
    z = p[:, D_QKV_W:D_QKV_W + W_MIX]
    b_raw = p[:, D_QKV_W + W_MIX:D_QKV_W + 2 * W_MIX]
    a_raw = p[:, D_QKV_W + 2 * W_MIX:]
    ones = _head_ones()
    qkv = _conv_tile(raw, ext_scr, cw_ref, j == 0)
    q, k, v, beta, g = _gdn_token_math(qkv, b_raw, a_raw, nal_ref[...], dtb_ref[...], ones)

    c = min(CHUNK, ct)
    ha = _HeadAlgebra(c)
    lt = (_iota((c, c), 0) >= _iota((c, c), 1)).astype(BF16)
    units = []
    for c0 in range(0, ct, c):
        sl = slice(c0, c0 + c)
        gc = _dot_const(g[sl], lt, _NN, CUMSUM_PIECES, const_left=True)
        gc_cols = jnp.sum(gc * ha.eye, axis=0, keepdims=True)
        diff = gc - gc_cols
        decay = jnp.where(ha.incl, jnp.exp(jnp.where(ha.incl, diff, 0.0)), 0.0)
        kb = k[sl] * beta[sl]
        e_gc = jnp.exp(gc)
        g_last = gc[c - 1:c, :]
        units.append(dict(decay=decay, kb=kb, q=q[sl], k=k[sl], vb=v[sl] * beta[sl], kbe=kb * e_gc,
                          k_out=k[sl] * jnp.exp(g_last - gc), q_in=q[sl] * e_gc, e_last=jnp.exp(g_last)))
    all_units = units
    state = s_scr[...]
    o_rows = []
    for w0 in range(0, len(all_units), WAVE):
        units = all_units[w0:w0 + WAVE]
        for u in units:
            kq = ha.nt(jnp.concatenate([u["kb"], u["q"]], axis=0), ha.bd(u["k"]))
            u["a_mat"] = jnp.where(ha.strict, kq[:c] * u["decay"], 0.0)
            u["qk"] = kq[c:] * u["decay"]
        for u, t_inv in zip(units, ha.inv_unit_lower_many([u["a_mat"] for u in units])):
            u["t_inv"] = t_inv
        for u in units:
            u["u"] = ha.nn(u["t_inv"], ha.bd(u["vb"]), BF16)
            u["w"] = ha.nn(u["t_inv"], ha.bd(u["kbe"]), BF16)
        for u in units:
            u["g_mat"] = ha.diag_bd(u["e_last"]) - ha.tn_bd(u["k_out"], u["w"])
            u["h_mat"] = ha.tn_bd(u["k_out"], u["u"])
            u["q_hat"] = u["q_in"] - ha.nn(u["qk"], ha.bd(u["w"]))
            u["o1"] = ha.nn(u["qk"], ha.bd(u["u"]))
            zz = lax.dot_general(jnp.concatenate([u["q_hat"], u["g_mat"]], axis=0).astype(BF16),
                                 state.astype(BF16), _NN, preferred_element_type=F32)
            o_rows.append(zz[:c] + u["o1"])
            state = zz[c:] + u["h_mat"]
    o = o_rows[0] if len(o_rows) == 1 else jnp.concatenate(o_rows, axis=0)
    s_scr[...] = state
    o_ref[...] = _head_rms_finish(o, z, ones, ng_ref[...])

    @pl.when(j == last)
    def _():
        for h in range(N_HEADS):
            hs = slice(h * HEAD_DIM, (h + 1) * HEAD_DIM)
            s_out_ref[h] = state[hs, hs]
        conv_out_ref[...] = raw[ct - SUBLANES:ct, :]


def _gdn_params(lp):
    return [lp["gdn_conv_w"], _row(jnp.repeat(lp["gdn_A_log"], HEAD_DIM)),
            _row(jnp.repeat(lp["gdn_dt_bias"], HEAD_DIM)), _row(jnp.tile(lp["gdn_norm_g"], N_HEADS))]


def _gdn_call(p_d, n_seq, seq_len, lp):
    ct = min(ROW_TILE, seq_len)
    p3 = p_d.reshape(n_seq, seq_len, D_PACK_W)
    params = _gdn_params(lp)
    o, s1, conv_tail = pl.pallas_call(
        _gdn_kernel,
        grid=(n_seq, seq_len // ct),
        in_specs=[pl.BlockSpec((None, ct, D_PACK_W), lambda b, j: (b, j, 0))]
        + [pl.BlockSpec(a.shape, lambda b, j: (0, 0)) for a in params],
        out_specs=[
            pl.BlockSpec((None, ct, W_MIX), lambda b, j: (b, j, 0)),
            pl.BlockSpec((None, N_HEADS, HEAD_DIM, HEAD_DIM), lambda b, j: (b, 0, 0, 0)),
            pl.BlockSpec((None, SUBLANES, D_QKV_W), lambda b, j: (b, 0, 0)),
        ],
        out_shape=[
            jax.ShapeDtypeStruct((n_seq, seq_len, W_MIX), F32),
            jax.ShapeDtypeStruct((n_seq, N_HEADS, HEAD_DIM, HEAD_DIM), F32),
            jax.ShapeDtypeStruct((n_seq, SUBLANES, D_QKV_W), F32),
        ],
        scratch_shapes=[pltpu.VMEM((ct + SUBLANES, D_QKV_W), F32), pltpu.VMEM((W_MIX, W_MIX), F32)],
        compiler_params=pltpu.CompilerParams(dimension_semantics=("parallel", "arbitrary"),
                                             vmem_limit_bytes=VMEM_LIMIT),
        name="gdn_prompt",
    )(p3, *params)
    return o.reshape(n_seq * seq_len, W_MIX), s1, conv_tail[:, SUBLANES - (CONV_W - 1):, :]


def _decode_pre_kernel(pa_ref, pb_ref, pc_ref, pd_ref, shift_ref, h0_ref, lconv_ref, gconv_ref, cos_ref, sin_ref,
                       mu_ref, w0_ref, w2_ref, a0_ref, a2_ref, kk_ref, ka_ref,
                       lcw_ref, lcb_ref, lgw_ref, lgb_ref, lsp_ref, gcw_ref, nal_ref, dtb_ref,
                       vt_ref, vn_ref, oc_ref, h1_ref, lconv1_ref, gconv1_ref):
    ones = _head_ones()
    pa_full = pa_ref[...]
    pa = pa_full[:, :A_SHIFT_W]
    pm = pa + (shift_ref[...] - pa) * mu_ref[...]
    r, k, v, ld, av, bv = _rwkv_token_math(pm, w0_ref[...], w2_ref[...], a0_ref[...], a2_ref[...],
                                           kk_ref[...], ka_ref[...], ones)
    vecs = [r, jnp.exp(ld), k, v, av, bv]
    plain = [r, k, v, pa_full[:, A_SHIFT_W:]]
    pb = pb_ref[...]
    cos, sin = cos_ref[...], sin_ref[...]
    vecs += [_rotary(pb[:, 0:W_MIX], cos, sin), _rotary(pb[:, W_MIX:2 * W_MIX], cos, sin) * (HEAD_DIM ** -0.5),
             pb[:, 2 * W_MIX:3 * W_MIX]]
    plain.append(pb[:, 3 * W_MIX:])
    pc = pc_ref[...]
    xr = pc[:, :W_MIX]
    taps = [lconv_ref[i] for i in range(CONV_W - 1)] + [xr]
    xc = taps[0] * lcw_ref[0:1, :]
    for i in range(1, CONV_W):
        xc = xc + taps[i] * lcw_ref[i:i + 1, :]
    xc = xc + lcb_ref[...]
    a, b = _lru_token_math(xc, lgw_ref[...], lgb_ref[...], lsp_ref[...])
    hcur = a * h0_ref[...] + b
    oc_ref[...] = hcur * _silu(pc[:, W_MIX:])
    h1_ref[...] = hcur
    for i in range(CONV_W - 1):
        lconv1_ref[i] = taps[i + 1]
    pd = pd_ref[...]
    raw = pd[:, :D_QKV_W]
    gtaps = [gconv_ref[i] for i in range(CONV_W - 1)] + [raw]
    qkv = gtaps[0] * gcw_ref[0:1, :]
    for i in range(1, CONV_W):
        qkv = qkv + gtaps[i] * gcw_ref[i:i + 1, :]
    q, kg, vg, beta, g = _gdn_token_math(qkv, pd[:, D_QKV_W + W_MIX:D_QKV_W + 2 * W_MIX],
                                         pd[:, D_QKV_W + 2 * W_MIX:], nal_ref[...], dtb_ref[...], ones)
    for i in range(CONV_W - 1):
        gconv1_ref[i] = gtaps[i + 1]
    vecs += [q, kg, vg, beta, g]
    plain.append(pd[:, D_QKV_W:D_QKV_W + W_MIX])
    assert len(vecs) == N_VEC_T and len(plain) == N_VEC_PLAIN
    for i, vec in enumerate(vecs):
        vt_ref[i] = vec.T
    for i, vec in enumerate(plain):
        vn_ref[i] = vec


def _decode_state_kernel(vt_ref, wkv_ref, ret_ref, gdn_ref, gam_ref, wkv1_ref, ret1_ref, gdn1_ref, o_ref):
    v_r, v_w, v_k, v_v, v_a, v_b, r_q, r_k, r_v, g_q, g_k, g_v, g_beta, g_g = range(N_VEC_T)
    hd = HEAD_DIM
    n = vt_ref.shape[-1]
    row = lambda idx, i: vt_ref[idx, pl.ds(i, 1), :]
    rows_of = lambda i: pl.ds(pl.multiple_of(i * hd, hd), hd)
    gamma = gam_ref[...]
    beta = vt_ref[g_beta, 0:1, :]
    eg = jnp.exp(vt_ref[g_g, 0:1, :])

    def first_pass(i, carry):
        acc_ret, acc_w, acc_q = carry
        rows = rows_of(i)
        s = wkv_ref[rows, :]
        sa = jnp.sum(s * vt_ref[v_a], axis=0, keepdims=True)
        s = s * vt_ref[v_w] + sa * vt_ref[v_b] + row(v_v, i) * vt_ref[v_k]
        wkv1_ref[rows, :] = s
        o_ref[0, pl.ds(i, 1), :] = jnp.sum(s * vt_ref[v_r], axis=0, keepdims=True)
        s = ret_ref[rows, :] * gamma + row(r_k, i) * vt_ref[r_v]
        ret1_ref[rows, :] = s
        acc_ret = acc_ret + row(r_q, i) * s
        s = gdn_ref[rows, :]
        return acc_ret, acc_w + row(g_k, i) * s, acc_q + row(g_q, i) * s

    zeros = jnp.zeros((hd, n), F32)
    acc_ret, acc_w, acc_q = lax.fori_loop(0, hd, first_pass, (zeros, zeros, zeros))
    o_ref[1] = acc_ret
    v_new = vt_ref[g_v] * beta - acc_w * (beta * eg)
    qk = jnp.sum(vt_ref[g_q] * vt_ref[g_k], axis=0, keepdims=True)
    o_ref[2] = acc_q * eg + qk * v_new

    def second_pass(i, carry):
        rows = rows_of(i)
        gdn1_ref[rows, :] = gdn_ref[rows, :] * eg + row(g_k, i) * v_new
        return carry

    lax.fori_loop(0, hd, second_pass, 0)


def _decode_finish_kernel(ot_ref, vn_ref, rk_ref, lng_ref, lnb_ref, ng_ref, oa_ref, ob_ref, od_ref):
    ones = _head_ones()
    r, k, v, z_a, z_b, z_d = (vn_ref[i] for i in range(N_VEC_PLAIN))
    oa_ref[...] = _rwkv_finish(ot_ref[0].T, r, k, v, z_a, rk_ref[...], lng_ref[...], lnb_ref[...], ones)
    ob_ref[...] = _head_rms_finish(ot_ref[1].T, z_b, ones)
    od_ref[...] = _head_rms_finish(ot_ref[2].T, z_d, ones, ng_ref[...])


def _batch_minor(state):
    n_layers, n = state.shape[:2]
    return jnp.transpose(state, (0, 2, 3, 4, 1)).reshape(n_layers, -1, n)


def _batch_major(flat_state):
    n = flat_state.shape[-1]
    return jnp.transpose(flat_state.reshape(N_HEADS, HEAD_DIM, HEAD_DIM, n), (3, 0, 1, 2))


def _decode_layer(l, p_a, p_b, p_c, p_d, carried, cos_t, sin_t, lp):
    n = p_a.shape[0]
    assert n % LANES == 0, "the decode state kernel keeps the batch on lanes"
    flat = HEAD_DIM * HEAD_DIM
    taps = CONV_W - 1
    rwkv_params = [_row(lp["rwkv_mu"]), _row(lp["rwkv_w0"]), lp["rwkv_w2"], _row(lp["rwkv_a0"]), lp["rwkv_a2"],
                   _row(lp["rwkv_k_k"]), _row(lp["rwkv_k_a"])]
    gdn_params = _gdn_params(lp)
    full = lambda a: pl.BlockSpec(a.shape, lambda i: (0,) * a.ndim)
    layer_blk = lambda a: pl.BlockSpec((None,) + a.shape[1:], lambda i, nd=a.ndim: (l,) + (0,) * (nd - 1))
    projs = [p_a, p_b, p_c, p_d]
    layered = [carried["shift"], carried["lru_h"], carried["lru_conv"], carried["gdn_conv"]]
    consts = [cos_t, sin_t, *rwkv_params, *_lru_params(lp), *gdn_params[:3]]
    out_shapes = [
        jax.ShapeDtypeStruct((N_VEC_T, W_MIX, n), F32),
        jax.ShapeDtypeStruct((N_VEC_PLAIN, n, W_MIX), F32),
        jax.ShapeDtypeStruct((n, W_MIX), F32),
        jax.ShapeDtypeStruct((n, W_MIX), F32),
        jax.ShapeDtypeStruct((taps, n, W_MIX), F32),
        jax.ShapeDtypeStruct((taps, n, D_QKV_W), F32),
    ]
    vec_t, vec_n, o_c, lru_h1, lru_conv1, gdn_conv1 = pl.pallas_call(
        _decode_pre_kernel,
        grid=(1,),
        in_specs=[full(a) for a in projs] + [layer_blk(a) for a in layered] + [full(a) for a in consts],
        out_specs=[pl.BlockSpec(s.shape, lambda i, nd=len(s.shape): (0,) * nd) for s in out_shapes],
        out_shape=out_shapes,
        compiler_params=pltpu.CompilerParams(dimension_semantics=("arbitrary",), vmem_limit_bytes=VMEM_LIMIT),
        name="decode_tokens",
    )(*projs, *layered, *consts)

    gam = jnp.broadcast_to((1.0 - 2.0 ** (-5.0 - jnp.arange(N_HEADS, dtype=F32)))[:, None, None], (N_HEADS, 1, n))
    state_in = pl.BlockSpec((None, flat, n), lambda h: (l, h, 0))
    state_out = pl.BlockSpec((flat, n), lambda h: (h, 0))
    wkv1, ret1, gdn1, o_t = pl.pallas_call(
        _decode_state_kernel,
        grid=(N_HEADS,),
        in_specs=[pl.BlockSpec((N_VEC_T, HEAD_DIM, n), lambda h: (0, h, 0)), state_in, state_in, state_in,
                  pl.BlockSpec((None, 1, n), lambda h: (h, 0, 0))],
        out_specs=[state_out, state_out, state_out, pl.BlockSpec((3, HEAD_DIM, n), lambda h: (0, h, 0))],
        out_shape=[jax.ShapeDtypeStruct((N_HEADS * flat, n), F32)] * 3 + [jax.ShapeDtypeStruct((3, W_MIX, n), F32)],
        compiler_params=pltpu.CompilerParams(dimension_semantics=("parallel",), vmem_limit_bytes=VMEM_LIMIT),
        name="decode_states",
    )(vec_t, carried["wkv"], carried["ret"], carried["gdn"], gam)

    finish_ins = [o_t, vec_n, _row(lp["rwkv_r_k"]), _row(lp["rwkv_ln_g"]), _row(lp["rwkv_ln_b"]), gdn_params[3]]
    o_a, o_b, o_d = pl.pallas_call(
        _decode_finish_kernel,
        grid=(1,),
        in_specs=[full(a) for a in finish_ins],
        out_specs=[pl.BlockSpec((n, W_MIX), lambda i: (0, 0))] * 3,
        out_shape=[jax.ShapeDtypeStruct((n, W_MIX), F32)] * 3,
        compiler_params=pltpu.CompilerParams(dimension_semantics=("arbitrary",), vmem_limit_bytes=VMEM_LIMIT),
        name="decode_finish",
    )(*finish_ins)
    new_states = (_batch_major(wkv1), p_a[:, :A_SHIFT_W], _batch_major(ret1), lru_h1,
                  jnp.transpose(lru_conv1, (1, 0, 2)), _batch_major(gdn1), jnp.transpose(gdn_conv1, (1, 0, 2)))
    return (o_a, o_b, o_c, o_d), new_states


def _pack_in_weights(w_in):
    off_d = A_W + B_W + C_W
    off_ba = off_d + D_QKV_W
    off_z = off_ba + 2 * N_HEADS
    off_g = off_d + D_W
    w_pack = jnp.concatenate([
        w_in[:, :off_ba],
        w_in[:, off_z:off_g],
        jnp.repeat(w_in[:, off_ba:off_ba + N_HEADS], HEAD_DIM, axis=1),
        jnp.repeat(w_in[:, off_ba + N_HEADS:off_z], HEAD_DIM, axis=1),
    ], axis=1)
    return w_pack, w_in[:, off_g:]


def _prompt_layer(p_a, p_b, p_c, p_d, n_seq, seq_len, cos_t, sin_t, lp):
    o_a, wkv1, shift1 = _rwkv_call(p_a, n_seq, seq_len, lp)
    o_b, ret1 = _ret_call(p_b, n_seq, seq_len, cos_t, sin_t)
    o_c, lru_h1, lru_conv1 = _lru_call(p_c, n_seq, seq_len, lp)
    o_d, gdn1, gdn_conv1 = _gdn_call(p_d, n_seq, seq_len, lp)
    return (o_a, o_b, o_c, o_d), (wkv1, shift1, ret1, lru_h1, lru_conv1, gdn1, gdn_conv1)


def _run_group(x, mods, pos, carried, layers, final_g):
    n_seq, seq_len, d = x.shape
    x2 = x.reshape(n_seq * seq_len, d)
    cos_t, sin_t = _rope_tables(pos)
    new = []
    n_layers = len(layers)
    for l, lp in enumerate(layers):
        shift, scale, gate = (mods[l][:, i * d:(i + 1) * d] for i in range(3))
        p_a, p_b, p_c, p_d = _inproj_call(x2, scale, shift, lp["norm_g"], lp["w_pack"], seq_len)
        if carried is None:
            branches, st = _prompt_layer(p_a, p_b, p_c, p_d, n_seq, seq_len, cos_t, sin_t, lp)
        else:
            branches, st = _decode_layer(l, p_a, p_b, p_c, p_d, carried, cos_t, sin_t, lp)
        new.append(st)
        x2 = _outproj_call(x2, scale, shift, gate, lp["norm_g"], branches, lp["w_gate"], lp["w_up_bf16"],
                           lp["w_out_bf16"], final_g, seq_len, final=(l == n_layers - 1))
    stacked = tuple(jnp.stack([s[i] for s in new], axis=0) for i in range(7))
    return x2.reshape(n_seq, seq_len, d), stacked


def kernel(x_prompt, x_sample, c_prompt, c_sample, state_rwkv_wkv, state_rwkv_shift, state_ret, state_lru_h, state_lru_conv, state_gdn, state_gdn_conv, ada_w, ada_b, norm_g, w_in, rwkv_mu, rwkv_w0, rwkv_w2, rwkv_a0, rwkv_a2, rwkv_k_k, rwkv_k_a, rwkv_r_k, rwkv_ln_g, rwkv_ln_b, lru_conv_w, lru_conv_b, lru_gate_w, lru_gate_b, lru_lambda, gdn_conv_w, gdn_A_log, gdn_dt_bias, gdn_norm_g, w_up, w_out, final_g):
    n_layers = ada_w.shape[0]
    n_prompt, seq_len, _ = x_prompt.shape
    n_sample, dec_len, _ = x_sample.shape
    assert dec_len == 1, "the decode path handles one token per sequence"
    layers = []
    w_in_bf16 = w_in.astype(BF16)
    for l in range(n_layers):
        w_pack, w_gate = _pack_in_weights(w_in_bf16[l])
        layers.append(dict(
            norm_g=norm_g[l], w_pack=w_pack, w_gate=w_gate, w_up_bf16=w_up[l].astype(BF16),
            w_out_bf16=w_out[l].astype(BF16),
            rwkv_mu=rwkv_mu[l], rwkv_w0=rwkv_w0[l], rwkv_w2=rwkv_w2[l], rwkv_a0=rwkv_a0[l], rwkv_a2=rwkv_a2[l],
            rwkv_k_k=rwkv_k_k[l], rwkv_k_a=rwkv_k_a[l], rwkv_r_k=rwkv_r_k[l], rwkv_ln_g=rwkv_ln_g[l],
            rwkv_ln_b=rwkv_ln_b[l], lru_conv_w=lru_conv_w[l], lru_conv_b=lru_conv_b[l],
            lru_gate_w=lru_gate_w[l], lru_gate_b=lru_gate_b[l], lru_lambda=lru_lambda[l],
            gdn_conv_w=gdn_conv_w[l], gdn_A_log=gdn_A_log[l], gdn_dt_bias=gdn_dt_bias[l],
            gdn_norm_g=gdn_norm_g[l]))
    mods = _ada_call(jnp.concatenate([c_prompt, c_sample], axis=0), ada_w, ada_b)
    mods_p = [mods[l, :n_prompt] for l in range(n_layers)]
    mods_s = [mods[l, n_prompt:] for l in range(n_layers)]

    y_prompt, new_p = _run_group(x_prompt, mods_p, jnp.arange(seq_len, dtype=jnp.int32), None, layers, final_g)
    carried = dict(wkv=_batch_minor(state_rwkv_wkv), ret=_batch_minor(state_ret), gdn=_batch_minor(state_gdn),
                   shift=state_rwkv_shift, lru_h=state_lru_h,
                   lru_conv=jnp.transpose(state_lru_conv, (0, 2, 1, 3)),
                   gdn_conv=jnp.transpose(state_gdn_conv, (0, 2, 1, 3)))
    pos_s = PAST_LEN + jnp.arange(dec_len, dtype=jnp.int32)
    y_sample, new_s = _run_group(x_sample, mods_s, pos_s, carried, layers, final_g)
    return (y_prompt, y_sample) + new_p + new_s
```

```python
import functools
import math

import jax
import jax.numpy as jnp
from jax import lax
from jax.experimental import pallas as pl
from jax.experimental.pallas import tpu as pltpu

F32 = jnp.float32
BF16 = jnp.bfloat16
HI = lax.Precision.HIGHEST

N_HEADS = 4
HEAD_DIM = 64
W_MIX = N_HEADS * HEAD_DIM
LORA = 64
CONV_W = 4
N_BRANCH = 4
LRU_C = 8.0
ROPE_BASE = 10000.0
EPS = 1e-6
RWKV_GN_EPS = 64e-5
PAST_LEN = 16384
A_SHIFT_W = 3 * W_MIX + 2 * LORA
A_W = A_SHIFT_W + W_MIX
B_W = 4 * W_MIX
C_W = 2 * W_MIX
D_QKV_W = 3 * W_MIX
D_W = D_QKV_W + 2 * N_HEADS + W_MIX
D_PACK_W = D_QKV_W + 3 * W_MIX

SUBLANES = 8
LANES = 128
VMEM_LIMIT = 56 * 1024 * 1024

CHUNK = 64
RET_CHUNK = 128
INV_BLOCK = 16
WAVE = 8
ROW_TILE = 512
PROJ_TILE = 512
OUT_TILE = 1024
N_VEC_T = 14
N_VEC_PLAIN = 6


def _mm(a, b, prec=HI):
    return lax.dot_general(a, b, (((1,), (0,)), ((), ())), precision=prec, preferred_element_type=F32)


def _mm_nt(a, b, prec=HI):
    return lax.dot_general(a, b, (((1,), (1,)), ((), ())), precision=prec, preferred_element_type=F32)


def _mm_tn(a, b, prec=HI):
    return lax.dot_general(a, b, (((0,), (0,)), ((), ())), precision=prec, preferred_element_type=F32)


_NN = (((1,), (0,)), ((), ()))
_NT = (((1,), (1,)), ((), ()))
_TN = (((0,), (0,)), ((), ()))

P_INV = 1
P_STATE = 1
P_MISC = 1
HEAD_SUM_PIECES = 1
CUMSUM_PIECES = 2


class _Split:
    def __init__(self, x, passes):
        self.hi = x.astype(BF16)
        self.lo = (x - self.hi.astype(F32)).astype(BF16) if passes > 1 else None


def _dotp(a, b, dims=_NN, passes=1):
    a = a if isinstance(a, _Split) else _Split(a, passes)
    b = b if isinstance(b, _Split) else _Split(b, passes)
    d = lambda x, y: lax.dot_general(x, y, dims, preferred_element_type=F32)
    out = d(a.hi, b.hi)
    if passes > 1:
        out = out + (d(a.hi, b.lo) + d(a.lo, b.hi))
    return out


def _iota(shape, dim):
    return lax.broadcasted_iota(jnp.int32, shape, dim)


def _silu(x):
    return x * jax.nn.sigmoid(x)


def _softplus(x):
    return jnp.maximum(x, 0.0) + jnp.log1p(jnp.exp(-jnp.abs(x)))


def _pieces(x, n):
    out = []
    for i in range(n):
        p = x.astype(BF16)
        out.append(p)
        if i + 1 < n:
            x = x - p.astype(F32)
    return out


def _dot_const(x, const, dims=_NN, n=2, const_left=False):
    out = None
    for p in _pieces(x, n):
        t = lax.dot_general(*((const, p) if const_left else (p, const)), dims, preferred_element_type=F32)
        out = t if out is None else out + t
    return out


def _head_ones():
    return (_iota((W_MIX, W_MIX), 0) // HEAD_DIM == _iota((W_MIX, W_MIX), 1) // HEAD_DIM).astype(BF16)


def _head_sum(x, ones, signed=False):
    return _dot_const(x, ones, n=HEAD_SUM_PIECES + (1 if signed else 0))


def _rms(x):
    return x * lax.rsqrt(jnp.mean(x * x, axis=-1, keepdims=True) + EPS)


def _inv_unit_lower(a):
    return _inv_unit_lower_many([a])[0]


def _inv_unit_lower_many(mats):
    n = mats[0].shape[0]
    ri, ci = _iota((n, n), 0), _iota((n, n), 1)
    eye = (ri == ci).astype(F32)
    diag_blk = (ri // INV_BLOCK) == (ci // INV_BLOCK)
    mm = lambda x, y: _dotp(x, y, _NN, P_INV)
    sp = lambda x: _Split(x, P_INV)
    d = [jnp.where(diag_blk, a, 0.0) for a in mats]
    nb = [a - di for a, di in zip(mats, d)]
    td = [eye - di for di in d]
    p = d
    for _ in range(int(math.log2(INV_BLOCK)) - 1):
        ps = [sp(pi) for pi in p]
        p = [mm(pi, pi) for pi in ps]
        td = [mm(ti, eye + pi) for ti, pi in zip(td, p)]
    tds = [sp(ti) for ti in td]
    x = [mm(ti, ni) for ti, ni in zip(tds, nb)]
    t = [eye - xi for xi in x]
    p = x
    for _ in range(int(math.log2(n // INV_BLOCK)) - 1):
        ps = [sp(pi) for pi in p]
        p = [mm(pi, pi) for pi in ps]
        t = [mm(ti, eye + pi) for ti, pi in zip(t, p)]
    return [mm(ti, tdi) for ti, tdi in zip(t, tds)]


class _HeadAlgebra:
    def __init__(self, c):
        assert c == HEAD_DIM, "side-by-side head products need CHUNK == HEAD_DIM"
        w = W_MIX
        row, lane = _iota((c, w), 0), _iota((c, w), 1)
        col = lane % HEAD_DIM
        tile_lane = _iota((c, LANES), 1)
        self.tile_head = [tile_lane // HEAD_DIM == h for h in range(LANES // HEAD_DIM)]
        self.eye = (row == col).astype(F32)
        self.strict = row > col
        self.incl = row >= col
        self.inv_blk = (row // INV_BLOCK) == (col // INV_BLOCK)
        r2, c2 = _iota((w, w), 0), _iota((w, w), 1)
        self.eye_full = r2 == c2
        self.same_head = (r2 // HEAD_DIM) == (c2 // HEAD_DIM)

    def bd(self, y):
        yb = y.astype(BF16)
        zero = jnp.zeros((yb.shape[0], LANES), BF16)
        blocks = []
        for t in range(W_MIX // LANES):
            tile = yb[:, t * LANES:(t + 1) * LANES]
            for m in self.tile_head:
                kept = jnp.where(m, tile, zero)
                blocks.append(jnp.concatenate([kept if s == t else zero for s in range(W_MIX // LANES)], axis=1))
        return jnp.concatenate(blocks, axis=0)

    def nn(self, x, bd_y, out=F32):
        return lax.dot_general(x.astype(BF16), bd_y, _NN, preferred_element_type=F32).astype(out)

    def nt(self, x, bd_y):
        return lax.dot_general(x.astype(BF16), bd_y, _NT, preferred_element_type=F32)

    def tn_bd(self, x, y):
        full = lax.dot_general(x.astype(BF16), y.astype(BF16), _TN, preferred_element_type=F32)
        return jnp.where(self.same_head, full, 0.0)

    def diag_bd(self, row_vec):
        return jnp.where(self.eye_full, row_vec, 0.0)

    def plus_eye(self, bd_p):
        return jnp.where(self.eye_full, jnp.ones_like(bd_p), bd_p)

    def inv_unit_lower_many(self, mats):
        mats = [a.astype(BF16) for a in mats]
        zero = jnp.zeros_like(mats[0])
        eye = self.eye.astype(BF16)
        d = [jnp.where(self.inv_blk, a, zero) for a in mats]
        nb = [jnp.where(self.inv_blk, zero, a) for a in mats]
        td = [eye - di for di in d]
        p = d
        bdp = [self.bd(pi) for pi in p]
        for _ in range(int(math.log2(INV_BLOCK)) - 1):
            p = [self.nn(pi, bi, BF16) for pi, bi in zip(p, bdp)]
            bdp = [self.bd(pi) for pi in p]
            td = [self.nn(ti, self.plus_eye(bi), BF16) for ti, bi in zip(td, bdp)]
        bd_td = [self.bd(ti) for ti in td]
        x = [self.nn(ti, self.bd(ni), BF16) for ti, ni in zip(td, nb)]
        t = [eye - xi for xi in x]
        p = x
        bdp = [self.bd(pi) for pi in p]
        for _ in range(int(math.log2(HEAD_DIM // INV_BLOCK)) - 1):
            p = [self.nn(pi, bi, BF16) for pi, bi in zip(p, bdp)]
            bdp = [self.bd(pi) for pi in p]
            t = [self.nn(ti, self.plus_eye(bi), BF16) for ti, bi in zip(t, bdp)]
        return [self.nn(ti, bi, BF16) for ti, bi in zip(t, bd_td)]


def _ada_kernel(c_ref, w_ref, b_ref, o_ref):
    o_ref[...] = _dotp(_silu(c_ref[...]), w_ref[...], _NN, 3) + b_ref[...]


def _ada_call(c_all, ada_w, ada_b):
    n_layers, d, d3 = ada_w.shape
    rows = c_all.shape[0]
    return pl.pallas_call(
        _ada_kernel,
        grid=(n_layers, d3 // d),
        in_specs=[
            pl.BlockSpec((rows, d), lambda l, j: (0, 0)),
            pl.BlockSpec((None, d, d), lambda l, j: (l, 0, j)),
            pl.BlockSpec((None, 1, d), lambda l, j: (l, 0, j)),
        ],
        out_specs=pl.BlockSpec((None, rows, d), lambda l, j: (l, 0, j)),
        out_shape=jax.ShapeDtypeStruct((n_layers, rows, d3), F32),
        compiler_params=pltpu.CompilerParams(dimension_semantics=("arbitrary", "arbitrary"),
                                             vmem_limit_bytes=VMEM_LIMIT),
        name="ada_mod",
    )(c_all, ada_w, ada_b.reshape(n_layers, 1, d3))


def _modulated_norm(x, g, scale, shift):
    return _rms(x) * g * (1.0 + scale) + shift


def _inproj_kernel(x_ref, sc_ref, sh_ref, g_ref, w_ref, oa_ref, ob_ref, oc_ref, od_ref):
    h = _modulated_norm(x_ref[...], g_ref[...], sc_ref[...], sh_ref[...]).astype(BF16)
    lo = 0
    for o_ref in (oa_ref, ob_ref, oc_ref, od_ref):
        wd = o_ref.shape[-1]
        o_ref[...] = jnp.dot(h, w_ref[:, lo:lo + wd], preferred_element_type=F32)
        lo += wd


def _mod_specs(mods, tm, seq_len):
    d = mods[0].shape[-1]
    if seq_len == 1:
        return [m for m in mods], [pl.BlockSpec((tm, d), lambda i: (i, 0)) for _ in mods]
    per_seq = seq_len // tm
    return ([m.reshape(m.shape[0], 1, d) for m in mods],
            [pl.BlockSpec((None, 1, d), lambda i: (i // per_seq, 0, 0)) for _ in mods])


def _inproj_call(x2, scale, shift, g, w_pack, seq_len):
    m, d = x2.shape
    tm = min(PROJ_TILE, m, seq_len) if seq_len > 1 else m
    widths = (A_W, B_W, C_W, D_PACK_W)
    mods, mod_specs = _mod_specs((scale, shift), tm, seq_len)
    return pl.pallas_call(
        _inproj_kernel,
        grid=(m // tm,),
        in_specs=[pl.BlockSpec((tm, d), lambda i: (i, 0))] + mod_specs + [
            pl.BlockSpec((1, d), lambda i: (0, 0)),
            pl.BlockSpec(w_pack.shape, lambda i: (0, 0)),
        ],
        out_specs=[pl.BlockSpec((tm, wd), lambda i: (i, 0)) for wd in widths],
        out_shape=[jax.ShapeDtypeStruct((m, wd), F32) for wd in widths],
        compiler_params=pltpu.CompilerParams(dimension_semantics=("parallel",), vmem_limit_bytes=VMEM_LIMIT),
        name="in_proj",
    )(x2, *mods, g.reshape(1, d), w_pack)


def _outproj_kernel(x_ref, sc_ref, sh_ref, gt_ref, g_ref, ba_ref, bb_ref, bc_ref, bd_ref,
                    wg_ref, wup_ref, wout_ref, fg_ref, o_ref, *, final):
    x = x_ref[...]
    d = x.shape[-1]
    h = _modulated_norm(x, g_ref[...], sc_ref[...], sh_ref[...]).astype(BF16)
    merged = jnp.zeros(x.shape, F32)
    for n, br_ref in enumerate((ba_ref, bb_ref, bc_ref, bd_ref)):
        gl = jnp.dot(h, wg_ref[:, n * d:(n + 1) * d], preferred_element_type=F32)
        up = jnp.dot(br_ref[...].astype(BF16), wup_ref[n], preferred_element_type=F32)
        merged = merged + jax.nn.sigmoid(gl) * up
    out = jnp.dot(merged.astype(BF16), wout_ref[...], preferred_element_type=F32)
    xn = x + gt_ref[...] * out
    if final:
        xn = _rms(xn) * fg_ref[...]
    o_ref[...] = xn


def _outproj_call(x2, scale, shift, gate, g, branches, wg, wup, wout, final_g, seq_len, final):
    m, d = x2.shape
    tm = min(OUT_TILE, m, seq_len) if seq_len > 1 else m
    mods, mod_specs = _mod_specs((scale, shift, gate), tm, seq_len)
    full = lambda a: pl.BlockSpec(a.shape, lambda i: (0,) * a.ndim)
    return pl.pallas_call(
        functools.partial(_outproj_kernel, final=final),
        grid=(m // tm,),
        in_specs=[pl.BlockSpec((tm, d), lambda i: (i, 0))] + mod_specs + [pl.BlockSpec((1, d), lambda i: (0, 0))]
        + [pl.BlockSpec((tm, W_MIX), lambda i: (i, 0)) for _ in branches]
        + [full(wg), full(wup), full(wout), pl.BlockSpec((1, d), lambda i: (0, 0))],
        out_specs=pl.BlockSpec((tm, d), lambda i: (i, 0)),
        out_shape=jax.ShapeDtypeStruct((m, d), F32),
        compiler_params=pltpu.CompilerParams(dimension_semantics=("parallel",), vmem_limit_bytes=VMEM_LIMIT),
        name="out_proj",
    )(x2, *mods, g.reshape(1, d), *branches, wg, wup, wout, final_g.reshape(1, d))


def _rwkv_token_math(pm, w0, w2, a0, a2, k_k, k_a, ones):
    r = pm[:, 0:W_MIX]
    k = pm[:, W_MIX:2 * W_MIX]
    v = pm[:, 2 * W_MIX:3 * W_MIX]
    wd = pm[:, 3 * W_MIX:3 * W_MIX + LORA]
    ad = pm[:, 3 * W_MIX + LORA:]
    w_log = -_softplus(-(w0 + _dotp(jnp.tanh(wd), w2, _NN, P_MISC))) - 0.5
    log_decay = -jnp.exp(w_log)
    a = jax.nn.sigmoid(a0 + _dotp(ad, a2, _NN, P_MISC))
    kx = k * k_k
    kk = kx * lax.rsqrt(_head_sum(kx * kx, ones) + EPS)
    k = k * (1.0 + (a - 1.0) * k_a)
    return r, k, v, log_decay, -kk, kk * a


def _rwkv_finish(o, r, k, v, z, r_k, ln_g, ln_b, ones):
    mean = _head_sum(o, ones, signed=True) * (1.0 / HEAD_DIM)
    dlt = o - mean
    var = _head_sum(dlt * dlt, ones) * (1.0 / HEAD_DIM)
    on = dlt * lax.rsqrt(var + RWKV_GN_EPS) * ln_g + ln_b
    bonus = _head_sum(r * k * r_k, ones, signed=True) * v
    return (on + bonus) * _silu(z)


def _swap_halves(x):
    half = HEAD_DIM // 2
    n = x.shape[-1]
    first = (_iota(x.shape, 1) & half) == 0
    return jnp.where(first, pltpu.roll(x, n - half, axis=1), pltpu.roll(x, half, axis=1))


def _rotary(x, cos, sin):
    return x * cos + _swap_halves(x) * sin


def _lru_token_math(xc, gate_w, gate_b, lam):
    gates = _dotp(xc, gate_w, _NN, P_MISC) + gate_b
    r_gate = jax.nn.sigmoid(gates[:, :W_MIX])
    i_gate = jax.nn.sigmoid(gates[:, W_MIX:])
    log_a = -LRU_C * r_gate * _softplus(-lam)
    a = jnp.exp(log_a)
    b = jnp.sqrt(1.0 - jnp.exp(2.0 * log_a)) * (i_gate * xc)
    return a, b


def _gdn_token_math(qkv, b_raw, a_raw, a_log, dt_bias, ones):
    qkv = _silu(qkv)
    q = qkv[:, 0:W_MIX]
    k = qkv[:, W_MIX:2 * W_MIX]
    v = qkv[:, 2 * W_MIX:]
    q = q * lax.rsqrt(_head_sum(q * q, ones) + EPS) * (HEAD_DIM ** -0.5)
    k = k * lax.rsqrt(_head_sum(k * k, ones) + EPS)
    beta = jax.nn.sigmoid(b_raw)
    g = -jnp.exp(a_log) * _softplus(a_raw + dt_bias)
    return q, k, v, beta, g


def _head_rms_finish(o, z, ones, gain=None):
    y = o * lax.rsqrt(_head_sum(o * o, ones) * (1.0 / HEAD_DIM) + EPS)
    if gain is not None:
        y = y * gain
    return y * _silu(z)


def _conv_tile(u, ext_ref, w_ref, first):
    n = u.shape[0]

    @pl.when(first)
    def _():
        ext_ref[0:SUBLANES, :] = jnp.zeros((SUBLANES, u.shape[1]), F32)

    ext_ref[SUBLANES:SUBLANES + n, :] = u
    out = None
    for j in range(CONV_W):
        term = _rows_back(u, ext_ref, CONV_W - 1 - j) * w_ref[j:j + 1, :]
        out = term if out is None else out + term
    ext_ref[0:SUBLANES, :] = u[n - SUBLANES:n, :]
    return out


def _rows_back(u, ext_ref, back):
    if back == 0:
        return u
    n, ch = u.shape
    tiles = (n // SUBLANES, SUBLANES, ch)
    pos = _iota((1, SUBLANES, 1), 1)
    earlier = ext_ref[0:n, :].reshape(tiles)
    return pltpu.roll(jnp.where(pos >= SUBLANES - back, earlier, u.reshape(tiles)), back, axis=1).reshape(n, ch)


def _rwkv_kernel(p_ref, mu_ref, w0_ref, w2_ref, a0_ref, a2_ref, kk_ref, ka_ref, rk_ref, lng_ref, lnb_ref,
                 o_ref, s_out_ref, shift_out_ref, s_scr, ext_scr):
    j = pl.program_id(1)
    last = pl.num_programs(1) - 1
    ct = p_ref.shape[0]

    @pl.when(j == 0)
    def _():
        s_scr[...] = jnp.zeros(s_scr.shape, F32)
        ext_scr[0:SUBLANES, :] = jnp.zeros((SUBLANES, A_SHIFT_W), F32)

    p = p_ref[...]
    pa = p[:, :A_SHIFT_W]
    z = p[:, A_SHIFT_W:]
    ext_scr[SUBLANES:SUBLANES + ct, :] = pa
    prev = _rows_back(pa, ext_scr, 1)
    ext_scr[0:SUBLANES, :] = pa[ct - SUBLANES:ct, :]
    pm = pa + (prev - pa) * mu_ref[...]
    ones = _head_ones()
    r, k, v, ld, av, bv = _rwkv_token_math(pm, w0_ref[...], w2_ref[...], a0_ref[...], a2_ref[...],
                                           kk_ref[...], ka_ref[...], ones)

    c = min(CHUNK, ct)
    ha = _HeadAlgebra(c)
    lt = (_iota((c, c), 0) >= _iota((c, c), 1)).astype(BF16)
    units = []
    for c0 in range(0, ct, c):
        sl = slice(c0, c0 + c)
        ldc = ld[sl]
        cum = _dot_const(ldc, lt, _NN, CUMSUM_PIECES, const_left=True)
        e_neg = jnp.exp(-cum)
        e_out = jnp.exp(cum[c - 1:c, :] - cum)
        units.append(dict(a=av[sl] * jnp.exp(cum - ldc), r=r[sl] * jnp.exp(cum), b=bv[sl] * e_neg, k=k[sl] * e_neg,
                          bo=bv[sl] * e_out, ko=k[sl] * e_out, v=v[sl], g=jnp.exp(cum[c - 1:c, :])))
    all_units = units
    state = s_scr[...]
    o_rows = []
    for w0 in range(0, len(all_units), WAVE):
        units = all_units[w0:w0 + WAVE]
        for u in units:
            lhs = jnp.concatenate([u["a"], u["r"]], axis=0)
            u["mb"] = ha.nt(lhs, ha.bd(u["b"]))
            u["mk"] = ha.nt(lhs, ha.bd(u["k"]))
            u["bd_v"] = ha.bd(u["v"])
        for u in units:
            u["m_ab"] = jnp.where(ha.strict, u["mb"][:c], 0.0)
            u["m_ak"] = jnp.where(ha.strict, u["mk"][:c], 0.0)
            u["m_rb"] = jnp.where(ha.incl, u["mb"][c:], 0.0)
            u["m_rk"] = jnp.where(ha.incl, u["mk"][c:], 0.0)
        for u, t_inv in zip(units, ha.inv_unit_lower_many([-u["m_ab"] for u in units])):
            u["t_inv"] = t_inv
        for u in units:
            u["makv"] = ha.nn(u["m_ak"], u["bd_v"], BF16)
        for u in units:
            u["a_hat"] = ha.nn(u["t_inv"], ha.bd(u["a"]), BF16)
            u["u1"] = ha.nn(u["t_inv"], ha.bd(u["makv"]), BF16)
        for u in units:
            u["r_hat"] = u["r"] + ha.nn(u["m_rb"], ha.bd(u["a_hat"]))
            u["o1"] = ha.nn(u["m_rb"], ha.bd(u["u1"])) + ha.nn(u["m_rk"], u["bd_v"])
            u["g_t"] = ha.diag_bd(u["g"]) + ha.tn_bd(u["bo"], u["a_hat"])
            u["h_t"] = ha.tn_bd(jnp.concatenate([u["bo"], u["ko"]], axis=0),
                                jnp.concatenate([u["u1"], u["v"].astype(BF16)], axis=0))
            zz = lax.dot_general(jnp.concatenate([u["r_hat"], u["g_t"]], axis=0).astype(BF16), state.astype(BF16),
                                 _NN, preferred_element_type=F32)
            o_rows.append(zz[:c] + u["o1"])
            state = zz[c:] + u["h_t"]
    o = o_rows[0] if len(o_rows) == 1 else jnp.concatenate(o_rows, axis=0)
    s_scr[...] = state
    o_ref[...] = _rwkv_finish(o, r, k, v, z, rk_ref[...], lng_ref[...], lnb_ref[...], ones)

    @pl.when(j == last)
    def _():
        eye_h = (_iota((HEAD_DIM, HEAD_DIM), 0) == _iota((HEAD_DIM, HEAD_DIM), 1)).astype(F32)
        for h in range(N_HEADS):
            hs = slice(h * HEAD_DIM, (h + 1) * HEAD_DIM)
            s_out_ref[h] = _mm_nt(eye_h, state[hs, hs])
        shift_out_ref[...] = pa[ct - 1:ct, :]


def _row(a):
    return a.reshape(1, -1)


def _rwkv_call(p_a, n_seq, seq_len, lp):
    ct = min(ROW_TILE, seq_len)
    p3 = p_a.reshape(n_seq, seq_len, A_W)
    params = [_row(lp["rwkv_mu"]), _row(lp["rwkv_w0"]), lp["rwkv_w2"], _row(lp["rwkv_a0"]), lp["rwkv_a2"],
              _row(lp["rwkv_k_k"]), _row(lp["rwkv_k_a"]), _row(lp["rwkv_r_k"]), _row(lp["rwkv_ln_g"]),
              _row(lp["rwkv_ln_b"])]
    o, s1, shift1 = pl.pallas_call(
        _rwkv_kernel,
        grid=(n_seq, seq_len // ct),
        in_specs=[pl.BlockSpec((None, ct, A_W), lambda b, j: (b, j, 0))]
        + [pl.BlockSpec(a.shape, lambda b, j: (0, 0)) for a in params],
        out_specs=[
            pl.BlockSpec((None, ct, W_MIX), lambda b, j: (b, j, 0)),
            pl.BlockSpec((None, N_HEADS, HEAD_DIM, HEAD_DIM), lambda b, j: (b, 0, 0, 0)),
            pl.BlockSpec((None, 1, A_SHIFT_W), lambda b, j: (b, 0, 0)),
        ],
        out_shape=[
            jax.ShapeDtypeStruct((n_seq, seq_len, W_MIX), F32),
            jax.ShapeDtypeStruct((n_seq, N_HEADS, HEAD_DIM, HEAD_DIM), F32),
            jax.ShapeDtypeStruct((n_seq, 1, A_SHIFT_W), F32),
        ],
        scratch_shapes=[pltpu.VMEM((W_MIX, W_MIX), F32), pltpu.VMEM((ct + SUBLANES, A_SHIFT_W), F32)],
        compiler_params=pltpu.CompilerParams(dimension_semantics=("parallel", "arbitrary"),
                                             vmem_limit_bytes=VMEM_LIMIT),
        name="rwkv7_prompt",
    )(p3, *params)
    return o.reshape(n_seq * seq_len, W_MIX), s1, shift1.reshape(n_seq, A_SHIFT_W)


def _ret_kernel(p_ref, cos_ref, sin_ref, o_ref, s_out_ref, s_scr):
    j = pl.program_id(1)
    last = pl.num_programs(1) - 1
    ct = p_ref.shape[0]

    @pl.when(j == 0)
    def _():
        s_scr[...] = jnp.zeros(s_scr.shape, F32)

    p = p_ref[...]
    cos, sin = cos_ref[...], sin_ref[...]
    q = _rotary(p[:, 0:W_MIX], cos, sin)
    k = _rotary(p[:, W_MIX:2 * W_MIX], cos, sin) * (HEAD_DIM ** -0.5)
    v = p[:, 2 * W_MIX:3 * W_MIX]
    z = p[:, 3 * W_MIX:]
    c = min(RET_CHUNK, ct)
    ri, ci = _iota((c, c), 0), _iota((c, c), 1)
    causal = ri >= ci
    rel = jnp.where(causal, ri - ci, 0).astype(F32)
    idx = _iota((c, 1), 0).astype(F32)
    states = [s_scr[h] for h in range(N_HEADS)]
    units = []
    for h in range(N_HEADS):
        lg = math.log(1.0 - 2.0 ** (-5.0 - h))
        consts = dict(decay=jnp.where(causal, jnp.exp(lg * rel), 0.0), q_dec=jnp.exp(lg * (idx + 1.0)),
                      k_dec=jnp.exp(lg * (c - 1.0 - idx)), g_c=math.exp(lg * c))
        hs = slice(h * HEAD_DIM, (h + 1) * HEAD_DIM)
        for c0 in range(0, ct, c):
            sl = slice(c0, c0 + c)
            units.append(dict(consts, h=h, q=_Split(q[sl, hs], 1), k=k[sl, hs], v=_Split(v[sl, hs], 1)))
    for u in units:
        u["s_in"] = _dotp(u["q"], u["k"], _NT, P_MISC) * u["decay"]
        u["kv"] = _dotp(u["k"] * u["k_dec"], u["v"], _TN, P_MISC)
    for u in units:
        u["o"] = _dotp(u["s_in"], u["v"], _NN, P_MISC)
    for u in units:
        u["s0"] = states[u["h"]]
        states[u["h"]] = u["s0"] * u["g_c"] + u["kv"]
    o_heads_all = [[] for _ in range(N_HEADS)]
    for u in units:
        o_heads_all[u["h"]].append(u["o"] + _dotp(u["q"], u["s0"], _NN, P_MISC) * u["q_dec"])
    cols = [oh[0] if len(oh) == 1 else jnp.concatenate(oh, axis=0) for oh in o_heads_all]
    o = jnp.concatenate(cols, axis=1)
    for h in range(N_HEADS):
        s_scr[h] = states[h]
    o_ref[...] = _head_rms_finish(o, z, _head_ones())

    @pl.when(j == last)
    def _():
        for h in range(N_HEADS):
            s_out_ref[h] = states[h]


def _rope_tables(pos):
    half = HEAD_DIM // 2
    inv = ROPE_BASE ** (-jnp.arange(half, dtype=F32) / half)
    ang = pos.astype(F32)[:, None] * inv[None, :]
    cos, sin = jnp.cos(ang), jnp.sin(ang)
    cos_t = jnp.tile(jnp.concatenate([cos, cos], axis=-1), (1, N_HEADS))
    sin_t = jnp.tile(jnp.concatenate([-sin, sin], axis=-1), (1, N_HEADS))
    return cos_t, sin_t


def _ret_call(p_b, n_seq, seq_len, cos_t, sin_t):
    ct = min(ROW_TILE, seq_len)
    p3 = p_b.reshape(n_seq, seq_len, B_W)
    o, s1 = pl.pallas_call(
        _ret_kernel,
        grid=(n_seq, seq_len // ct),
        in_specs=[
            pl.BlockSpec((None, ct, B_W), lambda b, j: (b, j, 0)),
            pl.BlockSpec((ct, W_MIX), lambda b, j: (j, 0)),
            pl.BlockSpec((ct, W_MIX), lambda b, j: (j, 0)),
        ],
        out_specs=[
            pl.BlockSpec((None, ct, W_MIX), lambda b, j: (b, j, 0)),
            pl.BlockSpec((None, N_HEADS, HEAD_DIM, HEAD_DIM), lambda b, j: (b, 0, 0, 0)),
        ],
        out_shape=[
            jax.ShapeDtypeStruct((n_seq, seq_len, W_MIX), F32),
            jax.ShapeDtypeStruct((n_seq, N_HEADS, HEAD_DIM, HEAD_DIM), F32),
        ],
        scratch_shapes=[pltpu.VMEM((N_HEADS, HEAD_DIM, HEAD_DIM), F32)],
        compiler_params=pltpu.CompilerParams(dimension_semantics=("parallel", "arbitrary"),
                                             vmem_limit_bytes=VMEM_LIMIT),
        name="retention_prompt",
    )(p3, cos_t, sin_t)
    return o.reshape(n_seq * seq_len, W_MIX), s1


def _affine_scan(a, b, span):
    n, w = a.shape
    if span == SUBLANES and n > span:
        shape, axis = (n // span, span, w), 1
        a, b = a.reshape(shape), b.reshape(shape)
        pos = _iota((1, span, 1), 1)
    else:
        assert span >= n
        axis = 0
        pos = _iota((n, 1), 0)
    dist = 1
    while dist < span:
        keep = pos >= dist
        a_prev = jnp.where(keep, pltpu.roll(a, dist, axis=axis), 1.0)
        b_prev = jnp.where(keep, pltpu.roll(b, dist, axis=axis), 0.0)
        b = a * b_prev + b
        a = a * a_prev
        dist *= 2
    return a.reshape(n, w), b.reshape(n, w)


def _lru_kernel(p_ref, cw_ref, cb_ref, gw_ref, gb_ref, sp_ref, o_ref, h_out_ref, conv_out_ref,
                ext_scr, h_scr, ab_scr, hin_scr):
    j = pl.program_id(1)
    last = pl.num_programs(1) - 1
    ct = p_ref.shape[0]

    @pl.when(j == 0)
    def _():
        h_scr[...] = jnp.zeros(h_scr.shape, F32)

    p = p_ref[...]
    xr = p[:, :W_MIX]
    z = p[:, W_MIX:]
    xc = _conv_tile(xr, ext_scr, cw_ref, j == 0) + cb_ref[...]
    a, b = _lru_token_math(xc, gw_ref[...], gb_ref[...], sp_ref[...])
    n_grp = ct // SUBLANES
    a, b = _affine_scan(a, b, SUBLANES)
    n_tiles = W_MIX // LANES
    for t in range(n_tiles):
        ab_scr[t] = a[:, t * LANES:(t + 1) * LANES]
        ab_scr[n_tiles + t] = b[:, t * LANES:(t + 1) * LANES]
    ends = pl.ds(SUBLANES - 1, n_grp, stride=SUBLANES)
    a_end = jnp.concatenate([ab_scr[t, ends, :] for t in range(n_tiles)], axis=1)
    b_end = jnp.concatenate([ab_scr[n_tiles + t, ends, :] for t in range(n_tiles)], axis=1)
    a_end, b_end = _affine_scan(a_end, b_end, n_grp)
    h_prev = h_scr[...]
    h_end = a_end * h_prev + b_end
    grp = _iota((n_grp, 1), 0)
    hin_scr[...] = jnp.where(grp == 0, h_prev, pltpu.roll(h_end, 1, axis=0))
    h_in = jnp.concatenate([jnp.broadcast_to(hin_scr[g:g + 1, :], (SUBLANES, W_MIX)) for g in range(n_grp)], axis=0)
    hcur = a * h_in + b
    h_scr[...] = h_end[n_grp - 1:n_grp, :]
    o_ref[...] = hcur * _silu(z)

    @pl.when(j == last)
    def _():
        h_out_ref[...] = hcur[ct - 1:ct, :]
        conv_out_ref[...] = xr[ct - SUBLANES:ct, :]


def _block_diag_gates(gate_w):
    out = jnp.zeros((W_MIX, 2 * W_MIX), F32)
    for g in range(2):
        for n in range(N_HEADS):
            out = out.at[n * HEAD_DIM:(n + 1) * HEAD_DIM,
                         g * W_MIX + n * HEAD_DIM:g * W_MIX + (n + 1) * HEAD_DIM].set(gate_w[g, n])
    return out


def _lru_params(lp):
    return [lp["lru_conv_w"], _row(lp["lru_conv_b"]), _block_diag_gates(lp["lru_gate_w"]),
            _row(lp["lru_gate_b"]), _row(lp["lru_lambda"])]


def _lru_call(p_c, n_seq, seq_len, lp):
    ct = min(ROW_TILE, seq_len)
    p3 = p_c.reshape(n_seq, seq_len, C_W)
    params = _lru_params(lp)
    o, h1, conv_tail = pl.pallas_call(
        _lru_kernel,
        grid=(n_seq, seq_len // ct),
        in_specs=[pl.BlockSpec((None, ct, C_W), lambda b, j: (b, j, 0))]
        + [pl.BlockSpec(a.shape, lambda b, j: (0, 0)) for a in params],
        out_specs=[
            pl.BlockSpec((None, ct, W_MIX), lambda b, j: (b, j, 0)),
            pl.BlockSpec((None, 1, W_MIX), lambda b, j: (b, 0, 0)),
            pl.BlockSpec((None, SUBLANES, W_MIX), lambda b, j: (b, 0, 0)),
        ],
        out_shape=[
            jax.ShapeDtypeStruct((n_seq, seq_len, W_MIX), F32),
            jax.ShapeDtypeStruct((n_seq, 1, W_MIX), F32),
            jax.ShapeDtypeStruct((n_seq, SUBLANES, W_MIX), F32),
        ],
        scratch_shapes=[pltpu.VMEM((ct + SUBLANES, W_MIX), F32), pltpu.VMEM((1, W_MIX), F32),
                        pltpu.VMEM((2 * W_MIX // LANES, ct, LANES), F32),
                        pltpu.VMEM((ct // SUBLANES, W_MIX), F32)],
        compiler_params=pltpu.CompilerParams(dimension_semantics=("parallel", "arbitrary"),
                                             vmem_limit_bytes=VMEM_LIMIT),
        name="rglru_prompt",
    )(p3, *params)
    return (o.reshape(n_seq * seq_len, W_MIX), h1.reshape(n_seq, W_MIX),
            conv_tail[:, SUBLANES - (CONV_W - 1):, :])


def _gdn_kernel(p_ref, cw_ref, nal_ref, dtb_ref, ng_ref, o_ref, s_out_ref, conv_out_ref, ext_scr, s_scr):
    j = pl.program_id(1)
    last = pl.num_programs(1) - 1
    ct = p_ref.shape[0]

    @pl.when(j == 0)
    def _():
        s_scr[...] = jnp.zeros(s_scr.shape, F32)

    p = p_ref[...]
    raw = p[:, :D_QKV_W]
    z = p[:, D_QKV_W:D_QKV_W + W_MIX]
    b_raw = p[:, D_QKV_W + W_MIX:D_QKV_W + 2 * W_MIX]
    a_raw = p[:, D_QKV_W + 2 * W_MIX:]
    ones = _head_ones()
    qkv = _conv_tile(raw, ext_scr, cw_ref, j == 0)
    q, k, v, beta, g = _gdn_token_math(qkv, b_raw, a_raw, nal_ref[...], dtb_ref[...], ones)

    c = min(CHUNK, ct)
    ha = _HeadAlgebra(c)
    lt = (_iota((c, c), 0) >= _iota((c, c), 1)).astype(BF16)
    units = []
    for c0 in range(0, ct, c):
        sl = slice(c0, c0 + c)
        gc = _dot_const(g[sl], lt, _NN, CUMSUM_PIECES, const_left=True)
        gc_cols = jnp.sum(gc * ha.eye, axis=0, keepdims=True)
        diff = gc - gc_cols
        decay = jnp.where(ha.incl, jnp.exp(jnp.where(ha.incl, diff, 0.0)), 0.0)
        kb = k[sl] * beta[sl]
        e_gc = jnp.exp(gc)
        g_last = gc[c - 1:c, :]
        units.append(dict(decay=decay, kb=kb, q=q[sl], k=k[sl], vb=v[sl] * beta[sl], kbe=kb * e_gc,
                          k_out=k[sl] * jnp.exp(g_last - gc), q_in=q[sl] * e_gc, e_last=jnp.exp(g_last)))
    all_units = units
    state = s_scr[...]
    o_rows = []
    for w0 in range(0, len(all_units), WAVE):
        units = all_units[w0:w0 + WAVE]
        for u in units:
            kq = ha.nt(jnp.concatenate([u["kb"], u["q"]], axis=0), ha.bd(u["k"]))
            u["a_mat"] = jnp.where(ha.strict, kq[:c] * u["decay"], 0.0)
            u["qk"] = kq[c:] * u["decay"]
        for u, t_inv in zip(units, ha.inv_unit_lower_many([u["a_mat"] for u in units])):
            u["t_inv"] = t_inv
        for u in units:
            u["u"] = ha.nn(u["t_inv"], ha.bd(u["vb"]), BF16)
            u["w"] = ha.nn(u["t_inv"], ha.bd(u["kbe"]), BF16)
        for u in units:
            u["g_mat"] = ha.diag_bd(u["e_last"]) - ha.tn_bd(u["k_out"], u["w"])
            u["h_mat"] = ha.tn_bd(u["k_out"], u["u"])
            u["q_hat"] = u["q_in"] - ha.nn(u["qk"], ha.bd(u["w"]))
            u["o1"] = ha.nn(u["qk"], ha.bd(u["u"]))
            zz = lax.dot_general(jnp.concatenate([u["q_hat"], u["g_mat"]], axis=0).astype(BF16),
                                 state.astype(BF16), _NN, preferred_element_type=F32)
            o_rows.append(zz[:c] + u["o1"])
            state = zz[c:] + u["h_mat"]
    o = o_rows[0] if len(o_rows) == 1 else jnp.concatenate(o_rows, axis=0)
    s_scr[...] = state
    o_ref[...] = _head_rms_finish(o, z, ones, ng_ref[...])

    @pl.when(j == last)
    def _():
        for h in range(N_HEADS):
            hs = slice(h * HEAD_DIM, (h + 1) * HEAD_DIM)
            s_out_ref[h] = state[hs, hs]
        conv_out_ref[...] = raw[ct - SUBLANES:ct, :]


def _gdn_params(lp):
    return [lp["gdn_conv_w"], _row(jnp.repeat(lp["gdn_A_log"], HEAD_DIM)),
            _row(jnp.repeat(lp["gdn_dt_bias"], HEAD_DIM)), _row(jnp.tile(lp["gdn_norm_g"], N_HEADS))]


def _gdn_call(p_d, n_seq, seq_len, lp):
    ct = min(ROW_TILE, seq_len)
    p3 = p_d.reshape(n_seq, seq_len, D_PACK_W)
    params = _gdn_params(lp)
    o, s1, conv_tail = pl.pallas_call(
        _gdn_kernel,
        grid=(n_seq, seq_len // ct),
        in_specs=[pl.BlockSpec((None, ct, D_PACK_W), lambda b, j: (b, j, 0))]
        + [pl.BlockSpec(a.shape, lambda b, j: (0, 0)) for a in params],
        out_specs=[
            pl.BlockSpec((None, ct, W_MIX), lambda b, j: (b, j, 0)),
            pl.BlockSpec((None, N_HEADS, HEAD_DIM, HEAD_DIM), lambda b, j: (b, 0, 0, 0)),
            pl.BlockSpec((None, SUBLANES, D_QKV_W), lambda b, j: (b, 0, 0)),
        ],
        out_shape=[
            jax.ShapeDtypeStruct((n_seq, seq_len, W_MIX), F32),
            jax.ShapeDtypeStruct((n_seq, N_HEADS, HEAD_DIM, HEAD_DIM), F32),
            jax.ShapeDtypeStruct((n_seq, SUBLANES, D_QKV_W), F32),
        ],
        scratch_shapes=[pltpu.VMEM((ct + SUBLANES, D_QKV_W), F32), pltpu.VMEM((W_MIX, W_MIX), F32)],
        compiler_params=pltpu.CompilerParams(dimension_semantics=("parallel", "arbitrary"),
                                             vmem_limit_bytes=VMEM_LIMIT),
        name="gdn_prompt",
    )(p3, *params)
    return o.reshape(n_seq * seq_len, W_MIX), s1, conv_tail[:, SUBLANES - (CONV_W - 1):, :]


def _decode_pre_kernel(pa_ref, pb_ref, pc_ref, pd_ref, shift_ref, h0_ref, lconv_ref, gconv_ref, cos_ref, sin_ref,
                       mu_ref, w0_ref, w2_ref, a0_ref, a2_ref, kk_ref, ka_ref,
                       lcw_ref, lcb_ref, lgw_ref, lgb_ref, lsp_ref, gcw_ref, nal_ref, dtb_ref,
                       vt_ref, vn_ref, oc_ref, h1_ref, lconv1_ref, gconv1_ref):
    ones = _head_ones()
    pa_full = pa_ref[...]
    pa = pa_full[:, :A_SHIFT_W]
    pm = pa + (shift_ref[...] - pa) * mu_ref[...]
    r, k, v, ld, av, bv = _rwkv_token_math(pm, w0_ref[...], w2_ref[...], a0_ref[...], a2_ref[...],
                                           kk_ref[...], ka_ref[...], ones)
    vecs = [r, jnp.exp(ld), k, v, av, bv]
    plain = [r, k, v, pa_full[:, A_SHIFT_W:]]
    pb = pb_ref[...]
    cos, sin = cos_ref[...], sin_ref[...]
    vecs += [_rotary(pb[:, 0:W_MIX], cos, sin), _rotary(pb[:, W_MIX:2 * W_MIX], cos, sin) * (HEAD_DIM ** -0.5),
             pb[:, 2 * W_MIX:3 * W_MIX]]
    plain.append(pb[:, 3 * W_MIX:])
    pc = pc_ref[...]
    xr = pc[:, :W_MIX]
    taps = [lconv_ref[i] for i in range(CONV_W - 1)] + [xr]
    xc = taps[0] * lcw_ref[0:1, :]
    for i in range(1, CONV_W):
        xc = xc + taps[i] * lcw_ref[i:i + 1, :]
    xc = xc + lcb_ref[...]
    a, b = _lru_token_math(xc, lgw_ref[...], lgb_ref[...], lsp_ref[...])
    hcur = a * h0_ref[...] + b
    oc_ref[...] = hcur * _silu(pc[:, W_MIX:])
    h1_ref[...] = hcur
    for i in range(CONV_W - 1):
        lconv1_ref[i] = taps[i + 1]
    pd = pd_ref[...]
    raw = pd[:, :D_QKV_W]
    gtaps = [gconv_ref[i] for i in range(CONV_W - 1)] + [raw]
    qkv = gtaps[0] * gcw_ref[0:1, :]
    for i in range(1, CONV_W):
        qkv = qkv + gtaps[i] * gcw_ref[i:i + 1, :]
    q, kg, vg, beta, g = _gdn_token_math(qkv, pd[:, D_QKV_W + W_MIX:D_QKV_W + 2 * W_MIX],
                                         pd[:, D_QKV_W + 2 * W_MIX:], nal_ref[...], dtb_ref[...], ones)
    for i in range(CONV_W - 1):
        gconv1_ref[i] = gtaps[i + 1]
    vecs += [q, kg, vg, beta, g]
    plain.append(pd[:, D_QKV_W:D_QKV_W + W_MIX])
    assert len(vecs) == N_VEC_T and len(plain) == N_VEC_PLAIN
    for i, vec in enumerate(vecs):
        vt_ref[i] = vec.T
    for i, vec in enumerate(plain):
        vn_ref[i] = vec


def _decode_state_kernel(vt_ref, wkv_ref, ret_ref, gdn_ref, gam_ref, wkv1_ref, ret1_ref, gdn1_ref, o_ref):
    v_r, v_w, v_k, v_v, v_a, v_b, r_q, r_k, r_v, g_q, g_k, g_v, g_beta, g_g = range(N_VEC_T)
    hd = HEAD_DIM
    n = vt_ref.shape[-1]
    row = lambda idx, i: vt_ref[idx, pl.ds(i, 1), :]
    rows_of = lambda i: pl.ds(pl.multiple_of(i * hd, hd), hd)
    gamma = gam_ref[...]
    beta = vt_ref[g_beta, 0:1, :]
    eg = jnp.exp(vt_ref[g_g, 0:1, :])

    def first_pass(i, carry):
        acc_ret, acc_w, acc_q = carry
        rows = rows_of(i)
        s = wkv_ref[rows, :]
        sa = jnp.sum(s * vt_ref[v_a], axis=0, keepdims=True)
        s = s * vt_ref[v_w] + sa * vt_ref[v_b] + row(v_v, i) * vt_ref[v_k]
        wkv1_ref[rows, :] = s
        o_ref[0, pl.ds(i, 1), :] = jnp.sum(s * vt_ref[v_r], axis=0, keepdims=True)
        s = ret_ref[rows, :] * gamma + row(r_k, i) * vt_ref[r_v]
        ret1_ref[rows, :] = s
        acc_ret = acc_ret + row(r_q, i) * s
        s = gdn_ref[rows, :]
        return acc_ret, acc_w + row(g_k, i) * s, acc_q + row(g_q, i) * s

    zeros = jnp.zeros((hd, n), F32)
    acc_ret, acc_w, acc_q = lax.fori_loop(0, hd, first_pass, (zeros, zeros, zeros))
    o_ref[1] = acc_ret
    v_new = vt_ref[g_v] * beta - acc_w * (beta * eg)
    qk = jnp.sum(vt_ref[g_q] * vt_ref[g_k], axis=0, keepdims=True)
    o_ref[2] = acc_q * eg + qk * v_new

    def second_pass(i, carry):
        rows = rows_of(i)
        gdn1_ref[rows, :] = gdn_ref[rows, :] * eg + row(g_k, i) * v_new
        return carry

    lax.fori_loop(0, hd, second_pass, 0)


def _decode_finish_kernel(ot_ref, vn_ref, rk_ref, lng_ref, lnb_ref, ng_ref, oa_ref, ob_ref, od_ref):
    ones = _head_ones()
    r, k, v, z_a, z_b, z_d = (vn_ref[i] for i in range(N_VEC_PLAIN))
    oa_ref[...] = _rwkv_finish(ot_ref[0].T, r, k, v, z_a, rk_ref[...], lng_ref[...], lnb_ref[...], ones)
    ob_ref[...] = _head_rms_finish(ot_ref[1].T, z_b, ones)
    od_ref[...] = _head_rms_finish(ot_ref[2].T, z_d, ones, ng_ref[...])


def _batch_minor(state):
    n_layers, n = state.shape[:2]
    return jnp.transpose(state, (0, 2, 3, 4, 1)).reshape(n_layers, -1, n)


def _batch_major(flat_state):
    n = flat_state.shape[-1]
    return jnp.transpose(flat_state.reshape(N_HEADS, HEAD_DIM, HEAD_DIM, n), (3, 0, 1, 2))


def _decode_layer(l, p_a, p_b, p_c, p_d, carried, cos_t, sin_t, lp):
    n = p_a.shape[0]
    assert n % LANES == 0, "the decode state kernel keeps the batch on lanes"
    flat = HEAD_DIM * HEAD_DIM
    taps = CONV_W - 1
    rwkv_params = [_row(lp["rwkv_mu"]), _row(lp["rwkv_w0"]), lp["rwkv_w2"], _row(lp["rwkv_a0"]), lp["rwkv_a2"],
                   _row(lp["rwkv_k_k"]), _row(lp["rwkv_k_a"])]
    gdn_params = _gdn_params(lp)
    full = lambda a: pl.BlockSpec(a.shape, lambda i: (0,) * a.ndim)
    layer_blk = lambda a: pl.BlockSpec((None,) + a.shape[1:], lambda i, nd=a.ndim: (l,) + (0,) * (nd - 1))
    projs = [p_a, p_b, p_c, p_d]
    layered = [carried["shift"], carried["lru_h"], carried["lru_conv"], carried["gdn_conv"]]
    consts = [cos_t, sin_t, *rwkv_params, *_lru_params(lp), *gdn_params[:3]]
    out_shapes = [
        jax.ShapeDtypeStruct((N_VEC_T, W_MIX, n), F32),
        jax.ShapeDtypeStruct((N_VEC_PLAIN, n, W_MIX), F32),
        jax.ShapeDtypeStruct((n, W_MIX), F32),
        jax.ShapeDtypeStruct((n, W_MIX), F32),
        jax.ShapeDtypeStruct((taps, n, W_MIX), F32),
        jax.ShapeDtypeStruct((taps, n, D_QKV_W), F32),
    ]
    vec_t, vec_n, o_c, lru_h1, lru_conv1, gdn_conv1 = pl.pallas_call(
        _decode_pre_kernel,
        grid=(1,),
        in_specs=[full(a) for a in projs] + [layer_blk(a) for a in layered] + [full(a) for a in consts],
        out_specs=[pl.BlockSpec(s.shape, lambda i, nd=len(s.shape): (0,) * nd) for s in out_shapes],
        out_shape=out_shapes,
        compiler_params=pltpu.CompilerParams(dimension_semantics=("arbitrary",), vmem_limit_bytes=VMEM_LIMIT),
        name="decode_tokens",
    )(*projs, *layered, *consts)

    gam = jnp.broadcast_to((1.0 - 2.0 ** (-5.0 - jnp.arange(N_HEADS, dtype=F32)))[:, None, None], (N_HEADS, 1, n))
    state_in = pl.BlockSpec((None, flat, n), lambda h: (l, h, 0))
    state_out = pl.BlockSpec((flat, n), lambda h: (h, 0))
    wkv1, ret1, gdn1, o_t = pl.pallas_call(
        _decode_state_kernel,
        grid=(N_HEADS,),
        in_specs=[pl.BlockSpec((N_VEC_T, HEAD_DIM, n), lambda h: (0, h, 0)), state_in, state_in, state_in,
                  pl.BlockSpec((None, 1, n), lambda h: (h, 0, 0))],
        out_specs=[state_out, state_out, state_out, pl.BlockSpec((3, HEAD_DIM, n), lambda h: (0, h, 0))],
        out_shape=[jax.ShapeDtypeStruct((N_HEADS * flat, n), F32)] * 3 + [jax.ShapeDtypeStruct((3, W_MIX, n), F32)],
        compiler_params=pltpu.CompilerParams(dimension_semantics=("parallel",), vmem_limit_bytes=VMEM_LIMIT),
        name="decode_states",
    )(vec_t, carried["wkv"], carried["ret"], carried["gdn"], gam)

    finish_ins = [o_t, vec_n, _row(lp["rwkv_r_k"]), _row(lp["rwkv_ln_g"]), _row(lp["rwkv_ln_b"]), gdn_params[3]]
    o_a, o_b, o_d = pl.pallas_call(
        _decode_finish_kernel,
        grid=(1,),
        in_specs=[full(a) for a in finish_ins],
        out_specs=[pl.BlockSpec((n, W_MIX), lambda i: (0, 0))] * 3,
        out_shape=[jax.ShapeDtypeStruct((n, W_MIX), F32)] * 3,
        compiler_params=pltpu.CompilerParams(dimension_semantics=("arbitrary",), vmem_limit_bytes=VMEM_LIMIT),
        name="decode_finish",
    )(*finish_ins)
    new_states = (_batch_major(wkv1), p_a[:, :A_SHIFT_W], _batch_major(ret1), lru_h1,
                  jnp.transpose(lru_conv1, (1, 0, 2)), _batch_major(gdn1), jnp.transpose(gdn_conv1, (1, 0, 2)))
    return (o_a, o_b, o_c, o_d), new_states


def _pack_in_weights(w_in):
    off_d = A_W + B_W + C_W
    off_ba = off_d + D_QKV_W
    off_z = off_ba + 2 * N_HEADS
    off_g = off_d + D_W
    w_pack = jnp.concatenate([
        w_in[:, :off_ba],
        w_in[:, off_z:off_g],
        jnp.repeat(w_in[:, off_ba:off_ba + N_HEADS], HEAD_DIM, axis=1),
        jnp.repeat(w_in[:, off_ba + N_HEADS:off_z], HEAD_DIM, axis=1),
    ], axis=1)
    return w_pack, w_in[:, off_g:]


def _prompt_layer(p_a, p_b, p_c, p_d, n_seq, seq_len, cos_t, sin_t, lp):
    o_a, wkv1, shift1 = _rwkv_call(p_a, n_seq, seq_len, lp)
    o_b, ret1 = _ret_call(p_b, n_seq, seq_len, cos_t, sin_t)
    o_c, lru_h1, lru_conv1 = _lru_call(p_c, n_seq, seq_len, lp)
    o_d, gdn1, gdn_conv1 = _gdn_call(p_d, n_seq, seq_len, lp)
    return (o_a, o_b, o_c, o_d), (wkv1, shift1, ret1, lru_h1, lru_conv1, gdn1, gdn_conv1)


def _run_group(x, mods, pos, carried, layers, final_g):
    n_seq, seq_len, d = x.shape
    x2 = x.reshape(n_seq * seq_len, d)
    cos_t, sin_t = _rope_tables(pos)
    new = []
    n_layers = len(layers)
    for l, lp in enumerate(layers):
        shift, scale, gate = (mods[l][:, i * d:(i + 1) * d] for i in range(3))
        p_a, p_b, p_c, p_d = _inproj_call(x2, scale, shift, lp["norm_g"], lp["w_pack"], seq_len)
        if carried is None:
            branches, st = _prompt_layer(p_a, p_b, p_c, p_d, n_seq, seq_len, cos_t, sin_t, lp)
        else:
            branches, st = _decode_layer(l, p_a, p_b, p_c, p_d, carried, cos_t, sin_t, lp)
        new.append(st)
        x2 = _outproj_call(x2, scale, shift, gate, lp["norm_g"], branches, lp["w_gate"], lp["w_up_bf16"],
                           lp["w_out_bf16"], final_g, seq_len, final=(l == n_layers - 1))
    stacked = tuple(jnp.stack([s[i] for s in new], axis=0) for i in range(7))
    return x2.reshape(n_seq, seq_len, d), stacked


def kernel(x_prompt, x_sample, c_prompt, c_sample, state_rwkv_wkv, state_rwkv_shift, state_ret, state_lru_h, state_lru_conv, state_gdn, state_gdn_conv, ada_w, ada_b, norm_g, w_in, rwkv_mu, rwkv_w0, rwkv_w2, rwkv_a0, rwkv_a2, rwkv_k_k, rwkv_k_a, rwkv_r_k, rwkv_ln_g, rwkv_ln_b, lru_conv_w, lru_conv_b, lru_gate_w, lru_gate_b, lru_lambda, gdn_conv_w, gdn_A_log, gdn_dt_bias, gdn_norm_g, w_up, w_out, final_g):
    n_layers = ada_w.shape[0]
    n_prompt, seq_len, _ = x_prompt.shape
    n_sample, dec_len, _ = x_sample.shape
    assert dec_len == 1, "the decode path handles one token per sequence"
    layers = []
    w_in_bf16 = w_in.astype(BF16)
    for l in range(n_layers):
        w_pack, w_gate = _pack_in_weights(w_in_bf16[l])
        layers.append(dict(
            norm_g=norm_g[l], w_pack=w_pack, w_gate=w_gate, w_up_bf16=w_up[l].astype(BF16),
            w_out_bf16=w_out[l].astype(BF16),
            rwkv_mu=rwkv_mu[l], rwkv_w0=rwkv_w0[l], rwkv_w2=rwkv_w2[l], rwkv_a0=rwkv_a0[l], rwkv_a2=rwkv_a2[l],
            rwkv_k_k=rwkv_k_k[l], rwkv_k_a=rwkv_k_a[l], rwkv_r_k=rwkv_r_k[l], rwkv_ln_g=rwkv_ln_g[l],
            rwkv_ln_b=rwkv_ln_b[l], lru_conv_w=lru_conv_w[l], lru_conv_b=lru_conv_b[l],
            lru_gate_w=lru_gate_w[l], lru_gate_b=lru_gate_b[l], lru_lambda=lru_lambda[l],
            gdn_conv_w=gdn_conv_w[l], gdn_A_log=gdn_A_log[l], gdn_dt_bias=gdn_dt_bias[l],
            gdn_norm_g=gdn_norm_g[l]))
    mods = _ada_call(jnp.concatenate([c_prompt, c_sample], axis=0), ada_w, ada_b)
    mods_p = [mods[l, :n_prompt] for l in range(n_layers)]
    mods_s = [mods[l, n_prompt:] for l in range(n_layers)]

    y_prompt, new_p = _run_group(x_prompt, mods_p, jnp.arange(seq_len, dtype=jnp.int32), None, layers, final_g)
    carried = dict(wkv=_batch_minor(state_rwkv_wkv), ret=_batch_minor(state_ret), gdn=_batch_minor(state_gdn),
                   shift=state_rwkv_shift, lru_h=state_lru_h,
                   lru_conv=jnp.transpose(state_lru_conv, (0, 2, 1, 3)),
                   gdn_conv=jnp.transpose(state_gdn_conv, (0, 2, 1, 3)))
    pos_s = PAST_LEN + jnp.arange(dec_len, dtype=jnp.int32)
    y_sample, new_s = _run_group(x_sample, mods_s, pos_s, carried, layers, final_g)
    return (y_prompt, y_sample) + new_p + new_s
```

```python
import functools
import math

import jax
import jax.numpy as jnp
from jax import lax
from jax.experimental import pallas as pl
from jax.experimental.pallas import tpu as pltpu

F32 = jnp.float32
BF16 = jnp.bfloat16
HI = lax.Precision.HIGHEST

N_HEADS = 4
HEAD_DIM = 64
W_MIX = N_HEADS * HEAD_DIM
LORA = 64
CONV_W = 4
N_BRANCH = 4
LRU_C = 8.0
ROPE_BASE = 10000.0
EPS = 1e-6
RWKV_GN_EPS = 64e-5
PAST_LEN = 16384
A_SHIFT_W = 3 * W_MIX + 2 * LORA
A_W = A_SHIFT_W + W_MIX
B_W = 4 * W_MIX
C_W = 2 * W_MIX
D_QKV_W = 3 * W_MIX
D_W = D_QKV_W + 2 * N_HEADS + W_MIX
D_PACK_W = D_QKV_W + 3 * W_MIX

SUBLANES = 8
LANES = 128
VMEM_LIMIT = 56 * 1024 * 1024

CHUNK = 64
RET_CHUNK = 128
INV_BLOCK = 16
WAVE = 16
ROW_TILE = 1024
LRU_TILE = 512
PROJ_TILE = 512
OUT_TILE = 1024
N_VEC_T = 14
N_VEC_PLAIN = 6


def _mm(a, b, prec=HI):
    return lax.dot_general(a, b, (((1,), (0,)), ((), ())), precision=prec, preferred_element_type=F32)


def _mm_nt(a, b, prec=HI):
    return lax.dot_general(a, b, (((1,), (1,)), ((), ())), precision=prec, preferred_element_type=F32)


def _mm_tn(a, b, prec=HI):
    return lax.dot_general(a, b, (((0,), (0,)), ((), ())), precision=prec, preferred_element_type=F32)


_NN = (((1,), (0,)), ((), ()))
_NT = (((1,), (1,)), ((), ()))
_TN = (((0,), (0,)), ((), ()))

P_INV = 1
P_STATE = 1
P_MISC = 1
HEAD_SUM_PIECES = 1
CUMSUM_PIECES = 2


class _Split:
    def __init__(self, x, passes):
        self.hi = x.astype(BF16)
        self.lo = (x - self.hi.astype(F32)).astype(BF16) if passes > 1 else None


def _dotp(a, b, dims=_NN, passes=1):
    a = a if isinstance(a, _Split) else _Split(a, passes)
    b = b if isinstance(b, _Split) else _Split(b, passes)
    d = lambda x, y: lax.dot_general(x, y, dims, preferred_element_type=F32)
    out = d(a.hi, b.hi)
    if passes > 1:
        out = out + (d(a.hi, b.lo) + d(a.lo, b.hi))
    return out


def _iota(shape, dim):
    return lax.broadcasted_iota(jnp.int32, shape, dim)


def _silu(x):
    return x * jax.nn.sigmoid(x)


def _softplus(x):
    return jnp.maximum(x, 0.0) + jnp.log1p(jnp.exp(-jnp.abs(x)))


def _pieces(x, n):
    out = []
    for i in range(n):
        p = x.astype(BF16)
        out.append(p)
        if i + 1 < n:
            x = x - p.astype(F32)
    return out


def _dot_const(x, const, dims=_NN, n=2, const_left=False):
    out = None
    for p in _pieces(x, n):
        t = lax.dot_general(*((const, p) if const_left else (p, const)), dims, preferred_element_type=F32)
        out = t if out is None else out + t
    return out


def _head_ones():
    return (_iota((W_MIX, W_MIX), 0) // HEAD_DIM == _iota((W_MIX, W_MIX), 1) // HEAD_DIM).astype(BF16)


def _head_sum(x, ones, signed=False):
    return _dot_const(x, ones, n=HEAD_SUM_PIECES + (1 if signed else 0))


def _rms(x):
    return x * lax.rsqrt(jnp.mean(x * x, axis=-1, keepdims=True) + EPS)


def _inv_unit_lower(a):
    return _inv_unit_lower_many([a])[0]


def _inv_unit_lower_many(mats):
    n = mats[0].shape[0]
    ri, ci = _iota((n, n), 0), _iota((n, n), 1)
    eye = (ri == ci).astype(F32)
    diag_blk = (ri // INV_BLOCK) == (ci // INV_BLOCK)
    mm = lambda x, y: _dotp(x, y, _NN, P_INV)
    sp = lambda x: _Split(x, P_INV)
    d = [jnp.where(diag_blk, a, 0.0) for a in mats]
    nb = [a - di for a, di in zip(mats, d)]
    td = [eye - di for di in d]
    p = d
    for _ in range(int(math.log2(INV_BLOCK)) - 1):
        ps = [sp(pi) for pi in p]
        p = [mm(pi, pi) for pi in ps]
        td = [mm(ti, eye + pi) for ti, pi in zip(td, p)]
    tds = [sp(ti) for ti in td]
    x = [mm(ti, ni) for ti, ni in zip(tds, nb)]
    t = [eye - xi for xi in x]
    p = x
    for _ in range(int(math.log2(n // INV_BLOCK)) - 1):
        ps = [sp(pi) for pi in p]
        p = [mm(pi, pi) for pi in ps]
        t = [mm(ti, eye + pi) for ti, pi in zip(t, p)]
    return [mm(ti, tdi) for ti, tdi in zip(t, tds)]


class _HeadAlgebra:
    def __init__(self, c):
        assert c == HEAD_DIM, "side-by-side head products need CHUNK == HEAD_DIM"
        w = W_MIX
        row, lane = _iota((c, w), 0), _iota((c, w), 1)
        col = lane % HEAD_DIM
        tile_lane = _iota((c, LANES), 1)
        self.tile_head = [tile_lane // HEAD_DIM == h for h in range(LANES // HEAD_DIM)]
        self.eye = (row == col).astype(F32)
        self.strict = row > col
        self.incl = row >= col
        self.inv_blk = (row // INV_BLOCK) == (col // INV_BLOCK)
        r2, c2 = _iota((w, w), 0), _iota((w, w), 1)
        self.eye_full = r2 == c2
        self.same_head = (r2 // HEAD_DIM) == (c2 // HEAD_DIM)

    def bd(self, y):
        yb = y.astype(BF16)
        zero = jnp.zeros((yb.shape[0], LANES), BF16)
        blocks = []
        for t in range(W_MIX // LANES):
            tile = yb[:, t * LANES:(t + 1) * LANES]
            for m in self.tile_head:
                kept = jnp.where(m, tile, zero)
                blocks.append(jnp.concatenate([kept if s == t else zero for s in range(W_MIX // LANES)], axis=1))
        return jnp.concatenate(blocks, axis=0)

    def nn(self, x, bd_y, out=F32):
        return lax.dot_general(x.astype(BF16), bd_y, _NN, preferred_element_type=F32).astype(out)

    def nt(self, x, bd_y):
        return lax.dot_general(x.astype(BF16), bd_y, _NT, preferred_element_type=F32)

    def tn_bd(self, x, y):
        full = lax.dot_general(x.astype(BF16), y.astype(BF16), _TN, preferred_element_type=F32)
        return jnp.where(self.same_head, full, 0.0)

    def diag_bd(self, row_vec):
        return jnp.where(self.eye_full, row_vec, 0.0)

    def plus_eye(self, bd_p):
        return jnp.where(self.eye_full, jnp.ones_like(bd_p), bd_p)

    def inv_unit_lower_many(self, mats):
        mats = [a.astype(BF16) for a in mats]
        zero = jnp.zeros_like(mats[0])
        eye = self.eye.astype(BF16)
        d = [jnp.where(self.inv_blk, a, zero) for a in mats]
        nb = [jnp.where(self.inv_blk, zero, a) for a in mats]
        td = [eye - di for di in d]
        p = d
        bdp = [self.bd(pi) for pi in p]
        for _ in range(int(math.log2(INV_BLOCK)) - 1):
            p = [self.nn(pi, bi, BF16) for pi, bi in zip(p, bdp)]
            bdp = [self.bd(pi) for pi in p]
            td = [self.nn(ti, self.plus_eye(bi), BF16) for ti, bi in zip(td, bdp)]
        bd_td = [self.bd(ti) for ti in td]
        x = [self.nn(ti, self.bd(ni), BF16) for ti, ni in zip(td, nb)]
        t = [eye - xi for xi in x]
        p = x
        bdp = [self.bd(pi) for pi in p]
        for _ in range(int(math.log2(HEAD_DIM // INV_BLOCK)) - 1):
            p = [self.nn(pi, bi, BF16) for pi, bi in zip(p, bdp)]
            bdp = [self.bd(pi) for pi in p]
            t = [self.nn(ti, self.plus_eye(bi), BF16) for ti, bi in zip(t, bdp)]
        return [self.nn(ti, bi, BF16) for ti, bi in zip(t, bd_td)]


def _ada_kernel(c_ref, w_ref, b_ref, o_ref):
    o_ref[...] = _dotp(_silu(c_ref[...]), w_ref[...], _NN, 3) + b_ref[...]


def _ada_call(c_all, ada_w, ada_b):
    n_layers, d, d3 = ada_w.shape
    rows = c_all.shape[0]
    return pl.pallas_call(
        _ada_kernel,
        grid=(n_layers, d3 // d),
        in_specs=[
            pl.BlockSpec((rows, d), lambda l, j: (0, 0)),
            pl.BlockSpec((None, d, d), lambda l, j: (l, 0, j)),
            pl.BlockSpec((None, 1, d), lambda l, j: (l, 0, j)),
        ],
        out_specs=pl.BlockSpec((None, rows, d), lambda l, j: (l, 0, j)),
        out_shape=jax.ShapeDtypeStruct((n_layers, rows, d3), F32),
        compiler_params=pltpu.CompilerParams(dimension_semantics=("arbitrary", "arbitrary"),
                                             vmem_limit_bytes=VMEM_LIMIT),
        name="ada_mod",
    )(c_all, ada_w, ada_b.reshape(n_layers, 1, d3))


def _modulated_norm(x, g, scale, shift):
    return _rms(x) * g * (1.0 + scale) + shift


def _inproj_kernel(x_ref, sc_ref, sh_ref, g_ref, w_ref, oa_ref, ob_ref, oc_ref, od_ref):
    h = _modulated_norm(x_ref[...], g_ref[...], sc_ref[...], sh_ref[...]).astype(BF16)
    lo = 0
    for o_ref in (oa_ref, ob_ref, oc_ref, od_ref):
        wd = o_ref.shape[-1]
        o_ref[...] = jnp.dot(h, w_ref[:, lo:lo + wd], preferred_element_type=F32)
        lo += wd


def _mod_specs(mods, tm, seq_len):
    d = mods[0].shape[-1]
    if seq_len == 1:
        return [m for m in mods], [pl.BlockSpec((tm, d), lambda i: (i, 0)) for _ in mods]
    per_seq = seq_len // tm
    return ([m.reshape(m.shape[0], 1, d) for m in mods],
            [pl.BlockSpec((None, 1, d), lambda i: (i // per_seq, 0, 0)) for _ in mods])


def _inproj_call(x2, scale, shift, g, w_pack, seq_len):
    m, d = x2.shape
    tm = min(PROJ_TILE, m, seq_len) if seq_len > 1 else m
    widths = (A_W, B_W, C_W, D_PACK_W)
    mods, mod_specs = _mod_specs((scale, shift), tm, seq_len)
    return pl.pallas_call(
        _inproj_kernel,
        grid=(m // tm,),
        in_specs=[pl.BlockSpec((tm, d), lambda i: (i, 0))] + mod_specs + [
            pl.BlockSpec((1, d), lambda i: (0, 0)),
            pl.BlockSpec(w_pack.shape, lambda i: (0, 0)),
        ],
        out_specs=[pl.BlockSpec((tm, wd), lambda i: (i, 0)) for wd in widths],
        out_shape=[jax.ShapeDtypeStruct((m, wd), F32) for wd in widths],
        compiler_params=pltpu.CompilerParams(dimension_semantics=("parallel",), vmem_limit_bytes=VMEM_LIMIT),
        name="in_proj",
    )(x2, *mods, g.reshape(1, d), w_pack)


def _outproj_kernel(x_ref, sc_ref, sh_ref, gt_ref, g_ref, ba_ref, bb_ref, bc_ref, bd_ref,
                    wg_ref, wup_ref, wout_ref, fg_ref, o_ref, *, final):
    x = x_ref[...]
    d = x.shape[-1]
    h = _modulated_norm(x, g_ref[...], sc_ref[...], sh_ref[...]).astype(BF16)
    merged = jnp.zeros(x.shape, F32)
    for n, br_ref in enumerate((ba_ref, bb_ref, bc_ref, bd_ref)):
        gl = jnp.dot(h, wg_ref[:, n * d:(n + 1) * d], preferred_element_type=F32)
        up = jnp.dot(br_ref[...].astype(BF16), wup_ref[n], preferred_element_type=F32)
        merged = merged + jax.nn.sigmoid(gl) * up
    out = jnp.dot(merged.astype(BF16), wout_ref[...], preferred_element_type=F32)
    xn = x + gt_ref[...] * out
    if final:
        xn = _rms(xn) * fg_ref[...]
    o_ref[...] = xn


def _outproj_call(x2, scale, shift, gate, g, branches, wg, wup, wout, final_g, seq_len, final):
    m, d = x2.shape
    tm = min(OUT_TILE, m, seq_len) if seq_len > 1 else m
    mods, mod_specs = _mod_specs((scale, shift, gate), tm, seq_len)
    full = lambda a: pl.BlockSpec(a.shape, lambda i: (0,) * a.ndim)
    return pl.pallas_call(
        functools.partial(_outproj_kernel, final=final),
        grid=(m // tm,),
        in_specs=[pl.BlockSpec((tm, d), lambda i: (i, 0))] + mod_specs + [pl.BlockSpec((1, d), lambda i: (0, 0))]
        + [pl.BlockSpec((tm, W_MIX), lambda i: (i, 0)) for _ in branches]
        + [full(wg), full(wup), full(wout), pl.BlockSpec((1, d), lambda i: (0, 0))],
        out_specs=pl.BlockSpec((tm, d), lambda i: (i, 0)),
        out_shape=jax.ShapeDtypeStruct((m, d), F32),
        compiler_params=pltpu.CompilerParams(dimension_semantics=("parallel",), vmem_limit_bytes=VMEM_LIMIT),
        name="out_proj",
    )(x2, *mods, g.reshape(1, d), *branches, wg, wup, wout, final_g.reshape(1, d))


def _rwkv_token_math(pm, w0, w2, a0, a2, k_k, k_a, ones):
    r = pm[:, 0:W_MIX]
    k = pm[:, W_MIX:2 * W_MIX]
    v = pm[:, 2 * W_MIX:3 * W_MIX]
    wd = pm[:, 3 * W_MIX:3 * W_MIX + LORA]
    ad = pm[:, 3 * W_MIX + LORA:]
    w_log = -_softplus(-(w0 + _dotp(jnp.tanh(wd), w2, _NN, P_MISC))) - 0.5
    log_decay = -jnp.exp(w_log)
    a = jax.nn.sigmoid(a0 + _dotp(ad, a2, _NN, P_MISC))
    kx = k * k_k
    kk = kx * lax.rsqrt(_head_sum(kx * kx, ones) + EPS)
    k = k * (1.0 + (a - 1.0) * k_a)
    return r, k, v, log_decay, -kk, kk * a


def _rwkv_finish(o, r, k, v, z, r_k, ln_g, ln_b, ones):
    mean = _head_sum(o, ones, signed=True) * (1.0 / HEAD_DIM)
    dlt = o - mean
    var = _head_sum(dlt * dlt, ones) * (1.0 / HEAD_DIM)
    on = dlt * lax.rsqrt(var + RWKV_GN_EPS) * ln_g + ln_b
    bonus = _head_sum(r * k * r_k, ones, signed=True) * v
    return (on + bonus) * _silu(z)


def _swap_halves(x):
    half = HEAD_DIM // 2
    n = x.shape[-1]
    first = (_iota(x.shape, 1) & half) == 0
    return jnp.where(first, pltpu.roll(x, n - half, axis=1), pltpu.roll(x, half, axis=1))


def _rotary(x, cos, sin):
    return x * cos + _swap_halves(x) * sin


def _lru_token_math(xc, gate_w, gate_b, lam):
    gates = _dotp(xc, gate_w, _NN, P_MISC) + gate_b
    r_gate = jax.nn.sigmoid(gates[:, :W_MIX])
    i_gate = jax.nn.sigmoid(gates[:, W_MIX:])
    log_a = -LRU_C * r_gate * _softplus(-lam)
    a = jnp.exp(log_a)
    b = jnp.sqrt(1.0 - jnp.exp(2.0 * log_a)) * (i_gate * xc)
    return a, b


def _gdn_token_math(qkv, b_raw, a_raw, a_log, dt_bias, ones):
    qkv = _silu(qkv)
    q = qkv[:, 0:W_MIX]
    k = qkv[:, W_MIX:2 * W_MIX]
    v = qkv[:, 2 * W_MIX:]
    q = q * lax.rsqrt(_head_sum(q * q, ones) + EPS) * (HEAD_DIM ** -0.5)
    k = k * lax.rsqrt(_head_sum(k * k, ones) + EPS)
    beta = jax.nn.sigmoid(b_raw)
    g = -jnp.exp(a_log) * _softplus(a_raw + dt_bias)
    return q, k, v, beta, g


def _head_rms_finish(o, z, ones, gain=None):
    y = o * lax.rsqrt(_head_sum(o * o, ones) * (1.0 / HEAD_DIM) + EPS)
    if gain is not None:
        y = y * gain
    return y * _silu(z)


def _conv_tile(u, ext_ref, w_ref, first):
    n = u.shape[0]

    @pl.when(first)
    def _():
        ext_ref[0:SUBLANES, :] = jnp.zeros((SUBLANES, u.shape[1]), F32)

    ext_ref[SUBLANES:SUBLANES + n, :] = u
    out = None
    for j in range(CONV_W):
        term = _rows_back(u, ext_ref, CONV_W - 1 - j) * w_ref[j:j + 1, :]
        out = term if out is None else out + term
    ext_ref[0:SUBLANES, :] = u[n - SUBLANES:n, :]
    return out


def _rows_back(u, ext_ref, back):
    if back == 0:
        return u
    n, ch = u.shape
    tiles = (n // SUBLANES, SUBLANES, ch)
    pos = _iota((1, SUBLANES, 1), 1)
    earlier = ext_ref[0:n, :].reshape(tiles)
    return pltpu.roll(jnp.where(pos >= SUBLANES - back, earlier, u.reshape(tiles)), back, axis=1).reshape(n, ch)


def _rwkv_kernel(p_ref, mu_ref, w0_ref, w2_ref, a0_ref, a2_ref, kk_ref, ka_ref, rk_ref, lng_ref, lnb_ref,
                 o_ref, s_out_ref, shift_out_ref, s_scr, ext_scr):
    j = pl.program_id(1)
    last = pl.num_programs(1) - 1
    ct = p_ref.shape[0]

    @pl.when(j == 0)
    def _():
        s_scr[...] = jnp.zeros(s_scr.shape, F32)
        ext_scr[0:SUBLANES, :] = jnp.zeros((SUBLANES, A_SHIFT_W), F32)

    p = p_ref[...]
    pa = p[:, :A_SHIFT_W]
    z = p[:, A_SHIFT_W:]
    ext_scr[SUBLANES:SUBLANES + ct, :] = pa
    prev = _rows_back(pa, ext_scr, 1)
    ext_scr[0:SUBLANES, :] = pa[ct - SUBLANES:ct, :]
    pm = pa + (prev - pa) * mu_ref[...]
    ones = _head_ones()
    r, k, v, ld, av, bv = _rwkv_token_math(pm, w0_ref[...], w2_ref[...], a0_ref[...], a2_ref[...],
                                           kk_ref[...], ka_ref[...], ones)

    c = min(CHUNK, ct)
    ha = _HeadAlgebra(c)
    lt = (_iota((c, c), 0) >= _iota((c, c), 1)).astype(BF16)
    units = []
    for c0 in range(0, ct, c):
        sl = slice(c0, c0 + c)
        ldc = ld[sl]
        cum = _dot_const(ldc, lt, _NN, CUMSUM_PIECES, const_left=True)
        e_neg = jnp.exp(-cum)
        e_out = jnp.exp(cum[c - 1:c, :] - cum)
        units.append(dict(a=av[sl] * jnp.exp(cum - ldc), r=r[sl] * jnp.exp(cum), b=bv[sl] * e_neg, k=k[sl] * e_neg,
                          bo=bv[sl] * e_out, ko=k[sl] * e_out, v=v[sl], g=jnp.exp(cum[c - 1:c, :])))
    all_units = units
    state = s_scr[...]
    o_rows = []
    for w0 in range(0, len(all_units), WAVE):
        units = all_units[w0:w0 + WAVE]
        for u in units:
            lhs = jnp.concatenate([u["a"], u["r"]], axis=0)
            u["mb"] = ha.nt(lhs, ha.bd(u["b"]))
            u["mk"] = ha.nt(lhs, ha.bd(u["k"]))
            u["bd_v"] = ha.bd(u["v"])
        for u in units:
            u["m_ab"] = jnp.where(ha.strict, u["mb"][:c], 0.0)
            u["m_ak"] = jnp.where(ha.strict, u["mk"][:c], 0.0)
            u["m_rb"] = jnp.where(ha.incl, u["mb"][c:], 0.0)
            u["m_rk"] = jnp.where(ha.incl, u["mk"][c:], 0.0)
        for u, t_inv in zip(units, ha.inv_unit_lower_many([-u["m_ab"] for u in units])):
            u["t_inv"] = t_inv
        for u in units:
            u["makv"] = ha.nn(u["m_ak"], u["bd_v"], BF16)
        for u in units:
            u["a_hat"] = ha.nn(u["t_inv"], ha.bd(u["a"]), BF16)
            u["u1"] = ha.nn(u["t_inv"], ha.bd(u["makv"]), BF16)
        for u in units:
            u["r_hat"] = u["r"] + ha.nn(u["m_rb"], ha.bd(u["a_hat"]))
            u["o1"] = ha.nn(u["m_rb"], ha.bd(u["u1"])) + ha.nn(u["m_rk"], u["bd_v"])
            u["g_t"] = ha.diag_bd(u["g"]) + ha.tn_bd(u["bo"], u["a_hat"])
            u["h_t"] = ha.tn_bd(jnp.concatenate([u["bo"], u["ko"]], axis=0),
                                jnp.concatenate([u["u1"], u["v"].astype(BF16)], axis=0))
            zz = lax.dot_general(jnp.concatenate([u["r_hat"], u["g_t"]], axis=0).astype(BF16), state.astype(BF16),
                                 _NN, preferred_element_type=F32)
            o_rows.append(zz[:c] + u["o1"])
            state = zz[c:] + u["h_t"]
    o = o_rows[0] if len(o_rows) == 1 else jnp.concatenate(o_rows, axis=0)
    s_scr[...] = state
    o_ref[...] = _rwkv_finish(o, r, k, v, z, rk_ref[...], lng_ref[...], lnb_ref[...], ones)

    @pl.when(j == last)
    def _():
        eye_h = (_iota((HEAD_DIM, HEAD_DIM), 0) == _iota((HEAD_DIM, HEAD_DIM), 1)).astype(F32)
        for h in range(N_HEADS):
            hs = slice(h * HEAD_DIM, (h + 1) * HEAD_DIM)
            s_out_ref[h] = _mm_nt(eye_h, state[hs, hs])
        shift_out_ref[...] = pa[ct - 1:ct, :]


def _row(a):
    return a.reshape(1, -1)


def _rwkv_call(p_a, n_seq, seq_len, lp):
    ct = min(ROW_TILE, seq_len)
    p3 = p_a.reshape(n_seq, seq_len, A_W)
    params = [_row(lp["rwkv_mu"]), _row(lp["rwkv_w0"]), lp["rwkv_w2"], _row(lp["rwkv_a0"]), lp["rwkv_a2"],
              _row(lp["rwkv_k_k"]), _row(lp["rwkv_k_a"]), _row(lp["rwkv_r_k"]), _row(lp["rwkv_ln_g"]),
              _row(lp["rwkv_ln_b"])]
    o, s1, shift1 = pl.pallas_call(
        _rwkv_kernel,
        grid=(n_seq, seq_len // ct),
        in_specs=[pl.BlockSpec((None, ct, A_W), lambda b, j: (b, j, 0))]
        + [pl.BlockSpec(a.shape, lambda b, j: (0, 0)) for a in params],
        out_specs=[
            pl.BlockSpec((None, ct, W_MIX), lambda b, j: (b, j, 0)),
            pl.BlockSpec((None, N_HEADS, HEAD_DIM, HEAD_DIM), lambda b, j: (b, 0, 0, 0)),
            pl.BlockSpec((None, 1, A_SHIFT_W), lambda b, j: (b, 0, 0)),
        ],
        out_shape=[
            jax.ShapeDtypeStruct((n_seq, seq_len, W_MIX), F32),
            jax.ShapeDtypeStruct((n_seq, N_HEADS, HEAD_DIM, HEAD_DIM), F32),
            jax.ShapeDtypeStruct((n_seq, 1, A_SHIFT_W), F32),
        ],
        scratch_shapes=[pltpu.VMEM((W_MIX, W_MIX), F32), pltpu.VMEM((ct + SUBLANES, A_SHIFT_W), F32)],
        compiler_params=pltpu.CompilerParams(dimension_semantics=("parallel", "arbitrary"),
                                             vmem_limit_bytes=VMEM_LIMIT),
        name="rwkv7_prompt",
    )(p3, *params)
    return o.reshape(n_seq * seq_len, W_MIX), s1, shift1.reshape(n_seq, A_SHIFT_W)


def _ret_kernel(p_ref, cos_ref, sin_ref, o_ref, s_out_ref, s_scr):
    j = pl.program_id(1)
    last = pl.num_programs(1) - 1
    ct = p_ref.shape[0]

    @pl.when(j == 0)
    def _():
        s_scr[...] = jnp.zeros(s_scr.shape, F32)

    p = p_ref[...]
    cos, sin = cos_ref[...], sin_ref[...]
    q = _rotary(p[:, 0:W_MIX], cos, sin)
    k = _rotary(p[:, W_MIX:2 * W_MIX], cos, sin) * (HEAD_DIM ** -0.5)
    v = p[:, 2 * W_MIX:3 * W_MIX]
    z = p[:, 3 * W_MIX:]
    c = min(RET_CHUNK, ct)
    ri, ci = _iota((c, c), 0), _iota((c, c), 1)
    causal = ri >= ci
    rel = jnp.where(causal, ri - ci, 0).astype(F32)
    idx = _iota((c, 1), 0).astype(F32)
    states = [s_scr[h] for h in range(N_HEADS)]
    units = []
    for h in range(N_HEADS):
        lg = math.log(1.0 - 2.0 ** (-5.0 - h))
        consts = dict(decay=jnp.where(causal, jnp.exp(lg * rel), 0.0), q_dec=jnp.exp(lg * (idx + 1.0)),
                      k_dec=jnp.exp(lg * (c - 1.0 - idx)), g_c=math.exp(lg * c))
        hs = slice(h * HEAD_DIM, (h + 1) * HEAD_DIM)
        for c0 in range(0, ct, c):
            sl = slice(c0, c0 + c)
            units.append(dict(consts, h=h, q=_Split(q[sl, hs], 1), k=k[sl, hs], v=_Split(v[sl, hs], 1)))
    for u in units:
        u["s_in"] = _dotp(u["q"], u["k"], _NT, P_MISC) * u["decay"]
        u["kv"] = _dotp(u["k"] * u["k_dec"], u["v"], _TN, P_MISC)
    for u in units:
        u["o"] = _dotp(u["s_in"], u["v"], _NN, P_MISC)
    for u in units:
        u["s0"] = states[u["h"]]
        states[u["h"]] = u["s0"] * u["g_c"] + u["kv"]
    o_heads_all = [[] for _ in range(N_HEADS)]
    for u in units:
        o_heads_all[u["h"]].append(u["o"] + _dotp(u["q"], u["s0"], _NN, P_MISC) * u["q_dec"])
    cols = [oh[0] if len(oh) == 1 else jnp.concatenate(oh, axis=0) for oh in o_heads_all]
    o = jnp.concatenate(cols, axis=1)
    for h in range(N_HEADS):
        s_scr[h] = states[h]
    o_ref[...] = _head_rms_finish(o, z, _head_ones())

    @pl.when(j == last)
    def _():
        for h in range(N_HEADS):
            s_out_ref[h] = states[h]


def _rope_tables(pos):
    half = HEAD_DIM // 2
    inv = ROPE_BASE ** (-jnp.arange(half, dtype=F32) / half)
    ang = pos.astype(F32)[:, None] * inv[None, :]
    cos, sin = jnp.cos(ang), jnp.sin(ang)
    cos_t = jnp.tile(jnp.concatenate([cos, cos], axis=-1), (1, N_HEADS))
    sin_t = jnp.tile(jnp.concatenate([-sin, sin], axis=-1), (1, N_HEADS))
    return cos_t, sin_t


def _ret_call(p_b, n_seq, seq_len, cos_t, sin_t):
    ct = min(ROW_TILE, seq_len)
    p3 = p_b.reshape(n_seq, seq_len, B_W)
    o, s1 = pl.pallas_call(
        _ret_kernel,
        grid=(n_seq, seq_len // ct),
        in_specs=[
            pl.BlockSpec((None, ct, B_W), lambda b, j: (b, j, 0)),
            pl.BlockSpec((ct, W_MIX), lambda b, j: (j, 0)),
            pl.BlockSpec((ct, W_MIX), lambda b, j: (j, 0)),
        ],
        out_specs=[
            pl.BlockSpec((None, ct, W_MIX), lambda b, j: (b, j, 0)),
            pl.BlockSpec((None, N_HEADS, HEAD_DIM, HEAD_DIM), lambda b, j: (b, 0, 0, 0)),
        ],
        out_shape=[
            jax.ShapeDtypeStruct((n_seq, seq_len, W_MIX), F32),
            jax.ShapeDtypeStruct((n_seq, N_HEADS, HEAD_DIM, HEAD_DIM), F32),
        ],
        scratch_shapes=[pltpu.VMEM((N_HEADS, HEAD_DIM, HEAD_DIM), F32)],
        compiler_params=pltpu.CompilerParams(dimension_semantics=("parallel", "arbitrary"),
                                             vmem_limit_bytes=VMEM_LIMIT),
        name="retention_prompt",
    )(p3, cos_t, sin_t)
    return o.reshape(n_seq * seq_len, W_MIX), s1


def _affine_scan(a, b, span):
    n, w = a.shape
    if span == SUBLANES and n > span:
        shape, axis = (n // span, span, w), 1
        a, b = a.reshape(shape), b.reshape(shape)
        pos = _iota((1, span, 1), 1)
    else:
        assert span >= n
        axis = 0
        pos = _iota((n, 1), 0)
    dist = 1
    while dist < span:
        keep = pos >= dist
        a_prev = jnp.where(keep, pltpu.roll(a, dist, axis=axis), 1.0)
        b_prev = jnp.where(keep, pltpu.roll(b, dist, axis=axis), 0.0)
        b = a * b_prev + b
        a = a * a_prev
        dist *= 2
    return a.reshape(n, w), b.reshape(n, w)


def _lru_kernel(p_ref, cw_ref, cb_ref, gw_ref, gb_ref, sp_ref, o_ref, h_out_ref, conv_out_ref,
                ext_scr, h_scr, ab_scr, hin_scr):
    j = pl.program_id(1)
    last = pl.num_programs(1) - 1
    ct = p_ref.shape[0]

    @pl.when(j == 0)
    def _():
        h_scr[...] = jnp.zeros(h_scr.shape, F32)

    p = p_ref[...]
    xr = p[:, :W_MIX]
    z = p[:, W_MIX:]
    xc = _conv_tile(xr, ext_scr, cw_ref, j == 0) + cb_ref[...]
    a, b = _lru_token_math(xc, gw_ref[...], gb_ref[...], sp_ref[...])
    n_grp = ct // SUBLANES
    a, b = _affine_scan(a, b, SUBLANES)
    n_tiles = W_MIX // LANES
    for t in range(n_tiles):
        ab_scr[t] = a[:, t * LANES:(t + 1) * LANES]
        ab_scr[n_tiles + t] = b[:, t * LANES:(t + 1) * LANES]
    ends = pl.ds(SUBLANES - 1, n_grp, stride=SUBLANES)
    a_end = jnp.concatenate([ab_scr[t, ends, :] for t in range(n_tiles)], axis=1)
    b_end = jnp.concatenate([ab_scr[n_tiles + t, ends, :] for t in range(n_tiles)], axis=1)
    a_end, b_end = _affine_scan(a_end, b_end, n_grp)
    h_prev = h_scr[...]
    h_end = a_end * h_prev + b_end
    grp = _iota((n_grp, 1), 0)
    hin_scr[...] = jnp.where(grp == 0, h_prev, pltpu.roll(h_end, 1, axis=0))
    h_in = jnp.concatenate([jnp.broadcast_to(hin_scr[g:g + 1, :], (SUBLANES, W_MIX)) for g in range(n_grp)], axis=0)
    hcur = a * h_in + b
    h_scr[...] = h_end[n_grp - 1:n_grp, :]
    o_ref[...] = hcur * _silu(z)

    @pl.when(j == last)
    def _():
        h_out_ref[...] = hcur[ct - 1:ct, :]
        conv_out_ref[...] = xr[ct - SUBLANES:ct, :]


def _block_diag_gates(gate_w):
    out = jnp.zeros((W_MIX, 2 * W_MIX), F32)
    for g in range(2):
        for n in range(N_HEADS):
            out = out.at[n * HEAD_DIM:(n + 1) * HEAD_DIM,
                         g * W_MIX + n * HEAD_DIM:g * W_MIX + (n + 1) * HEAD_DIM].set(gate_w[g, n])
    return out


def _lru_params(lp):
    return [lp["lru_conv_w"], _row(lp["lru_conv_b"]), _block_diag_gates(lp["lru_gate_w"]),
            _row(lp["lru_gate_b"]), _row(lp["lru_lambda"])]


def _lru_call(p_c, n_seq, seq_len, lp):
    ct = min(LRU_TILE, seq_len)
    p3 = p_c.reshape(n_seq, seq_len, C_W)
    params = _lru_params(lp)
    o, h1, conv_tail = pl.pallas_call(
        _lru_kernel,
        grid=(n_seq, seq_len // ct),
        in_specs=[pl.BlockSpec((None, ct, C_W), lambda b, j: (b, j, 0))]
        + [pl.BlockSpec(a.shape, lambda b, j: (0, 0)) for a in params],
        out_specs=[
            pl.BlockSpec((None, ct, W_MIX), lambda b, j: (b, j, 0)),
            pl.BlockSpec((None, 1, W_MIX), lambda b, j: (b, 0, 0)),
            pl.BlockSpec((None, SUBLANES, W_MIX), lambda b, j: (b, 0, 0)),
        ],
        out_shape=[
            jax.ShapeDtypeStruct((n_seq, seq_len, W_MIX), F32),
            jax.ShapeDtypeStruct((n_seq, 1, W_MIX), F32),
            jax.ShapeDtypeStruct((n_seq, SUBLANES, W_MIX), F32),
        ],
        scratch_shapes=[pltpu.VMEM((ct + SUBLANES, W_MIX), F32), pltpu.VMEM((1, W_MIX), F32),
                        pltpu.VMEM((2 * W_MIX // LANES, ct, LANES), F32),
                        pltpu.VMEM((ct // SUBLANES, W_MIX), F32)],
        compiler_params=pltpu.CompilerParams(dimension_semantics=("parallel", "arbitrary"),
                                             vmem_limit_bytes=VMEM_LIMIT),
        name="rglru_prompt",
    )(p3, *params)
    return (o.reshape(n_seq * seq_len, W_MIX), h1.reshape(n_seq, W_MIX),
            conv_tail[:, SUBLANES - (CONV_W - 1):, :])


def _gdn_kernel(p_ref, cw_ref, nal_ref, dtb_ref, ng_ref, o_ref, s_out_ref, conv_out_ref, ext_scr, s_scr):
    j = pl.program_id(1)
    last = pl.num_programs(1) - 1
    ct = p_ref.shape[0]

    @pl.when(j == 0)
    def _():
        s_scr[...] = jnp.zeros(s_scr.shape, F32)

    p = p_ref[...]
    raw = p[:, :D_QKV_W]
    z = p[:, D_QKV_W:D_QKV_W + W_MIX]
    b_raw = p[:, D_QKV_W + W_MIX:D_QKV_W + 2 * W_MIX]
    a_raw = p[:, D_QKV_W + 2 * W_MIX:]
    ones = _head_ones()
    qkv = _conv_tile(raw, ext_scr, cw_ref, j == 0)
    q, k, v, beta, g = _gdn_token_math(qkv, b_raw, a_raw, nal_ref[...], dtb_ref[...], ones)

    c = min(CHUNK, ct)
    ha = _HeadAlgebra(c)
    lt = (_iota((c, c), 0) >= _iota((c, c), 1)).astype(BF16)
    units = []
    for c0 in range(0, ct, c):
        sl = slice(c0, c0 + c)
        gc = _dot_const(g[sl], lt, _NN, CUMSUM_PIECES, const_left=True)
        gc_cols = jnp.sum(gc * ha.eye, axis=0, keepdims=True)
        diff = gc - gc_cols
        decay = jnp.where(ha.incl, jnp.exp(jnp.where(ha.incl, diff, 0.0)), 0.0)
        kb = k[sl] * beta[sl]
        e_gc = jnp.exp(gc)
        g_last = gc[c - 1:c, :]
        units.append(dict(decay=decay, kb=kb, q=q[sl], k=k[sl], vb=v[sl] * beta[sl], kbe=kb * e_gc,
                          k_out=k[sl] * jnp.exp(g_last - gc), q_in=q[sl] * e_gc, e_last=jnp.exp(g_last)))
    all_units = units
    state = s_scr[...]
    o_rows = []
    for w0 in range(0, len(all_units), WAVE):
        units = all_units[w0:w0 + WAVE]
        for u in units:
            kq = ha.nt(jnp.concatenate([u["kb"], u["q"]], axis=0), ha.bd(u["k"]))
            u["a_mat"] = jnp.where(ha.strict, kq[:c] * u["decay"], 0.0)
            u["qk"] = kq[c:] * u["decay"]
        for u, t_inv in zip(units, ha.inv_unit_lower_many([u["a_mat"] for u in units])):
            u["t_inv"] = t_inv
        for u in units:
            u["u"] = ha.nn(u["t_inv"], ha.bd(u["vb"]), BF16)
            u["w"] = ha.nn(u["t_inv"], ha.bd(u["kbe"]), BF16)
        for u in units:
            u["g_mat"] = ha.diag_bd(u["e_last"]) - ha.tn_bd(u["k_out"], u["w"])
            u["h_mat"] = ha.tn_bd(u["k_out"], u["u"])
            u["q_hat"] = u["q_in"] - ha.nn(u["qk"], ha.bd(u["w"]))
            u["o1"] = ha.nn(u["qk"], ha.bd(u["u"]))
            zz = lax.dot_general(jnp.concatenate([u["q_hat"], u["g_mat"]], axis=0).astype(BF16),
                                 state.astype(BF16), _NN, preferred_element_type=F32)
            o_rows.append(zz[:c] + u["o1"])
            state = zz[c:] + u["h_mat"]
    o = o_rows[0] if len(o_rows) == 1 else jnp.concatenate(o_rows, axis=0)
    s_scr[...] = state
    o_ref[...] = _head_rms_finish(o, z, ones, ng_ref[...])

    @pl.when(j == last)
    def _():
        for h in range(N_HEADS):
            hs = slice(h * HEAD_DIM, (h + 1) * HEAD_DIM)
            s_out_ref[h] = state[hs, hs]
        conv_out_ref[...] = raw[ct - SUBLANES:ct, :]


def _gdn_params(lp):
    return [lp["gdn_conv_w"], _row(jnp.repeat(lp["gdn_A_log"], HEAD_DIM)),
            _row(jnp.repeat(lp["gdn_dt_bias"], HEAD_DIM)), _row(jnp.tile(lp["gdn_norm_g"], N_HEADS))]


def _gdn_call(p_d, n_seq, seq_len, lp):
    ct = min(ROW_TILE, seq_len)
    p3 = p_d.reshape(n_seq, seq_len, D_PACK_W)
    params = _gdn_params(lp)
    o, s1, conv_tail = pl.pallas_call(
        _gdn_kernel,
        grid=(n_seq, seq_len // ct),
        in_specs=[pl.BlockSpec((None, ct, D_PACK_W), lambda b, j: (b, j, 0))]
        + [pl.BlockSpec(a.shape, lambda b, j: (0, 0)) for a in params],
        out_specs=[
            pl.BlockSpec((None, ct, W_MIX), lambda b, j: (b, j, 0)),
            pl.BlockSpec((None, N_HEADS, HEAD_DIM, HEAD_DIM), lambda b, j: (b, 0, 0, 0)),
            pl.BlockSpec((None, SUBLANES, D_QKV_W), lambda b, j: (b, 0, 0)),
        ],
        out_shape=[
            jax.ShapeDtypeStruct((n_seq, seq_len, W_MIX), F32),
            jax.ShapeDtypeStruct((n_seq, N_HEADS, HEAD_DIM, HEAD_DIM), F32),
            jax.ShapeDtypeStruct((n_seq, SUBLANES, D_QKV_W), F32),
        ],
        scratch_shapes=[pltpu.VMEM((ct + SUBLANES, D_QKV_W), F32), pltpu.VMEM((W_MIX, W_MIX), F32)],
        compiler_params=pltpu.CompilerParams(dimension_semantics=("parallel", "arbitrary"),
                                             vmem_limit_bytes=VMEM_LIMIT),
        name="gdn_prompt",
    )(p3, *params)
    return o.reshape(n_seq * seq_len, W_MIX), s1, conv_tail[:, SUBLANES - (CONV_W - 1):, :]


def _decode_pre_kernel(pa_ref, pb_ref, pc_ref, pd_ref, shift_ref, h0_ref, lconv_ref, gconv_ref, cos_ref, sin_ref,
                       mu_ref, w0_ref, w2_ref, a0_ref, a2_ref, kk_ref, ka_ref,
                       lcw_ref, lcb_ref, lgw_ref, lgb_ref, lsp_ref, gcw_ref, nal_ref, dtb_ref,
                       vt_ref, vn_ref, oc_ref, h1_ref, lconv1_ref, gconv1_ref):
    ones = _head_ones()
    pa_full = pa_ref[...]
    pa = pa_full[:, :A_SHIFT_W]
    pm = pa + (shift_ref[...] - pa) * mu_ref[...]
    r, k, v, ld, av, bv = _rwkv_token_math(pm, w0_ref[...], w2_ref[...], a0_ref[...], a2_ref[...],
                                           kk_ref[...], ka_ref[...], ones)
    vecs = [r, jnp.exp(ld), k, v, av, bv]
    plain = [r, k, v, pa_full[:, A_SHIFT_W:]]
    pb = pb_ref[...]
    cos, sin = cos_ref[...], sin_ref[...]
    vecs += [_rotary(pb[:, 0:W_MIX], cos, sin), _rotary(pb[:, W_MIX:2 * W_MIX], cos, sin) * (HEAD_DIM ** -0.5),
             pb[:, 2 * W_MIX:3 * W_MIX]]
    plain.append(pb[:, 3 * W_MIX:])
    pc = pc_ref[...]
    xr = pc[:, :W_MIX]
    taps = [lconv_ref[i] for i in range(CONV_W - 1)] + [xr]
    xc = taps[0] * lcw_ref[0:1, :]
    for i in range(1, CONV_W):
        xc = xc + taps[i] * lcw_ref[i:i + 1, :]
    xc = xc + lcb_ref[...]
    a, b = _lru_token_math(xc, lgw_ref[...], lgb_ref[...], lsp_ref[...])
    hcur = a * h0_ref[...] + b
    oc_ref[...] = hcur * _silu(pc[:, W_MIX:])
    h1_ref[...] = hcur
    for i in range(CONV_W - 1):
        lconv1_ref[i] = taps[i + 1]
    pd = pd_ref[...]
    raw = pd[:, :D_QKV_W]
    gtaps = [gconv_ref[i] for i in range(CONV_W - 1)] + [raw]
    qkv = gtaps[0] * gcw_ref[0:1, :]
    for i in range(1, CONV_W):
        qkv = qkv + gtaps[i] * gcw_ref[i:i + 1, :]
    q, kg, vg, beta, g = _gdn_token_math(qkv, pd[:, D_QKV_W + W_MIX:D_QKV_W + 2 * W_MIX],
                                         pd[:, D_QKV_W + 2 * W_MIX:], nal_ref[...], dtb_ref[...], ones)
    for i in range(CONV_W - 1):
        gconv1_ref[i] = gtaps[i + 1]
    vecs += [q, kg, vg, beta, g]
    plain.append(pd[:, D_QKV_W:D_QKV_W + W_MIX])
    assert len(vecs) == N_VEC_T and len(plain) == N_VEC_PLAIN
    for i, vec in enumerate(vecs):
        vt_ref[i] = vec.T
    for i, vec in enumerate(plain):
        vn_ref[i] = vec


def _decode_state_kernel(vt_ref, wkv_ref, ret_ref, gdn_ref, gam_ref, wkv1_ref, ret1_ref, gdn1_ref, o_ref):
    v_r, v_w, v_k, v_v, v_a, v_b, r_q, r_k, r_v, g_q, g_k, g_v, g_beta, g_g = range(N_VEC_T)
    hd = HEAD_DIM
    n = vt_ref.shape[-1]
    row = lambda idx, i: vt_ref[idx, pl.ds(i, 1), :]
    rows_of = lambda i: pl.ds(pl.multiple_of(i * hd, hd), hd)
    gamma = gam_ref[...]
    beta = vt_ref[g_beta, 0:1, :]
    eg = jnp.exp(vt_ref[g_g, 0:1, :])

    def first_pass(i, carry):
        acc_ret, acc_w, acc_q = carry
        rows = rows_of(i)
        s = wkv_ref[rows, :]
        sa = jnp.sum(s * vt_ref[v_a], axis=0, keepdims=True)
        s = s * vt_ref[v_w] + sa * vt_ref[v_b] + row(v_v, i) * vt_ref[v_k]
        wkv1_ref[rows, :] = s
        o_ref[0, pl.ds(i, 1), :] = jnp.sum(s * vt_ref[v_r], axis=0, keepdims=True)
        s = ret_ref[rows, :] * gamma + row(r_k, i) * vt_ref[r_v]
        ret1_ref[rows, :] = s
        acc_ret = acc_ret + row(r_q, i) * s
        s = gdn_ref[rows, :]
        return acc_ret, acc_w + row(g_k, i) * s, acc_q + row(g_q, i) * s

    zeros = jnp.zeros((hd, n), F32)
    acc_ret, acc_w, acc_q = lax.fori_loop(0, hd, first_pass, (zeros, zeros, zeros))
    o_ref[1] = acc_ret
    v_new = vt_ref[g_v] * beta - acc_w * (beta * eg)
    qk = jnp.sum(vt_ref[g_q] * vt_ref[g_k], axis=0, keepdims=True)
    o_ref[2] = acc_q * eg + qk * v_new

    def second_pass(i, carry):
        rows = rows_of(i)
        gdn1_ref[rows, :] = gdn_ref[rows, :] * eg + row(g_k, i) * v_new
        return carry

    lax.fori_loop(0, hd, second_pass, 0)


def _decode_finish_kernel(ot_ref, vn_ref, rk_ref, lng_ref, lnb_ref, ng_ref, oa_ref, ob_ref, od_ref):
    ones = _head_ones()
    r, k, v, z_a, z_b, z_d = (vn_ref[i] for i in range(N_VEC_PLAIN))
    oa_ref[...] = _rwkv_finish(ot_ref[0].T, r, k, v, z_a, rk_ref[...], lng_ref[...], lnb_ref[...], ones)
    ob_ref[...] = _head_rms_finish(ot_ref[1].T, z_b, ones)
    od_ref[...] = _head_rms_finish(ot_ref[2].T, z_d, ones, ng_ref[...])


def _batch_minor(state):
    n_layers, n = state.shape[:2]
    return jnp.transpose(state, (0, 2, 3, 4, 1)).reshape(n_layers, -1, n)


def _batch_major(flat_state):
    n = flat_state.shape[-1]
    return jnp.transpose(flat_state.reshape(N_HEADS, HEAD_DIM, HEAD_DIM, n), (3, 0, 1, 2))


def _decode_layer(l, p_a, p_b, p_c, p_d, carried, cos_t, sin_t, lp):
    n = p_a.shape[0]
    assert n % LANES == 0, "the decode state kernel keeps the batch on lanes"
    flat = HEAD_DIM * HEAD_DIM
    taps = CONV_W - 1
    rwkv_params = [_row(lp["rwkv_mu"]), _row(lp["rwkv_w0"]), lp["rwkv_w2"], _row(lp["rwkv_a0"]), lp["rwkv_a2"],
                   _row(lp["rwkv_k_k"]), _row(lp["rwkv_k_a"])]
    gdn_params = _gdn_params(lp)
    full = lambda a: pl.BlockSpec(a.shape, lambda i: (0,) * a.ndim)
    layer_blk = lambda a: pl.BlockSpec((None,) + a.shape[1:], lambda i, nd=a.ndim: (l,) + (0,) * (nd - 1))
    projs = [p_a, p_b, p_c, p_d]
    layered = [carried["shift"], carried["lru_h"], carried["lru_conv"], carried["gdn_conv"]]
    consts = [cos_t, sin_t, *rwkv_params, *_lru_params(lp), *gdn_params[:3]]
    out_shapes = [
        jax.ShapeDtypeStruct((N_VEC_T, W_MIX, n), F32),
        jax.ShapeDtypeStruct((N_VEC_PLAIN, n, W_MIX), F32),
        jax.ShapeDtypeStruct((n, W_MIX), F32),
        jax.ShapeDtypeStruct((n, W_MIX), F32),
        jax.ShapeDtypeStruct((taps, n, W_MIX), F32),
        jax.ShapeDtypeStruct((taps, n, D_QKV_W), F32),
    ]
    vec_t, vec_n, o_c, lru_h1, lru_conv1, gdn_conv1 = pl.pallas_call(
        _decode_pre_kernel,
        grid=(1,),
        in_specs=[full(a) for a in projs] + [layer_blk(a) for a in layered] + [full(a) for a in consts],
        out_specs=[pl.BlockSpec(s.shape, lambda i, nd=len(s.shape): (0,) * nd) for s in out_shapes],
        out_shape=out_shapes,
        compiler_params=pltpu.CompilerParams(dimension_semantics=("arbitrary",), vmem_limit_bytes=VMEM_LIMIT),
        name="decode_tokens",
    )(*projs, *layered, *consts)

    gam = jnp.broadcast_to((1.0 - 2.0 ** (-5.0 - jnp.arange(N_HEADS, dtype=F32)))[:, None, None], (N_HEADS, 1, n))
    state_in = pl.BlockSpec((None, flat, n), lambda h: (l, h, 0))
    state_out = pl.BlockSpec((flat, n), lambda h: (h, 0))
    wkv1, ret1, gdn1, o_t = pl.pallas_call(
        _decode_state_kernel,
        grid=(N_HEADS,),
        in_specs=[pl.BlockSpec((N_VEC_T, HEAD_DIM, n), lambda h: (0, h, 0)), state_in, state_in, state_in,
                  pl.BlockSpec((None, 1, n), lambda h: (h, 0, 0))],
        out_specs=[state_out, state_out, state_out, pl.BlockSpec((3, HEAD_DIM, n), lambda h: (0, h, 0))],
        out_shape=[jax.ShapeDtypeStruct((N_HEADS * flat, n), F32)] * 3 + [jax.ShapeDtypeStruct((3, W_MIX, n), F32)],
        compiler_params=pltpu.CompilerParams(dimension_semantics=("parallel",), vmem_limit_bytes=VMEM_LIMIT),
        name="decode_states",
    )(vec_t, carried["wkv"], carried["ret"], carried["gdn"], gam)

    finish_ins = [o_t, vec_n, _row(lp["rwkv_r_k"]), _row(lp["rwkv_ln_g"]), _row(lp["rwkv_ln_b"]), gdn_params[3]]
    o_a, o_b, o_d = pl.pallas_call(
        _decode_finish_kernel,
        grid=(1,),
        in_specs=[full(a) for a in finish_ins],
        out_specs=[pl.BlockSpec((n, W_MIX), lambda i: (0, 0))] * 3,
        out_shape=[jax.ShapeDtypeStruct((n, W_MIX), F32)] * 3,
        compiler_params=pltpu.CompilerParams(dimension_semantics=("arbitrary",), vmem_limit_bytes=VMEM_LIMIT),
        name="decode_finish",
    )(*finish_ins)
    new_states = (_batch_major(wkv1), p_a[:, :A_SHIFT_W], _batch_major(ret1), lru_h1,
                  jnp.transpose(lru_conv1, (1, 0, 2)), _batch_major(gdn1), jnp.transpose(gdn_conv1, (1, 0, 2)))
    return (o_a, o_b, o_c, o_d), new_states


def _pack_in_weights(w_in):
    off_d = A_W + B_W + C_W
    off_ba = off_d + D_QKV_W
    off_z = off_ba + 2 * N_HEADS
    off_g = off_d + D_W
    w_pack = jnp.concatenate([
        w_in[:, :off_ba],
        w_in[:, off_z:off_g],
        jnp.repeat(w_in[:, off_ba:off_ba + N_HEADS], HEAD_DIM, axis=1),
        jnp.repeat(w_in[:, off_ba + N_HEADS:off_z], HEAD_DIM, axis=1),
    ], axis=1)
    return w_pack, w_in[:, off_g:]


def _prompt_layer(p_a, p_b, p_c, p_d, n_seq, seq_len, cos_t, sin_t, lp):
    o_a, wkv1, shift1 = _rwkv_call(p_a, n_seq, seq_len, lp)
    o_b, ret1 = _ret_call(p_b, n_seq, seq_len, cos_t, sin_t)
    o_c, lru_h1, lru_conv1 = _lru_call(p_c, n_seq, seq_len, lp)
    o_d, gdn1, gdn_conv1 = _gdn_call(p_d, n_seq, seq_len, lp)
    return (o_a, o_b, o_c, o_d), (wkv1, shift1, ret1, lru_h1, lru_conv1, gdn1, gdn_conv1)


def _run_group(x, mods, pos, carried, layers, final_g):
    n_seq, seq_len, d = x.shape
    x2 = x.reshape(n_seq * seq_len, d)
    cos_t, sin_t = _rope_tables(pos)
    new = []
    n_layers = len(layers)
    for l, lp in enumerate(layers):
        shift, scale, gate = (mods[l][:, i * d:(i + 1) * d] for i in range(3))
        p_a, p_b, p_c, p_d = _inproj_call(x2, scale, shift, lp["norm_g"], lp["w_pack"], seq_len)
        if carried is None:
            branches, st = _prompt_layer(p_a, p_b, p_c, p_d, n_seq, seq_len, cos_t, sin_t, lp)
        else:
            branches, st = _decode_layer(l, p_a, p_b, p_c, p_d, carried, cos_t, sin_t, lp)
        new.append(st)
        x2 = _outproj_call(x2, scale, shift, gate, lp["norm_g"], branches, lp["w_gate"], lp["w_up_bf16"],
                           lp["w_out_bf16"], final_g, seq_len, final=(l == n_layers - 1))
    stacked = tuple(jnp.stack([s[i] for s in new], axis=0) for i in range(7))
    return x2.reshape(n_seq, seq_len, d), stacked


def kernel(x_prompt, x_sample, c_prompt, c_sample, state_rwkv_wkv, state_rwkv_shift, state_ret, state_lru_h, state_lru_conv, state_gdn, state_gdn_conv, ada_w, ada_b, norm_g, w_in, rwkv_mu, rwkv_w0, rwkv_w2, rwkv_a0, rwkv_a2, rwkv_k_k, rwkv_k_a, rwkv_r_k, rwkv_ln_g, rwkv_ln_b, lru_conv_w, lru_conv_b, lru_gate_w, lru_gate_b, lru_lambda, gdn_conv_w, gdn_A_log, gdn_dt_bias, gdn_norm_g, w_up, w_out, final_g):
    n_layers = ada_w.shape[0]
    n_prompt, seq_len, _ = x_prompt.shape
    n_sample, dec_len, _ = x_sample.shape
    assert dec_len == 1, "the decode path handles one token per sequence"
    layers = []
    w_in_bf16 = w_in.astype(BF16)
    for l in range(n_layers):
        w_pack, w_gate = _pack_in_weights(w_in_bf16[l])
        layers.append(dict(
            norm_g=norm_g[l], w_pack=w_pack, w_gate=w_gate, w_up_bf16=w_up[l].astype(BF16),
            w_out_bf16=w_out[l].astype(BF16),
            rwkv_mu=rwkv_mu[l], rwkv_w0=rwkv_w0[l], rwkv_w2=rwkv_w2[l], rwkv_a0=rwkv_a0[l], rwkv_a2=rwkv_a2[l],
            rwkv_k_k=rwkv_k_k[l], rwkv_k_a=rwkv_k_a[l], rwkv_r_k=rwkv_r_k[l], rwkv_ln_g=rwkv_ln_g[l],
            rwkv_ln_b=rwkv_ln_b[l], lru_conv_w=lru_conv_w[l], lru_conv_b=lru_conv_b[l],
            lru_gate_w=lru_gate_w[l], lru_gate_b=lru_gate_b[l], lru_lambda=lru_lambda[l],
            gdn_conv_w=gdn_conv_w[l], gdn_A_log=gdn_A_log[l], gdn_dt_bias=gdn_dt_bias[l],
            gdn_norm_g=gdn_norm_g[l]))
    mods = _ada_call(jnp.concatenate([c_prompt, c_sample], axis=0), ada_w, ada_b)
    mods_p = [mods[l, :n_prompt] for l in range(n_layers)]
    mods_s = [mods[l, n_prompt:] for l in range(n_layers)]

    y_prompt, new_p = _run_group(x_prompt, mods_p, jnp.arange(seq_len, dtype=jnp.int32), None, layers, final_g)
    carried = dict(wkv=_batch_minor(state_rwkv_wkv), ret=_batch_minor(state_ret), gdn=_batch_minor(state_gdn),
                   shift=state_rwkv_shift, lru_h=state_lru_h,
                   lru_conv=jnp.transpose(state_lru_conv, (0, 2, 1, 3)),
                   gdn_conv=jnp.transpose(state_gdn_conv, (0, 2, 1, 3)))
    pos_s = PAST_LEN + jnp.arange(dec_len, dtype=jnp.int32)
    y_sample, new_s = _run_group(x_sample, mods_s, pos_s, carried, layers, final_g)
    return (y_prompt, y_sample) + new_p + new_s
```

```python
import functools
import math

import jax
import jax.numpy as jnp
from jax import lax
from jax.experimental import pallas as pl
from jax.experimental.pallas import tpu as pltpu

F32 = jnp.float32
BF16 = jnp.bfloat16
HI = lax.Precision.HIGHEST

N_HEADS = 4
HEAD_DIM = 64
W_MIX = N_HEADS * HEAD_DIM
LORA = 64
CONV_W = 4
N_BRANCH = 4
LRU_C = 8.0
ROPE_BASE = 10000.0
EPS = 1e-6
RWKV_GN_EPS = 64e-5
PAST_LEN = 16384
A_SHIFT_W = 3 * W_MIX + 2 * LORA
A_W = A_SHIFT_W + W_MIX
B_W = 4 * W_MIX
C_W = 2 * W_MIX
D_QKV_W = 3 * W_MIX
D_W = D_QKV_W + 2 * N_HEADS + W_MIX
D_PACK_W = D_QKV_W + 3 * W_MIX

SUBLANES = 8
LANES = 128
VMEM_LIMIT = 56 * 1024 * 1024

CHUNK = 64
RET_CHUNK = 128
INV_BLOCK = 16
WAVE = 16
ROW_TILE = 1024
LRU_TILE = 512
PROJ_TILE = 512
OUT_TILE = 1024
N_VEC_T = 14
N_VEC_PLAIN = 6


def _mm(a, b, prec=HI):
    return lax.dot_general(a, b, (((1,), (0,)), ((), ())), precision=prec, preferred_element_type=F32)


def _mm_nt(a, b, prec=HI):
    return lax.dot_general(a, b, (((1,), (1,)), ((), ())), precision=prec, preferred_element_type=F32)


def _mm_tn(a, b, prec=HI):
    return lax.dot_general(a, b, (((0,), (0,)), ((), ())), precision=prec, preferred_element_type=F32)


_NN = (((1,), (0,)), ((), ()))
_NT = (((1,), (1,)), ((), ()))
_TN = (((0,), (0,)), ((), ()))

P_INV = 1
P_STATE = 1
P_MISC = 1
HEAD_SUM_PIECES = 1
CUMSUM_PIECES = 2


class _Split:
    def __init__(self, x, passes):
        self.hi = x.astype(BF16)
        self.lo = (x - self.hi.astype(F32)).astype(BF16) if passes > 1 else None


def _dotp(a, b, dims=_NN, passes=1):
    a = a if isinstance(a, _Split) else _Split(a, passes)
    b = b if isinstance(b, _Split) else _Split(b, passes)
    d = lambda x, y: lax.dot_general(x, y, dims, preferred_element_type=F32)
    out = d(a.hi, b.hi)
    if passes > 1:
        out = out + (d(a.hi, b.lo) + d(a.lo, b.hi))
    return out


def _iota(shape, dim):
    return lax.broadcasted_iota(jnp.int32, shape, dim)


def _silu(x):
    return x * jax.nn.sigmoid(x)


def _softplus(x):
    return jnp.maximum(x, 0.0) + jnp.log1p(jnp.exp(-jnp.abs(x)))


def _pieces(x, n):
    out = []
    for i in range(n):
        p = x.astype(BF16)
        out.append(p)
        if i + 1 < n:
            x = x - p.astype(F32)
    return out


def _dot_const(x, const, dims=_NN, n=2, const_left=False):
    out = None
    for p in _pieces(x, n):
        t = lax.dot_general(*((const, p) if const_left else (p, const)), dims, preferred_element_type=F32)
        out = t if out is None else out + t
    return out


def _head_ones():
    return (_iota((W_MIX, W_MIX), 0) // HEAD_DIM == _iota((W_MIX, W_MIX), 1) // HEAD_DIM).astype(BF16)


def _head_sum(x, ones, signed=False):
    return _dot_const(x, ones, n=HEAD_SUM_PIECES + (1 if signed else 0))


def _rms(x):
    return x * lax.rsqrt(jnp.mean(x * x, axis=-1, keepdims=True) + EPS)


def _inv_unit_lower(a):
    return _inv_unit_lower_many([a])[0]


def _inv_unit_lower_many(mats):
    n = mats[0].shape[0]
    ri, ci = _iota((n, n), 0), _iota((n, n), 1)
    eye = (ri == ci).astype(F32)
    diag_blk = (ri // INV_BLOCK) == (ci // INV_BLOCK)
    mm = lambda x, y: _dotp(x, y, _NN, P_INV)
    sp = lambda x: _Split(x, P_INV)
    d = [jnp.where(diag_blk, a, 0.0) for a in mats]
    nb = [a - di for a, di in zip(mats, d)]
    td = [eye - di for di in d]
    p = d
    for _ in range(int(math.log2(INV_BLOCK)) - 1):
        ps = [sp(pi) for pi in p]
        p = [mm(pi, pi) for pi in ps]
        td = [mm(ti, eye + pi) for ti, pi in zip(td, p)]
    tds = [sp(ti) for ti in td]
    x = [mm(ti, ni) for ti, ni in zip(tds, nb)]
    t = [eye - xi for xi in x]
    p = x
    for _ in range(int(math.log2(n // INV_BLOCK)) - 1):
        ps = [sp(pi) for pi in p]
        p = [mm(pi, pi) for pi in ps]
        t = [mm(ti, eye + pi) for ti, pi in zip(t, p)]
    return [mm(ti, tdi) for ti, tdi in zip(t, tds)]


class _HeadAlgebra:
    def __init__(self, c):
        assert c == HEAD_DIM, "side-by-side head products need CHUNK == HEAD_DIM"
        w = W_MIX
        row, lane = _iota((c, w), 0), _iota((c, w), 1)
        col = lane % HEAD_DIM
        tile_lane = _iota((c, LANES), 1)
        self.tile_head = [tile_lane // HEAD_DIM == h for h in range(LANES // HEAD_DIM)]
        self.eye = (row == col).astype(F32)
        self.strict = row > col
        self.incl = row >= col
        self.inv_blk = (row // INV_BLOCK) == (col // INV_BLOCK)
        r2, c2 = _iota((w, w), 0), _iota((w, w), 1)
        self.eye_full = r2 == c2
        self.same_head = (r2 // HEAD_DIM) == (c2 // HEAD_DIM)

    def bd(self, y):
        yb = y.astype(BF16)
        zero = jnp.zeros((yb.shape[0], LANES), BF16)
        blocks = []
        for t in range(W_MIX // LANES):
            tile = yb[:, t * LANES:(t + 1) * LANES]
            for m in self.tile_head:
                kept = jnp.where(m, tile, zero)
                blocks.append(jnp.concatenate([kept if s == t else zero for s in range(W_MIX // LANES)], axis=1))
        return jnp.concatenate(blocks, axis=0)

    def nn(self, x, bd_y, out=F32):
        return lax.dot_general(x.astype(BF16), bd_y, _NN, preferred_element_type=F32).astype(out)

    def nt(self, x, bd_y):
        return lax.dot_general(x.astype(BF16), bd_y, _NT, preferred_element_type=F32)

    def tn_bd(self, x, y):
        full = lax.dot_general(x.astype(BF16), y.astype(BF16), _TN, preferred_element_type=F32)
        return jnp.where(self.same_head, full, 0.0)

    def diag_bd(self, row_vec):
        return jnp.where(self.eye_full, row_vec, 0.0)

    def plus_eye(self, bd_p):
        return jnp.where(self.eye_full, jnp.ones_like(bd_p), bd_p)

    def inv_unit_lower_many(self, mats):
        mats = [a.astype(BF16) for a in mats]
        zero = jnp.zeros_like(mats[0])
        eye = self.eye.astype(BF16)
        d = [jnp.where(self.inv_blk, a, zero) for a in mats]
        nb = [jnp.where(self.inv_blk, zero, a) for a in mats]
        td = [eye - di for di in d]
        p = d
        bdp = [self.bd(pi) for pi in p]
        for _ in range(int(math.log2(INV_BLOCK)) - 1):
            p = [self.nn(pi, bi, BF16) for pi, bi in zip(p, bdp)]
            bdp = [self.bd(pi) for pi in p]
            td = [self.nn(ti, self.plus_eye(bi), BF16) for ti, bi in zip(td, bdp)]
        bd_td = [self.bd(ti) for ti in td]
        x = [self.nn(ti, self.bd(ni), BF16) for ti, ni in zip(td, nb)]
        t = [eye - xi for xi in x]
        p = x
        bdp = [self.bd(pi) for pi in p]
        for _ in range(int(math.log2(HEAD_DIM // INV_BLOCK)) - 1):
            p = [self.nn(pi, bi, BF16) for pi, bi in zip(p, bdp)]
            bdp = [self.bd(pi) for pi in p]
            t = [self.nn(ti, self.plus_eye(bi), BF16) for ti, bi in zip(t, bdp)]
        return [self.nn(ti, bi, BF16) for ti, bi in zip(t, bd_td)]


def _ada_kernel(c_ref, w_ref, b_ref, o_ref):
    o_ref[...] = _dotp(_silu(c_ref[...]), w_ref[...], _NN, 3) + b_ref[...]


def _ada_call(c_all, ada_w, ada_b):
    n_layers, d, d3 = ada_w.shape
    rows = c_all.shape[0]
    return pl.pallas_call(
        _ada_kernel,
        grid=(n_layers, d3 // d),
        in_specs=[
            pl.BlockSpec((rows, d), lambda l, j: (0, 0)),
            pl.BlockSpec((None, d, d), lambda l, j: (l, 0, j)),
            pl.BlockSpec((None, 1, d), lambda l, j: (l, 0, j)),
        ],
        out_specs=pl.BlockSpec((None, rows, d), lambda l, j: (l, 0, j)),
        out_shape=jax.ShapeDtypeStruct((n_layers, rows, d3), F32),
        compiler_params=pltpu.CompilerParams(dimension_semantics=("arbitrary", "arbitrary"),
                                             vmem_limit_bytes=VMEM_LIMIT),
        name="ada_mod",
    )(c_all, ada_w, ada_b.reshape(n_layers, 1, d3))


def _modulated_norm(x, g, scale, shift):
    return _rms(x) * g * (1.0 + scale) + shift


def _inproj_kernel(x_ref, sc_ref, sh_ref, g_ref, w_ref, oa_ref, ob_ref, oc_ref, od_ref):
    h = _modulated_norm(x_ref[...], g_ref[...], sc_ref[...], sh_ref[...]).astype(BF16)
    lo = 0
    for o_ref in (oa_ref, ob_ref, oc_ref, od_ref):
        wd = o_ref.shape[-1]
        o_ref[...] = lax.dot_general(h, w_ref[lo:lo + wd, :], _NT, preferred_element_type=F32)
        lo += wd


def _mod_specs(mods, tm, seq_len):
    d = mods[0].shape[-1]
    if seq_len == 1:
        return [m for m in mods], [pl.BlockSpec((tm, d), lambda i: (i, 0)) for _ in mods]
    per_seq = seq_len // tm
    return ([m.reshape(m.shape[0], 1, d) for m in mods],
            [pl.BlockSpec((None, 1, d), lambda i: (i // per_seq, 0, 0)) for _ in mods])


def _inproj_call(x2, scale, shift, g, w_pack, seq_len):
    m, d = x2.shape
    tm = min(PROJ_TILE, m, seq_len) if seq_len > 1 else m
    widths = (A_W, B_W, C_W, D_PACK_W)
    mods, mod_specs = _mod_specs((scale, shift), tm, seq_len)
    return pl.pallas_call(
        _inproj_kernel,
        grid=(m // tm,),
        in_specs=[pl.BlockSpec((tm, d), lambda i: (i, 0))] + mod_specs + [
            pl.BlockSpec((1, d), lambda i: (0, 0)),
            pl.BlockSpec(w_pack.shape, lambda i: (0, 0)),
        ],
        out_specs=[pl.BlockSpec((tm, wd), lambda i: (i, 0)) for wd in widths],
        out_shape=[jax.ShapeDtypeStruct((m, wd), F32) for wd in widths],
        compiler_params=pltpu.CompilerParams(dimension_semantics=("parallel",), vmem_limit_bytes=VMEM_LIMIT),
        name="in_proj",
    )(x2, *mods, g.reshape(1, d), w_pack)


def _outproj_kernel(x_ref, sc_ref, sh_ref, gt_ref, g_ref, ba_ref, bb_ref, bc_ref, bd_ref,
                    wg_ref, wup_ref, wout_ref, fg_ref, o_ref, *, final):
    x = x_ref[...]
    d = x.shape[-1]
    h = _modulated_norm(x, g_ref[...], sc_ref[...], sh_ref[...]).astype(BF16)
    merged = jnp.zeros(x.shape, F32)
    for n, br_ref in enumerate((ba_ref, bb_ref, bc_ref, bd_ref)):
        gl = lax.dot_general(h, wg_ref[n * d:(n + 1) * d, :], _NT, preferred_element_type=F32)
        up = jnp.dot(br_ref[...].astype(BF16), wup_ref[n], preferred_element_type=F32)
        merged = merged + jax.nn.sigmoid(gl) * up
    out = jnp.dot(merged.astype(BF16), wout_ref[...], preferred_element_type=F32)
    xn = x + gt_ref[...] * out
    if final:
        xn = _rms(xn) * fg_ref[...]
    o_ref[...] = xn


def _outproj_call(x2, scale, shift, gate, g, branches, wg, wup, wout, final_g, seq_len, final):
    m, d = x2.shape
    tm = min(OUT_TILE, m, seq_len) if seq_len > 1 else m
    mods, mod_specs = _mod_specs((scale, shift, gate), tm, seq_len)
    full = lambda a: pl.BlockSpec(a.shape, lambda i: (0,) * a.ndim)
    return pl.pallas_call(
        functools.partial(_outproj_kernel, final=final),
        grid=(m // tm,),
        in_specs=[pl.BlockSpec((tm, d), lambda i: (i, 0))] + mod_specs + [pl.BlockSpec((1, d), lambda i: (0, 0))]
        + [pl.BlockSpec((tm, W_MIX), lambda i: (i, 0)) for _ in branches]
        + [full(wg), full(wup), full(wout), pl.BlockSpec((1, d), lambda i: (0, 0))],
        out_specs=pl.BlockSpec((tm, d), lambda i: (i, 0)),
        out_shape=jax.ShapeDtypeStruct((m, d), F32),
        compiler_params=pltpu.CompilerParams(dimension_semantics=("parallel",), vmem_limit_bytes=VMEM_LIMIT),
        name="out_proj",
    )(x2, *mods, g.reshape(1, d), *branches, wg, wup, wout, final_g.reshape(1, d))


def _rwkv_token_math(pm, w0, w2, a0, a2, k_k, k_a, ones):
    r = pm[:, 0:W_MIX]
    k = pm[:, W_MIX:2 * W_MIX]
    v = pm[:, 2 * W_MIX:3 * W_MIX]
    wd = pm[:, 3 * W_MIX:3 * W_MIX + LORA]
    ad = pm[:, 3 * W_MIX + LORA:]
    w_log = -_softplus(-(w0 + _dotp(jnp.tanh(wd), w2, _NN, P_MISC))) - 0.5
    log_decay = -jnp.exp(w_log)
    a = jax.nn.sigmoid(a0 + _dotp(ad, a2, _NN, P_MISC))
    kx = k * k_k
    kk = kx * lax.rsqrt(_head_sum(kx * kx, ones) + EPS)
    k = k * (1.0 + (a - 1.0) * k_a)
    return r, k, v, log_decay, -kk, kk * a


def _rwkv_finish(o, r, k, v, z, r_k, ln_g, ln_b, ones):
    mean = _head_sum(o, ones, signed=True) * (1.0 / HEAD_DIM)
    dlt = o - mean
    var = _head_sum(dlt * dlt, ones) * (1.0 / HEAD_DIM)
    on = dlt * lax.rsqrt(var + RWKV_GN_EPS) * ln_g + ln_b
    bonus = _head_sum(r * k * r_k, ones, signed=True) * v
    return (on + bonus) * _silu(z)


def _swap_halves(x):
    half = HEAD_DIM // 2
    n = x.shape[-1]
    first = (_iota(x.shape, 1) & half) == 0
    return jnp.where(first, pltpu.roll(x, n - half, axis=1), pltpu.roll(x, half, axis=1))


def _rotary(x, cos, sin):
    return x * cos + _swap_halves(x) * sin


def _lru_token_math(xc, gate_w, gate_b, lam):
    gates = _dotp(xc, gate_w, _NN, P_MISC) + gate_b
    r_gate = jax.nn.sigmoid(gates[:, :W_MIX])
    i_gate = jax.nn.sigmoid(gates[:, W_MIX:])
    log_a = -LRU_C * r_gate * _softplus(-lam)
    a = jnp.exp(log_a)
    b = jnp.sqrt(1.0 - jnp.exp(2.0 * log_a)) * (i_gate * xc)
    return a, b


def _gdn_token_math(qkv, b_raw, a_raw, a_log, dt_bias, ones):
    qkv = _silu(qkv)
    q = qkv[:, 0:W_MIX]
    k = qkv[:, W_MIX:2 * W_MIX]
    v = qkv[:, 2 * W_MIX:]
    q = q * lax.rsqrt(_head_sum(q * q, ones) + EPS) * (HEAD_DIM ** -0.5)
    k = k * lax.rsqrt(_head_sum(k * k, ones) + EPS)
    beta = jax.nn.sigmoid(b_raw)
    g = -jnp.exp(a_log) * _softplus(a_raw + dt_bias)
    return q, k, v, beta, g


def _head_rms_finish(o, z, ones, gain=None):
    y = o * lax.rsqrt(_head_sum(o * o, ones) * (1.0 / HEAD_DIM) + EPS)
    if gain is not None:
        y = y * gain
    return y * _silu(z)


def _conv_tile(u, ext_ref, w_ref, first):
    n = u.shape[0]

    @pl.when(first)
    def _():
        ext_ref[0:SUBLANES, :] = jnp.zeros((SUBLANES, u.shape[1]), F32)

    ext_ref[SUBLANES:SUBLANES + n, :] = u
    out = None
    for j in range(CONV_W):
        term = _rows_back(u, ext_ref, CONV_W - 1 - j) * w_ref[j:j + 1, :]
        out = term if out is None else out + term
    ext_ref[0:SUBLANES, :] = u[n - SUBLANES:n, :]
    return out


def _rows_back(u, ext_ref, back):
    if back == 0:
        return u
    n, ch = u.shape
    tiles = (n // SUBLANES, SUBLANES, ch)
    pos = _iota((1, SUBLANES, 1), 1)
    earlier = ext_ref[0:n, :].reshape(tiles)
    return pltpu.roll(jnp.where(pos >= SUBLANES - back, earlier, u.reshape(tiles)), back, axis=1).reshape(n, ch)


def _rwkv_kernel(p_ref, mu_ref, w0_ref, w2_ref, a0_ref, a2_ref, kk_ref, ka_ref, rk_ref, lng_ref, lnb_ref,
                 o_ref, s_out_ref, shift_out_ref, s_scr, ext_scr):
    j = pl.program_id(1)
    last = pl.num_programs(1) - 1
    ct = p_ref.shape[0]

    @pl.when(j == 0)
    def _():
        s_scr[...] = jnp.zeros(s_scr.shape, F32)
        ext_scr[0:SUBLANES, :] = jnp.zeros((SUBLANES, A_SHIFT_W), F32)

    p = p_ref[...]
    pa = p[:, :A_SHIFT_W]
    z = p[:, A_SHIFT_W:]
    ext_scr[SUBLANES:SUBLANES + ct, :] = pa
    prev = _rows_back(pa, ext_scr, 1)
    ext_scr[0:SUBLANES, :] = pa[ct - SUBLANES:ct, :]
    pm = pa + (prev - pa) * mu_ref[...]
    ones = _head_ones()
    r, k, v, ld, av, bv = _rwkv_token_math(pm, w0_ref[...], w2_ref[...], a0_ref[...], a2_ref[...],
                                           kk_ref[...], ka_ref[...], ones)

    c = min(CHUNK, ct)
    ha = _HeadAlgebra(c)
    lt = (_iota((c, c), 0) >= _iota((c, c), 1)).astype(BF16)
    units = []
    for c0 in range(0, ct, c):
        sl = slice(c0, c0 + c)
        ldc = ld[sl]
        cum = _dot_const(ldc, lt, _NN, CUMSUM_PIECES, const_left=True)
        e_neg = jnp.exp(-cum)
        e_out = jnp.exp(cum[c - 1:c, :] - cum)
        units.append(dict(a=av[sl] * jnp.exp(cum - ldc), r=r[sl] * jnp.exp(cum), b=bv[sl] * e_neg, k=k[sl] * e_neg,
                          bo=bv[sl] * e_out, ko=k[sl] * e_out, v=v[sl], g=jnp.exp(cum[c - 1:c, :])))
    all_units = units
    state = s_scr[...]
    o_rows = []
    for w0 in range(0, len(all_units), WAVE):
        units = all_units[w0:w0 + WAVE]
        for u in units:
            lhs = jnp.concatenate([u["a"], u["r"]], axis=0)
            u["mb"] = ha.nt(lhs, ha.bd(u["b"]))
            u["mk"] = ha.nt(lhs, ha.bd(u["k"]))
            u["bd_v"] = ha.bd(u["v"])
        for u in units:
            u["m_ab"] = jnp.where(ha.strict, u["mb"][:c], 0.0)
            u["m_ak"] = jnp.where(ha.strict, u["mk"][:c], 0.0)
            u["m_rb"] = jnp.where(ha.incl, u["mb"][c:], 0.0)
            u["m_rk"] = jnp.where(ha.incl, u["mk"][c:], 0.0)
        for u, t_inv in zip(units, ha.inv_unit_lower_many([-u["m_ab"] for u in units])):
            u["t_inv"] = t_inv
        for u in units:
            u["makv"] = ha.nn(u["m_ak"], u["bd_v"], BF16)
        for u in units:
            u["a_hat"] = ha.nn(u["t_inv"], ha.bd(u["a"]), BF16)
            u["u1"] = ha.nn(u["t_inv"], ha.bd(u["makv"]), BF16)
        for u in units:
            u["r_hat"] = u["r"] + ha.nn(u["m_rb"], ha.bd(u["a_hat"]))
            u["o1"] = ha.nn(u["m_rb"], ha.bd(u["u1"])) + ha.nn(u["m_rk"], u["bd_v"])
            u["g_t"] = ha.diag_bd(u["g"]) + ha.tn_bd(u["bo"], u["a_hat"])
            u["h_t"] = ha.tn_bd(jnp.concatenate([u["bo"], u["ko"]], axis=0),
                                jnp.concatenate([u["u1"], u["v"].astype(BF16)], axis=0))
            zz = lax.dot_general(jnp.concatenate([u["r_hat"], u["g_t"]], axis=0).astype(BF16), state.astype(BF16),
                                 _NN, preferred_element_type=F32)
            o_rows.append(zz[:c] + u["o1"])
            state = zz[c:] + u["h_t"]
    o = o_rows[0] if len(o_rows) == 1 else jnp.concatenate(o_rows, axis=0)
    s_scr[...] = state
    o_ref[...] = _rwkv_finish(o, r, k, v, z, rk_ref[...], lng_ref[...], lnb_ref[...], ones)

    @pl.when(j == last)
    def _():
        eye_h = (_iota((HEAD_DIM, HEAD_DIM), 0) == _iota((HEAD_DIM, HEAD_DIM), 1)).astype(F32)
        for h in range(N_HEADS):
            hs = slice(h * HEAD_DIM, (h + 1) * HEAD_DIM)
            s_out_ref[h] = _mm_nt(eye_h, state[hs, hs])
        shift_out_ref[...] = pa[ct - 1:ct, :]


def _row(a):
    return a.reshape(1, -1)


def _rwkv_call(p_a, n_seq, seq_len, lp):
    ct = min(ROW_TILE, seq_len)
    p3 = p_a.reshape(n_seq, seq_len, A_W)
    params = [_row(lp["rwkv_mu"]), _row(lp["rwkv_w0"]), lp["rwkv_w2"], _row(lp["rwkv_a0"]), lp["rwkv_a2"],
              _row(lp["rwkv_k_k"]), _row(lp["rwkv_k_a"]), _row(lp["rwkv_r_k"]), _row(lp["rwkv_ln_g"]),
              _row(lp["rwkv_ln_b"])]
    o, s1, shift1 = pl.pallas_call(
        _rwkv_kernel,
        grid=(n_seq, seq_len // ct),
        in_specs=[pl.BlockSpec((None, ct, A_W), lambda b, j: (b, j, 0))]
        + [pl.BlockSpec(a.shape, lambda b, j: (0, 0)) for a in params],
        out_specs=[
            pl.BlockSpec((None, ct, W_MIX), lambda b, j: (b, j, 0)),
            pl.BlockSpec((None, N_HEADS, HEAD_DIM, HEAD_DIM), lambda b, j: (b, 0, 0, 0)),
            pl.BlockSpec((None, 1, A_SHIFT_W), lambda b, j: (b, 0, 0)),
        ],
        out_shape=[
            jax.ShapeDtypeStruct((n_seq, seq_len, W_MIX), F32),
            jax.ShapeDtypeStruct((n_seq, N_HEADS, HEAD_DIM, HEAD_DIM), F32),
            jax.ShapeDtypeStruct((n_seq, 1, A_SHIFT_W), F32),
        ],
        scratch_shapes=[pltpu.VMEM((W_MIX, W_MIX), F32), pltpu.VMEM((ct + SUBLANES, A_SHIFT_W), F32)],
        compiler_params=pltpu.CompilerParams(dimension_semantics=("parallel", "arbitrary"),
                                             vmem_limit_bytes=VMEM_LIMIT),
        name="rwkv7_prompt",
    )(p3, *params)
    return o.reshape(n_seq * seq_len, W_MIX), s1, shift1.reshape(n_seq, A_SHIFT_W)


def _ret_kernel(p_ref, cos_ref, sin_ref, o_ref, s_out_ref, s_scr):
    j = pl.program_id(1)
    last = pl.num_programs(1) - 1
    ct = p_ref.shape[0]

    @pl.when(j == 0)
    def _():
        s_scr[...] = jnp.zeros(s_scr.shape, F32)

    p = p_ref[...]
    cos, sin = cos_ref[...], sin_ref[...]
    q = _rotary(p[:, 0:W_MIX], cos, sin)
    k = _rotary(p[:, W_MIX:2 * W_MIX], cos, sin) * (HEAD_DIM ** -0.5)
    v = p[:, 2 * W_MIX:3 * W_MIX]
    z = p[:, 3 * W_MIX:]
    c = min(RET_CHUNK, ct)
    ri, ci = _iota((c, c), 0), _iota((c, c), 1)
    causal = ri >= ci
    rel = jnp.where(causal, ri - ci, 0).astype(F32)
    idx = _iota((c, 1), 0).astype(F32)
    states = [s_scr[h] for h in range(N_HEADS)]
    units = []
    for h in range(N_HEADS):
        lg = math.log(1.0 - 2.0 ** (-5.0 - h))
        consts = dict(decay=jnp.where(causal, jnp.exp(lg * rel), 0.0), q_dec=jnp.exp(lg * (idx + 1.0)),
                      k_dec=jnp.exp(lg * (c - 1.0 - idx)), g_c=math.exp(lg * c))
        hs = slice(h * HEAD_DIM, (h + 1) * HEAD_DIM)
        for c0 in range(0, ct, c):
            sl = slice(c0, c0 + c)
            units.append(dict(consts, h=h, q=_Split(q[sl, hs], 1), k=k[sl, hs], v=_Split(v[sl, hs], 1)))
    for u in units:
        u["s_in"] = _dotp(u["q"], u["k"], _NT, P_MISC) * u["decay"]
        u["kv"] = _dotp(u["k"] * u["k_dec"], u["v"], _TN, P_MISC)
    for u in units:
        u["o"] = _dotp(u["s_in"], u["v"], _NN, P_MISC)
    for u in units:
        u["s0"] = states[u["h"]]
        states[u["h"]] = u["s0"] * u["g_c"] + u["kv"]
    o_heads_all = [[] for _ in range(N_HEADS)]
    for u in units:
        o_heads_all[u["h"]].append(u["o"] + _dotp(u["q"], u["s0"], _NN, P_MISC) * u["q_dec"])
    cols = [oh[0] if len(oh) == 1 else jnp.concatenate(oh, axis=0) for oh in o_heads_all]
    o = jnp.concatenate(cols, axis=1)
    for h in range(N_HEADS):
        s_scr[h] = states[h]
    o_ref[...] = _head_rms_finish(o, z, _head_ones())

    @pl.when(j == last)
    def _():
        for h in range(N_HEADS):
            s_out_ref[h] = states[h]


def _rope_tables(pos):
    half = HEAD_DIM // 2
    inv = ROPE_BASE ** (-jnp.arange(half, dtype=F32) / half)
    ang = pos.astype(F32)[:, None] * inv[None, :]
    cos, sin = jnp.cos(ang), jnp.sin(ang)
    cos_t = jnp.tile(jnp.concatenate([cos, cos], axis=-1), (1, N_HEADS))
    sin_t = jnp.tile(jnp.concatenate([-sin, sin], axis=-1), (1, N_HEADS))
    return cos_t, sin_t


def _ret_call(p_b, n_seq, seq_len, cos_t, sin_t):
    ct = min(ROW_TILE, seq_len)
    p3 = p_b.reshape(n_seq, seq_len, B_W)
    o, s1 = pl.pallas_call(
        _ret_kernel,
        grid=(n_seq, seq_len // ct),
        in_specs=[
            pl.BlockSpec((None, ct, B_W), lambda b, j: (b, j, 0)),
            pl.BlockSpec((ct, W_MIX), lambda b, j: (j, 0)),
            pl.BlockSpec((ct, W_MIX), lambda b, j: (j, 0)),
        ],
        out_specs=[
            pl.BlockSpec((None, ct, W_MIX), lambda b, j: (b, j, 0)),
            pl.BlockSpec((None, N_HEADS, HEAD_DIM, HEAD_DIM), lambda b, j: (b, 0, 0, 0)),
        ],
        out_shape=[
            jax.ShapeDtypeStruct((n_seq, seq_len, W_MIX), F32),
            jax.ShapeDtypeStruct((n_seq, N_HEADS, HEAD_DIM, HEAD_DIM), F32),
        ],
        scratch_shapes=[pltpu.VMEM((N_HEADS, HEAD_DIM, HEAD_DIM), F32)],
        compiler_params=pltpu.CompilerParams(dimension_semantics=("parallel", "arbitrary"),
                                             vmem_limit_bytes=VMEM_LIMIT),
        name="retention_prompt",
    )(p3, cos_t, sin_t)
    return o.reshape(n_seq * seq_len, W_MIX), s1


def _affine_scan(a, b, span):
    n, w = a.shape
    if span == SUBLANES and n > span:
        shape, axis = (n // span, span, w), 1
        a, b = a.reshape(shape), b.reshape(shape)
        pos = _iota((1, span, 1), 1)
    else:
        assert span >= n
        axis = 0
        pos = _iota((n, 1), 0)
    dist = 1
    while dist < span:
        keep = pos >= dist
        a_prev = jnp.where(keep, pltpu.roll(a, dist, axis=axis), 1.0)
        b_prev = jnp.where(keep, pltpu.roll(b, dist, axis=axis), 0.0)
        b = a * b_prev + b
        a = a * a_prev
        dist *= 2
    return a.reshape(n, w), b.reshape(n, w)


def _lru_kernel(p_ref, cw_ref, cb_ref, gw_ref, gb_ref, sp_ref, o_ref, h_out_ref, conv_out_ref,
                ext_scr, h_scr, ab_scr, hin_scr):
    j = pl.program_id(1)
    last = pl.num_programs(1) - 1
    ct = p_ref.shape[0]

    @pl.when(j == 0)
    def _():
        h_scr[...] = jnp.zeros(h_scr.shape, F32)

    p = p_ref[...]
    xr = p[:, :W_MIX]
    z = p[:, W_MIX:]
    xc = _conv_tile(xr, ext_scr, cw_ref, j == 0) + cb_ref[...]
    a, b = _lru_token_math(xc, gw_ref[...], gb_ref[...], sp_ref[...])
    n_grp = ct // SUBLANES
    a, b = _affine_scan(a, b, SUBLANES)
    n_tiles = W_MIX // LANES
    for t in range(n_tiles):
        ab_scr[t] = a[:, t * LANES:(t + 1) * LANES]
        ab_scr[n_tiles + t] = b[:, t * LANES:(t + 1) * LANES]
    ends = pl.ds(SUBLANES - 1, n_grp, stride=SUBLANES)
    a_end = jnp.concatenate([ab_scr[t, ends, :] for t in range(n_tiles)], axis=1)
    b_end = jnp.concatenate([ab_scr[n_tiles + t, ends, :] for t in range(n_tiles)], axis=1)
    a_end, b_end = _affine_scan(a_end, b_end, n_grp)
    h_prev = h_scr[...]
    h_end = a_end * h_prev + b_end
    grp = _iota((n_grp, 1), 0)
    hin_scr[...] = jnp.where(grp == 0, h_prev, pltpu.roll(h_end, 1, axis=0))
    h_in = jnp.concatenate([jnp.broadcast_to(hin_scr[g:g + 1, :], (SUBLANES, W_MIX)) for g in range(n_grp)], axis=0)
    hcur = a * h_in + b
    h_scr[...] = h_end[n_grp - 1:n_grp, :]
    o_ref[...] = hcur * _silu(z)

    @pl.when(j == last)
    def _():
        h_out_ref[...] = hcur[ct - 1:ct, :]
        conv_out_ref[...] = xr[ct - SUBLANES:ct, :]


def _block_diag_gates(gate_w):
    out = jnp.zeros((W_MIX, 2 * W_MIX), F32)
    for g in range(2):
        for n in range(N_HEADS):
            out = out.at[n * HEAD_DIM:(n + 1) * HEAD_DIM,
                         g * W_MIX + n * HEAD_DIM:g * W_MIX + (n + 1) * HEAD_DIM].set(gate_w[g, n])
    return out


def _lru_params(lp):
    return [lp["lru_conv_w"], _row(lp["lru_conv_b"]), _block_diag_gates(lp["lru_gate_w"]),
            _row(lp["lru_gate_b"]), _row(lp["lru_lambda"])]


def _lru_call(p_c, n_seq, seq_len, lp):
    ct = min(LRU_TILE, seq_len)
    p3 = p_c.reshape(n_seq, seq_len, C_W)
    params = _lru_params(lp)
    o, h1, conv_tail = pl.pallas_call(
        _lru_kernel,
        grid=(n_seq, seq_len // ct),
        in_specs=[pl.BlockSpec((None, ct, C_W), lambda b, j: (b, j, 0))]
        + [pl.BlockSpec(a.shape, lambda b, j: (0, 0)) for a in params],
        out_specs=[
            pl.BlockSpec((None, ct, W_MIX), lambda b, j: (b, j, 0)),
            pl.BlockSpec((None, 1, W_MIX), lambda b, j: (b, 0, 0)),
            pl.BlockSpec((None, SUBLANES, W_MIX), lambda b, j: (b, 0, 0)),
        ],
        out_shape=[
            jax.ShapeDtypeStruct((n_seq, seq_len, W_MIX), F32),
            jax.ShapeDtypeStruct((n_seq, 1, W_MIX), F32),
            jax.ShapeDtypeStruct((n_seq, SUBLANES, W_MIX), F32),
        ],
        scratch_shapes=[pltpu.VMEM((ct + SUBLANES, W_MIX), F32), pltpu.VMEM((1, W_MIX), F32),
                        pltpu.VMEM((2 * W_MIX // LANES, ct, LANES), F32),
                        pltpu.VMEM((ct // SUBLANES, W_MIX), F32)],
        compiler_params=pltpu.CompilerParams(dimension_semantics=("parallel", "arbitrary"),
                                             vmem_limit_bytes=VMEM_LIMIT),
        name="rglru_prompt",
    )(p3, *params)
    return (o.reshape(n_seq * seq_len, W_MIX), h1.reshape(n_seq, W_MIX),
            conv_tail[:, SUBLANES - (CONV_W - 1):, :])


def _gdn_kernel(p_ref, cw_ref, nal_ref, dtb_ref, ng_ref, o_ref, s_out_ref, conv_out_ref, ext_scr, s_scr):
    j = pl.program_id(1)
    last = pl.num_programs(1) - 1
    ct = p_ref.shape[0]

    @pl.when(j == 0)
    def _():
        s_scr[...] = jnp.zeros(s_scr.shape, F32)

    p = p_ref[...]
    raw = p[:, :D_QKV_W]
    z = p[:, D_QKV_W:D_QKV_W + W_MIX]
    b_raw = p[:, D_QKV_W + W_MIX:D_QKV_W + 2 * W_MIX]
    a_raw = p[:, D_QKV_W + 2 * W_MIX:]
    ones = _head_ones()
    qkv = _conv_tile(raw, ext_scr, cw_ref, j == 0)
    q, k, v, beta, g = _gdn_token_math(qkv, b_raw, a_raw, nal_ref[...], dtb_ref[...], ones)

    c = min(CHUNK, ct)
    ha = _HeadAlgebra(c)
    lt = (_iota((c, c), 0) >= _iota((c, c), 1)).astype(BF16)
    units = []
    for c0 in range(0, ct, c):
        sl = slice(c0, c0 + c)
        gc = _dot_const(g[sl], lt, _NN, CUMSUM_PIECES, const_left=True)
        gc_cols = jnp.sum(gc * ha.eye, axis=0, keepdims=True)
        diff = gc - gc_cols
        decay = jnp.where(ha.incl, jnp.exp(jnp.where(ha.incl, diff, 0.0)), 0.0)
        kb = k[sl] * beta[sl]
        e_gc = jnp.exp(gc)
        g_last = gc[c - 1:c, :]
        units.append(dict(decay=decay, kb=kb, q=q[sl], k=k[sl], vb=v[sl] * beta[sl], kbe=kb * e_gc,
                          k_out=k[sl] * jnp.exp(g_last - gc), q_in=q[sl] * e_gc, e_last=jnp.exp(g_last)))
    all_units = units
    state = s_scr[...]
    o_rows = []
    for w0 in range(0, len(all_units), WAVE):
        units = all_units[w0:w0 + WAVE]
        for u in units:
            kq = ha.nt(jnp.concatenate([u["kb"], u["q"]], axis=0), ha.bd(u["k"]))
            u["a_mat"] = jnp.where(ha.strict, kq[:c] * u["decay"], 0.0)
            u["qk"] = kq[c:] * u["decay"]
        for u, t_inv in zip(units, ha.inv_unit_lower_many([u["a_mat"] for u in units])):
            u["t_inv"] = t_inv
        for u in units:
            u["u"] = ha.nn(u["t_inv"], ha.bd(u["vb"]), BF16)
            u["w"] = ha.nn(u["t_inv"], ha.bd(u["kbe"]), BF16)
        for u in units:
            u["g_mat"] = ha.diag_bd(u["e_last"]) - ha.tn_bd(u["k_out"], u["w"])
            u["h_mat"] = ha.tn_bd(u["k_out"], u["u"])
            u["q_hat"] = u["q_in"] - ha.nn(u["qk"], ha.bd(u["w"]))
            u["o1"] = ha.nn(u["qk"], ha.bd(u["u"]))
            zz = lax.dot_general(jnp.concatenate([u["q_hat"], u["g_mat"]], axis=0).astype(BF16),
                                 state.astype(BF16), _NN, preferred_element_type=F32)
            o_rows.append(zz[:c] + u["o1"])
            state = zz[c:] + u["h_mat"]
    o = o_rows[0] if len(o_rows) == 1 else jnp.concatenate(o_rows, axis=0)
    s_scr[...] = state
    o_ref[...] = _head_rms_finish(o, z, ones, ng_ref[...])

    @pl.when(j == last)
    def _():
        for h in range(N_HEADS):
            hs = slice(h * HEAD_DIM, (h + 1) * HEAD_DIM)
            s_out_ref[h] = state[hs, hs]
        conv_out_ref[...] = raw[ct - SUBLANES:ct, :]


def _gdn_params(lp):
    return [lp["gdn_conv_w"], _row(jnp.repeat(lp["gdn_A_log"], HEAD_DIM)),
            _row(jnp.repeat(lp["gdn_dt_bias"], HEAD_DIM)), _row(jnp.tile(lp["gdn_norm_g"], N_HEADS))]


def _gdn_call(p_d, n_seq, seq_len, lp):
    ct = min(ROW_TILE, seq_len)
    p3 = p_d.reshape(n_seq, seq_len, D_PACK_W)
    params = _gdn_params(lp)
    o, s1, conv_tail = pl.pallas_call(
        _gdn_kernel,
        grid=(n_seq, seq_len // ct),
        in_specs=[pl.BlockSpec((None, ct, D_PACK_W), lambda b, j: (b, j, 0))]
        + [pl.BlockSpec(a.shape, lambda b, j: (0, 0)) for a in params],
        out_specs=[
            pl.BlockSpec((None, ct, W_MIX), lambda b, j: (b, j, 0)),
            pl.BlockSpec((None, N_HEADS, HEAD_DIM, HEAD_DIM), lambda b, j: (b, 0, 0, 0)),
            pl.BlockSpec((None, SUBLANES, D_QKV_W), lambda b, j: (b, 0, 0)),
        ],
        out_shape=[
            jax.ShapeDtypeStruct((n_seq, seq_len, W_MIX), F32),
            jax.ShapeDtypeStruct((n_seq, N_HEADS, HEAD_DIM, HEAD_DIM), F32),
            jax.ShapeDtypeStruct((n_seq, SUBLANES, D_QKV_W), F32),
        ],
        scratch_shapes=[pltpu.VMEM((ct + SUBLANES, D_QKV_W), F32), pltpu.VMEM((W_MIX, W_MIX), F32)],
        compiler_params=pltpu.CompilerParams(dimension_semantics=("parallel", "arbitrary"),
                                             vmem_limit_bytes=VMEM_LIMIT),
        name="gdn_prompt",
    )(p3, *params)
    return o.reshape(n_seq * seq_len, W_MIX), s1, conv_tail[:, SUBLANES - (CONV_W - 1):, :]


def _decode_pre_kernel(pa_ref, pb_ref, pc_ref, pd_ref, shift_ref, h0_ref, lconv_ref, gconv_ref, cos_ref, sin_ref,
                       mu_ref, w0_ref, w2_ref, a0_ref, a2_ref, kk_ref, ka_ref,
                       lcw_ref, lcb_ref, lgw_ref, lgb_ref, lsp_ref, gcw_ref, nal_ref, dtb_ref,
                       vt_ref, vn_ref, oc_ref, h1_ref, lconv1_ref, gconv1_ref):
    ones = _head_ones()
    pa_full = pa_ref[...]
    pa = pa_full[:, :A_SHIFT_W]
    pm = pa + (shift_ref[...] - pa) * mu_ref[...]
    r, k, v, ld, av, bv = _rwkv_token_math(pm, w0_ref[...], w2_ref[...], a0_ref[...], a2_ref[...],
                                           kk_ref[...], ka_ref[...], ones)
    vecs = [r, jnp.exp(ld), k, v, av, bv]
    plain = [r, k, v, pa_full[:, A_SHIFT_W:]]
    pb = pb_ref[...]
    cos, sin = cos_ref[...], sin_ref[...]
    vecs += [_rotary(pb[:, 0:W_MIX], cos, sin), _rotary(pb[:, W_MIX:2 * W_MIX], cos, sin) * (HEAD_DIM ** -0.5),
             pb[:, 2 * W_MIX:3 * W_MIX]]
    plain.append(pb[:, 3 * W_MIX:])
    pc = pc_ref[...]
    xr = pc[:, :W_MIX]
    taps = [lconv_ref[i] for i in range(CONV_W - 1)] + [xr]
    xc = taps[0] * lcw_ref[0:1, :]
    for i in range(1, CONV_W):
        xc = xc + taps[i] * lcw_ref[i:i + 1, :]
    xc = xc + lcb_ref[...]
    a, b = _lru_token_math(xc, lgw_ref[...], lgb_ref[...], lsp_ref[...])
    hcur = a * h0_ref[...] + b
    oc_ref[...] = hcur * _silu(pc[:, W_MIX:])
    h1_ref[...] = hcur
    for i in range(CONV_W - 1):
        lconv1_ref[i] = taps[i + 1]
    pd = pd_ref[...]
    raw = pd[:, :D_QKV_W]
    gtaps = [gconv_ref[i] for i in range(CONV_W - 1)] + [raw]
    qkv = gtaps[0] * gcw_ref[0:1, :]
    for i in range(1, CONV_W):
        qkv = qkv + gtaps[i] * gcw_ref[i:i + 1, :]
    q, kg, vg, beta, g = _gdn_token_math(qkv, pd[:, D_QKV_W + W_MIX:D_QKV_W + 2 * W_MIX],
                                         pd[:, D_QKV_W + 2 * W_MIX:], nal_ref[...], dtb_ref[...], ones)
    for i in range(CONV_W - 1):
        gconv1_ref[i] = gtaps[i + 1]
    vecs += [q, kg, vg, beta, g]
    plain.append(pd[:, D_QKV_W:D_QKV_W + W_MIX])
    assert len(vecs) == N_VEC_T and len(plain) == N_VEC_PLAIN
    for i, vec in enumerate(vecs):
        vt_ref[i] = vec.T
    for i, vec in enumerate(plain):
        vn_ref[i] = vec


def _decode_state_kernel(vt_ref, wkv_ref, ret_ref, gdn_ref, gam_ref, wkv1_ref, ret1_ref, gdn1_ref, o_ref):
    v_r, v_w, v_k, v_v, v_a, v_b, r_q, r_k, r_v, g_q, g_k, g_v, g_beta, g_g = range(N_VEC_T)
    hd = HEAD_DIM
    n = vt_ref.shape[-1]
    row = lambda idx, i: vt_ref[idx, pl.ds(i, 1), :]
    rows_of = lambda i: pl.ds(pl.multiple_of(i * hd, hd), hd)
    gamma = gam_ref[...]
    beta = vt_ref[g_beta, 0:1, :]
    eg = jnp.exp(vt_ref[g_g, 0:1, :])

    def first_pass(i, carry):
        acc_ret, acc_w, acc_q = carry
        rows = rows_of(i)
        s = wkv_ref[rows, :]
        sa = jnp.sum(s * vt_ref[v_a], axis=0, keepdims=True)
        s = s * vt_ref[v_w] + sa * vt_ref[v_b] + row(v_v, i) * vt_ref[v_k]
        wkv1_ref[rows, :] = s
        o_ref[0, pl.ds(i, 1), :] = jnp.sum(s * vt_ref[v_r], axis=0, keepdims=True)
        s = ret_ref[rows, :] * gamma + row(r_k, i) * vt_ref[r_v]
        ret1_ref[rows, :] = s
        acc_ret = acc_ret + row(r_q, i) * s
        s = gdn_ref[rows, :]
        return acc_ret, acc_w + row(g_k, i) * s, acc_q + row(g_q, i) * s

    zeros = jnp.zeros((hd, n), F32)
    acc_ret, acc_w, acc_q = lax.fori_loop(0, hd, first_pass, (zeros, zeros, zeros))
    o_ref[1] = acc_ret
    v_new = vt_ref[g_v] * beta - acc_w * (beta * eg)
    qk = jnp.sum(vt_ref[g_q] * vt_ref[g_k], axis=0, keepdims=True)
    o_ref[2] = acc_q * eg + qk * v_new

    def second_pass(i, carry):
        rows = rows_of(i)
        gdn1_ref[rows, :] = gdn_ref[rows, :] * eg + row(g_k, i) * v_new
        return carry

    lax.fori_loop(0, hd, second_pass, 0)


def _decode_finish_kernel(ot_ref, vn_ref, rk_ref, lng_ref, lnb_ref, ng_ref, oa_ref, ob_ref, od_ref):
    ones = _head_ones()
    r, k, v, z_a, z_b, z_d = (vn_ref[i] for i in range(N_VEC_PLAIN))
    oa_ref[...] = _rwkv_finish(ot_ref[0].T, r, k, v, z_a, rk_ref[...], lng_ref[...], lnb_ref[...], ones)
    ob_ref[...] = _head_rms_finish(ot_ref[1].T, z_b, ones)
    od_ref[...] = _head_rms_finish(ot_ref[2].T, z_d, ones, ng_ref[...])


def _batch_minor(state):
    n_layers, n = state.shape[:2]
    return jnp.transpose(state, (0, 2, 3, 4, 1)).reshape(n_layers, -1, n)


def _batch_major(flat_state):
    n = flat_state.shape[-1]
    return jnp.transpose(flat_state.reshape(N_HEADS, HEAD_DIM, HEAD_DIM, n), (3, 0, 1, 2))


def _decode_layer(l, p_a, p_b, p_c, p_d, carried, cos_t, sin_t, lp):
    n = p_a.shape[0]
    assert n % LANES == 0, "the decode state kernel keeps the batch on lanes"
    flat = HEAD_DIM * HEAD_DIM
    taps = CONV_W - 1
    rwkv_params = [_row(lp["rwkv_mu"]), _row(lp["rwkv_w0"]), lp["rwkv_w2"], _row(lp["rwkv_a0"]), lp["rwkv_a2"],
                   _row(lp["rwkv_k_k"]), _row(lp["rwkv_k_a"])]
    gdn_params = _gdn_params(lp)
    full = lambda a: pl.BlockSpec(a.shape, lambda i: (0,) * a.ndim)
    layer_blk = lambda a: pl.BlockSpec((None,) + a.shape[1:], lambda i, nd=a.ndim: (l,) + (0,) * (nd - 1))
    projs = [p_a, p_b, p_c, p_d]
    layered = [carried["shift"], carried["lru_h"], carried["lru_conv"], carried["gdn_conv"]]
    consts = [cos_t, sin_t, *rwkv_params, *_lru_params(lp), *gdn_params[:3]]
    out_shapes = [
        jax.ShapeDtypeStruct((N_VEC_T, W_MIX, n), F32),
        jax.ShapeDtypeStruct((N_VEC_PLAIN, n, W_MIX), F32),
        jax.ShapeDtypeStruct((n, W_MIX), F32),
        jax.ShapeDtypeStruct((n, W_MIX), F32),
        jax.ShapeDtypeStruct((taps, n, W_MIX), F32),
        jax.ShapeDtypeStruct((taps, n, D_QKV_W), F32),
    ]
    vec_t, vec_n, o_c, lru_h1, lru_conv1, gdn_conv1 = pl.pallas_call(
        _decode_pre_kernel,
        grid=(1,),
        in_specs=[full(a) for a in projs] + [layer_blk(a) for a in layered] + [full(a) for a in consts],
        out_specs=[pl.BlockSpec(s.shape, lambda i, nd=len(s.shape): (0,) * nd) for s in out_shapes],
        out_shape=out_shapes,
        compiler_params=pltpu.CompilerParams(dimension_semantics=("arbitrary",), vmem_limit_bytes=VMEM_LIMIT),
        name="decode_tokens",
    )(*projs, *layered, *consts)

    gam = jnp.broadcast_to((1.0 - 2.0 ** (-5.0 - jnp.arange(N_HEADS, dtype=F32)))[:, None, None], (N_HEADS, 1, n))
    state_in = pl.BlockSpec((None, flat, n), lambda h: (l, h, 0))
    state_out = pl.BlockSpec((flat, n), lambda h: (h, 0))
    wkv1, ret1, gdn1, o_t = pl.pallas_call(
        _decode_state_kernel,
        grid=(N_HEADS,),
        in_specs=[pl.BlockSpec((N_VEC_T, HEAD_DIM, n), lambda h: (0, h, 0)), state_in, state_in, state_in,
                  pl.BlockSpec((None, 1, n), lambda h: (h, 0, 0))],
        out_specs=[state_out, state_out, state_out, pl.BlockSpec((3, HEAD_DIM, n), lambda h: (0, h, 0))],
        out_shape=[jax.ShapeDtypeStruct((N_HEADS * flat, n), F32)] * 3 + [jax.ShapeDtypeStruct((3, W_MIX, n), F32)],
        compiler_params=pltpu.CompilerParams(dimension_semantics=("parallel",), vmem_limit_bytes=VMEM_LIMIT),
        name="decode_states",
    )(vec_t, carried["wkv"], carried["ret"], carried["gdn"], gam)

    finish_ins = [o_t, vec_n, _row(lp["rwkv_r_k"]), _row(lp["rwkv_ln_g"]), _row(lp["rwkv_ln_b"]), gdn_params[3]]
    o_a, o_b, o_d = pl.pallas_call(
        _decode_finish_kernel,
        grid=(1,),
        in_specs=[full(a) for a in finish_ins],
        out_specs=[pl.BlockSpec((n, W_MIX), lambda i: (0, 0))] * 3,
        out_shape=[jax.ShapeDtypeStruct((n, W_MIX), F32)] * 3,
        compiler_params=pltpu.CompilerParams(dimension_semantics=("arbitrary",), vmem_limit_bytes=VMEM_LIMIT),
        name="decode_finish",
    )(*finish_ins)
    new_states = (_batch_major(wkv1), p_a[:, :A_SHIFT_W], _batch_major(ret1), lru_h1,
                  jnp.transpose(lru_conv1, (1, 0, 2)), _batch_major(gdn1), jnp.transpose(gdn_conv1, (1, 0, 2)))
    return (o_a, o_b, o_c, o_d), new_states


def _pack_in_weights(w_t):
    off_d = A_W + B_W + C_W
    off_ba = off_d + D_QKV_W
    off_z = off_ba + 2 * N_HEADS
    off_g = off_d + D_W
    w_pack = jnp.concatenate([
        w_t[:off_ba],
        w_t[off_z:off_g],
        jnp.repeat(w_t[off_ba:off_ba + N_HEADS], HEAD_DIM, axis=0),
        jnp.repeat(w_t[off_ba + N_HEADS:off_z], HEAD_DIM, axis=0),
    ], axis=0)
    return w_pack, w_t[off_g:]


def _prompt_layer(p_a, p_b, p_c, p_d, n_seq, seq_len, cos_t, sin_t, lp):
    o_a, wkv1, shift1 = _rwkv_call(p_a, n_seq, seq_len, lp)
    o_b, ret1 = _ret_call(p_b, n_seq, seq_len, cos_t, sin_t)
    o_c, lru_h1, lru_conv1 = _lru_call(p_c, n_seq, seq_len, lp)
    o_d, gdn1, gdn_conv1 = _gdn_call(p_d, n_seq, seq_len, lp)
    return (o_a, o_b, o_c, o_d), (wkv1, shift1, ret1, lru_h1, lru_conv1, gdn1, gdn_conv1)


def _run_group(x, mods, pos, carried, layers, final_g):
    n_seq, seq_len, d = x.shape
    x2 = x.reshape(n_seq * seq_len, d)
    cos_t, sin_t = _rope_tables(pos)
    new = []
    n_layers = len(layers)
    for l, lp in enumerate(layers):
        shift, scale, gate = (mods[l][:, i * d:(i + 1) * d] for i in range(3))
        p_a, p_b, p_c, p_d = _inproj_call(x2, scale, shift, lp["norm_g"], lp["w_pack"], seq_len)
        if carried is None:
            branches, st = _prompt_layer(p_a, p_b, p_c, p_d, n_seq, seq_len, cos_t, sin_t, lp)
        else:
            branches, st = _decode_layer(l, p_a, p_b, p_c, p_d, carried, cos_t, sin_t, lp)
        new.append(st)
        x2 = _outproj_call(x2, scale, shift, gate, lp["norm_g"], branches, lp["w_gate"], lp["w_up_bf16"],
                           lp["w_out_bf16"], final_g, seq_len, final=(l == n_layers - 1))
    stacked = tuple(jnp.stack([s[i] for s in new], axis=0) for i in range(7))
    return x2.reshape(n_seq, seq_len, d), stacked


def kernel(x_prompt, x_sample, c_prompt, c_sample, state_rwkv_wkv, state_rwkv_shift, state_ret, state_lru_h, state_lru_conv, state_gdn, state_gdn_conv, ada_w, ada_b, norm_g, w_in, rwkv_mu, rwkv_w0, rwkv_w2, rwkv_a0, rwkv_a2, rwkv_k_k, rwkv_k_a, rwkv_r_k, rwkv_ln_g, rwkv_ln_b, lru_conv_w, lru_conv_b, lru_gate_w, lru_gate_b, lru_lambda, gdn_conv_w, gdn_A_log, gdn_dt_bias, gdn_norm_g, w_up, w_out, final_g):
    n_layers = ada_w.shape[0]
    n_prompt, seq_len, _ = x_prompt.shape
    n_sample, dec_len, _ = x_sample.shape
    assert dec_len == 1, "the decode path handles one token per sequence"
    layers = []
    w_in_t = jnp.swapaxes(w_in, 1, 2).astype(BF16)
    for l in range(n_layers):
        w_pack, w_gate = _pack_in_weights(w_in_t[l])
        layers.append(dict(
            norm_g=norm_g[l], w_pack=w_pack, w_gate=w_gate, w_up_bf16=w_up[l].astype(BF16),
            w_out_bf16=w_out[l].astype(BF16),
            rwkv_mu=rwkv_mu[l], rwkv_w0=rwkv_w0[l], rwkv_w2=rwkv_w2[l], rwkv_a0=rwkv_a0[l], rwkv_a2=rwkv_a2[l],
            rwkv_k_k=rwkv_k_k[l], rwkv_k_a=rwkv_k_a[l], rwkv_r_k=rwkv_r_k[l], rwkv_ln_g=rwkv_ln_g[l],
            rwkv_ln_b=rwkv_ln_b[l], lru_conv_w=lru_conv_w[l], lru_conv_b=lru_conv_b[l],
            lru_gate_w=lru_gate_w[l], lru_gate_b=lru_gate_b[l], lru_lambda=lru_lambda[l],
            gdn_conv_w=gdn_conv_w[l], gdn_A_log=gdn_A_log[l], gdn_dt_bias=gdn_dt_bias[l],
            gdn_norm_g=gdn_norm_g[l]))
    mods = _ada_call(jnp.concatenate([c_prompt, c_sample], axis=0), ada_w, ada_b)
    mods_p = [mods[l, :n_prompt] for l in range(n_layers)]
    mods_s = [mods[l, n_prompt:] for l in range(n_layers)]

    y_prompt, new_p = _run_group(x_prompt, mods_p, jnp.arange(seq_len, dtype=jnp.int32), None, layers, final_g)
    carried = dict(wkv=_batch_minor(state_rwkv_wkv), ret=_batch_minor(state_ret), gdn=_batch_minor(state_gdn),
                   shift=state_rwkv_shift, lru_h=state_lru_h,
                   lru_conv=jnp.transpose(state_lru_conv, (0, 2, 1, 3)),
                   gdn_conv=jnp.transpose(state_gdn_conv, (0, 2, 1, 3)))
    pos_s = PAST_LEN + jnp.arange(dec_len, dtype=jnp.int32)
    y_sample, new_s = _run_group(x_sample, mods_s, pos_s, carried, layers, final_g)
    return (y_prompt, y_sample) + new_p + new_s
```

```python
import functools
import math

import jax
import jax.numpy as jnp
from jax import lax
from jax.experimental import pallas as pl
from jax.experimental.pallas import tpu as pltpu

F32 = jnp.float32
BF16 = jnp.bfloat16
HI = lax.Precision.HIGHEST

N_HEADS = 4
HEAD_DIM = 64
W_MIX = N_HEADS * HEAD_DIM
LORA = 64
CONV_W = 4
N_BRANCH = 4
LRU_C = 8.0
ROPE_BASE = 10000.0
EPS = 1e-6
RWKV_GN_EPS = 64e-5
PAST_LEN = 16384
A_SHIFT_W = 3 * W_MIX + 2 * LORA
A_W = A_SHIFT_W + W_MIX
B_W = 4 * W_MIX
C_W = 2 * W_MIX
D_QKV_W = 3 * W_MIX
D_W = D_QKV_W + 2 * N_HEADS + W_MIX
D_PACK_W = D_QKV_W + 3 * W_MIX

SUBLANES = 8
LANES = 128
VMEM_LIMIT = 56 * 1024 * 1024

CHUNK = 64
RET_CHUNK = 128
INV_BLOCK = 16
WAVE = 16
ROW_TILE = 1024
LRU_TILE = 512
PROJ_TILE = 512
OUT_TILE = 1024
N_VEC_T = 14
N_VEC_PLAIN = 6


def _mm(a, b, prec=HI):
    return lax.dot_general(a, b, (((1,), (0,)), ((), ())), precision=prec, preferred_element_type=F32)


def _mm_nt(a, b, prec=HI):
    return lax.dot_general(a, b, (((1,), (1,)), ((), ())), precision=prec, preferred_element_type=F32)


def _mm_tn(a, b, prec=HI):
    return lax.dot_general(a, b, (((0,), (0,)), ((), ())), precision=prec, preferred_element_type=F32)


_NN = (((1,), (0,)), ((), ()))
_NT = (((1,), (1,)), ((), ()))
_TN = (((0,), (0,)), ((), ()))

P_INV = 1
P_STATE = 1
P_MISC = 1
HEAD_SUM_PIECES = 1
CUMSUM_PIECES = 2


class _Split:
    def __init__(self, x, passes):
        self.hi = x.astype(BF16)
        self.lo = (x - self.hi.astype(F32)).astype(BF16) if passes > 1 else None


def _dotp(a, b, dims=_NN, passes=1):
    a = a if isinstance(a, _Split) else _Split(a, passes)
    b = b if isinstance(b, _Split) else _Split(b, passes)
    d = lambda x, y: lax.dot_general(x, y, dims, preferred_element_type=F32)
    out = d(a.hi, b.hi)
    if passes > 1:
        out = out + (d(a.hi, b.lo) + d(a.lo, b.hi))
    return out


def _iota(shape, dim):
    return lax.broadcasted_iota(jnp.int32, shape, dim)


def _silu(x):
    return x * jax.nn.sigmoid(x)


def _softplus(x):
    return jnp.maximum(x, 0.0) + jnp.log1p(jnp.exp(-jnp.abs(x)))


def _pieces(x, n):
    out = []
    for i in range(n):
        p = x.astype(BF16)
        out.append(p)
        if i + 1 < n:
            x = x - p.astype(F32)
    return out


def _dot_const(x, const, dims=_NN, n=2, const_left=False):
    out = None
    for p in _pieces(x, n):
        t = lax.dot_general(*((const, p) if const_left else (p, const)), dims, preferred_element_type=F32)
        out = t if out is None else out + t
    return out


def _head_ones():
    return (_iota((W_MIX, W_MIX), 0) // HEAD_DIM == _iota((W_MIX, W_MIX), 1) // HEAD_DIM).astype(BF16)


def _head_sum(x, ones, signed=False):
    return _dot_const(x, ones, n=HEAD_SUM_PIECES + (1 if signed else 0))


def _rms(x):
    return x * lax.rsqrt(jnp.mean(x * x, axis=-1, keepdims=True) + EPS)


def _inv_unit_lower(a):
    return _inv_unit_lower_many([a])[0]


def _inv_unit_lower_many(mats):
    n = mats[0].shape[0]
    ri, ci = _iota((n, n), 0), _iota((n, n), 1)
    eye = (ri == ci).astype(F32)
    diag_blk = (ri // INV_BLOCK) == (ci // INV_BLOCK)
    mm = lambda x, y: _dotp(x, y, _NN, P_INV)
    sp = lambda x: _Split(x, P_INV)
    d = [jnp.where(diag_blk, a, 0.0) for a in mats]
    nb = [a - di for a, di in zip(mats, d)]
    td = [eye - di for di in d]
    p = d
    for _ in range(int(math.log2(INV_BLOCK)) - 1):
        ps = [sp(pi) for pi in p]
        p = [mm(pi, pi) for pi in ps]
        td = [mm(ti, eye + pi) for ti, pi in zip(td, p)]
    tds = [sp(ti) for ti in td]
    x = [mm(ti, ni) for ti, ni in zip(tds, nb)]
    t = [eye - xi for xi in x]
    p = x
    for _ in range(int(math.log2(n // INV_BLOCK)) - 1):
        ps = [sp(pi) for pi in p]
        p = [mm(pi, pi) for pi in ps]
        t = [mm(ti, eye + pi) for ti, pi in zip(t, p)]
    return [mm(ti, tdi) for ti, tdi in zip(t, tds)]


class _HeadAlgebra:
    def __init__(self, c):
        assert c == HEAD_DIM, "side-by-side head products need CHUNK == HEAD_DIM"
        w = W_MIX
        row, lane = _iota((c, w), 0), _iota((c, w), 1)
        col = lane % HEAD_DIM
        tile_lane = _iota((c, LANES), 1)
        self.tile_head = [tile_lane // HEAD_DIM == h for h in range(LANES // HEAD_DIM)]
        self.eye = (row == col).astype(F32)
        self.strict = row > col
        self.incl = row >= col
        self.inv_blk = (row // INV_BLOCK) == (col // INV_BLOCK)
        r2, c2 = _iota((w, w), 0), _iota((w, w), 1)
        self.eye_full = r2 == c2
        self.same_head = (r2 // HEAD_DIM) == (c2 // HEAD_DIM)

    def bd(self, y):
        yb = y.astype(BF16)
        zero = jnp.zeros((yb.shape[0], LANES), BF16)
        blocks = []
        for t in range(W_MIX // LANES):
            tile = yb[:, t * LANES:(t + 1) * LANES]
            for m in self.tile_head:
                kept = jnp.where(m, tile, zero)
                blocks.append(jnp.concatenate([kept if s == t else zero for s in range(W_MIX // LANES)], axis=1))
        return jnp.concatenate(blocks, axis=0)

    def nn(self, x, bd_y, out=F32):
        return lax.dot_general(x.astype(BF16), bd_y, _NN, preferred_element_type=F32).astype(out)

    def nt(self, x, bd_y):
        return lax.dot_general(x.astype(BF16), bd_y, _NT, preferred_element_type=F32)

    def tn_bd(self, x, y):
        full = lax.dot_general(x.astype(BF16), y.astype(BF16), _TN, preferred_element_type=F32)
        return jnp.where(self.same_head, full, 0.0)

    def diag_bd(self, row_vec):
        return jnp.where(self.eye_full, row_vec, 0.0)

    def plus_eye(self, bd_p):
        return jnp.where(self.eye_full, jnp.ones_like(bd_p), bd_p)

    def inv_unit_lower_many(self, mats):
        mats = [a.astype(BF16) for a in mats]
        zero = jnp.zeros_like(mats[0])
        eye = self.eye.astype(BF16)
        d = [jnp.where(self.inv_blk, a, zero) for a in mats]
        nb = [jnp.where(self.inv_blk, zero, a) for a in mats]
        td = [eye - di for di in d]
        p = d
        bdp = [self.bd(pi) for pi in p]
        for _ in range(int(math.log2(INV_BLOCK)) - 1):
            p = [self.nn(pi, bi, BF16) for pi, bi in zip(p, bdp)]
            bdp = [self.bd(pi) for pi in p]
            td = [self.nn(ti, self.plus_eye(bi), BF16) for ti, bi in zip(td, bdp)]
        bd_td = [self.bd(ti) for ti in td]
        x = [self.nn(ti, self.bd(ni), BF16) for ti, ni in zip(td, nb)]
        t = [eye - xi for xi in x]
        p = x
        bdp = [self.bd(pi) for pi in p]
        for _ in range(int(math.log2(HEAD_DIM // INV_BLOCK)) - 1):
            p = [self.nn(pi, bi, BF16) for pi, bi in zip(p, bdp)]
            bdp = [self.bd(pi) for pi in p]
            t = [self.nn(ti, self.plus_eye(bi), BF16) for ti, bi in zip(t, bdp)]
        return [self.nn(ti, bi, BF16) for ti, bi in zip(t, bd_td)]


def _ada_kernel(c_ref, w_ref, b_ref, o_ref):
    o_ref[...] = _dotp(_silu(c_ref[...]), w_ref[...], _NN, 3) + b_ref[...]


def _ada_call(c_all, ada_w, ada_b):
    n_layers, d, d3 = ada_w.shape
    rows = c_all.shape[0]
    return pl.pallas_call(
        _ada_kernel,
        grid=(n_layers, d3 // d),
        in_specs=[
            pl.BlockSpec((rows, d), lambda l, j: (0, 0)),
            pl.BlockSpec((None, d, d), lambda l, j: (l, 0, j)),
            pl.BlockSpec((None, 1, d), lambda l, j: (l, 0, j)),
        ],
        out_specs=pl.BlockSpec((None, rows, d), lambda l, j: (l, 0, j)),
        out_shape=jax.ShapeDtypeStruct((n_layers, rows, d3), F32),
        compiler_params=pltpu.CompilerParams(dimension_semantics=("arbitrary", "arbitrary"),
                                             vmem_limit_bytes=VMEM_LIMIT),
        name="ada_mod",
    )(c_all, ada_w, ada_b.reshape(n_layers, 1, d3))


def _modulated_norm(x, g, scale, shift):
    return _rms(x) * g * (1.0 + scale) + shift


def _inproj_kernel(x_ref, sc_ref, sh_ref, g_ref, w_ref, wz_ref, wba_ref, oa_ref, ob_ref, oc_ref, od_ref):
    h = _modulated_norm(x_ref[...], g_ref[...], sc_ref[...], sh_ref[...]).astype(BF16)
    proj = lambda w: lax.dot_general(h, w, _NT, preferred_element_type=F32)
    lo = 0
    for o_ref in (oa_ref, ob_ref, oc_ref):
        wd = o_ref.shape[-1]
        o_ref[...] = proj(w_ref[0, lo:lo + wd, :])
        lo += wd
    od_ref[:, :D_QKV_W] = proj(w_ref[0, lo:lo + D_QKV_W, :])
    od_ref[:, D_QKV_W:D_QKV_W + W_MIX] = proj(wz_ref[0])
    od_ref[:, D_QKV_W + W_MIX:] = proj(wba_ref[...])


def _weight_rows(layer, row0, n_rows, d):
    return pl.BlockSpec((pl.Element(1), pl.Element(n_rows), pl.Element(d)), lambda i: (layer, row0, 0))


def _mod_specs(mods, tm, seq_len):
    d = mods[0].shape[-1]
    if seq_len == 1:
        return [m for m in mods], [pl.BlockSpec((tm, d), lambda i: (i, 0)) for _ in mods]
    per_seq = seq_len // tm
    return ([m.reshape(m.shape[0], 1, d) for m in mods],
            [pl.BlockSpec((None, 1, d), lambda i: (i // per_seq, 0, 0)) for _ in mods])


def _inproj_call(x2, scale, shift, g, l, w_t, w_ba, seq_len):
    m, d = x2.shape
    tm = min(PROJ_TILE, m, seq_len) if seq_len > 1 else m
    widths = (A_W, B_W, C_W, D_PACK_W)
    mods, mod_specs = _mod_specs((scale, shift), tm, seq_len)
    off_ba = A_W + B_W + C_W + D_QKV_W
    off_z = off_ba + 2 * N_HEADS
    return pl.pallas_call(
        _inproj_kernel,
        grid=(m // tm,),
        in_specs=[pl.BlockSpec((tm, d), lambda i: (i, 0))] + mod_specs + [
            pl.BlockSpec((1, d), lambda i: (0, 0)),
            _weight_rows(l, 0, off_ba, d),
            _weight_rows(l, off_z, W_MIX, d),
            pl.BlockSpec(w_ba.shape, lambda i: (0, 0)),
        ],
        out_specs=[pl.BlockSpec((tm, wd), lambda i: (i, 0)) for wd in widths],
        out_shape=[jax.ShapeDtypeStruct((m, wd), F32) for wd in widths],
        compiler_params=pltpu.CompilerParams(dimension_semantics=("parallel",), vmem_limit_bytes=VMEM_LIMIT),
        name="in_proj",
    )(x2, *mods, g.reshape(1, d), w_t, w_t, w_ba)


def _outproj_kernel(x_ref, sc_ref, sh_ref, gt_ref, g_ref, ba_ref, bb_ref, bc_ref, bd_ref,
                    wg_ref, wup_ref, wout_ref, fg_ref, o_ref, *, final):
    x = x_ref[...]
    d = x.shape[-1]
    h = _modulated_norm(x, g_ref[...], sc_ref[...], sh_ref[...]).astype(BF16)
    merged = jnp.zeros(x.shape, F32)
    for n, br_ref in enumerate((ba_ref, bb_ref, bc_ref, bd_ref)):
        gl = lax.dot_general(h, wg_ref[0, n * d:(n + 1) * d, :], _NT, preferred_element_type=F32)
        up = jnp.dot(br_ref[...].astype(BF16), wup_ref[n], preferred_element_type=F32)
        merged = merged + jax.nn.sigmoid(gl) * up
    out = jnp.dot(merged.astype(BF16), wout_ref[...], preferred_element_type=F32)
    xn = x + gt_ref[...] * out
    if final:
        xn = _rms(xn) * fg_ref[...]
    o_ref[...] = xn


def _outproj_call(x2, scale, shift, gate, g, branches, l, w_t, wup, wout, final_g, seq_len, final):
    m, d = x2.shape
    tm = min(OUT_TILE, m, seq_len) if seq_len > 1 else m
    mods, mod_specs = _mod_specs((scale, shift, gate), tm, seq_len)
    full = lambda a: pl.BlockSpec(a.shape, lambda i: (0,) * a.ndim)
    off_g = A_W + B_W + C_W + D_W
    gate_rows = _weight_rows(l, off_g, N_BRANCH * d, d)
    return pl.pallas_call(
        functools.partial(_outproj_kernel, final=final),
        grid=(m // tm,),
        in_specs=[pl.BlockSpec((tm, d), lambda i: (i, 0))] + mod_specs + [pl.BlockSpec((1, d), lambda i: (0, 0))]
        + [pl.BlockSpec((tm, W_MIX), lambda i: (i, 0)) for _ in branches]
        + [gate_rows, full(wup), full(wout), pl.BlockSpec((1, d), lambda i: (0, 0))],
        out_specs=pl.BlockSpec((tm, d), lambda i: (i, 0)),
        out_shape=jax.ShapeDtypeStruct((m, d), F32),
        compiler_params=pltpu.CompilerParams(dimension_semantics=("parallel",), vmem_limit_bytes=VMEM_LIMIT),
        name="out_proj",
    )(x2, *mods, g.reshape(1, d), *branches, w_t, wup, wout, final_g.reshape(1, d))


def _rwkv_token_math(pm, w0, w2, a0, a2, k_k, k_a, ones):
    r = pm[:, 0:W_MIX]
    k = pm[:, W_MIX:2 * W_MIX]
    v = pm[:, 2 * W_MIX:3 * W_MIX]
    wd = pm[:, 3 * W_MIX:3 * W_MIX + LORA]
    ad = pm[:, 3 * W_MIX + LORA:]
    w_log = -_softplus(-(w0 + _dotp(jnp.tanh(wd), w2, _NN, P_MISC))) - 0.5
    log_decay = -jnp.exp(w_log)
    a = jax.nn.sigmoid(a0 + _dotp(ad, a2, _NN, P_MISC))
    kx = k * k_k
    kk = kx * lax.rsqrt(_head_sum(kx * kx, ones) + EPS)
    k = k * (1.0 + (a - 1.0) * k_a)
    return r, k, v, log_decay, -kk, kk * a


def _rwkv_finish(o, r, k, v, z, r_k, ln_g, ln_b, ones):
    mean = _head_sum(o, ones, signed=True) * (1.0 / HEAD_DIM)
    dlt = o - mean
    var = _head_sum(dlt * dlt, ones) * (1.0 / HEAD_DIM)
    on = dlt * lax.rsqrt(var + RWKV_GN_EPS) * ln_g + ln_b
    bonus = _head_sum(r * k * r_k, ones, signed=True) * v
    return (on + bonus) * _silu(z)


def _swap_halves(x):
    half = HEAD_DIM // 2
    n = x.shape[-1]
    first = (_iota(x.shape, 1) & half) == 0
    return jnp.where(first, pltpu.roll(x, n - half, axis=1), pltpu.roll(x, half, axis=1))


def _rotary(x, cos, sin):
    return x * cos + _swap_halves(x) * sin


def _lru_token_math(xc, gate_w, gate_b, lam):
    gates = _dotp(xc, gate_w, _NN, P_MISC) + gate_b
    r_gate = jax.nn.sigmoid(gates[:, :W_MIX])
    i_gate = jax.nn.sigmoid(gates[:, W_MIX:])
    log_a = -LRU_C * r_gate * _softplus(-lam)
    a = jnp.exp(log_a)
    b = jnp.sqrt(1.0 - jnp.exp(2.0 * log_a)) * (i_gate * xc)
    return a, b


def _gdn_token_math(qkv, b_raw, a_raw, a_log, dt_bias, ones):
    qkv = _silu(qkv)
    q = qkv[:, 0:W_MIX]
    k = qkv[:, W_MIX:2 * W_MIX]
    v = qkv[:, 2 * W_MIX:]
    q = q * lax.rsqrt(_head_sum(q * q, ones) + EPS) * (HEAD_DIM ** -0.5)
    k = k * lax.rsqrt(_head_sum(k * k, ones) + EPS)
    beta = jax.nn.sigmoid(b_raw)
    g = -jnp.exp(a_log) * _softplus(a_raw + dt_bias)
    return q, k, v, beta, g


def _head_rms_finish(o, z, ones, gain=None):
    y = o * lax.rsqrt(_head_sum(o * o, ones) * (1.0 / HEAD_DIM) + EPS)
    if gain is not None:
        y = y * gain
    return y * _silu(z)


def _conv_tile(u, ext_ref, w_ref, first):
    n = u.shape[0]

    @pl.when(first)
    def _():
        ext_ref[0:SUBLANES, :] = jnp.zeros((SUBLANES, u.shape[1]), F32)

    ext_ref[SUBLANES:SUBLANES + n, :] = u
    out = None
    for j in range(CONV_W):
        term = _rows_back(u, ext_ref, CONV_W - 1 - j) * w_ref[j:j + 1, :]
        out = term if out is None else out + term
    ext_ref[0:SUBLANES, :] = u[n - SUBLANES:n, :]
    return out


def _rows_back(u, ext_ref, back):
    if back == 0:
        return u
    n, ch = u.shape
    tiles = (n // SUBLANES, SUBLANES, ch)
    pos = _iota((1, SUBLANES, 1), 1)
    earlier = ext_ref[0:n, :].reshape(tiles)
    return pltpu.roll(jnp.where(pos >= SUBLANES - back, earlier, u.reshape(tiles)), back, axis=1).reshape(n, ch)


def _rwkv_kernel(p_ref, mu_ref, w0_ref, w2_ref, a0_ref, a2_ref, kk_ref, ka_ref, rk_ref, lng_ref, lnb_ref,
                 o_ref, s_out_ref, shift_out_ref, s_scr, ext_scr):
    j = pl.program_id(1)
    last = pl.num_programs(1) - 1
    ct = p_ref.shape[0]

    @pl.when(j == 0)
    def _():
        s_scr[...] = jnp.zeros(s_scr.shape, F32)
        ext_scr[0:SUBLANES, :] = jnp.zeros((SUBLANES, A_SHIFT_W), F32)

    p = p_ref[...]
    pa = p[:, :A_SHIFT_W]
    z = p[:, A_SHIFT_W:]
    ext_scr[SUBLANES:SUBLANES + ct, :] = pa
    prev = _rows_back(pa, ext_scr, 1)
    ext_scr[0:SUBLANES, :] = pa[ct - SUBLANES:ct, :]
    pm = pa + (prev - pa) * mu_ref[...]
    ones = _head_ones()
    r, k, v, ld, av, bv = _rwkv_token_math(pm, w0_ref[...], w2_ref[...], a0_ref[...], a2_ref[...],
                                           kk_ref[...], ka_ref[...], ones)

    c = min(CHUNK, ct)
    ha = _HeadAlgebra(c)
    lt = (_iota((c, c), 0) >= _iota((c, c), 1)).astype(BF16)
    units = []
    for c0 in range(0, ct, c):
        sl = slice(c0, c0 + c)
        ldc = ld[sl]
        cum = _dot_const(ldc, lt, _NN, CUMSUM_PIECES, const_left=True)
        e_neg = jnp.exp(-cum)
        e_out = jnp.exp(cum[c - 1:c, :] - cum)
        units.append(dict(a=av[sl] * jnp.exp(cum - ldc), r=r[sl] * jnp.exp(cum), b=bv[sl] * e_neg, k=k[sl] * e_neg,
                          bo=bv[sl] * e_out, ko=k[sl] * e_out, v=v[sl], g=jnp.exp(cum[c - 1:c, :])))
    all_units = units
    state = s_scr[...]
    o_rows = []
    for w0 in range(0, len(all_units), WAVE):
        units = all_units[w0:w0 + WAVE]
        for u in units:
            lhs = jnp.concatenate([u["a"], u["r"]], axis=0)
            u["mb"] = ha.nt(lhs, ha.bd(u["b"]))
            u["mk"] = ha.nt(lhs, ha.bd(u["k"]))
            u["bd_v"] = ha.bd(u["v"])
        for u in units:
            u["m_ab"] = jnp.where(ha.strict, u["mb"][:c], 0.0)
            u["m_ak"] = jnp.where(ha.strict, u["mk"][:c], 0.0)
            u["m_rb"] = jnp.where(ha.incl, u["mb"][c:], 0.0)
            u["m_rk"] = jnp.where(ha.incl, u["mk"][c:], 0.0)
        for u, t_inv in zip(units, ha.inv_unit_lower_many([-u["m_ab"] for u in units])):
            u["t_inv"] = t_inv
        for u in units:
            u["makv"] = ha.nn(u["m_ak"], u["bd_v"], BF16)
        for u in units:
            u["a_hat"] = ha.nn(u["t_inv"], ha.bd(u["a"]), BF16)
            u["u1"] = ha.nn(u["t_inv"], ha.bd(u["makv"]), BF16)
        for u in units:
            u["r_hat"] = u["r"] + ha.nn(u["m_rb"], ha.bd(u["a_hat"]))
            u["o1"] = ha.nn(u["m_rb"], ha.bd(u["u1"])) + ha.nn(u["m_rk"], u["bd_v"])
            u["g_t"] = ha.diag_bd(u["g"]) + ha.tn_bd(u["bo"], u["a_hat"])
            u["h_t"] = ha.tn_bd(jnp.concatenate([u["bo"], u["ko"]], axis=0),
                                jnp.concatenate([u["u1"], u["v"].astype(BF16)], axis=0))
            zz = lax.dot_general(jnp.concatenate([u["r_hat"], u["g_t"]], axis=0).astype(BF16), state.astype(BF16),
                                 _NN, preferred_element_type=F32)
            o_rows.append(zz[:c] + u["o1"])
            state = zz[c:] + u["h_t"]
    o = o_rows[0] if len(o_rows) == 1 else jnp.concatenate(o_rows, axis=0)
    s_scr[...] = state
    o_ref[...] = _rwkv_finish(o, r, k, v, z, rk_ref[...], lng_ref[...], lnb_ref[...], ones)

    @pl.when(j == last)
    def _():
        eye_h = (_iota((HEAD_DIM, HEAD_DIM), 0) == _iota((HEAD_DIM, HEAD_DIM), 1)).astype(F32)
        for h in range(N_HEADS):
            hs = slice(h * HEAD_DIM, (h + 1) * HEAD_DIM)
            s_out_ref[h] = _mm_nt(eye_h, state[hs, hs])
        shift_out_ref[...] = pa[ct - 1:ct, :]


def _row(a):
    return a.reshape(1, -1)


def _rwkv_call(p_a, n_seq, seq_len, lp):
    ct = min(ROW_TILE, seq_len)
    p3 = p_a.reshape(n_seq, seq_len, A_W)
    params = [_row(lp["rwkv_mu"]), _row(lp["rwkv_w0"]), lp["rwkv_w2"], _row(lp["rwkv_a0"]), lp["rwkv_a2"],
              _row(lp["rwkv_k_k"]), _row(lp["rwkv_k_a"]), _row(lp["rwkv_r_k"]), _row(lp["rwkv_ln_g"]),
              _row(lp["rwkv_ln_b"])]
    o, s1, shift1 = pl.pallas_call(
        _rwkv_kernel,
        grid=(n_seq, seq_len // ct),
        in_specs=[pl.BlockSpec((None, ct, A_W), lambda b, j: (b, j, 0))]
        + [pl.BlockSpec(a.shape, lambda b, j: (0, 0)) for a in params],
        out_specs=[
            pl.BlockSpec((None, ct, W_MIX), lambda b, j: (b, j, 0)),
            pl.BlockSpec((None, N_HEADS, HEAD_DIM, HEAD_DIM), lambda b, j: (b, 0, 0, 0)),
            pl.BlockSpec((None, 1, A_SHIFT_W), lambda b, j: (b, 0, 0)),
        ],
        out_shape=[
            jax.ShapeDtypeStruct((n_seq, seq_len, W_MIX), F32),
            jax.ShapeDtypeStruct((n_seq, N_HEADS, HEAD_DIM, HEAD_DIM), F32),
            jax.ShapeDtypeStruct((n_seq, 1, A_SHIFT_W), F32),
        ],
        scratch_shapes=[pltpu.VMEM((W_MIX, W_MIX), F32), pltpu.VMEM((ct + SUBLANES, A_SHIFT_W), F32)],
        compiler_params=pltpu.CompilerParams(dimension_semantics=("parallel", "arbitrary"),
                                             vmem_limit_bytes=VMEM_LIMIT),
        name="rwkv7_prompt",
    )(p3, *params)
    return o.reshape(n_seq * seq_len, W_MIX), s1, shift1.reshape(n_seq, A_SHIFT_W)


def _ret_kernel(p_ref, cos_ref, sin_ref, o_ref, s_out_ref, s_scr):
    j = pl.program_id(1)
    last = pl.num_programs(1) - 1
    ct = p_ref.shape[0]

    @pl.when(j == 0)
    def _():
        s_scr[...] = jnp.zeros(s_scr.shape, F32)

    p = p_ref[...]
    cos, sin = cos_ref[...], sin_ref[...]
    q = _rotary(p[:, 0:W_MIX], cos, sin)
    k = _rotary(p[:, W_MIX:2 * W_MIX], cos, sin) * (HEAD_DIM ** -0.5)
    v = p[:, 2 * W_MIX:3 * W_MIX]
    z = p[:, 3 * W_MIX:]
    c = min(RET_CHUNK, ct)
    ri, ci = _iota((c, c), 0), _iota((c, c), 1)
    causal = ri >= ci
    rel = jnp.where(causal, ri - ci, 0).astype(F32)
    idx = _iota((c, 1), 0).astype(F32)
    states = [s_scr[h] for h in range(N_HEADS)]
    units = []
    for h in range(N_HEADS):
        lg = math.log(1.0 - 2.0 ** (-5.0 - h))
        consts = dict(decay=jnp.where(causal, jnp.exp(lg * rel), 0.0), q_dec=jnp.exp(lg * (idx + 1.0)),
                      k_dec=jnp.exp(lg * (c - 1.0 - idx)), g_c=math.exp(lg * c))
        hs = slice(h * HEAD_DIM, (h + 1) * HEAD_DIM)
        for c0 in range(0, ct, c):
            sl = slice(c0, c0 + c)
            units.append(dict(consts, h=h, q=_Split(q[sl, hs], 1), k=k[sl, hs], v=_Split(v[sl, hs], 1)))
    for u in units:
        u["s_in"] = _dotp(u["q"], u["k"], _NT, P_MISC) * u["decay"]
        u["kv"] = _dotp(u["k"] * u["k_dec"], u["v"], _TN, P_MISC)
    for u in units:
        u["o"] = _dotp(u["s_in"], u["v"], _NN, P_MISC)
    for u in units:
        u["s0"] = states[u["h"]]
        states[u["h"]] = u["s0"] * u["g_c"] + u["kv"]
    o_heads_all = [[] for _ in range(N_HEADS)]
    for u in units:
        o_heads_all[u["h"]].append(u["o"] + _dotp(u["q"], u["s0"], _NN, P_MISC) * u["q_dec"])
    cols = [oh[0] if len(oh) == 1 else jnp.concatenate(oh, axis=0) for oh in o_heads_all]
    o = jnp.concatenate(cols, axis=1)
    for h in range(N_HEADS):
        s_scr[h] = states[h]
    o_ref[...] = _head_rms_finish(o, z, _head_ones())

    @pl.when(j == last)
    def _():
        for h in range(N_HEADS):
            s_out_ref[h] = states[h]


def _rope_tables(pos):
    half = HEAD_DIM // 2
    inv = ROPE_BASE ** (-jnp.arange(half, dtype=F32) / half)
    ang = pos.astype(F32)[:, None] * inv[None, :]
    cos, sin = jnp.cos(ang), jnp.sin(ang)
    cos_t = jnp.tile(jnp.concatenate([cos, cos], axis=-1), (1, N_HEADS))
    sin_t = jnp.tile(jnp.concatenate([-sin, sin], axis=-1), (1, N_HEADS))
    return cos_t, sin_t


def _ret_call(p_b, n_seq, seq_len, cos_t, sin_t):
    ct = min(ROW_TILE, seq_len)
    p3 = p_b.reshape(n_seq, seq_len, B_W)
    o, s1 = pl.pallas_call(
        _ret_kernel,
        grid=(n_seq, seq_len // ct),
        in_specs=[
            pl.BlockSpec((None, ct, B_W), lambda b, j: (b, j, 0)),
            pl.BlockSpec((ct, W_MIX), lambda b, j: (j, 0)),
            pl.BlockSpec((ct, W_MIX), lambda b, j: (j, 0)),
        ],
        out_specs=[
            pl.BlockSpec((None, ct, W_MIX), lambda b, j: (b, j, 0)),
            pl.BlockSpec((None, N_HEADS, HEAD_DIM, HEAD_DIM), lambda b, j: (b, 0, 0, 0)),
        ],
        out_shape=[
            jax.ShapeDtypeStruct((n_seq, seq_len, W_MIX), F32),
            jax.ShapeDtypeStruct((n_seq, N_HEADS, HEAD_DIM, HEAD_DIM), F32),
        ],
        scratch_shapes=[pltpu.VMEM((N_HEADS, HEAD_DIM, HEAD_DIM), F32)],
        compiler_params=pltpu.CompilerParams(dimension_semantics=("parallel", "arbitrary"),
                                             vmem_limit_bytes=VMEM_LIMIT),
        name="retention_prompt",
    )(p3, cos_t, sin_t)
    return o.reshape(n_seq * seq_len, W_MIX), s1


def _affine_scan(a, b, span):
    n, w = a.shape
    if span == SUBLANES and n > span:
        shape, axis = (n // span, span, w), 1
        a, b = a.reshape(shape), b.reshape(shape)
        pos = _iota((1, span, 1), 1)
    else:
        assert span >= n
        axis = 0
        pos = _iota((n, 1), 0)
    dist = 1
    while dist < span:
        keep = pos >= dist
        a_prev = jnp.where(keep, pltpu.roll(a, dist, axis=axis), 1.0)
        b_prev = jnp.where(keep, pltpu.roll(b, dist, axis=axis), 0.0)
        b = a * b_prev + b
        a = a * a_prev
        dist *= 2
    return a.reshape(n, w), b.reshape(n, w)


def _lru_kernel(p_ref, cw_ref, cb_ref, gw_ref, gb_ref, sp_ref, o_ref, h_out_ref, conv_out_ref,
                ext_scr, h_scr, ab_scr, hin_scr):
    j = pl.program_id(1)
    last = pl.num_programs(1) - 1
    ct = p_ref.shape[0]

    @pl.when(j == 0)
    def _():
        h_scr[...] = jnp.zeros(h_scr.shape, F32)

    p = p_ref[...]
    xr = p[:, :W_MIX]
    z = p[:, W_MIX:]
    xc = _conv_tile(xr, ext_scr, cw_ref, j == 0) + cb_ref[...]
    a, b = _lru_token_math(xc, gw_ref[...], gb_ref[...], sp_ref[...])
    n_grp = ct // SUBLANES
    a, b = _affine_scan(a, b, SUBLANES)
    n_tiles = W_MIX // LANES
    for t in range(n_tiles):
        ab_scr[t] = a[:, t * LANES:(t + 1) * LANES]
        ab_scr[n_tiles + t] = b[:, t * LANES:(t + 1) * LANES]
    ends = pl.ds(SUBLANES - 1, n_grp, stride=SUBLANES)
    a_end = jnp.concatenate([ab_scr[t, ends, :] for t in range(n_tiles)], axis=1)
    b_end = jnp.concatenate([ab_scr[n_tiles + t, ends, :] for t in range(n_tiles)], axis=1)
    a_end, b_end = _affine_scan(a_end, b_end, n_grp)
    h_prev = h_scr[...]
    h_end = a_end * h_prev + b_end
    grp = _iota((n_grp, 1), 0)
    hin_scr[...] = jnp.where(grp == 0, h_prev, pltpu.roll(h_end, 1, axis=0))
    h_in = jnp.concatenate([jnp.broadcast_to(hin_scr[g:g + 1, :], (SUBLANES, W_MIX)) for g in range(n_grp)], axis=0)
    hcur = a * h_in + b
    h_scr[...] = h_end[n_grp - 1:n_grp, :]
    o_ref[...] = hcur * _silu(z)

    @pl.when(j == last)
    def _():
        h_out_ref[...] = hcur[ct - 1:ct, :]
        conv_out_ref[...] = xr[ct - SUBLANES:ct, :]


def _block_diag_gates(gate_w):
    out = jnp.zeros((W_MIX, 2 * W_MIX), F32)
    for g in range(2):
        for n in range(N_HEADS):
            out = out.at[n * HEAD_DIM:(n + 1) * HEAD_DIM,
                         g * W_MIX + n * HEAD_DIM:g * W_MIX + (n + 1) * HEAD_DIM].set(gate_w[g, n])
    return out


def _lru_params(lp):
    return [lp["lru_conv_w"], _row(lp["lru_conv_b"]), _block_diag_gates(lp["lru_gate_w"]),
            _row(lp["lru_gate_b"]), _row(lp["lru_lambda"])]


def _lru_call(p_c, n_seq, seq_len, lp):
    ct = min(LRU_TILE, seq_len)
    p3 = p_c.reshape(n_seq, seq_len, C_W)
    params = _lru_params(lp)
    o, h1, conv_tail = pl.pallas_call(
        _lru_kernel,
        grid=(n_seq, seq_len // ct),
        in_specs=[pl.BlockSpec((None, ct, C_W), lambda b, j: (b, j, 0))]
        + [pl.BlockSpec(a.shape, lambda b, j: (0, 0)) for a in params],
        out_specs=[
            pl.BlockSpec((None, ct, W_MIX), lambda b, j: (b, j, 0)),
            pl.BlockSpec((None, 1, W_MIX), lambda b, j: (b, 0, 0)),
            pl.BlockSpec((None, SUBLANES, W_MIX), lambda b, j: (b, 0, 0)),
        ],
        out_shape=[
            jax.ShapeDtypeStruct((n_seq, seq_len, W_MIX), F32),
            jax.ShapeDtypeStruct((n_seq, 1, W_MIX), F32),
            jax.ShapeDtypeStruct((n_seq, SUBLANES, W_MIX), F32),
        ],
        scratch_shapes=[pltpu.VMEM((ct + SUBLANES, W_MIX), F32), pltpu.VMEM((1, W_MIX), F32),
                        pltpu.VMEM((2 * W_MIX // LANES, ct, LANES), F32),
                        pltpu.VMEM((ct // SUBLANES, W_MIX), F32)],
        compiler_params=pltpu.CompilerParams(dimension_semantics=("parallel", "arbitrary"),
                                             vmem_limit_bytes=VMEM_LIMIT),
        name="rglru_prompt",
    )(p3, *params)
    return (o.reshape(n_seq * seq_len, W_MIX), h1.reshape(n_seq, W_MIX),
            conv_tail[:, SUBLANES - (CONV_W - 1):, :])


def _gdn_kernel(p_ref, cw_ref, nal_ref, dtb_ref, ng_ref, o_ref, s_out_ref, conv_out_ref, ext_scr, s_scr):
    j = pl.program_id(1)
    last = pl.num_programs(1) - 1
    ct = p_ref.shape[0]

    @pl.when(j == 0)
    def _():
        s_scr[...] = jnp.zeros(s_scr.shape, F32)

    p = p_ref[...]
    raw = p[:, :D_QKV_W]
    z = p[:, D_QKV_W:D_QKV_W + W_MIX]
    b_raw = p[:, D_QKV_W + W_MIX:D_QKV_W + 2 * W_MIX]
    a_raw = p[:, D_QKV_W + 2 * W_MIX:]
    ones = _head_ones()
    qkv = _conv_tile(raw, ext_scr, cw_ref, j == 0)
    q, k, v, beta, g = _gdn_token_math(qkv, b_raw, a_raw, nal_ref[...], dtb_ref[...], ones)

    c = min(CHUNK, ct)
    ha = _HeadAlgebra(c)
    lt = (_iota((c, c), 0) >= _iota((c, c), 1)).astype(BF16)
    units = []
    for c0 in range(0, ct, c):
        sl = slice(c0, c0 + c)
        gc = _dot_const(g[sl], lt, _NN, CUMSUM_PIECES, const_left=True)
        gc_cols = jnp.sum(gc * ha.eye, axis=0, keepdims=True)
        diff = gc - gc_cols
        decay = jnp.where(ha.incl, jnp.exp(jnp.where(ha.incl, diff, 0.0)), 0.0)
        kb = k[sl] * beta[sl]
        e_gc = jnp.exp(gc)
        g_last = gc[c - 1:c, :]
        units.append(dict(decay=decay, kb=kb, q=q[sl], k=k[sl], vb=v[sl] * beta[sl], kbe=kb * e_gc,
                          k_out=k[sl] * jnp.exp(g_last - gc), q_in=q[sl] * e_gc, e_last=jnp.exp(g_last)))
    all_units = units
    state = s_scr[...]
    o_rows = []
    for w0 in range(0, len(all_units), WAVE):
        units = all_units[w0:w0 + WAVE]
        for u in units:
            kq = ha.nt(jnp.concatenate([u["kb"], u["q"]], axis=0), ha.bd(u["k"]))
            u["a_mat"] = jnp.where(ha.strict, kq[:c] * u["decay"], 0.0)
            u["qk"] = kq[c:] * u["decay"]
        for u, t_inv in zip(units, ha.inv_unit_lower_many([u["a_mat"] for u in units])):
            u["t_inv"] = t_inv
        for u in units:
            u["u"] = ha.nn(u["t_inv"], ha.bd(u["vb"]), BF16)
            u["w"] = ha.nn(u["t_inv"], ha.bd(u["kbe"]), BF16)
        for u in units:
            u["g_mat"] = ha.diag_bd(u["e_last"]) - ha.tn_bd(u["k_out"], u["w"])
            u["h_mat"] = ha.tn_bd(u["k_out"], u["u"])
            u["q_hat"] = u["q_in"] - ha.nn(u["qk"], ha.bd(u["w"]))
            u["o1"] = ha.nn(u["qk"], ha.bd(u["u"]))
            zz = lax.dot_general(jnp.concatenate([u["q_hat"], u["g_mat"]], axis=0).astype(BF16),
                                 state.astype(BF16), _NN, preferred_element_type=F32)
            o_rows.append(zz[:c] + u["o1"])
            state = zz[c:] + u["h_mat"]
    o = o_rows[0] if len(o_rows) == 1 else jnp.concatenate(o_rows, axis=0)
    s_scr[...] = state
    o_ref[...] = _head_rms_finish(o, z, ones, ng_ref[...])

    @pl.when(j == last)
    def _():
        for h in range(N_HEADS):
            hs = slice(h * HEAD_DIM, (h + 1) * HEAD_DIM)
            s_out_ref[h] = state[hs, hs]
        conv_out_ref[...] = raw[ct - SUBLANES:ct, :]


def _gdn_params(lp):
    return [lp["gdn_conv_w"], _row(jnp.repeat(lp["gdn_A_log"], HEAD_DIM)),
            _row(jnp.repeat(lp["gdn_dt_bias"], HEAD_DIM)), _row(jnp.tile(lp["gdn_norm_g"], N_HEADS))]


def _gdn_call(p_d, n_seq, seq_len, lp):
    ct = min(ROW_TILE, seq_len)
    p3 = p_d.reshape(n_seq, seq_len, D_PACK_W)
    params = _gdn_params(lp)
    o, s1, conv_tail = pl.pallas_call(
        _gdn_kernel,
        grid=(n_seq, seq_len // ct),
        in_specs=[pl.BlockSpec((None, ct, D_PACK_W), lambda b, j: (b, j, 0))]
        + [pl.BlockSpec(a.shape, lambda b, j: (0, 0)) for a in params],
        out_specs=[
            pl.BlockSpec((None, ct, W_MIX), lambda b, j: (b, j, 0)),
            pl.BlockSpec((None, N_HEADS, HEAD_DIM, HEAD_DIM), lambda b, j: (b, 0, 0, 0)),
            pl.BlockSpec((None, SUBLANES, D_QKV_W), lambda b, j: (b, 0, 0)),
        ],
        out_shape=[
            jax.ShapeDtypeStruct((n_seq, seq_len, W_MIX), F32),
            jax.ShapeDtypeStruct((n_seq, N_HEADS, HEAD_DIM, HEAD_DIM), F32),
            jax.ShapeDtypeStruct((n_seq, SUBLANES, D_QKV_W), F32),
        ],
        scratch_shapes=[pltpu.VMEM((ct + SUBLANES, D_QKV_W), F32), pltpu.VMEM((W_MIX, W_MIX), F32)],
        compiler_params=pltpu.CompilerParams(dimension_semantics=("parallel", "arbitrary"),
                                             vmem_limit_bytes=VMEM_LIMIT),
        name="gdn_prompt",
    )(p3, *params)
    return o.reshape(n_seq * seq_len, W_MIX), s1, conv_tail[:, SUBLANES - (CONV_W - 1):, :]


def _decode_pre_kernel(pa_ref, pb_ref, pc_ref, pd_ref, shift_ref, h0_ref, lconv_ref, gconv_ref, cos_ref, sin_ref,
                       mu_ref, w0_ref, w2_ref, a0_ref, a2_ref, kk_ref, ka_ref,
                       lcw_ref, lcb_ref, lgw_ref, lgb_ref, lsp_ref, gcw_ref, nal_ref, dtb_ref,
                       vt_ref, vn_ref, oc_ref, h1_ref, lconv1_ref, gconv1_ref):
    ones = _head_ones()
    pa_full = pa_ref[...]
    pa = pa_full[:, :A_SHIFT_W]
    pm = pa + (shift_ref[...] - pa) * mu_ref[...]
    r, k, v, ld, av, bv = _rwkv_token_math(pm, w0_ref[...], w2_ref[...], a0_ref[...], a2_ref[...],
                                           kk_ref[...], ka_ref[...], ones)
    vecs = [r, jnp.exp(ld), k, v, av, bv]
    plain = [r, k, v, pa_full[:, A_SHIFT_W:]]
    pb = pb_ref[...]
    cos, sin = cos_ref[...], sin_ref[...]
    vecs += [_rotary(pb[:, 0:W_MIX], cos, sin), _rotary(pb[:, W_MIX:2 * W_MIX], cos, sin) * (HEAD_DIM ** -0.5),
             pb[:, 2 * W_MIX:3 * W_MIX]]
    plain.append(pb[:, 3 * W_MIX:])
    pc = pc_ref[...]
    xr = pc[:, :W_MIX]
    taps = [lconv_ref[i] for i in range(CONV_W - 1)] + [xr]
    xc = taps[0] * lcw_ref[0:1, :]
    for i in range(1, CONV_W):
        xc = xc + taps[i] * lcw_ref[i:i + 1, :]
    xc = xc + lcb_ref[...]
    a, b = _lru_token_math(xc, lgw_ref[...], lgb_ref[...], lsp_ref[...])
    hcur = a * h0_ref[...] + b
    oc_ref[...] = hcur * _silu(pc[:, W_MIX:])
    h1_ref[...] = hcur
    for i in range(CONV_W - 1):
        lconv1_ref[i] = taps[i + 1]
    pd = pd_ref[...]
    raw = pd[:, :D_QKV_W]
    gtaps = [gconv_ref[i] for i in range(CONV_W - 1)] + [raw]
    qkv = gtaps[0] * gcw_ref[0:1, :]
    for i in range(1, CONV_W):
        qkv = qkv + gtaps[i] * gcw_ref[i:i + 1, :]
    q, kg, vg, beta, g = _gdn_token_math(qkv, pd[:, D_QKV_W + W_MIX:D_QKV_W + 2 * W_MIX],
                                         pd[:, D_QKV_W + 2 * W_MIX:], nal_ref[...], dtb_ref[...], ones)
    for i in range(CONV_W - 1):
        gconv1_ref[i] = gtaps[i + 1]
    vecs += [q, kg, vg, beta, g]
    plain.append(pd[:, D_QKV_W:D_QKV_W + W_MIX])
    assert len(vecs) == N_VEC_T and len(plain) == N_VEC_PLAIN
    for i, vec in enumerate(vecs):
        vt_ref[i] = vec.T
    for i, vec in enumerate(plain):
        vn_ref[i] = vec


def _decode_state_kernel(vt_ref, wkv_ref, ret_ref, gdn_ref, gam_ref, wkv1_ref, ret1_ref, gdn1_ref, o_ref):
    v_r, v_w, v_k, v_v, v_a, v_b, r_q, r_k, r_v, g_q, g_k, g_v, g_beta, g_g = range(N_VEC_T)
    hd = HEAD_DIM
    n = vt_ref.shape[-1]
    row = lambda idx, i: vt_ref[idx, pl.ds(i, 1), :]
    rows_of = lambda i: pl.ds(pl.multiple_of(i * hd, hd), hd)
    gamma = gam_ref[...]
    beta = vt_ref[g_beta, 0:1, :]
    eg = jnp.exp(vt_ref[g_g, 0:1, :])

    def first_pass(i, carry):
        acc_ret, acc_w, acc_q = carry
        rows = rows_of(i)
        s = wkv_ref[rows, :]
        sa = jnp.sum(s * vt_ref[v_a], axis=0, keepdims=True)
        s = s * vt_ref[v_w] + sa * vt_ref[v_b] + row(v_v, i) * vt_ref[v_k]
        wkv1_ref[rows, :] = s
        o_ref[0, pl.ds(i, 1), :] = jnp.sum(s * vt_ref[v_r], axis=0, keepdims=True)
        s = ret_ref[rows, :] * gamma + row(r_k, i) * vt_ref[r_v]
        ret1_ref[rows, :] = s
        acc_ret = acc_ret + row(r_q, i) * s
        s = gdn_ref[rows, :]
        return acc_ret, acc_w + row(g_k, i) * s, acc_q + row(g_q, i) * s

    zeros = jnp.zeros((hd, n), F32)
    acc_ret, acc_w, acc_q = lax.fori_loop(0, hd, first_pass, (zeros, zeros, zeros))
    o_ref[1] = acc_ret
    v_new = vt_ref[g_v] * beta - acc_w * (beta * eg)
    qk = jnp.sum(vt_ref[g_q] * vt_ref[g_k], axis=0, keepdims=True)
    o_ref[2] = acc_q * eg + qk * v_new

    def second_pass(i, carry):
        rows = rows_of(i)
        gdn1_ref[rows, :] = gdn_ref[rows, :] * eg + row(g_k, i) * v_new
        return carry

    lax.fori_loop(0, hd, second_pass, 0)


def _decode_finish_kernel(ot_ref, vn_ref, rk_ref, lng_ref, lnb_ref, ng_ref, oa_ref, ob_ref, od_ref):
    ones = _head_ones()
    r, k, v, z_a, z_b, z_d = (vn_ref[i] for i in range(N_VEC_PLAIN))
    oa_ref[...] = _rwkv_finish(ot_ref[0].T, r, k, v, z_a, rk_ref[...], lng_ref[...], lnb_ref[...], ones)
    ob_ref[...] = _head_rms_finish(ot_ref[1].T, z_b, ones)
    od_ref[...] = _head_rms_finish(ot_ref[2].T, z_d, ones, ng_ref[...])


def _batch_minor(state):
    n_layers, n = state.shape[:2]
    return jnp.transpose(state, (0, 2, 3, 4, 1)).reshape(n_layers, -1, n)


def _batch_major(flat_state):
    n = flat_state.shape[-1]
    return jnp.transpose(flat_state.reshape(N_HEADS, HEAD_DIM, HEAD_DIM, n), (3, 0, 1, 2))


def _decode_layer(l, p_a, p_b, p_c, p_d, carried, cos_t, sin_t, lp):
    n = p_a.shape[0]
    assert n % LANES == 0, "the decode state kernel keeps the batch on lanes"
    flat = HEAD_DIM * HEAD_DIM
    taps = CONV_W - 1
    rwkv_params = [_row(lp["rwkv_mu"]), _row(lp["rwkv_w0"]), lp["rwkv_w2"], _row(lp["rwkv_a0"]), lp["rwkv_a2"],
                   _row(lp["rwkv_k_k"]), _row(lp["rwkv_k_a"])]
    gdn_params = _gdn_params(lp)
    full = lambda a: pl.BlockSpec(a.shape, lambda i: (0,) * a.ndim)
    layer_blk = lambda a: pl.BlockSpec((None,) + a.shape[1:], lambda i, nd=a.ndim: (l,) + (0,) * (nd - 1))
    projs = [p_a, p_b, p_c, p_d]
    layered = [carried["shift"], carried["lru_h"], carried["lru_conv"], carried["gdn_conv"]]
    consts = [cos_t, sin_t, *rwkv_params, *_lru_params(lp), *gdn_params[:3]]
    out_shapes = [
        jax.ShapeDtypeStruct((N_VEC_T, W_MIX, n), F32),
        jax.ShapeDtypeStruct((N_VEC_PLAIN, n, W_MIX), F32),
        jax.ShapeDtypeStruct((n, W_MIX), F32),
        jax.ShapeDtypeStruct((n, W_MIX), F32),
        jax.ShapeDtypeStruct((taps, n, W_MIX), F32),
        jax.ShapeDtypeStruct((taps, n, D_QKV_W), F32),
    ]
    vec_t, vec_n, o_c, lru_h1, lru_conv1, gdn_conv1 = pl.pallas_call(
        _decode_pre_kernel,
        grid=(1,),
        in_specs=[full(a) for a in projs] + [layer_blk(a) for a in layered] + [full(a) for a in consts],
        out_specs=[pl.BlockSpec(s.shape, lambda i, nd=len(s.shape): (0,) * nd) for s in out_shapes],
        out_shape=out_shapes,
        compiler_params=pltpu.CompilerParams(dimension_semantics=("arbitrary",), vmem_limit_bytes=VMEM_LIMIT),
        name="decode_tokens",
    )(*projs, *layered, *consts)

    gam = jnp.broadcast_to((1.0 - 2.0 ** (-5.0 - jnp.arange(N_HEADS, dtype=F32)))[:, None, None], (N_HEADS, 1, n))
    state_in = pl.BlockSpec((None, flat, n), lambda h: (l, h, 0))
    state_out = pl.BlockSpec((flat, n), lambda h: (h, 0))
    wkv1, ret1, gdn1, o_t = pl.pallas_call(
        _decode_state_kernel,
        grid=(N_HEADS,),
        in_specs=[pl.BlockSpec((N_VEC_T, HEAD_DIM, n), lambda h: (0, h, 0)), state_in, state_in, state_in,
                  pl.BlockSpec((None, 1, n), lambda h: (h, 0, 0))],
        out_specs=[state_out, state_out, state_out, pl.BlockSpec((3, HEAD_DIM, n), lambda h: (0, h, 0))],
        out_shape=[jax.ShapeDtypeStruct((N_HEADS * flat, n), F32)] * 3 + [jax.ShapeDtypeStruct((3, W_MIX, n), F32)],
        compiler_params=pltpu.CompilerParams(dimension_semantics=("parallel",), vmem_limit_bytes=VMEM_LIMIT),
        name="decode_states",
    )(vec_t, carried["wkv"], carried["ret"], carried["gdn"], gam)

    finish_ins = [o_t, vec_n, _row(lp["rwkv_r_k"]), _row(lp["rwkv_ln_g"]), _row(lp["rwkv_ln_b"]), gdn_params[3]]
    o_a, o_b, o_d = pl.pallas_call(
        _decode_finish_kernel,
        grid=(1,),
        in_specs=[full(a) for a in finish_ins],
        out_specs=[pl.BlockSpec((n, W_MIX), lambda i: (0, 0))] * 3,
        out_shape=[jax.ShapeDtypeStruct((n, W_MIX), F32)] * 3,
        compiler_params=pltpu.CompilerParams(dimension_semantics=("arbitrary",), vmem_limit_bytes=VMEM_LIMIT),
        name="decode_finish",
    )(*finish_ins)
    new_states = (_batch_major(wkv1), p_a[:, :A_SHIFT_W], _batch_major(ret1), lru_h1,
                  jnp.transpose(lru_conv1, (1, 0, 2)), _batch_major(gdn1), jnp.transpose(gdn_conv1, (1, 0, 2)))
    return (o_a, o_b, o_c, o_d), new_states


def _per_lane_beta_a_rows(w_t, layer):
    off_ba = A_W + B_W + C_W + D_QKV_W
    rows = lax.slice(w_t, (layer, off_ba, 0), (layer + 1, off_ba + 2 * N_HEADS, w_t.shape[2]))[0]
    return jnp.repeat(rows, HEAD_DIM, axis=0)


def _prompt_layer(p_a, p_b, p_c, p_d, n_seq, seq_len, cos_t, sin_t, lp):
    o_a, wkv1, shift1 = _rwkv_call(p_a, n_seq, seq_len, lp)
    o_b, ret1 = _ret_call(p_b, n_seq, seq_len, cos_t, sin_t)
    o_c, lru_h1, lru_conv1 = _lru_call(p_c, n_seq, seq_len, lp)
    o_d, gdn1, gdn_conv1 = _gdn_call(p_d, n_seq, seq_len, lp)
    return (o_a, o_b, o_c, o_d), (wkv1, shift1, ret1, lru_h1, lru_conv1, gdn1, gdn_conv1)


def _run_group(x, mods, pos, carried, layers, final_g):
    n_seq, seq_len, d = x.shape
    x2 = x.reshape(n_seq * seq_len, d)
    cos_t, sin_t = _rope_tables(pos)
    new = []
    n_layers = len(layers)
    for l, lp in enumerate(layers):
        shift, scale, gate = (mods[l][:, i * d:(i + 1) * d] for i in range(3))
        p_a, p_b, p_c, p_d = _inproj_call(x2, scale, shift, lp["norm_g"], l, lp["w_t"], lp["w_ba"], seq_len)
        if carried is None:
            branches, st = _prompt_layer(p_a, p_b, p_c, p_d, n_seq, seq_len, cos_t, sin_t, lp)
        else:
            branches, st = _decode_layer(l, p_a, p_b, p_c, p_d, carried, cos_t, sin_t, lp)
        new.append(st)
        x2 = _outproj_call(x2, scale, shift, gate, lp["norm_g"], branches, l, lp["w_t"], lp["w_up_bf16"],
                           lp["w_out_bf16"], final_g, seq_len, final=(l == n_layers - 1))
    stacked = tuple(jnp.stack([s[i] for s in new], axis=0) for i in range(7))
    return x2.reshape(n_seq, seq_len, d), stacked


def kernel(x_prompt, x_sample, c_prompt, c_sample, state_rwkv_wkv, state_rwkv_shift, state_ret, state_lru_h, state_lru_conv, state_gdn, state_gdn_conv, ada_w, ada_b, norm_g, w_in, rwkv_mu, rwkv_w0, rwkv_w2, rwkv_a0, rwkv_a2, rwkv_k_k, rwkv_k_a, rwkv_r_k, rwkv_ln_g, rwkv_ln_b, lru_conv_w, lru_conv_b, lru_gate_w, lru_gate_b, lru_lambda, gdn_conv_w, gdn_A_log, gdn_dt_bias, gdn_norm_g, w_up, w_out, final_g):
    n_layers = ada_w.shape[0]
    n_prompt, seq_len, _ = x_prompt.shape
    n_sample, dec_len, _ = x_sample.shape
    assert dec_len == 1, "the decode path handles one token per sequence"
    layers = []
    w_in_t = jnp.swapaxes(w_in, 1, 2).astype(BF16)
    for l in range(n_layers):
        layers.append(dict(
            norm_g=norm_g[l], w_t=w_in_t, w_ba=_per_lane_beta_a_rows(w_in_t, l), w_up_bf16=w_up[l].astype(BF16),
            w_out_bf16=w_out[l].astype(BF16),
            rwkv_mu=rwkv_mu[l], rwkv_w0=rwkv_w0[l], rwkv_w2=rwkv_w2[l], rwkv_a0=rwkv_a0[l], rwkv_a2=rwkv_a2[l],
            rwkv_k_k=rwkv_k_k[l], rwkv_k_a=rwkv_k_a[l], rwkv_r_k=rwkv_r_k[l], rwkv_ln_g=rwkv_ln_g[l],
            rwkv_ln_b=rwkv_ln_b[l], lru_conv_w=lru_conv_w[l], lru_conv_b=lru_conv_b[l],
            lru_gate_w=lru_gate_w[l], lru_gate_b=lru_gate_b[l], lru_lambda=lru_lambda[l],
            gdn_conv_w=gdn_conv_w[l], gdn_A_log=gdn_A_log[l], gdn_dt_bias=gdn_dt_bias[l],
            gdn_norm_g=gdn_norm_g[l]))
    mods = _ada_call(jnp.concatenate([c_prompt, c_sample], axis=0), ada_w, ada_b)
    mods_p = [mods[l, :n_prompt] for l in range(n_layers)]
    mods_s = [mods[l, n_prompt:] for l in range(n_layers)]

    y_prompt, new_p = _run_group(x_prompt, mods_p, jnp.arange(seq_len, dtype=jnp.int32), None, layers, final_g)
    carried = dict(wkv=_batch_minor(state_rwkv_wkv), ret=_batch_minor(state_ret), gdn=_batch_minor(state_gdn),
                   shift=state_rwkv_shift, lru_h=state_lru_h,
                   lru_conv=jnp.transpose(state_lru_conv, (0, 2, 1, 3)),
                   gdn_conv=jnp.transpose(state_gdn_conv, (0, 2, 1, 3)))
    pos_s = PAST_LEN + jnp.arange(dec_len, dtype=jnp.int32)
    y_sample, new_s = _run_group(x_sample, mods_s, pos_s, carried, layers, final_g)
    return (y_prompt, y_sample) + new_p + new_s
```

```python
import functools
import math

import jax
import jax.numpy as jnp
from jax import lax
from jax.experimental import pallas as pl
from jax.experimental.pallas import tpu as pltpu

F32 = jnp.float32
BF16 = jnp.bfloat16
HI = lax.Precision.HIGHEST

N_HEADS = 4
HEAD_DIM = 64
W_MIX = N_HEADS * HEAD_DIM
LORA = 64
CONV_W = 4
N_BRANCH = 4
LRU_C = 8.0
ROPE_BASE = 10000.0
EPS = 1e-6
RWKV_GN_EPS = 64e-5
PAST_LEN = 16384
A_SHIFT_W = 3 * W_MIX + 2 * LORA
A_W = A_SHIFT_W + W_MIX
B_W = 4 * W_MIX
C_W = 2 * W_MIX
D_QKV_W = 3 * W_MIX
D_W = D_QKV_W + 2 * N_HEADS + W_MIX
D_PACK_W = D_QKV_W + 3 * W_MIX

SUBLANES = 8
LANES = 128
VMEM_LIMIT = 56 * 1024 * 1024

CHUNK = 64
RET_CHUNK = 128
INV_BLOCK = 16
WAVE = 16
ROW_TILE = 1024
LRU_TILE = 512
PROJ_TILE = 512
OUT_TILE = 1024
N_VEC_T = 14
N_VEC_PLAIN = 6


def _mm(a, b, prec=HI):
    return lax.dot_general(a, b, (((1,), (0,)), ((), ())), precision=prec, preferred_element_type=F32)


def _mm_nt(a, b, prec=HI):
    return lax.dot_general(a, b, (((1,), (1,)), ((), ())), precision=prec, preferred_element_type=F32)


def _mm_tn(a, b, prec=HI):
    return lax.dot_general(a, b, (((0,), (0,)), ((), ())), precision=prec, preferred_element_type=F32)


_NN = (((1,), (0,)), ((), ()))
_NT = (((1,), (1,)), ((), ()))
_TN = (((0,), (0,)), ((), ()))

P_INV = 1
P_STATE = 1
P_MISC = 1
HEAD_SUM_PIECES = 1
CUMSUM_PIECES = 2


class _Split:
    def __init__(self, x, passes):
        self.hi = x.astype(BF16)
        self.lo = (x - self.hi.astype(F32)).astype(BF16) if passes > 1 else None


def _dotp(a, b, dims=_NN, passes=1):
    a = a if isinstance(a, _Split) else _Split(a, passes)
    b = b if isinstance(b, _Split) else _Split(b, passes)
    d = lambda x, y: lax.dot_general(x, y, dims, preferred_element_type=F32)
    out = d(a.hi, b.hi)
    if passes > 1:
        out = out + (d(a.hi, b.lo) + d(a.lo, b.hi))
    return out


def _iota(shape, dim):
    return lax.broadcasted_iota(jnp.int32, shape, dim)


def _silu(x):
    return x * jax.nn.sigmoid(x)


def _softplus(x):
    return jnp.maximum(x, 0.0) + jnp.log1p(jnp.exp(-jnp.abs(x)))


def _pieces(x, n):
    out = []
    for i in range(n):
        p = x.astype(BF16)
        out.append(p)
        if i + 1 < n:
            x = x - p.astype(F32)
    return out


def _dot_const(x, const, dims=_NN, n=2, const_left=False):
    out = None
    for p in _pieces(x, n):
        t = lax.dot_general(*((const, p) if const_left else (p, const)), dims, preferred_element_type=F32)
        out = t if out is None else out + t
    return out


def _head_ones():
    return (_iota((W_MIX, W_MIX), 0) // HEAD_DIM == _iota((W_MIX, W_MIX), 1) // HEAD_DIM).astype(BF16)


def _head_sum(x, ones, signed=False):
    return _dot_const(x, ones, n=HEAD_SUM_PIECES + (1 if signed else 0))


def _rms(x):
    return x * lax.rsqrt(jnp.mean(x * x, axis=-1, keepdims=True) + EPS)


def _inv_unit_lower(a):
    return _inv_unit_lower_many([a])[0]


def _inv_unit_lower_many(mats):
    n = mats[0].shape[0]
    ri, ci = _iota((n, n), 0), _iota((n, n), 1)
    eye = (ri == ci).astype(F32)
    diag_blk = (ri // INV_BLOCK) == (ci // INV_BLOCK)
    mm = lambda x, y: _dotp(x, y, _NN, P_INV)
    sp = lambda x: _Split(x, P_INV)
    d = [jnp.where(diag_blk, a, 0.0) for a in mats]
    nb = [a - di for a, di in zip(mats, d)]
    td = [eye - di for di in d]
    p = d
    for _ in range(int(math.log2(INV_BLOCK)) - 1):
        ps = [sp(pi) for pi in p]
        p = [mm(pi, pi) for pi in ps]
        td = [mm(ti, eye + pi) for ti, pi in zip(td, p)]
    tds = [sp(ti) for ti in td]
    x = [mm(ti, ni) for ti, ni in zip(tds, nb)]
    t = [eye - xi for xi in x]
    p = x
    for _ in range(int(math.log2(n // INV_BLOCK)) - 1):
        ps = [sp(pi) for pi in p]
        p = [mm(pi, pi) for pi in ps]
        t = [mm(ti, eye + pi) for ti, pi in zip(t, p)]
    return [mm(ti, tdi) for ti, tdi in zip(t, tds)]


class _HeadAlgebra:
    def __init__(self, c):
        assert c == HEAD_DIM, "side-by-side head products need CHUNK == HEAD_DIM"
        w = W_MIX
        row, lane = _iota((c, w), 0), _iota((c, w), 1)
        col = lane % HEAD_DIM
        tile_lane = _iota((c, LANES), 1)
        self.tile_head = [tile_lane // HEAD_DIM == h for h in range(LANES // HEAD_DIM)]
        self.eye = (row == col).astype(F32)
        self.strict = row > col
        self.incl = row >= col
        self.inv_blk = (row // INV_BLOCK) == (col // INV_BLOCK)
        r2, c2 = _iota((w, w), 0), _iota((w, w), 1)
        self.eye_full = r2 == c2
        self.same_head = (r2 // HEAD_DIM) == (c2 // HEAD_DIM)

    def bd(self, y):
        yb = y.astype(BF16)
        zero = jnp.zeros((yb.shape[0], LANES), BF16)
        blocks = []
        for t in range(W_MIX // LANES):
            tile = yb[:, t * LANES:(t + 1) * LANES]
            for m in self.tile_head:
                kept = jnp.where(m, tile, zero)
                blocks.append(jnp.concatenate([kept if s == t else zero for s in range(W_MIX // LANES)], axis=1))
        return jnp.concatenate(blocks, axis=0)

    def nn(self, x, bd_y, out=F32):
        return lax.dot_general(x.astype(BF16), bd_y, _NN, preferred_element_type=F32).astype(out)

    def nt(self, x, bd_y):
        return lax.dot_general(x.astype(BF16), bd_y, _NT, preferred_element_type=F32)

    def tn_bd(self, x, y):
        full = lax.dot_general(x.astype(BF16), y.astype(BF16), _TN, preferred_element_type=F32)
        return jnp.where(self.same_head, full, 0.0)

    def diag_bd(self, row_vec):
        return jnp.where(self.eye_full, row_vec, 0.0)

    def plus_eye(self, bd_p):
        return jnp.where(self.eye_full, jnp.ones_like(bd_p), bd_p)

    def inv_unit_lower_many(self, mats):
        mats = [a.astype(BF16) for a in mats]
        zero = jnp.zeros_like(mats[0])
        eye = self.eye.astype(BF16)
        d = [jnp.where(self.inv_blk, a, zero) for a in mats]
        nb = [jnp.where(self.inv_blk, zero, a) for a in mats]
        td = [eye - di for di in d]
        p = d
        bdp = [self.bd(pi) for pi in p]
        for _ in range(int(math.log2(INV_BLOCK)) - 1):
            p = [self.nn(pi, bi, BF16) for pi, bi in zip(p, bdp)]
            bdp = [self.bd(pi) for pi in p]
            td = [self.nn(ti, self.plus_eye(bi), BF16) for ti, bi in zip(td, bdp)]
        bd_td = [self.bd(ti) for ti in td]
        x = [self.nn(ti, self.bd(ni), BF16) for ti, ni in zip(td, nb)]
        t = [eye - xi for xi in x]
        p = x
        bdp = [self.bd(pi) for pi in p]
        for _ in range(int(math.log2(HEAD_DIM // INV_BLOCK)) - 1):
            p = [self.nn(pi, bi, BF16) for pi, bi in zip(p, bdp)]
            bdp = [self.bd(pi) for pi in p]
            t = [self.nn(ti, self.plus_eye(bi), BF16) for ti, bi in zip(t, bdp)]
        return [self.nn(ti, bi, BF16) for ti, bi in zip(t, bd_td)]


def _ada_kernel(cp_ref, cs_ref, w_ref, b_ref, op_ref, os_ref):
    w = _Split(w_ref[...], 3)
    op_ref[...] = _dotp(_silu(cp_ref[...]), w, _NN, 3) + b_ref[...]
    os_ref[...] = _dotp(_silu(cs_ref[...]), w, _NN, 3) + b_ref[...]


def _ada_call(c_prompt, c_sample, ada_w, ada_b):
    n_layers, d, d3 = ada_w.shape
    n_p, n_s = c_prompt.shape[0], c_sample.shape[0]
    return pl.pallas_call(
        _ada_kernel,
        grid=(n_layers, d3 // d),
        in_specs=[
            pl.BlockSpec((n_p, d), lambda l, j: (0, 0)),
            pl.BlockSpec((n_s, d), lambda l, j: (0, 0)),
            pl.BlockSpec((None, d, d), lambda l, j: (l, 0, j)),
            pl.BlockSpec((None, 1, d), lambda l, j: (l, 0, j)),
        ],
        out_specs=[pl.BlockSpec((None, n_p, d), lambda l, j: (l, 0, j)),
                   pl.BlockSpec((None, n_s, d), lambda l, j: (l, 0, j))],
        out_shape=[jax.ShapeDtypeStruct((n_layers, n_p, d3), F32), jax.ShapeDtypeStruct((n_layers, n_s, d3), F32)],
        compiler_params=pltpu.CompilerParams(dimension_semantics=("arbitrary", "arbitrary"),
                                             vmem_limit_bytes=VMEM_LIMIT),
        name="ada_mod",
    )(c_prompt, c_sample, ada_w, ada_b.reshape(n_layers, 1, d3))


def _modulated_norm(x, g, scale, shift):
    return _rms(x) * g * (1.0 + scale) + shift


def _inproj_kernel(x_ref, sc_ref, sh_ref, g_ref, w_ref, wz_ref, wba_ref, oa_ref, ob_ref, oc_ref, od_ref):
    h = _modulated_norm(x_ref[...], g_ref[...], sc_ref[...], sh_ref[...]).astype(BF16)
    proj = lambda w: lax.dot_general(h, w, _NT, preferred_element_type=F32)
    lo = 0
    for o_ref in (oa_ref, ob_ref, oc_ref):
        wd = o_ref.shape[-1]
        o_ref[...] = proj(w_ref[0, lo:lo + wd, :])
        lo += wd
    od_ref[:, :D_QKV_W] = proj(w_ref[0, lo:lo + D_QKV_W, :])
    od_ref[:, D_QKV_W:D_QKV_W + W_MIX] = proj(wz_ref[0])
    od_ref[:, D_QKV_W + W_MIX:] = proj(wba_ref[...])


def _weight_rows(layer, row0, n_rows, d):
    return pl.BlockSpec((pl.Element(1), pl.Element(n_rows), pl.Element(d)), lambda i: (layer, row0, 0))


MOD_SHIFT, MOD_SCALE, MOD_GATE = 0, 1, 2


def _mod_specs(mods, layer, parts, d, tm, seq_len):
    if seq_len == 1:
        return [mods] * len(parts), [pl.BlockSpec((None, tm, d), lambda i, c=c: (layer, i, c)) for c in parts]
    per_seq = seq_len // tm
    by_seq = mods.reshape(mods.shape[0], mods.shape[1], 1, mods.shape[2])
    return ([by_seq] * len(parts),
            [pl.BlockSpec((None, None, 1, d), lambda i, c=c: (layer, i // per_seq, 0, c)) for c in parts])


class _LayerParam:
    def __init__(self, arr, layer):
        self.arr, self.layer = arr, layer

    def spec(self):
        return pl.BlockSpec((None,) + self.arr.shape[1:], lambda *_: (self.layer,) + (0,) * (self.arr.ndim - 1))


def _spec_of(a):
    if isinstance(a, _LayerParam):
        return a.spec()
    return pl.BlockSpec(a.shape, lambda *_: (0,) * a.ndim)


def _arr_of(a):
    return a.arr if isinstance(a, _LayerParam) else a


def _inproj_call(x2, mods_all, g, l, w_t, w_ba, seq_len):
    m, d = x2.shape
    tm = min(PROJ_TILE, m, seq_len) if seq_len > 1 else m
    widths = (A_W, B_W, C_W, D_PACK_W)
    mods, mod_specs = _mod_specs(mods_all, l, (MOD_SCALE, MOD_SHIFT), d, tm, seq_len)
    off_ba = A_W + B_W + C_W + D_QKV_W
    off_z = off_ba + 2 * N_HEADS
    return pl.pallas_call(
        _inproj_kernel,
        grid=(m // tm,),
        in_specs=[pl.BlockSpec((tm, d), lambda i: (i, 0))] + mod_specs + [
            _spec_of(g),
            _weight_rows(l, 0, off_ba, d),
            _weight_rows(l, off_z, W_MIX, d),
            _spec_of(w_ba),
        ],
        out_specs=[pl.BlockSpec((tm, wd), lambda i: (i, 0)) for wd in widths],
        out_shape=[jax.ShapeDtypeStruct((m, wd), F32) for wd in widths],
        compiler_params=pltpu.CompilerParams(dimension_semantics=("parallel",), vmem_limit_bytes=VMEM_LIMIT),
        name="in_proj",
    )(x2, *mods, _arr_of(g), w_t, w_t, _arr_of(w_ba))


def _outproj_kernel(x_ref, sc_ref, sh_ref, gt_ref, g_ref, ba_ref, bb_ref, bc_ref, bd_ref,
                    wg_ref, wup_ref, wout_ref, fg_ref, o_ref, *, final):
    x = x_ref[...]
    d = x.shape[-1]
    h = _modulated_norm(x, g_ref[...], sc_ref[...], sh_ref[...]).astype(BF16)
    merged = jnp.zeros(x.shape, F32)
    for n, br_ref in enumerate((ba_ref, bb_ref, bc_ref, bd_ref)):
        gl = lax.dot_general(h, wg_ref[0, n * d:(n + 1) * d, :], _NT, preferred_element_type=F32)
        up = jnp.dot(br_ref[...].astype(BF16), wup_ref[n], preferred_element_type=F32)
        merged = merged + jax.nn.sigmoid(gl) * up
    out = jnp.dot(merged.astype(BF16), wout_ref[...], preferred_element_type=F32)
    xn = x + gt_ref[...] * out
    if final:
        xn = _rms(xn) * fg_ref[...]
    o_ref[...] = xn


def _outproj_call(x2, mods_all, g, branches, l, w_t, wup, wout, final_g, seq_len, final):
    m, d = x2.shape
    tm = min(OUT_TILE, m, seq_len) if seq_len > 1 else m
    mods, mod_specs = _mod_specs(mods_all, l, (MOD_SCALE, MOD_SHIFT, MOD_GATE), d, tm, seq_len)
    full = lambda a: pl.BlockSpec(a.shape, lambda i: (0,) * a.ndim)
    off_g = A_W + B_W + C_W + D_W
    gate_rows = _weight_rows(l, off_g, N_BRANCH * d, d)
    return pl.pallas_call(
        functools.partial(_outproj_kernel, final=final),
        grid=(m // tm,),
        in_specs=[pl.BlockSpec((tm, d), lambda i: (i, 0))] + mod_specs + [_spec_of(g)]
        + [pl.BlockSpec((tm, W_MIX), lambda i: (i, 0)) for _ in branches]
        + [gate_rows, _spec_of(wup), _spec_of(wout), pl.BlockSpec((1, d), lambda i: (0, 0))],
        out_specs=pl.BlockSpec((tm, d), lambda i: (i, 0)),
        out_shape=jax.ShapeDtypeStruct((m, d), F32),
        compiler_params=pltpu.CompilerParams(dimension_semantics=("parallel",), vmem_limit_bytes=VMEM_LIMIT),
        name="out_proj",
    )(x2, *mods, _arr_of(g), *branches, w_t, _arr_of(wup), _arr_of(wout), final_g.reshape(1, d))


def _rwkv_token_math(pm, w0, w2, a0, a2, k_k, k_a, ones):
    r = pm[:, 0:W_MIX]
    k = pm[:, W_MIX:2 * W_MIX]
    v = pm[:, 2 * W_MIX:3 * W_MIX]
    wd = pm[:, 3 * W_MIX:3 * W_MIX + LORA]
    ad = pm[:, 3 * W_MIX + LORA:]
    w_log = -_softplus(-(w0 + _dotp(jnp.tanh(wd), w2, _NN, P_MISC))) - 0.5
    log_decay = -jnp.exp(w_log)
    a = jax.nn.sigmoid(a0 + _dotp(ad, a2, _NN, P_MISC))
    kx = k * k_k
    kk = kx * lax.rsqrt(_head_sum(kx * kx, ones) + EPS)
    k = k * (1.0 + (a - 1.0) * k_a)
    return r, k, v, log_decay, -kk, kk * a


def _rwkv_finish(o, r, k, v, z, r_k, ln_g, ln_b, ones):
    mean = _head_sum(o, ones, signed=True) * (1.0 / HEAD_DIM)
    dlt = o - mean
    var = _head_sum(dlt * dlt, ones) * (1.0 / HEAD_DIM)
    on = dlt * lax.rsqrt(var + RWKV_GN_EPS) * ln_g + ln_b
    bonus = _head_sum(r * k * r_k, ones, signed=True) * v
    return (on + bonus) * _silu(z)


def _swap_halves(x):
    half = HEAD_DIM // 2
    n = x.shape[-1]
    first = (_iota(x.shape, 1) & half) == 0
    return jnp.where(first, pltpu.roll(x, n - half, axis=1), pltpu.roll(x, half, axis=1))


def _rotary(x, cos, sin):
    return x * cos + _swap_halves(x) * sin


def _lru_token_math(xc, gate_w, gate_b, lam):
    gates = _dotp(xc, gate_w, _NN, P_MISC) + gate_b
    r_gate = jax.nn.sigmoid(gates[:, :W_MIX])
    i_gate = jax.nn.sigmoid(gates[:, W_MIX:])
    log_a = -LRU_C * r_gate * _softplus(-lam)
    a = jnp.exp(log_a)
    b = jnp.sqrt(1.0 - jnp.exp(2.0 * log_a)) * (i_gate * xc)
    return a, b


def _gdn_token_math(qkv, b_raw, a_raw, a_log, dt_bias, ones):
    qkv = _silu(qkv)
    q = qkv[:, 0:W_MIX]
    k = qkv[:, W_MIX:2 * W_MIX]
    v = qkv[:, 2 * W_MIX:]
    q = q * lax.rsqrt(_head_sum(q * q, ones) + EPS) * (HEAD_DIM ** -0.5)
    k = k * lax.rsqrt(_head_sum(k * k, ones) + EPS)
    beta = jax.nn.sigmoid(b_raw)
    g = -jnp.exp(a_log) * _softplus(a_raw + dt_bias)
    return q, k, v, beta, g


def _head_rms_finish(o, z, ones, gain=None):
    y = o * lax.rsqrt(_head_sum(o * o, ones) * (1.0 / HEAD_DIM) + EPS)
    if gain is not None:
        y = y * gain
    return y * _silu(z)


def _conv_tile(u, ext_ref, w_ref, first):
    n = u.shape[0]

    @pl.when(first)
    def _():
        ext_ref[0:SUBLANES, :] = jnp.zeros((SUBLANES, u.shape[1]), F32)

    ext_ref[SUBLANES:SUBLANES + n, :] = u
    out = None
    for j in range(CONV_W):
        term = _rows_back(u, ext_ref, CONV_W - 1 - j) * w_ref[j:j + 1, :]
        out = term if out is None else out + term
    ext_ref[0:SUBLANES, :] = u[n - SUBLANES:n, :]
    return out


def _rows_back(u, ext_ref, back):
    if back == 0:
        return u
    n, ch = u.shape
    tiles = (n // SUBLANES, SUBLANES, ch)
    pos = _iota((1, SUBLANES, 1), 1)
    earlier = ext_ref[0:n, :].reshape(tiles)
    return pltpu.roll(jnp.where(pos >= SUBLANES - back, earlier, u.reshape(tiles)), back, axis=1).reshape(n, ch)


def _rwkv_kernel(p_ref, mu_ref, w0_ref, w2_ref, a0_ref, a2_ref, kk_ref, ka_ref, rk_ref, lng_ref, lnb_ref,
                 o_ref, s_out_ref, shift_out_ref, s_scr, ext_scr):
    j = pl.program_id(1)
    last = pl.num_programs(1) - 1
    ct = p_ref.shape[0]

    @pl.when(j == 0)
    def _():
        s_scr[...] = jnp.zeros(s_scr.shape, F32)
        ext_scr[0:SUBLANES, :] = jnp.zeros((SUBLANES, A_SHIFT_W), F32)

    p = p_ref[...]
    pa = p[:, :A_SHIFT_W]
    z = p[:, A_SHIFT_W:]
    ext_scr[SUBLANES:SUBLANES + ct, :] = pa
    prev = _rows_back(pa, ext_scr, 1)
    ext_scr[0:SUBLANES, :] = pa[ct - SUBLANES:ct, :]
    pm = pa + (prev - pa) * mu_ref[...]
    ones = _head_ones()
    r, k, v, ld, av, bv = _rwkv_token_math(pm, w0_ref[...], w2_ref[...], a0_ref[...], a2_ref[...],
                                           kk_ref[...], ka_ref[...], ones)

    c = min(CHUNK, ct)
    ha = _HeadAlgebra(c)
    lt = (_iota((c, c), 0) >= _iota((c, c), 1)).astype(BF16)
    units = []
    for c0 in range(0, ct, c):
        sl = slice(c0, c0 + c)
        ldc = ld[sl]
        cum = _dot_const(ldc, lt, _NN, CUMSUM_PIECES, const_left=True)
        e_neg = jnp.exp(-cum)
        e_out = jnp.exp(cum[c - 1:c, :] - cum)
        units.append(dict(a=av[sl] * jnp.exp(cum - ldc), r=r[sl] * jnp.exp(cum), b=bv[sl] * e_neg, k=k[sl] * e_neg,
                          bo=bv[sl] * e_out, ko=k[sl] * e_out, v=v[sl], g=jnp.exp(cum[c - 1:c, :])))
    all_units = units
    state = s_scr[...]
    o_rows = []
    for w0 in range(0, len(all_units), WAVE):
        units = all_units[w0:w0 + WAVE]
        for u in units:
            lhs = jnp.concatenate([u["a"], u["r"]], axis=0)
            u["mb"] = ha.nt(lhs, ha.bd(u["b"]))
            u["mk"] = ha.nt(lhs, ha.bd(u["k"]))
            u["bd_v"] = ha.bd(u["v"])
        for u in units:
            u["m_ab"] = jnp.where(ha.strict, u["mb"][:c], 0.0)
            u["m_ak"] = jnp.where(ha.strict, u["mk"][:c], 0.0)
            u["m_rb"] = jnp.where(ha.incl, u["mb"][c:], 0.0)
            u["m_rk"] = jnp.where(ha.incl, u["mk"][c:], 0.0)
        for u, t_inv in zip(units, ha.inv_unit_lower_many([-u["m_ab"] for u in units])):
            u["t_inv"] = t_inv
        for u in units:
            u["makv"] = ha.nn(u["m_ak"], u["bd_v"], BF16)
        for u in units:
            u["a_hat"] = ha.nn(u["t_inv"], ha.bd(u["a"]), BF16)
            u["u1"] = ha.nn(u["t_inv"], ha.bd(u["makv"]), BF16)
        for u in units:
            u["r_hat"] = u["r"] + ha.nn(u["m_rb"], ha.bd(u["a_hat"]))
            u["o1"] = ha.nn(u["m_rb"], ha.bd(u["u1"])) + ha.nn(u["m_rk"], u["bd_v"])
            u["g_t"] = ha.diag_bd(u["g"]) + ha.tn_bd(u["bo"], u["a_hat"])
            u["h_t"] = ha.tn_bd(jnp.concatenate([u["bo"], u["ko"]], axis=0),
                                jnp.concatenate([u["u1"], u["v"].astype(BF16)], axis=0))
            zz = lax.dot_general(jnp.concatenate([u["r_hat"], u["g_t"]], axis=0).astype(BF16), state.astype(BF16),
                                 _NN, preferred_element_type=F32)
            o_rows.append(zz[:c] + u["o1"])
            state = zz[c:] + u["h_t"]
    o = o_rows[0] if len(o_rows) == 1 else jnp.concatenate(o_rows, axis=0)
    s_scr[...] = state
    o_ref[...] = _rwkv_finish(o, r, k, v, z, rk_ref[...], lng_ref[...], lnb_ref[...], ones)

    @pl.when(j == last)
    def _():
        eye_h = (_iota((HEAD_DIM, HEAD_DIM), 0) == _iota((HEAD_DIM, HEAD_DIM), 1)).astype(F32)
        for h in range(N_HEADS):
            hs = slice(h * HEAD_DIM, (h + 1) * HEAD_DIM)
            s_out_ref[h] = _mm_nt(eye_h, state[hs, hs])
        shift_out_ref[...] = pa[ct - 1:ct, :]


def _row(a):
    return a if isinstance(a, _LayerParam) else a.reshape(1, -1)


def _rwkv_call(p_a, n_seq, seq_len, lp):
    ct = min(ROW_TILE, seq_len)
    p3 = p_a.reshape(n_seq, seq_len, A_W)
    params = [_row(lp["rwkv_mu"]), _row(lp["rwkv_w0"]), lp["rwkv_w2"], _row(lp["rwkv_a0"]), lp["rwkv_a2"],
              _row(lp["rwkv_k_k"]), _row(lp["rwkv_k_a"]), _row(lp["rwkv_r_k"]), _row(lp["rwkv_ln_g"]),
              _row(lp["rwkv_ln_b"])]
    o, s1, shift1 = pl.pallas_call(
        _rwkv_kernel,
        grid=(n_seq, seq_len // ct),
        in_specs=[pl.BlockSpec((None, ct, A_W), lambda b, j: (b, j, 0))]
        + [_spec_of(a) for a in params],
        out_specs=[
            pl.BlockSpec((None, ct, W_MIX), lambda b, j: (b, j, 0)),
            pl.BlockSpec((None, N_HEADS, HEAD_DIM, HEAD_DIM), lambda b, j: (b, 0, 0, 0)),
            pl.BlockSpec((None, 1, A_SHIFT_W), lambda b, j: (b, 0, 0)),
        ],
        out_shape=[
            jax.ShapeDtypeStruct((n_seq, seq_len, W_MIX), F32),
            jax.ShapeDtypeStruct((n_seq, N_HEADS, HEAD_DIM, HEAD_DIM), F32),
            jax.ShapeDtypeStruct((n_seq, 1, A_SHIFT_W), F32),
        ],
        scratch_shapes=[pltpu.VMEM((W_MIX, W_MIX), F32), pltpu.VMEM((ct + SUBLANES, A_SHIFT_W), F32)],
        compiler_params=pltpu.CompilerParams(dimension_semantics=("parallel", "arbitrary"),
                                             vmem_limit_bytes=VMEM_LIMIT),
        name="rwkv7_prompt",
    )(p3, *[_arr_of(a) for a in params])
    return o.reshape(n_seq * seq_len, W_MIX), s1, shift1.reshape(n_seq, A_SHIFT_W)


def _ret_kernel(p_ref, cos_ref, sin_ref, o_ref, s_out_ref, s_scr):
    j = pl.program_id(1)
    last = pl.num_programs(1) - 1
    ct = p_ref.shape[0]

    @pl.when(j == 0)
    def _():
        s_scr[...] = jnp.zeros(s_scr.shape, F32)

    p = p_ref[...]
    cos, sin = cos_ref[...], sin_ref[...]
    q = _rotary(p[:, 0:W_MIX], cos, sin)
    k = _rotary(p[:, W_MIX:2 * W_MIX], cos, sin) * (HEAD_DIM ** -0.5)
    v = p[:, 2 * W_MIX:3 * W_MIX]
    z = p[:, 3 * W_MIX:]
    c = min(RET_CHUNK, ct)
    ri, ci = _iota((c, c), 0), _iota((c, c), 1)
    causal = ri >= ci
    rel = jnp.where(causal, ri - ci, 0).astype(F32)
    idx = _iota((c, 1), 0).astype(F32)
    states = [s_scr[h] for h in range(N_HEADS)]
    units = []
    for h in range(N_HEADS):
        lg = math.log(1.0 - 2.0 ** (-5.0 - h))
        consts = dict(decay=jnp.where(causal, jnp.exp(lg * rel), 0.0), q_dec=jnp.exp(lg * (idx + 1.0)),
                      k_dec=jnp.exp(lg * (c - 1.0 - idx)), g_c=math.exp(lg * c))
        hs = slice(h * HEAD_DIM, (h + 1) * HEAD_DIM)
        for c0 in range(0, ct, c):
            sl = slice(c0, c0 + c)
            units.append(dict(consts, h=h, q=_Split(q[sl, hs], 1), k=k[sl, hs], v=_Split(v[sl, hs], 1)))
    for u in units:
        u["s_in"] = _dotp(u["q"], u["k"], _NT, P_MISC) * u["decay"]
        u["kv"] = _dotp(u["k"] * u["k_dec"], u["v"], _TN, P_MISC)
    for u in units:
        u["o"] = _dotp(u["s_in"], u["v"], _NN, P_MISC)
    for u in units:
        u["s0"] = states[u["h"]]
        states[u["h"]] = u["s0"] * u["g_c"] + u["kv"]
    o_heads_all = [[] for _ in range(N_HEADS)]
    for u in units:
        o_heads_all[u["h"]].append(u["o"] + _dotp(u["q"], u["s0"], _NN, P_MISC) * u["q_dec"])
    cols = [oh[0] if len(oh) == 1 else jnp.concatenate(oh, axis=0) for oh in o_heads_all]
    o = jnp.concatenate(cols, axis=1)
    for h in range(N_HEADS):
        s_scr[h] = states[h]
    o_ref[...] = _head_rms_finish(o, z, _head_ones())

    @pl.when(j == last)
    def _():
        for h in range(N_HEADS):
            s_out_ref[h] = states[h]


def _rope_tables(pos):
    half = HEAD_DIM // 2
    inv = ROPE_BASE ** (-jnp.arange(half, dtype=F32) / half)
    ang = pos.astype(F32)[:, None] * inv[None, :]
    cos, sin = jnp.cos(ang), jnp.sin(ang)
    cos_t = jnp.tile(jnp.concatenate([cos, cos], axis=-1), (1, N_HEADS))
    sin_t = jnp.tile(jnp.concatenate([-sin, sin], axis=-1), (1, N_HEADS))
    return cos_t, sin_t


def _ret_call(p_b, n_seq, seq_len, cos_t, sin_t):
    ct = min(ROW_TILE, seq_len)
    p3 = p_b.reshape(n_seq, seq_len, B_W)
    o, s1 = pl.pallas_call(
        _ret_kernel,
        grid=(n_seq, seq_len // ct),
        in_specs=[
            pl.BlockSpec((None, ct, B_W), lambda b, j: (b, j, 0)),
            pl.BlockSpec((ct, W_MIX), lambda b, j: (j, 0)),
            pl.BlockSpec((ct, W_MIX), lambda b, j: (j, 0)),
        ],
        out_specs=[
            pl.BlockSpec((None, ct, W_MIX), lambda b, j: (b, j, 0)),
            pl.BlockSpec((None, N_HEADS, HEAD_DIM, HEAD_DIM), lambda b, j: (b, 0, 0, 0)),
        ],
        out_shape=[
            jax.ShapeDtypeStruct((n_seq, seq_len, W_MIX), F32),
            jax.ShapeDtypeStruct((n_seq, N_HEADS, HEAD_DIM, HEAD_DIM), F32),
        ],
        scratch_shapes=[pltpu.VMEM((N_HEADS, HEAD_DIM, HEAD_DIM), F32)],
        compiler_params=pltpu.CompilerParams(dimension_semantics=("parallel", "arbitrary"),
                                             vmem_limit_bytes=VMEM_LIMIT),
        name="retention_prompt",
    )(p3, cos_t, sin_t)
    return o.reshape(n_seq * seq_len, W_MIX), s1


def _affine_scan(a, b, span):
    n, w = a.shape
    if span == SUBLANES and n > span:
        shape, axis = (n // span, span, w), 1
        a, b = a.reshape(shape), b.reshape(shape)
        pos = _iota((1, span, 1), 1)
    else:
        assert span >= n
        axis = 0
        pos = _iota((n, 1), 0)
    dist = 1
    while dist < span:
        keep = pos >= dist
        a_prev = jnp.where(keep, pltpu.roll(a, dist, axis=axis), 1.0)
        b_prev = jnp.where(keep, pltpu.roll(b, dist, axis=axis), 0.0)
        b = a * b_prev + b
        a = a * a_prev
        dist *= 2
    return a.reshape(n, w), b.reshape(n, w)


def _lru_kernel(p_ref, cw_ref, cb_ref, gw_ref, gb_ref, sp_ref, o_ref, h_out_ref, conv_out_ref,
                ext_scr, h_scr, ab_scr, hin_scr):
    j = pl.program_id(1)
    last = pl.num_programs(1) - 1
    ct = p_ref.shape[0]

    @pl.when(j == 0)
    def _():
        h_scr[...] = jnp.zeros(h_scr.shape, F32)

    p = p_ref[...]
    xr = p[:, :W_MIX]
    z = p[:, W_MIX:]
    xc = _conv_tile(xr, ext_scr, cw_ref, j == 0) + cb_ref[...]
    a, b = _lru_token_math(xc, gw_ref[...], gb_ref[...], sp_ref[...])
    n_grp = ct // SUBLANES
    a, b = _affine_scan(a, b, SUBLANES)
    n_tiles = W_MIX // LANES
    for t in range(n_tiles):
        ab_scr[t] = a[:, t * LANES:(t + 1) * LANES]
        ab_scr[n_tiles + t] = b[:, t * LANES:(t + 1) * LANES]
    ends = pl.ds(SUBLANES - 1, n_grp, stride=SUBLANES)
    a_end = jnp.concatenate([ab_scr[t, ends, :] for t in range(n_tiles)], axis=1)
    b_end = jnp.concatenate([ab_scr[n_tiles + t, ends, :] for t in range(n_tiles)], axis=1)
    a_end, b_end = _affine_scan(a_end, b_end, n_grp)
    h_prev = h_scr[...]
    h_end = a_end * h_prev + b_end
    grp = _iota((n_grp, 1), 0)
    hin_scr[...] = jnp.where(grp == 0, h_prev, pltpu.roll(h_end, 1, axis=0))
    h_in = jnp.concatenate([jnp.broadcast_to(hin_scr[g:g + 1, :], (SUBLANES, W_MIX)) for g in range(n_grp)], axis=0)
    hcur = a * h_in + b
    h_scr[...] = h_end[n_grp - 1:n_grp, :]
    o_ref[...] = hcur * _silu(z)

    @pl.when(j == last)
    def _():
        h_out_ref[...] = hcur[ct - 1:ct, :]
        conv_out_ref[...] = xr[ct - SUBLANES:ct, :]


def _block_diag_gates(gate_w):
    out = jnp.zeros((gate_w.shape[0], W_MIX, 2 * W_MIX), F32)
    for g in range(2):
        for n in range(N_HEADS):
            out = out.at[:, n * HEAD_DIM:(n + 1) * HEAD_DIM,
                         g * W_MIX + n * HEAD_DIM:g * W_MIX + (n + 1) * HEAD_DIM].set(gate_w[:, g, n])
    return out


def _lru_params(lp):
    return [lp["lru_conv_w"], lp["lru_conv_b"], lp["lru_gate_w"], lp["lru_gate_b"], lp["lru_lambda"]]


def _lru_call(p_c, n_seq, seq_len, lp):
    ct = min(LRU_TILE, seq_len)
    p3 = p_c.reshape(n_seq, seq_len, C_W)
    params = _lru_params(lp)
    o, h1, conv_tail = pl.pallas_call(
        _lru_kernel,
        grid=(n_seq, seq_len // ct),
        in_specs=[pl.BlockSpec((None, ct, C_W), lambda b, j: (b, j, 0))]
        + [_spec_of(a) for a in params],
        out_specs=[
            pl.BlockSpec((None, ct, W_MIX), lambda b, j: (b, j, 0)),
            pl.BlockSpec((None, 1, W_MIX), lambda b, j: (b, 0, 0)),
            pl.BlockSpec((None, SUBLANES, W_MIX), lambda b, j: (b, 0, 0)),
        ],
        out_shape=[
            jax.ShapeDtypeStruct((n_seq, seq_len, W_MIX), F32),
            jax.ShapeDtypeStruct((n_seq, 1, W_MIX), F32),
            jax.ShapeDtypeStruct((n_seq, SUBLANES, W_MIX), F32),
        ],
        scratch_shapes=[pltpu.VMEM((ct + SUBLANES, W_MIX), F32), pltpu.VMEM((1, W_MIX), F32),
                        pltpu.VMEM((2 * W_MIX // LANES, ct, LANES), F32),
                        pltpu.VMEM((ct // SUBLANES, W_MIX), F32)],
        compiler_params=pltpu.CompilerParams(dimension_semantics=("parallel", "arbitrary"),
                                             vmem_limit_bytes=VMEM_LIMIT),
        name="rglru_prompt",
    )(p3, *[_arr_of(a) for a in params])
    return (o.reshape(n_seq * seq_len, W_MIX), h1.reshape(n_seq, W_MIX),
            conv_tail[:, SUBLANES - (CONV_W - 1):, :])


def _gdn_kernel(p_ref, cw_ref, nal_ref, dtb_ref, ng_ref, o_ref, s_out_ref, conv_out_ref, ext_scr, s_scr):
    j = pl.program_id(1)
    last = pl.num_programs(1) - 1
    ct = p_ref.shape[0]

    @pl.when(j == 0)
    def _():
        s_scr[...] = jnp.zeros(s_scr.shape, F32)

    p = p_ref[...]
    raw = p[:, :D_QKV_W]
    z = p[:, D_QKV_W:D_QKV_W + W_MIX]
    b_raw = p[:, D_QKV_W + W_MIX:D_QKV_W + 2 * W_MIX]
    a_raw = p[:, D_QKV_W + 2 * W_MIX:]
    ones = _head_ones()
    qkv = _conv_tile(raw, ext_scr, cw_ref, j == 0)
    q, k, v, beta, g = _gdn_token_math(qkv, b_raw, a_raw, nal_ref[...], dtb_ref[...], ones)

    c = min(CHUNK, ct)
    ha = _HeadAlgebra(c)
    lt = (_iota((c, c), 0) >= _iota((c, c), 1)).astype(BF16)
    units = []
    for c0 in range(0, ct, c):
        sl = slice(c0, c0 + c)
        gc = _dot_const(g[sl], lt, _NN, CUMSUM_PIECES, const_left=True)
        gc_cols = jnp.sum(gc * ha.eye, axis=0, keepdims=True)
        diff = gc - gc_cols
        decay = jnp.where(ha.incl, jnp.exp(jnp.where(ha.incl, diff, 0.0)), 0.0)
        kb = k[sl] * beta[sl]
        e_gc = jnp.exp(gc)
        g_last = gc[c - 1:c, :]
        units.append(dict(decay=decay, kb=kb, q=q[sl], k=k[sl], vb=v[sl] * beta[sl], kbe=kb * e_gc,
                          k_out=k[sl] * jnp.exp(g_last - gc), q_in=q[sl] * e_gc, e_last=jnp.exp(g_last)))
    all_units = units
    state = s_scr[...]
    o_rows = []
    for w0 in range(0, len(all_units), WAVE):
        units = all_units[w0:w0 + WAVE]
        for u in units:
            kq = ha.nt(jnp.concatenate([u["kb"], u["q"]], axis=0), ha.bd(u["k"]))
            u["a_mat"] = jnp.where(ha.strict, kq[:c] * u["decay"], 0.0)
            u["qk"] = kq[c:] * u["decay"]
        for u, t_inv in zip(units, ha.inv_unit_lower_many([u["a_mat"] for u in units])):
            u["t_inv"] = t_inv
        for u in units:
            u["u"] = ha.nn(u["t_inv"], ha.bd(u["vb"]), BF16)
            u["w"] = ha.nn(u["t_inv"], ha.bd(u["kbe"]), BF16)
        for u in units:
            u["g_mat"] = ha.diag_bd(u["e_last"]) - ha.tn_bd(u["k_out"], u["w"])
            u["h_mat"] = ha.tn_bd(u["k_out"], u["u"])
            u["q_hat"] = u["q_in"] - ha.nn(u["qk"], ha.bd(u["w"]))
            u["o1"] = ha.nn(u["qk"], ha.bd(u["u"]))
            zz = lax.dot_general(jnp.concatenate([u["q_hat"], u["g_mat"]], axis=0).astype(BF16),
                                 state.astype(BF16), _NN, preferred_element_type=F32)
            o_rows.append(zz[:c] + u["o1"])
            state = zz[c:] + u["h_mat"]
    o = o_rows[0] if len(o_rows) == 1 else jnp.concatenate(o_rows, axis=0)
    s_scr[...] = state
    o_ref[...] = _head_rms_finish(o, z, ones, ng_ref[...])

    @pl.when(j == last)
    def _():
        for h in range(N_HEADS):
            hs = slice(h * HEAD_DIM, (h + 1) * HEAD_DIM)
            s_out_ref[h] = state[hs, hs]
        conv_out_ref[...] = raw[ct - SUBLANES:ct, :]


def _gdn_params(lp):
    return [lp["gdn_conv_w"], lp["gdn_A_log"], lp["gdn_dt_bias"], lp["gdn_norm_g"]]


def _gdn_call(p_d, n_seq, seq_len, lp):
    ct = min(ROW_TILE, seq_len)
    p3 = p_d.reshape(n_seq, seq_len, D_PACK_W)
    params = _gdn_params(lp)
    o, s1, conv_tail = pl.pallas_call(
        _gdn_kernel,
        grid=(n_seq, seq_len // ct),
        in_specs=[pl.BlockSpec((None, ct, D_PACK_W), lambda b, j: (b, j, 0))]
        + [_spec_of(a) for a in params],
        out_specs=[
            pl.BlockSpec((None, ct, W_MIX), lambda b, j: (b, j, 0)),
            pl.BlockSpec((None, N_HEADS, HEAD_DIM, HEAD_DIM), lambda b, j: (b, 0, 0, 0)),
            pl.BlockSpec((None, SUBLANES, D_QKV_W), lambda b, j: (b, 0, 0)),
        ],
        out_shape=[
            jax.ShapeDtypeStruct((n_seq, seq_len, W_MIX), F32),
            jax.ShapeDtypeStruct((n_seq, N_HEADS, HEAD_DIM, HEAD_DIM), F32),
            jax.ShapeDtypeStruct((n_seq, SUBLANES, D_QKV_W), F32),
        ],
        scratch_shapes=[pltpu.VMEM((ct + SUBLANES, D_QKV_W), F32), pltpu.VMEM((W_MIX, W_MIX), F32)],
        compiler_params=pltpu.CompilerParams(dimension_semantics=("parallel", "arbitrary"),
                                             vmem_limit_bytes=VMEM_LIMIT),
        name="gdn_prompt",
    )(p3, *[_arr_of(a) for a in params])
    return o.reshape(n_seq * seq_len, W_MIX), s1, conv_tail[:, SUBLANES - (CONV_W - 1):, :]


def _decode_pre_kernel(pa_ref, pb_ref, pc_ref, pd_ref, shift_ref, h0_ref, lconv_ref, gconv_ref, cos_ref, sin_ref,
                       mu_ref, w0_ref, w2_ref, a0_ref, a2_ref, kk_ref, ka_ref,
                       lcw_ref, lcb_ref, lgw_ref, lgb_ref, lsp_ref, gcw_ref, nal_ref, dtb_ref,
                       vt_ref, vn_ref, oc_ref, h1_ref, lconv1_ref, gconv1_ref):
    ones = _head_ones()
    pa_full = pa_ref[...]
    pa = pa_full[:, :A_SHIFT_W]
    pm = pa + (shift_ref[...] - pa) * mu_ref[...]
    r, k, v, ld, av, bv = _rwkv_token_math(pm, w0_ref[...], w2_ref[...], a0_ref[...], a2_ref[...],
                                           kk_ref[...], ka_ref[...], ones)
    vecs = [r, jnp.exp(ld), k, v, av, bv]
    plain = [r, k, v, pa_full[:, A_SHIFT_W:]]
    pb = pb_ref[...]
    cos, sin = cos_ref[...], sin_ref[...]
    vecs += [_rotary(pb[:, 0:W_MIX], cos, sin), _rotary(pb[:, W_MIX:2 * W_MIX], cos, sin) * (HEAD_DIM ** -0.5),
             pb[:, 2 * W_MIX:3 * W_MIX]]
    plain.append(pb[:, 3 * W_MIX:])
    pc = pc_ref[...]
    xr = pc[:, :W_MIX]
    taps = [lconv_ref[i] for i in range(CONV_W - 1)] + [xr]
    xc = taps[0] * lcw_ref[0:1, :]
    for i in range(1, CONV_W):
        xc = xc + taps[i] * lcw_ref[i:i + 1, :]
    xc = xc + lcb_ref[...]
    a, b = _lru_token_math(xc, lgw_ref[...], lgb_ref[...], lsp_ref[...])
    hcur = a * h0_ref[...] + b
    oc_ref[...] = hcur * _silu(pc[:, W_MIX:])
    h1_ref[...] = hcur
    for i in range(CONV_W - 1):
        lconv1_ref[i] = taps[i + 1]
    pd = pd_ref[...]
    raw = pd[:, :D_QKV_W]
    gtaps = [gconv_ref[i] for i in range(CONV_W - 1)] + [raw]
    qkv = gtaps[0] * gcw_ref[0:1, :]
    for i in range(1, CONV_W):
        qkv = qkv + gtaps[i] * gcw_ref[i:i + 1, :]
    q, kg, vg, beta, g = _gdn_token_math(qkv, pd[:, D_QKV_W + W_MIX:D_QKV_W + 2 * W_MIX],
                                         pd[:, D_QKV_W + 2 * W_MIX:], nal_ref[...], dtb_ref[...], ones)
    for i in range(CONV_W - 1):
        gconv1_ref[i] = gtaps[i + 1]
    vecs += [q, kg, vg, beta, g]
    plain.append(pd[:, D_QKV_W:D_QKV_W + W_MIX])
    assert len(vecs) == N_VEC_T and len(plain) == N_VEC_PLAIN
    for i, vec in enumerate(vecs):
        vt_ref[i] = vec.T
    for i, vec in enumerate(plain):
        vn_ref[i] = vec


def _decode_state_kernel(vt_ref, wkv_ref, ret_ref, gdn_ref, gam_ref, wkv1_ref, ret1_ref, gdn1_ref, o_ref):
    v_r, v_w, v_k, v_v, v_a, v_b, r_q, r_k, r_v, g_q, g_k, g_v, g_beta, g_g = range(N_VEC_T)
    hd = HEAD_DIM
    n = vt_ref.shape[-1]
    row = lambda idx, i: vt_ref[idx, pl.ds(i, 1), :]
    rows_of = lambda i: pl.ds(pl.multiple_of(i * hd, hd), hd)
    gamma = gam_ref[...]
    beta = vt_ref[g_beta, 0:1, :]
    eg = jnp.exp(vt_ref[g_g, 0:1, :])

    def first_pass(i, carry):
        acc_ret, acc_w, acc_q = carry
        rows = rows_of(i)
        s = wkv_ref[rows, :]
        sa = jnp.sum(s * vt_ref[v_a], axis=0, keepdims=True)
        s = s * vt_ref[v_w] + sa * vt_ref[v_b] + row(v_v, i) * vt_ref[v_k]
        wkv1_ref[rows, :] = s
        o_ref[0, pl.ds(i, 1), :] = jnp.sum(s * vt_ref[v_r], axis=0, keepdims=True)
        s = ret_ref[rows, :] * gamma + row(r_k, i) * vt_ref[r_v]
        ret1_ref[rows, :] = s
        acc_ret = acc_ret + row(r_q, i) * s
        s = gdn_ref[rows, :]
        return acc_ret, acc_w + row(g_k, i) * s, acc_q + row(g_q, i) * s

    zeros = jnp.zeros((hd, n), F32)
    acc_ret, acc_w, acc_q = lax.fori_loop(0, hd, first_pass, (zeros, zeros, zeros))
    o_ref[1] = acc_ret
    v_new = vt_ref[g_v] * beta - acc_w * (beta * eg)
    qk = jnp.sum(vt_ref[g_q] * vt_ref[g_k], axis=0, keepdims=True)
    o_ref[2] = acc_q * eg + qk * v_new

    def second_pass(i, carry):
        rows = rows_of(i)
        gdn1_ref[rows, :] = gdn_ref[rows, :] * eg + row(g_k, i) * v_new
        return carry

    lax.fori_loop(0, hd, second_pass, 0)


def _decode_finish_kernel(ot_ref, vn_ref, rk_ref, lng_ref, lnb_ref, ng_ref, oa_ref, ob_ref, od_ref):
    ones = _head_ones()
    r, k, v, z_a, z_b, z_d = (vn_ref[i] for i in range(N_VEC_PLAIN))
    oa_ref[...] = _rwkv_finish(ot_ref[0].T, r, k, v, z_a, rk_ref[...], lng_ref[...], lnb_ref[...], ones)
    ob_ref[...] = _head_rms_finish(ot_ref[1].T, z_b, ones)
    od_ref[...] = _head_rms_finish(ot_ref[2].T, z_d, ones, ng_ref[...])


def _batch_minor(state):
    n_layers, n = state.shape[:2]
    return jnp.transpose(state, (0, 2, 3, 4, 1)).reshape(n_layers, -1, n)


def _batch_major(flat_state):
    n = flat_state.shape[-1]
    return jnp.transpose(flat_state.reshape(N_HEADS, HEAD_DIM, HEAD_DIM, n), (3, 0, 1, 2))


def _decode_layer(l, p_a, p_b, p_c, p_d, carried, cos_t, sin_t, lp):
    n = p_a.shape[0]
    assert n % LANES == 0, "the decode state kernel keeps the batch on lanes"
    flat = HEAD_DIM * HEAD_DIM
    taps = CONV_W - 1
    rwkv_params = [_row(lp["rwkv_mu"]), _row(lp["rwkv_w0"]), lp["rwkv_w2"], _row(lp["rwkv_a0"]), lp["rwkv_a2"],
                   _row(lp["rwkv_k_k"]), _row(lp["rwkv_k_a"])]
    gdn_params = _gdn_params(lp)
    full = lambda a: pl.BlockSpec(a.shape, lambda i: (0,) * a.ndim)
    layer_blk = lambda a: pl.BlockSpec((None,) + a.shape[1:], lambda i, nd=a.ndim: (l,) + (0,) * (nd - 1))
    projs = [p_a, p_b, p_c, p_d]
    layered = [carried["shift"], carried["lru_h"], carried["lru_conv"], carried["gdn_conv"]]
    consts = [cos_t, sin_t, *rwkv_params, *_lru_params(lp), *gdn_params[:3]]
    out_shapes = [
        jax.ShapeDtypeStruct((N_VEC_T, W_MIX, n), F32),
        jax.ShapeDtypeStruct((N_VEC_PLAIN, n, W_MIX), F32),
        jax.ShapeDtypeStruct((n, W_MIX), F32),
        jax.ShapeDtypeStruct((n, W_MIX), F32),
        jax.ShapeDtypeStruct((taps, n, W_MIX), F32),
        jax.ShapeDtypeStruct((taps, n, D_QKV_W), F32),
    ]
    vec_t, vec_n, o_c, lru_h1, lru_conv1, gdn_conv1 = pl.pallas_call(
        _decode_pre_kernel,
        grid=(1,),
        in_specs=[full(a) for a in projs] + [layer_blk(a) for a in layered] + [_spec_of(a) for a in consts],
        out_specs=[pl.BlockSpec(s.shape, lambda i, nd=len(s.shape): (0,) * nd) for s in out_shapes],
        out_shape=out_shapes,
        compiler_params=pltpu.CompilerParams(dimension_semantics=("arbitrary",), vmem_limit_bytes=VMEM_LIMIT),
        name="decode_tokens",
    )(*projs, *layered, *[_arr_of(a) for a in consts])

    gam = jnp.broadcast_to((1.0 - 2.0 ** (-5.0 - jnp.arange(N_HEADS, dtype=F32)))[:, None, None], (N_HEADS, 1, n))
    state_in = pl.BlockSpec((None, flat, n), lambda h: (l, h, 0))
    state_out = pl.BlockSpec((flat, n), lambda h: (h, 0))
    wkv1, ret1, gdn1, o_t = pl.pallas_call(
        _decode_state_kernel,
        grid=(N_HEADS,),
        in_specs=[pl.BlockSpec((N_VEC_T, HEAD_DIM, n), lambda h: (0, h, 0)), state_in, state_in, state_in,
                  pl.BlockSpec((None, 1, n), lambda h: (h, 0, 0))],
        out_specs=[state_out, state_out, state_out, pl.BlockSpec((3, HEAD_DIM, n), lambda h: (0, h, 0))],
        out_shape=[jax.ShapeDtypeStruct((N_HEADS * flat, n), F32)] * 3 + [jax.ShapeDtypeStruct((3, W_MIX, n), F32)],
        compiler_params=pltpu.CompilerParams(dimension_semantics=("parallel",), vmem_limit_bytes=VMEM_LIMIT),
        name="decode_states",
    )(vec_t, carried["wkv"], carried["ret"], carried["gdn"], gam)

    finish_ins = [o_t, vec_n, _row(lp["rwkv_r_k"]), _row(lp["rwkv_ln_g"]), _row(lp["rwkv_ln_b"]), gdn_params[3]]
    o_a, o_b, o_d = pl.pallas_call(
        _decode_finish_kernel,
        grid=(1,),
        in_specs=[_spec_of(a) for a in finish_ins],
        out_specs=[pl.BlockSpec((n, W_MIX), lambda i: (0, 0))] * 3,
        out_shape=[jax.ShapeDtypeStruct((n, W_MIX), F32)] * 3,
        compiler_params=pltpu.CompilerParams(dimension_semantics=("arbitrary",), vmem_limit_bytes=VMEM_LIMIT),
        name="decode_finish",
    )(*[_arr_of(a) for a in finish_ins])
    new_states = (_batch_major(wkv1), p_a[:, :A_SHIFT_W], _batch_major(ret1), lru_h1,
                  jnp.transpose(lru_conv1, (1, 0, 2)), _batch_major(gdn1), jnp.transpose(gdn_conv1, (1, 0, 2)))
    return (o_a, o_b, o_c, o_d), new_states


def _prompt_layer(p_a, p_b, p_c, p_d, n_seq, seq_len, cos_t, sin_t, lp):
    o_a, wkv1, shift1 = _rwkv_call(p_a, n_seq, seq_len, lp)
    o_b, ret1 = _ret_call(p_b, n_seq, seq_len, cos_t, sin_t)
    o_c, lru_h1, lru_conv1 = _lru_call(p_c, n_seq, seq_len, lp)
    o_d, gdn1, gdn_conv1 = _gdn_call(p_d, n_seq, seq_len, lp)
    return (o_a, o_b, o_c, o_d), (wkv1, shift1, ret1, lru_h1, lru_conv1, gdn1, gdn_conv1)


def _run_group(x, mods, pos, carried, layers, final_g):
    n_seq, seq_len, d = x.shape
    x2 = x.reshape(n_seq * seq_len, d)
    cos_t, sin_t = _rope_tables(pos)
    new = []
    n_layers = len(layers)
    for l, lp in enumerate(layers):
        p_a, p_b, p_c, p_d = _inproj_call(x2, mods, lp["norm_g"], l, lp["w_t"], lp["w_ba"], seq_len)
        if carried is None:
            branches, st = _prompt_layer(p_a, p_b, p_c, p_d, n_seq, seq_len, cos_t, sin_t, lp)
        else:
            branches, st = _decode_layer(l, p_a, p_b, p_c, p_d, carried, cos_t, sin_t, lp)
        new.append(st)
        x2 = _outproj_call(x2, mods, lp["norm_g"], branches, l, lp["w_t"], lp["w_up_bf16"],
                           lp["w_out_bf16"], final_g, seq_len, final=(l == n_layers - 1))
    stacked = tuple(jnp.stack([s[i] for s in new], axis=0) for i in range(7))
    return x2.reshape(n_seq, seq_len, d), stacked


def kernel(x_prompt, x_sample, c_prompt, c_sample, state_rwkv_wkv, state_rwkv_shift, state_ret, state_lru_h, state_lru_conv, state_gdn, state_gdn_conv, ada_w, ada_b, norm_g, w_in, rwkv_mu, rwkv_w0, rwkv_w2, rwkv_a0, rwkv_a2, rwkv_k_k, rwkv_k_a, rwkv_r_k, rwkv_ln_g, rwkv_ln_b, lru_conv_w, lru_conv_b, lru_gate_w, lru_gate_b, lru_lambda, gdn_conv_w, gdn_A_log, gdn_dt_bias, gdn_norm_g, w_up, w_out, final_g):
    n_layers = ada_w.shape[0]
    n_prompt, seq_len, _ = x_prompt.shape
    n_sample, dec_len, _ = x_sample.shape
    assert dec_len == 1, "the decode path handles one token per sequence"
    w_in_t = jnp.swapaxes(w_in, 1, 2).astype(BF16)
    off_ba = A_W + B_W + C_W + D_QKV_W
    rows = lambda a: a.reshape(n_layers, 1, -1)
    stacked = dict(
        norm_g=rows(norm_g), w_up_bf16=w_up.astype(BF16), w_out_bf16=w_out.astype(BF16),
        w_ba=jnp.repeat(w_in_t[:, off_ba:off_ba + 2 * N_HEADS], HEAD_DIM, axis=1),
        rwkv_mu=rows(rwkv_mu), rwkv_w0=rows(rwkv_w0), rwkv_w2=rwkv_w2, rwkv_a0=rows(rwkv_a0), rwkv_a2=rwkv_a2,
        rwkv_k_k=rows(rwkv_k_k), rwkv_k_a=rows(rwkv_k_a), rwkv_r_k=rows(rwkv_r_k), rwkv_ln_g=rows(rwkv_ln_g),
        rwkv_ln_b=rows(rwkv_ln_b), lru_conv_w=lru_conv_w, lru_conv_b=rows(lru_conv_b),
        lru_gate_w=_block_diag_gates(lru_gate_w), lru_gate_b=rows(lru_gate_b), lru_lambda=rows(lru_lambda),
        gdn_conv_w=gdn_conv_w, gdn_A_log=rows(jnp.repeat(gdn_A_log, HEAD_DIM, axis=1)),
        gdn_dt_bias=rows(jnp.repeat(gdn_dt_bias, HEAD_DIM, axis=1)),
        gdn_norm_g=rows(jnp.tile(gdn_norm_g, (1, N_HEADS))))
    layers = [dict({name: _LayerParam(arr, l) for name, arr in stacked.items()}, w_t=w_in_t)
              for l in range(n_layers)]
    mods_p, mods_s = _ada_call(c_prompt, c_sample, ada_w, ada_b)

    y_prompt, new_p = _run_group(x_prompt, mods_p, jnp.arange(seq_len, dtype=jnp.int32), None, layers, final_g)
    carried = dict(wkv=_batch_minor(state_rwkv_wkv), ret=_batch_minor(state_ret), gdn=_batch_minor(state_gdn),
                   shift=state_rwkv_shift, lru_h=state_lru_h,
                   lru_conv=jnp.transpose(state_lru_conv, (0, 2, 1, 3)),
                   gdn_conv=jnp.transpose(state_gdn_conv, (0, 2, 1, 3)))
    pos_s = PAST_LEN + jnp.arange(dec_len, dtype=jnp.int32)
    y_sample, new_s = _run_group(x_sample, mods_s, pos_s, carried, layers, final_g)
    return (y_prompt, y_sample) + new_p + new_s
```

```python
import functools
import math

import jax
import jax.numpy as jnp
from jax import lax
from jax.experimental import pallas as pl
from jax.experimental.pallas import tpu as pltpu

F32 = jnp.float32
BF16 = jnp.bfloat16
HI = lax.Precision.HIGHEST

N_HEADS = 4
HEAD_DIM = 64
W_MIX = N_HEADS * HEAD_DIM
LORA = 64
CONV_W = 4
N_BRANCH = 4
LRU_C = 8.0
ROPE_BASE = 10000.0
EPS = 1e-6
RWKV_GN_EPS = 64e-5
PAST_LEN = 16384
A_SHIFT_W = 3 * W_MIX + 2 * LORA
A_W = A_SHIFT_W + W_MIX
B_W = 4 * W_MIX
C_W = 2 * W_MIX
D_QKV_W = 3 * W_MIX
D_W = D_QKV_W + 2 * N_HEADS + W_MIX
D_PACK_W = D_QKV_W + 3 * W_MIX

SUBLANES = 8
LANES = 128
VMEM_LIMIT = 56 * 1024 * 1024

CHUNK = 64
RET_CHUNK = 128
INV_BLOCK = 16
WAVE = 16
ROW_TILE = 1024
LRU_TILE = 512
PROJ_TILE = 512
OUT_TILE = 1024
N_VEC_T = 14
N_VEC_PLAIN = 6


def _mm(a, b, prec=HI):
    return lax.dot_general(a, b, (((1,), (0,)), ((), ())), precision=prec, preferred_element_type=F32)


def _mm_nt(a, b, prec=HI):
    return lax.dot_general(a, b, (((1,), (1,)), ((), ())), precision=prec, preferred_element_type=F32)


def _mm_tn(a, b, prec=HI):
    return lax.dot_general(a, b, (((0,), (0,)), ((), ())), precision=prec, preferred_element_type=F32)


_NN = (((1,), (0,)), ((), ()))
_NT = (((1,), (1,)), ((), ()))
_TN = (((0,), (0,)), ((), ()))

P_INV = 1
P_STATE = 1
P_MISC = 1
HEAD_SUM_PIECES = 1
CUMSUM_PIECES = 2


class _Split:
    def __init__(self, x, passes):
        self.hi = x.astype(BF16)
        self.lo = (x - self.hi.astype(F32)).astype(BF16) if passes > 1 else None


def _dotp(a, b, dims=_NN, passes=1):
    a = a if isinstance(a, _Split) else _Split(a, passes)
    b = b if isinstance(b, _Split) else _Split(b, passes)
    d = lambda x, y: lax.dot_general(x, y, dims, preferred_element_type=F32)
    out = d(a.hi, b.hi)
    if passes > 1:
        out = out + (d(a.hi, b.lo) + d(a.lo, b.hi))
    return out


def _iota(shape, dim):
    return lax.broadcasted_iota(jnp.int32, shape, dim)


def _silu(x):
    return x * jax.nn.sigmoid(x)


def _softplus(x):
    return jnp.maximum(x, 0.0) + jnp.log1p(jnp.exp(-jnp.abs(x)))


def _pieces(x, n):
    out = []
    for i in range(n):
        p = x.astype(BF16)
        out.append(p)
        if i + 1 < n:
            x = x - p.astype(F32)
    return out


def _dot_const(x, const, dims=_NN, n=2, const_left=False):
    out = None
    for p in _pieces(x, n):
        t = lax.dot_general(*((const, p) if const_left else (p, const)), dims, preferred_element_type=F32)
        out = t if out is None else out + t
    return out


def _head_ones():
    return (_iota((W_MIX, W_MIX), 0) // HEAD_DIM == _iota((W_MIX, W_MIX), 1) // HEAD_DIM).astype(BF16)


def _head_sum(x, ones, signed=False):
    return _dot_const(x, ones, n=HEAD_SUM_PIECES + (1 if signed else 0))


def _rms(x):
    return x * lax.rsqrt(jnp.mean(x * x, axis=-1, keepdims=True) + EPS)


def _inv_unit_lower(a):
    return _inv_unit_lower_many([a])[0]


def _inv_unit_lower_many(mats):
    n = mats[0].shape[0]
    ri, ci = _iota((n, n), 0), _iota((n, n), 1)
    eye = (ri == ci).astype(F32)
    diag_blk = (ri // INV_BLOCK) == (ci // INV_BLOCK)
    mm = lambda x, y: _dotp(x, y, _NN, P_INV)
    sp = lambda x: _Split(x, P_INV)
    d = [jnp.where(diag_blk, a, 0.0) for a in mats]
    nb = [a - di for a, di in zip(mats, d)]
    td = [eye - di for di in d]
    p = d
    for _ in range(int(math.log2(INV_BLOCK)) - 1):
        ps = [sp(pi) for pi in p]
        p = [mm(pi, pi) for pi in ps]
        td = [mm(ti, eye + pi) for ti, pi in zip(td, p)]
    tds = [sp(ti) for ti in td]
    x = [mm(ti, ni) for ti, ni in zip(tds, nb)]
    t = [eye - xi for xi in x]
    p = x
    for _ in range(int(math.log2(n // INV_BLOCK)) - 1):
        ps = [sp(pi) for pi in p]
        p = [mm(pi, pi) for pi in ps]
        t = [mm(ti, eye + pi) for ti, pi in zip(t, p)]
    return [mm(ti, tdi) for ti, tdi in zip(t, tds)]


class _HeadAlgebra:
    def __init__(self, c):
        assert c == HEAD_DIM, "side-by-side head products need CHUNK == HEAD_DIM"
        w = W_MIX
        row, lane = _iota((c, w), 0), _iota((c, w), 1)
        col = lane % HEAD_DIM
        tile_lane = _iota((c, LANES), 1)
        self.tile_head = [tile_lane // HEAD_DIM == h for h in range(LANES // HEAD_DIM)]
        self.eye = (row == col).astype(F32)
        self.strict = row > col
        self.incl = row >= col
        self.inv_blk = (row // INV_BLOCK) == (col // INV_BLOCK)
        r2, c2 = _iota((w, w), 0), _iota((w, w), 1)
        self.eye_full = r2 == c2
        self.same_head = (r2 // HEAD_DIM) == (c2 // HEAD_DIM)

    def bd(self, y):
        yb = y.astype(BF16)
        zero = jnp.zeros((yb.shape[0], LANES), BF16)
        blocks = []
        for t in range(W_MIX // LANES):
            tile = yb[:, t * LANES:(t + 1) * LANES]
            for m in self.tile_head:
                kept = jnp.where(m, tile, zero)
                blocks.append(jnp.concatenate([kept if s == t else zero for s in range(W_MIX // LANES)], axis=1))
        return jnp.concatenate(blocks, axis=0)

    def nn(self, x, bd_y, out=F32):
        return lax.dot_general(x.astype(BF16), bd_y, _NN, preferred_element_type=F32).astype(out)

    def nt(self, x, bd_y):
        return lax.dot_general(x.astype(BF16), bd_y, _NT, preferred_element_type=F32)

    def tn_bd(self, x, y):
        full = lax.dot_general(x.astype(BF16), y.astype(BF16), _TN, preferred_element_type=F32)
        return jnp.where(self.same_head, full, 0.0)

    def diag_bd(self, row_vec):
        return jnp.where(self.eye_full, row_vec, 0.0)

    def plus_eye(self, bd_p):
        return jnp.where(self.eye_full, jnp.ones_like(bd_p), bd_p)

    def inv_unit_lower_many(self, mats):
        mats = [a.astype(BF16) for a in mats]
        zero = jnp.zeros_like(mats[0])
        eye = self.eye.astype(BF16)
        d = [jnp.where(self.inv_blk, a, zero) for a in mats]
        nb = [jnp.where(self.inv_blk, zero, a) for a in mats]
        td = [eye - di for di in d]
        p = d
        bdp = [self.bd(pi) for pi in p]
        for _ in range(int(math.log2(INV_BLOCK)) - 1):
            p = [self.nn(pi, bi, BF16) for pi, bi in zip(p, bdp)]
            bdp = [self.bd(pi) for pi in p]
            td = [self.nn(ti, self.plus_eye(bi), BF16) for ti, bi in zip(td, bdp)]
        bd_td = [self.bd(ti) for ti in td]
        x = [self.nn(ti, self.bd(ni), BF16) for ti, ni in zip(td, nb)]
        t = [eye - xi for xi in x]
        p = x
        bdp = [self.bd(pi) for pi in p]
        for _ in range(int(math.log2(HEAD_DIM // INV_BLOCK)) - 1):
            p = [self.nn(pi, bi, BF16) for pi, bi in zip(p, bdp)]
            bdp = [self.bd(pi) for pi in p]
            t = [self.nn(ti, self.plus_eye(bi), BF16) for ti, bi in zip(t, bdp)]
        return [self.nn(ti, bi, BF16) for ti, bi in zip(t, bd_td)]


def _ada_kernel(cp_ref, cs_ref, w_ref, b_ref, op_ref, os_ref):
    w = _Split(w_ref[...], 3)
    op_ref[...] = _dotp(_silu(cp_ref[...]), w, _NN, 3) + b_ref[...]
    os_ref[...] = _dotp(_silu(cs_ref[...]), w, _NN, 3) + b_ref[...]


def _ada_call(c_prompt, c_sample, ada_w, ada_b):
    n_layers, d, d3 = ada_w.shape
    n_p, n_s = c_prompt.shape[0], c_sample.shape[0]
    return pl.pallas_call(
        _ada_kernel,
        grid=(n_layers, d3 // d),
        in_specs=[
            pl.BlockSpec((n_p, d), lambda l, j: (0, 0)),
            pl.BlockSpec((n_s, d), lambda l, j: (0, 0)),
            pl.BlockSpec((None, d, d), lambda l, j: (l, 0, j)),
            pl.BlockSpec((None, 1, d), lambda l, j: (l, 0, j)),
        ],
        out_specs=[pl.BlockSpec((None, n_p, d), lambda l, j: (l, 0, j)),
                   pl.BlockSpec((None, n_s, d), lambda l, j: (l, 0, j))],
        out_shape=[jax.ShapeDtypeStruct((n_layers, n_p, d3), F32), jax.ShapeDtypeStruct((n_layers, n_s, d3), F32)],
        compiler_params=pltpu.CompilerParams(dimension_semantics=("arbitrary", "arbitrary"),
                                             vmem_limit_bytes=VMEM_LIMIT),
        name="ada_mod",
    )(c_prompt, c_sample, ada_w, ada_b.reshape(n_layers, 1, d3))


def _modulated_norm(x, g, scale, shift):
    return _rms(x) * g * (1.0 + scale) + shift


def _inproj_kernel(x_ref, sc_ref, sh_ref, g_ref, w_ref, wz_ref, wba_ref, oa_ref, ob_ref, oc_ref, od_ref):
    h = _modulated_norm(x_ref[...], g_ref[...], sc_ref[...], sh_ref[...]).astype(BF16)
    proj = lambda w: lax.dot_general(h, w, _NT, preferred_element_type=F32)
    lo = 0
    for o_ref in (oa_ref, ob_ref, oc_ref):
        wd = o_ref.shape[-1]
        o_ref[...] = proj(w_ref[0, lo:lo + wd, :])
        lo += wd
    od_ref[:, :D_QKV_W] = proj(w_ref[0, lo:lo + D_QKV_W, :])
    od_ref[:, D_QKV_W:D_QKV_W + W_MIX] = proj(wz_ref[0])
    od_ref[:, D_QKV_W + W_MIX:] = proj(wba_ref[...])


def _weight_rows(layer, row0, n_rows, d):
    return pl.BlockSpec((pl.Element(1), pl.Element(n_rows), pl.Element(d)), lambda i: (layer, row0, 0))


MOD_SHIFT, MOD_SCALE, MOD_GATE = 0, 1, 2


def _mod_specs(mods, layer, parts, d, tm, seq_len):
    if seq_len == 1:
        return [mods] * len(parts), [pl.BlockSpec((None, tm, d), lambda i, c=c: (layer, i, c)) for c in parts]
    per_seq = seq_len // tm
    by_seq = mods.reshape(mods.shape[0], mods.shape[1], 1, mods.shape[2])
    return ([by_seq] * len(parts),
            [pl.BlockSpec((None, None, 1, d), lambda i, c=c: (layer, i // per_seq, 0, c)) for c in parts])


class _LayerParam:
    def __init__(self, arr, layer):
        self.arr, self.layer = arr, layer

    def spec(self):
        return pl.BlockSpec((None,) + self.arr.shape[1:], lambda *_: (self.layer,) + (0,) * (self.arr.ndim - 1))


def _spec_of(a):
    if isinstance(a, _LayerParam):
        return a.spec()
    return pl.BlockSpec(a.shape, lambda *_: (0,) * a.ndim)


def _arr_of(a):
    return a.arr if isinstance(a, _LayerParam) else a


def _inproj_call(x2, mods_all, g, l, w_t, w_ba, seq_len):
    m, d = x2.shape
    tm = min(PROJ_TILE, m, seq_len) if seq_len > 1 else m
    widths = (A_W, B_W, C_W, D_PACK_W)
    mods, mod_specs = _mod_specs(mods_all, l, (MOD_SCALE, MOD_SHIFT), d, tm, seq_len)
    off_ba = A_W + B_W + C_W + D_QKV_W
    off_z = off_ba + 2 * N_HEADS
    return pl.pallas_call(
        _inproj_kernel,
        grid=(m // tm,),
        in_specs=[pl.BlockSpec((tm, d), lambda i: (i, 0))] + mod_specs + [
            _spec_of(g),
            _weight_rows(l, 0, off_ba, d),
            _weight_rows(l, off_z, W_MIX, d),
            _spec_of(w_ba),
        ],
        out_specs=[pl.BlockSpec((tm, wd), lambda i: (i, 0)) for wd in widths],
        out_shape=[jax.ShapeDtypeStruct((m, wd), F32) for wd in widths],
        compiler_params=pltpu.CompilerParams(dimension_semantics=("parallel",), vmem_limit_bytes=VMEM_LIMIT),
        name="in_proj",
    )(x2, *mods, _arr_of(g), w_t, w_t, _arr_of(w_ba))


def _outproj_kernel(x_ref, sc_ref, sh_ref, gt_ref, g_ref, ba_ref, bb_ref, bc_ref, bd_ref,
                    wg_ref, wup_ref, wout_ref, fg_ref, o_ref, *, final):
    x = x_ref[...]
    d = x.shape[-1]
    ups = [jnp.dot(br_ref[...].astype(BF16), wup_ref[n], preferred_element_type=F32)
           for n, br_ref in enumerate((ba_ref, bb_ref, bc_ref, bd_ref))]
    h = _modulated_norm(x, g_ref[...], sc_ref[...], sh_ref[...]).astype(BF16)
    merged = None
    for n, up in enumerate(ups):
        gl = lax.dot_general(h, wg_ref[0, n * d:(n + 1) * d, :], _NT, preferred_element_type=F32)
        term = jax.nn.sigmoid(gl) * up
        merged = term if merged is None else merged + term
    out = jnp.dot(merged.astype(BF16), wout_ref[...], preferred_element_type=F32)
    xn = x + gt_ref[...] * out
    if final:
        xn = _rms(xn) * fg_ref[...]
    o_ref[...] = xn


def _outproj_call(x2, mods_all, g, branches, l, w_t, wup, wout, final_g, seq_len, final):
    m, d = x2.shape
    tm = min(OUT_TILE, m, seq_len) if seq_len > 1 else m
    mods, mod_specs = _mod_specs(mods_all, l, (MOD_SCALE, MOD_SHIFT, MOD_GATE), d, tm, seq_len)
    full = lambda a: pl.BlockSpec(a.shape, lambda i: (0,) * a.ndim)
    off_g = A_W + B_W + C_W + D_W
    gate_rows = _weight_rows(l, off_g, N_BRANCH * d, d)
    return pl.pallas_call(
        functools.partial(_outproj_kernel, final=final),
        grid=(m // tm,),
        in_specs=[pl.BlockSpec((tm, d), lambda i: (i, 0))] + mod_specs + [_spec_of(g)]
        + [pl.BlockSpec((tm, W_MIX), lambda i: (i, 0)) for _ in branches]
        + [gate_rows, _spec_of(wup), _spec_of(wout), pl.BlockSpec((1, d), lambda i: (0, 0))],
        out_specs=pl.BlockSpec((tm, d), lambda i: (i, 0)),
        out_shape=jax.ShapeDtypeStruct((m, d), F32),
        compiler_params=pltpu.CompilerParams(dimension_semantics=("parallel",), vmem_limit_bytes=VMEM_LIMIT),
        name="out_proj",
    )(x2, *mods, _arr_of(g), *branches, w_t, _arr_of(wup), _arr_of(wout), final_g.reshape(1, d))


def _rwkv_token_math(pm, w0, w2, a0, a2, k_k, k_a, ones):
    r = pm[:, 0:W_MIX]
    k = pm[:, W_MIX:2 * W_MIX]
    v = pm[:, 2 * W_MIX:3 * W_MIX]
    wd = pm[:, 3 * W_MIX:3 * W_MIX + LORA]
    ad = pm[:, 3 * W_MIX + LORA:]
    w_log = -_softplus(-(w0 + _dotp(jnp.tanh(wd), w2, _NN, P_MISC))) - 0.5
    log_decay = -jnp.exp(w_log)
    a = jax.nn.sigmoid(a0 + _dotp(ad, a2, _NN, P_MISC))
    kx = k * k_k
    kk = kx * lax.rsqrt(_head_sum(kx * kx, ones) + EPS)
    k = k * (1.0 + (a - 1.0) * k_a)
    return r, k, v, log_decay, -kk, kk * a


def _rwkv_finish(o, r, k, v, z, r_k, ln_g, ln_b, ones):
    mean = _head_sum(o, ones, signed=True) * (1.0 / HEAD_DIM)
    dlt = o - mean
    var = _head_sum(dlt * dlt, ones) * (1.0 / HEAD_DIM)
    on = dlt * lax.rsqrt(var + RWKV_GN_EPS) * ln_g + ln_b
    bonus = _head_sum(r * k * r_k, ones, signed=True) * v
    return (on + bonus) * _silu(z)


def _swap_halves(x):
    half = HEAD_DIM // 2
    n = x.shape[-1]
    first = (_iota(x.shape, 1) & half) == 0
    return jnp.where(first, pltpu.roll(x, n - half, axis=1), pltpu.roll(x, half, axis=1))


def _rotary(x, cos, sin):
    return x * cos + _swap_halves(x) * sin


def _lru_token_math(xc, gate_w, gate_b, lam):
    gates = _dotp(xc, gate_w, _NN, P_MISC) + gate_b
    r_gate = jax.nn.sigmoid(gates[:, :W_MIX])
    i_gate = jax.nn.sigmoid(gates[:, W_MIX:])
    log_a = -LRU_C * r_gate * _softplus(-lam)
    a = jnp.exp(log_a)
    b = jnp.sqrt(1.0 - jnp.exp(2.0 * log_a)) * (i_gate * xc)
    return a, b


def _gdn_token_math(qkv, b_raw, a_raw, a_log, dt_bias, ones):
    qkv = _silu(qkv)
    q = qkv[:, 0:W_MIX]
    k = qkv[:, W_MIX:2 * W_MIX]
    v = qkv[:, 2 * W_MIX:]
    q = q * lax.rsqrt(_head_sum(q * q, ones) + EPS) * (HEAD_DIM ** -0.5)
    k = k * lax.rsqrt(_head_sum(k * k, ones) + EPS)
    beta = jax.nn.sigmoid(b_raw)
    g = -jnp.exp(a_log) * _softplus(a_raw + dt_bias)
    return q, k, v, beta, g


def _head_rms_finish(o, z, ones, gain=None):
    y = o * lax.rsqrt(_head_sum(o * o, ones) * (1.0 / HEAD_DIM) + EPS)
    if gain is not None:
        y = y * gain
    return y * _silu(z)


def _conv_tile(u, ext_ref, w_ref, first):
    n = u.shape[0]

    @pl.when(first)
    def _():
        ext_ref[0:SUBLANES, :] = jnp.zeros((SUBLANES, u.shape[1]), F32)

    ext_ref[SUBLANES:SUBLANES + n, :] = u
    out = None
    for j in range(CONV_W):
        term = _rows_back(u, ext_ref, CONV_W - 1 - j) * w_ref[j:j + 1, :]
        out = term if out is None else out + term
    ext_ref[0:SUBLANES, :] = u[n - SUBLANES:n, :]
    return out


def _rows_back(u, ext_ref, back):
    if back == 0:
        return u
    n, ch = u.shape
    tiles = (n // SUBLANES, SUBLANES, ch)
    pos = _iota((1, SUBLANES, 1), 1)
    earlier = ext_ref[0:n, :].reshape(tiles)
    return pltpu.roll(jnp.where(pos >= SUBLANES - back, earlier, u.reshape(tiles)), back, axis=1).reshape(n, ch)


def _rwkv_kernel(p_ref, mu_ref, w0_ref, w2_ref, a0_ref, a2_ref, kk_ref, ka_ref, rk_ref, lng_ref, lnb_ref,
                 o_ref, s_out_ref, shift_out_ref, s_scr, ext_scr):
    j = pl.program_id(1)
    last = pl.num_programs(1) - 1
    ct = p_ref.shape[0]

    @pl.when(j == 0)
    def _():
        s_scr[...] = jnp.zeros(s_scr.shape, F32)
        ext_scr[0:SUBLANES, :] = jnp.zeros((SUBLANES, A_SHIFT_W), F32)

    p = p_ref[...]
    pa = p[:, :A_SHIFT_W]
    z = p[:, A_SHIFT_W:]
    ext_scr[SUBLANES:SUBLANES + ct, :] = pa
    prev = _rows_back(pa, ext_scr, 1)
    ext_scr[0:SUBLANES, :] = pa[ct - SUBLANES:ct, :]
    pm = pa + (prev - pa) * mu_ref[...]
    ones = _head_ones()
    r, k, v, ld, av, bv = _rwkv_token_math(pm, w0_ref[...], w2_ref[...], a0_ref[...], a2_ref[...],
                                           kk_ref[...], ka_ref[...], ones)

    c = min(CHUNK, ct)
    ha = _HeadAlgebra(c)
    lt = (_iota((c, c), 0) >= _iota((c, c), 1)).astype(BF16)
    units = []
    for c0 in range(0, ct, c):
        sl = slice(c0, c0 + c)
        ldc = ld[sl]
        cum = _dot_const(ldc, lt, _NN, CUMSUM_PIECES, const_left=True)
        e_neg = jnp.exp(-cum)
        e_out = jnp.exp(cum[c - 1:c, :] - cum)
        units.append(dict(a=av[sl] * jnp.exp(cum - ldc), r=r[sl] * jnp.exp(cum), b=bv[sl] * e_neg, k=k[sl] * e_neg,
                          bo=bv[sl] * e_out, ko=k[sl] * e_out, v=v[sl], g=jnp.exp(cum[c - 1:c, :])))
    all_units = units
    state = s_scr[...]
    o_rows = []
    for w0 in range(0, len(all_units), WAVE):
        units = all_units[w0:w0 + WAVE]
        for u in units:
            lhs = jnp.concatenate([u["a"], u["r"]], axis=0)
            u["mb"] = ha.nt(lhs, ha.bd(u["b"]))
            u["mk"] = ha.nt(lhs, ha.bd(u["k"]))
            u["bd_v"] = ha.bd(u["v"])
        for u in units:
            u["m_ab"] = jnp.where(ha.strict, u["mb"][:c], 0.0)
            u["m_ak"] = jnp.where(ha.strict, u["mk"][:c], 0.0)
            u["m_rb"] = jnp.where(ha.incl, u["mb"][c:], 0.0)
            u["m_rk"] = jnp.where(ha.incl, u["mk"][c:], 0.0)
        for u, t_inv in zip(units, ha.inv_unit_lower_many([-u["m_ab"] for u in units])):
            u["t_inv"] = t_inv
        for u in units:
            u["makv"] = ha.nn(u["m_ak"], u["bd_v"], BF16)
        for u in units:
            u["a_hat"] = ha.nn(u["t_inv"], ha.bd(u["a"]), BF16)
            u["u1"] = ha.nn(u["t_inv"], ha.bd(u["makv"]), BF16)
        for u in units:
            u["r_hat"] = u["r"] + ha.nn(u["m_rb"], ha.bd(u["a_hat"]))
            u["o1"] = ha.nn(u["m_rb"], ha.bd(u["u1"])) + ha.nn(u["m_rk"], u["bd_v"])
            u["g_t"] = ha.diag_bd(u["g"]) + ha.tn_bd(u["bo"], u["a_hat"])
            u["h_t"] = ha.tn_bd(jnp.concatenate([u["bo"], u["ko"]], axis=0),
                                jnp.concatenate([u["u1"], u["v"].astype(BF16)], axis=0))
            zz = lax.dot_general(jnp.concatenate([u["r_hat"], u["g_t"]], axis=0).astype(BF16), state.astype(BF16),
                                 _NN, preferred_element_type=F32)
            o_rows.append(zz[:c] + u["o1"])
            state = zz[c:] + u["h_t"]
    o = o_rows[0] if len(o_rows) == 1 else jnp.concatenate(o_rows, axis=0)
    s_scr[...] = state
    o_ref[...] = _rwkv_finish(o, r, k, v, z, rk_ref[...], lng_ref[...], lnb_ref[...], ones)

    @pl.when(j == last)
    def _():
        eye_h = (_iota((HEAD_DIM, HEAD_DIM), 0) == _iota((HEAD_DIM, HEAD_DIM), 1)).astype(F32)
        for h in range(N_HEADS):
            hs = slice(h * HEAD_DIM, (h + 1) * HEAD_DIM)
            s_out_ref[h] = _mm_nt(eye_h, state[hs, hs])
        shift_out_ref[...] = pa[ct - 1:ct, :]


def _row(a):
    return a if isinstance(a, _LayerParam) else a.reshape(1, -1)


def _rwkv_call(p_a, n_seq, seq_len, lp):
    ct = min(ROW_TILE, seq_len)
    p3 = p_a.reshape(n_seq, seq_len, A_W)
    params = [_row(lp["rwkv_mu"]), _row(lp["rwkv_w0"]), lp["rwkv_w2"], _row(lp["rwkv_a0"]), lp["rwkv_a2"],
              _row(lp["rwkv_k_k"]), _row(lp["rwkv_k_a"]), _row(lp["rwkv_r_k"]), _row(lp["rwkv_ln_g"]),
              _row(lp["rwkv_ln_b"])]
    o, s1, shift1 = pl.pallas_call(
        _rwkv_kernel,
        grid=(n_seq, seq_len // ct),
        in_specs=[pl.BlockSpec((None, ct, A_W), lambda b, j: (b, j, 0))]
        + [_spec_of(a) for a in params],
        out_specs=[
            pl.BlockSpec((None, ct, W_MIX), lambda b, j: (b, j, 0)),
            pl.BlockSpec((None, N_HEADS, HEAD_DIM, HEAD_DIM), lambda b, j: (b, 0, 0, 0)),
            pl.BlockSpec((None, 1, A_SHIFT_W), lambda b, j: (b, 0, 0)),
        ],
        out_shape=[
            jax.ShapeDtypeStruct((n_seq, seq_len, W_MIX), F32),
            jax.ShapeDtypeStruct((n_seq, N_HEADS, HEAD_DIM, HEAD_DIM), F32),
            jax.ShapeDtypeStruct((n_seq, 1, A_SHIFT_W), F32),
        ],
        scratch_shapes=[pltpu.VMEM((W_MIX, W_MIX), F32), pltpu.VMEM((ct + SUBLANES, A_SHIFT_W), F32)],
        compiler_params=pltpu.CompilerParams(dimension_semantics=("parallel", "arbitrary"),
                                             vmem_limit_bytes=VMEM_LIMIT),
        name="rwkv7_prompt",
    )(p3, *[_arr_of(a) for a in params])
    return o.reshape(n_seq * seq_len, W_MIX), s1, shift1.reshape(n_seq, A_SHIFT_W)


def _ret_kernel(p_ref, cos_ref, sin_ref, o_ref, s_out_ref, s_scr):
    j = pl.program_id(1)
    last = pl.num_programs(1) - 1
    ct = p_ref.shape[0]

    @pl.when(j == 0)
    def _():
        s_scr[...] = jnp.zeros(s_scr.shape, F32)

    p = p_ref[...]
    cos, sin = cos_ref[...], sin_ref[...]
    q = _rotary(p[:, 0:W_MIX], cos, sin)
    k = _rotary(p[:, W_MIX:2 * W_MIX], cos, sin) * (HEAD_DIM ** -0.5)
    v = p[:, 2 * W_MIX:3 * W_MIX]
    z = p[:, 3 * W_MIX:]
    c = min(RET_CHUNK, ct)
    ri, ci = _iota((c, c), 0), _iota((c, c), 1)
    causal = ri >= ci
    rel = jnp.where(causal, ri - ci, 0).astype(F32)
    idx = _iota((c, 1), 0).astype(F32)
    states = [s_scr[h] for h in range(N_HEADS)]
    units = []
    for h in range(N_HEADS):
        lg = math.log(1.0 - 2.0 ** (-5.0 - h))
        consts = dict(decay=jnp.where(causal, jnp.exp(lg * rel), 0.0), q_dec=jnp.exp(lg * (idx + 1.0)),
                      k_dec=jnp.exp(lg * (c - 1.0 - idx)), g_c=math.exp(lg * c))
        hs = slice(h * HEAD_DIM, (h + 1) * HEAD_DIM)
        for c0 in range(0, ct, c):
            sl = slice(c0, c0 + c)
            units.append(dict(consts, h=h, q=_Split(q[sl, hs], 1), k=k[sl, hs], v=_Split(v[sl, hs], 1)))
    for u in units:
        u["s_in"] = _dotp(u["q"], u["k"], _NT, P_MISC) * u["decay"]
        u["kv"] = _dotp(u["k"] * u["k_dec"], u["v"], _TN, P_MISC)
    for u in units:
        u["o"] = _dotp(u["s_in"], u["v"], _NN, P_MISC)
    for u in units:
        u["s0"] = states[u["h"]]
        states[u["h"]] = u["s0"] * u["g_c"] + u["kv"]
    o_heads_all = [[] for _ in range(N_HEADS)]
    for u in units:
        o_heads_all[u["h"]].append(u["o"] + _dotp(u["q"], u["s0"], _NN, P_MISC) * u["q_dec"])
    cols = [oh[0] if len(oh) == 1 else jnp.concatenate(oh, axis=0) for oh in o_heads_all]
    o = jnp.concatenate(cols, axis=1)
    for h in range(N_HEADS):
        s_scr[h] = states[h]
    o_ref[...] = _head_rms_finish(o, z, _head_ones())

    @pl.when(j == last)
    def _():
        for h in range(N_HEADS):
            s_out_ref[h] = states[h]


def _rope_tables(pos):
    half = HEAD_DIM // 2
    inv = ROPE_BASE ** (-jnp.arange(half, dtype=F32) / half)
    ang = pos.astype(F32)[:, None] * inv[None, :]
    cos, sin = jnp.cos(ang), jnp.sin(ang)
    cos_t = jnp.tile(jnp.concatenate([cos, cos], axis=-1), (1, N_HEADS))
    sin_t = jnp.tile(jnp.concatenate([-sin, sin], axis=-1), (1, N_HEADS))
    return cos_t, sin_t


def _ret_call(p_b, n_seq, seq_len, cos_t, sin_t):
    ct = min(ROW_TILE, seq_len)
    p3 = p_b.reshape(n_seq, seq_len, B_W)
    o, s1 = pl.pallas_call(
        _ret_kernel,
        grid=(n_seq, seq_len // ct),
        in_specs=[
            pl.BlockSpec((None, ct, B_W), lambda b, j: (b, j, 0)),
            pl.BlockSpec((ct, W_MIX), lambda b, j: (j, 0)),
            pl.BlockSpec((ct, W_MIX), lambda b, j: (j, 0)),
        ],
        out_specs=[
            pl.BlockSpec((None, ct, W_MIX), lambda b, j: (b, j, 0)),
            pl.BlockSpec((None, N_HEADS, HEAD_DIM, HEAD_DIM), lambda b, j: (b, 0, 0, 0)),
        ],
        out_shape=[
            jax.ShapeDtypeStruct((n_seq, seq_len, W_MIX), F32),
            jax.ShapeDtypeStruct((n_seq, N_HEADS, HEAD_DIM, HEAD_DIM), F32),
        ],
        scratch_shapes=[pltpu.VMEM((N_HEADS, HEAD_DIM, HEAD_DIM), F32)],
        compiler_params=pltpu.CompilerParams(dimension_semantics=("parallel", "arbitrary"),
                                             vmem_limit_bytes=VMEM_LIMIT),
        name="retention_prompt",
    )(p3, cos_t, sin_t)
    return o.reshape(n_seq * seq_len, W_MIX), s1


def _affine_scan(a, b, span):
    n, w = a.shape
    if span == SUBLANES and n > span:
        shape, axis = (n // span, span, w), 1
        a, b = a.reshape(shape), b.reshape(shape)
        pos = _iota((1, span, 1), 1)
    else:
        assert span >= n
        axis = 0
        pos = _iota((n, 1), 0)
    dist = 1
    while dist < span:
        keep = pos >= dist
        a_prev = jnp.where(keep, pltpu.roll(a, dist, axis=axis), 1.0)
        b_prev = jnp.where(keep, pltpu.roll(b, dist, axis=axis), 0.0)
        b = a * b_prev + b
        a = a * a_prev
        dist *= 2
    return a.reshape(n, w), b.reshape(n, w)


def _lru_kernel(p_ref, cw_ref, cb_ref, gw_ref, gb_ref, sp_ref, o_ref, h_out_ref, conv_out_ref,
                ext_scr, h_scr, ab_scr, hin_scr):
    j = pl.program_id(1)
    last = pl.num_programs(1) - 1
    ct = p_ref.shape[0]

    @pl.when(j == 0)
    def _():
        h_scr[...] = jnp.zeros(h_scr.shape, F32)

    p = p_ref[...]
    xr = p[:, :W_MIX]
    z = p[:, W_MIX:]
    xc = _conv_tile(xr, ext_scr, cw_ref, j == 0) + cb_ref[...]
    a, b = _lru_token_math(xc, gw_ref[...], gb_ref[...], sp_ref[...])
    n_grp = ct // SUBLANES
    a, b = _affine_scan(a, b, SUBLANES)
    n_tiles = W_MIX // LANES
    for t in range(n_tiles):
        ab_scr[t] = a[:, t * LANES:(t + 1) * LANES]
        ab_scr[n_tiles + t] = b[:, t * LANES:(t + 1) * LANES]
    ends = pl.ds(SUBLANES - 1, n_grp, stride=SUBLANES)
    a_end = jnp.concatenate([ab_scr[t, ends, :] for t in range(n_tiles)], axis=1)
    b_end = jnp.concatenate([ab_scr[n_tiles + t, ends, :] for t in range(n_tiles)], axis=1)
    a_end, b_end = _affine_scan(a_end, b_end, n_grp)
    h_prev = h_scr[...]
    h_end = a_end * h_prev + b_end
    grp = _iota((n_grp, 1), 0)
    hin_scr[...] = jnp.where(grp == 0, h_prev, pltpu.roll(h_end, 1, axis=0))
    h_in = jnp.concatenate([jnp.broadcast_to(hin_scr[g:g + 1, :], (SUBLANES, W_MIX)) for g in range(n_grp)], axis=0)
    hcur = a * h_in + b
    h_scr[...] = h_end[n_grp - 1:n_grp, :]
    o_ref[...] = hcur * _silu(z)

    @pl.when(j == last)
    def _():
        h_out_ref[...] = hcur[ct - 1:ct, :]
        conv_out_ref[...] = xr[ct - SUBLANES:ct, :]


def _block_diag_gates(gate_w):
    out = jnp.zeros((gate_w.shape[0], W_MIX, 2 * W_MIX), F32)
    for g in range(2):
        for n in range(N_HEADS):
            out = out.at[:, n * HEAD_DIM:(n + 1) * HEAD_DIM,
                         g * W_MIX + n * HEAD_DIM:g * W_MIX + (n + 1) * HEAD_DIM].set(gate_w[:, g, n])
    return out


def _lru_params(lp):
    return [lp["lru_conv_w"], lp["lru_conv_b"], lp["lru_gate_w"], lp["lru_gate_b"], lp["lru_lambda"]]


def _lru_call(p_c, n_seq, seq_len, lp):
    ct = min(LRU_TILE, seq_len)
    p3 = p_c.reshape(n_seq, seq_len, C_W)
    params = _lru_params(lp)
    o, h1, conv_tail = pl.pallas_call(
        _lru_kernel,
        grid=(n_seq, seq_len // ct),
        in_specs=[pl.BlockSpec((None, ct, C_W), lambda b, j: (b, j, 0))]
        + [_spec_of(a) for a in params],
        out_specs=[
            pl.BlockSpec((None, ct, W_MIX), lambda b, j: (b, j, 0)),
            pl.BlockSpec((None, 1, W_MIX), lambda b, j: (b, 0, 0)),
            pl.BlockSpec((None, SUBLANES, W_MIX), lambda b, j: (b, 0, 0)),
        ],
        out_shape=[
            jax.ShapeDtypeStruct((n_seq, seq_len, W_MIX), F32),
            jax.ShapeDtypeStruct((n_seq, 1, W_MIX), F32),
            jax.ShapeDtypeStruct((n_seq, SUBLANES, W_MIX), F32),
        ],
        scratch_shapes=[pltpu.VMEM((ct + SUBLANES, W_MIX), F32), pltpu.VMEM((1, W_MIX), F32),
                        pltpu.VMEM((2 * W_MIX // LANES, ct, LANES), F32),
                        pltpu.VMEM((ct // SUBLANES, W_MIX), F32)],
        compiler_params=pltpu.CompilerParams(dimension_semantics=("parallel", "arbitrary"),
                                             vmem_limit_bytes=VMEM_LIMIT),
        name="rglru_prompt",
    )(p3, *[_arr_of(a) for a in params])
    return (o.reshape(n_seq * seq_len, W_MIX), h1.reshape(n_seq, W_MIX),
            conv_tail[:, SUBLANES - (CONV_W - 1):, :])


def _gdn_kernel(p_ref, cw_ref, nal_ref, dtb_ref, ng_ref, o_ref, s_out_ref, conv_out_ref, ext_scr, s_scr):
    j = pl.program_id(1)
    last = pl.num_programs(1) - 1
    ct = p_ref.shape[0]

    @pl.when(j == 0)
    def _():
        s_scr[...] = jnp.zeros(s_scr.shape, F32)

    p = p_ref[...]
    raw = p[:, :D_QKV_W]
    z = p[:, D_QKV_W:D_QKV_W + W_MIX]
    b_raw = p[:, D_QKV_W + W_MIX:D_QKV_W + 2 * W_MIX]
    a_raw = p[:, D_QKV_W + 2 * W_MIX:]
    ones = _head_ones()
    qkv = _conv_tile(raw, ext_scr, cw_ref, j == 0)
    q, k, v, beta, g = _gdn_token_math(qkv, b_raw, a_raw, nal_ref[...], dtb_ref[...], ones)

    c = min(CHUNK, ct)
    ha = _HeadAlgebra(c)
    lt = (_iota((c, c), 0) >= _iota((c, c), 1)).astype(BF16)
    units = []
    for c0 in range(0, ct, c):
        sl = slice(c0, c0 + c)
        gc = _dot_const(g[sl], lt, _NN, CUMSUM_PIECES, const_left=True)
        gc_cols = jnp.sum(gc * ha.eye, axis=0, keepdims=True)
        diff = gc - gc_cols
        decay = jnp.where(ha.incl, jnp.exp(jnp.where(ha.incl, diff, 0.0)), 0.0)
        kb = k[sl] * beta[sl]
        e_gc = jnp.exp(gc)
        g_last = gc[c - 1:c, :]
        units.append(dict(decay=decay, kb=kb, q=q[sl], k=k[sl], vb=v[sl] * beta[sl], kbe=kb * e_gc,
                          k_out=k[sl] * jnp.exp(g_last - gc), q_in=q[sl] * e_gc, e_last=jnp.exp(g_last)))
    all_units = units
    state = s_scr[...]
    o_rows = []
    for w0 in range(0, len(all_units), WAVE):
        units = all_units[w0:w0 + WAVE]
        for u in units:
            kq = ha.nt(jnp.concatenate([u["kb"], u["q"]], axis=0), ha.bd(u["k"]))
            u["a_mat"] = jnp.where(ha.strict, kq[:c] * u["decay"], 0.0)
            u["qk"] = kq[c:] * u["decay"]
        for u, t_inv in zip(units, ha.inv_unit_lower_many([u["a_mat"] for u in units])):
            u["t_inv"] = t_inv
        for u in units:
            u["u"] = ha.nn(u["t_inv"], ha.bd(u["vb"]), BF16)
            u["w"] = ha.nn(u["t_inv"], ha.bd(u["kbe"]), BF16)
        for u in units:
            u["g_mat"] = ha.diag_bd(u["e_last"]) - ha.tn_bd(u["k_out"], u["w"])
            u["h_mat"] = ha.tn_bd(u["k_out"], u["u"])
            u["q_hat"] = u["q_in"] - ha.nn(u["qk"], ha.bd(u["w"]))
            u["o1"] = ha.nn(u["qk"], ha.bd(u["u"]))
            zz = lax.dot_general(jnp.concatenate([u["q_hat"], u["g_mat"]], axis=0).astype(BF16),
                                 state.astype(BF16), _NN, preferred_element_type=F32)
            o_rows.append(zz[:c] + u["o1"])
            state = zz[c:] + u["h_mat"]
    o = o_rows[0] if len(o_rows) == 1 else jnp.concatenate(o_rows, axis=0)
    s_scr[...] = state
    o_ref[...] = _head_rms_finish(o, z, ones, ng_ref[...])

    @pl.when(j == last)
    def _():
        for h in range(N_HEADS):
            hs = slice(h * HEAD_DIM, (h + 1) * HEAD_DIM)
            s_out_ref[h] = state[hs, hs]
        conv_out_ref[...] = raw[ct - SUBLANES:ct, :]


def _gdn_params(lp):
    return [lp["gdn_conv_w"], lp["gdn_A_log"], lp["gdn_dt_bias"], lp["gdn_norm_g"]]


def _gdn_call(p_d, n_seq, seq_len, lp):
    ct = min(ROW_TILE, seq_len)
    p3 = p_d.reshape(n_seq, seq_len, D_PACK_W)
    params = _gdn_params(lp)
    o, s1, conv_tail = pl.pallas_call(
        _gdn_kernel,
        grid=(n_seq, seq_len // ct),
        in_specs=[pl.BlockSpec((None, ct, D_PACK_W), lambda b, j: (b, j, 0))]
        + [_spec_of(a) for a in params],
        out_specs=[
            pl.BlockSpec((None, ct, W_MIX), lambda b, j: (b, j, 0)),
            pl.BlockSpec((None, N_HEADS, HEAD_DIM, HEAD_DIM), lambda b, j: (b, 0, 0, 0)),
            pl.BlockSpec((None, SUBLANES, D_QKV_W), lambda b, j: (b, 0, 0)),
        ],
        out_shape=[
            jax.ShapeDtypeStruct((n_seq, seq_len, W_MIX), F32),
            jax.ShapeDtypeStruct((n_seq, N_HEADS, HEAD_DIM, HEAD_DIM), F32),
            jax.ShapeDtypeStruct((n_seq, SUBLANES, D_QKV_W), F32),
        ],
        scratch_shapes=[pltpu.VMEM((ct + SUBLANES, D_QKV_W), F32), pltpu.VMEM((W_MIX, W_MIX), F32)],
        compiler_params=pltpu.CompilerParams(dimension_semantics=("parallel", "arbitrary"),
                                             vmem_limit_bytes=VMEM_LIMIT),
        name="gdn_prompt",
    )(p3, *[_arr_of(a) for a in params])
    return o.reshape(n_seq * seq_len, W_MIX), s1, conv_tail[:, SUBLANES - (CONV_W - 1):, :]


def _decode_pre_kernel(pa_ref, pb_ref, pc_ref, pd_ref, shift_ref, h0_ref, lconv_ref, gconv_ref, cos_ref, sin_ref,
                       mu_ref, w0_ref, w2_ref, a0_ref, a2_ref, kk_ref, ka_ref,
                       lcw_ref, lcb_ref, lgw_ref, lgb_ref, lsp_ref, gcw_ref, nal_ref, dtb_ref,
                       vt_ref, vn_ref, oc_ref, h1_ref, lconv1_ref, gconv1_ref):
    ones = _head_ones()
    pa_full = pa_ref[...]
    pa = pa_full[:, :A_SHIFT_W]
    pm = pa + (shift_ref[...] - pa) * mu_ref[...]
    r, k, v, ld, av, bv = _rwkv_token_math(pm, w0_ref[...], w2_ref[...], a0_ref[...], a2_ref[...],
                                           kk_ref[...], ka_ref[...], ones)
    vecs = [r, jnp.exp(ld), k, v, av, bv]
    plain = [r, k, v, pa_full[:, A_SHIFT_W:]]
    pb = pb_ref[...]
    cos, sin = cos_ref[...], sin_ref[...]
    vecs += [_rotary(pb[:, 0:W_MIX], cos, sin), _rotary(pb[:, W_MIX:2 * W_MIX], cos, sin) * (HEAD_DIM ** -0.5),
             pb[:, 2 * W_MIX:3 * W_MIX]]
    plain.append(pb[:, 3 * W_MIX:])
    pc = pc_ref[...]
    xr = pc[:, :W_MIX]
    taps = [lconv_ref[i] for i in range(CONV_W - 1)] + [xr]
    xc = taps[0] * lcw_ref[0:1, :]
    for i in range(1, CONV_W):
        xc = xc + taps[i] * lcw_ref[i:i + 1, :]
    xc = xc + lcb_ref[...]
    a, b = _lru_token_math(xc, lgw_ref[...], lgb_ref[...], lsp_ref[...])
    hcur = a * h0_ref[...] + b
    oc_ref[...] = hcur * _silu(pc[:, W_MIX:])
    h1_ref[...] = hcur
    for i in range(CONV_W - 1):
        lconv1_ref[i] = taps[i + 1]
    pd = pd_ref[...]
    raw = pd[:, :D_QKV_W]
    gtaps = [gconv_ref[i] for i in range(CONV_W - 1)] + [raw]
    qkv = gtaps[0] * gcw_ref[0:1, :]
    for i in range(1, CONV_W):
        qkv = qkv + gtaps[i] * gcw_ref[i:i + 1, :]
    q, kg, vg, beta, g = _gdn_token_math(qkv, pd[:, D_QKV_W + W_MIX:D_QKV_W + 2 * W_MIX],
                                         pd[:, D_QKV_W + 2 * W_MIX:], nal_ref[...], dtb_ref[...], ones)
    for i in range(CONV_W - 1):
        gconv1_ref[i] = gtaps[i + 1]
    vecs += [q, kg, vg, beta, g]
    plain.append(pd[:, D_QKV_W:D_QKV_W + W_MIX])
    assert len(vecs) == N_VEC_T and len(plain) == N_VEC_PLAIN
    for i, vec in enumerate(vecs):
        vt_ref[i] = vec.T
    for i, vec in enumerate(plain):
        vn_ref[i] = vec


def _decode_state_kernel(vt_ref, wkv_ref, ret_ref, gdn_ref, gam_ref, wkv1_ref, ret1_ref, gdn1_ref, o_ref):
    v_r, v_w, v_k, v_v, v_a, v_b, r_q, r_k, r_v, g_q, g_k, g_v, g_beta, g_g = range(N_VEC_T)
    hd = HEAD_DIM
    n = vt_ref.shape[-1]
    row = lambda idx, i: vt_ref[idx, pl.ds(i, 1), :]
    rows_of = lambda i: pl.ds(pl.multiple_of(i * hd, hd), hd)
    gamma = gam_ref[...]
    beta = vt_ref[g_beta, 0:1, :]
    eg = jnp.exp(vt_ref[g_g, 0:1, :])

    def first_pass(i, carry):
        acc_ret, acc_w, acc_q = carry
        rows = rows_of(i)
        s = wkv_ref[rows, :]
        sa = jnp.sum(s * vt_ref[v_a], axis=0, keepdims=True)
        s = s * vt_ref[v_w] + sa * vt_ref[v_b] + row(v_v, i) * vt_ref[v_k]
        wkv1_ref[rows, :] = s
        o_ref[0, pl.ds(i, 1), :] = jnp.sum(s * vt_ref[v_r], axis=0, keepdims=True)
        s = ret_ref[rows, :] * gamma + row(r_k, i) * vt_ref[r_v]
        ret1_ref[rows, :] = s
        acc_ret = acc_ret + row(r_q, i) * s
        s = gdn_ref[rows, :]
        return acc_ret, acc_w + row(g_k, i) * s, acc_q + row(g_q, i) * s

    zeros = jnp.zeros((hd, n), F32)
    acc_ret, acc_w, acc_q = lax.fori_loop(0, hd, first_pass, (zeros, zeros, zeros))
    o_ref[1] = acc_ret
    v_new = vt_ref[g_v] * beta - acc_w * (beta * eg)
    qk = jnp.sum(vt_ref[g_q] * vt_ref[g_k], axis=0, keepdims=True)
    o_ref[2] = acc_q * eg + qk * v_new

    def second_pass(i, carry):
        rows = rows_of(i)
        gdn1_ref[rows, :] = gdn_ref[rows, :] * eg + row(g_k, i) * v_new
        return carry

    lax.fori_loop(0, hd, second_pass, 0)


def _decode_finish_kernel(ot_ref, vn_ref, rk_ref, lng_ref, lnb_ref, ng_ref, oa_ref, ob_ref, od_ref):
    ones = _head_ones()
    r, k, v, z_a, z_b, z_d = (vn_ref[i] for i in range(N_VEC_PLAIN))
    oa_ref[...] = _rwkv_finish(ot_ref[0].T, r, k, v, z_a, rk_ref[...], lng_ref[...], lnb_ref[...], ones)
    ob_ref[...] = _head_rms_finish(ot_ref[1].T, z_b, ones)
    od_ref[...] = _head_rms_finish(ot_ref[2].T, z_d, ones, ng_ref[...])


def _batch_minor(state):
    n_layers, n = state.shape[:2]
    return jnp.transpose(state, (0, 2, 3, 4, 1)).reshape(n_layers, -1, n)


def _batch_major(flat_state):
    n = flat_state.shape[-1]
    return jnp.transpose(flat_state.reshape(N_HEADS, HEAD_DIM, HEAD_DIM, n), (3, 0, 1, 2))


def _decode_layer(l, p_a, p_b, p_c, p_d, carried, cos_t, sin_t, lp):
    n = p_a.shape[0]
    assert n % LANES == 0, "the decode state kernel keeps the batch on lanes"
    flat = HEAD_DIM * HEAD_DIM
    taps = CONV_W - 1
    rwkv_params = [_row(lp["rwkv_mu"]), _row(lp["rwkv_w0"]), lp["rwkv_w2"], _row(lp["rwkv_a0"]), lp["rwkv_a2"],
                   _row(lp["rwkv_k_k"]), _row(lp["rwkv_k_a"])]
    gdn_params = _gdn_params(lp)
    full = lambda a: pl.BlockSpec(a.shape, lambda i: (0,) * a.ndim)
    layer_blk = lambda a: pl.BlockSpec((None,) + a.shape[1:], lambda i, nd=a.ndim: (l,) + (0,) * (nd - 1))
    projs = [p_a, p_b, p_c, p_d]
    layered = [carried["shift"], carried["lru_h"], carried["lru_conv"], carried["gdn_conv"]]
    consts = [cos_t, sin_t, *rwkv_params, *_lru_params(lp), *gdn_params[:3]]
    out_shapes = [
        jax.ShapeDtypeStruct((N_VEC_T, W_MIX, n), F32),
        jax.ShapeDtypeStruct((N_VEC_PLAIN, n, W_MIX), F32),
        jax.ShapeDtypeStruct((n, W_MIX), F32),
        jax.ShapeDtypeStruct((n, W_MIX), F32),
        jax.ShapeDtypeStruct((taps, n, W_MIX), F32),
        jax.ShapeDtypeStruct((taps, n, D_QKV_W), F32),
    ]
    vec_t, vec_n, o_c, lru_h1, lru_conv1, gdn_conv1 = pl.pallas_call(
        _decode_pre_kernel,
        grid=(1,),
        in_specs=[full(a) for a in projs] + [layer_blk(a) for a in layered] + [_spec_of(a) for a in consts],
        out_specs=[pl.BlockSpec(s.shape, lambda i, nd=len(s.shape): (0,) * nd) for s in out_shapes],
        out_shape=out_shapes,
        compiler_params=pltpu.CompilerParams(dimension_semantics=("arbitrary",), vmem_limit_bytes=VMEM_LIMIT),
        name="decode_tokens",
    )(*projs, *layered, *[_arr_of(a) for a in consts])

    gam = jnp.broadcast_to((1.0 - 2.0 ** (-5.0 - jnp.arange(N_HEADS, dtype=F32)))[:, None, None], (N_HEADS, 1, n))
    state_in = pl.BlockSpec((None, flat, n), lambda h: (l, h, 0))
    state_out = pl.BlockSpec((flat, n), lambda h: (h, 0))
    wkv1, ret1, gdn1, o_t = pl.pallas_call(
        _decode_state_kernel,
        grid=(N_HEADS,),
        in_specs=[pl.BlockSpec((N_VEC_T, HEAD_DIM, n), lambda h: (0, h, 0)), state_in, state_in, state_in,
                  pl.BlockSpec((None, 1, n), lambda h: (h, 0, 0))],
        out_specs=[state_out, state_out, state_out, pl.BlockSpec((3, HEAD_DIM, n), lambda h: (0, h, 0))],
        out_shape=[jax.ShapeDtypeStruct((N_HEADS * flat, n), F32)] * 3 + [jax.ShapeDtypeStruct((3, W_MIX, n), F32)],
        compiler_params=pltpu.CompilerParams(dimension_semantics=("parallel",), vmem_limit_bytes=VMEM_LIMIT),
        name="decode_states",
    )(vec_t, carried["wkv"], carried["ret"], carried["gdn"], gam)

    finish_ins = [o_t, vec_n, _row(lp["rwkv_r_k"]), _row(lp["rwkv_ln_g"]), _row(lp["rwkv_ln_b"]), gdn_params[3]]
    o_a, o_b, o_d = pl.pallas_call(
        _decode_finish_kernel,
        grid=(1,),
        in_specs=[_spec_of(a) for a in finish_ins],
        out_specs=[pl.BlockSpec((n, W_MIX), lambda i: (0, 0))] * 3,
        out_shape=[jax.ShapeDtypeStruct((n, W_MIX), F32)] * 3,
        compiler_params=pltpu.CompilerParams(dimension_semantics=("arbitrary",), vmem_limit_bytes=VMEM_LIMIT),
        name="decode_finish",
    )(*[_arr_of(a) for a in finish_ins])
    new_states = (_batch_major(wkv1), p_a[:, :A_SHIFT_W], _batch_major(ret1), lru_h1,
                  jnp.transpose(lru_conv1, (1, 0, 2)), _batch_major(gdn1), jnp.transpose(gdn_conv1, (1, 0, 2)))
    return (o_a, o_b, o_c, o_d), new_states


def _prompt_layer(p_a, p_b, p_c, p_d, n_seq, seq_len, cos_t, sin_t, lp):
    o_a, wkv1, shift1 = _rwkv_call(p_a, n_seq, seq_len, lp)
    o_b, ret1 = _ret_call(p_b, n_seq, seq_len, cos_t, sin_t)
    o_c, lru_h1, lru_conv1 = _lru_call(p_c, n_seq, seq_len, lp)
    o_d, gdn1, gdn_conv1 = _gdn_call(p_d, n_seq, seq_len, lp)
    return (o_a, o_b, o_c, o_d), (wkv1, shift1, ret1, lru_h1, lru_conv1, gdn1, gdn_conv1)


def _run_group(x, mods, pos, carried, layers, final_g):
    n_seq, seq_len, d = x.shape
    x2 = x.reshape(n_seq * seq_len, d)
    cos_t, sin_t = _rope_tables(pos)
    new = []
    n_layers = len(layers)
    for l, lp in enumerate(layers):
        p_a, p_b, p_c, p_d = _inproj_call(x2, mods, lp["norm_g"], l, lp["w_t"], lp["w_ba"], seq_len)
        if carried is None:
            branches, st = _prompt_layer(p_a, p_b, p_c, p_d, n_seq, seq_len, cos_t, sin_t, lp)
        else:
            branches, st = _decode_layer(l, p_a, p_b, p_c, p_d, carried, cos_t, sin_t, lp)
        new.append(st)
        x2 = _outproj_call(x2, mods, lp["norm_g"], branches, l, lp["w_t"], lp["w_up_bf16"],
                           lp["w_out_bf16"], final_g, seq_len, final=(l == n_layers - 1))
    stacked = tuple(jnp.stack([s[i] for s in new], axis=0) for i in range(7))
    return x2.reshape(n_seq, seq_len, d), stacked


def kernel(x_prompt, x_sample, c_prompt, c_sample, state_rwkv_wkv, state_rwkv_shift, state_ret, state_lru_h, state_lru_conv, state_gdn, state_gdn_conv, ada_w, ada_b, norm_g, w_in, rwkv_mu, rwkv_w0, rwkv_w2, rwkv_a0, rwkv_a2, rwkv_k_k, rwkv_k_a, rwkv_r_k, rwkv_ln_g, rwkv_ln_b, lru_conv_w, lru_conv_b, lru_gate_w, lru_gate_b, lru_lambda, gdn_conv_w, gdn_A_log, gdn_dt_bias, gdn_norm_g, w_up, w_out, final_g):
    n_layers = ada_w.shape[0]
    n_prompt, seq_len, _ = x_prompt.shape
    n_sample, dec_len, _ = x_sample.shape
    assert dec_len == 1, "the decode path handles one token per sequence"
    w_in_t = jnp.swapaxes(w_in, 1, 2).astype(BF16)
    off_ba = A_W + B_W + C_W + D_QKV_W
    rows = lambda a: a.reshape(n_layers, 1, -1)
    stacked = dict(
        norm_g=rows(norm_g), w_up_bf16=w_up.astype(BF16), w_out_bf16=w_out.astype(BF16),
        w_ba=jnp.repeat(w_in_t[:, off_ba:off_ba + 2 * N_HEADS], HEAD_DIM, axis=1),
        rwkv_mu=rows(rwkv_mu), rwkv_w0=rows(rwkv_w0), rwkv_w2=rwkv_w2, rwkv_a0=rows(rwkv_a0), rwkv_a2=rwkv_a2,
        rwkv_k_k=rows(rwkv_k_k), rwkv_k_a=rows(rwkv_k_a), rwkv_r_k=rows(rwkv_r_k), rwkv_ln_g=rows(rwkv_ln_g),
        rwkv_ln_b=rows(rwkv_ln_b), lru_conv_w=lru_conv_w, lru_conv_b=rows(lru_conv_b),
        lru_gate_w=_block_diag_gates(lru_gate_w), lru_gate_b=rows(lru_gate_b), lru_lambda=rows(lru_lambda),
        gdn_conv_w=gdn_conv_w, gdn_A_log=rows(jnp.repeat(gdn_A_log, HEAD_DIM, axis=1)),
        gdn_dt_bias=rows(jnp.repeat(gdn_dt_bias, HEAD_DIM, axis=1)),
        gdn_norm_g=rows(jnp.tile(gdn_norm_g, (1, N_HEADS))))
    layers = [dict({name: _LayerParam(arr, l) for name, arr in stacked.items()}, w_t=w_in_t)
              for l in range(n_layers)]
    mods_p, mods_s = _ada_call(c_prompt, c_sample, ada_w, ada_b)

    y_prompt, new_p = _run_group(x_prompt, mods_p, jnp.arange(seq_len, dtype=jnp.int32), None, layers, final_g)
    carried = dict(wkv=_batch_minor(state_rwkv_wkv), ret=_batch_minor(state_ret), gdn=_batch_minor(state_gdn),
                   shift=state_rwkv_shift, lru_h=state_lru_h,
                   lru_conv=jnp.transpose(state_lru_conv, (0, 2, 1, 3)),
                   gdn_conv=jnp.transpose(state_gdn_conv, (0, 2, 1, 3)))
    pos_s = PAST_LEN + jnp.arange(dec_len, dtype=jnp.int32)
    y_sample, new_s = _run_group(x_sample, mods_s, pos_s, carried, layers, final_g)
    return (y_prompt, y_sample) + new_p + new_s
```

```python
import functools
import math

import jax
import jax.numpy as jnp
from jax import lax
from jax.experimental import pallas as pl
from jax.experimental.pallas import tpu as pltpu

F32 = jnp.float32
BF16 = jnp.bfloat16
HI = lax.Precision.HIGHEST

N_HEADS = 4
HEAD_DIM = 64
W_MIX = N_HEADS * HEAD_DIM
LORA = 64
CONV_W = 4
N_BRANCH = 4
LRU_C = 8.0
ROPE_BASE = 10000.0
EPS = 1e-6
RWKV_GN_EPS = 64e-5
PAST_LEN = 16384
A_SHIFT_W = 3 * W_MIX + 2 * LORA
A_W = A_SHIFT_W + W_MIX
B_W = 4 * W_MIX
C_W = 2 * W_MIX
D_QKV_W = 3 * W_MIX
D_W = D_QKV_W + 2 * N_HEADS + W_MIX
D_PACK_W = D_QKV_W + 3 * W_MIX

SUBLANES = 8
LANES = 128
VMEM_LIMIT = 56 * 1024 * 1024

CHUNK = 64
RET_CHUNK = 128
INV_BLOCK = 16
WAVE = 16
ROW_TILE = 1024
LRU_TILE = 512
PROJ_TILE = 512
OUT_TILE = 1024
N_VEC_T = 14
N_VEC_PLAIN = 6


def _mm(a, b, prec=HI):
    return lax.dot_general(a, b, (((1,), (0,)), ((), ())), precision=prec, preferred_element_type=F32)


def _mm_nt(a, b, prec=HI):
    return lax.dot_general(a, b, (((1,), (1,)), ((), ())), precision=prec, preferred_element_type=F32)


def _mm_tn(a, b, prec=HI):
    return lax.dot_general(a, b, (((0,), (0,)), ((), ())), precision=prec, preferred_element_type=F32)


_NN = (((1,), (0,)), ((), ()))
_NT = (((1,), (1,)), ((), ()))
_TN = (((0,), (0,)), ((), ()))

P_INV = 1
P_STATE = 1
P_MISC = 1
HEAD_SUM_PIECES = 1
CUMSUM_PIECES = 2


class _Split:
    def __init__(self, x, passes):
        self.hi = x.astype(BF16)
        self.lo = (x - self.hi.astype(F32)).astype(BF16) if passes > 1 else None


def _dotp(a, b, dims=_NN, passes=1):
    a = a if isinstance(a, _Split) else _Split(a, passes)
    b = b if isinstance(b, _Split) else _Split(b, passes)
    d = lambda x, y: lax.dot_general(x, y, dims, preferred_element_type=F32)
    out = d(a.hi, b.hi)
    if passes > 1:
        out = out + (d(a.hi, b.lo) + d(a.lo, b.hi))
    return out


def _iota(shape, dim):
    return lax.broadcasted_iota(jnp.int32, shape, dim)


def _sigmoid(x):
    return 0.5 * jnp.tanh(0.5 * x) + 0.5


def _silu(x):
    return x * _sigmoid(x)


def _softplus(x):
    return jnp.maximum(x, 0.0) + jnp.log(1.0 + jnp.exp(-jnp.abs(x)))


def _pieces(x, n):
    out = []
    for i in range(n):
        p = x.astype(BF16)
        out.append(p)
        if i + 1 < n:
            x = x - p.astype(F32)
    return out


def _dot_const(x, const, dims=_NN, n=2, const_left=False):
    out = None
    for p in _pieces(x, n):
        t = lax.dot_general(*((const, p) if const_left else (p, const)), dims, preferred_element_type=F32)
        out = t if out is None else out + t
    return out


def _head_ones():
    return (_iota((W_MIX, W_MIX), 0) // HEAD_DIM == _iota((W_MIX, W_MIX), 1) // HEAD_DIM).astype(BF16)


def _head_sum(x, ones, signed=False):
    return _dot_const(x, ones, n=HEAD_SUM_PIECES + (1 if signed else 0))


def _rms(x):
    return x * lax.rsqrt(jnp.mean(x * x, axis=-1, keepdims=True) + EPS)


def _inv_unit_lower(a):
    return _inv_unit_lower_many([a])[0]


def _inv_unit_lower_many(mats):
    n = mats[0].shape[0]
    ri, ci = _iota((n, n), 0), _iota((n, n), 1)
    eye = (ri == ci).astype(F32)
    diag_blk = (ri // INV_BLOCK) == (ci // INV_BLOCK)
    mm = lambda x, y: _dotp(x, y, _NN, P_INV)
    sp = lambda x: _Split(x, P_INV)
    d = [jnp.where(diag_blk, a, 0.0) for a in mats]
    nb = [a - di for a, di in zip(mats, d)]
    td = [eye - di for di in d]
    p = d
    for _ in range(int(math.log2(INV_BLOCK)) - 1):
        ps = [sp(pi) for pi in p]
        p = [mm(pi, pi) for pi in ps]
        td = [mm(ti, eye + pi) for ti, pi in zip(td, p)]
    tds = [sp(ti) for ti in td]
    x = [mm(ti, ni) for ti, ni in zip(tds, nb)]
    t = [eye - xi for xi in x]
    p = x
    for _ in range(int(math.log2(n // INV_BLOCK)) - 1):
        ps = [sp(pi) for pi in p]
        p = [mm(pi, pi) for pi in ps]
        t = [mm(ti, eye + pi) for ti, pi in zip(t, p)]
    return [mm(ti, tdi) for ti, tdi in zip(t, tds)]


class _HeadAlgebra:
    def __init__(self, c):
        assert c == HEAD_DIM, "side-by-side head products need CHUNK == HEAD_DIM"
        w = W_MIX
        row, lane = _iota((c, w), 0), _iota((c, w), 1)
        col = lane % HEAD_DIM
        tile_lane = _iota((c, LANES), 1)
        self.tile_head = [tile_lane // HEAD_DIM == h for h in range(LANES // HEAD_DIM)]
        self.eye = (row == col).astype(F32)
        self.strict = row > col
        self.incl = row >= col
        self.inv_blk = (row // INV_BLOCK) == (col // INV_BLOCK)
        r2, c2 = _iota((w, w), 0), _iota((w, w), 1)
        self.eye_full = r2 == c2
        self.same_head = (r2 // HEAD_DIM) == (c2 // HEAD_DIM)

    def bd(self, y):
        yb = y.astype(BF16)
        zero = jnp.zeros((yb.shape[0], LANES), BF16)
        blocks = []
        for t in range(W_MIX // LANES):
            tile = yb[:, t * LANES:(t + 1) * LANES]
            for m in self.tile_head:
                kept = jnp.where(m, tile, zero)
                blocks.append(jnp.concatenate([kept if s == t else zero for s in range(W_MIX // LANES)], axis=1))
        return jnp.concatenate(blocks, axis=0)

    def nn(self, x, bd_y, out=F32):
        return lax.dot_general(x.astype(BF16), bd_y, _NN, preferred_element_type=F32).astype(out)

    def nt(self, x, bd_y):
        return lax.dot_general(x.astype(BF16), bd_y, _NT, preferred_element_type=F32)

    def tn_bd(self, x, y):
        full = lax.dot_general(x.astype(BF16), y.astype(BF16), _TN, preferred_element_type=F32)
        return jnp.where(self.same_head, full, 0.0)

    def diag_bd(self, row_vec):
        return jnp.where(self.eye_full, row_vec, 0.0)

    def plus_eye(self, bd_p):
        return jnp.where(self.eye_full, jnp.ones_like(bd_p), bd_p)

    def inv_unit_lower_many(self, mats):
        mats = [a.astype(BF16) for a in mats]
        zero = jnp.zeros_like(mats[0])
        eye = self.eye.astype(BF16)
        d = [jnp.where(self.inv_blk, a, zero) for a in mats]
        nb = [jnp.where(self.inv_blk, zero, a) for a in mats]
        td = [eye - di for di in d]
        p = d
        bdp = [self.bd(pi) for pi in p]
        for _ in range(int(math.log2(INV_BLOCK)) - 1):
            p = [self.nn(pi, bi, BF16) for pi, bi in zip(p, bdp)]
            bdp = [self.bd(pi) for pi in p]
            td = [self.nn(ti, self.plus_eye(bi), BF16) for ti, bi in zip(td, bdp)]
        bd_td = [self.bd(ti) for ti in td]
        x = [self.nn(ti, self.bd(ni), BF16) for ti, ni in zip(td, nb)]
        t = [eye - xi for xi in x]
        p = x
        bdp = [self.bd(pi) for pi in p]
        for _ in range(int(math.log2(HEAD_DIM // INV_BLOCK)) - 1):
            p = [self.nn(pi, bi, BF16) for pi, bi in zip(p, bdp)]
            bdp = [self.bd(pi) for pi in p]
            t = [self.nn(ti, self.plus_eye(bi), BF16) for ti, bi in zip(t, bdp)]
        return [self.nn(ti, bi, BF16) for ti, bi in zip(t, bd_td)]


def _ada_kernel(cp_ref, cs_ref, w_ref, b_ref, op_ref, os_ref):
    w = _Split(w_ref[...], 3)
    op_ref[...] = _dotp(_silu(cp_ref[...]), w, _NN, 3) + b_ref[...]
    os_ref[...] = _dotp(_silu(cs_ref[...]), w, _NN, 3) + b_ref[...]


def _ada_call(c_prompt, c_sample, ada_w, ada_b):
    n_layers, d, d3 = ada_w.shape
    n_p, n_s = c_prompt.shape[0], c_sample.shape[0]
    return pl.pallas_call(
        _ada_kernel,
        grid=(n_layers, d3 // d),
        in_specs=[
            pl.BlockSpec((n_p, d), lambda l, j: (0, 0)),
            pl.BlockSpec((n_s, d), lambda l, j: (0, 0)),
            pl.BlockSpec((None, d, d), lambda l, j: (l, 0, j)),
            pl.BlockSpec((None, 1, d), lambda l, j: (l, 0, j)),
        ],
        out_specs=[pl.BlockSpec((None, n_p, d), lambda l, j: (l, 0, j)),
                   pl.BlockSpec((None, n_s, d), lambda l, j: (l, 0, j))],
        out_shape=[jax.ShapeDtypeStruct((n_layers, n_p, d3), F32), jax.ShapeDtypeStruct((n_layers, n_s, d3), F32)],
        compiler_params=pltpu.CompilerParams(dimension_semantics=("arbitrary", "arbitrary"),
                                             vmem_limit_bytes=VMEM_LIMIT),
        name="ada_mod",
    )(c_prompt, c_sample, ada_w, ada_b.reshape(n_layers, 1, d3))


def _modulated_norm(x, g, scale, shift):
    return _rms(x) * g * (1.0 + scale) + shift


def _inproj_kernel(x_ref, sc_ref, sh_ref, g_ref, w_ref, wz_ref, wba_ref, oa_ref, ob_ref, oc_ref, od_ref):
    h = _modulated_norm(x_ref[...], g_ref[...], sc_ref[...], sh_ref[...]).astype(BF16)
    proj = lambda w: lax.dot_general(h, w, _NT, preferred_element_type=F32)
    lo = 0
    for o_ref in (oa_ref, ob_ref, oc_ref):
        wd = o_ref.shape[-1]
        o_ref[...] = proj(w_ref[0, lo:lo + wd, :])
        lo += wd
    od_ref[:, :D_QKV_W] = proj(w_ref[0, lo:lo + D_QKV_W, :])
    od_ref[:, D_QKV_W:D_QKV_W + W_MIX] = proj(wz_ref[0])
    od_ref[:, D_QKV_W + W_MIX:] = proj(wba_ref[...])


def _weight_rows(layer, row0, n_rows, d):
    return pl.BlockSpec((pl.Element(1), pl.Element(n_rows), pl.Element(d)), lambda i: (layer, row0, 0))


MOD_SHIFT, MOD_SCALE, MOD_GATE = 0, 1, 2


def _mod_specs(mods, layer, parts, d, tm, seq_len):
    if seq_len == 1:
        return [mods] * len(parts), [pl.BlockSpec((None, tm, d), lambda i, c=c: (layer, i, c)) for c in parts]
    per_seq = seq_len // tm
    by_seq = mods.reshape(mods.shape[0], mods.shape[1], 1, mods.shape[2])
    return ([by_seq] * len(parts),
            [pl.BlockSpec((None, None, 1, d), lambda i, c=c: (layer, i // per_seq, 0, c)) for c in parts])


class _LayerParam:
    def __init__(self, arr, layer):
        self.arr, self.layer = arr, layer

    def spec(self):
        return pl.BlockSpec((None,) + self.arr.shape[1:], lambda *_: (self.layer,) + (0,) * (self.arr.ndim - 1))


def _spec_of(a):
    if isinstance(a, _LayerParam):
        return a.spec()
    return pl.BlockSpec(a.shape, lambda *_: (0,) * a.ndim)


def _arr_of(a):
    return a.arr if isinstance(a, _LayerParam) else a


def _inproj_call(x2, mods_all, g, l, w_t, w_ba, seq_len):
    m, d = x2.shape
    tm = min(PROJ_TILE, m, seq_len) if seq_len > 1 else m
    widths = (A_W, B_W, C_W, D_PACK_W)
    mods, mod_specs = _mod_specs(mods_all, l, (MOD_SCALE, MOD_SHIFT), d, tm, seq_len)
    off_ba = A_W + B_W + C_W + D_QKV_W
    off_z = off_ba + 2 * N_HEADS
    return pl.pallas_call(
        _inproj_kernel,
        grid=(m // tm,),
        in_specs=[pl.BlockSpec((tm, d), lambda i: (i, 0))] + mod_specs + [
            _spec_of(g),
            _weight_rows(l, 0, off_ba, d),
            _weight_rows(l, off_z, W_MIX, d),
            _spec_of(w_ba),
        ],
        out_specs=[pl.BlockSpec((tm, wd), lambda i: (i, 0)) for wd in widths],
        out_shape=[jax.ShapeDtypeStruct((m, wd), F32) for wd in widths],
        compiler_params=pltpu.CompilerParams(dimension_semantics=("parallel",), vmem_limit_bytes=VMEM_LIMIT),
        name="in_proj",
    )(x2, *mods, _arr_of(g), w_t, w_t, _arr_of(w_ba))


def _outproj_kernel(x_ref, sc_ref, sh_ref, gt_ref, g_ref, ba_ref, bb_ref, bc_ref, bd_ref,
                    wg_ref, wup_ref, wout_ref, fg_ref, o_ref, *, final):
    x = x_ref[...]
    d = x.shape[-1]
    h = _modulated_norm(x, g_ref[...], sc_ref[...], sh_ref[...]).astype(BF16)
    merged = jnp.zeros(x.shape, F32)
    for n, br_ref in enumerate((ba_ref, bb_ref, bc_ref, bd_ref)):
        gl = lax.dot_general(h, wg_ref[0, n * d:(n + 1) * d, :], _NT, preferred_element_type=F32)
        up = jnp.dot(br_ref[...].astype(BF16), wup_ref[n], preferred_element_type=F32)
        merged = merged + _sigmoid(gl) * up
    out = jnp.dot(merged.astype(BF16), wout_ref[...], preferred_element_type=F32)
    xn = x + gt_ref[...] * out
    if final:
        xn = _rms(xn) * fg_ref[...]
    o_ref[...] = xn


def _outproj_call(x2, mods_all, g, branches, l, w_t, wup, wout, final_g, seq_len, final):
    m, d = x2.shape
    tm = min(OUT_TILE, m, seq_len) if seq_len > 1 else m
    mods, mod_specs = _mod_specs(mods_all, l, (MOD_SCALE, MOD_SHIFT, MOD_GATE), d, tm, seq_len)
    full = lambda a: pl.BlockSpec(a.shape, lambda i: (0,) * a.ndim)
    off_g = A_W + B_W + C_W + D_W
    gate_rows = _weight_rows(l, off_g, N_BRANCH * d, d)
    return pl.pallas_call(
        functools.partial(_outproj_kernel, final=final),
        grid=(m // tm,),
        in_specs=[pl.BlockSpec((tm, d), lambda i: (i, 0))] + mod_specs + [_spec_of(g)]
        + [pl.BlockSpec((tm, W_MIX), lambda i: (i, 0)) for _ in branches]
        + [gate_rows, _spec_of(wup), _spec_of(wout), pl.BlockSpec((1, d), lambda i: (0, 0))],
        out_specs=pl.BlockSpec((tm, d), lambda i: (i, 0)),
        out_shape=jax.ShapeDtypeStruct((m, d), F32),
        compiler_params=pltpu.CompilerParams(dimension_semantics=("parallel",), vmem_limit_bytes=VMEM_LIMIT),
        name="out_proj",
    )(x2, *mods, _arr_of(g), *branches, w_t, _arr_of(wup), _arr_of(wout), final_g.reshape(1, d))


def _rwkv_token_math(pm, w0, w2, a0, a2, k_k, k_a, ones):
    r = pm[:, 0:W_MIX]
    k = pm[:, W_MIX:2 * W_MIX]
    v = pm[:, 2 * W_MIX:3 * W_MIX]
    wd = pm[:, 3 * W_MIX:3 * W_MIX + LORA]
    ad = pm[:, 3 * W_MIX + LORA:]
    w_log = -_softplus(-(w0 + _dotp(jnp.tanh(wd), w2, _NN, P_MISC))) - 0.5
    log_decay = -jnp.exp(w_log)
    a = _sigmoid(a0 + _dotp(ad, a2, _NN, P_MISC))
    kx = k * k_k
    kk = kx * lax.rsqrt(_head_sum(kx * kx, ones) + EPS)
    k = k * (1.0 + (a - 1.0) * k_a)
    return r, k, v, log_decay, -kk, kk * a


def _rwkv_finish(o, r, k, v, z, r_k, ln_g, ln_b, ones):
    mean = _head_sum(o, ones) * (1.0 / HEAD_DIM)
    dlt = o - mean
    var = _head_sum(dlt * dlt, ones) * (1.0 / HEAD_DIM)
    on = dlt * lax.rsqrt(var + RWKV_GN_EPS) * ln_g + ln_b
    bonus = _head_sum(r * k * r_k, ones, signed=True) * v
    return (on + bonus) * _silu(z)


def _swap_halves(x):
    half = HEAD_DIM // 2
    n = x.shape[-1]
    first = (_iota(x.shape, 1) & half) == 0
    return jnp.where(first, pltpu.roll(x, n - half, axis=1), pltpu.roll(x, half, axis=1))


def _rotary(x, cos, sin):
    return x * cos + _swap_halves(x) * sin


def _lru_token_math(xc, gate_w, gate_b, lam):
    gates = _dotp(xc, gate_w, _NN, P_MISC) + gate_b
    r_gate = _sigmoid(gates[:, :W_MIX])
    i_gate = _sigmoid(gates[:, W_MIX:])
    log_a = -LRU_C * r_gate * _softplus(-lam)
    a = jnp.exp(log_a)
    b = jnp.sqrt(1.0 - jnp.exp(2.0 * log_a)) * (i_gate * xc)
    return a, b


def _gdn_token_math(qkv, b_raw, a_raw, a_log, dt_bias, ones):
    qkv = _silu(qkv)
    q = qkv[:, 0:W_MIX]
    k = qkv[:, W_MIX:2 * W_MIX]
    v = qkv[:, 2 * W_MIX:]
    q = q * lax.rsqrt(_head_sum(q * q, ones) + EPS) * (HEAD_DIM ** -0.5)
    k = k * lax.rsqrt(_head_sum(k * k, ones) + EPS)
    beta = _sigmoid(b_raw)
    g = -jnp.exp(a_log) * _softplus(a_raw + dt_bias)
    return q, k, v, beta, g


def _head_rms_finish(o, z, ones, gain=None):
    y = o * lax.rsqrt(_head_sum(o * o, ones) * (1.0 / HEAD_DIM) + EPS)
    if gain is not None:
        y = y * gain
    return y * _silu(z)


def _conv_tile(u, ext_ref, w_ref, first):
    n = u.shape[0]

    @pl.when(first)
    def _():
        ext_ref[0:SUBLANES, :] = jnp.zeros((SUBLANES, u.shape[1]), F32)

    ext_ref[SUBLANES:SUBLANES + n, :] = u
    out = None
    for j in range(CONV_W):
        term = _rows_back(u, ext_ref, CONV_W - 1 - j) * w_ref[j:j + 1, :]
        out = term if out is None else out + term
    ext_ref[0:SUBLANES, :] = u[n - SUBLANES:n, :]
    return out


def _rows_back(u, ext_ref, back):
    if back == 0:
        return u
    n, ch = u.shape
    tiles = (n // SUBLANES, SUBLANES, ch)
    pos = _iota((1, SUBLANES, 1), 1)
    earlier = ext_ref[0:n, :].reshape(tiles)
    return pltpu.roll(jnp.where(pos >= SUBLANES - back, earlier, u.reshape(tiles)), back, axis=1).reshape(n, ch)


def _rwkv_kernel(p_ref, mu_ref, w0_ref, w2_ref, a0_ref, a2_ref, kk_ref, ka_ref, rk_ref, lng_ref, lnb_ref,
                 o_ref, s_out_ref, shift_out_ref, s_scr, ext_scr):
    j = pl.program_id(1)
    last = pl.num_programs(1) - 1
    ct = p_ref.shape[0]

    @pl.when(j == 0)
    def _():
        s_scr[...] = jnp.zeros(s_scr.shape, F32)
        ext_scr[0:SUBLANES, :] = jnp.zeros((SUBLANES, A_SHIFT_W), F32)

    p = p_ref[...]
    pa = p[:, :A_SHIFT_W]
    z = p[:, A_SHIFT_W:]
    ext_scr[SUBLANES:SUBLANES + ct, :] = pa
    prev = _rows_back(pa, ext_scr, 1)
    ext_scr[0:SUBLANES, :] = pa[ct - SUBLANES:ct, :]
    pm = pa + (prev - pa) * mu_ref[...]
    ones = _head_ones()
    r, k, v, ld, av, bv = _rwkv_token_math(pm, w0_ref[...], w2_ref[...], a0_ref[...], a2_ref[...],
                                           kk_ref[...], ka_ref[...], ones)

    c = min(CHUNK, ct)
    ha = _HeadAlgebra(c)
    lt = (_iota((c, c), 0) >= _iota((c, c), 1)).astype(BF16)
    units = []
    for c0 in range(0, ct, c):
        sl = slice(c0, c0 + c)
        ldc = ld[sl]
        cum = _dot_const(ldc, lt, _NN, CUMSUM_PIECES, const_left=True)
        e_neg = jnp.exp(-cum)
        e_out = jnp.exp(cum[c - 1:c, :] - cum)
        units.append(dict(a=av[sl] * jnp.exp(cum - ldc), r=r[sl] * jnp.exp(cum), b=bv[sl] * e_neg, k=k[sl] * e_neg,
                          bo=bv[sl] * e_out, ko=k[sl] * e_out, v=v[sl], g=jnp.exp(cum[c - 1:c, :])))
    all_units = units
    state = s_scr[...]
    o_rows = []
    for w0 in range(0, len(all_units), WAVE):
        units = all_units[w0:w0 + WAVE]
        for u in units:
            lhs = jnp.concatenate([u["a"], u["r"]], axis=0)
            u["mb"] = ha.nt(lhs, ha.bd(u["b"]))
            u["mk"] = ha.nt(lhs, ha.bd(u["k"]))
            u["bd_v"] = ha.bd(u["v"])
        for u in units:
            u["m_ab"] = jnp.where(ha.strict, u["mb"][:c], 0.0)
            u["m_ak"] = jnp.where(ha.strict, u["mk"][:c], 0.0)
            u["m_rb"] = jnp.where(ha.incl, u["mb"][c:], 0.0)
            u["m_rk"] = jnp.where(ha.incl, u["mk"][c:], 0.0)
        for u, t_inv in zip(units, ha.inv_unit_lower_many([-u["m_ab"] for u in units])):
            u["t_inv"] = t_inv
        for u in units:
            u["makv"] = ha.nn(u["m_ak"], u["bd_v"], BF16)
        for u in units:
            u["a_hat"] = ha.nn(u["t_inv"], ha.bd(u["a"]), BF16)
            u["u1"] = ha.nn(u["t_inv"], ha.bd(u["makv"]), BF16)
        for u in units:
            u["r_hat"] = u["r"] + ha.nn(u["m_rb"], ha.bd(u["a_hat"]))
            u["o1"] = ha.nn(u["m_rb"], ha.bd(u["u1"])) + ha.nn(u["m_rk"], u["bd_v"])
            u["g_t"] = ha.diag_bd(u["g"]) + ha.tn_bd(u["bo"], u["a_hat"])
            u["h_t"] = ha.tn_bd(jnp.concatenate([u["bo"], u["ko"]], axis=0),
                                jnp.concatenate([u["u1"], u["v"].astype(BF16)], axis=0))
            zz = lax.dot_general(jnp.concatenate([u["r_hat"], u["g_t"]], axis=0).astype(BF16), state.astype(BF16),
                                 _NN, preferred_element_type=F32)
            o_rows.append(zz[:c] + u["o1"])
            state = zz[c:] + u["h_t"]
    o = o_rows[0] if len(o_rows) == 1 else jnp.concatenate(o_rows, axis=0)
    s_scr[...] = state
    o_ref[...] = _rwkv_finish(o, r, k, v, z, rk_ref[...], lng_ref[...], lnb_ref[...], ones)

    @pl.when(j == last)
    def _():
        eye_h = (_iota((HEAD_DIM, HEAD_DIM), 0) == _iota((HEAD_DIM, HEAD_DIM), 1)).astype(F32)
        for h in range(N_HEADS):
            hs = slice(h * HEAD_DIM, (h + 1) * HEAD_DIM)
            s_out_ref[h] = _mm_nt(eye_h, state[hs, hs])
        shift_out_ref[...] = pa[ct - 1:ct, :]


def _row(a):
    return a if isinstance(a, _LayerParam) else a.reshape(1, -1)


def _rwkv_call(p_a, n_seq, seq_len, lp):
    ct = min(ROW_TILE, seq_len)
    p3 = p_a.reshape(n_seq, seq_len, A_W)
    params = [_row(lp["rwkv_mu"]), _row(lp["rwkv_w0"]), lp["rwkv_w2"], _row(lp["rwkv_a0"]), lp["rwkv_a2"],
              _row(lp["rwkv_k_k"]), _row(lp["rwkv_k_a"]), _row(lp["rwkv_r_k"]), _row(lp["rwkv_ln_g"]),
              _row(lp["rwkv_ln_b"])]
    o, s1, shift1 = pl.pallas_call(
        _rwkv_kernel,
        grid=(n_seq, seq_len // ct),
        in_specs=[pl.BlockSpec((None, ct, A_W), lambda b, j: (b, j, 0))]
        + [_spec_of(a) for a in params],
        out_specs=[
            pl.BlockSpec((None, ct, W_MIX), lambda b, j: (b, j, 0)),
            pl.BlockSpec((None, N_HEADS, HEAD_DIM, HEAD_DIM), lambda b, j: (b, 0, 0, 0)),
            pl.BlockSpec((None, 1, A_SHIFT_W), lambda b, j: (b, 0, 0)),
        ],
        out_shape=[
            jax.ShapeDtypeStruct((n_seq, seq_len, W_MIX), F32),
            jax.ShapeDtypeStruct((n_seq, N_HEADS, HEAD_DIM, HEAD_DIM), F32),
            jax.ShapeDtypeStruct((n_seq, 1, A_SHIFT_W), F32),
        ],
        scratch_shapes=[pltpu.VMEM((W_MIX, W_MIX), F32), pltpu.VMEM((ct + SUBLANES, A_SHIFT_W), F32)],
        compiler_params=pltpu.CompilerParams(dimension_semantics=("parallel", "arbitrary"),
                                             vmem_limit_bytes=VMEM_LIMIT),
        name="rwkv7_prompt",
    )(p3, *[_arr_of(a) for a in params])
    return o.reshape(n_seq * seq_len, W_MIX), s1, shift1.reshape(n_seq, A_SHIFT_W)


def _ret_kernel(p_ref, cos_ref, sin_ref, o_ref, s_out_ref, s_scr):
    j = pl.program_id(1)
    last = pl.num_programs(1) - 1
    ct = p_ref.shape[0]

    @pl.when(j == 0)
    def _():
        s_scr[...] = jnp.zeros(s_scr.shape, F32)

    p = p_ref[...]
    cos, sin = cos_ref[...], sin_ref[...]
    q = _rotary(p[:, 0:W_MIX], cos, sin)
    k = _rotary(p[:, W_MIX:2 * W_MIX], cos, sin) * (HEAD_DIM ** -0.5)
    v = p[:, 2 * W_MIX:3 * W_MIX]
    z = p[:, 3 * W_MIX:]
    c = min(RET_CHUNK, ct)
    ri, ci = _iota((c, c), 0), _iota((c, c), 1)
    causal = ri >= ci
    rel = jnp.where(causal, ri - ci, 0).astype(F32)
    idx = _iota((c, 1), 0).astype(F32)
    states = [s_scr[h] for h in range(N_HEADS)]
    units = []
    for h in range(N_HEADS):
        lg = math.log(1.0 - 2.0 ** (-5.0 - h))
        consts = dict(decay=jnp.where(causal, jnp.exp(lg * rel), 0.0), q_dec=jnp.exp(lg * (idx + 1.0)),
                      k_dec=jnp.exp(lg * (c - 1.0 - idx)), g_c=math.exp(lg * c))
        hs = slice(h * HEAD_DIM, (h + 1) * HEAD_DIM)
        for c0 in range(0, ct, c):
            sl = slice(c0, c0 + c)
            units.append(dict(consts, h=h, q=_Split(q[sl, hs], 1), k=k[sl, hs], v=_Split(v[sl, hs], 1)))
    for u in units:
        u["s_in"] = _dotp(u["q"], u["k"], _NT, P_MISC) * u["decay"]
        u["kv"] = _dotp(u["k"] * u["k_dec"], u["v"], _TN, P_MISC)
    for u in units:
        u["o"] = _dotp(u["s_in"], u["v"], _NN, P_MISC)
    for u in units:
        u["s0"] = states[u["h"]]
        states[u["h"]] = u["s0"] * u["g_c"] + u["kv"]
    o_heads_all = [[] for _ in range(N_HEADS)]
    for u in units:
        o_heads_all[u["h"]].append(u["o"] + _dotp(u["q"], u["s0"], _NN, P_MISC) * u["q_dec"])
    cols = [oh[0] if len(oh) == 1 else jnp.concatenate(oh, axis=0) for oh in o_heads_all]
    o = jnp.concatenate(cols, axis=1)
    for h in range(N_HEADS):
        s_scr[h] = states[h]
    o_ref[...] = _head_rms_finish(o, z, _head_ones())

    @pl.when(j == last)
    def _():
        for h in range(N_HEADS):
            s_out_ref[h] = states[h]


def _rope_tables(pos):
    half = HEAD_DIM // 2
    inv = ROPE_BASE ** (-jnp.arange(half, dtype=F32) / half)
    ang = pos.astype(F32)[:, None] * inv[None, :]
    cos, sin = jnp.cos(ang), jnp.sin(ang)
    cos_t = jnp.tile(jnp.concatenate([cos, cos], axis=-1), (1, N_HEADS))
    sin_t = jnp.tile(jnp.concatenate([-sin, sin], axis=-1), (1, N_HEADS))
    return cos_t, sin_t


def _ret_call(p_b, n_seq, seq_len, cos_t, sin_t):
    ct = min(ROW_TILE, seq_len)
    p3 = p_b.reshape(n_seq, seq_len, B_W)
    o, s1 = pl.pallas_call(
        _ret_kernel,
        grid=(n_seq, seq_len // ct),
        in_specs=[
            pl.BlockSpec((None, ct, B_W), lambda b, j: (b, j, 0)),
            pl.BlockSpec((ct, W_MIX), lambda b, j: (j, 0)),
            pl.BlockSpec((ct, W_MIX), lambda b, j: (j, 0)),
        ],
        out_specs=[
            pl.BlockSpec((None, ct, W_MIX), lambda b, j: (b, j, 0)),
            pl.BlockSpec((None, N_HEADS, HEAD_DIM, HEAD_DIM), lambda b, j: (b, 0, 0, 0)),
        ],
        out_shape=[
            jax.ShapeDtypeStruct((n_seq, seq_len, W_MIX), F32),
            jax.ShapeDtypeStruct((n_seq, N_HEADS, HEAD_DIM, HEAD_DIM), F32),
        ],
        scratch_shapes=[pltpu.VMEM((N_HEADS, HEAD_DIM, HEAD_DIM), F32)],
        compiler_params=pltpu.CompilerParams(dimension_semantics=("parallel", "arbitrary"),
                                             vmem_limit_bytes=VMEM_LIMIT),
        name="retention_prompt",
    )(p3, cos_t, sin_t)
    return o.reshape(n_seq * seq_len, W_MIX), s1


def _affine_scan(a, b, span):
    n, w = a.shape
    if span == SUBLANES and n > span:
        shape, axis = (n // span, span, w), 1
        a, b = a.reshape(shape), b.reshape(shape)
        pos = _iota((1, span, 1), 1)
    else:
        assert span >= n
        axis = 0
        pos = _iota((n, 1), 0)
    dist = 1
    while dist < span:
        keep = pos >= dist
        a_prev = jnp.where(keep, pltpu.roll(a, dist, axis=axis), 1.0)
        b_prev = jnp.where(keep, pltpu.roll(b, dist, axis=axis), 0.0)
        b = a * b_prev + b
        a = a * a_prev
        dist *= 2
    return a.reshape(n, w), b.reshape(n, w)


def _lru_kernel(p_ref, cw_ref, cb_ref, gw_ref, gb_ref, sp_ref, o_ref, h_out_ref, conv_out_ref,
                ext_scr, h_scr, ab_scr, hin_scr):
    j = pl.program_id(1)
    last = pl.num_programs(1) - 1
    ct = p_ref.shape[0]

    @pl.when(j == 0)
    def _():
        h_scr[...] = jnp.zeros(h_scr.shape, F32)

    p = p_ref[...]
    xr = p[:, :W_MIX]
    z = p[:, W_MIX:]
    xc = _conv_tile(xr, ext_scr, cw_ref, j == 0) + cb_ref[...]
    a, b = _lru_token_math(xc, gw_ref[...], gb_ref[...], sp_ref[...])
    n_grp = ct // SUBLANES
    a, b = _affine_scan(a, b, SUBLANES)
    n_tiles = W_MIX // LANES
    for t in range(n_tiles):
        ab_scr[t] = a[:, t * LANES:(t + 1) * LANES]
        ab_scr[n_tiles + t] = b[:, t * LANES:(t + 1) * LANES]
    ends = pl.ds(SUBLANES - 1, n_grp, stride=SUBLANES)
    a_end = jnp.concatenate([ab_scr[t, ends, :] for t in range(n_tiles)], axis=1)
    b_end = jnp.concatenate([ab_scr[n_tiles + t, ends, :] for t in range(n_tiles)], axis=1)
    a_end, b_end = _affine_scan(a_end, b_end, n_grp)
    h_prev = h_scr[...]
    h_end = a_end * h_prev + b_end
    grp = _iota((n_grp, 1), 0)
    hin_scr[...] = jnp.where(grp == 0, h_prev, pltpu.roll(h_end, 1, axis=0))
    h_in = jnp.concatenate([jnp.broadcast_to(hin_scr[g:g + 1, :], (SUBLANES, W_MIX)) for g in range(n_grp)], axis=0)
    hcur = a * h_in + b
    h_scr[...] = h_end[n_grp - 1:n_grp, :]
    o_ref[...] = hcur * _silu(z)

    @pl.when(j == last)
    def _():
        h_out_ref[...] = hcur[ct - 1:ct, :]
        conv_out_ref[...] = xr[ct - SUBLANES:ct, :]


def _block_diag_gates(gate_w):
    out = jnp.zeros((gate_w.shape[0], W_MIX, 2 * W_MIX), F32)
    for g in range(2):
        for n in range(N_HEADS):
            out = out.at[:, n * HEAD_DIM:(n + 1) * HEAD_DIM,
                         g * W_MIX + n * HEAD_DIM:g * W_MIX + (n + 1) * HEAD_DIM].set(gate_w[:, g, n])
    return out


def _lru_params(lp):
    return [lp["lru_conv_w"], lp["lru_conv_b"], lp["lru_gate_w"], lp["lru_gate_b"], lp["lru_lambda"]]


def _lru_call(p_c, n_seq, seq_len, lp):
    ct = min(LRU_TILE, seq_len)
    p3 = p_c.reshape(n_seq, seq_len, C_W)
    params = _lru_params(lp)
    o, h1, conv_tail = pl.pallas_call(
        _lru_kernel,
        grid=(n_seq, seq_len // ct),
        in_specs=[pl.BlockSpec((None, ct, C_W), lambda b, j: (b, j, 0))]
        + [_spec_of(a) for a in params],
        out_specs=[
            pl.BlockSpec((None, ct, W_MIX), lambda b, j: (b, j, 0)),
            pl.BlockSpec((None, 1, W_MIX), lambda b, j: (b, 0, 0)),
            pl.BlockSpec((None, SUBLANES, W_MIX), lambda b, j: (b, 0, 0)),
        ],
        out_shape=[
            jax.ShapeDtypeStruct((n_seq, seq_len, W_MIX), F32),
            jax.ShapeDtypeStruct((n_seq, 1, W_MIX), F32),
            jax.ShapeDtypeStruct((n_seq, SUBLANES, W_MIX), F32),
        ],
        scratch_shapes=[pltpu.VMEM((ct + SUBLANES, W_MIX), F32), pltpu.VMEM((1, W_MIX), F32),
                        pltpu.VMEM((2 * W_MIX // LANES, ct, LANES), F32),
                        pltpu.VMEM((ct // SUBLANES, W_MIX), F32)],
        compiler_params=pltpu.CompilerParams(dimension_semantics=("parallel", "arbitrary"),
                                             vmem_limit_bytes=VMEM_LIMIT),
        name="rglru_prompt",
    )(p3, *[_arr_of(a) for a in params])
    return (o.reshape(n_seq * seq_len, W_MIX), h1.reshape(n_seq, W_MIX),
            conv_tail[:, SUBLANES - (CONV_W - 1):, :])


def _gdn_kernel(p_ref, cw_ref, nal_ref, dtb_ref, ng_ref, o_ref, s_out_ref, conv_out_ref, ext_scr, s_scr):
    j = pl.program_id(1)
    last = pl.num_programs(1) - 1
    ct = p_ref.shape[0]

    @pl.when(j == 0)
    def _():
        s_scr[...] = jnp.zeros(s_scr.shape, F32)

    p = p_ref[...]
    raw = p[:, :D_QKV_W]
    z = p[:, D_QKV_W:D_QKV_W + W_MIX]
    b_raw = p[:, D_QKV_W + W_MIX:D_QKV_W + 2 * W_MIX]
    a_raw = p[:, D_QKV_W + 2 * W_MIX:]
    ones = _head_ones()
    qkv = _conv_tile(raw, ext_scr, cw_ref, j == 0)
    q, k, v, beta, g = _gdn_token_math(qkv, b_raw, a_raw, nal_ref[...], dtb_ref[...], ones)

    c = min(CHUNK, ct)
    ha = _HeadAlgebra(c)
    lt = (_iota((c, c), 0) >= _iota((c, c), 1)).astype(BF16)
    units = []
    for c0 in range(0, ct, c):
        sl = slice(c0, c0 + c)
        gc = _dot_const(g[sl], lt, _NN, CUMSUM_PIECES, const_left=True)
        gc_cols = jnp.sum(gc * ha.eye, axis=0, keepdims=True)
        diff = gc - gc_cols
        decay = jnp.where(ha.incl, jnp.exp(jnp.where(ha.incl, diff, 0.0)), 0.0)
        kb = k[sl] * beta[sl]
        e_gc = jnp.exp(gc)
        g_last = gc[c - 1:c, :]
        units.append(dict(decay=decay, kb=kb, q=q[sl], k=k[sl], vb=v[sl] * beta[sl], kbe=kb * e_gc,
                          k_out=k[sl] * jnp.exp(g_last - gc), q_in=q[sl] * e_gc, e_last=jnp.exp(g_last)))
    all_units = units
    state = s_scr[...]
    o_rows = []
    for w0 in range(0, len(all_units), WAVE):
        units = all_units[w0:w0 + WAVE]
        for u in units:
            kq = ha.nt(jnp.concatenate([u["kb"], u["q"]], axis=0), ha.bd(u["k"]))
            u["a_mat"] = jnp.where(ha.strict, kq[:c] * u["decay"], 0.0)
            u["qk"] = kq[c:] * u["decay"]
        for u, t_inv in zip(units, ha.inv_unit_lower_many([u["a_mat"] for u in units])):
            u["t_inv"] = t_inv
        for u in units:
            u["u"] = ha.nn(u["t_inv"], ha.bd(u["vb"]), BF16)
            u["w"] = ha.nn(u["t_inv"], ha.bd(u["kbe"]), BF16)
        for u in units:
            u["g_mat"] = ha.diag_bd(u["e_last"]) - ha.tn_bd(u["k_out"], u["w"])
            u["h_mat"] = ha.tn_bd(u["k_out"], u["u"])
            u["q_hat"] = u["q_in"] - ha.nn(u["qk"], ha.bd(u["w"]))
            u["o1"] = ha.nn(u["qk"], ha.bd(u["u"]))
            zz = lax.dot_general(jnp.concatenate([u["q_hat"], u["g_mat"]], axis=0).astype(BF16),
                                 state.astype(BF16), _NN, preferred_element_type=F32)
            o_rows.append(zz[:c] + u["o1"])
            state = zz[c:] + u["h_mat"]
    o = o_rows[0] if len(o_rows) == 1 else jnp.concatenate(o_rows, axis=0)
    s_scr[...] = state
    o_ref[...] = _head_rms_finish(o, z, ones, ng_ref[...])

    @pl.when(j == last)
    def _():
        for h in range(N_HEADS):
            hs = slice(h * HEAD_DIM, (h + 1) * HEAD_DIM)
            s_out_ref[h] = state[hs, hs]
        conv_out_ref[...] = raw[ct - SUBLANES:ct, :]


def _gdn_params(lp):
    return [lp["gdn_conv_w"], lp["gdn_A_log"], lp["gdn_dt_bias"], lp["gdn_norm_g"]]


def _gdn_call(p_d, n_seq, seq_len, lp):
    ct = min(ROW_TILE, seq_len)
    p3 = p_d.reshape(n_seq, seq_len, D_PACK_W)
    params = _gdn_params(lp)
    o, s1, conv_tail = pl.pallas_call(
        _gdn_kernel,
        grid=(n_seq, seq_len // ct),
        in_specs=[pl.BlockSpec((None, ct, D_PACK_W), lambda b, j: (b, j, 0))]
        + [_spec_of(a) for a in params],
        out_specs=[
            pl.BlockSpec((None, ct, W_MIX), lambda b, j: (b, j, 0)),
            pl.BlockSpec((None, N_HEADS, HEAD_DIM, HEAD_DIM), lambda b, j: (b, 0, 0, 0)),
            pl.BlockSpec((None, SUBLANES, D_QKV_W), lambda b, j: (b, 0, 0)),
        ],
        out_shape=[
            jax.ShapeDtypeStruct((n_seq, seq_len, W_MIX), F32),
            jax.ShapeDtypeStruct((n_seq, N_HEADS, HEAD_DIM, HEAD_DIM), F32),
            jax.ShapeDtypeStruct((n_seq, SUBLANES, D_QKV_W), F32),
        ],
        scratch_shapes=[pltpu.VMEM((ct + SUBLANES, D_QKV_W), F32), pltpu.VMEM((W_MIX, W_MIX), F32)],
        compiler_params=pltpu.CompilerParams(dimension_semantics=("parallel", "arbitrary"),
                                             vmem_limit_bytes=VMEM_LIMIT),
        name="gdn_prompt",
    )(p3, *[_arr_of(a) for a in params])
    return o.reshape(n_seq * seq_len, W_MIX), s1, conv_tail[:, SUBLANES - (CONV_W - 1):, :]


def _decode_pre_kernel(pa_ref, pb_ref, pc_ref, pd_ref, shift_ref, h0_ref, lconv_ref, gconv_ref, cos_ref, sin_ref,
                       mu_ref, w0_ref, w2_ref, a0_ref, a2_ref, kk_ref, ka_ref,
                       lcw_ref, lcb_ref, lgw_ref, lgb_ref, lsp_ref, gcw_ref, nal_ref, dtb_ref,
                       vt_ref, vn_ref, oc_ref, h1_ref, lconv1_ref, gconv1_ref):
    ones = _head_ones()
    pa_full = pa_ref[...]
    pa = pa_full[:, :A_SHIFT_W]
    pm = pa + (shift_ref[...] - pa) * mu_ref[...]
    r, k, v, ld, av, bv = _rwkv_token_math(pm, w0_ref[...], w2_ref[...], a0_ref[...], a2_ref[...],
                                           kk_ref[...], ka_ref[...], ones)
    vecs = [r, jnp.exp(ld), k, v, av, bv]
    plain = [r, k, v, pa_full[:, A_SHIFT_W:]]
    pb = pb_ref[...]
    cos, sin = cos_ref[...], sin_ref[...]
    vecs += [_rotary(pb[:, 0:W_MIX], cos, sin), _rotary(pb[:, W_MIX:2 * W_MIX], cos, sin) * (HEAD_DIM ** -0.5),
             pb[:, 2 * W_MIX:3 * W_MIX]]
    plain.append(pb[:, 3 * W_MIX:])
    pc = pc_ref[...]
    xr = pc[:, :W_MIX]
    taps = [lconv_ref[i] for i in range(CONV_W - 1)] + [xr]
    xc = taps[0] * lcw_ref[0:1, :]
    for i in range(1, CONV_W):
        xc = xc + taps[i] * lcw_ref[i:i + 1, :]
    xc = xc + lcb_ref[...]
    a, b = _lru_token_math(xc, lgw_ref[...], lgb_ref[...], lsp_ref[...])
    hcur = a * h0_ref[...] + b
    oc_ref[...] = hcur * _silu(pc[:, W_MIX:])
    h1_ref[...] = hcur
    for i in range(CONV_W - 1):
        lconv1_ref[i] = taps[i + 1]
    pd = pd_ref[...]
    raw = pd[:, :D_QKV_W]
    gtaps = [gconv_ref[i] for i in range(CONV_W - 1)] + [raw]
    qkv = gtaps[0] * gcw_ref[0:1, :]
    for i in range(1, CONV_W):
        qkv = qkv + gtaps[i] * gcw_ref[i:i + 1, :]
    q, kg, vg, beta, g = _gdn_token_math(qkv, pd[:, D_QKV_W + W_MIX:D_QKV_W + 2 * W_MIX],
                                         pd[:, D_QKV_W + 2 * W_MIX:], nal_ref[...], dtb_ref[...], ones)
    for i in range(CONV_W - 1):
        gconv1_ref[i] = gtaps[i + 1]
    vecs += [q, kg, vg, beta, g]
    plain.append(pd[:, D_QKV_W:D_QKV_W + W_MIX])
    assert len(vecs) == N_VEC_T and len(plain) == N_VEC_PLAIN
    for i, vec in enumerate(vecs):
        vt_ref[i] = vec.T
    for i, vec in enumerate(plain):
        vn_ref[i] = vec


def _decode_state_kernel(vt_ref, wkv_ref, ret_ref, gdn_ref, gam_ref, wkv1_ref, ret1_ref, gdn1_ref, o_ref):
    v_r, v_w, v_k, v_v, v_a, v_b, r_q, r_k, r_v, g_q, g_k, g_v, g_beta, g_g = range(N_VEC_T)
    hd = HEAD_DIM
    n = vt_ref.shape[-1]
    row = lambda idx, i: vt_ref[idx, pl.ds(i, 1), :]
    rows_of = lambda i: pl.ds(pl.multiple_of(i * hd, hd), hd)
    gamma = gam_ref[...]
    beta = vt_ref[g_beta, 0:1, :]
    eg = jnp.exp(vt_ref[g_g, 0:1, :])

    def first_pass(i, carry):
        acc_ret, acc_w, acc_q = carry
        rows = rows_of(i)
        s = wkv_ref[rows, :]
        sa = jnp.sum(s * vt_ref[v_a], axis=0, keepdims=True)
        s = s * vt_ref[v_w] + sa * vt_ref[v_b] + row(v_v, i) * vt_ref[v_k]
        wkv1_ref[rows, :] = s
        o_ref[0, pl.ds(i, 1), :] = jnp.sum(s * vt_ref[v_r], axis=0, keepdims=True)
        s = ret_ref[rows, :] * gamma + row(r_k, i) * vt_ref[r_v]
        ret1_ref[rows, :] = s
        acc_ret = acc_ret + row(r_q, i) * s
        s = gdn_ref[rows, :]
        return acc_ret, acc_w + row(g_k, i) * s, acc_q + row(g_q, i) * s

    zeros = jnp.zeros((hd, n), F32)
    acc_ret, acc_w, acc_q = lax.fori_loop(0, hd, first_pass, (zeros, zeros, zeros))
    o_ref[1] = acc_ret
    v_new = vt_ref[g_v] * beta - acc_w * (beta * eg)
    qk = jnp.sum(vt_ref[g_q] * vt_ref[g_k], axis=0, keepdims=True)
    o_ref[2] = acc_q * eg + qk * v_new

    def second_pass(i, carry):
        rows = rows_of(i)
        gdn1_ref[rows, :] = gdn_ref[rows, :] * eg + row(g_k, i) * v_new
        return carry

    lax.fori_loop(0, hd, second_pass, 0)


def _decode_finish_kernel(ot_ref, vn_ref, rk_ref, lng_ref, lnb_ref, ng_ref, oa_ref, ob_ref, od_ref):
    ones = _head_ones()
    r, k, v, z_a, z_b, z_d = (vn_ref[i] for i in range(N_VEC_PLAIN))
    oa_ref[...] = _rwkv_finish(ot_ref[0].T, r, k, v, z_a, rk_ref[...], lng_ref[...], lnb_ref[...], ones)
    ob_ref[...] = _head_rms_finish(ot_ref[1].T, z_b, ones)
    od_ref[...] = _head_rms_finish(ot_ref[2].T, z_d, ones, ng_ref[...])


def _batch_minor(state):
    n_layers, n = state.shape[:2]
    return jnp.transpose(state, (0, 2, 3, 4, 1)).reshape(n_layers, -1, n)


def _batch_major(flat_state):
    n = flat_state.shape[-1]
    return jnp.transpose(flat_state.reshape(N_HEADS, HEAD_DIM, HEAD_DIM, n), (3, 0, 1, 2))


def _decode_layer(l, p_a, p_b, p_c, p_d, carried, cos_t, sin_t, lp):
    n = p_a.shape[0]
    assert n % LANES == 0, "the decode state kernel keeps the batch on lanes"
    flat = HEAD_DIM * HEAD_DIM
    taps = CONV_W - 1
    rwkv_params = [_row(lp["rwkv_mu"]), _row(lp["rwkv_w0"]), lp["rwkv_w2"], _row(lp["rwkv_a0"]), lp["rwkv_a2"],
                   _row(lp["rwkv_k_k"]), _row(lp["rwkv_k_a"])]
    gdn_params = _gdn_params(lp)
    full = lambda a: pl.BlockSpec(a.shape, lambda i: (0,) * a.ndim)
    layer_blk = lambda a: pl.BlockSpec((None,) + a.shape[1:], lambda i, nd=a.ndim: (l,) + (0,) * (nd - 1))
    projs = [p_a, p_b, p_c, p_d]
    layered = [carried["shift"], carried["lru_h"], carried["lru_conv"], carried["gdn_conv"]]
    consts = [cos_t, sin_t, *rwkv_params, *_lru_params(lp), *gdn_params[:3]]
    out_shapes = [
        jax.ShapeDtypeStruct((N_VEC_T, W_MIX, n), F32),
        jax.ShapeDtypeStruct((N_VEC_PLAIN, n, W_MIX), F32),
        jax.ShapeDtypeStruct((n, W_MIX), F32),
        jax.ShapeDtypeStruct((n, W_MIX), F32),
        jax.ShapeDtypeStruct((taps, n, W_MIX), F32),
        jax.ShapeDtypeStruct((taps, n, D_QKV_W), F32),
    ]
    vec_t, vec_n, o_c, lru_h1, lru_conv1, gdn_conv1 = pl.pallas_call(
        _decode_pre_kernel,
        grid=(1,),
        in_specs=[full(a) for a in projs] + [layer_blk(a) for a in layered] + [_spec_of(a) for a in consts],
        out_specs=[pl.BlockSpec(s.shape, lambda i, nd=len(s.shape): (0,) * nd) for s in out_shapes],
        out_shape=out_shapes,
        compiler_params=pltpu.CompilerParams(dimension_semantics=("arbitrary",), vmem_limit_bytes=VMEM_LIMIT),
        name="decode_tokens",
    )(*projs, *layered, *[_arr_of(a) for a in consts])

    gam = jnp.broadcast_to((1.0 - 2.0 ** (-5.0 - jnp.arange(N_HEADS, dtype=F32)))[:, None, None], (N_HEADS, 1, n))
    state_in = pl.BlockSpec((None, flat, n), lambda h: (l, h, 0))
    state_out = pl.BlockSpec((flat, n), lambda h: (h, 0))
    wkv1, ret1, gdn1, o_t = pl.pallas_call(
        _decode_state_kernel,
        grid=(N_HEADS,),
        in_specs=[pl.BlockSpec((N_VEC_T, HEAD_DIM, n), lambda h: (0, h, 0)), state_in, state_in, state_in,
                  pl.BlockSpec((None, 1, n), lambda h: (h, 0, 0))],
        out_specs=[state_out, state_out, state_out, pl.BlockSpec((3, HEAD_DIM, n), lambda h: (0, h, 0))],
        out_shape=[jax.ShapeDtypeStruct((N_HEADS * flat, n), F32)] * 3 + [jax.ShapeDtypeStruct((3, W_MIX, n), F32)],
        compiler_params=pltpu.CompilerParams(dimension_semantics=("parallel",), vmem_limit_bytes=VMEM_LIMIT),
        name="decode_states",
    )(vec_t, carried["wkv"], carried["ret"], carried["gdn"], gam)

    finish_ins = [o_t, vec_n, _row(lp["rwkv_r_k"]), _row(lp["rwkv_ln_g"]), _row(lp["rwkv_ln_b"]), gdn_params[3]]
    o_a, o_b, o_d = pl.pallas_call(
        _decode_finish_kernel,
        grid=(1,),
        in_specs=[_spec_of(a) for a in finish_ins],
        out_specs=[pl.BlockSpec((n, W_MIX), lambda i: (0, 0))] * 3,
        out_shape=[jax.ShapeDtypeStruct((n, W_MIX), F32)] * 3,
        compiler_params=pltpu.CompilerParams(dimension_semantics=("arbitrary",), vmem_limit_bytes=VMEM_LIMIT),
        name="decode_finish",
    )(*[_arr_of(a) for a in finish_ins])
    new_states = (_batch_major(wkv1), p_a[:, :A_SHIFT_W], _batch_major(ret1), lru_h1,
                  jnp.transpose(lru_conv1, (1, 0, 2)), _batch_major(gdn1), jnp.transpose(gdn_conv1, (1, 0, 2)))
    return (o_a, o_b, o_c, o_d), new_states


def _prompt_layer(p_a, p_b, p_c, p_d, n_seq, seq_len, cos_t, sin_t, lp):
    o_a, wkv1, shift1 = _rwkv_call(p_a, n_seq, seq_len, lp)
    o_b, ret1 = _ret_call(p_b, n_seq, seq_len, cos_t, sin_t)
    o_c, lru_h1, lru_conv1 = _lru_call(p_c, n_seq, seq_len, lp)
    o_d, gdn1, gdn_conv1 = _gdn_call(p_d, n_seq, seq_len, lp)
    return (o_a, o_b, o_c, o_d), (wkv1, shift1, ret1, lru_h1, lru_conv1, gdn1, gdn_conv1)


def _run_group(x, mods, pos, carried, layers, final_g):
    n_seq, seq_len, d = x.shape
    x2 = x.reshape(n_seq * seq_len, d)
    cos_t, sin_t = _rope_tables(pos)
    new = []
    n_layers = len(layers)
    for l, lp in enumerate(layers):
        p_a, p_b, p_c, p_d = _inproj_call(x2, mods, lp["norm_g"], l, lp["w_t"], lp["w_ba"], seq_len)
        if carried is None:
            branches, st = _prompt_layer(p_a, p_b, p_c, p_d, n_seq, seq_len, cos_t, sin_t, lp)
        else:
            branches, st = _decode_layer(l, p_a, p_b, p_c, p_d, carried, cos_t, sin_t, lp)
        new.append(st)
        x2 = _outproj_call(x2, mods, lp["norm_g"], branches, l, lp["w_t"], lp["w_up_bf16"],
                           lp["w_out_bf16"], final_g, seq_len, final=(l == n_layers - 1))
    stacked = tuple(jnp.stack([s[i] for s in new], axis=0) for i in range(7))
    return x2.reshape(n_seq, seq_len, d), stacked


def kernel(x_prompt, x_sample, c_prompt, c_sample, state_rwkv_wkv, state_rwkv_shift, state_ret, state_lru_h, state_lru_conv, state_gdn, state_gdn_conv, ada_w, ada_b, norm_g, w_in, rwkv_mu, rwkv_w0, rwkv_w2, rwkv_a0, rwkv_a2, rwkv_k_k, rwkv_k_a, rwkv_r_k, rwkv_ln_g, rwkv_ln_b, lru_conv_w, lru_conv_b, lru_gate_w, lru_gate_b, lru_lambda, gdn_conv_w, gdn_A_log, gdn_dt_bias, gdn_norm_g, w_up, w_out, final_g):
    n_layers = ada_w.shape[0]
    n_prompt, seq_len, _ = x_prompt.shape
    n_sample, dec_len, _ = x_sample.shape
    assert dec_len == 1, "the decode path handles one token per sequence"
    w_in_t = jnp.swapaxes(w_in, 1, 2).astype(BF16)
    off_ba = A_W + B_W + C_W + D_QKV_W
    rows = lambda a: a.reshape(n_layers, 1, -1)
    stacked = dict(
        norm_g=rows(norm_g), w_up_bf16=w_up.astype(BF16), w_out_bf16=w_out.astype(BF16),
        w_ba=jnp.repeat(w_in_t[:, off_ba:off_ba + 2 * N_HEADS], HEAD_DIM, axis=1),
        rwkv_mu=rows(rwkv_mu), rwkv_w0=rows(rwkv_w0), rwkv_w2=rwkv_w2, rwkv_a0=rows(rwkv_a0), rwkv_a2=rwkv_a2,
        rwkv_k_k=rows(rwkv_k_k), rwkv_k_a=rows(rwkv_k_a), rwkv_r_k=rows(rwkv_r_k), rwkv_ln_g=rows(rwkv_ln_g),
        rwkv_ln_b=rows(rwkv_ln_b), lru_conv_w=lru_conv_w, lru_conv_b=rows(lru_conv_b),
        lru_gate_w=_block_diag_gates(lru_gate_w), lru_gate_b=rows(lru_gate_b), lru_lambda=rows(lru_lambda),
        gdn_conv_w=gdn_conv_w, gdn_A_log=rows(jnp.repeat(gdn_A_log, HEAD_DIM, axis=1)),
        gdn_dt_bias=rows(jnp.repeat(gdn_dt_bias, HEAD_DIM, axis=1)),
        gdn_norm_g=rows(jnp.tile(gdn_norm_g, (1, N_HEADS))))
    layers = [dict({name: _LayerParam(arr, l) for name, arr in stacked.items()}, w_t=w_in_t)
              for l in range(n_layers)]
    mods_p, mods_s = _ada_call(c_prompt, c_sample, ada_w, ada_b)

    y_prompt, new_p = _run_group(x_prompt, mods_p, jnp.arange(seq_len, dtype=jnp.int32), None, layers, final_g)
    carried = dict(wkv=_batch_minor(state_rwkv_wkv), ret=_batch_minor(state_ret), gdn=_batch_minor(state_gdn),
                   shift=state_rwkv_shift, lru_h=state_lru_h,
                   lru_conv=jnp.transpose(state_lru_conv, (0, 2, 1, 3)),
                   gdn_conv=jnp.transpose(state_gdn_conv, (0, 2, 1, 3)))
    pos_s = PAST_LEN + jnp.arange(dec_len, dtype=jnp.int32)
    y_sample, new_s = _run_group(x_sample, mods_s, pos_s, carried, layers, final_g)
    return (y_prompt, y_sample) + new_p + new_s
```

```python
import functools
import math

import jax
import jax.numpy as jnp
from jax import lax
from jax.experimental import pallas as pl
from jax.experimental.pallas import tpu as pltpu

F32 = jnp.float32
BF16 = jnp.bfloat16
HI = lax.Precision.HIGHEST

N_HEADS = 4
HEAD_DIM = 64
W_MIX = N_HEADS * HEAD_DIM
LORA = 64
CONV_W = 4
N_BRANCH = 4
LRU_C = 8.0
ROPE_BASE = 10000.0
EPS = 1e-6
RWKV_GN_EPS = 64e-5
PAST_LEN = 16384
A_SHIFT_W = 3 * W_MIX + 2 * LORA
A_W = A_SHIFT_W + W_MIX
B_W = 4 * W_MIX
C_W = 2 * W_MIX
D_QKV_W = 3 * W_MIX
D_W = D_QKV_W + 2 * N_HEADS + W_MIX
D_PACK_W = D_QKV_W + 3 * W_MIX

SUBLANES = 8
LANES = 128
VMEM_LIMIT = 56 * 1024 * 1024

CHUNK = 64
RET_CHUNK = 128
INV_BLOCK = 16
WAVE = 16
ROW_TILE = 1024
LRU_TILE = 512
PROJ_TILE = 512
OUT_TILE = 1024
N_VEC_T = 14
N_VEC_PLAIN = 6


def _mm(a, b, prec=HI):
    return lax.dot_general(a, b, (((1,), (0,)), ((), ())), precision=prec, preferred_element_type=F32)


def _mm_nt(a, b, prec=HI):
    return lax.dot_general(a, b, (((1,), (1,)), ((), ())), precision=prec, preferred_element_type=F32)


def _mm_tn(a, b, prec=HI):
    return lax.dot_general(a, b, (((0,), (0,)), ((), ())), precision=prec, preferred_element_type=F32)


_NN = (((1,), (0,)), ((), ()))
_NT = (((1,), (1,)), ((), ()))
_TN = (((0,), (0,)), ((), ()))

P_INV = 1
P_STATE = 1
P_MISC = 1
HEAD_SUM_PIECES = 1
CUMSUM_PIECES = 2


class _Split:
    def __init__(self, x, passes):
        self.hi = x.astype(BF16)
        self.lo = (x - self.hi.astype(F32)).astype(BF16) if passes > 1 else None


def _dotp(a, b, dims=_NN, passes=1):
    a = a if isinstance(a, _Split) else _Split(a, passes)
    b = b if isinstance(b, _Split) else _Split(b, passes)
    d = lambda x, y: lax.dot_general(x, y, dims, preferred_element_type=F32)
    out = d(a.hi, b.hi)
    if passes > 1:
        out = out + (d(a.hi, b.lo) + d(a.lo, b.hi))
    return out


def _iota(shape, dim):
    return lax.broadcasted_iota(jnp.int32, shape, dim)


def _sigmoid(x):
    return 0.5 * jnp.tanh(0.5 * x) + 0.5


def _silu(x):
    return x * _sigmoid(x)


def _softplus(x):
    return jnp.maximum(x, 0.0) + jnp.log(1.0 + jnp.exp(-jnp.abs(x)))


def _pieces(x, n):
    out = []
    for i in range(n):
        p = x.astype(BF16)
        out.append(p)
        if i + 1 < n:
            x = x - p.astype(F32)
    return out


def _dot_const(x, const, dims=_NN, n=2, const_left=False):
    out = None
    for p in _pieces(x, n):
        t = lax.dot_general(*((const, p) if const_left else (p, const)), dims, preferred_element_type=F32)
        out = t if out is None else out + t
    return out


def _head_ones():
    return (_iota((W_MIX, W_MIX), 0) // HEAD_DIM == _iota((W_MIX, W_MIX), 1) // HEAD_DIM).astype(BF16)


def _head_sum(x, ones, signed=False):
    return _dot_const(x, ones, n=HEAD_SUM_PIECES + (1 if signed else 0))


def _rms(x):
    return x * lax.rsqrt(jnp.mean(x * x, axis=-1, keepdims=True) + EPS)


def _inv_unit_lower(a):
    return _inv_unit_lower_many([a])[0]


def _inv_unit_lower_many(mats):
    n = mats[0].shape[0]
    ri, ci = _iota((n, n), 0), _iota((n, n), 1)
    eye = (ri == ci).astype(F32)
    diag_blk = (ri // INV_BLOCK) == (ci // INV_BLOCK)
    mm = lambda x, y: _dotp(x, y, _NN, P_INV)
    sp = lambda x: _Split(x, P_INV)
    d = [jnp.where(diag_blk, a, 0.0) for a in mats]
    nb = [a - di for a, di in zip(mats, d)]
    td = [eye - di for di in d]
    p = d
    for _ in range(int(math.log2(INV_BLOCK)) - 1):
        ps = [sp(pi) for pi in p]
        p = [mm(pi, pi) for pi in ps]
        td = [mm(ti, eye + pi) for ti, pi in zip(td, p)]
    tds = [sp(ti) for ti in td]
    x = [mm(ti, ni) for ti, ni in zip(tds, nb)]
    t = [eye - xi for xi in x]
    p = x
    for _ in range(int(math.log2(n // INV_BLOCK)) - 1):
        ps = [sp(pi) for pi in p]
        p = [mm(pi, pi) for pi in ps]
        t = [mm(ti, eye + pi) for ti, pi in zip(t, p)]
    return [mm(ti, tdi) for ti, tdi in zip(t, tds)]


class _HeadAlgebra:
    def __init__(self, c):
        assert c == HEAD_DIM, "side-by-side head products need CHUNK == HEAD_DIM"
        w = W_MIX
        row, lane = _iota((c, w), 0), _iota((c, w), 1)
        col = lane % HEAD_DIM
        tile_lane = _iota((c, LANES), 1)
        self.tile_head = [tile_lane // HEAD_DIM == h for h in range(LANES // HEAD_DIM)]
        self.eye = (row == col).astype(F32)
        self.strict = row > col
        self.incl = row >= col
        self.inv_blk = (row // INV_BLOCK) == (col // INV_BLOCK)
        r2, c2 = _iota((w, w), 0), _iota((w, w), 1)
        self.eye_full = r2 == c2
        self.same_head = (r2 // HEAD_DIM) == (c2 // HEAD_DIM)

    def bd(self, y):
        yb = y.astype(BF16)
        zero = jnp.zeros((yb.shape[0], LANES), BF16)
        blocks = []
        for t in range(W_MIX // LANES):
            tile = yb[:, t * LANES:(t + 1) * LANES]
            for m in self.tile_head:
                kept = jnp.where(m, tile, zero)
                blocks.append(jnp.concatenate([kept if s == t else zero for s in range(W_MIX // LANES)], axis=1))
        return jnp.concatenate(blocks, axis=0)

    def nn(self, x, bd_y, out=F32):
        return lax.dot_general(x.astype(BF16), bd_y, _NN, preferred_element_type=F32).astype(out)

    def nt(self, x, bd_y):
        return lax.dot_general(x.astype(BF16), bd_y, _NT, preferred_element_type=F32)

    def tn_bd(self, x, y):
        full = lax.dot_general(x.astype(BF16), y.astype(BF16), _TN, preferred_element_type=F32)
        return jnp.where(self.same_head, full, 0.0)

    def diag_bd(self, row_vec):
        return jnp.where(self.eye_full, row_vec, 0.0)

    def plus_eye(self, bd_p):
        return jnp.where(self.eye_full, jnp.ones_like(bd_p), bd_p)

    def inv_unit_lower_many(self, mats):
        mats = [a.astype(BF16) for a in mats]
        zero = jnp.zeros_like(mats[0])
        eye = self.eye.astype(BF16)
        d = [jnp.where(self.inv_blk, a, zero) for a in mats]
        nb = [jnp.where(self.inv_blk, zero, a) for a in mats]
        td = [eye - di for di in d]
        p = d
        bdp = [self.bd(pi) for pi in p]
        for _ in range(int(math.log2(INV_BLOCK)) - 1):
            p = [self.nn(pi, bi, BF16) for pi, bi in zip(p, bdp)]
            bdp = [self.bd(pi) for pi in p]
            td = [self.nn(ti, self.plus_eye(bi), BF16) for ti, bi in zip(td, bdp)]
        bd_td = [self.bd(ti) for ti in td]
        x = [self.nn(ti, self.bd(ni), BF16) for ti, ni in zip(td, nb)]
        t = [eye - xi for xi in x]
        p = x
        bdp = [self.bd(pi) for pi in p]
        for _ in range(int(math.log2(HEAD_DIM // INV_BLOCK)) - 1):
            p = [self.nn(pi, bi, BF16) for pi, bi in zip(p, bdp)]
            bdp = [self.bd(pi) for pi in p]
            t = [self.nn(ti, self.plus_eye(bi), BF16) for ti, bi in zip(t, bdp)]
        return [self.nn(ti, bi, BF16) for ti, bi in zip(t, bd_td)]


def _ada_kernel(cp_ref, cs_ref, w_ref, b_ref, op_ref, os_ref):
    w = _Split(w_ref[...], 3)
    op_ref[...] = _dotp(_silu(cp_ref[...]), w, _NN, 3) + b_ref[...]
    os_ref[...] = _dotp(_silu(cs_ref[...]), w, _NN, 3) + b_ref[...]


def _ada_call(c_prompt, c_sample, ada_w, ada_b):
    n_layers, d, d3 = ada_w.shape
    n_p, n_s = c_prompt.shape[0], c_sample.shape[0]
    return pl.pallas_call(
        _ada_kernel,
        grid=(n_layers, d3 // d),
        in_specs=[
            pl.BlockSpec((n_p, d), lambda l, j: (0, 0)),
            pl.BlockSpec((n_s, d), lambda l, j: (0, 0)),
            pl.BlockSpec((None, d, d), lambda l, j: (l, 0, j)),
            pl.BlockSpec((None, 1, d), lambda l, j: (l, 0, j)),
        ],
        out_specs=[pl.BlockSpec((None, n_p, d), lambda l, j: (l, 0, j)),
                   pl.BlockSpec((None, n_s, d), lambda l, j: (l, 0, j))],
        out_shape=[jax.ShapeDtypeStruct((n_layers, n_p, d3), F32), jax.ShapeDtypeStruct((n_layers, n_s, d3), F32)],
        compiler_params=pltpu.CompilerParams(dimension_semantics=("arbitrary", "arbitrary"),
                                             vmem_limit_bytes=VMEM_LIMIT),
        name="ada_mod",
    )(c_prompt, c_sample, ada_w, ada_b.reshape(n_layers, 1, d3))


def _modulated_norm(x, g, scale, shift):
    return _rms(x) * g * (1.0 + scale) + shift


def _inproj_kernel(x_ref, sc_ref, sh_ref, g_ref, w_ref, wz_ref, wba_ref, wqk_ref, oa_ref, ob_ref, oc_ref, od_ref):
    h = _modulated_norm(x_ref[...], g_ref[...], sc_ref[...], sh_ref[...]).astype(BF16)
    proj = lambda w: lax.dot_general(h, w, _NT, preferred_element_type=F32)
    oa_ref[...] = proj(w_ref[0, 0:A_W, :])
    ob_ref[:, :2 * W_MIX] = proj(wqk_ref[...])
    ob_ref[:, 2 * W_MIX:] = proj(w_ref[0, A_W + 2 * W_MIX:A_W + B_W, :])
    oc_ref[...] = proj(w_ref[0, A_W + B_W:A_W + B_W + C_W, :])
    lo = A_W + B_W + C_W
    od_ref[:, :D_QKV_W] = proj(w_ref[0, lo:lo + D_QKV_W, :])
    od_ref[:, D_QKV_W:D_QKV_W + W_MIX] = proj(wz_ref[0])
    od_ref[:, D_QKV_W + W_MIX:] = proj(wba_ref[...])


def _weight_rows(layer, row0, n_rows, d):
    return pl.BlockSpec((pl.Element(1), pl.Element(n_rows), pl.Element(d)), lambda i: (layer, row0, 0))


MOD_SHIFT, MOD_SCALE, MOD_GATE = 0, 1, 2


def _mod_specs(mods, layer, parts, d, tm, seq_len):
    if seq_len == 1:
        return [mods] * len(parts), [pl.BlockSpec((None, tm, d), lambda i, c=c: (layer, i, c)) for c in parts]
    per_seq = seq_len // tm
    by_seq = mods.reshape(mods.shape[0], mods.shape[1], 1, mods.shape[2])
    return ([by_seq] * len(parts),
            [pl.BlockSpec((None, None, 1, d), lambda i, c=c: (layer, i // per_seq, 0, c)) for c in parts])


class _LayerParam:
    def __init__(self, arr, layer):
        self.arr, self.layer = arr, layer

    def spec(self):
        return pl.BlockSpec((None,) + self.arr.shape[1:], lambda *_: (self.layer,) + (0,) * (self.arr.ndim - 1))


def _spec_of(a):
    if isinstance(a, _LayerParam):
        return a.spec()
    return pl.BlockSpec(a.shape, lambda *_: (0,) * a.ndim)


def _arr_of(a):
    return a.arr if isinstance(a, _LayerParam) else a


def _inproj_call(x2, mods_all, g, l, w_t, w_ba, w_qk, seq_len):
    m, d = x2.shape
    tm = min(PROJ_TILE, m, seq_len) if seq_len > 1 else m
    widths = (A_W, B_W, C_W, D_PACK_W)
    mods, mod_specs = _mod_specs(mods_all, l, (MOD_SCALE, MOD_SHIFT), d, tm, seq_len)
    off_ba = A_W + B_W + C_W + D_QKV_W
    off_z = off_ba + 2 * N_HEADS
    return pl.pallas_call(
        _inproj_kernel,
        grid=(m // tm,),
        in_specs=[pl.BlockSpec((tm, d), lambda i: (i, 0))] + mod_specs + [
            _spec_of(g),
            _weight_rows(l, 0, off_ba, d),
            _weight_rows(l, off_z, W_MIX, d),
            _spec_of(w_ba),
            _spec_of(w_qk),
        ],
        out_specs=[pl.BlockSpec((tm, wd), lambda i: (i, 0)) for wd in widths],
        out_shape=[jax.ShapeDtypeStruct((m, wd), F32) for wd in widths],
        compiler_params=pltpu.CompilerParams(dimension_semantics=("parallel",), vmem_limit_bytes=VMEM_LIMIT),
        name="in_proj",
    )(x2, *mods, _arr_of(g), w_t, w_t, _arr_of(w_ba), _arr_of(w_qk))


def _outproj_kernel(x_ref, sc_ref, sh_ref, gt_ref, g_ref, ba_ref, bb_ref, bc_ref, bd_ref,
                    wg_ref, wup_ref, wout_ref, fg_ref, o_ref, *, final):
    x = x_ref[...]
    d = x.shape[-1]
    h = _modulated_norm(x, g_ref[...], sc_ref[...], sh_ref[...]).astype(BF16)
    merged = jnp.zeros(x.shape, F32)
    for n, br_ref in enumerate((ba_ref, bb_ref, bc_ref, bd_ref)):
        gl = lax.dot_general(h, wg_ref[0, n * d:(n + 1) * d, :], _NT, preferred_element_type=F32)
        up = jnp.dot(br_ref[...].astype(BF16), wup_ref[n], preferred_element_type=F32)
        merged = merged + _sigmoid(gl) * up
    out = jnp.dot(merged.astype(BF16), wout_ref[...], preferred_element_type=F32)
    xn = x + gt_ref[...] * out
    if final:
        xn = _rms(xn) * fg_ref[...]
    o_ref[...] = xn


def _outproj_call(x2, mods_all, g, branches, l, w_t, wup, wout, final_g, seq_len, final):
    m, d = x2.shape
    tm = min(OUT_TILE, m, seq_len) if seq_len > 1 else m
    mods, mod_specs = _mod_specs(mods_all, l, (MOD_SCALE, MOD_SHIFT, MOD_GATE), d, tm, seq_len)
    full = lambda a: pl.BlockSpec(a.shape, lambda i: (0,) * a.ndim)
    off_g = A_W + B_W + C_W + D_W
    gate_rows = _weight_rows(l, off_g, N_BRANCH * d, d)
    return pl.pallas_call(
        functools.partial(_outproj_kernel, final=final),
        grid=(m // tm,),
        in_specs=[pl.BlockSpec((tm, d), lambda i: (i, 0))] + mod_specs + [_spec_of(g)]
        + [pl.BlockSpec((tm, W_MIX), lambda i: (i, 0)) for _ in branches]
        + [gate_rows, _spec_of(wup), _spec_of(wout), pl.BlockSpec((1, d), lambda i: (0, 0))],
        out_specs=pl.BlockSpec((tm, d), lambda i: (i, 0)),
        out_shape=jax.ShapeDtypeStruct((m, d), F32),
        compiler_params=pltpu.CompilerParams(dimension_semantics=("parallel",), vmem_limit_bytes=VMEM_LIMIT),
        name="out_proj",
    )(x2, *mods, _arr_of(g), *branches, w_t, _arr_of(wup), _arr_of(wout), final_g.reshape(1, d))


def _rwkv_token_math(pm, w0, w2, a0, a2, k_k, k_a, ones):
    r = pm[:, 0:W_MIX]
    k = pm[:, W_MIX:2 * W_MIX]
    v = pm[:, 2 * W_MIX:3 * W_MIX]
    wd = pm[:, 3 * W_MIX:3 * W_MIX + LORA]
    ad = pm[:, 3 * W_MIX + LORA:]
    w_log = -_softplus(-(w0 + _dotp(jnp.tanh(wd), w2, _NN, P_MISC))) - 0.5
    log_decay = -jnp.exp(w_log)
    a = _sigmoid(a0 + _dotp(ad, a2, _NN, P_MISC))
    kx = k * k_k
    kk = kx * lax.rsqrt(_head_sum(kx * kx, ones) + EPS)
    k = k * (1.0 + (a - 1.0) * k_a)
    return r, k, v, log_decay, -kk, kk * a


def _rwkv_finish(o, r, k, v, z, r_k, ln_g, ln_b, ones):
    mean = _head_sum(o, ones) * (1.0 / HEAD_DIM)
    dlt = o - mean
    var = _head_sum(dlt * dlt, ones) * (1.0 / HEAD_DIM)
    on = dlt * lax.rsqrt(var + RWKV_GN_EPS) * ln_g + ln_b
    bonus = _head_sum(r * k * r_k, ones, signed=True) * v
    return (on + bonus) * _silu(z)


def _swap_halves(x):
    half = HEAD_DIM // 2
    n = x.shape[-1]
    first = (_iota(x.shape, 1) & half) == 0
    return jnp.where(first, pltpu.roll(x, n - half, axis=1), pltpu.roll(x, half, axis=1))


def _rotary(x, cos, sin):
    return x * cos + _swap_halves(x) * sin


def _lru_token_math(xc, gate_w, gate_b, lam):
    gates = _dotp(xc, gate_w, _NN, P_MISC) + gate_b
    r_gate = _sigmoid(gates[:, :W_MIX])
    i_gate = _sigmoid(gates[:, W_MIX:])
    log_a = -LRU_C * r_gate * _softplus(-lam)
    a = jnp.exp(log_a)
    b = jnp.sqrt(1.0 - jnp.exp(2.0 * log_a)) * (i_gate * xc)
    return a, b


def _gdn_token_math(qkv, b_raw, a_raw, a_log, dt_bias, ones):
    qkv = _silu(qkv)
    q = qkv[:, 0:W_MIX]
    k = qkv[:, W_MIX:2 * W_MIX]
    v = qkv[:, 2 * W_MIX:]
    q = q * lax.rsqrt(_head_sum(q * q, ones) + EPS) * (HEAD_DIM ** -0.5)
    k = k * lax.rsqrt(_head_sum(k * k, ones) + EPS)
    beta = _sigmoid(b_raw)
    g = -jnp.exp(a_log) * _softplus(a_raw + dt_bias)
    return q, k, v, beta, g


def _head_rms_finish(o, z, ones, gain=None):
    y = o * lax.rsqrt(_head_sum(o * o, ones) * (1.0 / HEAD_DIM) + EPS)
    if gain is not None:
        y = y * gain
    return y * _silu(z)


def _conv_tile(u, ext_ref, w_ref, first):
    n = u.shape[0]

    @pl.when(first)
    def _():
        ext_ref[0:SUBLANES, :] = jnp.zeros((SUBLANES, u.shape[1]), F32)

    ext_ref[SUBLANES:SUBLANES + n, :] = u
    out = None
    for j in range(CONV_W):
        term = _rows_back(u, ext_ref, CONV_W - 1 - j) * w_ref[j:j + 1, :]
        out = term if out is None else out + term
    ext_ref[0:SUBLANES, :] = u[n - SUBLANES:n, :]
    return out


def _rows_back(u, ext_ref, back):
    if back == 0:
        return u
    n, ch = u.shape
    tiles = (n // SUBLANES, SUBLANES, ch)
    pos = _iota((1, SUBLANES, 1), 1)
    earlier = ext_ref[0:n, :].reshape(tiles)
    return pltpu.roll(jnp.where(pos >= SUBLANES - back, earlier, u.reshape(tiles)), back, axis=1).reshape(n, ch)


def _rwkv_kernel(p_ref, mu_ref, w0_ref, w2_ref, a0_ref, a2_ref, kk_ref, ka_ref, rk_ref, lng_ref, lnb_ref,
                 o_ref, s_out_ref, shift_out_ref, s_scr, ext_scr):
    j = pl.program_id(1)
    last = pl.num_programs(1) - 1
    ct = p_ref.shape[0]

    @pl.when(j == 0)
    def _():
        s_scr[...] = jnp.zeros(s_scr.shape, F32)
        ext_scr[0:SUBLANES, :] = jnp.zeros((SUBLANES, A_SHIFT_W), F32)

    p = p_ref[...]
    pa = p[:, :A_SHIFT_W]
    z = p[:, A_SHIFT_W:]
    ext_scr[SUBLANES:SUBLANES + ct, :] = pa
    prev = _rows_back(pa, ext_scr, 1)
    ext_scr[0:SUBLANES, :] = pa[ct - SUBLANES:ct, :]
    pm = pa + (prev - pa) * mu_ref[...]
    ones = _head_ones()
    r, k, v, ld, av, bv = _rwkv_token_math(pm, w0_ref[...], w2_ref[...], a0_ref[...], a2_ref[...],
                                           kk_ref[...], ka_ref[...], ones)

    c = min(CHUNK, ct)
    ha = _HeadAlgebra(c)
    lt = (_iota((c, c), 0) >= _iota((c, c), 1)).astype(BF16)
    units = []
    for c0 in range(0, ct, c):
        sl = slice(c0, c0 + c)
        ldc = ld[sl]
        cum = _dot_const(ldc, lt, _NN, CUMSUM_PIECES, const_left=True)
        e_neg = jnp.exp(-cum)
        e_out = jnp.exp(cum[c - 1:c, :] - cum)
        units.append(dict(a=av[sl] * jnp.exp(cum - ldc), r=r[sl] * jnp.exp(cum), b=bv[sl] * e_neg, k=k[sl] * e_neg,
                          bo=bv[sl] * e_out, ko=k[sl] * e_out, v=v[sl], g=jnp.exp(cum[c - 1:c, :])))
    all_units = units
    state = s_scr[...]
    o_rows = []
    for w0 in range(0, len(all_units), WAVE):
        units = all_units[w0:w0 + WAVE]
        for u in units:
            lhs = jnp.concatenate([u["a"], u["r"]], axis=0)
            u["mb"] = ha.nt(lhs, ha.bd(u["b"]))
            u["mk"] = ha.nt(lhs, ha.bd(u["k"]))
            u["bd_v"] = ha.bd(u["v"])
        for u in units:
            u["m_ab"] = jnp.where(ha.strict, u["mb"][:c], 0.0)
            u["m_ak"] = jnp.where(ha.strict, u["mk"][:c], 0.0)
            u["m_rb"] = jnp.where(ha.incl, u["mb"][c:], 0.0)
            u["m_rk"] = jnp.where(ha.incl, u["mk"][c:], 0.0)
        for u, t_inv in zip(units, ha.inv_unit_lower_many([-u["m_ab"] for u in units])):
            u["t_inv"] = t_inv
        for u in units:
            u["makv"] = ha.nn(u["m_ak"], u["bd_v"], BF16)
        for u in units:
            u["a_hat"] = ha.nn(u["t_inv"], ha.bd(u["a"]), BF16)
            u["u1"] = ha.nn(u["t_inv"], ha.bd(u["makv"]), BF16)
        for u in units:
            u["r_hat"] = u["r"] + ha.nn(u["m_rb"], ha.bd(u["a_hat"]))
            u["o1"] = ha.nn(u["m_rb"], ha.bd(u["u1"])) + ha.nn(u["m_rk"], u["bd_v"])
            u["g_t"] = ha.diag_bd(u["g"]) + ha.tn_bd(u["bo"], u["a_hat"])
            u["h_t"] = ha.tn_bd(jnp.concatenate([u["bo"], u["ko"]], axis=0),
                                jnp.concatenate([u["u1"], u["v"].astype(BF16)], axis=0))
            zz = lax.dot_general(jnp.concatenate([u["r_hat"], u["g_t"]], axis=0).astype(BF16), state.astype(BF16),
                                 _NN, preferred_element_type=F32)
            o_rows.append(zz[:c] + u["o1"])
            state = zz[c:] + u["h_t"]
    o = o_rows[0] if len(o_rows) == 1 else jnp.concatenate(o_rows, axis=0)
    s_scr[...] = state
    o_ref[...] = _rwkv_finish(o, r, k, v, z, rk_ref[...], lng_ref[...], lnb_ref[...], ones)

    @pl.when(j == last)
    def _():
        eye_h = (_iota((HEAD_DIM, HEAD_DIM), 0) == _iota((HEAD_DIM, HEAD_DIM), 1)).astype(F32)
        for h in range(N_HEADS):
            hs = slice(h * HEAD_DIM, (h + 1) * HEAD_DIM)
            s_out_ref[h] = _mm_nt(eye_h, state[hs, hs])
        shift_out_ref[...] = pa[ct - 1:ct, :]


def _row(a):
    return a if isinstance(a, _LayerParam) else a.reshape(1, -1)


def _rwkv_call(p_a, n_seq, seq_len, lp):
    ct = min(ROW_TILE, seq_len)
    p3 = p_a.reshape(n_seq, seq_len, A_W)
    params = [_row(lp["rwkv_mu"]), _row(lp["rwkv_w0"]), lp["rwkv_w2"], _row(lp["rwkv_a0"]), lp["rwkv_a2"],
              _row(lp["rwkv_k_k"]), _row(lp["rwkv_k_a"]), _row(lp["rwkv_r_k"]), _row(lp["rwkv_ln_g"]),
              _row(lp["rwkv_ln_b"])]
    o, s1, shift1 = pl.pallas_call(
        _rwkv_kernel,
        grid=(n_seq, seq_len // ct),
        in_specs=[pl.BlockSpec((None, ct, A_W), lambda b, j: (b, j, 0))]
        + [_spec_of(a) for a in params],
        out_specs=[
            pl.BlockSpec((None, ct, W_MIX), lambda b, j: (b, j, 0)),
            pl.BlockSpec((None, N_HEADS, HEAD_DIM, HEAD_DIM), lambda b, j: (b, 0, 0, 0)),
            pl.BlockSpec((None, 1, A_SHIFT_W), lambda b, j: (b, 0, 0)),
        ],
        out_shape=[
            jax.ShapeDtypeStruct((n_seq, seq_len, W_MIX), F32),
            jax.ShapeDtypeStruct((n_seq, N_HEADS, HEAD_DIM, HEAD_DIM), F32),
            jax.ShapeDtypeStruct((n_seq, 1, A_SHIFT_W), F32),
        ],
        scratch_shapes=[pltpu.VMEM((W_MIX, W_MIX), F32), pltpu.VMEM((ct + SUBLANES, A_SHIFT_W), F32)],
        compiler_params=pltpu.CompilerParams(dimension_semantics=("parallel", "arbitrary"),
                                             vmem_limit_bytes=VMEM_LIMIT),
        name="rwkv7_prompt",
    )(p3, *[_arr_of(a) for a in params])
    return o.reshape(n_seq * seq_len, W_MIX), s1, shift1.reshape(n_seq, A_SHIFT_W)


def _ret_kernel(p_ref, cos_ref, sin_ref, o_ref, s_out_ref, s_scr):
    j = pl.program_id(1)
    last = pl.num_programs(1) - 1
    ct = p_ref.shape[0]

    @pl.when(j == 0)
    def _():
        s_scr[...] = jnp.zeros(s_scr.shape, F32)

    p = p_ref[...]
    cos, sin = cos_ref[...], sin_ref[...]
    v = p[:, 2 * W_MIX:3 * W_MIX]
    z = p[:, 3 * W_MIX:]
    c = min(RET_CHUNK, ct)
    ri, ci = _iota((c, c), 0), _iota((c, c), 1)
    causal = ri >= ci
    rel = jnp.where(causal, ri - ci, 0).astype(F32)
    idx = _iota((c, 1), 0).astype(F32)
    half, per_tile = HEAD_DIM // 2, LANES // HEAD_DIM
    lane = _iota((1, LANES), 1)
    qk_head = [(lane // half) % per_tile == hh for hh in range(per_tile)]
    v_head = [lane // HEAD_DIM == hh for hh in range(per_tile)]
    row_head = (_iota((LANES, 1), 0) // half) % per_tile
    n_tiles = W_MIX // LANES
    o_tiles = []
    final_states = []
    for t in range(n_tiles):
        ts = slice(t * LANES, (t + 1) * LANES)
        cos_t, sin_t = cos[:, ts], sin[:, ts]
        q_t = p[:, ts]
        k_t = p[:, W_MIX + t * LANES:W_MIX + (t + 1) * LANES]
        q_t = q_t * cos_t + pltpu.roll(q_t, LANES // 2, axis=1) * sin_t
        k_t = (k_t * cos_t + pltpu.roll(k_t, LANES // 2, axis=1) * sin_t) * (HEAD_DIM ** -0.5)
        v_t = v[:, ts]
        heads = []
        for hh in range(per_tile):
            lg = math.log(1.0 - 2.0 ** (-5.0 - (t * per_tile + hh)))
            heads.append(dict(decay=jnp.where(causal, jnp.exp(lg * rel), 0.0), q_dec=jnp.exp(lg * (idx + 1.0)),
                              k_dec=jnp.exp(lg * (c - 1.0 - idx)), g_c=math.exp(lg * c)))
        g_rows = jnp.where(row_head == 0, heads[0]["g_c"], heads[1]["g_c"])
        q_dec = jnp.where(v_head[0], heads[0]["q_dec"], heads[1]["q_dec"])
        units = []
        for c0 in range(0, ct, c):
            sl = slice(c0, c0 + c)
            units.append(dict(q=_Split(q_t[sl], 1), k=k_t[sl], v=v_t[sl]))
        for u in units:
            u["km"] = [jnp.where(m, u["k"], 0.0) for m in qk_head]
            u["vm"] = [_Split(jnp.where(m, u["v"], 0.0), 1) for m in v_head]
            u["s_in"] = [_dotp(u["q"], km, _NT, P_MISC) * hd["decay"] for km, hd in zip(u["km"], heads)]
            kv = [_dotp(km * hd["k_dec"], vm, _TN, P_MISC) for km, vm, hd in zip(u["km"], u["vm"], heads)]
            u["kv"] = kv[0] + kv[1]
        for u in units:
            o_in = [_dotp(s_in, vm, _NN, P_MISC) for s_in, vm in zip(u["s_in"], u["vm"])]
            u["o"] = o_in[0] + o_in[1]
        state = s_scr[t]
        for u in units:
            u["s0"] = state
            state = state * g_rows + u["kv"]
        rows = [u["o"] + _dotp(u["q"], u["s0"], _NN, P_MISC) * q_dec for u in units]
        o_tiles.append(rows[0] if len(rows) == 1 else jnp.concatenate(rows, axis=0))
        s_scr[t] = state
        final_states.append(state)
    o = jnp.concatenate(o_tiles, axis=1)
    o_ref[...] = _head_rms_finish(o, z, _head_ones())

    @pl.when(j == last)
    def _():
        for t in range(n_tiles):
            for hh in range(per_tile):
                r0 = hh * half
                cols = slice(hh * HEAD_DIM, (hh + 1) * HEAD_DIM)
                st = final_states[t]
                s_out_ref[t * per_tile + hh] = jnp.concatenate(
                    [st[r0:r0 + half, cols], st[LANES // 2 + r0:LANES // 2 + r0 + half, cols]], axis=0)


def _rotary_tile_rows():
    half, per_tile = HEAD_DIM // 2, LANES // HEAD_DIM
    order = []
    for t in range(W_MIX // LANES):
        for part in range(2):
            for hh in range(per_tile):
                base = (t * per_tile + hh) * HEAD_DIM + part * half
                order.extend(range(base, base + half))
    return order


def _rope_tables(pos, tile_paired=False):
    half = HEAD_DIM // 2
    inv = ROPE_BASE ** (-jnp.arange(half, dtype=F32) / half)
    ang = pos.astype(F32)[:, None] * inv[None, :]
    cos, sin = jnp.cos(ang), jnp.sin(ang)
    cos_t = jnp.tile(jnp.concatenate([cos, cos], axis=-1), (1, N_HEADS))
    if tile_paired:
        per_tile = LANES // HEAD_DIM
        sin_t = jnp.tile(jnp.concatenate([-sin] * per_tile + [sin] * per_tile, axis=-1), (1, W_MIX // LANES))
    else:
        sin_t = jnp.tile(jnp.concatenate([-sin, sin], axis=-1), (1, N_HEADS))
    return cos_t, sin_t


def _ret_call(p_b, n_seq, seq_len, cos_t, sin_t):
    ct = min(ROW_TILE, seq_len)
    p3 = p_b.reshape(n_seq, seq_len, B_W)
    o, s1 = pl.pallas_call(
        _ret_kernel,
        grid=(n_seq, seq_len // ct),
        in_specs=[
            pl.BlockSpec((None, ct, B_W), lambda b, j: (b, j, 0)),
            pl.BlockSpec((ct, W_MIX), lambda b, j: (j, 0)),
            pl.BlockSpec((ct, W_MIX), lambda b, j: (j, 0)),
        ],
        out_specs=[
            pl.BlockSpec((None, ct, W_MIX), lambda b, j: (b, j, 0)),
            pl.BlockSpec((None, N_HEADS, HEAD_DIM, HEAD_DIM), lambda b, j: (b, 0, 0, 0)),
        ],
        out_shape=[
            jax.ShapeDtypeStruct((n_seq, seq_len, W_MIX), F32),
            jax.ShapeDtypeStruct((n_seq, N_HEADS, HEAD_DIM, HEAD_DIM), F32),
        ],
        scratch_shapes=[pltpu.VMEM((W_MIX // LANES, LANES, LANES), F32)],
        compiler_params=pltpu.CompilerParams(dimension_semantics=("parallel", "arbitrary"),
                                             vmem_limit_bytes=VMEM_LIMIT),
        name="retention_prompt",
    )(p3, cos_t, sin_t)
    return o.reshape(n_seq * seq_len, W_MIX), s1


def _affine_scan(a, b, span):
    n, w = a.shape
    if span == SUBLANES and n > span:
        shape, axis = (n // span, span, w), 1
        a, b = a.reshape(shape), b.reshape(shape)
        pos = _iota((1, span, 1), 1)
    else:
        assert span >= n
        axis = 0
        pos = _iota((n, 1), 0)
    dist = 1
    while dist < span:
        keep = pos >= dist
        a_prev = jnp.where(keep, pltpu.roll(a, dist, axis=axis), 1.0)
        b_prev = jnp.where(keep, pltpu.roll(b, dist, axis=axis), 0.0)
        b = a * b_prev + b
        a = a * a_prev
        dist *= 2
    return a.reshape(n, w), b.reshape(n, w)


def _lru_kernel(p_ref, cw_ref, cb_ref, gw_ref, gb_ref, sp_ref, o_ref, h_out_ref, conv_out_ref,
                ext_scr, h_scr, ab_scr, hin_scr):
    j = pl.program_id(1)
    last = pl.num_programs(1) - 1
    ct = p_ref.shape[0]

    @pl.when(j == 0)
    def _():
        h_scr[...] = jnp.zeros(h_scr.shape, F32)

    p = p_ref[...]
    xr = p[:, :W_MIX]
    z = p[:, W_MIX:]
    xc = _conv_tile(xr, ext_scr, cw_ref, j == 0) + cb_ref[...]
    a, b = _lru_token_math(xc, gw_ref[...], gb_ref[...], sp_ref[...])
    n_grp = ct // SUBLANES
    a, b = _affine_scan(a, b, SUBLANES)
    n_tiles = W_MIX // LANES
    for t in range(n_tiles):
        ab_scr[t] = a[:, t * LANES:(t + 1) * LANES]
        ab_scr[n_tiles + t] = b[:, t * LANES:(t + 1) * LANES]
    ends = pl.ds(SUBLANES - 1, n_grp, stride=SUBLANES)
    a_end = jnp.concatenate([ab_scr[t, ends, :] for t in range(n_tiles)], axis=1)
    b_end = jnp.concatenate([ab_scr[n_tiles + t, ends, :] for t in range(n_tiles)], axis=1)
    a_end, b_end = _affine_scan(a_end, b_end, n_grp)
    h_prev = h_scr[...]
    h_end = a_end * h_prev + b_end
    grp = _iota((n_grp, 1), 0)
    hin_scr[...] = jnp.where(grp == 0, h_prev, pltpu.roll(h_end, 1, axis=0))
    h_in = jnp.concatenate([jnp.broadcast_to(hin_scr[g:g + 1, :], (SUBLANES, W_MIX)) for g in range(n_grp)], axis=0)
    hcur = a * h_in + b
    h_scr[...] = h_end[n_grp - 1:n_grp, :]
    o_ref[...] = hcur * _silu(z)

    @pl.when(j == last)
    def _():
        h_out_ref[...] = hcur[ct - 1:ct, :]
        conv_out_ref[...] = xr[ct - SUBLANES:ct, :]


def _block_diag_gates(gate_w):
    out = jnp.zeros((gate_w.shape[0], W_MIX, 2 * W_MIX), F32)
    for g in range(2):
        for n in range(N_HEADS):
            out = out.at[:, n * HEAD_DIM:(n + 1) * HEAD_DIM,
                         g * W_MIX + n * HEAD_DIM:g * W_MIX + (n + 1) * HEAD_DIM].set(gate_w[:, g, n])
    return out


def _lru_params(lp):
    return [lp["lru_conv_w"], lp["lru_conv_b"], lp["lru_gate_w"], lp["lru_gate_b"], lp["lru_lambda"]]


def _lru_call(p_c, n_seq, seq_len, lp):
    ct = min(LRU_TILE, seq_len)
    p3 = p_c.reshape(n_seq, seq_len, C_W)
    params = _lru_params(lp)
    o, h1, conv_tail = pl.pallas_call(
        _lru_kernel,
        grid=(n_seq, seq_len // ct),
        in_specs=[pl.BlockSpec((None, ct, C_W), lambda b, j: (b, j, 0))]
        + [_spec_of(a) for a in params],
        out_specs=[
            pl.BlockSpec((None, ct, W_MIX), lambda b, j: (b, j, 0)),
            pl.BlockSpec((None, 1, W_MIX), lambda b, j: (b, 0, 0)),
            pl.BlockSpec((None, SUBLANES, W_MIX), lambda b, j: (b, 0, 0)),
        ],
        out_shape=[
            jax.ShapeDtypeStruct((n_seq, seq_len, W_MIX), F32),
            jax.ShapeDtypeStruct((n_seq, 1, W_MIX), F32),
            jax.ShapeDtypeStruct((n_seq, SUBLANES, W_MIX), F32),
        ],
        scratch_shapes=[pltpu.VMEM((ct + SUBLANES, W_MIX), F32), pltpu.VMEM((1, W_MIX), F32),
                        pltpu.VMEM((2 * W_MIX // LANES, ct, LANES), F32),
                        pltpu.VMEM((ct // SUBLANES, W_MIX), F32)],
        compiler_params=pltpu.CompilerParams(dimension_semantics=("parallel", "arbitrary"),
                                             vmem_limit_bytes=VMEM_LIMIT),
        name="rglru_prompt",
    )(p3, *[_arr_of(a) for a in params])
    return (o.reshape(n_seq * seq_len, W_MIX), h1.reshape(n_seq, W_MIX),
            conv_tail[:, SUBLANES - (CONV_W - 1):, :])


def _gdn_kernel(p_ref, cw_ref, nal_ref, dtb_ref, ng_ref, o_ref, s_out_ref, conv_out_ref, ext_scr, s_scr):
    j = pl.program_id(1)
    last = pl.num_programs(1) - 1
    ct = p_ref.shape[0]

    @pl.when(j == 0)
    def _():
        s_scr[...] = jnp.zeros(s_scr.shape, F32)

    p = p_ref[...]
    raw = p[:, :D_QKV_W]
    z = p[:, D_QKV_W:D_QKV_W + W_MIX]
    b_raw = p[:, D_QKV_W + W_MIX:D_QKV_W + 2 * W_MIX]
    a_raw = p[:, D_QKV_W + 2 * W_MIX:]
    ones = _head_ones()
    qkv = _conv_tile(raw, ext_scr, cw_ref, j == 0)
    q, k, v, beta, g = _gdn_token_math(qkv, b_raw, a_raw, nal_ref[...], dtb_ref[...], ones)

    c = min(CHUNK, ct)
    ha = _HeadAlgebra(c)
    lt = (_iota((c, c), 0) >= _iota((c, c), 1)).astype(BF16)
    units = []
    for c0 in range(0, ct, c):
        sl = slice(c0, c0 + c)
        gc = _dot_const(g[sl], lt, _NN, CUMSUM_PIECES, const_left=True)
        gc_cols = jnp.sum(gc * ha.eye, axis=0, keepdims=True)
        diff = gc - gc_cols
        decay = jnp.where(ha.incl, jnp.exp(jnp.where(ha.incl, diff, 0.0)), 0.0)
        kb = k[sl] * beta[sl]
        e_gc = jnp.exp(gc)
        g_last = gc[c - 1:c, :]
        units.append(dict(decay=decay, kb=kb, q=q[sl], k=k[sl], vb=v[sl] * beta[sl], kbe=kb * e_gc,
                          k_out=k[sl] * jnp.exp(g_last - gc), q_in=q[sl] * e_gc, e_last=jnp.exp(g_last)))
    all_units = units
    state = s_scr[...]
    o_rows = []
    for w0 in range(0, len(all_units), WAVE):
        units = all_units[w0:w0 + WAVE]
        for u in units:
            kq = ha.nt(jnp.concatenate([u["kb"], u["q"]], axis=0), ha.bd(u["k"]))
            u["a_mat"] = jnp.where(ha.strict, kq[:c] * u["decay"], 0.0)
            u["qk"] = kq[c:] * u["decay"]
        for u, t_inv in zip(units, ha.inv_unit_lower_many([u["a_mat"] for u in units])):
            u["t_inv"] = t_inv
        for u in units:
            u["u"] = ha.nn(u["t_inv"], ha.bd(u["vb"]), BF16)
            u["w"] = ha.nn(u["t_inv"], ha.bd(u["kbe"]), BF16)
        for u in units:
            u["g_mat"] = ha.diag_bd(u["e_last"]) - ha.tn_bd(u["k_out"], u["w"])
            u["h_mat"] = ha.tn_bd(u["k_out"], u["u"])
            u["q_hat"] = u["q_in"] - ha.nn(u["qk"], ha.bd(u["w"]))
            u["o1"] = ha.nn(u["qk"], ha.bd(u["u"]))
            zz = lax.dot_general(jnp.concatenate([u["q_hat"], u["g_mat"]], axis=0).astype(BF16),
                                 state.astype(BF16), _NN, preferred_element_type=F32)
            o_rows.append(zz[:c] + u["o1"])
            state = zz[c:] + u["h_mat"]
    o = o_rows[0] if len(o_rows) == 1 else jnp.concatenate(o_rows, axis=0)
    s_scr[...] = state
    o_ref[...] = _head_rms_finish(o, z, ones, ng_ref[...])

    @pl.when(j == last)
    def _():
        for h in range(N_HEADS):
            hs = slice(h * HEAD_DIM, (h + 1) * HEAD_DIM)
            s_out_ref[h] = state[hs, hs]
        conv_out_ref[...] = raw[ct - SUBLANES:ct, :]


def _gdn_params(lp):
    return [lp["gdn_conv_w"], lp["gdn_A_log"], lp["gdn_dt_bias"], lp["gdn_norm_g"]]


def _gdn_call(p_d, n_seq, seq_len, lp):
    ct = min(ROW_TILE, seq_len)
    p3 = p_d.reshape(n_seq, seq_len, D_PACK_W)
    params = _gdn_params(lp)
    o, s1, conv_tail = pl.pallas_call(
        _gdn_kernel,
        grid=(n_seq, seq_len // ct),
        in_specs=[pl.BlockSpec((None, ct, D_PACK_W), lambda b, j: (b, j, 0))]
        + [_spec_of(a) for a in params],
        out_specs=[
            pl.BlockSpec((None, ct, W_MIX), lambda b, j: (b, j, 0)),
            pl.BlockSpec((None, N_HEADS, HEAD_DIM, HEAD_DIM), lambda b, j: (b, 0, 0, 0)),
            pl.BlockSpec((None, SUBLANES, D_QKV_W), lambda b, j: (b, 0, 0)),
        ],
        out_shape=[
            jax.ShapeDtypeStruct((n_seq, seq_len, W_MIX), F32),
            jax.ShapeDtypeStruct((n_seq, N_HEADS, HEAD_DIM, HEAD_DIM), F32),
            jax.ShapeDtypeStruct((n_seq, SUBLANES, D_QKV_W), F32),
        ],
        scratch_shapes=[pltpu.VMEM((ct + SUBLANES, D_QKV_W), F32), pltpu.VMEM((W_MIX, W_MIX), F32)],
        compiler_params=pltpu.CompilerParams(dimension_semantics=("parallel", "arbitrary"),
                                             vmem_limit_bytes=VMEM_LIMIT),
        name="gdn_prompt",
    )(p3, *[_arr_of(a) for a in params])
    return o.reshape(n_seq * seq_len, W_MIX), s1, conv_tail[:, SUBLANES - (CONV_W - 1):, :]


def _decode_pre_kernel(pa_ref, pb_ref, pc_ref, pd_ref, shift_ref, h0_ref, lconv_ref, gconv_ref, cos_ref, sin_ref,
                       mu_ref, w0_ref, w2_ref, a0_ref, a2_ref, kk_ref, ka_ref,
                       lcw_ref, lcb_ref, lgw_ref, lgb_ref, lsp_ref, gcw_ref, nal_ref, dtb_ref,
                       vt_ref, vn_ref, oc_ref, h1_ref, lconv1_ref, gconv1_ref):
    ones = _head_ones()
    pa_full = pa_ref[...]
    pa = pa_full[:, :A_SHIFT_W]
    pm = pa + (shift_ref[...] - pa) * mu_ref[...]
    r, k, v, ld, av, bv = _rwkv_token_math(pm, w0_ref[...], w2_ref[...], a0_ref[...], a2_ref[...],
                                           kk_ref[...], ka_ref[...], ones)
    vecs = [r, jnp.exp(ld), k, v, av, bv]
    plain = [r, k, v, pa_full[:, A_SHIFT_W:]]
    pb = pb_ref[...]
    cos, sin = cos_ref[...], sin_ref[...]
    vecs += [_rotary(pb[:, 0:W_MIX], cos, sin), _rotary(pb[:, W_MIX:2 * W_MIX], cos, sin) * (HEAD_DIM ** -0.5),
             pb[:, 2 * W_MIX:3 * W_MIX]]
    plain.append(pb[:, 3 * W_MIX:])
    pc = pc_ref[...]
    xr = pc[:, :W_MIX]
    taps = [lconv_ref[i] for i in range(CONV_W - 1)] + [xr]
    xc = taps[0] * lcw_ref[0:1, :]
    for i in range(1, CONV_W):
        xc = xc + taps[i] * lcw_ref[i:i + 1, :]
    xc = xc + lcb_ref[...]
    a, b = _lru_token_math(xc, lgw_ref[...], lgb_ref[...], lsp_ref[...])
    hcur = a * h0_ref[...] + b
    oc_ref[...] = hcur * _silu(pc[:, W_MIX:])
    h1_ref[...] = hcur
    for i in range(CONV_W - 1):
        lconv1_ref[i] = taps[i + 1]
    pd = pd_ref[...]
    raw = pd[:, :D_QKV_W]
    gtaps = [gconv_ref[i] for i in range(CONV_W - 1)] + [raw]
    qkv = gtaps[0] * gcw_ref[0:1, :]
    for i in range(1, CONV_W):
        qkv = qkv + gtaps[i] * gcw_ref[i:i + 1, :]
    q, kg, vg, beta, g = _gdn_token_math(qkv, pd[:, D_QKV_W + W_MIX:D_QKV_W + 2 * W_MIX],
                                         pd[:, D_QKV_W + 2 * W_MIX:], nal_ref[...], dtb_ref[...], ones)
    for i in range(CONV_W - 1):
        gconv1_ref[i] = gtaps[i + 1]
    vecs += [q, kg, vg, beta, g]
    plain.append(pd[:, D_QKV_W:D_QKV_W + W_MIX])
    assert len(vecs) == N_VEC_T and len(plain) == N_VEC_PLAIN
    for i, vec in enumerate(vecs):
        vt_ref[i] = vec.T
    for i, vec in enumerate(plain):
        vn_ref[i] = vec


def _decode_state_kernel(vt_ref, wkv_ref, ret_ref, gdn_ref, gam_ref, wkv1_ref, ret1_ref, gdn1_ref, o_ref):
    v_r, v_w, v_k, v_v, v_a, v_b, r_q, r_k, r_v, g_q, g_k, g_v, g_beta, g_g = range(N_VEC_T)
    hd = HEAD_DIM
    n = vt_ref.shape[-1]
    row = lambda idx, i: vt_ref[idx, pl.ds(i, 1), :]
    rows_of = lambda i: pl.ds(pl.multiple_of(i * hd, hd), hd)
    gamma = gam_ref[...]
    beta = vt_ref[g_beta, 0:1, :]
    eg = jnp.exp(vt_ref[g_g, 0:1, :])

    def first_pass(i, carry):
        acc_ret, acc_w, acc_q = carry
        rows = rows_of(i)
        s = wkv_ref[rows, :]
        sa = jnp.sum(s * vt_ref[v_a], axis=0, keepdims=True)
        s = s * vt_ref[v_w] + sa * vt_ref[v_b] + row(v_v, i) * vt_ref[v_k]
        wkv1_ref[rows, :] = s
        o_ref[0, pl.ds(i, 1), :] = jnp.sum(s * vt_ref[v_r], axis=0, keepdims=True)
        s = ret_ref[rows, :] * gamma + row(r_k, i) * vt_ref[r_v]
        ret1_ref[rows, :] = s
        acc_ret = acc_ret + row(r_q, i) * s
        s = gdn_ref[rows, :]
        return acc_ret, acc_w + row(g_k, i) * s, acc_q + row(g_q, i) * s

    zeros = jnp.zeros((hd, n), F32)
    acc_ret, acc_w, acc_q = lax.fori_loop(0, hd, first_pass, (zeros, zeros, zeros))
    o_ref[1] = acc_ret
    v_new = vt_ref[g_v] * beta - acc_w * (beta * eg)
    qk = jnp.sum(vt_ref[g_q] * vt_ref[g_k], axis=0, keepdims=True)
    o_ref[2] = acc_q * eg + qk * v_new

    def second_pass(i, carry):
        rows = rows_of(i)
        gdn1_ref[rows, :] = gdn_ref[rows, :] * eg + row(g_k, i) * v_new
        return carry

    lax.fori_loop(0, hd, second_pass, 0)


def _decode_finish_kernel(ot_ref, vn_ref, rk_ref, lng_ref, lnb_ref, ng_ref, oa_ref, ob_ref, od_ref):
    ones = _head_ones()
    r, k, v, z_a, z_b, z_d = (vn_ref[i] for i in range(N_VEC_PLAIN))
    oa_ref[...] = _rwkv_finish(ot_ref[0].T, r, k, v, z_a, rk_ref[...], lng_ref[...], lnb_ref[...], ones)
    ob_ref[...] = _head_rms_finish(ot_ref[1].T, z_b, ones)
    od_ref[...] = _head_rms_finish(ot_ref[2].T, z_d, ones, ng_ref[...])


def _batch_minor(state):
    n_layers, n = state.shape[:2]
    return jnp.transpose(state, (0, 2, 3, 4, 1)).reshape(n_layers, -1, n)


def _batch_major(flat_state):
    n = flat_state.shape[-1]
    return jnp.transpose(flat_state.reshape(N_HEADS, HEAD_DIM, HEAD_DIM, n), (3, 0, 1, 2))


def _decode_layer(l, p_a, p_b, p_c, p_d, carried, cos_t, sin_t, lp):
    n = p_a.shape[0]
    assert n % LANES == 0, "the decode state kernel keeps the batch on lanes"
    flat = HEAD_DIM * HEAD_DIM
    taps = CONV_W - 1
    rwkv_params = [_row(lp["rwkv_mu"]), _row(lp["rwkv_w0"]), lp["rwkv_w2"], _row(lp["rwkv_a0"]), lp["rwkv_a2"],
                   _row(lp["rwkv_k_k"]), _row(lp["rwkv_k_a"])]
    gdn_params = _gdn_params(lp)
    full = lambda a: pl.BlockSpec(a.shape, lambda i: (0,) * a.ndim)
    layer_blk = lambda a: pl.BlockSpec((None,) + a.shape[1:], lambda i, nd=a.ndim: (l,) + (0,) * (nd - 1))
    projs = [p_a, p_b, p_c, p_d]
    layered = [carried["shift"], carried["lru_h"], carried["lru_conv"], carried["gdn_conv"]]
    consts = [cos_t, sin_t, *rwkv_params, *_lru_params(lp), *gdn_params[:3]]
    out_shapes = [
        jax.ShapeDtypeStruct((N_VEC_T, W_MIX, n), F32),
        jax.ShapeDtypeStruct((N_VEC_PLAIN, n, W_MIX), F32),
        jax.ShapeDtypeStruct((n, W_MIX), F32),
        jax.ShapeDtypeStruct((n, W_MIX), F32),
        jax.ShapeDtypeStruct((taps, n, W_MIX), F32),
        jax.ShapeDtypeStruct((taps, n, D_QKV_W), F32),
    ]
    vec_t, vec_n, o_c, lru_h1, lru_conv1, gdn_conv1 = pl.pallas_call(
        _decode_pre_kernel,
        grid=(1,),
        in_specs=[full(a) for a in projs] + [layer_blk(a) for a in layered] + [_spec_of(a) for a in consts],
        out_specs=[pl.BlockSpec(s.shape, lambda i, nd=len(s.shape): (0,) * nd) for s in out_shapes],
        out_shape=out_shapes,
        compiler_params=pltpu.CompilerParams(dimension_semantics=("arbitrary",), vmem_limit_bytes=VMEM_LIMIT),
        name="decode_tokens",
    )(*projs, *layered, *[_arr_of(a) for a in consts])

    gam = jnp.broadcast_to((1.0 - 2.0 ** (-5.0 - jnp.arange(N_HEADS, dtype=F32)))[:, None, None], (N_HEADS, 1, n))
    state_in = pl.BlockSpec((None, flat, n), lambda h: (l, h, 0))
    state_out = pl.BlockSpec((flat, n), lambda h: (h, 0))
    wkv1, ret1, gdn1, o_t = pl.pallas_call(
        _decode_state_kernel,
        grid=(N_HEADS,),
        in_specs=[pl.BlockSpec((N_VEC_T, HEAD_DIM, n), lambda h: (0, h, 0)), state_in, state_in, state_in,
                  pl.BlockSpec((None, 1, n), lambda h: (h, 0, 0))],
        out_specs=[state_out, state_out, state_out, pl.BlockSpec((3, HEAD_DIM, n), lambda h: (0, h, 0))],
        out_shape=[jax.ShapeDtypeStruct((N_HEADS * flat, n), F32)] * 3 + [jax.ShapeDtypeStruct((3, W_MIX, n), F32)],
        compiler_params=pltpu.CompilerParams(dimension_semantics=("parallel",), vmem_limit_bytes=VMEM_LIMIT),
        name="decode_states",
    )(vec_t, carried["wkv"], carried["ret"], carried["gdn"], gam)

    finish_ins = [o_t, vec_n, _row(lp["rwkv_r_k"]), _row(lp["rwkv_ln_g"]), _row(lp["rwkv_ln_b"]), gdn_params[3]]
    o_a, o_b, o_d = pl.pallas_call(
        _decode_finish_kernel,
        grid=(1,),
        in_specs=[_spec_of(a) for a in finish_ins],
        out_specs=[pl.BlockSpec((n, W_MIX), lambda i: (0, 0))] * 3,
        out_shape=[jax.ShapeDtypeStruct((n, W_MIX), F32)] * 3,
        compiler_params=pltpu.CompilerParams(dimension_semantics=("arbitrary",), vmem_limit_bytes=VMEM_LIMIT),
        name="decode_finish",
    )(*[_arr_of(a) for a in finish_ins])
    new_states = (_batch_major(wkv1), p_a[:, :A_SHIFT_W], _batch_major(ret1), lru_h1,
                  jnp.transpose(lru_conv1, (1, 0, 2)), _batch_major(gdn1), jnp.transpose(gdn_conv1, (1, 0, 2)))
    return (o_a, o_b, o_c, o_d), new_states


def _prompt_layer(p_a, p_b, p_c, p_d, n_seq, seq_len, cos_t, sin_t, lp):
    o_a, wkv1, shift1 = _rwkv_call(p_a, n_seq, seq_len, lp)
    o_b, ret1 = _ret_call(p_b, n_seq, seq_len, cos_t, sin_t)
    o_c, lru_h1, lru_conv1 = _lru_call(p_c, n_seq, seq_len, lp)
    o_d, gdn1, gdn_conv1 = _gdn_call(p_d, n_seq, seq_len, lp)
    return (o_a, o_b, o_c, o_d), (wkv1, shift1, ret1, lru_h1, lru_conv1, gdn1, gdn_conv1)


def _run_group(x, mods, pos, carried, layers, final_g):
    n_seq, seq_len, d = x.shape
    x2 = x.reshape(n_seq * seq_len, d)
    cos_t, sin_t = _rope_tables(pos, tile_paired=carried is None)
    new = []
    n_layers = len(layers)
    for l, lp in enumerate(layers):
        w_qk = lp["w_qk_tiles"] if carried is None else lp["w_qk"]
        p_a, p_b, p_c, p_d = _inproj_call(x2, mods, lp["norm_g"], l, lp["w_t"], lp["w_ba"], w_qk, seq_len)
        if carried is None:
            branches, st = _prompt_layer(p_a, p_b, p_c, p_d, n_seq, seq_len, cos_t, sin_t, lp)
        else:
            branches, st = _decode_layer(l, p_a, p_b, p_c, p_d, carried, cos_t, sin_t, lp)
        new.append(st)
        x2 = _outproj_call(x2, mods, lp["norm_g"], branches, l, lp["w_t"], lp["w_up_bf16"],
                           lp["w_out_bf16"], final_g, seq_len, final=(l == n_layers - 1))
    stacked = tuple(jnp.stack([s[i] for s in new], axis=0) for i in range(7))
    return x2.reshape(n_seq, seq_len, d), stacked


def kernel(x_prompt, x_sample, c_prompt, c_sample, state_rwkv_wkv, state_rwkv_shift, state_ret, state_lru_h, state_lru_conv, state_gdn, state_gdn_conv, ada_w, ada_b, norm_g, w_in, rwkv_mu, rwkv_w0, rwkv_w2, rwkv_a0, rwkv_a2, rwkv_k_k, rwkv_k_a, rwkv_r_k, rwkv_ln_g, rwkv_ln_b, lru_conv_w, lru_conv_b, lru_gate_w, lru_gate_b, lru_lambda, gdn_conv_w, gdn_A_log, gdn_dt_bias, gdn_norm_g, w_up, w_out, final_g):
    n_layers = ada_w.shape[0]
    n_prompt, seq_len, _ = x_prompt.shape
    n_sample, dec_len, _ = x_sample.shape
    assert dec_len == 1, "the decode path handles one token per sequence"
    w_in_t = jnp.swapaxes(w_in, 1, 2).astype(BF16)
    off_ba = A_W + B_W + C_W + D_QKV_W
    rows = lambda a: a.reshape(n_layers, 1, -1)
    tile_rows = _rotary_tile_rows()
    qk_rows = jnp.asarray([A_W + r for r in tile_rows] + [A_W + W_MIX + r for r in tile_rows], jnp.int32)
    stacked = dict(
        norm_g=rows(norm_g), w_up_bf16=w_up.astype(BF16), w_out_bf16=w_out.astype(BF16),
        w_ba=jnp.repeat(w_in_t[:, off_ba:off_ba + 2 * N_HEADS], HEAD_DIM, axis=1),
        w_qk=w_in_t[:, A_W:A_W + 2 * W_MIX],
        w_qk_tiles=jnp.take(w_in_t, qk_rows, axis=1),
        rwkv_mu=rows(rwkv_mu), rwkv_w0=rows(rwkv_w0), rwkv_w2=rwkv_w2, rwkv_a0=rows(rwkv_a0), rwkv_a2=rwkv_a2,
        rwkv_k_k=rows(rwkv_k_k), rwkv_k_a=rows(rwkv_k_a), rwkv_r_k=rows(rwkv_r_k), rwkv_ln_g=rows(rwkv_ln_g),
        rwkv_ln_b=rows(rwkv_ln_b), lru_conv_w=lru_conv_w, lru_conv_b=rows(lru_conv_b),
        lru_gate_w=_block_diag_gates(lru_gate_w), lru_gate_b=rows(lru_gate_b), lru_lambda=rows(lru_lambda),
        gdn_conv_w=gdn_conv_w, gdn_A_log=rows(jnp.repeat(gdn_A_log, HEAD_DIM, axis=1)),
        gdn_dt_bias=rows(jnp.repeat(gdn_dt_bias, HEAD_DIM, axis=1)),
        gdn_norm_g=rows(jnp.tile(gdn_norm_g, (1, N_HEADS))))
    layers = [dict({name: _LayerParam(arr, l) for name, arr in stacked.items()}, w_t=w_in_t)
              for l in range(n_layers)]
    mods_p, mods_s = _ada_call(c_prompt, c_sample, ada_w, ada_b)

    y_prompt, new_p = _run_group(x_prompt, mods_p, jnp.arange(seq_len, dtype=jnp.int32), None, layers, final_g)
    carried = dict(wkv=_batch_minor(state_rwkv_wkv), ret=_batch_minor(state_ret), gdn=_batch_minor(state_gdn),
                   shift=state_rwkv_shift, lru_h=state_lru_h,
                   lru_conv=jnp.transpose(state_lru_conv, (0, 2, 1, 3)),
                   gdn_conv=jnp.transpose(state_gdn_conv, (0, 2, 1, 3)))
    pos_s = PAST_LEN + jnp.arange(dec_len, dtype=jnp.int32)
    y_sample, new_s = _run_group(x_sample, mods_s, pos_s, carried, layers, final_g)
    return (y_prompt, y_sample) + new_p + new_s
```

```python
import functools
import math

import jax
import jax.numpy as jnp
from jax import lax
from jax.experimental import pallas as pl
from jax.experimental.pallas import tpu as pltpu

F32 = jnp.float32
BF16 = jnp.bfloat16
HI = lax.Precision.HIGHEST

N_HEADS = 4
HEAD_DIM = 64
W_MIX = N_HEADS * HEAD_DIM
LORA = 64
CONV_W = 4
N_BRANCH = 4
LRU_C = 8.0
ROPE_BASE = 10000.0
EPS = 1e-6
RWKV_GN_EPS = 64e-5
PAST_LEN = 16384
A_SHIFT_W = 3 * W_MIX + 2 * LORA
A_W = A_SHIFT_W + W_MIX
B_W = 4 * W_MIX
C_W = 2 * W_MIX
D_QKV_W = 3 * W_MIX
D_W = D_QKV_W + 2 * N_HEADS + W_MIX
D_PACK_W = D_QKV_W + 3 * W_MIX

SUBLANES = 8
LANES = 128
VMEM_LIMIT = 56 * 1024 * 1024

CHUNK = 64
RET_CHUNK = 128
INV_BLOCK = 16
WAVE = 16
ROW_TILE = 1024
LRU_TILE = 512
PROJ_TILE = 512
OUT_TILE = 1024
N_VEC_T = 14
N_VEC_PLAIN = 6


def _mm(a, b, prec=HI):
    return lax.dot_general(a, b, (((1,), (0,)), ((), ())), precision=prec, preferred_element_type=F32)


def _mm_nt(a, b, prec=HI):
    return lax.dot_general(a, b, (((1,), (1,)), ((), ())), precision=prec, preferred_element_type=F32)


def _mm_tn(a, b, prec=HI):
    return lax.dot_general(a, b, (((0,), (0,)), ((), ())), precision=prec, preferred_element_type=F32)


_NN = (((1,), (0,)), ((), ()))
_NT = (((1,), (1,)), ((), ()))
_TN = (((0,), (0,)), ((), ()))

P_INV = 1
P_STATE = 1
P_MISC = 1
HEAD_SUM_PIECES = 1
CUMSUM_PIECES = 2


class _Split:
    def __init__(self, x, passes):
        self.hi = x.astype(BF16)
        self.lo = (x - self.hi.astype(F32)).astype(BF16) if passes > 1 else None


def _dotp(a, b, dims=_NN, passes=1):
    a = a if isinstance(a, _Split) else _Split(a, passes)
    b = b if isinstance(b, _Split) else _Split(b, passes)
    d = lambda x, y: lax.dot_general(x, y, dims, preferred_element_type=F32)
    out = d(a.hi, b.hi)
    if passes > 1:
        out = out + (d(a.hi, b.lo) + d(a.lo, b.hi))
    return out


def _iota(shape, dim):
    return lax.broadcasted_iota(jnp.int32, shape, dim)


def _sigmoid(x):
    return 0.5 * jnp.tanh(0.5 * x) + 0.5


def _silu(x):
    return x * _sigmoid(x)


def _softplus(x):
    return jnp.maximum(x, 0.0) + jnp.log(1.0 + jnp.exp(-jnp.abs(x)))


def _pieces(x, n):
    out = []
    for i in range(n):
        p = x.astype(BF16)
        out.append(p)
        if i + 1 < n:
            x = x - p.astype(F32)
    return out


def _dot_const(x, const, dims=_NN, n=2, const_left=False):
    out = None
    for p in _pieces(x, n):
        t = lax.dot_general(*((const, p) if const_left else (p, const)), dims, preferred_element_type=F32)
        out = t if out is None else out + t
    return out


def _head_ones():
    return (_iota((W_MIX, W_MIX), 0) // HEAD_DIM == _iota((W_MIX, W_MIX), 1) // HEAD_DIM).astype(BF16)


def _head_sum(x, ones, signed=False):
    return _dot_const(x, ones, n=HEAD_SUM_PIECES + (1 if signed else 0))


def _rms(x):
    return x * lax.rsqrt(jnp.mean(x * x, axis=-1, keepdims=True) + EPS)


def _inv_unit_lower(a):
    return _inv_unit_lower_many([a])[0]


def _inv_unit_lower_many(mats):
    n = mats[0].shape[0]
    ri, ci = _iota((n, n), 0), _iota((n, n), 1)
    eye = (ri == ci).astype(F32)
    diag_blk = (ri // INV_BLOCK) == (ci // INV_BLOCK)
    mm = lambda x, y: _dotp(x, y, _NN, P_INV)
    sp = lambda x: _Split(x, P_INV)
    d = [jnp.where(diag_blk, a, 0.0) for a in mats]
    nb = [a - di for a, di in zip(mats, d)]
    td = [eye - di for di in d]
    p = d
    for _ in range(int(math.log2(INV_BLOCK)) - 1):
        ps = [sp(pi) for pi in p]
        p = [mm(pi, pi) for pi in ps]
        td = [mm(ti, eye + pi) for ti, pi in zip(td, p)]
    tds = [sp(ti) for ti in td]
    x = [mm(ti, ni) for ti, ni in zip(tds, nb)]
    t = [eye - xi for xi in x]
    p = x
    for _ in range(int(math.log2(n // INV_BLOCK)) - 1):
        ps = [sp(pi) for pi in p]
        p = [mm(pi, pi) for pi in ps]
        t = [mm(ti, eye + pi) for ti, pi in zip(t, p)]
    return [mm(ti, tdi) for ti, tdi in zip(t, tds)]


class _HeadAlgebra:
    def __init__(self, c):
        assert c == HEAD_DIM, "side-by-side head products need CHUNK == HEAD_DIM"
        w = W_MIX
        row, lane = _iota((c, w), 0), _iota((c, w), 1)
        col = lane % HEAD_DIM
        tile_lane = _iota((c, LANES), 1)
        self.tile_head = [tile_lane // HEAD_DIM == h for h in range(LANES // HEAD_DIM)]
        self.eye = (row == col).astype(F32)
        self.strict = row > col
        self.incl = row >= col
        self.inv_blk = (row // INV_BLOCK) == (col // INV_BLOCK)
        r2, c2 = _iota((w, w), 0), _iota((w, w), 1)
        self.eye_full = r2 == c2
        self.same_head = (r2 // HEAD_DIM) == (c2 // HEAD_DIM)

    def bd(self, y):
        yb = y.astype(BF16)
        zero = jnp.zeros((yb.shape[0], LANES), BF16)
        blocks = []
        for t in range(W_MIX // LANES):
            tile = yb[:, t * LANES:(t + 1) * LANES]
            for m in self.tile_head:
                kept = jnp.where(m, tile, zero)
                blocks.append(jnp.concatenate([kept if s == t else zero for s in range(W_MIX // LANES)], axis=1))
        return jnp.concatenate(blocks, axis=0)

    def nn(self, x, bd_y, out=F32):
        return lax.dot_general(x.astype(BF16), bd_y, _NN, preferred_element_type=F32).astype(out)

    def nt(self, x, bd_y):
        return lax.dot_general(x.astype(BF16), bd_y, _NT, preferred_element_type=F32)

    def tn_bd(self, x, y):
        full = lax.dot_general(x.astype(BF16), y.astype(BF16), _TN, preferred_element_type=F32)
        return jnp.where(self.same_head, full, 0.0)

    def diag_bd(self, row_vec):
        return jnp.where(self.eye_full, row_vec, 0.0)

    def plus_eye(self, bd_p):
        return jnp.where(self.eye_full, jnp.ones_like(bd_p), bd_p)

    def inv_unit_lower_many(self, mats):
        mats = [a.astype(BF16) for a in mats]
        zero = jnp.zeros_like(mats[0])
        eye = self.eye.astype(BF16)
        d = [jnp.where(self.inv_blk, a, zero) for a in mats]
        nb = [jnp.where(self.inv_blk, zero, a) for a in mats]
        td = [eye - di for di in d]
        p = d
        bdp = [self.bd(pi) for pi in p]
        for _ in range(int(math.log2(INV_BLOCK)) - 1):
            p = [self.nn(pi, bi, BF16) for pi, bi in zip(p, bdp)]
            bdp = [self.bd(pi) for pi in p]
            td = [self.nn(ti, self.plus_eye(bi), BF16) for ti, bi in zip(td, bdp)]
        bd_td = [self.bd(ti) for ti in td]
        x = [self.nn(ti, self.bd(ni), BF16) for ti, ni in zip(td, nb)]
        t = [eye - xi for xi in x]
        p = x
        bdp = [self.bd(pi) for pi in p]
        for _ in range(int(math.log2(HEAD_DIM // INV_BLOCK)) - 1):
            p = [self.nn(pi, bi, BF16) for pi, bi in zip(p, bdp)]
            bdp = [self.bd(pi) for pi in p]
            t = [self.nn(ti, self.plus_eye(bi), BF16) for ti, bi in zip(t, bdp)]
        return [self.nn(ti, bi, BF16) for ti, bi in zip(t, bd_td)]


def _ada_kernel(cp_ref, cs_ref, w_ref, b_ref, op_ref, os_ref):
    w = _Split(w_ref[...], 3)
    op_ref[...] = _dotp(_silu(cp_ref[...]), w, _NN, 3) + b_ref[...]
    os_ref[...] = _dotp(_silu(cs_ref[...]), w, _NN, 3) + b_ref[...]


def _ada_call(c_prompt, c_sample, ada_w, ada_b):
    n_layers, d, d3 = ada_w.shape
    n_p, n_s = c_prompt.shape[0], c_sample.shape[0]
    return pl.pallas_call(
        _ada_kernel,
        grid=(n_layers, d3 // d),
        in_specs=[
            pl.BlockSpec((n_p, d), lambda l, j: (0, 0)),
            pl.BlockSpec((n_s, d), lambda l, j: (0, 0)),
            pl.BlockSpec((None, d, d), lambda l, j: (l, 0, j)),
            pl.BlockSpec((None, 1, d), lambda l, j: (l, 0, j)),
        ],
        out_specs=[pl.BlockSpec((None, n_p, d), lambda l, j: (l, 0, j)),
                   pl.BlockSpec((None, n_s, d), lambda l, j: (l, 0, j))],
        out_shape=[jax.ShapeDtypeStruct((n_layers, n_p, d3), F32), jax.ShapeDtypeStruct((n_layers, n_s, d3), F32)],
        compiler_params=pltpu.CompilerParams(dimension_semantics=("arbitrary", "arbitrary"),
                                             vmem_limit_bytes=VMEM_LIMIT),
        name="ada_mod",
    )(c_prompt, c_sample, ada_w, ada_b.reshape(n_layers, 1, d3))


def _modulated_norm(x, g, scale, shift):
    return _rms(x) * g * (1.0 + scale) + shift


def _inproj_kernel(x_ref, sc_ref, sh_ref, g_ref, w_ref, wz_ref, wba_ref, wqk_ref, oa_ref, ob_ref, oc_ref, od_ref):
    h = _modulated_norm(x_ref[...], g_ref[...], sc_ref[...], sh_ref[...]).astype(BF16)
    proj = lambda w: lax.dot_general(h, w, _NT, preferred_element_type=F32)
    oa_ref[...] = proj(w_ref[0, 0:A_W, :])
    ob_ref[:, :2 * W_MIX] = proj(wqk_ref[...])
    ob_ref[:, 2 * W_MIX:] = proj(w_ref[0, A_W + 2 * W_MIX:A_W + B_W, :])
    oc_ref[...] = proj(w_ref[0, A_W + B_W:A_W + B_W + C_W, :])
    lo = A_W + B_W + C_W
    od_ref[:, :D_QKV_W] = proj(w_ref[0, lo:lo + D_QKV_W, :])
    od_ref[:, D_QKV_W:D_QKV_W + W_MIX] = proj(wz_ref[0])
    od_ref[:, D_QKV_W + W_MIX:] = proj(wba_ref[...])


def _weight_rows(layer, row0, n_rows, d):
    return pl.BlockSpec((pl.Element(1), pl.Element(n_rows), pl.Element(d)), lambda i: (layer, row0, 0))


MOD_SHIFT, MOD_SCALE, MOD_GATE = 0, 1, 2


def _mod_specs(mods, layer, parts, d, tm, seq_len):
    if seq_len == 1:
        return [mods] * len(parts), [pl.BlockSpec((None, tm, d), lambda i, c=c: (layer, i, c)) for c in parts]
    per_seq = seq_len // tm
    by_seq = mods.reshape(mods.shape[0], mods.shape[1], 1, mods.shape[2])
    return ([by_seq] * len(parts),
            [pl.BlockSpec((None, None, 1, d), lambda i, c=c: (layer, i // per_seq, 0, c)) for c in parts])


class _LayerParam:
    def __init__(self, arr, layer):
        self.arr, self.layer = arr, layer

    def spec(self):
        return pl.BlockSpec((None,) + self.arr.shape[1:], lambda *_: (self.layer,) + (0,) * (self.arr.ndim - 1))


def _spec_of(a):
    if isinstance(a, _LayerParam):
        return a.spec()
    return pl.BlockSpec(a.shape, lambda *_: (0,) * a.ndim)


def _arr_of(a):
    return a.arr if isinstance(a, _LayerParam) else a


def _inproj_call(x2, mods_all, g, l, w_t, w_ba, w_qk, seq_len):
    m, d = x2.shape
    tm = min(PROJ_TILE, m, seq_len) if seq_len > 1 else m
    widths = (A_W, B_W, C_W, D_PACK_W)
    mods, mod_specs = _mod_specs(mods_all, l, (MOD_SCALE, MOD_SHIFT), d, tm, seq_len)
    off_ba = A_W + B_W + C_W + D_QKV_W
    off_z = off_ba + 2 * N_HEADS
    return pl.pallas_call(
        _inproj_kernel,
        grid=(m // tm,),
        in_specs=[pl.BlockSpec((tm, d), lambda i: (i, 0))] + mod_specs + [
            _spec_of(g),
            _weight_rows(l, 0, off_ba, d),
            _weight_rows(l, off_z, W_MIX, d),
            _spec_of(w_ba),
            _spec_of(w_qk),
        ],
        out_specs=[pl.BlockSpec((tm, wd), lambda i: (i, 0)) for wd in widths],
        out_shape=[jax.ShapeDtypeStruct((m, wd), F32) for wd in widths],
        compiler_params=pltpu.CompilerParams(dimension_semantics=("parallel",), vmem_limit_bytes=VMEM_LIMIT),
        name="in_proj",
    )(x2, *mods, _arr_of(g), w_t, w_t, _arr_of(w_ba), _arr_of(w_qk))


def _outproj_kernel(x_ref, sc_ref, sh_ref, gt_ref, g_ref, ba_ref, bb_ref, bc_ref, bd_ref,
                    wg_ref, wup_ref, wout_ref, fg_ref, o_ref, *, final):
    x = x_ref[...]
    d = x.shape[-1]
    h = _modulated_norm(x, g_ref[...], sc_ref[...], sh_ref[...]).astype(BF16)
    merged = jnp.zeros(x.shape, F32)
    for n, br_ref in enumerate((ba_ref, bb_ref, bc_ref, bd_ref)):
        gl = lax.dot_general(h, wg_ref[0, n * d:(n + 1) * d, :], _NT, preferred_element_type=F32)
        up = jnp.dot(br_ref[...].astype(BF16), wup_ref[n], preferred_element_type=F32)
        merged = merged + _sigmoid(gl) * up
    out = jnp.dot(merged.astype(BF16), wout_ref[...], preferred_element_type=F32)
    xn = x + gt_ref[...] * out
    if final:
        xn = _rms(xn) * fg_ref[...]
    o_ref[...] = xn


def _outproj_call(x2, mods_all, g, branches, l, w_t, wup, wout, final_g, seq_len, final):
    m, d = x2.shape
    tm = min(OUT_TILE, m, seq_len) if seq_len > 1 else m
    mods, mod_specs = _mod_specs(mods_all, l, (MOD_SCALE, MOD_SHIFT, MOD_GATE), d, tm, seq_len)
    full = lambda a: pl.BlockSpec(a.shape, lambda i: (0,) * a.ndim)
    off_g = A_W + B_W + C_W + D_W
    gate_rows = _weight_rows(l, off_g, N_BRANCH * d, d)
    return pl.pallas_call(
        functools.partial(_outproj_kernel, final=final),
        grid=(m // tm,),
        in_specs=[pl.BlockSpec((tm, d), lambda i: (i, 0))] + mod_specs + [_spec_of(g)]
        + [pl.BlockSpec((tm, W_MIX), lambda i: (i, 0)) for _ in branches]
        + [gate_rows, _spec_of(wup), _spec_of(wout), pl.BlockSpec((1, d), lambda i: (0, 0))],
        out_specs=pl.BlockSpec((tm, d), lambda i: (i, 0)),
        out_shape=jax.ShapeDtypeStruct((m, d), F32),
        compiler_params=pltpu.CompilerParams(dimension_semantics=("parallel",), vmem_limit_bytes=VMEM_LIMIT),
        name="out_proj",
    )(x2, *mods, _arr_of(g), *branches, w_t, _arr_of(wup), _arr_of(wout), final_g.reshape(1, d))


def _rwkv_token_math(pm, w0, w2, a0, a2, k_k, k_a, ones):
    r = pm[:, 0:W_MIX]
    k = pm[:, W_MIX:2 * W_MIX]
    v = pm[:, 2 * W_MIX:3 * W_MIX]
    wd = pm[:, 3 * W_MIX:3 * W_MIX + LORA]
    ad = pm[:, 3 * W_MIX + LORA:]
    w_log = -_softplus(-(w0 + _dotp(jnp.tanh(wd), w2, _NN, P_MISC))) - 0.5
    log_decay = -jnp.exp(w_log)
    a = _sigmoid(a0 + _dotp(ad, a2, _NN, P_MISC))
    kx = k * k_k
    kk = kx * lax.rsqrt(_head_sum(kx * kx, ones) + EPS)
    k = k * (1.0 + (a - 1.0) * k_a)
    return r, k, v, log_decay, -kk, kk * a


def _rwkv_finish(o, r, k, v, z, r_k, ln_g, ln_b, ones):
    mean = _head_sum(o, ones) * (1.0 / HEAD_DIM)
    dlt = o - mean
    var = _head_sum(dlt * dlt, ones) * (1.0 / HEAD_DIM)
    on = dlt * lax.rsqrt(var + RWKV_GN_EPS) * ln_g + ln_b
    bonus = _head_sum(r * k * r_k, ones, signed=True) * v
    return (on + bonus) * _silu(z)


def _swap_halves(x):
    half = HEAD_DIM // 2
    n = x.shape[-1]
    first = (_iota(x.shape, 1) & half) == 0
    return jnp.where(first, pltpu.roll(x, n - half, axis=1), pltpu.roll(x, half, axis=1))


def _rotary(x, cos, sin):
    return x * cos + _swap_halves(x) * sin


def _lru_token_math(xc, gate_w, gate_b, lam):
    gates = _dotp(xc, gate_w, _NN, P_MISC) + gate_b
    r_gate = _sigmoid(gates[:, :W_MIX])
    i_gate = _sigmoid(gates[:, W_MIX:])
    log_a = -LRU_C * r_gate * _softplus(-lam)
    a = jnp.exp(log_a)
    b = jnp.sqrt(1.0 - jnp.exp(2.0 * log_a)) * (i_gate * xc)
    return a, b


def _gdn_token_math(qkv, b_raw, a_raw, a_log, dt_bias, ones):
    qkv = _silu(qkv)
    q = qkv[:, 0:W_MIX]
    k = qkv[:, W_MIX:2 * W_MIX]
    v = qkv[:, 2 * W_MIX:]
    q = q * lax.rsqrt(_head_sum(q * q, ones) + EPS) * (HEAD_DIM ** -0.5)
    k = k * lax.rsqrt(_head_sum(k * k, ones) + EPS)
    beta = _sigmoid(b_raw)
    g = -jnp.exp(a_log) * _softplus(a_raw + dt_bias)
    return q, k, v, beta, g


def _head_rms_finish(o, z, ones, gain=None):
    y = o * lax.rsqrt(_head_sum(o * o, ones) * (1.0 / HEAD_DIM) + EPS)
    if gain is not None:
        y = y * gain
    return y * _silu(z)


def _conv_tile(u, ext_ref, w_ref, first):
    n = u.shape[0]

    @pl.when(first)
    def _():
        ext_ref[0:SUBLANES, :] = jnp.zeros((SUBLANES, u.shape[1]), F32)

    ext_ref[SUBLANES:SUBLANES + n, :] = u
    out = None
    for j in range(CONV_W):
        term = _rows_back(u, ext_ref, CONV_W - 1 - j) * w_ref[j:j + 1, :]
        out = term if out is None else out + term
    ext_ref[0:SUBLANES, :] = u[n - SUBLANES:n, :]
    return out


def _rows_back(u, ext_ref, back):
    if back == 0:
        return u
    n, ch = u.shape
    tiles = (n // SUBLANES, SUBLANES, ch)
    pos = _iota((1, SUBLANES, 1), 1)
    earlier = ext_ref[0:n, :].reshape(tiles)
    return pltpu.roll(jnp.where(pos >= SUBLANES - back, earlier, u.reshape(tiles)), back, axis=1).reshape(n, ch)


def _rwkv_kernel(p_ref, mu_ref, w0_ref, w2_ref, a0_ref, a2_ref, kk_ref, ka_ref, rk_ref, lng_ref, lnb_ref,
                 o_ref, s_out_ref, shift_out_ref, s_scr, ext_scr):
    j = pl.program_id(1)
    last = pl.num_programs(1) - 1
    ct = p_ref.shape[0]

    @pl.when(j == 0)
    def _():
        s_scr[...] = jnp.zeros(s_scr.shape, F32)
        ext_scr[0:SUBLANES, :] = jnp.zeros((SUBLANES, A_SHIFT_W), F32)

    p = p_ref[...]
    pa = p[:, :A_SHIFT_W]
    z = p[:, A_SHIFT_W:]
    ext_scr[SUBLANES:SUBLANES + ct, :] = pa
    prev = _rows_back(pa, ext_scr, 1)
    ext_scr[0:SUBLANES, :] = pa[ct - SUBLANES:ct, :]
    pm = pa + (prev - pa) * mu_ref[...]
    ones = _head_ones()
    r, k, v, ld, av, bv = _rwkv_token_math(pm, w0_ref[...], w2_ref[...], a0_ref[...], a2_ref[...],
                                           kk_ref[...], ka_ref[...], ones)

    c = min(CHUNK, ct)
    ha = _HeadAlgebra(c)
    lt = (_iota((c, c), 0) >= _iota((c, c), 1)).astype(BF16)
    units = []
    for c0 in range(0, ct, c):
        sl = slice(c0, c0 + c)
        ldc = ld[sl]
        cum = _dot_const(ldc, lt, _NN, CUMSUM_PIECES, const_left=True)
        e_neg = jnp.exp(-cum)
        e_out = jnp.exp(cum[c - 1:c, :] - cum)
        units.append(dict(a=av[sl] * jnp.exp(cum - ldc), r=r[sl] * jnp.exp(cum), b=bv[sl] * e_neg, k=k[sl] * e_neg,
                          bo=bv[sl] * e_out, ko=k[sl] * e_out, v=v[sl], g=jnp.exp(cum[c - 1:c, :])))
    all_units = units
    state = s_scr[...]
    o_rows = []
    for w0 in range(0, len(all_units), WAVE):
        units = all_units[w0:w0 + WAVE]
        for u in units:
            lhs = jnp.concatenate([u["a"], u["r"]], axis=0)
            u["mb"] = ha.nt(lhs, ha.bd(u["b"]))
            u["mk"] = ha.nt(lhs, ha.bd(u["k"]))
            u["bd_v"] = ha.bd(u["v"])
        for u in units:
            u["m_ab"] = jnp.where(ha.strict, u["mb"][:c], 0.0)
            u["m_ak"] = jnp.where(ha.strict, u["mk"][:c], 0.0)
            u["m_rb"] = jnp.where(ha.incl, u["mb"][c:], 0.0)
            u["m_rk"] = jnp.where(ha.incl, u["mk"][c:], 0.0)
        for u, t_inv in zip(units, ha.inv_unit_lower_many([-u["m_ab"] for u in units])):
            u["t_inv"] = t_inv
        for u in units:
            u["makv"] = ha.nn(u["m_ak"], u["bd_v"], BF16)
        for u in units:
            u["a_hat"] = ha.nn(u["t_inv"], ha.bd(u["a"]), BF16)
            u["u1"] = ha.nn(u["t_inv"], ha.bd(u["makv"]), BF16)
        for u in units:
            u["r_hat"] = u["r"] + ha.nn(u["m_rb"], ha.bd(u["a_hat"]))
            u["o1"] = ha.nn(u["m_rb"], ha.bd(u["u1"])) + ha.nn(u["m_rk"], u["bd_v"])
            u["g_t"] = ha.diag_bd(u["g"]) + ha.tn_bd(u["bo"], u["a_hat"])
            u["h_t"] = ha.tn_bd(jnp.concatenate([u["bo"], u["ko"]], axis=0),
                                jnp.concatenate([u["u1"], u["v"].astype(BF16)], axis=0))
            zz = lax.dot_general(jnp.concatenate([u["r_hat"], u["g_t"]], axis=0).astype(BF16), state.astype(BF16),
                                 _NN, preferred_element_type=F32)
            o_rows.append(zz[:c] + u["o1"])
            state = zz[c:] + u["h_t"]
    o = o_rows[0] if len(o_rows) == 1 else jnp.concatenate(o_rows, axis=0)
    s_scr[...] = state
    o_ref[...] = _rwkv_finish(o, r, k, v, z, rk_ref[...], lng_ref[...], lnb_ref[...], ones)

    @pl.when(j == last)
    def _():
        eye_h = (_iota((HEAD_DIM, HEAD_DIM), 0) == _iota((HEAD_DIM, HEAD_DIM), 1)).astype(F32)
        for h in range(N_HEADS):
            hs = slice(h * HEAD_DIM, (h + 1) * HEAD_DIM)
            s_out_ref[h] = _mm_nt(eye_h, state[hs, hs])
        shift_out_ref[...] = pa[ct - 1:ct, :]


def _row(a):
    return a if isinstance(a, _LayerParam) else a.reshape(1, -1)


def _rwkv_call(p_a, n_seq, seq_len, lp):
    ct = min(ROW_TILE, seq_len)
    p3 = p_a.reshape(n_seq, seq_len, A_W)
    params = [_row(lp["rwkv_mu"]), _row(lp["rwkv_w0"]), lp["rwkv_w2"], _row(lp["rwkv_a0"]), lp["rwkv_a2"],
              _row(lp["rwkv_k_k"]), _row(lp["rwkv_k_a"]), _row(lp["rwkv_r_k"]), _row(lp["rwkv_ln_g"]),
              _row(lp["rwkv_ln_b"])]
    o, s1, shift1 = pl.pallas_call(
        _rwkv_kernel,
        grid=(n_seq, seq_len // ct),
        in_specs=[pl.BlockSpec((None, ct, A_W), lambda b, j: (b, j, 0))]
        + [_spec_of(a) for a in params],
        out_specs=[
            pl.BlockSpec((None, ct, W_MIX), lambda b, j: (b, j, 0)),
            pl.BlockSpec((None, N_HEADS, HEAD_DIM, HEAD_DIM), lambda b, j: (b, 0, 0, 0)),
            pl.BlockSpec((None, 1, A_SHIFT_W), lambda b, j: (b, 0, 0)),
        ],
        out_shape=[
            jax.ShapeDtypeStruct((n_seq, seq_len, W_MIX), F32),
            jax.ShapeDtypeStruct((n_seq, N_HEADS, HEAD_DIM, HEAD_DIM), F32),
            jax.ShapeDtypeStruct((n_seq, 1, A_SHIFT_W), F32),
        ],
        scratch_shapes=[pltpu.VMEM((W_MIX, W_MIX), F32), pltpu.VMEM((ct + SUBLANES, A_SHIFT_W), F32)],
        compiler_params=pltpu.CompilerParams(dimension_semantics=("parallel", "arbitrary"),
                                             vmem_limit_bytes=VMEM_LIMIT),
        name="rwkv7_prompt",
    )(p3, *[_arr_of(a) for a in params])
    return o.reshape(n_seq * seq_len, W_MIX), s1, shift1.reshape(n_seq, A_SHIFT_W)


def _ret_kernel(p_ref, cos_ref, sin_ref, o_ref, s_out_ref, s_scr):
    j = pl.program_id(1)
    last = pl.num_programs(1) - 1
    ct = p_ref.shape[0]

    @pl.when(j == 0)
    def _():
        s_scr[...] = jnp.zeros(s_scr.shape, F32)

    p = p_ref[...]
    cos, sin = cos_ref[...], sin_ref[...]
    v = p[:, 2 * W_MIX:3 * W_MIX]
    z = p[:, 3 * W_MIX:]
    c = min(RET_CHUNK, ct)
    ri, ci = _iota((c, c), 0), _iota((c, c), 1)
    causal = ri >= ci
    rel = jnp.where(causal, ri - ci, 0).astype(F32)
    idx = _iota((c, 1), 0).astype(F32)
    half, per_tile = HEAD_DIM // 2, LANES // HEAD_DIM
    lane = _iota((1, LANES), 1)
    qk_head = [(lane // half) % per_tile == hh for hh in range(per_tile)]
    v_head = [lane // HEAD_DIM == hh for hh in range(per_tile)]
    row_head = (_iota((LANES, 1), 0) // half) % per_tile
    n_tiles = W_MIX // LANES
    o_tiles = []
    final_states = []
    for t in range(n_tiles):
        ts = slice(t * LANES, (t + 1) * LANES)
        cos_t, sin_t = cos[:, ts], sin[:, ts]
        q_t = p[:, ts]
        k_t = p[:, W_MIX + t * LANES:W_MIX + (t + 1) * LANES]
        q_t = q_t * cos_t + pltpu.roll(q_t, LANES // 2, axis=1) * sin_t
        k_t = (k_t * cos_t + pltpu.roll(k_t, LANES // 2, axis=1) * sin_t) * (HEAD_DIM ** -0.5)
        v_t = v[:, ts]
        heads = []
        for hh in range(per_tile):
            lg = math.log(1.0 - 2.0 ** (-5.0 - (t * per_tile + hh)))
            heads.append(dict(decay=jnp.where(causal, jnp.exp(lg * rel), 0.0), q_dec=jnp.exp(lg * (idx + 1.0)),
                              k_dec=jnp.exp(lg * (c - 1.0 - idx)), g_c=math.exp(lg * c)))
        g_rows = jnp.where(row_head == 0, heads[0]["g_c"], heads[1]["g_c"])
        q_dec = jnp.where(v_head[0], heads[0]["q_dec"], heads[1]["q_dec"])
        units = []
        for c0 in range(0, ct, c):
            sl = slice(c0, c0 + c)
            units.append(dict(q=_Split(q_t[sl], 1), k=k_t[sl], v=v_t[sl]))
        for u in units:
            u["km"] = [jnp.where(m, u["k"], 0.0) for m in qk_head]
            u["vm"] = [_Split(jnp.where(m, u["v"], 0.0), 1) for m in v_head]
            u["s_in"] = [_dotp(u["q"], km, _NT, P_MISC) * hd["decay"] for km, hd in zip(u["km"], heads)]
            kv = [_dotp(km * hd["k_dec"], vm, _TN, P_MISC) for km, vm, hd in zip(u["km"], u["vm"], heads)]
            u["kv"] = kv[0] + kv[1]
        for u in units:
            o_in = [_dotp(s_in, vm, _NN, P_MISC) for s_in, vm in zip(u["s_in"], u["vm"])]
            u["o"] = o_in[0] + o_in[1]
        state = s_scr[t]
        for u in units:
            u["s0"] = state
            state = state * g_rows + u["kv"]
        rows = [u["o"] + _dotp(u["q"], u["s0"], _NN, P_MISC) * q_dec for u in units]
        o_tiles.append(rows[0] if len(rows) == 1 else jnp.concatenate(rows, axis=0))
        s_scr[t] = state
        final_states.append(state)
    o = jnp.concatenate(o_tiles, axis=1)
    o_ref[...] = _head_rms_finish(o, z, _head_ones())

    @pl.when(j == last)
    def _():
        for t in range(n_tiles):
            for hh in range(per_tile):
                r0 = hh * half
                cols = slice(hh * HEAD_DIM, (hh + 1) * HEAD_DIM)
                st = final_states[t]
                s_out_ref[t * per_tile + hh] = jnp.concatenate(
                    [st[r0:r0 + half, cols], st[LANES // 2 + r0:LANES // 2 + r0 + half, cols]], axis=0)


def _rope_tables(pos, tile_paired=False):
    half = HEAD_DIM // 2
    inv = ROPE_BASE ** (-jnp.arange(half, dtype=F32) / half)
    ang = pos.astype(F32)[:, None] * inv[None, :]
    cos, sin = jnp.cos(ang), jnp.sin(ang)
    cos_t = jnp.tile(jnp.concatenate([cos, cos], axis=-1), (1, N_HEADS))
    if tile_paired:
        per_tile = LANES // HEAD_DIM
        sin_t = jnp.tile(jnp.concatenate([-sin] * per_tile + [sin] * per_tile, axis=-1), (1, W_MIX // LANES))
    else:
        sin_t = jnp.tile(jnp.concatenate([-sin, sin], axis=-1), (1, N_HEADS))
    return cos_t, sin_t


def _ret_call(p_b, n_seq, seq_len, cos_t, sin_t):
    ct = min(ROW_TILE, seq_len)
    p3 = p_b.reshape(n_seq, seq_len, B_W)
    o, s1 = pl.pallas_call(
        _ret_kernel,
        grid=(n_seq, seq_len // ct),
        in_specs=[
            pl.BlockSpec((None, ct, B_W), lambda b, j: (b, j, 0)),
            pl.BlockSpec((ct, W_MIX), lambda b, j: (j, 0)),
            pl.BlockSpec((ct, W_MIX), lambda b, j: (j, 0)),
        ],
        out_specs=[
            pl.BlockSpec((None, ct, W_MIX), lambda b, j: (b, j, 0)),
            pl.BlockSpec((None, N_HEADS, HEAD_DIM, HEAD_DIM), lambda b, j: (b, 0, 0, 0)),
        ],
        out_shape=[
            jax.ShapeDtypeStruct((n_seq, seq_len, W_MIX), F32),
            jax.ShapeDtypeStruct((n_seq, N_HEADS, HEAD_DIM, HEAD_DIM), F32),
        ],
        scratch_shapes=[pltpu.VMEM((W_MIX // LANES, LANES, LANES), F32)],
        compiler_params=pltpu.CompilerParams(dimension_semantics=("parallel", "arbitrary"),
                                             vmem_limit_bytes=VMEM_LIMIT),
        name="retention_prompt",
    )(p3, cos_t, sin_t)
    return o.reshape(n_seq * seq_len, W_MIX), s1


def _affine_scan(a, b, span):
    n, w = a.shape
    if span == SUBLANES and n > span:
        shape, axis = (n // span, span, w), 1
        a, b = a.reshape(shape), b.reshape(shape)
        pos = _iota((1, span, 1), 1)
    else:
        assert span >= n
        axis = 0
        pos = _iota((n, 1), 0)
    dist = 1
    while dist < span:
        keep = pos >= dist
        a_prev = jnp.where(keep, pltpu.roll(a, dist, axis=axis), 1.0)
        b_prev = jnp.where(keep, pltpu.roll(b, dist, axis=axis), 0.0)
        b = a * b_prev + b
        a = a * a_prev
        dist *= 2
    return a.reshape(n, w), b.reshape(n, w)


def _lru_kernel(p_ref, cw_ref, cb_ref, gw_ref, gb_ref, sp_ref, o_ref, h_out_ref, conv_out_ref,
                ext_scr, h_scr, ab_scr, hin_scr):
    j = pl.program_id(1)
    last = pl.num_programs(1) - 1
    ct = p_ref.shape[0]

    @pl.when(j == 0)
    def _():
        h_scr[...] = jnp.zeros(h_scr.shape, F32)

    p = p_ref[...]
    xr = p[:, :W_MIX]
    z = p[:, W_MIX:]
    xc = _conv_tile(xr, ext_scr, cw_ref, j == 0) + cb_ref[...]
    a, b = _lru_token_math(xc, gw_ref[...], gb_ref[...], sp_ref[...])
    n_grp = ct // SUBLANES
    a, b = _affine_scan(a, b, SUBLANES)
    n_tiles = W_MIX // LANES
    for t in range(n_tiles):
        ab_scr[t] = a[:, t * LANES:(t + 1) * LANES]
        ab_scr[n_tiles + t] = b[:, t * LANES:(t + 1) * LANES]
    ends = pl.ds(SUBLANES - 1, n_grp, stride=SUBLANES)
    a_end = jnp.concatenate([ab_scr[t, ends, :] for t in range(n_tiles)], axis=1)
    b_end = jnp.concatenate([ab_scr[n_tiles + t, ends, :] for t in range(n_tiles)], axis=1)
    a_end, b_end = _affine_scan(a_end, b_end, n_grp)
    h_prev = h_scr[...]
    h_end = a_end * h_prev + b_end
    grp = _iota((n_grp, 1), 0)
    hin_scr[...] = jnp.where(grp == 0, h_prev, pltpu.roll(h_end, 1, axis=0))
    h_in = jnp.concatenate([jnp.broadcast_to(hin_scr[g:g + 1, :], (SUBLANES, W_MIX)) for g in range(n_grp)], axis=0)
    hcur = a * h_in + b
    h_scr[...] = h_end[n_grp - 1:n_grp, :]
    o_ref[...] = hcur * _silu(z)

    @pl.when(j == last)
    def _():
        h_out_ref[...] = hcur[ct - 1:ct, :]
        conv_out_ref[...] = xr[ct - SUBLANES:ct, :]


def _block_diag_gates(gate_w):
    out = jnp.zeros((gate_w.shape[0], W_MIX, 2 * W_MIX), F32)
    for g in range(2):
        for n in range(N_HEADS):
            out = out.at[:, n * HEAD_DIM:(n + 1) * HEAD_DIM,
                         g * W_MIX + n * HEAD_DIM:g * W_MIX + (n + 1) * HEAD_DIM].set(gate_w[:, g, n])
    return out


def _lru_params(lp):
    return [lp["lru_conv_w"], lp["lru_conv_b"], lp["lru_gate_w"], lp["lru_gate_b"], lp["lru_lambda"]]


def _lru_call(p_c, n_seq, seq_len, lp):
    ct = min(LRU_TILE, seq_len)
    p3 = p_c.reshape(n_seq, seq_len, C_W)
    params = _lru_params(lp)
    o, h1, conv_tail = pl.pallas_call(
        _lru_kernel,
        grid=(n_seq, seq_len // ct),
        in_specs=[pl.BlockSpec((None, ct, C_W), lambda b, j: (b, j, 0))]
        + [_spec_of(a) for a in params],
        out_specs=[
            pl.BlockSpec((None, ct, W_MIX), lambda b, j: (b, j, 0)),
            pl.BlockSpec((None, 1, W_MIX), lambda b, j: (b, 0, 0)),
            pl.BlockSpec((None, SUBLANES, W_MIX), lambda b, j: (b, 0, 0)),
        ],
        out_shape=[
            jax.ShapeDtypeStruct((n_seq, seq_len, W_MIX), F32),
            jax.ShapeDtypeStruct((n_seq, 1, W_MIX), F32),
            jax.ShapeDtypeStruct((n_seq, SUBLANES, W_MIX), F32),
        ],
        scratch_shapes=[pltpu.VMEM((ct + SUBLANES, W_MIX), F32), pltpu.VMEM((1, W_MIX), F32),
                        pltpu.VMEM((2 * W_MIX // LANES, ct, LANES), F32),
                        pltpu.VMEM((ct // SUBLANES, W_MIX), F32)],
        compiler_params=pltpu.CompilerParams(dimension_semantics=("parallel", "arbitrary"),
                                             vmem_limit_bytes=VMEM_LIMIT),
        name="rglru_prompt",
    )(p3, *[_arr_of(a) for a in params])
    return (o.reshape(n_seq * seq_len, W_MIX), h1.reshape(n_seq, W_MIX),
            conv_tail[:, SUBLANES - (CONV_W - 1):, :])


def _gdn_kernel(p_ref, cw_ref, nal_ref, dtb_ref, ng_ref, o_ref, s_out_ref, conv_out_ref, ext_scr, s_scr):
    j = pl.program_id(1)
    last = pl.num_programs(1) - 1
    ct = p_ref.shape[0]

    @pl.when(j == 0)
    def _():
        s_scr[...] = jnp.zeros(s_scr.shape, F32)

    p = p_ref[...]
    raw = p[:, :D_QKV_W]
    z = p[:, D_QKV_W:D_QKV_W + W_MIX]
    b_raw = p[:, D_QKV_W + W_MIX:D_QKV_W + 2 * W_MIX]
    a_raw = p[:, D_QKV_W + 2 * W_MIX:]
    ones = _head_ones()
    qkv = _conv_tile(raw, ext_scr, cw_ref, j == 0)
    q, k, v, beta, g = _gdn_token_math(qkv, b_raw, a_raw, nal_ref[...], dtb_ref[...], ones)

    c = min(CHUNK, ct)
    ha = _HeadAlgebra(c)
    lt = (_iota((c, c), 0) >= _iota((c, c), 1)).astype(BF16)
    units = []
    for c0 in range(0, ct, c):
        sl = slice(c0, c0 + c)
        gc = _dot_const(g[sl], lt, _NN, CUMSUM_PIECES, const_left=True)
        gc_cols = jnp.sum(gc * ha.eye, axis=0, keepdims=True)
        diff = gc - gc_cols
        decay = jnp.where(ha.incl, jnp.exp(jnp.where(ha.incl, diff, 0.0)), 0.0)
        kb = k[sl] * beta[sl]
        e_gc = jnp.exp(gc)
        g_last = gc[c - 1:c, :]
        units.append(dict(decay=decay, kb=kb, q=q[sl], k=k[sl], vb=v[sl] * beta[sl], kbe=kb * e_gc,
                          k_out=k[sl] * jnp.exp(g_last - gc), q_in=q[sl] * e_gc, e_last=jnp.exp(g_last)))
    all_units = units
    state = s_scr[...]
    o_rows = []
    for w0 in range(0, len(all_units), WAVE):
        units = all_units[w0:w0 + WAVE]
        for u in units:
            kq = ha.nt(jnp.concatenate([u["kb"], u["q"]], axis=0), ha.bd(u["k"]))
            u["a_mat"] = jnp.where(ha.strict, kq[:c] * u["decay"], 0.0)
            u["qk"] = kq[c:] * u["decay"]
        for u, t_inv in zip(units, ha.inv_unit_lower_many([u["a_mat"] for u in units])):
            u["t_inv"] = t_inv
        for u in units:
            u["u"] = ha.nn(u["t_inv"], ha.bd(u["vb"]), BF16)
            u["w"] = ha.nn(u["t_inv"], ha.bd(u["kbe"]), BF16)
        for u in units:
            u["g_mat"] = ha.diag_bd(u["e_last"]) - ha.tn_bd(u["k_out"], u["w"])
            u["h_mat"] = ha.tn_bd(u["k_out"], u["u"])
            u["q_hat"] = u["q_in"] - ha.nn(u["qk"], ha.bd(u["w"]))
            u["o1"] = ha.nn(u["qk"], ha.bd(u["u"]))
            zz = lax.dot_general(jnp.concatenate([u["q_hat"], u["g_mat"]], axis=0).astype(BF16),
                                 state.astype(BF16), _NN, preferred_element_type=F32)
            o_rows.append(zz[:c] + u["o1"])
            state = zz[c:] + u["h_mat"]
    o = o_rows[0] if len(o_rows) == 1 else jnp.concatenate(o_rows, axis=0)
    s_scr[...] = state
    o_ref[...] = _head_rms_finish(o, z, ones, ng_ref[...])

    @pl.when(j == last)
    def _():
        for h in range(N_HEADS):
            hs = slice(h * HEAD_DIM, (h + 1) * HEAD_DIM)
            s_out_ref[h] = state[hs, hs]
        conv_out_ref[...] = raw[ct - SUBLANES:ct, :]


def _gdn_params(lp):
    return [lp["gdn_conv_w"], lp["gdn_A_log"], lp["gdn_dt_bias"], lp["gdn_norm_g"]]


def _gdn_call(p_d, n_seq, seq_len, lp):
    ct = min(ROW_TILE, seq_len)
    p3 = p_d.reshape(n_seq, seq_len, D_PACK_W)
    params = _gdn_params(lp)
    o, s1, conv_tail = pl.pallas_call(
        _gdn_kernel,
        grid=(n_seq, seq_len // ct),
        in_specs=[pl.BlockSpec((None, ct, D_PACK_W), lambda b, j: (b, j, 0))]
        + [_spec_of(a) for a in params],
        out_specs=[
            pl.BlockSpec((None, ct, W_MIX), lambda b, j: (b, j, 0)),
            pl.BlockSpec((None, N_HEADS, HEAD_DIM, HEAD_DIM), lambda b, j: (b, 0, 0, 0)),
            pl.BlockSpec((None, SUBLANES, D_QKV_W), lambda b, j: (b, 0, 0)),
        ],
        out_shape=[
            jax.ShapeDtypeStruct((n_seq, seq_len, W_MIX), F32),
            jax.ShapeDtypeStruct((n_seq, N_HEADS, HEAD_DIM, HEAD_DIM), F32),
            jax.ShapeDtypeStruct((n_seq, SUBLANES, D_QKV_W), F32),
        ],
        scratch_shapes=[pltpu.VMEM((ct + SUBLANES, D_QKV_W), F32), pltpu.VMEM((W_MIX, W_MIX), F32)],
        compiler_params=pltpu.CompilerParams(dimension_semantics=("parallel", "arbitrary"),
                                             vmem_limit_bytes=VMEM_LIMIT),
        name="gdn_prompt",
    )(p3, *[_arr_of(a) for a in params])
    return o.reshape(n_seq * seq_len, W_MIX), s1, conv_tail[:, SUBLANES - (CONV_W - 1):, :]


def _decode_pre_kernel(pa_ref, pb_ref, pc_ref, pd_ref, shift_ref, h0_ref, lconv_ref, gconv_ref, cos_ref, sin_ref,
                       mu_ref, w0_ref, w2_ref, a0_ref, a2_ref, kk_ref, ka_ref,
                       lcw_ref, lcb_ref, lgw_ref, lgb_ref, lsp_ref, gcw_ref, nal_ref, dtb_ref,
                       vt_ref, vn_ref, oc_ref, h1_ref, lconv1_ref, gconv1_ref):
    ones = _head_ones()
    pa_full = pa_ref[...]
    pa = pa_full[:, :A_SHIFT_W]
    pm = pa + (shift_ref[...] - pa) * mu_ref[...]
    r, k, v, ld, av, bv = _rwkv_token_math(pm, w0_ref[...], w2_ref[...], a0_ref[...], a2_ref[...],
                                           kk_ref[...], ka_ref[...], ones)
    vecs = [r, jnp.exp(ld), k, v, av, bv]
    plain = [r, k, v, pa_full[:, A_SHIFT_W:]]
    pb = pb_ref[...]
    cos, sin = cos_ref[...], sin_ref[...]
    vecs += [_rotary(pb[:, 0:W_MIX], cos, sin), _rotary(pb[:, W_MIX:2 * W_MIX], cos, sin) * (HEAD_DIM ** -0.5),
             pb[:, 2 * W_MIX:3 * W_MIX]]
    plain.append(pb[:, 3 * W_MIX:])
    pc = pc_ref[...]
    xr = pc[:, :W_MIX]
    taps = [lconv_ref[i] for i in range(CONV_W - 1)] + [xr]
    xc = taps[0] * lcw_ref[0:1, :]
    for i in range(1, CONV_W):
        xc = xc + taps[i] * lcw_ref[i:i + 1, :]
    xc = xc + lcb_ref[...]
    a, b = _lru_token_math(xc, lgw_ref[...], lgb_ref[...], lsp_ref[...])
    hcur = a * h0_ref[...] + b
    oc_ref[...] = hcur * _silu(pc[:, W_MIX:])
    h1_ref[...] = hcur
    for i in range(CONV_W - 1):
        lconv1_ref[i] = taps[i + 1]
    pd = pd_ref[...]
    raw = pd[:, :D_QKV_W]
    gtaps = [gconv_ref[i] for i in range(CONV_W - 1)] + [raw]
    qkv = gtaps[0] * gcw_ref[0:1, :]
    for i in range(1, CONV_W):
        qkv = qkv + gtaps[i] * gcw_ref[i:i + 1, :]
    q, kg, vg, beta, g = _gdn_token_math(qkv, pd[:, D_QKV_W + W_MIX:D_QKV_W + 2 * W_MIX],
                                         pd[:, D_QKV_W + 2 * W_MIX:], nal_ref[...], dtb_ref[...], ones)
    for i in range(CONV_W - 1):
        gconv1_ref[i] = gtaps[i + 1]
    vecs += [q, kg, vg, beta, g]
    plain.append(pd[:, D_QKV_W:D_QKV_W + W_MIX])
    assert len(vecs) == N_VEC_T and len(plain) == N_VEC_PLAIN
    for i, vec in enumerate(vecs):
        vt_ref[i] = vec.T
    for i, vec in enumerate(plain):
        vn_ref[i] = vec


def _decode_state_kernel(vt_ref, wkv_ref, ret_ref, gdn_ref, gam_ref, wkv1_ref, ret1_ref, gdn1_ref, o_ref):
    v_r, v_w, v_k, v_v, v_a, v_b, r_q, r_k, r_v, g_q, g_k, g_v, g_beta, g_g = range(N_VEC_T)
    hd = HEAD_DIM
    n = vt_ref.shape[-1]
    row = lambda idx, i: vt_ref[idx, pl.ds(i, 1), :]
    rows_of = lambda i: pl.ds(pl.multiple_of(i * hd, hd), hd)
    gamma = gam_ref[...]
    beta = vt_ref[g_beta, 0:1, :]
    eg = jnp.exp(vt_ref[g_g, 0:1, :])

    def first_pass(i, carry):
        acc_ret, acc_w, acc_q = carry
        rows = rows_of(i)
        s = wkv_ref[rows, :]
        sa = jnp.sum(s * vt_ref[v_a], axis=0, keepdims=True)
        s = s * vt_ref[v_w] + sa * vt_ref[v_b] + row(v_v, i) * vt_ref[v_k]
        wkv1_ref[rows, :] = s
        o_ref[0, pl.ds(i, 1), :] = jnp.sum(s * vt_ref[v_r], axis=0, keepdims=True)
        s = ret_ref[rows, :] * gamma + row(r_k, i) * vt_ref[r_v]
        ret1_ref[rows, :] = s
        acc_ret = acc_ret + row(r_q, i) * s
        s = gdn_ref[rows, :]
        return acc_ret, acc_w + row(g_k, i) * s, acc_q + row(g_q, i) * s

    zeros = jnp.zeros((hd, n), F32)
    acc_ret, acc_w, acc_q = lax.fori_loop(0, hd, first_pass, (zeros, zeros, zeros))
    o_ref[1] = acc_ret
    v_new = vt_ref[g_v] * beta - acc_w * (beta * eg)
    qk = jnp.sum(vt_ref[g_q] * vt_ref[g_k], axis=0, keepdims=True)
    o_ref[2] = acc_q * eg + qk * v_new

    def second_pass(i, carry):
        rows = rows_of(i)
        gdn1_ref[rows, :] = gdn_ref[rows, :] * eg + row(g_k, i) * v_new
        return carry

    lax.fori_loop(0, hd, second_pass, 0)


def _decode_finish_kernel(ot_ref, vn_ref, rk_ref, lng_ref, lnb_ref, ng_ref, oa_ref, ob_ref, od_ref):
    ones = _head_ones()
    r, k, v, z_a, z_b, z_d = (vn_ref[i] for i in range(N_VEC_PLAIN))
    oa_ref[...] = _rwkv_finish(ot_ref[0].T, r, k, v, z_a, rk_ref[...], lng_ref[...], lnb_ref[...], ones)
    ob_ref[...] = _head_rms_finish(ot_ref[1].T, z_b, ones)
    od_ref[...] = _head_rms_finish(ot_ref[2].T, z_d, ones, ng_ref[...])


def _batch_minor(state):
    n_layers, n = state.shape[:2]
    return jnp.transpose(state, (0, 2, 3, 4, 1)).reshape(n_layers, -1, n)


def _batch_major(flat_state):
    n = flat_state.shape[-1]
    return jnp.transpose(flat_state.reshape(N_HEADS, HEAD_DIM, HEAD_DIM, n), (3, 0, 1, 2))


def _decode_layer(l, p_a, p_b, p_c, p_d, carried, cos_t, sin_t, lp):
    n = p_a.shape[0]
    assert n % LANES == 0, "the decode state kernel keeps the batch on lanes"
    flat = HEAD_DIM * HEAD_DIM
    taps = CONV_W - 1
    rwkv_params = [_row(lp["rwkv_mu"]), _row(lp["rwkv_w0"]), lp["rwkv_w2"], _row(lp["rwkv_a0"]), lp["rwkv_a2"],
                   _row(lp["rwkv_k_k"]), _row(lp["rwkv_k_a"])]
    gdn_params = _gdn_params(lp)
    full = lambda a: pl.BlockSpec(a.shape, lambda i: (0,) * a.ndim)
    layer_blk = lambda a: pl.BlockSpec((None,) + a.shape[1:], lambda i, nd=a.ndim: (l,) + (0,) * (nd - 1))
    projs = [p_a, p_b, p_c, p_d]
    layered = [carried["shift"], carried["lru_h"], carried["lru_conv"], carried["gdn_conv"]]
    consts = [cos_t, sin_t, *rwkv_params, *_lru_params(lp), *gdn_params[:3]]
    out_shapes = [
        jax.ShapeDtypeStruct((N_VEC_T, W_MIX, n), F32),
        jax.ShapeDtypeStruct((N_VEC_PLAIN, n, W_MIX), F32),
        jax.ShapeDtypeStruct((n, W_MIX), F32),
        jax.ShapeDtypeStruct((n, W_MIX), F32),
        jax.ShapeDtypeStruct((taps, n, W_MIX), F32),
        jax.ShapeDtypeStruct((taps, n, D_QKV_W), F32),
    ]
    vec_t, vec_n, o_c, lru_h1, lru_conv1, gdn_conv1 = pl.pallas_call(
        _decode_pre_kernel,
        grid=(1,),
        in_specs=[full(a) for a in projs] + [layer_blk(a) for a in layered] + [_spec_of(a) for a in consts],
        out_specs=[pl.BlockSpec(s.shape, lambda i, nd=len(s.shape): (0,) * nd) for s in out_shapes],
        out_shape=out_shapes,
        compiler_params=pltpu.CompilerParams(dimension_semantics=("arbitrary",), vmem_limit_bytes=VMEM_LIMIT),
        name="decode_tokens",
    )(*projs, *layered, *[_arr_of(a) for a in consts])

    gam = jnp.broadcast_to((1.0 - 2.0 ** (-5.0 - jnp.arange(N_HEADS, dtype=F32)))[:, None, None], (N_HEADS, 1, n))
    state_in = pl.BlockSpec((None, flat, n), lambda h: (l, h, 0))
    state_out = pl.BlockSpec((flat, n), lambda h: (h, 0))
    wkv1, ret1, gdn1, o_t = pl.pallas_call(
        _decode_state_kernel,
        grid=(N_HEADS,),
        in_specs=[pl.BlockSpec((N_VEC_T, HEAD_DIM, n), lambda h: (0, h, 0)), state_in, state_in, state_in,
                  pl.BlockSpec((None, 1, n), lambda h: (h, 0, 0))],
        out_specs=[state_out, state_out, state_out, pl.BlockSpec((3, HEAD_DIM, n), lambda h: (0, h, 0))],
        out_shape=[jax.ShapeDtypeStruct((N_HEADS * flat, n), F32)] * 3 + [jax.ShapeDtypeStruct((3, W_MIX, n), F32)],
        compiler_params=pltpu.CompilerParams(dimension_semantics=("parallel",), vmem_limit_bytes=VMEM_LIMIT),
        name="decode_states",
    )(vec_t, carried["wkv"], carried["ret"], carried["gdn"], gam)

    finish_ins = [o_t, vec_n, _row(lp["rwkv_r_k"]), _row(lp["rwkv_ln_g"]), _row(lp["rwkv_ln_b"]), gdn_params[3]]
    o_a, o_b, o_d = pl.pallas_call(
        _decode_finish_kernel,
        grid=(1,),
        in_specs=[_spec_of(a) for a in finish_ins],
        out_specs=[pl.BlockSpec((n, W_MIX), lambda i: (0, 0))] * 3,
        out_shape=[jax.ShapeDtypeStruct((n, W_MIX), F32)] * 3,
        compiler_params=pltpu.CompilerParams(dimension_semantics=("arbitrary",), vmem_limit_bytes=VMEM_LIMIT),
        name="decode_finish",
    )(*[_arr_of(a) for a in finish_ins])
    new_states = (_batch_major(wkv1), p_a[:, :A_SHIFT_W], _batch_major(ret1), lru_h1,
                  jnp.transpose(lru_conv1, (1, 0, 2)), _batch_major(gdn1), jnp.transpose(gdn_conv1, (1, 0, 2)))
    return (o_a, o_b, o_c, o_d), new_states


def _prompt_layer(p_a, p_b, p_c, p_d, n_seq, seq_len, cos_t, sin_t, lp):
    o_a, wkv1, shift1 = _rwkv_call(p_a, n_seq, seq_len, lp)
    o_b, ret1 = _ret_call(p_b, n_seq, seq_len, cos_t, sin_t)
    o_c, lru_h1, lru_conv1 = _lru_call(p_c, n_seq, seq_len, lp)
    o_d, gdn1, gdn_conv1 = _gdn_call(p_d, n_seq, seq_len, lp)
    return (o_a, o_b, o_c, o_d), (wkv1, shift1, ret1, lru_h1, lru_conv1, gdn1, gdn_conv1)


def _run_group(x, mods, pos, carried, layers, final_g):
    n_seq, seq_len, d = x.shape
    x2 = x.reshape(n_seq * seq_len, d)
    cos_t, sin_t = _rope_tables(pos, tile_paired=carried is None)
    new = []
    n_layers = len(layers)
    for l, lp in enumerate(layers):
        w_qk = lp["w_qk_tiles"] if carried is None else lp["w_qk"]
        p_a, p_b, p_c, p_d = _inproj_call(x2, mods, lp["norm_g"], l, lp["w_t"], lp["w_ba"], w_qk, seq_len)
        if carried is None:
            branches, st = _prompt_layer(p_a, p_b, p_c, p_d, n_seq, seq_len, cos_t, sin_t, lp)
        else:
            branches, st = _decode_layer(l, p_a, p_b, p_c, p_d, carried, cos_t, sin_t, lp)
        new.append(st)
        x2 = _outproj_call(x2, mods, lp["norm_g"], branches, l, lp["w_t"], lp["w_up_bf16"],
                           lp["w_out_bf16"], final_g, seq_len, final=(l == n_layers - 1))
    stacked = tuple(jnp.stack([s[i] for s in new], axis=0) for i in range(7))
    return x2.reshape(n_seq, seq_len, d), stacked


def kernel(x_prompt, x_sample, c_prompt, c_sample, state_rwkv_wkv, state_rwkv_shift, state_ret, state_lru_h, state_lru_conv, state_gdn, state_gdn_conv, ada_w, ada_b, norm_g, w_in, rwkv_mu, rwkv_w0, rwkv_w2, rwkv_a0, rwkv_a2, rwkv_k_k, rwkv_k_a, rwkv_r_k, rwkv_ln_g, rwkv_ln_b, lru_conv_w, lru_conv_b, lru_gate_w, lru_gate_b, lru_lambda, gdn_conv_w, gdn_A_log, gdn_dt_bias, gdn_norm_g, w_up, w_out, final_g):
    n_layers = ada_w.shape[0]
    n_prompt, seq_len, _ = x_prompt.shape
    n_sample, dec_len, _ = x_sample.shape
    assert dec_len == 1, "the decode path handles one token per sequence"
    w_in_t = jnp.swapaxes(w_in, 1, 2).astype(BF16)
    off_ba = A_W + B_W + C_W + D_QKV_W
    rows = lambda a: a.reshape(n_layers, 1, -1)
    per_tile = LANES // HEAD_DIM
    w_qk_tiles = w_in_t[:, A_W:A_W + 2 * W_MIX].reshape(
        n_layers, 2, W_MIX // LANES, per_tile, 2, HEAD_DIM // 2, -1).transpose(0, 1, 2, 4, 3, 5, 6).reshape(
        n_layers, 2 * W_MIX, -1)
    stacked = dict(
        norm_g=rows(norm_g), w_up_bf16=w_up.astype(BF16), w_out_bf16=w_out.astype(BF16),
        w_ba=jnp.repeat(w_in_t[:, off_ba:off_ba + 2 * N_HEADS], HEAD_DIM, axis=1),
        w_qk=w_in_t[:, A_W:A_W + 2 * W_MIX],
        w_qk_tiles=w_qk_tiles,
        rwkv_mu=rows(rwkv_mu), rwkv_w0=rows(rwkv_w0), rwkv_w2=rwkv_w2, rwkv_a0=rows(rwkv_a0), rwkv_a2=rwkv_a2,
        rwkv_k_k=rows(rwkv_k_k), rwkv_k_a=rows(rwkv_k_a), rwkv_r_k=rows(rwkv_r_k), rwkv_ln_g=rows(rwkv_ln_g),
        rwkv_ln_b=rows(rwkv_ln_b), lru_conv_w=lru_conv_w, lru_conv_b=rows(lru_conv_b),
        lru_gate_w=_block_diag_gates(lru_gate_w), lru_gate_b=rows(lru_gate_b), lru_lambda=rows(lru_lambda),
        gdn_conv_w=gdn_conv_w, gdn_A_log=rows(jnp.repeat(gdn_A_log, HEAD_DIM, axis=1)),
        gdn_dt_bias=rows(jnp.repeat(gdn_dt_bias, HEAD_DIM, axis=1)),
        gdn_norm_g=rows(jnp.tile(gdn_norm_g, (1, N_HEADS))))
    layers = [dict({name: _LayerParam(arr, l) for name, arr in stacked.items()}, w_t=w_in_t)
              for l in range(n_layers)]
    mods_p, mods_s = _ada_call(c_prompt, c_sample, ada_w, ada_b)

    y_prompt, new_p = _run_group(x_prompt, mods_p, jnp.arange(seq_len, dtype=jnp.int32), None, layers, final_g)
    carried = dict(wkv=_batch_minor(state_rwkv_wkv), ret=_batch_minor(state_ret), gdn=_batch_minor(state_gdn),
                   shift=state_rwkv_shift, lru_h=state_lru_h,
                   lru_conv=jnp.transpose(state_lru_conv, (0, 2, 1, 3)),
                   gdn_conv=jnp.transpose(state_gdn_conv, (0, 2, 1, 3)))
    pos_s = PAST_LEN + jnp.arange(dec_len, dtype=jnp.int32)
    y_sample, new_s = _run_group(x_sample, mods_s, pos_s, carried, layers, final_g)
    return (y_prompt, y_sample) + new_p + new_s
```

```python
import functools
import math

import jax
import jax.numpy as jnp
from jax import lax
from jax.experimental import pallas as pl
from jax.experimental.pallas import tpu as pltpu

F32 = jnp.float32
BF16 = jnp.bfloat16
HI = lax.Precision.HIGHEST

N_HEADS = 4
HEAD_DIM = 64
W_MIX = N_HEADS * HEAD_DIM
LORA = 64
CONV_W = 4
N_BRANCH = 4
LRU_C = 8.0
ROPE_BASE = 10000.0
EPS = 1e-6
RWKV_GN_EPS = 64e-5
PAST_LEN = 16384
A_SHIFT_W = 3 * W_MIX + 2 * LORA
A_W = A_SHIFT_W + W_MIX
B_W = 4 * W_MIX
C_W = 2 * W_MIX
D_QKV_W = 3 * W_MIX
D_W = D_QKV_W + 2 * N_HEADS + W_MIX
D_PACK_W = D_QKV_W + 3 * W_MIX

SUBLANES = 8
LANES = 128
VMEM_LIMIT = 56 * 1024 * 1024

CHUNK = 64
RET_CHUNK = 128
INV_BLOCK = 16
WAVE = 16
ROW_TILE = 1024
LRU_TILE = 512
PROJ_TILE = 512
OUT_TILE = 1024
N_VEC_T = 14
N_VEC_PLAIN = 6


def _mm_nt(a, b, prec=HI):
    return lax.dot_general(a, b, (((1,), (1,)), ((), ())), precision=prec, preferred_element_type=F32)


_NN = (((1,), (0,)), ((), ()))
_NT = (((1,), (1,)), ((), ()))
_TN = (((0,), (0,)), ((), ()))

P_MISC = 1
HEAD_SUM_PIECES = 1
CUMSUM_PIECES = 2


class _Split:
    def __init__(self, x, passes):
        self.hi = x.astype(BF16)
        self.lo = (x - self.hi.astype(F32)).astype(BF16) if passes > 1 else None


def _dotp(a, b, dims=_NN, passes=1):
    a = a if isinstance(a, _Split) else _Split(a, passes)
    b = b if isinstance(b, _Split) else _Split(b, passes)
    d = lambda x, y: lax.dot_general(x, y, dims, preferred_element_type=F32)
    out = d(a.hi, b.hi)
    if passes > 1:
        out = out + (d(a.hi, b.lo) + d(a.lo, b.hi))
    return out


def _iota(shape, dim):
    return lax.broadcasted_iota(jnp.int32, shape, dim)


def _sigmoid(x):
    return 0.5 * jnp.tanh(0.5 * x) + 0.5


def _silu(x):
    return x * _sigmoid(x)


def _softplus(x):
    return jnp.maximum(x, 0.0) + jnp.log(1.0 + jnp.exp(-jnp.abs(x)))


def _pieces(x, n):
    out = []
    for i in range(n):
        p = x.astype(BF16)
        out.append(p)
        if i + 1 < n:
            x = x - p.astype(F32)
    return out


def _dot_const(x, const, dims=_NN, n=2, const_left=False):
    out = None
    for p in _pieces(x, n):
        t = lax.dot_general(*((const, p) if const_left else (p, const)), dims, preferred_element_type=F32)
        out = t if out is None else out + t
    return out


def _head_ones():
    return (_iota((W_MIX, W_MIX), 0) // HEAD_DIM == _iota((W_MIX, W_MIX), 1) // HEAD_DIM).astype(BF16)


def _head_sum(x, ones, signed=False):
    return _dot_const(x, ones, n=HEAD_SUM_PIECES + (1 if signed else 0))


def _rms(x):
    return x * lax.rsqrt(jnp.mean(x * x, axis=-1, keepdims=True) + EPS)


class _HeadAlgebra:
    def __init__(self, c):
        assert c == HEAD_DIM, "side-by-side head products need CHUNK == HEAD_DIM"
        w = W_MIX
        row, lane = _iota((c, w), 0), _iota((c, w), 1)
        col = lane % HEAD_DIM
        tile_lane = _iota((c, LANES), 1)
        self.tile_head = [tile_lane // HEAD_DIM == h for h in range(LANES // HEAD_DIM)]
        self.eye = (row == col).astype(F32)
        self.strict = row > col
        self.incl = row >= col
        self.inv_blk = (row // INV_BLOCK) == (col // INV_BLOCK)
        r2, c2 = _iota((w, w), 0), _iota((w, w), 1)
        self.eye_full = r2 == c2
        self.same_head = (r2 // HEAD_DIM) == (c2 // HEAD_DIM)

    def bd(self, y):
        yb = y.astype(BF16)
        zero = jnp.zeros((yb.shape[0], LANES), BF16)
        blocks = []
        for t in range(W_MIX // LANES):
            tile = yb[:, t * LANES:(t + 1) * LANES]
            for m in self.tile_head:
                kept = jnp.where(m, tile, zero)
                blocks.append(jnp.concatenate([kept if s == t else zero for s in range(W_MIX // LANES)], axis=1))
        return jnp.concatenate(blocks, axis=0)

    def nn(self, x, bd_y, out=F32):
        return lax.dot_general(x.astype(BF16), bd_y, _NN, preferred_element_type=F32).astype(out)

    def nt(self, x, bd_y):
        return lax.dot_general(x.astype(BF16), bd_y, _NT, preferred_element_type=F32)

    def tn_bd(self, x, y):
        full = lax.dot_general(x.astype(BF16), y.astype(BF16), _TN, preferred_element_type=F32)
        return jnp.where(self.same_head, full, 0.0)

    def diag_bd(self, row_vec):
        return jnp.where(self.eye_full, row_vec, 0.0)

    def plus_eye(self, bd_p):
        return jnp.where(self.eye_full, jnp.ones_like(bd_p), bd_p)

    def inv_unit_lower_many(self, mats):
        mats = [a.astype(BF16) for a in mats]
        zero = jnp.zeros_like(mats[0])
        eye = self.eye.astype(BF16)
        d = [jnp.where(self.inv_blk, a, zero) for a in mats]
        nb = [jnp.where(self.inv_blk, zero, a) for a in mats]
        td = [eye - di for di in d]
        p = d
        bdp = [self.bd(pi) for pi in p]
        for _ in range(int(math.log2(INV_BLOCK)) - 1):
            p = [self.nn(pi, bi, BF16) for pi, bi in zip(p, bdp)]
            bdp = [self.bd(pi) for pi in p]
            td = [self.nn(ti, self.plus_eye(bi), BF16) for ti, bi in zip(td, bdp)]
        bd_td = [self.bd(ti) for ti in td]
        x = [self.nn(ti, self.bd(ni), BF16) for ti, ni in zip(td, nb)]
        t = [eye - xi for xi in x]
        p = x
        bdp = [self.bd(pi) for pi in p]
        for _ in range(int(math.log2(HEAD_DIM // INV_BLOCK)) - 1):
            p = [self.nn(pi, bi, BF16) for pi, bi in zip(p, bdp)]
            bdp = [self.bd(pi) for pi in p]
            t = [self.nn(ti, self.plus_eye(bi), BF16) for ti, bi in zip(t, bdp)]
        return [self.nn(ti, bi, BF16) for ti, bi in zip(t, bd_td)]


def _ada_kernel(cp_ref, cs_ref, w_ref, b_ref, op_ref, os_ref):
    w = _Split(w_ref[...], 3)
    op_ref[...] = _dotp(_silu(cp_ref[...]), w, _NN, 3) + b_ref[...]
    os_ref[...] = _dotp(_silu(cs_ref[...]), w, _NN, 3) + b_ref[...]


def _ada_call(c_prompt, c_sample, ada_w, ada_b):
    n_layers, d, d3 = ada_w.shape
    n_p, n_s = c_prompt.shape[0], c_sample.shape[0]
    return pl.pallas_call(
        _ada_kernel,
        grid=(n_layers, d3 // d),
        in_specs=[
            pl.BlockSpec((n_p, d), lambda l, j: (0, 0)),
            pl.BlockSpec((n_s, d), lambda l, j: (0, 0)),
            pl.BlockSpec((None, d, d), lambda l, j: (l, 0, j)),
            pl.BlockSpec((None, 1, d), lambda l, j: (l, 0, j)),
        ],
        out_specs=[pl.BlockSpec((None, n_p, d), lambda l, j: (l, 0, j)),
                   pl.BlockSpec((None, n_s, d), lambda l, j: (l, 0, j))],
        out_shape=[jax.ShapeDtypeStruct((n_layers, n_p, d3), F32), jax.ShapeDtypeStruct((n_layers, n_s, d3), F32)],
        compiler_params=pltpu.CompilerParams(dimension_semantics=("arbitrary", "arbitrary"),
                                             vmem_limit_bytes=VMEM_LIMIT),
        name="ada_mod",
    )(c_prompt, c_sample, ada_w, ada_b.reshape(n_layers, 1, d3))


def _modulated_norm(x, g, scale, shift):
    return _rms(x) * g * (1.0 + scale) + shift


def _inproj_kernel(x_ref, sc_ref, sh_ref, g_ref, w_ref, wz_ref, wba_ref, wqk_ref, oa_ref, ob_ref, oc_ref, od_ref):
    h = _modulated_norm(x_ref[...], g_ref[...], sc_ref[...], sh_ref[...]).astype(BF16)
    proj = lambda w: lax.dot_general(h, w, _NT, preferred_element_type=F32)
    oa_ref[...] = proj(w_ref[0, 0:A_W, :])
    ob_ref[:, :2 * W_MIX] = proj(wqk_ref[...])
    ob_ref[:, 2 * W_MIX:] = proj(w_ref[0, A_W + 2 * W_MIX:A_W + B_W, :])
    oc_ref[...] = proj(w_ref[0, A_W + B_W:A_W + B_W + C_W, :])
    lo = A_W + B_W + C_W
    od_ref[:, :D_QKV_W] = proj(w_ref[0, lo:lo + D_QKV_W, :])
    od_ref[:, D_QKV_W:D_QKV_W + W_MIX] = proj(wz_ref[0])
    od_ref[:, D_QKV_W + W_MIX:] = proj(wba_ref[...])


def _weight_rows(layer, row0, n_rows, d):
    return pl.BlockSpec((pl.Element(1), pl.Element(n_rows), pl.Element(d)), lambda i: (layer, row0, 0))


MOD_SHIFT, MOD_SCALE, MOD_GATE = 0, 1, 2


def _mod_specs(mods, layer, parts, d, tm, seq_len):
    if seq_len == 1:
        return [mods] * len(parts), [pl.BlockSpec((None, tm, d), lambda i, c=c: (layer, i, c)) for c in parts]
    per_seq = seq_len // tm
    by_seq = mods.reshape(mods.shape[0], mods.shape[1], 1, mods.shape[2])
    return ([by_seq] * len(parts),
            [pl.BlockSpec((None, None, 1, d), lambda i, c=c: (layer, i // per_seq, 0, c)) for c in parts])


class _LayerParam:
    def __init__(self, arr, layer):
        self.arr, self.layer = arr, layer

    def spec(self):
        return pl.BlockSpec((None,) + self.arr.shape[1:], lambda *_: (self.layer,) + (0,) * (self.arr.ndim - 1))


def _spec_of(a):
    if isinstance(a, _LayerParam):
        return a.spec()
    return pl.BlockSpec(a.shape, lambda *_: (0,) * a.ndim)


def _arr_of(a):
    return a.arr if isinstance(a, _LayerParam) else a


def _inproj_call(x2, mods_all, g, l, w_t, w_ba, w_qk, seq_len):
    m, d = x2.shape
    tm = min(PROJ_TILE, m, seq_len) if seq_len > 1 else m
    widths = (A_W, B_W, C_W, D_PACK_W)
    mods, mod_specs = _mod_specs(mods_all, l, (MOD_SCALE, MOD_SHIFT), d, tm, seq_len)
    off_ba = A_W + B_W + C_W + D_QKV_W
    off_z = off_ba + 2 * N_HEADS
    return pl.pallas_call(
        _inproj_kernel,
        grid=(m // tm,),
        in_specs=[pl.BlockSpec((tm, d), lambda i: (i, 0))] + mod_specs + [
            _spec_of(g),
            _weight_rows(l, 0, off_ba, d),
            _weight_rows(l, off_z, W_MIX, d),
            _spec_of(w_ba),
            _spec_of(w_qk),
        ],
        out_specs=[pl.BlockSpec((tm, wd), lambda i: (i, 0)) for wd in widths],
        out_shape=[jax.ShapeDtypeStruct((m, wd), F32) for wd in widths],
        compiler_params=pltpu.CompilerParams(dimension_semantics=("parallel",), vmem_limit_bytes=VMEM_LIMIT),
        name="in_proj",
    )(x2, *mods, _arr_of(g), w_t, w_t, _arr_of(w_ba), _arr_of(w_qk))


def _outproj_kernel(x_ref, sc_ref, sh_ref, gt_ref, g_ref, ba_ref, bb_ref, bc_ref, bd_ref,
                    wg_ref, wup_ref, wout_ref, fg_ref, o_ref, *, final):
    x = x_ref[...]
    d = x.shape[-1]
    h = _modulated_norm(x, g_ref[...], sc_ref[...], sh_ref[...]).astype(BF16)
    merged = jnp.zeros(x.shape, F32)
    for n, br_ref in enumerate((ba_ref, bb_ref, bc_ref, bd_ref)):
        gl = lax.dot_general(h, wg_ref[0, n * d:(n + 1) * d, :], _NT, preferred_element_type=F32)
        up = jnp.dot(br_ref[...].astype(BF16), wup_ref[n], preferred_element_type=F32)
        merged = merged + _sigmoid(gl) * up
    out = jnp.dot(merged.astype(BF16), wout_ref[...], preferred_element_type=F32)
    xn = x + gt_ref[...] * out
    if final:
        xn = _rms(xn) * fg_ref[...]
    o_ref[...] = xn


def _outproj_call(x2, mods_all, g, branches, l, w_t, wup, wout, final_g, seq_len, final):
    m, d = x2.shape
    tm = min(OUT_TILE, m, seq_len) if seq_len > 1 else m
    mods, mod_specs = _mod_specs(mods_all, l, (MOD_SCALE, MOD_SHIFT, MOD_GATE), d, tm, seq_len)
    off_g = A_W + B_W + C_W + D_W
    gate_rows = _weight_rows(l, off_g, N_BRANCH * d, d)
    return pl.pallas_call(
        functools.partial(_outproj_kernel, final=final),
        grid=(m // tm,),
        in_specs=[pl.BlockSpec((tm, d), lambda i: (i, 0))] + mod_specs + [_spec_of(g)]
        + [pl.BlockSpec((tm, W_MIX), lambda i: (i, 0)) for _ in branches]
        + [gate_rows, _spec_of(wup), _spec_of(wout), pl.BlockSpec((1, d), lambda i: (0, 0))],
        out_specs=pl.BlockSpec((tm, d), lambda i: (i, 0)),
        out_shape=jax.ShapeDtypeStruct((m, d), F32),
        compiler_params=pltpu.CompilerParams(dimension_semantics=("parallel",), vmem_limit_bytes=VMEM_LIMIT),
        name="out_proj",
    )(x2, *mods, _arr_of(g), *branches, w_t, _arr_of(wup), _arr_of(wout), final_g.reshape(1, d))


def _rwkv_token_math(pm, w0, w2, a0, a2, k_k, k_a, ones):
    r = pm[:, 0:W_MIX]
    k = pm[:, W_MIX:2 * W_MIX]
    v = pm[:, 2 * W_MIX:3 * W_MIX]
    wd = pm[:, 3 * W_MIX:3 * W_MIX + LORA]
    ad = pm[:, 3 * W_MIX + LORA:]
    w_log = -_softplus(-(w0 + _dotp(jnp.tanh(wd), w2, _NN, P_MISC))) - 0.5
    log_decay = -jnp.exp(w_log)
    a = _sigmoid(a0 + _dotp(ad, a2, _NN, P_MISC))
    kx = k * k_k
    kk = kx * lax.rsqrt(_head_sum(kx * kx, ones) + EPS)
    k = k * (1.0 + (a - 1.0) * k_a)
    return r, k, v, log_decay, -kk, kk * a


def _rwkv_finish(o, r, k, v, z, r_k, ln_g, ln_b, ones):
    mean = _head_sum(o, ones) * (1.0 / HEAD_DIM)
    dlt = o - mean
    var = _head_sum(dlt * dlt, ones) * (1.0 / HEAD_DIM)
    on = dlt * lax.rsqrt(var + RWKV_GN_EPS) * ln_g + ln_b
    bonus = _head_sum(r * k * r_k, ones, signed=True) * v
    return (on + bonus) * _silu(z)


def _swap_halves(x):
    half = HEAD_DIM // 2
    n = x.shape[-1]
    first = (_iota(x.shape, 1) & half) == 0
    return jnp.where(first, pltpu.roll(x, n - half, axis=1), pltpu.roll(x, half, axis=1))


def _rotary(x, cos, sin):
    return x * cos + _swap_halves(x) * sin


def _lru_token_math(xc, gate_w, gate_b, lam):
    gates = _dotp(xc, gate_w, _NN, P_MISC) + gate_b
    r_gate = _sigmoid(gates[:, :W_MIX])
    i_gate = _sigmoid(gates[:, W_MIX:])
    log_a = -LRU_C * r_gate * _softplus(-lam)
    a = jnp.exp(log_a)
    b = jnp.sqrt(1.0 - jnp.exp(2.0 * log_a)) * (i_gate * xc)
    return a, b


def _gdn_token_math(qkv, b_raw, a_raw, a_log, dt_bias, ones):
    qkv = _silu(qkv)
    q = qkv[:, 0:W_MIX]
    k = qkv[:, W_MIX:2 * W_MIX]
    v = qkv[:, 2 * W_MIX:]
    q = q * lax.rsqrt(_head_sum(q * q, ones) + EPS) * (HEAD_DIM ** -0.5)
    k = k * lax.rsqrt(_head_sum(k * k, ones) + EPS)
    beta = _sigmoid(b_raw)
    g = -jnp.exp(a_log) * _softplus(a_raw + dt_bias)
    return q, k, v, beta, g


def _head_rms_finish(o, z, ones, gain=None):
    y = o * lax.rsqrt(_head_sum(o * o, ones) * (1.0 / HEAD_DIM) + EPS)
    if gain is not None:
        y = y * gain
    return y * _silu(z)


def _conv_tile(u, ext_ref, w_ref, first):
    n = u.shape[0]

    @pl.when(first)
    def _():
        ext_ref[0:SUBLANES, :] = jnp.zeros((SUBLANES, u.shape[1]), F32)

    ext_ref[SUBLANES:SUBLANES + n, :] = u
    out = None
    for j in range(CONV_W):
        term = _rows_back(u, ext_ref, CONV_W - 1 - j) * w_ref[j:j + 1, :]
        out = term if out is None else out + term
    ext_ref[0:SUBLANES, :] = u[n - SUBLANES:n, :]
    return out


def _rows_back(u, ext_ref, back):
    if back == 0:
        return u
    n, ch = u.shape
    tiles = (n // SUBLANES, SUBLANES, ch)
    pos = _iota((1, SUBLANES, 1), 1)
    earlier = ext_ref[0:n, :].reshape(tiles)
    return pltpu.roll(jnp.where(pos >= SUBLANES - back, earlier, u.reshape(tiles)), back, axis=1).reshape(n, ch)


def _rwkv_kernel(p_ref, mu_ref, w0_ref, w2_ref, a0_ref, a2_ref, kk_ref, ka_ref, rk_ref, lng_ref, lnb_ref,
                 o_ref, s_out_ref, shift_out_ref, s_scr, ext_scr):
    j = pl.program_id(1)
    last = pl.num_programs(1) - 1
    ct = p_ref.shape[0]

    @pl.when(j == 0)
    def _():
        s_scr[...] = jnp.zeros(s_scr.shape, F32)
        ext_scr[0:SUBLANES, :] = jnp.zeros((SUBLANES, A_SHIFT_W), F32)

    p = p_ref[...]
    pa = p[:, :A_SHIFT_W]
    z = p[:, A_SHIFT_W:]
    ext_scr[SUBLANES:SUBLANES + ct, :] = pa
    prev = _rows_back(pa, ext_scr, 1)
    ext_scr[0:SUBLANES, :] = pa[ct - SUBLANES:ct, :]
    pm = pa + (prev - pa) * mu_ref[...]
    ones = _head_ones()
    r, k, v, ld, av, bv = _rwkv_token_math(pm, w0_ref[...], w2_ref[...], a0_ref[...], a2_ref[...],
                                           kk_ref[...], ka_ref[...], ones)

    c = min(CHUNK, ct)
    ha = _HeadAlgebra(c)
    lt = (_iota((c, c), 0) >= _iota((c, c), 1)).astype(BF16)
    units = []
    for c0 in range(0, ct, c):
        sl = slice(c0, c0 + c)
        ldc = ld[sl]
        cum = _dot_const(ldc, lt, _NN, CUMSUM_PIECES, const_left=True)
        e_neg = jnp.exp(-cum)
        e_out = jnp.exp(cum[c - 1:c, :] - cum)
        units.append(dict(a=av[sl] * jnp.exp(cum - ldc), r=r[sl] * jnp.exp(cum), b=bv[sl] * e_neg, k=k[sl] * e_neg,
                          bo=bv[sl] * e_out, ko=k[sl] * e_out, v=v[sl], g=jnp.exp(cum[c - 1:c, :])))
    all_units = units
    state = s_scr[...]
    o_rows = []
    for w0 in range(0, len(all_units), WAVE):
        units = all_units[w0:w0 + WAVE]
        for u in units:
            lhs = jnp.concatenate([u["a"], u["r"]], axis=0)
            u["mb"] = ha.nt(lhs, ha.bd(u["b"]))
            u["mk"] = ha.nt(lhs, ha.bd(u["k"]))
            u["bd_v"] = ha.bd(u["v"])
        for u in units:
            u["m_ab"] = jnp.where(ha.strict, u["mb"][:c], 0.0)
            u["m_ak"] = jnp.where(ha.strict, u["mk"][:c], 0.0)
            u["m_rb"] = jnp.where(ha.incl, u["mb"][c:], 0.0)
            u["m_rk"] = jnp.where(ha.incl, u["mk"][c:], 0.0)
        for u, t_inv in zip(units, ha.inv_unit_lower_many([-u["m_ab"] for u in units])):
            u["t_inv"] = t_inv
        for u in units:
            u["makv"] = ha.nn(u["m_ak"], u["bd_v"], BF16)
        for u in units:
            u["a_hat"] = ha.nn(u["t_inv"], ha.bd(u["a"]), BF16)
            u["u1"] = ha.nn(u["t_inv"], ha.bd(u["makv"]), BF16)
        for u in units:
            u["r_hat"] = u["r"] + ha.nn(u["m_rb"], ha.bd(u["a_hat"]))
            u["o1"] = ha.nn(u["m_rb"], ha.bd(u["u1"])) + ha.nn(u["m_rk"], u["bd_v"])
            u["g_t"] = ha.diag_bd(u["g"]) + ha.tn_bd(u["bo"], u["a_hat"])
            u["h_t"] = ha.tn_bd(jnp.concatenate([u["bo"], u["ko"]], axis=0),
                                jnp.concatenate([u["u1"], u["v"].astype(BF16)], axis=0))
            zz = lax.dot_general(jnp.concatenate([u["r_hat"], u["g_t"]], axis=0).astype(BF16), state.astype(BF16),
                                 _NN, preferred_element_type=F32)
            o_rows.append(zz[:c] + u["o1"])
            state = zz[c:] + u["h_t"]
    o = o_rows[0] if len(o_rows) == 1 else jnp.concatenate(o_rows, axis=0)
    s_scr[...] = state
    o_ref[...] = _rwkv_finish(o, r, k, v, z, rk_ref[...], lng_ref[...], lnb_ref[...], ones)

    @pl.when(j == last)
    def _():
        eye_h = (_iota((HEAD_DIM, HEAD_DIM), 0) == _iota((HEAD_DIM, HEAD_DIM), 1)).astype(F32)
        for h in range(N_HEADS):
            hs = slice(h * HEAD_DIM, (h + 1) * HEAD_DIM)
            s_out_ref[h] = _mm_nt(eye_h, state[hs, hs])
        shift_out_ref[...] = pa[ct - 1:ct, :]


def _row(a):
    return a if isinstance(a, _LayerParam) else a.reshape(1, -1)


def _rwkv_call(p_a, n_seq, seq_len, lp):
    ct = min(ROW_TILE, seq_len)
    p3 = p_a.reshape(n_seq, seq_len, A_W)
    params = [_row(lp["rwkv_mu"]), _row(lp["rwkv_w0"]), lp["rwkv_w2"], _row(lp["rwkv_a0"]), lp["rwkv_a2"],
              _row(lp["rwkv_k_k"]), _row(lp["rwkv_k_a"]), _row(lp["rwkv_r_k"]), _row(lp["rwkv_ln_g"]),
              _row(lp["rwkv_ln_b"])]
    o, s1, shift1 = pl.pallas_call(
        _rwkv_kernel,
        grid=(n_seq, seq_len // ct),
        in_specs=[pl.BlockSpec((None, ct, A_W), lambda b, j: (b, j, 0))]
        + [_spec_of(a) for a in params],
        out_specs=[
            pl.BlockSpec((None, ct, W_MIX), lambda b, j: (b, j, 0)),
            pl.BlockSpec((None, N_HEADS, HEAD_DIM, HEAD_DIM), lambda b, j: (b, 0, 0, 0)),
            pl.BlockSpec((None, 1, A_SHIFT_W), lambda b, j: (b, 0, 0)),
        ],
        out_shape=[
            jax.ShapeDtypeStruct((n_seq, seq_len, W_MIX), F32),
            jax.ShapeDtypeStruct((n_seq, N_HEADS, HEAD_DIM, HEAD_DIM), F32),
            jax.ShapeDtypeStruct((n_seq, 1, A_SHIFT_W), F32),
        ],
        scratch_shapes=[pltpu.VMEM((W_MIX, W_MIX), F32), pltpu.VMEM((ct + SUBLANES, A_SHIFT_W), F32)],
        compiler_params=pltpu.CompilerParams(dimension_semantics=("parallel", "arbitrary"),
                                             vmem_limit_bytes=VMEM_LIMIT),
        name="rwkv7_prompt",
    )(p3, *[_arr_of(a) for a in params])
    return o.reshape(n_seq * seq_len, W_MIX), s1, shift1.reshape(n_seq, A_SHIFT_W)


def _ret_kernel(p_ref, cos_ref, sin_ref, o_ref, s_out_ref, s_scr):
    j = pl.program_id(1)
    last = pl.num_programs(1) - 1
    ct = p_ref.shape[0]

    @pl.when(j == 0)
    def _():
        s_scr[...] = jnp.zeros(s_scr.shape, F32)

    p = p_ref[...]
    cos, sin = cos_ref[...], sin_ref[...]
    v = p[:, 2 * W_MIX:3 * W_MIX]
    z = p[:, 3 * W_MIX:]
    c = min(RET_CHUNK, ct)
    ri, ci = _iota((c, c), 0), _iota((c, c), 1)
    causal = ri >= ci
    rel = jnp.where(causal, ri - ci, 0).astype(F32)
    idx = _iota((c, 1), 0).astype(F32)
    half, per_tile = HEAD_DIM // 2, LANES // HEAD_DIM
    lane = _iota((1, LANES), 1)
    qk_head = [(lane // half) % per_tile == hh for hh in range(per_tile)]
    v_head = [lane // HEAD_DIM == hh for hh in range(per_tile)]
    row_head = (_iota((LANES, 1), 0) // half) % per_tile
    n_tiles = W_MIX // LANES
    o_tiles = []
    final_states = []
    for t in range(n_tiles):
        ts = slice(t * LANES, (t + 1) * LANES)
        cos_t, sin_t = cos[:, ts], sin[:, ts]
        q_t = p[:, ts]
        k_t = p[:, W_MIX + t * LANES:W_MIX + (t + 1) * LANES]
        q_t = q_t * cos_t + pltpu.roll(q_t, LANES // 2, axis=1) * sin_t
        k_t = (k_t * cos_t + pltpu.roll(k_t, LANES // 2, axis=1) * sin_t) * (HEAD_DIM ** -0.5)
        v_t = v[:, ts]
        heads = []
        for hh in range(per_tile):
            lg = math.log(1.0 - 2.0 ** (-5.0 - (t * per_tile + hh)))
            heads.append(dict(decay=jnp.where(causal, jnp.exp(lg * rel), 0.0), q_dec=jnp.exp(lg * (idx + 1.0)),
                              k_dec=jnp.exp(lg * (c - 1.0 - idx)), g_c=math.exp(lg * c)))
        g_rows = jnp.where(row_head == 0, heads[0]["g_c"], heads[1]["g_c"])
        q_dec = jnp.where(v_head[0], heads[0]["q_dec"], heads[1]["q_dec"])
        units = []
        for c0 in range(0, ct, c):
            sl = slice(c0, c0 + c)
            units.append(dict(q=_Split(q_t[sl], 1), k=k_t[sl], v=v_t[sl]))
        for u in units:
            u["km"] = [jnp.where(m, u["k"], 0.0) for m in qk_head]
            u["vm"] = [_Split(jnp.where(m, u["v"], 0.0), 1) for m in v_head]
            u["s_in"] = [_dotp(u["q"], km, _NT, P_MISC) * hd["decay"] for km, hd in zip(u["km"], heads)]
            kv = [_dotp(km * hd["k_dec"], vm, _TN, P_MISC) for km, vm, hd in zip(u["km"], u["vm"], heads)]
            u["kv"] = kv[0] + kv[1]
        for u in units:
            o_in = [_dotp(s_in, vm, _NN, P_MISC) for s_in, vm in zip(u["s_in"], u["vm"])]
            u["o"] = o_in[0] + o_in[1]
        state = s_scr[t]
        for u in units:
            u["s0"] = state
            state = state * g_rows + u["kv"]
        rows = [u["o"] + _dotp(u["q"], u["s0"], _NN, P_MISC) * q_dec for u in units]
        o_tiles.append(rows[0] if len(rows) == 1 else jnp.concatenate(rows, axis=0))
        s_scr[t] = state
        final_states.append(state)
    o = jnp.concatenate(o_tiles, axis=1)
    o_ref[...] = _head_rms_finish(o, z, _head_ones())

    @pl.when(j == last)
    def _():
        for t in range(n_tiles):
            for hh in range(per_tile):
                r0 = hh * half
                cols = slice(hh * HEAD_DIM, (hh + 1) * HEAD_DIM)
                st = final_states[t]
                s_out_ref[t * per_tile + hh] = jnp.concatenate(
                    [st[r0:r0 + half, cols], st[LANES // 2 + r0:LANES // 2 + r0 + half, cols]], axis=0)


def _rope_tables(pos, tile_paired=False):
    half = HEAD_DIM // 2
    inv = ROPE_BASE ** (-jnp.arange(half, dtype=F32) / half)
    ang = pos.astype(F32)[:, None] * inv[None, :]
    cos, sin = jnp.cos(ang), jnp.sin(ang)
    cos_t = jnp.tile(jnp.concatenate([cos, cos], axis=-1), (1, N_HEADS))
    if tile_paired:
        per_tile = LANES // HEAD_DIM
        sin_t = jnp.tile(jnp.concatenate([-sin] * per_tile + [sin] * per_tile, axis=-1), (1, W_MIX // LANES))
    else:
        sin_t = jnp.tile(jnp.concatenate([-sin, sin], axis=-1), (1, N_HEADS))
    return cos_t, sin_t


def _ret_call(p_b, n_seq, seq_len, cos_t, sin_t):
    ct = min(ROW_TILE, seq_len)
    p3 = p_b.reshape(n_seq, seq_len, B_W)
    o, s1 = pl.pallas_call(
        _ret_kernel,
        grid=(n_seq, seq_len // ct),
        in_specs=[
            pl.BlockSpec((None, ct, B_W), lambda b, j: (b, j, 0)),
            pl.BlockSpec((ct, W_MIX), lambda b, j: (j, 0)),
            pl.BlockSpec((ct, W_MIX), lambda b, j: (j, 0)),
        ],
        out_specs=[
            pl.BlockSpec((None, ct, W_MIX), lambda b, j: (b, j, 0)),
            pl.BlockSpec((None, N_HEADS, HEAD_DIM, HEAD_DIM), lambda b, j: (b, 0, 0, 0)),
        ],
        out_shape=[
            jax.ShapeDtypeStruct((n_seq, seq_len, W_MIX), F32),
            jax.ShapeDtypeStruct((n_seq, N_HEADS, HEAD_DIM, HEAD_DIM), F32),
        ],
        scratch_shapes=[pltpu.VMEM((W_MIX // LANES, LANES, LANES), F32)],
        compiler_params=pltpu.CompilerParams(dimension_semantics=("parallel", "arbitrary"),
                                             vmem_limit_bytes=VMEM_LIMIT),
        name="retention_prompt",
    )(p3, cos_t, sin_t)
    return o.reshape(n_seq * seq_len, W_MIX), s1


def _affine_scan(a, b, span):
    n, w = a.shape
    if span == SUBLANES and n > span:
        shape, axis = (n // span, span, w), 1
        a, b = a.reshape(shape), b.reshape(shape)
        pos = _iota((1, span, 1), 1)
    else:
        assert span >= n
        axis = 0
        pos = _iota((n, 1), 0)
    dist = 1
    while dist < span:
        keep = pos >= dist
        a_prev = jnp.where(keep, pltpu.roll(a, dist, axis=axis), 1.0)
        b_prev = jnp.where(keep, pltpu.roll(b, dist, axis=axis), 0.0)
        b = a * b_prev + b
        a = a * a_prev
        dist *= 2
    return a.reshape(n, w), b.reshape(n, w)


def _lru_kernel(p_ref, cw_ref, cb_ref, gw_ref, gb_ref, sp_ref, o_ref, h_out_ref, conv_out_ref,
                ext_scr, h_scr, ab_scr, hin_scr):
    j = pl.program_id(1)
    last = pl.num_programs(1) - 1
    ct = p_ref.shape[0]

    @pl.when(j == 0)
    def _():
        h_scr[...] = jnp.zeros(h_scr.shape, F32)

    p = p_ref[...]
    xr = p[:, :W_MIX]
    z = p[:, W_MIX:]
    xc = _conv_tile(xr, ext_scr, cw_ref, j == 0) + cb_ref[...]
    a, b = _lru_token_math(xc, gw_ref[...], gb_ref[...], sp_ref[...])
    n_grp = ct // SUBLANES
    a, b = _affine_scan(a, b, SUBLANES)
    n_tiles = W_MIX // LANES
    for t in range(n_tiles):
        ab_scr[t] = a[:, t * LANES:(t + 1) * LANES]
        ab_scr[n_tiles + t] = b[:, t * LANES:(t + 1) * LANES]
    ends = pl.ds(SUBLANES - 1, n_grp, stride=SUBLANES)
    a_end = jnp.concatenate([ab_scr[t, ends, :] for t in range(n_tiles)], axis=1)
    b_end = jnp.concatenate([ab_scr[n_tiles + t, ends, :] for t in range(n_tiles)], axis=1)
    a_end, b_end = _affine_scan(a_end, b_end, n_grp)
    h_prev = h_scr[...]
    h_end = a_end * h_prev + b_end
    grp = _iota((n_grp, 1), 0)
    hin_scr[...] = jnp.where(grp == 0, h_prev, pltpu.roll(h_end, 1, axis=0))
    h_in = jnp.concatenate([jnp.broadcast_to(hin_scr[g:g + 1, :], (SUBLANES, W_MIX)) for g in range(n_grp)], axis=0)
    hcur = a * h_in + b
    h_scr[...] = h_end[n_grp - 1:n_grp, :]
    o_ref[...] = hcur * _silu(z)

    @pl.when(j == last)
    def _():
        h_out_ref[...] = hcur[ct - 1:ct, :]
        conv_out_ref[...] = xr[ct - SUBLANES:ct, :]


def _block_diag_gates(gate_w):
    out = jnp.zeros((gate_w.shape[0], W_MIX, 2 * W_MIX), F32)
    for g in range(2):
        for n in range(N_HEADS):
            out = out.at[:, n * HEAD_DIM:(n + 1) * HEAD_DIM,
                         g * W_MIX + n * HEAD_DIM:g * W_MIX + (n + 1) * HEAD_DIM].set(gate_w[:, g, n])
    return out


def _lru_params(lp):
    return [lp["lru_conv_w"], lp["lru_conv_b"], lp["lru_gate_w"], lp["lru_gate_b"], lp["lru_lambda"]]


def _lru_call(p_c, n_seq, seq_len, lp):
    ct = min(LRU_TILE, seq_len)
    p3 = p_c.reshape(n_seq, seq_len, C_W)
    params = _lru_params(lp)
    o, h1, conv_tail = pl.pallas_call(
        _lru_kernel,
        grid=(n_seq, seq_len // ct),
        in_specs=[pl.BlockSpec((None, ct, C_W), lambda b, j: (b, j, 0))]
        + [_spec_of(a) for a in params],
        out_specs=[
            pl.BlockSpec((None, ct, W_MIX), lambda b, j: (b, j, 0)),
            pl.BlockSpec((None, 1, W_MIX), lambda b, j: (b, 0, 0)),
            pl.BlockSpec((None, SUBLANES, W_MIX), lambda b, j: (b, 0, 0)),
        ],
        out_shape=[
            jax.ShapeDtypeStruct((n_seq, seq_len, W_MIX), F32),
            jax.ShapeDtypeStruct((n_seq, 1, W_MIX), F32),
            jax.ShapeDtypeStruct((n_seq, SUBLANES, W_MIX), F32),
        ],
        scratch_shapes=[pltpu.VMEM((ct + SUBLANES, W_MIX), F32), pltpu.VMEM((1, W_MIX), F32),
                        pltpu.VMEM((2 * W_MIX // LANES, ct, LANES), F32),
                        pltpu.VMEM((ct // SUBLANES, W_MIX), F32)],
        compiler_params=pltpu.CompilerParams(dimension_semantics=("parallel", "arbitrary"),
                                             vmem_limit_bytes=VMEM_LIMIT),
        name="rglru_prompt",
    )(p3, *[_arr_of(a) for a in params])
    return (o.reshape(n_seq * seq_len, W_MIX), h1.reshape(n_seq, W_MIX),
            conv_tail[:, SUBLANES - (CONV_W - 1):, :])


def _gdn_kernel(p_ref, cw_ref, nal_ref, dtb_ref, ng_ref, o_ref, s_out_ref, conv_out_ref, ext_scr, s_scr):
    j = pl.program_id(1)
    last = pl.num_programs(1) - 1
    ct = p_ref.shape[0]

    @pl.when(j == 0)
    def _():
        s_scr[...] = jnp.zeros(s_scr.shape, F32)

    p = p_ref[...]
    raw = p[:, :D_QKV_W]
    z = p[:, D_QKV_W:D_QKV_W + W_MIX]
    b_raw = p[:, D_QKV_W + W_MIX:D_QKV_W + 2 * W_MIX]
    a_raw = p[:, D_QKV_W + 2 * W_MIX:]
    ones = _head_ones()
    qkv = _conv_tile(raw, ext_scr, cw_ref, j == 0)
    q, k, v, beta, g = _gdn_token_math(qkv, b_raw, a_raw, nal_ref[...], dtb_ref[...], ones)

    c = min(CHUNK, ct)
    ha = _HeadAlgebra(c)
    lt = (_iota((c, c), 0) >= _iota((c, c), 1)).astype(BF16)
    units = []
    for c0 in range(0, ct, c):
        sl = slice(c0, c0 + c)
        gc = _dot_const(g[sl], lt, _NN, CUMSUM_PIECES, const_left=True)
        gc_cols = jnp.sum(gc * ha.eye, axis=0, keepdims=True)
        diff = gc - gc_cols
        decay = jnp.where(ha.incl, jnp.exp(jnp.where(ha.incl, diff, 0.0)), 0.0)
        kb = k[sl] * beta[sl]
        e_gc = jnp.exp(gc)
        g_last = gc[c - 1:c, :]
        units.append(dict(decay=decay, kb=kb, q=q[sl], k=k[sl], vb=v[sl] * beta[sl], kbe=kb * e_gc,
                          k_out=k[sl] * jnp.exp(g_last - gc), q_in=q[sl] * e_gc, e_last=jnp.exp(g_last)))
    all_units = units
    state = s_scr[...]
    o_rows = []
    for w0 in range(0, len(all_units), WAVE):
        units = all_units[w0:w0 + WAVE]
        for u in units:
            kq = ha.nt(jnp.concatenate([u["kb"], u["q"]], axis=0), ha.bd(u["k"]))
            u["a_mat"] = jnp.where(ha.strict, kq[:c] * u["decay"], 0.0)
            u["qk"] = kq[c:] * u["decay"]
        for u, t_inv in zip(units, ha.inv_unit_lower_many([u["a_mat"] for u in units])):
            u["t_inv"] = t_inv
        for u in units:
            u["u"] = ha.nn(u["t_inv"], ha.bd(u["vb"]), BF16)
            u["w"] = ha.nn(u["t_inv"], ha.bd(u["kbe"]), BF16)
        for u in units:
            u["g_mat"] = ha.diag_bd(u["e_last"]) - ha.tn_bd(u["k_out"], u["w"])
            u["h_mat"] = ha.tn_bd(u["k_out"], u["u"])
            u["q_hat"] = u["q_in"] - ha.nn(u["qk"], ha.bd(u["w"]))
            u["o1"] = ha.nn(u["qk"], ha.bd(u["u"]))
            zz = lax.dot_general(jnp.concatenate([u["q_hat"], u["g_mat"]], axis=0).astype(BF16),
                                 state.astype(BF16), _NN, preferred_element_type=F32)
            o_rows.append(zz[:c] + u["o1"])
            state = zz[c:] + u["h_mat"]
    o = o_rows[0] if len(o_rows) == 1 else jnp.concatenate(o_rows, axis=0)
    s_scr[...] = state
    o_ref[...] = _head_rms_finish(o, z, ones, ng_ref[...])

    @pl.when(j == last)
    def _():
        for h in range(N_HEADS):
            hs = slice(h * HEAD_DIM, (h + 1) * HEAD_DIM)
            s_out_ref[h] = state[hs, hs]
        conv_out_ref[...] = raw[ct - SUBLANES:ct, :]


def _gdn_params(lp):
    return [lp["gdn_conv_w"], lp["gdn_A_log"], lp["gdn_dt_bias"], lp["gdn_norm_g"]]


def _gdn_call(p_d, n_seq, seq_len, lp):
    ct = min(ROW_TILE, seq_len)
    p3 = p_d.reshape(n_seq, seq_len, D_PACK_W)
    params = _gdn_params(lp)
    o, s1, conv_tail = pl.pallas_call(
        _gdn_kernel,
        grid=(n_seq, seq_len // ct),
        in_specs=[pl.BlockSpec((None, ct, D_PACK_W), lambda b, j: (b, j, 0))]
        + [_spec_of(a) for a in params],
        out_specs=[
            pl.BlockSpec((None, ct, W_MIX), lambda b, j: (b, j, 0)),
            pl.BlockSpec((None, N_HEADS, HEAD_DIM, HEAD_DIM), lambda b, j: (b, 0, 0, 0)),
            pl.BlockSpec((None, SUBLANES, D_QKV_W), lambda b, j: (b, 0, 0)),
        ],
        out_shape=[
            jax.ShapeDtypeStruct((n_seq, seq_len, W_MIX), F32),
            jax.ShapeDtypeStruct((n_seq, N_HEADS, HEAD_DIM, HEAD_DIM), F32),
            jax.ShapeDtypeStruct((n_seq, SUBLANES, D_QKV_W), F32),
        ],
        scratch_shapes=[pltpu.VMEM((ct + SUBLANES, D_QKV_W), F32), pltpu.VMEM((W_MIX, W_MIX), F32)],
        compiler_params=pltpu.CompilerParams(dimension_semantics=("parallel", "arbitrary"),
                                             vmem_limit_bytes=VMEM_LIMIT),
        name="gdn_prompt",
    )(p3, *[_arr_of(a) for a in params])
    return o.reshape(n_seq * seq_len, W_MIX), s1, conv_tail[:, SUBLANES - (CONV_W - 1):, :]


def _decode_pre_kernel(pa_ref, pb_ref, pc_ref, pd_ref, shift_ref, h0_ref, lconv_ref, gconv_ref, cos_ref, sin_ref,
                       mu_ref, w0_ref, w2_ref, a0_ref, a2_ref, kk_ref, ka_ref,
                       lcw_ref, lcb_ref, lgw_ref, lgb_ref, lsp_ref, gcw_ref, nal_ref, dtb_ref,
                       vt_ref, vn_ref, oc_ref, h1_ref, lconv1_ref, gconv1_ref):
    ones = _head_ones()
    pa_full = pa_ref[...]
    pa = pa_full[:, :A_SHIFT_W]
    pm = pa + (shift_ref[...] - pa) * mu_ref[...]
    r, k, v, ld, av, bv = _rwkv_token_math(pm, w0_ref[...], w2_ref[...], a0_ref[...], a2_ref[...],
                                           kk_ref[...], ka_ref[...], ones)
    vecs = [r, jnp.exp(ld), k, v, av, bv]
    plain = [r, k, v, pa_full[:, A_SHIFT_W:]]
    pb = pb_ref[...]
    cos, sin = cos_ref[...], sin_ref[...]
    vecs += [_rotary(pb[:, 0:W_MIX], cos, sin), _rotary(pb[:, W_MIX:2 * W_MIX], cos, sin) * (HEAD_DIM ** -0.5),
             pb[:, 2 * W_MIX:3 * W_MIX]]
    plain.append(pb[:, 3 * W_MIX:])
    pc = pc_ref[...]
    xr = pc[:, :W_MIX]
    taps = [lconv_ref[i] for i in range(CONV_W - 1)] + [xr]
    xc = taps[0] * lcw_ref[0:1, :]
    for i in range(1, CONV_W):
        xc = xc + taps[i] * lcw_ref[i:i + 1, :]
    xc = xc + lcb_ref[...]
    a, b = _lru_token_math(xc, lgw_ref[...], lgb_ref[...], lsp_ref[...])
    hcur = a * h0_ref[...] + b
    oc_ref[...] = hcur * _silu(pc[:, W_MIX:])
    h1_ref[...] = hcur
    for i in range(CONV_W - 1):
        lconv1_ref[i] = taps[i + 1]
    pd = pd_ref[...]
    raw = pd[:, :D_QKV_W]
    gtaps = [gconv_ref[i] for i in range(CONV_W - 1)] + [raw]
    qkv = gtaps[0] * gcw_ref[0:1, :]
    for i in range(1, CONV_W):
        qkv = qkv + gtaps[i] * gcw_ref[i:i + 1, :]
    q, kg, vg, beta, g = _gdn_token_math(qkv, pd[:, D_QKV_W + W_MIX:D_QKV_W + 2 * W_MIX],
                                         pd[:, D_QKV_W + 2 * W_MIX:], nal_ref[...], dtb_ref[...], ones)
    for i in range(CONV_W - 1):
        gconv1_ref[i] = gtaps[i + 1]
    vecs += [q, kg, vg, beta, g]
    plain.append(pd[:, D_QKV_W:D_QKV_W + W_MIX])
    assert len(vecs) == N_VEC_T and len(plain) == N_VEC_PLAIN
    for i, vec in enumerate(vecs):
        vt_ref[i] = vec.T
    for i, vec in enumerate(plain):
        vn_ref[i] = vec


def _decode_state_kernel(vt_ref, wkv_ref, ret_ref, gdn_ref, gam_ref, wkv1_ref, ret1_ref, gdn1_ref, o_ref):
    v_r, v_w, v_k, v_v, v_a, v_b, r_q, r_k, r_v, g_q, g_k, g_v, g_beta, g_g = range(N_VEC_T)
    hd = HEAD_DIM
    n = vt_ref.shape[-1]
    row = lambda idx, i: vt_ref[idx, pl.ds(i, 1), :]
    rows_of = lambda i: pl.ds(pl.multiple_of(i * hd, hd), hd)
    gamma = gam_ref[...]
    beta = vt_ref[g_beta, 0:1, :]
    eg = jnp.exp(vt_ref[g_g, 0:1, :])

    def first_pass(i, carry):
        acc_ret, acc_w, acc_q = carry
        rows = rows_of(i)
        s = wkv_ref[rows, :]
        sa = jnp.sum(s * vt_ref[v_a], axis=0, keepdims=True)
        s = s * vt_ref[v_w] + sa * vt_ref[v_b] + row(v_v, i) * vt_ref[v_k]
        wkv1_ref[rows, :] = s
        o_ref[0, pl.ds(i, 1), :] = jnp.sum(s * vt_ref[v_r], axis=0, keepdims=True)
        s = ret_ref[rows, :] * gamma + row(r_k, i) * vt_ref[r_v]
        ret1_ref[rows, :] = s
        acc_ret = acc_ret + row(r_q, i) * s
        s = gdn_ref[rows, :]
        return acc_ret, acc_w + row(g_k, i) * s, acc_q + row(g_q, i) * s

    zeros = jnp.zeros((hd, n), F32)
    acc_ret, acc_w, acc_q = lax.fori_loop(0, hd, first_pass, (zeros, zeros, zeros))
    o_ref[1] = acc_ret
    v_new = vt_ref[g_v] * beta - acc_w * (beta * eg)
    qk = jnp.sum(vt_ref[g_q] * vt_ref[g_k], axis=0, keepdims=True)
    o_ref[2] = acc_q * eg + qk * v_new

    def second_pass(i, carry):
        rows = rows_of(i)
        gdn1_ref[rows, :] = gdn_ref[rows, :] * eg + row(g_k, i) * v_new
        return carry

    lax.fori_loop(0, hd, second_pass, 0)


def _decode_finish_kernel(ot_ref, vn_ref, rk_ref, lng_ref, lnb_ref, ng_ref, oa_ref, ob_ref, od_ref):
    ones = _head_ones()
    r, k, v, z_a, z_b, z_d = (vn_ref[i] for i in range(N_VEC_PLAIN))
    oa_ref[...] = _rwkv_finish(ot_ref[0].T, r, k, v, z_a, rk_ref[...], lng_ref[...], lnb_ref[...], ones)
    ob_ref[...] = _head_rms_finish(ot_ref[1].T, z_b, ones)
    od_ref[...] = _head_rms_finish(ot_ref[2].T, z_d, ones, ng_ref[...])


def _batch_minor(state):
    n_layers, n = state.shape[:2]
    return jnp.transpose(state, (0, 2, 3, 4, 1)).reshape(n_layers, -1, n)


def _batch_major(flat_state):
    n = flat_state.shape[-1]
    return jnp.transpose(flat_state.reshape(N_HEADS, HEAD_DIM, HEAD_DIM, n), (3, 0, 1, 2))


def _decode_layer(l, p_a, p_b, p_c, p_d, carried, cos_t, sin_t, lp):
    n = p_a.shape[0]
    assert n % LANES == 0, "the decode state kernel keeps the batch on lanes"
    flat = HEAD_DIM * HEAD_DIM
    taps = CONV_W - 1
    rwkv_params = [_row(lp["rwkv_mu"]), _row(lp["rwkv_w0"]), lp["rwkv_w2"], _row(lp["rwkv_a0"]), lp["rwkv_a2"],
                   _row(lp["rwkv_k_k"]), _row(lp["rwkv_k_a"])]
    gdn_params = _gdn_params(lp)
    full = lambda a: pl.BlockSpec(a.shape, lambda i: (0,) * a.ndim)
    layer_blk = lambda a: pl.BlockSpec((None,) + a.shape[1:], lambda i, nd=a.ndim: (l,) + (0,) * (nd - 1))
    projs = [p_a, p_b, p_c, p_d]
    layered = [carried["shift"], carried["lru_h"], carried["lru_conv"], carried["gdn_conv"]]
    consts = [cos_t, sin_t, *rwkv_params, *_lru_params(lp), *gdn_params[:3]]
    out_shapes = [
        jax.ShapeDtypeStruct((N_VEC_T, W_MIX, n), F32),
        jax.ShapeDtypeStruct((N_VEC_PLAIN, n, W_MIX), F32),
        jax.ShapeDtypeStruct((n, W_MIX), F32),
        jax.ShapeDtypeStruct((n, W_MIX), F32),
        jax.ShapeDtypeStruct((taps, n, W_MIX), F32),
        jax.ShapeDtypeStruct((taps, n, D_QKV_W), F32),
    ]
    vec_t, vec_n, o_c, lru_h1, lru_conv1, gdn_conv1 = pl.pallas_call(
        _decode_pre_kernel,
        grid=(1,),
        in_specs=[full(a) for a in projs] + [layer_blk(a) for a in layered] + [_spec_of(a) for a in consts],
        out_specs=[pl.BlockSpec(s.shape, lambda i, nd=len(s.shape): (0,) * nd) for s in out_shapes],
        out_shape=out_shapes,
        compiler_params=pltpu.CompilerParams(dimension_semantics=("arbitrary",), vmem_limit_bytes=VMEM_LIMIT),
        name="decode_tokens",
    )(*projs, *layered, *[_arr_of(a) for a in consts])

    gam = jnp.broadcast_to((1.0 - 2.0 ** (-5.0 - jnp.arange(N_HEADS, dtype=F32)))[:, None, None], (N_HEADS, 1, n))
    state_in = pl.BlockSpec((None, flat, n), lambda h: (l, h, 0))
    state_out = pl.BlockSpec((flat, n), lambda h: (h, 0))
    wkv1, ret1, gdn1, o_t = pl.pallas_call(
        _decode_state_kernel,
        grid=(N_HEADS,),
        in_specs=[pl.BlockSpec((N_VEC_T, HEAD_DIM, n), lambda h: (0, h, 0)), state_in, state_in, state_in,
                  pl.BlockSpec((None, 1, n), lambda h: (h, 0, 0))],
        out_specs=[state_out, state_out, state_out, pl.BlockSpec((3, HEAD_DIM, n), lambda h: (0, h, 0))],
        out_shape=[jax.ShapeDtypeStruct((N_HEADS * flat, n), F32)] * 3 + [jax.ShapeDtypeStruct((3, W_MIX, n), F32)],
        compiler_params=pltpu.CompilerParams(dimension_semantics=("parallel",), vmem_limit_bytes=VMEM_LIMIT),
        name="decode_states",
    )(vec_t, carried["wkv"], carried["ret"], carried["gdn"], gam)

    finish_ins = [o_t, vec_n, _row(lp["rwkv_r_k"]), _row(lp["rwkv_ln_g"]), _row(lp["rwkv_ln_b"]), gdn_params[3]]
    o_a, o_b, o_d = pl.pallas_call(
        _decode_finish_kernel,
        grid=(1,),
        in_specs=[_spec_of(a) for a in finish_ins],
        out_specs=[pl.BlockSpec((n, W_MIX), lambda i: (0, 0))] * 3,
        out_shape=[jax.ShapeDtypeStruct((n, W_MIX), F32)] * 3,
        compiler_params=pltpu.CompilerParams(dimension_semantics=("arbitrary",), vmem_limit_bytes=VMEM_LIMIT),
        name="decode_finish",
    )(*[_arr_of(a) for a in finish_ins])
    new_states = (_batch_major(wkv1), p_a[:, :A_SHIFT_W], _batch_major(ret1), lru_h1,
                  jnp.transpose(lru_conv1, (1, 0, 2)), _batch_major(gdn1), jnp.transpose(gdn_conv1, (1, 0, 2)))
    return (o_a, o_b, o_c, o_d), new_states


def _prompt_layer(p_a, p_b, p_c, p_d, n_seq, seq_len, cos_t, sin_t, lp):
    o_a, wkv1, shift1 = _rwkv_call(p_a, n_seq, seq_len, lp)
    o_b, ret1 = _ret_call(p_b, n_seq, seq_len, cos_t, sin_t)
    o_c, lru_h1, lru_conv1 = _lru_call(p_c, n_seq, seq_len, lp)
    o_d, gdn1, gdn_conv1 = _gdn_call(p_d, n_seq, seq_len, lp)
    return (o_a, o_b, o_c, o_d), (wkv1, shift1, ret1, lru_h1, lru_conv1, gdn1, gdn_conv1)


def _run_group(x, mods, pos, carried, layers, final_g):
    n_seq, seq_len, d = x.shape
    x2 = x.reshape(n_seq * seq_len, d)
    cos_t, sin_t = _rope_tables(pos, tile_paired=carried is None)
    new = []
    n_layers = len(layers)
    for l, lp in enumerate(layers):
        w_qk = lp["w_qk_tiles"] if carried is None else lp["w_qk"]
        p_a, p_b, p_c, p_d = _inproj_call(x2, mods, lp["norm_g"], l, lp["w_t"], lp["w_ba"], w_qk, seq_len)
        if carried is None:
            branches, st = _prompt_layer(p_a, p_b, p_c, p_d, n_seq, seq_len, cos_t, sin_t, lp)
        else:
            branches, st = _decode_layer(l, p_a, p_b, p_c, p_d, carried, cos_t, sin_t, lp)
        new.append(st)
        x2 = _outproj_call(x2, mods, lp["norm_g"], branches, l, lp["w_t"], lp["w_up_bf16"],
                           lp["w_out_bf16"], final_g, seq_len, final=(l == n_layers - 1))
    stacked = tuple(jnp.stack([s[i] for s in new], axis=0) for i in range(7))
    return x2.reshape(n_seq, seq_len, d), stacked


def kernel(x_prompt, x_sample, c_prompt, c_sample, state_rwkv_wkv, state_rwkv_shift, state_ret, state_lru_h, state_lru_conv, state_gdn, state_gdn_conv, ada_w, ada_b, norm_g, w_in, rwkv_mu, rwkv_w0, rwkv_w2, rwkv_a0, rwkv_a2, rwkv_k_k, rwkv_k_a, rwkv_r_k, rwkv_ln_g, rwkv_ln_b, lru_conv_w, lru_conv_b, lru_gate_w, lru_gate_b, lru_lambda, gdn_conv_w, gdn_A_log, gdn_dt_bias, gdn_norm_g, w_up, w_out, final_g):
    n_layers = ada_w.shape[0]
    n_prompt, seq_len, _ = x_prompt.shape
    n_sample, dec_len, _ = x_sample.shape
    assert dec_len == 1, "the decode path handles one token per sequence"
    w_in_t = jnp.swapaxes(w_in, 1, 2).astype(BF16)
    off_ba = A_W + B_W + C_W + D_QKV_W
    rows = lambda a: a.reshape(n_layers, 1, -1)
    per_tile = LANES // HEAD_DIM
    w_qk_tiles = w_in_t[:, A_W:A_W + 2 * W_MIX].reshape(
        n_layers, 2, W_MIX // LANES, per_tile, 2, HEAD_DIM // 2, -1).transpose(0, 1, 2, 4, 3, 5, 6).reshape(
        n_layers, 2 * W_MIX, -1)
    stacked = dict(
        norm_g=rows(norm_g), w_up_bf16=w_up.astype(BF16), w_out_bf16=w_out.astype(BF16),
        w_ba=jnp.repeat(w_in_t[:, off_ba:off_ba + 2 * N_HEADS], HEAD_DIM, axis=1),
        w_qk=w_in_t[:, A_W:A_W + 2 * W_MIX],
        w_qk_tiles=w_qk_tiles,
        rwkv_mu=rows(rwkv_mu), rwkv_w0=rows(rwkv_w0), rwkv_w2=rwkv_w2, rwkv_a0=rows(rwkv_a0), rwkv_a2=rwkv_a2,
        rwkv_k_k=rows(rwkv_k_k), rwkv_k_a=rows(rwkv_k_a), rwkv_r_k=rows(rwkv_r_k), rwkv_ln_g=rows(rwkv_ln_g),
        rwkv_ln_b=rows(rwkv_ln_b), lru_conv_w=lru_conv_w, lru_conv_b=rows(lru_conv_b),
        lru_gate_w=_block_diag_gates(lru_gate_w), lru_gate_b=rows(lru_gate_b), lru_lambda=rows(lru_lambda),
        gdn_conv_w=gdn_conv_w, gdn_A_log=rows(jnp.repeat(gdn_A_log, HEAD_DIM, axis=1)),
        gdn_dt_bias=rows(jnp.repeat(gdn_dt_bias, HEAD_DIM, axis=1)),
        gdn_norm_g=rows(jnp.tile(gdn_norm_g, (1, N_HEADS))))
    layers = [dict({name: _LayerParam(arr, l) for name, arr in stacked.items()}, w_t=w_in_t)
              for l in range(n_layers)]
    mods_p, mods_s = _ada_call(c_prompt, c_sample, ada_w, ada_b)

    y_prompt, new_p = _run_group(x_prompt, mods_p, jnp.arange(seq_len, dtype=jnp.int32), None, layers, final_g)
    carried = dict(wkv=_batch_minor(state_rwkv_wkv), ret=_batch_minor(state_ret), gdn=_batch_minor(state_gdn),
                   shift=state_rwkv_shift, lru_h=state_lru_h,
                   lru_conv=jnp.transpose(state_lru_conv, (0, 2, 1, 3)),
                   gdn_conv=jnp.transpose(state_gdn_conv, (0, 2, 1, 3)))
    pos_s = PAST_LEN + jnp.arange(dec_len, dtype=jnp.int32)
    y_sample, new_s = _run_group(x_sample, mods_s, pos_s, carried, layers, final_g)
    return (y_prompt, y_sample) + new_p + new_s
```
